```python
import math
import jax, jax.numpy as jnp
from jax import lax
import numpy as np

D_MODEL = 1024
BATCH = 16
SEQ = 2048
DEPTH = 4

CHUNK = 64
N_META = 16
SSD_PAD = (-N_META) % CHUNK
D_MIX = 2 * D_MODEL
D_A = D_MIX // 4
D_B = D_MIX // 2
D_C = D_MIX // 4
CONV_A_K = 3
SSM_HEAD_DIM = 64
SSM_HEADS = D_B // SSM_HEAD_DIM
SSM_GROUPS = 2
SSM_HPG = SSM_HEADS // SSM_GROUPS
SSM_STATE = 128
SSM_CONV_K = 4
CONF_K = 31
NORM_EPS = 1e-6
LN_EPS = 1e-5

IN_SIZES = [D_A, D_A, D_A, D_A,
            D_B, D_B, SSM_GROUPS * SSM_STATE, SSM_GROUPS * SSM_STATE, SSM_HEADS,
            D_C, D_C, D_C]
N_IN = int(sum(IN_SIZES))
IN_SPLITS = [int(v) for v in np.cumsum(IN_SIZES)[:-1]]

kernel_name = "hybrid_conv_ssd_conformer_trunk"


def rmsnorm(x, g):
    xf = x.astype(jnp.float32)
    y = xf * lax.rsqrt(jnp.mean(xf * xf, axis=-1, keepdims=True) + NORM_EPS)
    return (y * g.astype(jnp.float32)).astype(x.dtype)


def layernorm(x, g, b):
    xf = x.astype(jnp.float32)
    mu = jnp.mean(xf, axis=-1, keepdims=True)
    var = jnp.mean(jnp.square(xf - mu), axis=-1, keepdims=True)
    y = (xf - mu) * lax.rsqrt(var + LN_EPS)
    return (y * g.astype(jnp.float32) + b.astype(jnp.float32)).astype(x.dtype)


def causal_dwconv(x, w, b=None):
    k, c = w.shape
    out = lax.conv_general_dilated(
        x, w.astype(x.dtype).reshape(k, 1, c), window_strides=(1,), padding=[(k - 1, 0)],
        dimension_numbers=('NWC', 'WIO', 'NWC'), feature_group_count=c)
    if b is not None:
        out = out + b.astype(x.dtype)
    return out


def ssd_scan(x, b_mat, c_mat, dt_raw, dt_bias, a_log, d_skip):
    in_dtype = x.dtype
    f32 = jnp.float32
    bsz, l, _ = x.shape
    dt = jax.nn.softplus(dt_raw.astype(f32) + dt_bias.astype(f32))
    a = -jnp.exp(a_log.astype(f32))
    xg = x.astype(f32).reshape(bsz, l, SSM_GROUPS, SSM_HPG, SSM_HEAD_DIM)
    dtg = dt.reshape(bsz, l, SSM_GROUPS, SSM_HPG)
    xdt = xg * dtg[..., None]
    da = dtg * a.reshape(SSM_GROUPS, SSM_HPG)
    bm = b_mat.astype(f32).reshape(bsz, l, SSM_GROUPS, SSM_STATE)
    cm = c_mat.astype(f32).reshape(bsz, l, SSM_GROUPS, SSM_STATE)

    def front_pad(t):
        return jnp.pad(t, [(0, 0), (SSD_PAD, 0)] + [(0, 0)] * (t.ndim - 2))

    xdt, da, bm, cm = front_pad(xdt), front_pad(da), front_pad(bm), front_pad(cm)
    lp = l + SSD_PAD
    nc = lp // CHUNK
    xdt = xdt.reshape(bsz, nc, CHUNK, SSM_GROUPS, SSM_HPG, SSM_HEAD_DIM)
    bm = bm.reshape(bsz, nc, CHUNK, SSM_GROUPS, SSM_STATE)
    cm = cm.reshape(bsz, nc, CHUNK, SSM_GROUPS, SSM_STATE)
    da = da.reshape(bsz, nc, CHUNK, SSM_GROUPS, SSM_HPG).transpose(0, 3, 4, 1, 2)
    a_cum = jnp.cumsum(da, axis=-1)

    causal = jnp.tril(jnp.ones((CHUNK, CHUNK), dtype=bool))
    seg = a_cum[..., :, None] - a_cum[..., None, :]
    decay = jnp.exp(jnp.where(causal, seg, -jnp.inf))
    cb = jnp.einsum('bclgn,bcsgn->bcgls', cm, bm)
    y_diag = jnp.einsum('bcgls,bgecls,bcsgep->bclgep', cb, decay, xdt)

    decay_states = jnp.exp(a_cum[..., -1:] - a_cum)
    states = jnp.einsum('bclgn,bgecl,bclgep->bcgepn', bm, decay_states, xdt)
    chunk_decay = jnp.exp(a_cum[..., -1])

    def step(carry, inp):
        st, dec = inp
        new = carry * dec[..., None, None] + st
        return new, carry

    init = jnp.zeros((bsz, SSM_GROUPS, SSM_HPG, SSM_HEAD_DIM, SSM_STATE), f32)
    _, prev = lax.scan(step, init, (jnp.moveaxis(states, 1, 0), jnp.moveaxis(chunk_decay, 3, 0)))
    prev = jnp.moveaxis(prev, 0, 1)

    y_off = jnp.einsum('bclgn,bcgepn,bgecl->bclgep', cm, prev, jnp.exp(a_cum))
    y = (y_diag + y_off).reshape(bsz, lp, SSM_GROUPS, SSM_HPG, SSM_HEAD_DIM)[:, SSD_PAD:]
    y = y + xg * d_skip.astype(f32).reshape(SSM_GROUPS, SSM_HPG)[..., None]
    return y.reshape(bsz, l, D_B).astype(in_dtype)


def hybrid_mixer(h, w_in, w_out, conv_a_w, ssm_conv_w, ssm_conv_b, dt_bias, a_log, d_skip,
                 ssm_norm_g, conf_conv_w, conf_conv_b, conf_ln_g, conf_ln_b):
    proj = jnp.einsum('bld,dn->bln', h, w_in.astype(h.dtype))
    (a_b, a_c, a_x, a_z, b_z, b_x, b_bm, b_cm, b_dt, c_a, c_g, c_z) = jnp.split(proj, IN_SPLITS, axis=-1)

    y_a = a_b * causal_dwconv(a_c * a_x, conv_a_w) * jax.nn.silu(a_z)

    xbc = jax.nn.silu(causal_dwconv(jnp.concatenate([b_x, b_bm, b_cm], axis=-1), ssm_conv_w, ssm_conv_b))
    xs, bs, cs = jnp.split(xbc, [D_B, D_B + SSM_GROUPS * SSM_STATE], axis=-1)
    y_b = ssd_scan(xs, bs, cs, b_dt, dt_bias, a_log, d_skip)
    y_b = rmsnorm(y_b * jax.nn.silu(b_z), ssm_norm_g)

    u = c_a * jax.nn.sigmoid(c_g)
    u = causal_dwconv(u, conf_conv_w, conf_conv_b)
    u = layernorm(u, conf_ln_g, conf_ln_b)
    y_c = jax.nn.silu(u) * jax.nn.silu(c_z)

    y = jnp.concatenate([y_a, y_b, y_c], axis=-1)
    return jnp.einsum('blm,md->bld', y, w_out.astype(y.dtype))


def _fwd_setup_inputs(seed: int = 0) -> dict:
    key = jax.random.key(seed)
    ks = jax.random.split(key, 20)
    f32 = jnp.float32
    x = jax.random.normal(ks[0], (BATCH, SEQ, D_MODEL), f32)
    meta = jax.random.normal(ks[1], (N_META, D_MODEL), f32)
    pre_g = 1.0 + 0.05 * jax.random.normal(ks[2], (DEPTH, D_MODEL), f32)
    post_g = 1.0 + 0.05 * jax.random.normal(ks[3], (DEPTH, D_MODEL), f32)
    w_in = jax.random.normal(ks[4], (DEPTH, D_MODEL, N_IN), f32) * D_MODEL ** -0.5
    w_out = jax.random.normal(ks[5], (DEPTH, D_MIX, D_MODEL), f32) * D_MIX ** -0.5
    conv_a_w = jax.random.normal(ks[6], (DEPTH, CONV_A_K, D_A), f32) * CONV_A_K ** -0.5
    n_xbc = D_B + 2 * SSM_GROUPS * SSM_STATE
    ssm_conv_w = jax.random.normal(ks[7], (DEPTH, SSM_CONV_K, n_xbc), f32) * SSM_CONV_K ** -0.5
    ssm_conv_b = 0.02 * jax.random.normal(ks[8], (DEPTH, n_xbc), f32)
    dt0 = jnp.exp(jax.random.uniform(ks[9], (DEPTH, SSM_HEADS), f32, math.log(1e-3), math.log(1e-1)))
    dt_bias = dt0 + jnp.log(-jnp.expm1(-dt0))
    a_log = jnp.log(jax.random.uniform(ks[10], (DEPTH, SSM_HEADS), f32, 1.0, 16.0))
    d_skip = 1.0 + 0.1 * jax.random.normal(ks[11], (DEPTH, SSM_HEADS), f32)
    ssm_norm_g = 1.0 + 0.05 * jax.random.normal(ks[12], (DEPTH, D_B), f32)
    conf_conv_w = jax.random.normal(ks[13], (DEPTH, CONF_K, D_C), f32) * CONF_K ** -0.5
    conf_conv_b = 0.02 * jax.random.normal(ks[14], (DEPTH, D_C), f32)
    conf_ln_g = 1.0 + 0.05 * jax.random.normal(ks[15], (DEPTH, D_C), f32)
    conf_ln_b = 0.02 * jax.random.normal(ks[16], (DEPTH, D_C), f32)
    return {"x": x, "meta": meta, "pre_g": pre_g, "post_g": post_g, "w_in": w_in, "w_out": w_out,
            "conv_a_w": conv_a_w, "ssm_conv_w": ssm_conv_w, "ssm_conv_b": ssm_conv_b,
            "dt_bias": dt_bias, "a_log": a_log, "d_skip": d_skip, "ssm_norm_g": ssm_norm_g,
            "conf_conv_w": conf_conv_w, "conf_conv_b": conf_conv_b,
            "conf_ln_g": conf_ln_g, "conf_ln_b": conf_ln_b}


def _fwd_reference(x, meta, pre_g, post_g, w_in, w_out, conv_a_w, ssm_conv_w, ssm_conv_b,
              dt_bias, a_log, d_skip, ssm_norm_g, conf_conv_w, conf_conv_b, conf_ln_g, conf_ln_b):
    bsz = x.shape[0]
    meta_b = jnp.broadcast_to(meta.astype(x.dtype)[None], (bsz, N_META, D_MODEL))
    h = jnp.concatenate([meta_b, x], axis=1)
    for i in range(DEPTH):
        m = hybrid_mixer(rmsnorm(h, pre_g[i]), w_in[i], w_out[i], conv_a_w[i], ssm_conv_w[i],
                         ssm_conv_b[i], dt_bias[i], a_log[i], d_skip[i], ssm_norm_g[i],
                         conf_conv_w[i], conf_conv_b[i], conf_ln_g[i], conf_ln_b[i])
        h = h + rmsnorm(m, post_g[i])
    return h[:, N_META:]


import jax as _jax
import jax.numpy as _jnp

TWIN_FORMAT = 'train_step'
FWD_PARAMS = ['x', 'meta', 'pre_g', 'post_g', 'w_in', 'w_out', 'conv_a_w', 'ssm_conv_w', 'ssm_conv_b', 'dt_bias', 'a_log', 'd_skip', 'ssm_norm_g', 'conf_conv_w', 'conf_conv_b', 'conf_ln_g', 'conf_ln_b']
TWIN_WEIGHTS = ['meta', 'pre_g', 'post_g', 'w_in', 'w_out', 'conv_a_w', 'ssm_conv_w', 'ssm_conv_b', 'dt_bias', 'a_log', 'd_skip', 'ssm_norm_g', 'conf_conv_w', 'conf_conv_b', 'conf_ln_g', 'conf_ln_b']
TWIN_DIFF_INPUT = 'x'
TWIN_INPUTS = ['x', 'meta', 'pre_g', 'post_g', 'w_in', 'w_out', 'conv_a_w', 'ssm_conv_w', 'ssm_conv_b', 'dt_bias', 'a_log', 'd_skip', 'ssm_norm_g', 'conf_conv_w', 'conf_conv_b', 'conf_ln_g', 'conf_ln_b', 'loss_target', 'm_meta', 'm_pre_g', 'm_post_g', 'm_w_in', 'm_w_out', 'm_conv_a_w', 'm_ssm_conv_w', 'm_ssm_conv_b', 'm_dt_bias', 'm_a_log', 'm_d_skip', 'm_ssm_norm_g', 'm_conf_conv_w', 'm_conf_conv_b', 'm_conf_ln_g', 'm_conf_ln_b', 'v_meta', 'v_pre_g', 'v_post_g', 'v_w_in', 'v_w_out', 'v_conv_a_w', 'v_ssm_conv_w', 'v_ssm_conv_b', 'v_dt_bias', 'v_a_log', 'v_d_skip', 'v_ssm_norm_g', 'v_conf_conv_w', 'v_conf_conv_b', 'v_conf_ln_g', 'v_conf_ln_b']
TWIN_OUTPUTS = ['loss', 'grad_x', 'grad_meta', 'grad_pre_g', 'grad_post_g', 'grad_w_in', 'grad_w_out', 'grad_conv_a_w', 'grad_ssm_conv_w', 'grad_ssm_conv_b', 'grad_dt_bias', 'grad_a_log', 'grad_d_skip', 'grad_ssm_norm_g', 'grad_conf_conv_w', 'grad_conf_conv_b', 'grad_conf_ln_g', 'grad_conf_ln_b', 'delta_meta', 'delta_pre_g', 'delta_post_g', 'delta_w_in', 'delta_w_out', 'delta_conv_a_w', 'delta_ssm_conv_w', 'delta_ssm_conv_b', 'delta_dt_bias', 'delta_a_log', 'delta_d_skip', 'delta_ssm_norm_g', 'delta_conf_conv_w', 'delta_conf_conv_b', 'delta_conf_ln_g', 'delta_conf_ln_b', 'new_m_meta', 'new_m_pre_g', 'new_m_post_g', 'new_m_w_in', 'new_m_w_out', 'new_m_conv_a_w', 'new_m_ssm_conv_w', 'new_m_ssm_conv_b', 'new_m_dt_bias', 'new_m_a_log', 'new_m_d_skip', 'new_m_ssm_norm_g', 'new_m_conf_conv_w', 'new_m_conf_conv_b', 'new_m_conf_ln_g', 'new_m_conf_ln_b', 'new_v_meta', 'new_v_pre_g', 'new_v_post_g', 'new_v_w_in', 'new_v_w_out', 'new_v_conv_a_w', 'new_v_ssm_conv_w', 'new_v_ssm_conv_b', 'new_v_dt_bias', 'new_v_a_log', 'new_v_d_skip', 'new_v_ssm_norm_g', 'new_v_conf_conv_w', 'new_v_conf_conv_b', 'new_v_conf_ln_g', 'new_v_conf_ln_b']
TWIN_LEAF_KINDS = {'loss': 'loss', 'grad_x': 'grad_x', 'grad_meta': 'grad_w', 'grad_pre_g': 'grad_w', 'grad_post_g': 'grad_w', 'grad_w_in': 'grad_w', 'grad_w_out': 'grad_w', 'grad_conv_a_w': 'grad_w', 'grad_ssm_conv_w': 'grad_w', 'grad_ssm_conv_b': 'grad_w', 'grad_dt_bias': 'grad_w', 'grad_a_log': 'grad_w', 'grad_d_skip': 'grad_w', 'grad_ssm_norm_g': 'grad_w', 'grad_conf_conv_w': 'grad_w', 'grad_conf_conv_b': 'grad_w', 'grad_conf_ln_g': 'grad_w', 'grad_conf_ln_b': 'grad_w', 'delta_meta': 'delta_w', 'delta_pre_g': 'delta_w', 'delta_post_g': 'delta_w', 'delta_w_in': 'delta_w', 'delta_w_out': 'delta_w', 'delta_conv_a_w': 'delta_w', 'delta_ssm_conv_w': 'delta_w', 'delta_ssm_conv_b': 'delta_w', 'delta_dt_bias': 'delta_w', 'delta_a_log': 'delta_w', 'delta_d_skip': 'delta_w', 'delta_ssm_norm_g': 'delta_w', 'delta_conf_conv_w': 'delta_w', 'delta_conf_conv_b': 'delta_w', 'delta_conf_ln_g': 'delta_w', 'delta_conf_ln_b': 'delta_w', 'new_m_meta': 'new_m', 'new_m_pre_g': 'new_m', 'new_m_post_g': 'new_m', 'new_m_w_in': 'new_m', 'new_m_w_out': 'new_m', 'new_m_conv_a_w': 'new_m', 'new_m_ssm_conv_w': 'new_m', 'new_m_ssm_conv_b': 'new_m', 'new_m_dt_bias': 'new_m', 'new_m_a_log': 'new_m', 'new_m_d_skip': 'new_m', 'new_m_ssm_norm_g': 'new_m', 'new_m_conf_conv_w': 'new_m', 'new_m_conf_conv_b': 'new_m', 'new_m_conf_ln_g': 'new_m', 'new_m_conf_ln_b': 'new_m', 'new_v_meta': 'new_v', 'new_v_pre_g': 'new_v', 'new_v_post_g': 'new_v', 'new_v_w_in': 'new_v', 'new_v_w_out': 'new_v', 'new_v_conv_a_w': 'new_v', 'new_v_ssm_conv_w': 'new_v', 'new_v_ssm_conv_b': 'new_v', 'new_v_dt_bias': 'new_v', 'new_v_a_log': 'new_v', 'new_v_d_skip': 'new_v', 'new_v_ssm_norm_g': 'new_v', 'new_v_conf_conv_w': 'new_v', 'new_v_conf_conv_b': 'new_v', 'new_v_conf_ln_g': 'new_v', 'new_v_conf_ln_b': 'new_v'}


def _forward(args):
    return _fwd_reference(*[args[k] for k in FWD_PARAMS])


def _output_shape():
    out = _jax.eval_shape(lambda: _forward(_fwd_setup_inputs(0)))
    return out.shape, out.dtype

N_MICROBATCH = 1
ADAM_LR = 0.001
ADAM_B1 = 0.9
ADAM_B2 = 0.999
ADAM_EPS = 1e-08
ADAM_WD = 0.01
ADAM_STEP = 10
PER_EXAMPLE_BATCH_AXIS = {'x': 0, 'loss_target': 0}
SHARED_INPUTS = []
_WEIGHT_DTYPES = {'meta': _jnp.float32, 'pre_g': _jnp.float32, 'post_g': _jnp.float32, 'w_in': _jnp.float32, 'w_out': _jnp.float32, 'conv_a_w': _jnp.float32, 'ssm_conv_w': _jnp.float32, 'ssm_conv_b': _jnp.float32, 'dt_bias': _jnp.float32, 'a_log': _jnp.float32, 'd_skip': _jnp.float32, 'ssm_norm_g': _jnp.float32, 'conf_conv_w': _jnp.float32, 'conf_conv_b': _jnp.float32, 'conf_ln_g': _jnp.float32, 'conf_ln_b': _jnp.float32}
MOMENT_SCALE = {'meta': 5.500168e-02, 'pre_g': 1.319236e+00, 'post_g': 3.198539e+01, 'w_in': 5.402601e-01, 'w_out': 1.947706e+00, 'conv_a_w': 3.718308e-01, 'ssm_conv_w': 1.166012e+00, 'ssm_conv_b': 3.842473e+00, 'dt_bias': 1.431580e+00, 'a_log': 4.606785e+00, 'd_skip': 5.822565e+00, 'ssm_norm_g': 2.049669e+00, 'conf_conv_w': 2.506409e-01, 'conf_conv_b': 1.755373e+00, 'conf_ln_g': 6.397609e-01, 'conf_ln_b': 9.775375e-01}


def _to_microbatches(a, axis):
    t = _jnp.moveaxis(a, axis, 0)
    t = t.reshape((N_MICROBATCH, t.shape[0] // N_MICROBATCH) + t.shape[1:])
    return _jnp.moveaxis(t, 1, axis + 1)


def setup_inputs(seed: int = 0) -> dict:
    inp = _fwd_setup_inputs(seed)
    key = _jax.random.fold_in(_jax.random.key(seed), 7919)
    shape, _ = _output_shape()
    out = dict(inp)
    out["loss_target"] = _jax.random.normal(_jax.random.fold_in(key, 0), shape, _jnp.float32)
    for i, name in enumerate(TWIN_WEIGHTS):
        w = inp[name].astype(_jnp.float32)
        if MOMENT_SCALE is None:
            s = _jnp.sqrt(_jnp.mean(_jnp.square(w)) + 1e-30)
        else:
            s = MOMENT_SCALE[name]
        km, kv = _jax.random.split(_jax.random.fold_in(key, i + 1))
        out[name] = w
        out["m_" + name] = s * _jax.random.normal(km, w.shape, _jnp.float32)
        out["v_" + name] = (s * s) * _jax.random.uniform(kv, w.shape, _jnp.float32, 0.5, 1.5)
    if N_MICROBATCH > 1:
        for name, axis in PER_EXAMPLE_BATCH_AXIS.items():
            out[name] = _to_microbatches(out[name], axis)
    return {'x': out['x'], 'meta': out['meta'], 'pre_g': out['pre_g'], 'post_g': out['post_g'], 'w_in': out['w_in'], 'w_out': out['w_out'], 'conv_a_w': out['conv_a_w'], 'ssm_conv_w': out['ssm_conv_w'], 'ssm_conv_b': out['ssm_conv_b'], 'dt_bias': out['dt_bias'], 'a_log': out['a_log'], 'd_skip': out['d_skip'], 'ssm_norm_g': out['ssm_norm_g'], 'conf_conv_w': out['conf_conv_w'], 'conf_conv_b': out['conf_conv_b'], 'conf_ln_g': out['conf_ln_g'], 'conf_ln_b': out['conf_ln_b'], 'loss_target': out['loss_target'], 'm_meta': out['m_meta'], 'm_pre_g': out['m_pre_g'], 'm_post_g': out['m_post_g'], 'm_w_in': out['m_w_in'], 'm_w_out': out['m_w_out'], 'm_conv_a_w': out['m_conv_a_w'], 'm_ssm_conv_w': out['m_ssm_conv_w'], 'm_ssm_conv_b': out['m_ssm_conv_b'], 'm_dt_bias': out['m_dt_bias'], 'm_a_log': out['m_a_log'], 'm_d_skip': out['m_d_skip'], 'm_ssm_norm_g': out['m_ssm_norm_g'], 'm_conf_conv_w': out['m_conf_conv_w'], 'm_conf_conv_b': out['m_conf_conv_b'], 'm_conf_ln_g': out['m_conf_ln_g'], 'm_conf_ln_b': out['m_conf_ln_b'], 'v_meta': out['v_meta'], 'v_pre_g': out['v_pre_g'], 'v_post_g': out['v_post_g'], 'v_w_in': out['v_w_in'], 'v_w_out': out['v_w_out'], 'v_conv_a_w': out['v_conv_a_w'], 'v_ssm_conv_w': out['v_ssm_conv_w'], 'v_ssm_conv_b': out['v_ssm_conv_b'], 'v_dt_bias': out['v_dt_bias'], 'v_a_log': out['v_a_log'], 'v_d_skip': out['v_d_skip'], 'v_ssm_norm_g': out['v_ssm_norm_g'], 'v_conf_conv_w': out['v_conf_conv_w'], 'v_conf_conv_b': out['v_conf_conv_b'], 'v_conf_ln_g': out['v_conf_ln_g'], 'v_conf_ln_b': out['v_conf_ln_b']}


def _loss(weights, diff, rest, loss_target):
    with _jax.named_scope("forward"):
        args = {**rest, TWIN_DIFF_INPUT: diff, **{k: w.astype(_WEIGHT_DTYPES[k]) for k, w in weights.items()}}
        y = _forward(args)
    with _jax.named_scope("loss_head"):
        err = _jnp.square(y.astype(_jnp.float32) - loss_target)
        return 0.5 * _jnp.sum(_jnp.mean(err, axis=-1)) if err.ndim else 0.5 * err


def _adamw(w, g, m, v):
    m = ADAM_B1 * m + (1.0 - ADAM_B1) * g
    v = ADAM_B2 * v + (1.0 - ADAM_B2) * _jnp.square(g)
    m_hat = m / (1.0 - ADAM_B1 ** ADAM_STEP)
    v_hat = v / (1.0 - ADAM_B2 ** ADAM_STEP)
    delta = -ADAM_LR * (m_hat / (_jnp.sqrt(v_hat) + ADAM_EPS) + ADAM_WD * w)
    return delta, m, v


def reference(x, meta, pre_g, post_g, w_in, w_out, conv_a_w, ssm_conv_w, ssm_conv_b, dt_bias, a_log, d_skip, ssm_norm_g, conf_conv_w, conf_conv_b, conf_ln_g, conf_ln_b, loss_target, m_meta, m_pre_g, m_post_g, m_w_in, m_w_out, m_conv_a_w, m_ssm_conv_w, m_ssm_conv_b, m_dt_bias, m_a_log, m_d_skip, m_ssm_norm_g, m_conf_conv_w, m_conf_conv_b, m_conf_ln_g, m_conf_ln_b, v_meta, v_pre_g, v_post_g, v_w_in, v_w_out, v_conv_a_w, v_ssm_conv_w, v_ssm_conv_b, v_dt_bias, v_a_log, v_d_skip, v_ssm_norm_g, v_conf_conv_w, v_conf_conv_b, v_conf_ln_g, v_conf_ln_b):
    given = dict(x=x, meta=meta, pre_g=pre_g, post_g=post_g, w_in=w_in, w_out=w_out, conv_a_w=conv_a_w, ssm_conv_w=ssm_conv_w, ssm_conv_b=ssm_conv_b, dt_bias=dt_bias, a_log=a_log, d_skip=d_skip, ssm_norm_g=ssm_norm_g, conf_conv_w=conf_conv_w, conf_conv_b=conf_conv_b, conf_ln_g=conf_ln_g, conf_ln_b=conf_ln_b, loss_target=loss_target, m_meta=m_meta, m_pre_g=m_pre_g, m_post_g=m_post_g, m_w_in=m_w_in, m_w_out=m_w_out, m_conv_a_w=m_conv_a_w, m_ssm_conv_w=m_ssm_conv_w, m_ssm_conv_b=m_ssm_conv_b, m_dt_bias=m_dt_bias, m_a_log=m_a_log, m_d_skip=m_d_skip, m_ssm_norm_g=m_ssm_norm_g, m_conf_conv_w=m_conf_conv_w, m_conf_conv_b=m_conf_conv_b, m_conf_ln_g=m_conf_ln_g, m_conf_ln_b=m_conf_ln_b, v_meta=v_meta, v_pre_g=v_pre_g, v_post_g=v_post_g, v_w_in=v_w_in, v_w_out=v_w_out, v_conv_a_w=v_conv_a_w, v_ssm_conv_w=v_ssm_conv_w, v_ssm_conv_b=v_ssm_conv_b, v_dt_bias=v_dt_bias, v_a_log=v_a_log, v_d_skip=v_d_skip, v_ssm_norm_g=v_ssm_norm_g, v_conf_conv_w=v_conf_conv_w, v_conf_conv_b=v_conf_conv_b, v_conf_ln_g=v_conf_ln_g, v_conf_ln_b=v_conf_ln_b)
    weights = {n: given[n] for n in TWIN_WEIGHTS}
    shared = {n: given[n] for n in SHARED_INPUTS}
    per_example = {n: given[n] for n in ['x']}
    grad_fn = _jax.value_and_grad(_loss, argnums=(0, 1))

    def one_microbatch(ex, loss_target):
        ex = dict(ex)
        diff = ex.pop(TWIN_DIFF_INPUT)
        return grad_fn(weights, diff, {**shared, **ex}, loss_target)

    if N_MICROBATCH == 1:
        loss, (grad_w, grad_x) = one_microbatch(per_example, given["loss_target"])
    else:
        def body(carry, xs):
            loss_sum, grad_sum = carry
            l_k, (gw_k, gx_k) = one_microbatch(xs[0], xs[1])
            with _jax.named_scope("update"):
                return (loss_sum + l_k, _jax.tree.map(_jnp.add, grad_sum, gw_k)), gx_k

        init = (_jnp.zeros((), _jnp.float32), _jax.tree.map(_jnp.zeros_like, weights))
        (loss, grad_w), grad_x = _jax.lax.scan(body, init, (per_example, given["loss_target"]))
    with _jax.named_scope("update"):
        delta_w, new_m, new_v = {}, {}, {}
        for n in TWIN_WEIGHTS:
            delta_w[n], new_m[n], new_v[n] = _adamw(weights[n], grad_w[n], given["m_" + n], given["v_" + n])
    return (loss, grad_x, *[grad_w[n] for n in TWIN_WEIGHTS], *[delta_w[n] for n in TWIN_WEIGHTS],
            *[new_m[n] for n in TWIN_WEIGHTS], *[new_v[n] for n in TWIN_WEIGHTS])
```

```python
import functools

import jax
import jax.numpy as jnp
from jax import lax
from jax.experimental import pallas as pl
from jax.experimental.pallas import tpu as pltpu

F32 = jnp.float32
BF16 = jnp.bfloat16

D_MODEL = 1024
DEPTH = 4
SEQ = 2048
CHUNK = 64
N_META = 16
PAD = 48
LP = PAD + N_META + SEQ
D_A = 512
D_B = 1024
D_C = 512
N_HEADS = 16
HEAD_DIM = 64
N_STATE = 128
N_GROUPS = 2
GROUP_W = D_B // N_GROUPS
N_XBC = D_B + 2 * N_GROUPS * N_STATE
CONV_A_K = 3
SSM_K = 4
CONF_K = 31
NORM_EPS = 1e-6
LN_EPS = 1e-5
N_IN = 6160
N_INP = 6272
COL_BZ = 2048
COL_XBC = 3072
COL_C = 4608
COL_DT = 6144
LANE = 128
SUB = 8

ADAM_LR = 0.001
ADAM_B1 = 0.9
ADAM_B2 = 0.999
ADAM_EPS = 1e-08
ADAM_WD = 0.01
ADAM_STEP = 10

TILE = 192
HALO_A = 8
HALO_C = 32
MM_TM = 384
MM_TN = 896
VMEM_LIMIT = 56 * 1024 * 1024

MESH = pl.DeviceIdType.MESH
N_DEV = 8


def _silu(x):
    return x * jax.nn.sigmoid(x)


def _dsilu(x):
    s = jax.nn.sigmoid(x)
    return s * (1.0 + x * (1.0 - s))


def _cparams(sem=None):
    return pltpu.CompilerParams(dimension_semantics=sem, vmem_limit_bytes=VMEM_LIMIT)


def _mm(a, b, mode, out_dtype, tm, tn, tk, name):
    if mode == "nn":
        (m, k), (_, n) = a.shape, b.shape
        a_spec = pl.BlockSpec((tm, tk), lambda i, j, q: (i, q))
        b_spec = pl.BlockSpec((tk, tn), lambda i, j, q: (q, j))
        dims = (((1,), (0,)), ((), ()))
    elif mode == "nt":
        (m, k), (n, _) = a.shape, b.shape
        a_spec = pl.BlockSpec((tm, tk), lambda i, j, q: (i, q))
        b_spec = pl.BlockSpec((tn, tk), lambda i, j, q: (j, q))
        dims = (((1,), (1,)), ((), ()))
    else:
        (k, m), (_, n) = a.shape, b.shape
        a_spec = pl.BlockSpec((tk, tm), lambda i, j, q: (q, i))
        b_spec = pl.BlockSpec((tk, tn), lambda i, j, q: (q, j))
        dims = (((0,), (0,)), ((), ()))
    assert m % tm == 0 and n % tn == 0 and k % tk == 0, (name, a.shape, b.shape)
    nk = k // tk

    def body(a_ref, b_ref, o_ref, acc_ref):
        part = lax.dot_general(a_ref[...].astype(BF16), b_ref[...].astype(BF16), dims, preferred_element_type=F32)
        if nk == 1:
            o_ref[...] = part.astype(o_ref.dtype)
        else:
            q = pl.program_id(2)

            @pl.when(q == 0)
            def _():
                acc_ref[...] = part

            @pl.when(q > 0)
            def _():
                acc_ref[...] += part

            @pl.when(q == nk - 1)
            def _():
                o_ref[...] = acc_ref[...].astype(o_ref.dtype)

    return pl.pallas_call(
        body, name=name, grid=(m // tm, n // tn, nk),
        in_specs=[a_spec, b_spec], out_specs=pl.BlockSpec((tm, tn), lambda i, j, q: (i, j)),
        out_shape=jax.ShapeDtypeStruct((m, n), out_dtype),
        scratch_shapes=[pltpu.VMEM((tm, tn) if nk > 1 else (SUB, LANE), F32)],
        compiler_params=_cparams(("parallel", "parallel", "arbitrary")),
    )(a, b)


def _row_mask(i, tpe, rows):
    r = lax.broadcasted_iota(jnp.int32, (rows, 1), 0)
    return jnp.logical_or((i % tpe) != 0, r >= PAD)


def _rms_fwd(h, g, name):
    t, d = h.shape

    def body(h_ref, g_ref, o_ref):
        x = h_ref[...]
        r = lax.rsqrt(jnp.mean(x * x, axis=-1, keepdims=True) + NORM_EPS)
        o_ref[...] = (x * r * g_ref[...]).astype(BF16)

    return pl.pallas_call(
        body, name=name, grid=(t // TILE,),
        in_specs=[pl.BlockSpec((TILE, d), lambda i: (i, 0)), pl.BlockSpec((1, d), lambda i: (0, 0))],
        out_specs=pl.BlockSpec((TILE, d), lambda i: (i, 0)),
        out_shape=jax.ShapeDtypeStruct((t, d), BF16),
        compiler_params=_cparams(("parallel",)),
    )(h, g)


def _post_fwd(h, m, g, name):
    t, d = h.shape

    def body(h_ref, m_ref, g_ref, o_ref):
        x = m_ref[...]
        r = lax.rsqrt(jnp.mean(x * x, axis=-1, keepdims=True) + NORM_EPS)
        o_ref[...] = h_ref[...] + x * r * g_ref[...]

    row = pl.BlockSpec((TILE, d), lambda i: (i, 0))
    return pl.pallas_call(
        body, name=name, grid=(t // TILE,),
        in_specs=[row, row, pl.BlockSpec((1, d), lambda i: (0, 0))], out_specs=row,
        out_shape=jax.ShapeDtypeStruct((t, d), F32),
        compiler_params=_cparams(("parallel",)),
    )(h, m, g)


def _rms_bwd_math(x, g, dy):
    r = lax.rsqrt(jnp.mean(x * x, axis=-1, keepdims=True) + NORM_EPS)
    gdy = dy * g
    dx = r * gdy - x * (r * r * r) * jnp.mean(gdy * x, axis=-1, keepdims=True)
    return dx, dy * x * r


def _post_bwd(dh, m, g, name):
    t, d = dh.shape

    def body(dh_ref, m_ref, g_ref, dm_ref, dg_ref):
        dm, dgt = _rms_bwd_math(m_ref[...], g_ref[...], dh_ref[...])
        dm_ref[...] = dm

        @pl.when(pl.program_id(0) == 0)
        def _():
            dg_ref[...] = jnp.zeros_like(dg_ref)

        dg_ref[...] += jnp.sum(dgt, axis=0, keepdims=True)

    row = pl.BlockSpec((TILE, d), lambda i: (i, 0))
    vec = pl.BlockSpec((1, d), lambda i: (0, 0))
    return pl.pallas_call(
        body, name=name, grid=(t // TILE,),
        in_specs=[row, row, vec], out_specs=[row, vec],
        out_shape=[jax.ShapeDtypeStruct((t, d), F32), jax.ShapeDtypeStruct((1, d), F32)],
        compiler_params=_cparams(("arbitrary",)),
    )(dh, m, g)


def _rms_bwd(dh_res, dhn, h, g, lp, name):
    t, d = h.shape
    tpe = lp // TILE

    def body(dr_ref, dn_ref, h_ref, g_ref, dh_ref, dg_ref, dmeta_ref):
        i = pl.program_id(0)
        dx, dgt = _rms_bwd_math(h_ref[...], g_ref[...], dn_ref[...])
        dh = dr_ref[...] + dx
        dh_ref[...] = dh

        @pl.when(i == 0)
        def _():
            dg_ref[...] = jnp.zeros_like(dg_ref)
            dmeta_ref[...] = jnp.zeros_like(dmeta_ref)

        dg_ref[...] += jnp.sum(dgt, axis=0, keepdims=True)

        @pl.when((i % tpe) == 0)
        def _():
            dmeta_ref[...] += dh[PAD:PAD + N_META, :]

    row = pl.BlockSpec((TILE, d), lambda i: (i, 0))
    vec = pl.BlockSpec((1, d), lambda i: (0, 0))
    return pl.pallas_call(
        body, name=name, grid=(t // TILE,),
        in_specs=[row, row, row, vec],
        out_specs=[row, vec, pl.BlockSpec((N_META, d), lambda i: (0, 0))],
        out_shape=[jax.ShapeDtypeStruct((t, d), F32), jax.ShapeDtypeStruct((1, d), F32),
                   jax.ShapeDtypeStruct((N_META, d), F32)],
        compiler_params=_cparams(("arbitrary",)),
    )(dh_res, dhn, h, g)


def _loss_kernel(h, target, lp, name):
    t, d = h.shape
    cpe = lp // CHUNK
    nb = t // lp

    def body(h_ref, t_ref, dh_ref, loss_ref):
        b, j = pl.program_id(0), pl.program_id(1)

        @pl.when(jnp.logical_and(b == 0, j == 0))
        def _():
            loss_ref[...] = jnp.zeros_like(loss_ref)

        @pl.when(j == 0)
        def _():
            dh_ref[...] = jnp.zeros_like(dh_ref)

        @pl.when(j > 0)
        def _():
            err = h_ref[...] - t_ref[...]
            dh_ref[...] = err * (1.0 / d)
            loss_ref[...] += (0.5 / d) * jnp.sum(err * err)

    return pl.pallas_call(
        body, name=name, grid=(nb, cpe),
        in_specs=[pl.BlockSpec((CHUNK, d), lambda b, j: (b * cpe + j, 0)),
                  pl.BlockSpec((CHUNK, d), lambda b, j: (b * (cpe - 1) + jnp.maximum(j - 1, 0), 0))],
        out_specs=[pl.BlockSpec((CHUNK, d), lambda b, j: (b * cpe + j, 0)),
                   pl.BlockSpec((SUB, LANE), lambda b, j: (0, 0))],
        out_shape=[jax.ShapeDtypeStruct((t, d), F32), jax.ShapeDtypeStruct((SUB, LANE), F32)],
        compiler_params=_cparams(("arbitrary", "arbitrary")),
    )(h, target)


def _halo_specs(t, lp, width, col, halo):
    per = TILE // halo
    last = t // halo - 1
    cur = pl.BlockSpec((TILE, width), lambda i: (i, col))
    prev = pl.BlockSpec((halo, width), lambda i: (jnp.maximum(i * per - 1, 0), col))
    nxt = pl.BlockSpec((halo, width), lambda i: (jnp.minimum((i + 1) * per, last), col))
    return cur, prev, nxt


def _conv_a(ve, w):
    rows = ve.shape[0] - HALO_A
    return (w[0:1] * ve[HALO_A - 2:HALO_A - 2 + rows] + w[1:2] * ve[HALO_A - 1:HALO_A - 1 + rows]
            + w[2:3] * ve[HALO_A:HALO_A + rows])


def _a_fwd(proj, w, lp, name):
    t = proj.shape[0]
    tpe = lp // TILE
    cur, prev, _ = _halo_specs(t, lp, 4 * D_A, 0, HALO_A)

    def body(p_ref, ph_ref, w_ref, y_ref):
        first = (pl.program_id(0) % tpe) == 0
        v = p_ref[:, D_A:2 * D_A] * p_ref[:, 2 * D_A:3 * D_A]
        vh = jnp.where(first, 0.0, ph_ref[:, D_A:2 * D_A] * ph_ref[:, 2 * D_A:3 * D_A])
        cv = _conv_a(jnp.concatenate([vh, v], axis=0), w_ref[...])
        y_ref[...] = (p_ref[:, 0:D_A] * cv * _silu(p_ref[:, 3 * D_A:4 * D_A])).astype(BF16)

    return pl.pallas_call(
        body, name=name, grid=(t // TILE,),
        in_specs=[cur, prev, pl.BlockSpec((CONV_A_K, D_A), lambda i: (0, 0))],
        out_specs=pl.BlockSpec((TILE, D_A), lambda i: (i, 0)),
        out_shape=jax.ShapeDtypeStruct((t, D_A), BF16),
        compiler_params=_cparams(("parallel",)),
    )(proj, proj, w)


def _a_bwd(dy, proj, w, lp, dy_col, name):
    t = proj.shape[0]
    tpe = lp // TILE
    cur, prev, nxt = _halo_specs(t, lp, 4 * D_A, 0, HALO_A)
    dcur, _, dnxt = _halo_specs(t, lp, D_A, dy_col, HALO_A)

    def body(dy_ref, dyn_ref, p_ref, ph_ref, pn_ref, w_ref, dp_ref, dw_ref):
        i = pl.program_id(0)
        first = (i % tpe) == 0
        last = (i % tpe) == tpe - 1
        w = w_ref[...]
        ab, ac, ax, az = (p_ref[:, k * D_A:(k + 1) * D_A] for k in range(4))
        v = ac * ax
        vh = jnp.where(first, 0.0, ph_ref[:, D_A:2 * D_A] * ph_ref[:, 2 * D_A:3 * D_A])
        ve = jnp.concatenate([vh, v], axis=0)
        cv = _conv_a(ve, w)
        s = _silu(az)
        dy_ = dy_ref[...]
        dcv = dy_ * ab * s
        dcvn = jnp.where(last, 0.0, dyn_ref[...] * pn_ref[:, 0:D_A] * _silu(pn_ref[:, 3 * D_A:4 * D_A]))
        dce = jnp.concatenate([dcv, dcvn], axis=0)
        dv = w[2:3] * dce[0:TILE] + w[1:2] * dce[1:TILE + 1] + w[0:1] * dce[2:TILE + 2]
        dp = jnp.concatenate([dy_ * cv * s, dv * ax, dv * ac, dy_ * ab * cv * _dsilu(az)], axis=1)
        dp_ref[...] = jnp.where(_row_mask(i, tpe, TILE), dp, 0.0).astype(BF16)
        dw = jnp.concatenate(
            [jnp.sum(dcv * ve[HALO_A - 2 + k:HALO_A - 2 + k + TILE], axis=0, keepdims=True) for k in range(CONV_A_K)], axis=0)

        @pl.when(i == 0)
        def _():
            dw_ref[...] = jnp.zeros_like(dw_ref)

        dw_ref[...] += dw

    wspec = pl.BlockSpec((CONV_A_K, D_A), lambda i: (0, 0))
    return pl.pallas_call(
        body, name=name, grid=(t // TILE,),
        in_specs=[dcur, dnxt, cur, prev, nxt, wspec],
        out_specs=[pl.BlockSpec((TILE, 4 * D_A), lambda i: (i, 0)), wspec],
        out_shape=[jax.ShapeDtypeStruct((t, 4 * D_A), BF16), jax.ShapeDtypeStruct((CONV_A_K, D_A), F32)],
        compiler_params=_cparams(("arbitrary",)),
    )(dy, dy, proj, proj, proj, w)


def _conv_ssm(xe, w, rows, off):
    acc = w[0:1] * xe[off - 3:off - 3 + rows]
    for k in range(1, SSM_K):
        acc = acc + w[k:k + 1] * xe[off - 3 + k:off - 3 + k + rows]
    return acc


def _xbc_fwd(proj, w, b, lp, name):
    t = proj.shape[0]
    tpe = lp // TILE
    cur, prev, _ = _halo_specs(t, lp, N_XBC, COL_XBC // N_XBC, HALO_A)

    def body(x_ref, xh_ref, w_ref, b_ref, o_ref):
        first = (pl.program_id(0) % tpe) == 0
        xh = jnp.where(first, 0.0, xh_ref[...])
        xe = jnp.concatenate([xh, x_ref[...]], axis=0)
        o_ref[...] = _silu(_conv_ssm(xe, w_ref[...], TILE, HALO_A) + b_ref[...])

    return pl.pallas_call(
        body, name=name, grid=(t // TILE,),
        in_specs=[cur, prev, pl.BlockSpec((SSM_K, N_XBC), lambda i: (0, 0)), pl.BlockSpec((1, N_XBC), lambda i: (0, 0))],
        out_specs=pl.BlockSpec((TILE, N_XBC), lambda i: (i, 0)),
        out_shape=jax.ShapeDtypeStruct((t, N_XBC), F32),
        compiler_params=_cparams(("parallel",)),
    )(proj, proj, w, b)


def _xbc_bwd(dxbc, proj, w, b, lp, name):
    t = proj.shape[0]
    tpe = lp // TILE
    cur, prev, nxt = _halo_specs(t, lp, N_XBC, COL_XBC // N_XBC, HALO_A)
    dcur, _, dnxt = _halo_specs(t, lp, N_XBC, 0, HALO_A)

    def body(d_ref, dn_ref, x_ref, xh_ref, xn_ref, w_ref, b_ref, dx_ref, dw_ref, db_ref):
        i = pl.program_id(0)
        first = (i % tpe) == 0
        last = (i % tpe) == tpe - 1
        w = w_ref[...]
        xh = jnp.where(first, 0.0, xh_ref[...])
        xe = jnp.concatenate([xh, x_ref[...], xn_ref[...]], axis=0)
        pre = _conv_ssm(xe, w, TILE + HALO_A, HALO_A) + b_ref[...]
        de = jnp.concatenate([d_ref[...], jnp.where(last, 0.0, dn_ref[...])], axis=0)
        dpre = de * _dsilu(pre)
        dx = w[3:4] * dpre[0:TILE]
        for k in range(SSM_K - 1):
            dx = dx + w[k:k + 1] * dpre[3 - k:3 - k + TILE]
        dx_ref[...] = jnp.where(_row_mask(i, tpe, TILE), dx, 0.0).astype(BF16)
        dpc = dpre[0:TILE]
        dw = jnp.concatenate(
            [jnp.sum(dpc * xe[HALO_A - 3 + k:HALO_A - 3 + k + TILE], axis=0, keepdims=True) for k in range(SSM_K)], axis=0)

        @pl.when(i == 0)
        def _():
            dw_ref[...] = jnp.zeros_like(dw_ref)
            db_ref[...] = jnp.zeros_like(db_ref)

        dw_ref[...] += dw
        db_ref[...] += jnp.sum(dpc, axis=0, keepdims=True)

    wspec = pl.BlockSpec((SSM_K, N_XBC), lambda i: (0, 0))
    bspec = pl.BlockSpec((1, N_XBC), lambda i: (0, 0))
    return pl.pallas_call(
        body, name=name, grid=(t // TILE,),
        in_specs=[dcur, dnxt, cur, prev, nxt, wspec, bspec],
        out_specs=[pl.BlockSpec((TILE, N_XBC), lambda i: (i, 0)), wspec, bspec],
        out_shape=[jax.ShapeDtypeStruct((t, N_XBC), BF16), jax.ShapeDtypeStruct((SSM_K, N_XBC), F32),
                   jax.ShapeDtypeStruct((1, N_XBC), F32)],
        compiler_params=_cparams(("arbitrary",)),
    )(dxbc, dxbc, proj, proj, proj, w, b)


SUBROWS = 32


def _conv_conf(scr, w, rows, off, base):
    acc = w[0:1] * scr[pl.ds(base + off - (CONF_K - 1), rows), :]
    for k in range(1, CONF_K):
        acc = acc + w[k:k + 1] * scr[pl.ds(base + off - (CONF_K - 1) + k, rows), :]
    return acc


def _ln_fwd(u1, g, b):
    mu = jnp.mean(u1, axis=-1, keepdims=True)
    xc = u1 - mu
    rstd = lax.rsqrt(jnp.mean(xc * xc, axis=-1, keepdims=True) + LN_EPS)
    n = xc * rstd
    return n, rstd, n * g + b


def _c_fwd(proj, w, cb, g, b, lp, name):
    t = proj.shape[0]
    tpe = lp // TILE
    cur, prev, _ = _halo_specs(t, lp, 3 * D_C, COL_C // (3 * D_C), HALO_C)

    def body(p_ref, ph_ref, w_ref, cb_ref, g_ref, b_ref, y_ref, u0_scr):
        first = (pl.program_id(0) % tpe) == 0
        u0_scr[0:HALO_C, :] = jnp.where(first, 0.0, ph_ref[:, 0:D_C] * jax.nn.sigmoid(ph_ref[:, D_C:2 * D_C]))
        u0_scr[HALO_C:HALO_C + TILE, :] = p_ref[:, 0:D_C] * jax.nn.sigmoid(p_ref[:, D_C:2 * D_C])
        w = w_ref[...]
        for r0 in range(0, TILE, SUBROWS):
            u1 = _conv_conf(u0_scr, w, SUBROWS, HALO_C, r0) + cb_ref[...]
            _, _, u2 = _ln_fwd(u1, g_ref[...], b_ref[...])
            y_ref[r0:r0 + SUBROWS, :] = (_silu(u2) * _silu(p_ref[r0:r0 + SUBROWS, 2 * D_C:3 * D_C])).astype(BF16)

    vec = pl.BlockSpec((1, D_C), lambda i: (0, 0))
    return pl.pallas_call(
        body, name=name, grid=(t // TILE,),
        in_specs=[cur, prev, pl.BlockSpec((CONF_K, D_C), lambda i: (0, 0)), vec, vec, vec],
        out_specs=pl.BlockSpec((TILE, D_C), lambda i: (i, 0)),
        out_shape=jax.ShapeDtypeStruct((t, D_C), BF16),
        scratch_shapes=[pltpu.VMEM((HALO_C + TILE, D_C), F32)],
        compiler_params=_cparams(("parallel",)),
    )(proj, proj, w, cb, g, b)


def _c_bwd(dy, proj, w, cb, g, b, lp, dy_col, name):
    t = proj.shape[0]
    tpe = lp // TILE
    cur, prev, nxt = _halo_specs(t, lp, 3 * D_C, COL_C // (3 * D_C), HALO_C)
    dcur, _, dnxt = _halo_specs(t, lp, D_C, dy_col, HALO_C)
    ext = TILE + HALO_C

    def body(dy_ref, dyn_ref, p_ref, ph_ref, pn_ref, w_ref, cb_ref, g_ref, b_ref,
             dp_ref, dw_ref, dcb_ref, dg_ref, db_ref, u0_scr, du1_scr, wacc_scr):
        i = pl.program_id(0)
        first = (i % tpe) == 0
        last = (i % tpe) == tpe - 1
        w = w_ref[...]

        @pl.when(i == 0)
        def _():
            dw_ref[...] = jnp.zeros_like(dw_ref)
            dcb_ref[...] = jnp.zeros_like(dcb_ref)
            dg_ref[...] = jnp.zeros_like(dg_ref)
            db_ref[...] = jnp.zeros_like(db_ref)

        u0_scr[0:HALO_C, :] = jnp.where(first, 0.0, ph_ref[:, 0:D_C] * jax.nn.sigmoid(ph_ref[:, D_C:2 * D_C]))
        u0_scr[HALO_C:ext, :] = p_ref[:, 0:D_C] * jax.nn.sigmoid(p_ref[:, D_C:2 * D_C])
        u0_scr[ext:ext + HALO_C, :] = pn_ref[:, 0:D_C] * jax.nn.sigmoid(pn_ref[:, D_C:2 * D_C])
        du1_scr[ext:ext + SUB, :] = jnp.zeros((SUB, D_C), F32)
        dcb = jnp.zeros((1, D_C), F32)
        dg = jnp.zeros((1, D_C), F32)
        db = jnp.zeros((1, D_C), F32)
        for r0 in range(0, ext, SUBROWS):
            in_tile = r0 < TILE
            src, dsrc, q0 = (p_ref, dy_ref, r0) if in_tile else (pn_ref, dyn_ref, r0 - TILE)
            u1 = _conv_conf(u0_scr, w, SUBROWS, HALO_C, r0) + cb_ref[...]
            n, rstd, u2 = _ln_fwd(u1, g_ref[...], b_ref[...])
            cz = src[q0:q0 + SUBROWS, 2 * D_C:3 * D_C]
            dyc = dsrc[q0:q0 + SUBROWS, :]
            if not in_tile:
                dyc = jnp.where(last, 0.0, dyc)
            du2 = dyc * _silu(cz) * _dsilu(u2)
            dn = du2 * g_ref[...]
            du1 = rstd * (dn - jnp.mean(dn, axis=-1, keepdims=True) - n * jnp.mean(dn * n, axis=-1, keepdims=True))
            du1_scr[r0:r0 + SUBROWS, :] = du1
            if in_tile:
                dp_ref[r0:r0 + SUBROWS, 2 * D_C:3 * D_C] = (dyc * _silu(u2) * _dsilu(cz)).astype(BF16)
                dg = dg + jnp.sum(du2 * n, axis=0, keepdims=True)
                db = db + jnp.sum(du2, axis=0, keepdims=True)
                dcb = dcb + jnp.sum(du1, axis=0, keepdims=True)
        dcb_ref[...] += dcb
        dg_ref[...] += dg
        db_ref[...] += db
        mask = _row_mask(i, tpe, TILE)
        for r0 in range(0, TILE, SUBROWS):
            acc = w[0:1] * du1_scr[pl.ds(r0 + CONF_K - 1, SUBROWS), :]
            for k in range(1, CONF_K):
                acc = acc + w[k:k + 1] * du1_scr[pl.ds(r0 + CONF_K - 1 - k, SUBROWS), :]
            ca = p_ref[r0:r0 + SUBROWS, 0:D_C]
            sg = jax.nn.sigmoid(p_ref[r0:r0 + SUBROWS, D_C:2 * D_C])
            m = mask[r0:r0 + SUBROWS]
            dp_ref[r0:r0 + SUBROWS, 0:D_C] = jnp.where(m, acc * sg, 0.0).astype(BF16)
            dp_ref[r0:r0 + SUBROWS, D_C:2 * D_C] = jnp.where(m, acc * ca * sg * (1.0 - sg), 0.0).astype(BF16)
        for k in range(CONF_K):
            part = jnp.zeros((SUB, D_C), F32)
            for r0 in range(0, TILE, SUBROWS):
                prod = du1_scr[r0:r0 + SUBROWS, :] * u0_scr[pl.ds(HALO_C + r0 - (CONF_K - 1) + k, SUBROWS), :]
                for q in range(0, SUBROWS, SUB):
                    part = part + prod[q:q + SUB]
            wacc_scr[k:k + 1, :] = jnp.sum(part, axis=0, keepdims=True)
        dw_ref[...] += wacc_scr[0:CONF_K, :]

    vec = pl.BlockSpec((1, D_C), lambda i: (0, 0))
    wspec = pl.BlockSpec((CONF_K, D_C), lambda i: (0, 0))
    return pl.pallas_call(
        body, name=name, grid=(t // TILE,),
        in_specs=[dcur, dnxt, cur, prev, nxt, wspec, vec, vec, vec],
        out_specs=[pl.BlockSpec((TILE, 3 * D_C), lambda i: (i, 0)), wspec, vec, vec, vec],
        out_shape=[jax.ShapeDtypeStruct((t, 3 * D_C), BF16), jax.ShapeDtypeStruct((CONF_K, D_C), F32),
                   jax.ShapeDtypeStruct((1, D_C), F32), jax.ShapeDtypeStruct((1, D_C), F32),
                   jax.ShapeDtypeStruct((1, D_C), F32)],
        scratch_shapes=[pltpu.VMEM((ext + HALO_C, D_C), F32), pltpu.VMEM((ext + SUB, D_C), F32),
                        pltpu.VMEM((HALO_C, D_C), F32)],
        compiler_params=_cparams(("arbitrary",)),
    )(dy, dy, proj, proj, proj, w, cb, g, b)


def _split_dot(x, m_bf16, terms):
    acc = None
    rem = x
    for _ in range(terms):
        hi = rem.astype(BF16)
        part = jnp.dot(hi, m_bf16, preferred_element_type=F32)
        acc = part if acc is None else acc + part
        rem = rem - hi.astype(F32)
    return acc


def _split_dot_left(m_bf16, x, terms):
    acc = None
    rem = x
    for _ in range(terms):
        hi = rem.astype(BF16)
        part = jnp.dot(m_bf16, hi, preferred_element_type=F32)
        acc = part if acc is None else acc + part
        rem = rem - hi.astype(F32)
    return acc


def _tri(rows_ge_cols):
    r = lax.broadcasted_iota(jnp.int32, (CHUNK, CHUNK), 0)
    c = lax.broadcasted_iota(jnp.int32, (CHUNK, CHUNK), 1)
    return (r >= c) if rows_ge_cols else (r <= c)


def _softplus(x):
    return jnp.maximum(x, 0.0) + jnp.log(1.0 + jnp.exp(-jnp.abs(x)))


def _ssd_common(dtraw, dtb, dtb_t, a_log, a_log_t, e_mat, valid_col, valid_row):
    a = -jnp.exp(a_log)
    lane = lax.broadcasted_iota(jnp.int32, (1, LANE), 1)
    a = jnp.where(lane < N_HEADS, a, 0.0)
    a_t = -jnp.exp(a_log_t)
    dt = jnp.where(valid_col, _softplus(dtraw + dtb), 0.0)
    dt = jnp.where(lane < N_HEADS, dt, 0.0)
    dt_t = jnp.where(valid_row, _softplus(dtraw.T[0:N_HEADS, :] + dtb_t), 0.0)
    ltri = _tri(True).astype(BF16)
    utri = _tri(False).astype(BF16)
    big_a = _split_dot_left(ltri, dt * a, 3)
    big_a_t = _split_dot(dt_t * a_t, utri, 3)
    e_a = jnp.exp(big_a)
    d_s = jnp.exp(big_a[CHUNK - 1:CHUNK, :] - big_a)
    dt_x = _split_dot(dt, e_mat, 2)
    e_a_x = _split_dot(e_a, e_mat, 2)
    d_s_x = _split_dot(d_s, e_mat, 2)
    cd_x = e_a_x[CHUNK - 1:CHUNK, :]
    return a, dt, big_a, big_a_t, e_a, d_s, dt_x, e_a_x, d_s_x, cd_x


def _decay(big_a, big_a_t, h, transposed):
    col = big_a[:, h:h + 1]
    row = big_a_t[h:h + 1, :]
    if not transposed:
        seg = col - row
        return jnp.where(_tri(True), jnp.exp(jnp.minimum(seg, 0.0)), 0.0)
    seg = row - col
    return jnp.where(_tri(False), jnp.exp(jnp.minimum(seg, 0.0)), 0.0)


NT_DIMS = (((1,), (1,)), ((), ()))
TN_DIMS = (((0,), (0,)), ((), ()))


def _ssd_fwd(xbc, proj, dtb, a_log, d_skip_x, norm_g, e_mat, lp, name):
    t = xbc.shape[0]
    cpe = lp // CHUNK
    nb = t // lp
    dtb_p = jnp.pad(dtb.reshape(1, N_HEADS), ((0, 0), (0, LANE - N_HEADS)))
    alog_p = jnp.pad(a_log.reshape(1, N_HEADS), ((0, 0), (0, LANE - N_HEADS)))
    dtb_t = dtb.reshape(N_HEADS, 1)
    alog_t = a_log.reshape(N_HEADS, 1)

    def body(xbc_ref, bz_ref, dt_ref, dtb_ref, dtbt_ref, al_ref, alt_ref, dx_ref, g_ref, e_ref,
             yb_ref, ys_ref, st_ref, s_scr):
        c = pl.program_id(1)

        @pl.when(c == 0)
        def _():
            s_scr[...] = jnp.zeros_like(s_scr)

        rows = lax.broadcasted_iota(jnp.int32, (CHUNK, 1), 0)
        cols = lax.broadcasted_iota(jnp.int32, (1, CHUNK), 1)
        valid_col = jnp.logical_or(c > 0, rows >= PAD)
        valid_row = jnp.logical_or(c > 0, cols >= PAD)
        e_mat = e_ref[...]
        _, _, big_a, big_a_t, _, _, dt_x, e_a_x, d_s_x, cd_x = _ssd_common(
            dt_ref[...], dtb_ref[...], dtbt_ref[...], al_ref[...], alt_ref[...], e_mat, valid_col, valid_row)
        xs = xbc_ref[:, 0:D_B]
        bs = xbc_ref[:, D_B:D_B + N_GROUPS * N_STATE].astype(BF16)
        cs = xbc_ref[:, D_B + N_GROUPS * N_STATE:N_XBC].astype(BF16)
        xdt = xs * dt_x
        xdt_b = xdt.astype(BF16)
        st_prev = s_scr[...]
        st_ref[...] = st_prev.astype(BF16)
        st_b = st_prev.astype(BF16)
        u_b = (xdt * d_s_x).astype(BF16)
        y_parts = []
        for g in range(N_GROUPS):
            gs = slice(g * N_STATE, (g + 1) * N_STATE)
            gw = slice(g * GROUP_W, (g + 1) * GROUP_W)
            cb = lax.dot_general(cs[:, gs], bs[:, gs], NT_DIMS, preferred_element_type=F32)
            y_off = jnp.dot(cs[:, gs], st_b[:, gw], preferred_element_type=F32)
            diag = []
            for e in range(N_HEADS // N_GROUPS):
                h = g * (N_HEADS // N_GROUPS) + e
                m = (cb * _decay(big_a, big_a_t, h, False)).astype(BF16)
                diag.append(jnp.dot(m, xdt_b[:, h * HEAD_DIM:(h + 1) * HEAD_DIM], preferred_element_type=F32))
            y_parts.append(jnp.concatenate(diag, axis=1) + y_off * e_a_x[:, gw])
            new_st = lax.dot_general(bs[:, gs], u_b[:, gw], TN_DIMS, preferred_element_type=F32)
            s_scr[:, gw] = st_prev[:, gw] * cd_x[:, gw] + new_st
        y = jnp.concatenate(y_parts, axis=1) + xs * dx_ref[...]
        ys_ref[...] = y
        z = y * _silu(bz_ref[...])
        r = lax.rsqrt(jnp.mean(z * z, axis=-1, keepdims=True) + NORM_EPS)
        yb_ref[...] = (z * r * g_ref[...]).astype(BF16)

    def row(width, col):
        return pl.BlockSpec((CHUNK, width), lambda b, c: (b * cpe + c, col))

    def const(shape):
        return pl.BlockSpec(shape, lambda b, c: (0,) * len(shape))

    return pl.pallas_call(
        body, name=name, grid=(nb, cpe),
        in_specs=[row(N_XBC, 0), row(D_B, COL_BZ // D_B), row(LANE, COL_DT // LANE),
                  const((1, LANE)), const((N_HEADS, 1)), const((1, LANE)), const((N_HEADS, 1)),
                  const((1, D_B)), const((1, D_B)), const((LANE, D_B))],
        out_specs=[row(D_B, 0), row(D_B, 0),
                   pl.BlockSpec((None, N_STATE, D_B), lambda b, c: (b * cpe + c, 0, 0))],
        out_shape=[jax.ShapeDtypeStruct((t, D_B), BF16), jax.ShapeDtypeStruct((t, D_B), F32),
                   jax.ShapeDtypeStruct((nb * cpe, N_STATE, D_B), BF16)],
        scratch_shapes=[pltpu.VMEM((N_STATE, D_B), F32)],
        compiler_params=_cparams(("arbitrary", "arbitrary")),
    )(xbc, proj, proj, dtb_p, dtb_t, alog_p, alog_t, d_skip_x, norm_g, e_mat)


def _ssd_bwd(dy, y_ssd, xbc, proj, states, dtb, a_log, d_skip_x, norm_g, e_mat, e_mat_t, lp, name):
    t = xbc.shape[0]
    cpe = lp // CHUNK
    nb = t // lp
    hpg = N_HEADS // N_GROUPS
    dtb_p = jnp.pad(dtb.reshape(1, N_HEADS), ((0, 0), (0, LANE - N_HEADS)))
    alog_p = jnp.pad(a_log.reshape(1, N_HEADS), ((0, 0), (0, LANE - N_HEADS)))
    dtb_t = dtb.reshape(N_HEADS, 1)
    alog_t = a_log.reshape(N_HEADS, 1)

    def body(dy_ref, ys_ref, xbc_ref, bz_ref, dt_ref, st_ref, dtb_ref, dtbt_ref, al_ref, alt_ref, dx_ref, g_ref,
             e_ref, et_ref, dxbc_ref, dbz_ref, ddt_ref, dg_ref, ddtb_ref, dal_ref, dd_ref, ds_scr):
        c = pl.program_id(1)
        cc = cpe - 1 - c

        @pl.when(c == 0)
        def _():
            ds_scr[...] = jnp.zeros_like(ds_scr)

        @pl.when(jnp.logical_and(pl.program_id(0) == 0, c == 0))
        def _():
            dg_ref[...] = jnp.zeros_like(dg_ref)
            ddtb_ref[...] = jnp.zeros_like(ddtb_ref)
            dal_ref[...] = jnp.zeros_like(dal_ref)
            dd_ref[...] = jnp.zeros_like(dd_ref)

        rows = lax.broadcasted_iota(jnp.int32, (CHUNK, 1), 0)
        cols = lax.broadcasted_iota(jnp.int32, (1, CHUNK), 1)
        valid_col = jnp.logical_or(cc > 0, rows >= PAD)
        valid_row = jnp.logical_or(cc > 0, cols >= PAD)
        e_mat = e_ref[...]
        e_mat_t = et_ref[...]
        dtraw = dt_ref[...]
        a, dt, big_a, big_a_t, e_a, d_s, dt_x, e_a_x, d_s_x, cd_x = _ssd_common(
            dtraw, dtb_ref[...], dtbt_ref[...], al_ref[...], alt_ref[...], e_mat, valid_col, valid_row)
        xs = xbc_ref[:, 0:D_B]
        bs = xbc_ref[:, D_B:D_B + N_GROUPS * N_STATE].astype(BF16)
        cs = xbc_ref[:, D_B + N_GROUPS * N_STATE:N_XBC].astype(BF16)
        xdt = xs * dt_x
        xdt_b = xdt.astype(BF16)
        st_b = st_ref[...]
        dst = ds_scr[...]
        dst_b = dst.astype(BF16)

        ys = ys_ref[...]
        bz = bz_ref[...]
        sil = _silu(bz)
        z = ys * sil
        dz, dgt = _rms_bwd_math(z, g_ref[...], dy_ref[...])
        dg_ref[...] += jnp.sum(dgt, axis=0, keepdims=True)
        dbz_ref[...] = (dz * ys * _dsilu(bz)).astype(BF16)
        dys = dz * sil

        dd_lane = jnp.sum(dys * xs, axis=0, keepdims=True)
        dxs = dys * dx_ref[...]
        w_x = dys * e_a_x
        w_b = w_x.astype(BF16)
        dys_b = dys.astype(BF16)
        u_b = (xdt * d_s_x).astype(BF16)
        dxdt_parts, dbs_parts, dcs_parts, off_parts, g1_parts = [], [], [], [], []
        da_diag = jnp.zeros((CHUNK, LANE), F32)
        lane = lax.broadcasted_iota(jnp.int32, (1, LANE), 1)
        for g in range(N_GROUPS):
            gs = slice(g * N_STATE, (g + 1) * N_STATE)
            gw = slice(g * GROUP_W, (g + 1) * GROUP_W)
            cs_g, bs_g = cs[:, gs], bs[:, gs]
            dcs = lax.dot_general(w_b[:, gw], st_b[:, gw], NT_DIMS, preferred_element_type=F32)
            y_off = jnp.dot(cs_g, st_b[:, gw], preferred_element_type=F32)
            off_parts.append(y_off)
            dst_new = lax.dot_general(cs_g, w_b[:, gw], TN_DIMS, preferred_element_type=F32)
            g1 = jnp.dot(bs_g, dst_b[:, gw], preferred_element_type=F32)
            g1_parts.append(g1)
            dbs = lax.dot_general(u_b[:, gw], dst_b[:, gw], NT_DIMS, preferred_element_type=F32)
            cb = lax.dot_general(cs_g, bs_g, NT_DIMS, preferred_element_type=F32)
            cbt = lax.dot_general(bs_g, cs_g, NT_DIMS, preferred_element_type=F32)
            dcb = jnp.zeros((CHUNK, CHUNK), F32)
            dcbt = jnp.zeros((CHUNK, CHUNK), F32)
            dxdt_h = []
            for e in range(hpg):
                h = g * hpg + e
                hs = slice(h * HEAD_DIM, (h + 1) * HEAD_DIM)
                dec = _decay(big_a, big_a_t, h, False)
                dect = _decay(big_a, big_a_t, h, True)
                m = cb * dec
                mt = cbt * dect
                dxdt_h.append(jnp.dot(mt.astype(BF16), dys_b[:, hs], preferred_element_type=F32))
                dm = lax.dot_general(dys_b[:, hs], xdt_b[:, hs], NT_DIMS, preferred_element_type=F32)
                dmt = lax.dot_general(xdt_b[:, hs], dys_b[:, hs], NT_DIMS, preferred_element_type=F32)
                dcb = dcb + dm * dec
                dcbt = dcbt + dmt * dect
                da_h = jnp.sum(dm * m - dmt * mt, axis=1, keepdims=True)
                da_diag = da_diag + jnp.where(lane == h, da_h, 0.0)
            dcs = dcs + jnp.dot(dcb.astype(BF16), bs_g, preferred_element_type=F32)
            dbs = dbs + jnp.dot(dcbt.astype(BF16), cs_g, preferred_element_type=F32)
            dxdt_parts.append(jnp.concatenate(dxdt_h, axis=1) + g1 * d_s_x[:, gw])
            dbs_parts.append(dbs)
            dcs_parts.append(dcs)
            ds_scr[:, gw] = dst[:, gw] * cd_x[:, gw] + dst_new
        dxdt = jnp.concatenate(dxdt_parts, axis=1)
        y_off = jnp.concatenate(off_parts, axis=1)
        g1 = jnp.concatenate(g1_parts, axis=1)
        dds = _split_dot(g1 * xdt, e_mat_t, 2)
        da_off = _split_dot(w_x * y_off, e_mat_t, 2)
        ddt_x = _split_dot(dxdt * xs, e_mat_t, 2)
        dcd_lane = jnp.sum(dst * st_b.astype(F32), axis=0, keepdims=True)
        dcd = _split_dot(jnp.broadcast_to(dcd_lane, (SUB, D_B)), e_mat_t, 2)[0:1]
        dd_ref[...] += _split_dot(jnp.broadcast_to(dd_lane, (SUB, D_B)), e_mat_t, 2)[0:1]
        t_ds = dds * d_s
        d_a = da_diag + da_off - t_ds
        last_row = jnp.sum(t_ds, axis=0, keepdims=True) + dcd * e_a[CHUNK - 1:CHUNK, :]
        d_a = d_a + jnp.where(rows == CHUNK - 1, last_row, 0.0)
        dda = _split_dot_left(_tri(False).astype(BF16), d_a, 3)
        ddt = dda * a + ddt_x
        dal_ref[...] += jnp.sum(dda * dt * a, axis=0, keepdims=True)
        ddtraw = jnp.where(valid_col, ddt * jax.nn.sigmoid(dtraw + dtb_ref[...]), 0.0)
        ddtraw = jnp.where(lane < N_HEADS, ddtraw, 0.0)
        ddtb_ref[...] += jnp.sum(ddtraw, axis=0, keepdims=True)
        ddt_ref[...] = ddtraw.astype(BF16)
        dxs = dxs + dxdt * dt_x
        dxbc = jnp.concatenate([dxs] + dbs_parts + dcs_parts, axis=1)
        dxbc_ref[...] = jnp.where(valid_col, dxbc, 0.0)

    def row(width, col):
        return pl.BlockSpec((CHUNK, width), lambda b, c: (b * cpe + cpe - 1 - c, col))

    def const(shape):
        return pl.BlockSpec(shape, lambda b, c: (0,) * len(shape))

    return pl.pallas_call(
        body, name=name, grid=(nb, cpe),
        in_specs=[row(D_B, 0), row(D_B, 0), row(N_XBC, 0), row(D_B, COL_BZ // D_B), row(LANE, COL_DT // LANE),
                  pl.BlockSpec((None, N_STATE, D_B), lambda b, c: (b * cpe + cpe - 1 - c, 0, 0)),
                  const((1, LANE)), const((N_HEADS, 1)), const((1, LANE)), const((N_HEADS, 1)),
                  const((1, D_B)), const((1, D_B)), const((LANE, D_B)), const((D_B, LANE))],
        out_specs=[row(N_XBC, 0), row(D_B, 0), row(LANE, 0),
                   const((1, D_B)), const((1, LANE)), const((1, LANE)), const((1, LANE))],
        out_shape=[jax.ShapeDtypeStruct((t, N_XBC), F32), jax.ShapeDtypeStruct((t, D_B), BF16),
                   jax.ShapeDtypeStruct((t, LANE), BF16), jax.ShapeDtypeStruct((1, D_B), F32),
                   jax.ShapeDtypeStruct((1, LANE), F32), jax.ShapeDtypeStruct((1, LANE), F32),
                   jax.ShapeDtypeStruct((1, LANE), F32)],
        scratch_shapes=[pltpu.VMEM((N_STATE, D_B), F32)],
        compiler_params=_cparams(("arbitrary", "arbitrary")),
    )(dy, y_ssd, xbc, proj, proj, states, dtb_p, dtb_t, alog_p, alog_t, d_skip_x, norm_g, e_mat, e_mat_t)


HBM_SPEC = pl.BlockSpec(memory_space=pl.ANY)


def _mesh_pos():
    return lax.axis_index("x"), lax.axis_index("y"), lax.axis_index("c")


def _allgather(arrays, name):
    n = len(arrays)

    def body(*refs):
        xs, outs = refs[:n], refs[n:2 * n]
        send_sems, recv_sems, local_sems = refs[2 * n:]
        x, y, c = _mesh_pos()
        me, sibling = (x, y, c), (x, y, 1 - c)
        chips = [(1 - x, y), (x, 1 - y), (1 - x, 1 - y)]

        def slot(px, py, pc):
            return 4 * px + 2 * py + pc

        def copy(a, k, block, to, src=None):
            dst = outs[a].at[slot(*block)]
            return pltpu.make_async_remote_copy(
                src_ref=dst if src is None else src, dst_ref=dst, send_sem=send_sems.at[a, k], recv_sem=recv_sems.at[a, k],
                device_id=to, device_id_type=MESH)

        mine = [pltpu.make_async_copy(xs[a], outs[a].at[slot(*me)], local_sems.at[a]) for a in range(n)]
        for cp in mine:
            cp.start()
        first = []
        for a in range(n):
            first.append(copy(a, 0, me, sibling, src=xs[a]))
            first += [copy(a, 1 + j, me, (*chip, c), src=xs[a]) for j, chip in enumerate(chips)]
        for cp in first:
            cp.start()
        passed = []
        for j, chip in enumerate(chips):
            for a in range(n):
                copy(a, 1 + j, (*chip, c), me).wait_recv()
                cp = copy(a, 4 + j, (*chip, c), sibling)
                cp.start()
                passed.append(cp)
        for a in range(n):
            copy(a, 0, sibling, me).wait_recv()
        for j, chip in enumerate(chips):
            for a in range(n):
                copy(a, 4 + j, (*chip, 1 - c), me).wait_recv()
        for cp in first + passed:
            cp.wait_send()
        for cp in mine:
            cp.wait()

    return pl.pallas_call(
        body, name=name,
        in_specs=[HBM_SPEC] * n, out_specs=[HBM_SPEC] * n,
        out_shape=[jax.ShapeDtypeStruct((N_DEV,) + a.shape, a.dtype) for a in arrays],
        scratch_shapes=[pltpu.SemaphoreType.DMA((n, 7)), pltpu.SemaphoreType.DMA((n, 7)), pltpu.SemaphoreType.DMA((n,))],
    )(*arrays)


def _rs_sibling_exchange(arrays, name):
    n = len(arrays)

    def body(*refs):
        gs, outs = refs[:n], refs[n:2 * n]
        send_sems, recv_sems = refs[2 * n:]
        x, y, c = _mesh_pos()
        copies = []
        for a in range(n):
            for j in range(4):
                copies.append(pltpu.make_async_remote_copy(
                    src_ref=gs[a].at[2 * j + 1 - c], dst_ref=outs[a].at[j], send_sem=send_sems.at[a, j],
                    recv_sem=recv_sems.at[a, j], device_id=(x, y, 1 - c), device_id_type=MESH))
        for cp in copies:
            cp.start()
        for cp in copies:
            cp.wait()

    return pl.pallas_call(
        body, name=name,
        in_specs=[HBM_SPEC] * n, out_specs=[HBM_SPEC] * n,
        out_shape=[jax.ShapeDtypeStruct((4,) + a.shape[1:], a.dtype) for a in arrays],
        scratch_shapes=[pltpu.SemaphoreType.DMA((n, 4)), pltpu.SemaphoreType.DMA((n, 4))],
    )(*arrays)


def _rs_chip_exchange(arrays, name):
    n = len(arrays)

    def body(*refs):
        hs, outs = refs[:n], refs[n:2 * n]
        send_sems, recv_sems = refs[2 * n:]
        x, y, c = _mesh_pos()
        chips = [(1 - x, y), (x, 1 - y), (1 - x, 1 - y)]
        copies = []
        for a in range(n):
            for k, (px, py) in enumerate(chips):
                copies.append(pltpu.make_async_remote_copy(
                    src_ref=hs[a].at[2 * px + py], dst_ref=outs[a].at[k], send_sem=send_sems.at[a, k],
                    recv_sem=recv_sems.at[a, k], device_id=(px, py, c), device_id_type=MESH))
        for cp in copies:
            cp.start()
        for cp in copies:
            cp.wait()

    return pl.pallas_call(
        body, name=name,
        in_specs=[HBM_SPEC] * n, out_specs=[HBM_SPEC] * n,
        out_shape=[jax.ShapeDtypeStruct((3,) + a.shape[1:], a.dtype) for a in arrays],
        scratch_shapes=[pltpu.SemaphoreType.DMA((n, 3)), pltpu.SemaphoreType.DMA((n, 3))],
    )(*arrays)


def _rs_pair_sum(g, ra, c_idx, tr, name):
    _, rows, cols = g.shape

    def body(c_ref, g_ref, ra_ref, o_ref):
        o_ref[...] = g_ref[...] + ra_ref[...]

    return pl.pallas_call(
        body, name=name,
        grid_spec=pltpu.PrefetchScalarGridSpec(
            num_scalar_prefetch=1, grid=(4, rows // tr),
            in_specs=[pl.BlockSpec((None, tr, cols), lambda j, r, c: (2 * j + c[0], r, 0)),
                      pl.BlockSpec((None, tr, cols), lambda j, r, c: (j, r, 0))],
            out_specs=pl.BlockSpec((None, tr, cols), lambda j, r, c: (j, r, 0))),
        out_shape=jax.ShapeDtypeStruct((4, rows, cols), F32),
        compiler_params=_cparams(("parallel", "parallel")),
    )(c_idx, g, ra)


def _adamw_math(w, g, m, v):
    m = ADAM_B1 * m + (1.0 - ADAM_B1) * g
    v = ADAM_B2 * v + (1.0 - ADAM_B2) * (g * g)
    m_hat = m / (1.0 - ADAM_B1 ** ADAM_STEP)
    v_hat = v / (1.0 - ADAM_B2 ** ADAM_STEP)
    delta = -ADAM_LR * (m_hat / (jnp.sqrt(v_hat) + ADAM_EPS) + ADAM_WD * w)
    return delta, m, v


def _adamw_big(h, rb, chip_idx, w, m, v, tr, name):
    rows, cols = w.shape

    def body(j_ref, h_ref, r0_ref, r1_ref, r2_ref, w_ref, m_ref, v_ref, g_out, d_out, m_out, v_out):
        g = ((h_ref[...] + r0_ref[...]) + r1_ref[...]) + r2_ref[...]
        delta, m_new, v_new = _adamw_math(w_ref[...], g, m_ref[...], v_ref[...])
        g_out[...] = g
        d_out[...] = delta
        m_out[...] = m_new
        v_out[...] = v_new

    flat = pl.BlockSpec((tr, cols), lambda r, j: (r, 0))

    def rb_spec(k):
        return pl.BlockSpec((None, tr, cols), lambda r, j: (k, r, 0))

    return pl.pallas_call(
        body, name=name,
        grid_spec=pltpu.PrefetchScalarGridSpec(
            num_scalar_prefetch=1, grid=(rows // tr,),
            in_specs=[pl.BlockSpec((None, tr, cols), lambda r, j: (j[0], r, 0)), rb_spec(0), rb_spec(1), rb_spec(2),
                      flat, flat, flat],
            out_specs=[flat, flat, flat, flat]),
        out_shape=[jax.ShapeDtypeStruct((rows, cols), F32)] * 4,
        compiler_params=_cparams(("parallel",)),
    )(chip_idx, h, rb, rb, rb, w, m, v)


def _sum_devices(parts, name):
    _, p, _ = parts.shape

    def body(x_ref, o_ref):
        acc = x_ref[0]
        for d in range(1, N_DEV):
            acc = acc + x_ref[d]
        o_ref[...] = acc

    return pl.pallas_call(
        body, name=name, grid=(1,),
        in_specs=[pl.BlockSpec((N_DEV, p, LANE), lambda i: (0, 0, 0))],
        out_specs=pl.BlockSpec((p, LANE), lambda i: (0, 0)),
        out_shape=jax.ShapeDtypeStruct((p, LANE), F32),
        compiler_params=_cparams(("arbitrary",)),
    )(parts)


def _adamw_small(g, w, m, v, name):
    p = g.shape[0]

    def body(g_ref, w_ref, m_ref, v_ref, d_out, m_out, v_out):
        delta, m_new, v_new = _adamw_math(w_ref[...], g_ref[...], m_ref[...], v_ref[...])
        d_out[...] = delta
        m_out[...] = m_new
        v_out[...] = v_new

    spec = pl.BlockSpec((p, LANE), lambda i: (0, 0))
    return pl.pallas_call(
        body, name=name, grid=(1,),
        in_specs=[spec] * 4, out_specs=[spec] * 3,
        out_shape=[jax.ShapeDtypeStruct((p, LANE), F32)] * 3,
        compiler_params=_cparams(("arbitrary",)),
    )(g, w, m, v)


PACK_ALIGN = SUB * LANE

SMALL_PARAMS = (
    ("meta", (N_META, D_MODEL), 1),
    ("pre_g", (DEPTH, D_MODEL), None),
    ("post_g", (DEPTH, D_MODEL), None),
    ("conv_a_w", (DEPTH, CONV_A_K, D_A), 2),
    ("ssm_conv_w", (DEPTH, SSM_K, N_XBC), 2),
    ("ssm_conv_b", (DEPTH, N_XBC), None),
    ("dt_bias", (DEPTH, N_HEADS), None),
    ("a_log", (DEPTH, N_HEADS), None),
    ("d_skip", (DEPTH, N_HEADS), None),
    ("ssm_norm_g", (DEPTH, D_B), None),
    ("conf_conv_w", (DEPTH, CONF_K, D_C), 2),
    ("conf_conv_b", (DEPTH, D_C), None),
    ("conf_ln_g", (DEPTH, D_C), None),
    ("conf_ln_b", (DEPTH, D_C), None),
)


def _local_shape(shape, axis):
    if axis is None:
        return shape
    return tuple(s // N_DEV if k == axis else s for k, s in enumerate(shape))


def _pack(arrays):
    flat = []
    for a in arrays:
        v = a.reshape(-1).astype(F32)
        flat.append(jnp.pad(v, (0, (-v.shape[0]) % PACK_ALIGN)))
    return jnp.concatenate(flat).reshape(-1, LANE)


def _unpack(buf, shapes):
    flat = buf.reshape(-1)
    out, off = [], 0
    for s in shapes:
        size = 1
        for k in s:
            size *= k
        out.append(flat[off:off + size].reshape(s))
        off += size + (-size) % PACK_ALIGN
    return out


def _perm_cols(w):
    pad = jnp.zeros(w.shape[:-1] + (N_INP - N_IN,), w.dtype)
    return jnp.concatenate([w[..., :COL_C], w[..., COL_C + N_HEADS:N_IN], w[..., COL_C:COL_C + N_HEADS], pad], axis=-1)


def _unperm_cols(w):
    return jnp.concatenate([w[..., :COL_C], w[..., COL_DT:COL_DT + N_HEADS], w[..., COL_C:COL_DT]], axis=-1)


def kernel(x, meta, pre_g, post_g, w_in, w_out, conv_a_w, ssm_conv_w, ssm_conv_b, dt_bias, a_log, d_skip, ssm_norm_g, conf_conv_w, conf_conv_b, conf_ln_g, conf_ln_b, loss_target, m_meta, m_pre_g, m_post_g, m_w_in, m_w_out, m_conv_a_w, m_ssm_conv_w, m_ssm_conv_b, m_dt_bias, m_a_log, m_d_skip, m_ssm_norm_g, m_conf_conv_w, m_conf_conv_b, m_conf_ln_g, m_conf_ln_b, v_meta, v_pre_g, v_post_g, v_w_in, v_w_out, v_conv_a_w, v_ssm_conv_w, v_ssm_conv_b, v_dt_bias, v_a_log, v_d_skip, v_ssm_norm_g, v_conf_conv_w, v_conf_conv_b, v_conf_ln_g, v_conf_ln_b):
    weights = dict(meta=meta, pre_g=pre_g, post_g=post_g, conv_a_w=conv_a_w, ssm_conv_w=ssm_conv_w, ssm_conv_b=ssm_conv_b,
                   dt_bias=dt_bias, a_log=a_log, d_skip=d_skip, ssm_norm_g=ssm_norm_g, conf_conv_w=conf_conv_w,
                   conf_conv_b=conf_conv_b, conf_ln_g=conf_ln_g, conf_ln_b=conf_ln_b)
    mom1 = dict(meta=m_meta, pre_g=m_pre_g, post_g=m_post_g, conv_a_w=m_conv_a_w, ssm_conv_w=m_ssm_conv_w,
                ssm_conv_b=m_ssm_conv_b, dt_bias=m_dt_bias, a_log=m_a_log, d_skip=m_d_skip, ssm_norm_g=m_ssm_norm_g,
                conf_conv_w=m_conf_conv_w, conf_conv_b=m_conf_conv_b, conf_ln_g=m_conf_ln_g, conf_ln_b=m_conf_ln_b)
    mom2 = dict(meta=v_meta, pre_g=v_pre_g, post_g=v_post_g, conv_a_w=v_conv_a_w, ssm_conv_w=v_ssm_conv_w,
                ssm_conv_b=v_ssm_conv_b, dt_bias=v_dt_bias, a_log=v_a_log, d_skip=v_d_skip, ssm_norm_g=v_ssm_norm_g,
                conf_conv_w=v_conf_conv_w, conf_conv_b=v_conf_conv_b, conf_ln_g=v_conf_ln_g, conf_ln_b=v_conf_ln_b)
    nb, seq, d = x.shape
    lp = PAD + N_META + seq
    t = nb * lp
    assert lp % TILE == 0 and t % MM_TM == 0 and d == D_MODEL
    xi, yi, ci = _mesh_pos()
    dev = 4 * xi + 2 * yi + ci
    c_idx = ci.astype(jnp.int32).reshape(1)
    chip_idx = (2 * xi + yi).astype(jnp.int32).reshape(1)
    n_in_loc = N_IN // N_DEV
    n_out_loc = 2 * D_MODEL // N_DEV

    sharded_small = [n for n, _, ax in SMALL_PARAMS if ax is not None]
    small_shapes = {n: s for n, s, _ in SMALL_PARAMS}
    small_axis = {n: ax for n, _, ax in SMALL_PARAMS}
    sw_pack = _pack([weights[n] for n in sharded_small])
    w_in_g, w_out_g, sw_g = _allgather([w_in.astype(BF16), w_out.astype(BF16), sw_pack], "ag_weights")
    w_in_full = _perm_cols(jnp.transpose(w_in_g, (1, 2, 0, 3)).reshape(DEPTH, D_MODEL, N_IN))
    w_out_full = jnp.transpose(w_out_g, (1, 0, 2, 3)).reshape(DEPTH, 2 * D_MODEL, D_MODEL)
    w_out_full = jnp.concatenate([w_out_full[:, D_A:D_A + D_B], w_out_full[:, :D_A], w_out_full[:, D_A + D_B:]], axis=1)
    full = dict(weights)
    per_dev = [_unpack(sw_g[k], [_local_shape(small_shapes[n], small_axis[n]) for n in sharded_small]) for k in range(N_DEV)]
    for q, n in enumerate(sharded_small):
        full[n] = jnp.concatenate([per_dev[k][q] for k in range(N_DEV)], axis=small_axis[n])

    e_mat = (lax.broadcasted_iota(jnp.int32, (LANE, D_B), 0) == lax.broadcasted_iota(jnp.int32, (LANE, D_B), 1) // HEAD_DIM)
    e_mat = e_mat.astype(BF16)
    e_mat_t = e_mat.T

    front = jnp.concatenate([jnp.zeros((PAD, d), F32), full["meta"]], axis=0)
    h = jnp.concatenate([jnp.concatenate([front, x[b]], axis=0) for b in range(nb)], axis=0)
    saved = []
    for i in range(DEPTH):
        row = lambda a: a[i].reshape(1, -1)
        hn = _rms_fwd(h, row(pre_g), "rms_fwd")
        proj = _mm(hn, w_in_full[i], "nn", F32, MM_TM, MM_TN, D_MODEL, "mm_proj")
        ya = _a_fwd(proj, full["conv_a_w"][i], lp, "a_fwd")
        xbc = _xbc_fwd(proj, full["ssm_conv_w"][i], row(ssm_conv_b), lp, "xbc_fwd")
        yc = _c_fwd(proj, full["conf_conv_w"][i], row(conf_conv_b), row(conf_ln_g), row(conf_ln_b), lp, "c_fwd")
        d_skip_x = jnp.repeat(d_skip[i], HEAD_DIM).reshape(1, D_B)
        yb, yssd, states = _ssd_fwd(xbc, proj, dt_bias[i], a_log[i], d_skip_x, row(ssm_norm_g), e_mat, lp, "ssd_fwd")
        ycat = jnp.concatenate([yb, ya, yc], axis=1)
        m = _mm(ycat, w_out_full[i], "nn", F32, MM_TM, D_MODEL, 2 * D_MODEL, "mm_out")
        saved.append((h, hn, proj, ycat, yssd, states, xbc, m, d_skip_x))
        h = _post_fwd(h, m, row(post_g), "post_fwd")

    dh, loss_blk = _loss_kernel(h, loss_target.reshape(nb * seq, d), lp, "loss")

    grads = {n: [None] * DEPTH for n, _, _ in SMALL_PARAMS if n != "meta"}
    g_in, g_out = [None] * DEPTH, [None] * DEPTH
    dmeta = None
    for i in reversed(range(DEPTH)):
        row = lambda a: a[i].reshape(1, -1)
        h_i, hn, proj, ycat, yssd, states, xbc, m, d_skip_x = saved[i]
        dm, grads["post_g"][i] = _post_bwd(dh, m, row(post_g), "post_bwd")
        dy = _mm(dm, w_out_full[i], "nt", F32, MM_TM, 2 * D_MODEL, D_MODEL, "mm_dy")
        dw_out = _mm(ycat, dm, "tn", F32, D_MODEL, D_MODEL, MM_TM, "mm_dwout")
        dxbc, dbz, ddt, grads["ssm_norm_g"][i], ddtb, dal, dds = _ssd_bwd(
            dy, yssd, xbc, proj, states, dt_bias[i], a_log[i], d_skip_x, row(ssm_norm_g), e_mat, e_mat_t, lp, "ssd_bwd")
        grads["dt_bias"][i] = ddtb[:, :N_HEADS]
        grads["a_log"][i] = dal[:, :N_HEADS]
        grads["d_skip"][i] = dds[:, :N_HEADS]
        dpa, grads["conv_a_w"][i] = _a_bwd(dy, proj, full["conv_a_w"][i], lp, 2, "a_bwd")
        dpx, grads["ssm_conv_w"][i], grads["ssm_conv_b"][i] = _xbc_bwd(
            dxbc, proj, full["ssm_conv_w"][i], row(ssm_conv_b), lp, "xbc_bwd")
        dpc, grads["conf_conv_w"][i], grads["conf_conv_b"][i], grads["conf_ln_g"][i], grads["conf_ln_b"][i] = _c_bwd(
            dy, proj, full["conf_conv_w"][i], row(conf_conv_b), row(conf_ln_g), row(conf_ln_b), lp, 3, "c_bwd")
        dproj = jnp.concatenate([dpa, dbz, dpx, dpc, ddt], axis=1)
        dhn = _mm(dproj, w_in_full[i], "nt", F32, MM_TM, D_MODEL, MM_TN, "mm_dhn")
        dw_in = _mm(hn, dproj, "tn", F32, D_MODEL, MM_TN, MM_TM, "mm_dwin")
        dh, grads["pre_g"][i], dmeta = _rms_bwd(dh, dhn, h_i, row(pre_g), lp, "rms_bwd")
        g_in[i] = jnp.transpose(_unperm_cols(dw_in).reshape(D_MODEL, N_DEV, n_in_loc), (1, 0, 2))
        dw_out = jnp.concatenate([dw_out[D_B:D_B + D_A], dw_out[:D_B], dw_out[D_B + D_A:]], axis=0)
        g_out[i] = dw_out.reshape(N_DEV, n_out_loc, D_MODEL)
    grad_x = dh.reshape(nb, lp, d)[:, PAD + N_META:]

    g_in = jnp.stack(g_in, axis=1).reshape(N_DEV, DEPTH * D_MODEL, n_in_loc)
    g_out = jnp.stack(g_out, axis=1).reshape(N_DEV, DEPTH * n_out_loc, D_MODEL)
    ra_in, ra_out = _rs_sibling_exchange([g_in, g_out], "rs_sibling")
    h_in = _rs_pair_sum(g_in, ra_in, c_idx, 512, "rs_pair_sum_in")
    h_out = _rs_pair_sum(g_out, ra_out, c_idx, 256, "rs_pair_sum_out")
    rb_in, rb_out = _rs_chip_exchange([h_in, h_out], "rs_chips")
    big = {}
    big["w_in"] = _adamw_big(h_in, rb_in, chip_idx, w_in.reshape(-1, n_in_loc), m_w_in.reshape(-1, n_in_loc),
                             v_w_in.reshape(-1, n_in_loc), 512, "adamw_w_in")
    big["w_out"] = _adamw_big(h_out, rb_out, chip_idx, w_out.reshape(-1, D_MODEL), m_w_out.reshape(-1, D_MODEL),
                              v_w_out.reshape(-1, D_MODEL), 256, "adamw_w_out")
    big = {n: [a.reshape(s) for a in outs] for (n, outs), s in zip(big.items(), (w_in.shape, w_out.shape))}

    names = [n for n, _, _ in SMALL_PARAMS]
    partial = [loss_blk[0:1, 0:1], dmeta] + [jnp.concatenate(grads[n], axis=0) for n in names[1:]]
    part_pack = _pack(partial)
    (parts_g,) = _allgather([part_pack], "ag_small_grads")
    total = _unpack(_sum_devices(parts_g, "sum_small_grads"), [(1,)] + [small_shapes[n] for n in names])
    loss = total[0][0]
    g_small = {}
    for n, g in zip(names, total[1:]):
        ax = small_axis[n]
        if ax is not None:
            g = lax.dynamic_slice_in_dim(g, dev * (small_shapes[n][ax] // N_DEV), small_shapes[n][ax] // N_DEV, axis=ax)
        g_small[n] = g
    loc_shapes = [_local_shape(small_shapes[n], small_axis[n]) for n in names]
    d_pack, m_pack, v_pack = _adamw_small(_pack([g_small[n] for n in names]), _pack([weights[n] for n in names]),
                                          _pack([mom1[n] for n in names]), _pack([mom2[n] for n in names]), "adamw_small")
    d_small = dict(zip(names, _unpack(d_pack, loc_shapes)))
    m_small = dict(zip(names, _unpack(m_pack, loc_shapes)))
    v_small = dict(zip(names, _unpack(v_pack, loc_shapes)))

    order = ["meta", "pre_g", "post_g", "w_in", "w_out", "conv_a_w", "ssm_conv_w", "ssm_conv_b", "dt_bias", "a_log",
             "d_skip", "ssm_norm_g", "conf_conv_w", "conf_conv_b", "conf_ln_g", "conf_ln_b"]

    def pick(k, small):
        return [big[n][k] if n in big else small[n] for n in order]

    return (loss, grad_x, *pick(0, g_small), *pick(1, d_small), *pick(2, m_small), *pick(3, v_small))
```

```python
import functools

import jax
import jax.numpy as jnp
from jax import lax
from jax.experimental import pallas as pl
from jax.experimental.pallas import tpu as pltpu

F32 = jnp.float32
BF16 = jnp.bfloat16

D_MODEL = 1024
DEPTH = 4
SEQ = 2048
CHUNK = 64
N_META = 16
PAD = 48
LP = PAD + N_META + SEQ
D_A = 512
D_B = 1024
D_C = 512
N_HEADS = 16
HEAD_DIM = 64
N_STATE = 128
N_GROUPS = 2
GROUP_W = D_B // N_GROUPS
N_XBC = D_B + 2 * N_GROUPS * N_STATE
CONV_A_K = 3
SSM_K = 4
CONF_K = 31
NORM_EPS = 1e-6
LN_EPS = 1e-5
N_IN = 6160
N_INP = 6272
COL_BZ = 2048
COL_XBC = 3072
COL_C = 4608
COL_DT = 6144
LANE = 128
SUB = 8

ADAM_LR = 0.001
ADAM_B1 = 0.9
ADAM_B2 = 0.999
ADAM_EPS = 1e-08
ADAM_WD = 0.01
ADAM_STEP = 10

TILE = 192
HALO_A = 8
HALO_C = 32
MM_TM = 384
MM_TN = 896
VMEM_LIMIT = 56 * 1024 * 1024

MESH = pl.DeviceIdType.MESH
N_DEV = 8


def _silu(x):
    return x * jax.nn.sigmoid(x)


def _dsilu(x):
    s = jax.nn.sigmoid(x)
    return s * (1.0 + x * (1.0 - s))


def _cparams(sem=None):
    return pltpu.CompilerParams(dimension_semantics=sem, vmem_limit_bytes=VMEM_LIMIT)


def _mm(a, b, mode, out_dtype, tm, tn, tk, name, cols_outer=False):
    def ix(f):
        if cols_outer:
            return lambda j, i, q: f(i, j, q)
        return f

    if mode == "nn":
        (m, k), (_, n) = a.shape, b.shape
        a_spec = pl.BlockSpec((tm, tk), ix(lambda i, j, q: (i, q)))
        b_spec = pl.BlockSpec((tk, tn), ix(lambda i, j, q: (q, j)))
        dims = (((1,), (0,)), ((), ()))
    elif mode == "nt":
        (m, k), (n, _) = a.shape, b.shape
        a_spec = pl.BlockSpec((tm, tk), ix(lambda i, j, q: (i, q)))
        b_spec = pl.BlockSpec((tn, tk), ix(lambda i, j, q: (j, q)))
        dims = (((1,), (1,)), ((), ()))
    else:
        (k, m), (_, n) = a.shape, b.shape
        a_spec = pl.BlockSpec((tk, tm), ix(lambda i, j, q: (q, i)))
        b_spec = pl.BlockSpec((tk, tn), ix(lambda i, j, q: (q, j)))
        dims = (((0,), (0,)), ((), ()))
    assert m % tm == 0 and n % tn == 0 and k % tk == 0, (name, a.shape, b.shape)
    nk = k // tk
    grid = (n // tn, m // tm, nk) if cols_outer else (m // tm, n // tn, nk)

    def body(a_ref, b_ref, o_ref, acc_ref):
        part = lax.dot_general(a_ref[...].astype(BF16), b_ref[...].astype(BF16), dims, preferred_element_type=F32)
        if nk == 1:
            o_ref[...] = part.astype(o_ref.dtype)
        else:
            q = pl.program_id(2)

            @pl.when(q == 0)
            def _():
                acc_ref[...] = part

            @pl.when(q > 0)
            def _():
                acc_ref[...] += part

            @pl.when(q == nk - 1)
            def _():
                o_ref[...] = acc_ref[...].astype(o_ref.dtype)

    return pl.pallas_call(
        body, name=name, grid=grid,
        in_specs=[a_spec, b_spec], out_specs=pl.BlockSpec((tm, tn), ix(lambda i, j, q: (i, j))),
        out_shape=jax.ShapeDtypeStruct((m, n), out_dtype),
        scratch_shapes=[pltpu.VMEM((tm, tn) if nk > 1 else (SUB, LANE), F32)],
        compiler_params=_cparams(("parallel", "parallel", "arbitrary")),
    )(a, b)


def _row_mask(i, tpe, rows):
    r = lax.broadcasted_iota(jnp.int32, (rows, 1), 0)
    return jnp.logical_or((i % tpe) != 0, r >= PAD)


def _rms_fwd(h, g, name):
    t, d = h.shape

    def body(h_ref, g_ref, o_ref):
        x = h_ref[...]
        r = lax.rsqrt(jnp.mean(x * x, axis=-1, keepdims=True) + NORM_EPS)
        o_ref[...] = (x * r * g_ref[...]).astype(BF16)

    return pl.pallas_call(
        body, name=name, grid=(t // TILE,),
        in_specs=[pl.BlockSpec((TILE, d), lambda i: (i, 0)), pl.BlockSpec((1, d), lambda i: (0, 0))],
        out_specs=pl.BlockSpec((TILE, d), lambda i: (i, 0)),
        out_shape=jax.ShapeDtypeStruct((t, d), BF16),
        compiler_params=_cparams(("parallel",)),
    )(h, g)


def _post_fwd(h, m, g, name):
    t, d = h.shape

    def body(h_ref, m_ref, g_ref, o_ref):
        x = m_ref[...]
        r = lax.rsqrt(jnp.mean(x * x, axis=-1, keepdims=True) + NORM_EPS)
        o_ref[...] = h_ref[...] + x * r * g_ref[...]

    row = pl.BlockSpec((TILE, d), lambda i: (i, 0))
    return pl.pallas_call(
        body, name=name, grid=(t // TILE,),
        in_specs=[row, row, pl.BlockSpec((1, d), lambda i: (0, 0))], out_specs=row,
        out_shape=jax.ShapeDtypeStruct((t, d), F32),
        compiler_params=_cparams(("parallel",)),
    )(h, m, g)


def _rms_bwd_math(x, g, dy):
    r = lax.rsqrt(jnp.mean(x * x, axis=-1, keepdims=True) + NORM_EPS)
    gdy = dy * g
    dx = r * gdy - x * (r * r * r) * jnp.mean(gdy * x, axis=-1, keepdims=True)
    return dx, dy * x * r


def _post_bwd(dh, m, g, name):
    t, d = dh.shape

    def body(dh_ref, m_ref, g_ref, dm_ref, dg_ref):
        dm, dgt = _rms_bwd_math(m_ref[...], g_ref[...], dh_ref[...])
        dm_ref[...] = dm

        @pl.when(pl.program_id(0) == 0)
        def _():
            dg_ref[...] = jnp.zeros_like(dg_ref)

        dg_ref[...] += jnp.sum(dgt, axis=0, keepdims=True)

    row = pl.BlockSpec((TILE, d), lambda i: (i, 0))
    vec = pl.BlockSpec((1, d), lambda i: (0, 0))
    return pl.pallas_call(
        body, name=name, grid=(t // TILE,),
        in_specs=[row, row, vec], out_specs=[row, vec],
        out_shape=[jax.ShapeDtypeStruct((t, d), F32), jax.ShapeDtypeStruct((1, d), F32)],
        compiler_params=_cparams(("arbitrary",)),
    )(dh, m, g)


def _rms_bwd(dh_res, dhn, h, g, lp, name):
    t, d = h.shape
    tpe = lp // TILE

    def body(dr_ref, dn_ref, h_ref, g_ref, dh_ref, dg_ref, dmeta_ref):
        i = pl.program_id(0)
        dx, dgt = _rms_bwd_math(h_ref[...], g_ref[...], dn_ref[...])
        dh = dr_ref[...] + dx
        dh_ref[...] = dh

        @pl.when(i == 0)
        def _():
            dg_ref[...] = jnp.zeros_like(dg_ref)
            dmeta_ref[...] = jnp.zeros_like(dmeta_ref)

        dg_ref[...] += jnp.sum(dgt, axis=0, keepdims=True)

        @pl.when((i % tpe) == 0)
        def _():
            dmeta_ref[...] += dh[PAD:PAD + N_META, :]

    row = pl.BlockSpec((TILE, d), lambda i: (i, 0))
    vec = pl.BlockSpec((1, d), lambda i: (0, 0))
    return pl.pallas_call(
        body, name=name, grid=(t // TILE,),
        in_specs=[row, row, row, vec],
        out_specs=[row, vec, pl.BlockSpec((N_META, d), lambda i: (0, 0))],
        out_shape=[jax.ShapeDtypeStruct((t, d), F32), jax.ShapeDtypeStruct((1, d), F32),
                   jax.ShapeDtypeStruct((N_META, d), F32)],
        compiler_params=_cparams(("arbitrary",)),
    )(dh_res, dhn, h, g)


def _loss_kernel(h, target, lp, name):
    t, d = h.shape
    cpe = lp // CHUNK
    nb = t // lp

    def body(h_ref, t_ref, dh_ref, loss_ref):
        b, j = pl.program_id(0), pl.program_id(1)

        @pl.when(jnp.logical_and(b == 0, j == 0))
        def _():
            loss_ref[...] = jnp.zeros_like(loss_ref)

        @pl.when(j == 0)
        def _():
            dh_ref[...] = jnp.zeros_like(dh_ref)

        @pl.when(j > 0)
        def _():
            err = h_ref[...] - t_ref[...]
            dh_ref[...] = err * (1.0 / d)
            loss_ref[...] += (0.5 / d) * jnp.sum(err * err)

    return pl.pallas_call(
        body, name=name, grid=(nb, cpe),
        in_specs=[pl.BlockSpec((CHUNK, d), lambda b, j: (b * cpe + j, 0)),
                  pl.BlockSpec((CHUNK, d), lambda b, j: (b * (cpe - 1) + jnp.maximum(j - 1, 0), 0))],
        out_specs=[pl.BlockSpec((CHUNK, d), lambda b, j: (b * cpe + j, 0)),
                   pl.BlockSpec((SUB, LANE), lambda b, j: (0, 0))],
        out_shape=[jax.ShapeDtypeStruct((t, d), F32), jax.ShapeDtypeStruct((SUB, LANE), F32)],
        compiler_params=_cparams(("arbitrary", "arbitrary")),
    )(h, target)


def _halo_specs(t, lp, width, col, halo):
    per = TILE // halo
    last = t // halo - 1
    cur = pl.BlockSpec((TILE, width), lambda i: (i, col))
    prev = pl.BlockSpec((halo, width), lambda i: (jnp.maximum(i * per - 1, 0), col))
    nxt = pl.BlockSpec((halo, width), lambda i: (jnp.minimum((i + 1) * per, last), col))
    return cur, prev, nxt


def _conv_a(ve, w):
    rows = ve.shape[0] - HALO_A
    return (w[0:1] * ve[HALO_A - 2:HALO_A - 2 + rows] + w[1:2] * ve[HALO_A - 1:HALO_A - 1 + rows]
            + w[2:3] * ve[HALO_A:HALO_A + rows])


def _a_fwd(proj, w, lp, name):
    t = proj.shape[0]
    tpe = lp // TILE
    cur, prev, _ = _halo_specs(t, lp, 4 * D_A, 0, HALO_A)

    def body(p_ref, ph_ref, w_ref, y_ref):
        first = (pl.program_id(0) % tpe) == 0
        v = p_ref[:, D_A:2 * D_A] * p_ref[:, 2 * D_A:3 * D_A]
        vh = jnp.where(first, 0.0, ph_ref[:, D_A:2 * D_A] * ph_ref[:, 2 * D_A:3 * D_A])
        cv = _conv_a(jnp.concatenate([vh, v], axis=0), w_ref[...])
        y_ref[...] = (p_ref[:, 0:D_A] * cv * _silu(p_ref[:, 3 * D_A:4 * D_A])).astype(BF16)

    return pl.pallas_call(
        body, name=name, grid=(t // TILE,),
        in_specs=[cur, prev, pl.BlockSpec((CONV_A_K, D_A), lambda i: (0, 0))],
        out_specs=pl.BlockSpec((TILE, D_A), lambda i: (i, 0)),
        out_shape=jax.ShapeDtypeStruct((t, D_A), BF16),
        compiler_params=_cparams(("parallel",)),
    )(proj, proj, w)


def _a_bwd(dy, proj, w, lp, dy_col, name):
    t = proj.shape[0]
    tpe = lp // TILE
    cur, prev, nxt = _halo_specs(t, lp, 4 * D_A, 0, HALO_A)
    dcur, _, dnxt = _halo_specs(t, lp, D_A, dy_col, HALO_A)

    def body(dy_ref, dyn_ref, p_ref, ph_ref, pn_ref, w_ref, dp_ref, dw_ref):
        i = pl.program_id(0)
        first = (i % tpe) == 0
        last = (i % tpe) == tpe - 1
        w = w_ref[...]
        ab, ac, ax, az = (p_ref[:, k * D_A:(k + 1) * D_A] for k in range(4))
        v = ac * ax
        vh = jnp.where(first, 0.0, ph_ref[:, D_A:2 * D_A] * ph_ref[:, 2 * D_A:3 * D_A])
        ve = jnp.concatenate([vh, v], axis=0)
        cv = _conv_a(ve, w)
        s = _silu(az)
        dy_ = dy_ref[...]
        dcv = dy_ * ab * s
        dcvn = jnp.where(last, 0.0, dyn_ref[...] * pn_ref[:, 0:D_A] * _silu(pn_ref[:, 3 * D_A:4 * D_A]))
        dce = jnp.concatenate([dcv, dcvn], axis=0)
        dv = w[2:3] * dce[0:TILE] + w[1:2] * dce[1:TILE + 1] + w[0:1] * dce[2:TILE + 2]
        dp = jnp.concatenate([dy_ * cv * s, dv * ax, dv * ac, dy_ * ab * cv * _dsilu(az)], axis=1)
        dp_ref[...] = jnp.where(_row_mask(i, tpe, TILE), dp, 0.0).astype(BF16)
        dw = jnp.concatenate(
            [jnp.sum(dcv * ve[HALO_A - 2 + k:HALO_A - 2 + k + TILE], axis=0, keepdims=True) for k in range(CONV_A_K)], axis=0)

        @pl.when(i == 0)
        def _():
            dw_ref[...] = jnp.zeros_like(dw_ref)

        dw_ref[...] += dw

    wspec = pl.BlockSpec((CONV_A_K, D_A), lambda i: (0, 0))
    return pl.pallas_call(
        body, name=name, grid=(t // TILE,),
        in_specs=[dcur, dnxt, cur, prev, nxt, wspec],
        out_specs=[pl.BlockSpec((TILE, 4 * D_A), lambda i: (i, 0)), wspec],
        out_shape=[jax.ShapeDtypeStruct((t, 4 * D_A), BF16), jax.ShapeDtypeStruct((CONV_A_K, D_A), F32)],
        compiler_params=_cparams(("arbitrary",)),
    )(dy, dy, proj, proj, proj, w)


def _conv_ssm(xe, w, rows, off):
    acc = w[0:1] * xe[off - 3:off - 3 + rows]
    for k in range(1, SSM_K):
        acc = acc + w[k:k + 1] * xe[off - 3 + k:off - 3 + k + rows]
    return acc


def _xbc_fwd(proj, w, b, lp, name):
    t = proj.shape[0]
    tpe = lp // TILE
    cur, prev, _ = _halo_specs(t, lp, N_XBC, COL_XBC // N_XBC, HALO_A)

    def body(x_ref, xh_ref, w_ref, b_ref, o_ref):
        first = (pl.program_id(0) % tpe) == 0
        xh = jnp.where(first, 0.0, xh_ref[...])
        xe = jnp.concatenate([xh, x_ref[...]], axis=0)
        o_ref[...] = _silu(_conv_ssm(xe, w_ref[...], TILE, HALO_A) + b_ref[...])

    return pl.pallas_call(
        body, name=name, grid=(t // TILE,),
        in_specs=[cur, prev, pl.BlockSpec((SSM_K, N_XBC), lambda i: (0, 0)), pl.BlockSpec((1, N_XBC), lambda i: (0, 0))],
        out_specs=pl.BlockSpec((TILE, N_XBC), lambda i: (i, 0)),
        out_shape=jax.ShapeDtypeStruct((t, N_XBC), F32),
        compiler_params=_cparams(("parallel",)),
    )(proj, proj, w, b)


def _xbc_bwd(dxbc, proj, w, b, lp, name):
    t = proj.shape[0]
    tpe = lp // TILE
    cur, prev, nxt = _halo_specs(t, lp, N_XBC, COL_XBC // N_XBC, HALO_A)
    dcur, _, dnxt = _halo_specs(t, lp, N_XBC, 0, HALO_A)

    def body(d_ref, dn_ref, x_ref, xh_ref, xn_ref, w_ref, b_ref, dx_ref, dw_ref, db_ref):
        i = pl.program_id(0)
        first = (i % tpe) == 0
        last = (i % tpe) == tpe - 1
        w = w_ref[...]
        xh = jnp.where(first, 0.0, xh_ref[...])
        xe = jnp.concatenate([xh, x_ref[...], xn_ref[...]], axis=0)
        pre = _conv_ssm(xe, w, TILE + HALO_A, HALO_A) + b_ref[...]
        de = jnp.concatenate([d_ref[...], jnp.where(last, 0.0, dn_ref[...])], axis=0)
        dpre = de * _dsilu(pre)
        dx = w[3:4] * dpre[0:TILE]
        for k in range(SSM_K - 1):
            dx = dx + w[k:k + 1] * dpre[3 - k:3 - k + TILE]
        dx_ref[...] = jnp.where(_row_mask(i, tpe, TILE), dx, 0.0).astype(BF16)
        dpc = dpre[0:TILE]
        dw = jnp.concatenate(
            [jnp.sum(dpc * xe[HALO_A - 3 + k:HALO_A - 3 + k + TILE], axis=0, keepdims=True) for k in range(SSM_K)], axis=0)

        @pl.when(i == 0)
        def _():
            dw_ref[...] = jnp.zeros_like(dw_ref)
            db_ref[...] = jnp.zeros_like(db_ref)

        dw_ref[...] += dw
        db_ref[...] += jnp.sum(dpc, axis=0, keepdims=True)

    wspec = pl.BlockSpec((SSM_K, N_XBC), lambda i: (0, 0))
    bspec = pl.BlockSpec((1, N_XBC), lambda i: (0, 0))
    return pl.pallas_call(
        body, name=name, grid=(t // TILE,),
        in_specs=[dcur, dnxt, cur, prev, nxt, wspec, bspec],
        out_specs=[pl.BlockSpec((TILE, N_XBC), lambda i: (i, 0)), wspec, bspec],
        out_shape=[jax.ShapeDtypeStruct((t, N_XBC), BF16), jax.ShapeDtypeStruct((SSM_K, N_XBC), F32),
                   jax.ShapeDtypeStruct((1, N_XBC), F32)],
        compiler_params=_cparams(("arbitrary",)),
    )(dxbc, dxbc, proj, proj, proj, w, b)


SUBROWS = 32


def _conv_conf(scr, w, rows, off, base):
    acc = w[0:1] * scr[pl.ds(base + off - (CONF_K - 1), rows), :]
    for k in range(1, CONF_K):
        acc = acc + w[k:k + 1] * scr[pl.ds(base + off - (CONF_K - 1) + k, rows), :]
    return acc


def _ln_fwd(u1, g, b):
    mu = jnp.mean(u1, axis=-1, keepdims=True)
    xc = u1 - mu
    rstd = lax.rsqrt(jnp.mean(xc * xc, axis=-1, keepdims=True) + LN_EPS)
    n = xc * rstd
    return n, rstd, n * g + b


def _c_fwd(proj, w, cb, g, b, lp, name):
    t = proj.shape[0]
    tpe = lp // TILE
    cur, prev, _ = _halo_specs(t, lp, 3 * D_C, COL_C // (3 * D_C), HALO_C)

    def body(p_ref, ph_ref, w_ref, cb_ref, g_ref, b_ref, y_ref, u0_scr):
        first = (pl.program_id(0) % tpe) == 0
        u0_scr[0:HALO_C, :] = jnp.where(first, 0.0, ph_ref[:, 0:D_C] * jax.nn.sigmoid(ph_ref[:, D_C:2 * D_C]))
        u0_scr[HALO_C:HALO_C + TILE, :] = p_ref[:, 0:D_C] * jax.nn.sigmoid(p_ref[:, D_C:2 * D_C])
        w = w_ref[...]
        for r0 in range(0, TILE, SUBROWS):
            u1 = _conv_conf(u0_scr, w, SUBROWS, HALO_C, r0) + cb_ref[...]
            _, _, u2 = _ln_fwd(u1, g_ref[...], b_ref[...])
            y_ref[r0:r0 + SUBROWS, :] = (_silu(u2) * _silu(p_ref[r0:r0 + SUBROWS, 2 * D_C:3 * D_C])).astype(BF16)

    vec = pl.BlockSpec((1, D_C), lambda i: (0, 0))
    return pl.pallas_call(
        body, name=name, grid=(t // TILE,),
        in_specs=[cur, prev, pl.BlockSpec((CONF_K, D_C), lambda i: (0, 0)), vec, vec, vec],
        out_specs=pl.BlockSpec((TILE, D_C), lambda i: (i, 0)),
        out_shape=jax.ShapeDtypeStruct((t, D_C), BF16),
        scratch_shapes=[pltpu.VMEM((HALO_C + TILE, D_C), F32)],
        compiler_params=_cparams(("parallel",)),
    )(proj, proj, w, cb, g, b)


def _c_bwd(dy, proj, w, cb, g, b, lp, dy_col, name):
    t = proj.shape[0]
    tpe = lp // TILE
    cur, prev, nxt = _halo_specs(t, lp, 3 * D_C, COL_C // (3 * D_C), HALO_C)
    dcur, _, dnxt = _halo_specs(t, lp, D_C, dy_col, HALO_C)
    ext = TILE + HALO_C

    def body(dy_ref, dyn_ref, p_ref, ph_ref, pn_ref, w_ref, cb_ref, g_ref, b_ref,
             dp_ref, dw_ref, dcb_ref, dg_ref, db_ref, u0_scr, du1_scr, wacc_scr):
        i = pl.program_id(0)
        first = (i % tpe) == 0
        last = (i % tpe) == tpe - 1
        w = w_ref[...]

        @pl.when(i == 0)
        def _():
            dw_ref[...] = jnp.zeros_like(dw_ref)
            dcb_ref[...] = jnp.zeros_like(dcb_ref)
            dg_ref[...] = jnp.zeros_like(dg_ref)
            db_ref[...] = jnp.zeros_like(db_ref)

        u0_scr[0:HALO_C, :] = jnp.where(first, 0.0, ph_ref[:, 0:D_C] * jax.nn.sigmoid(ph_ref[:, D_C:2 * D_C]))
        u0_scr[HALO_C:ext, :] = p_ref[:, 0:D_C] * jax.nn.sigmoid(p_ref[:, D_C:2 * D_C])
        u0_scr[ext:ext + HALO_C, :] = pn_ref[:, 0:D_C] * jax.nn.sigmoid(pn_ref[:, D_C:2 * D_C])
        du1_scr[ext:ext + SUB, :] = jnp.zeros((SUB, D_C), F32)
        dcb = jnp.zeros((1, D_C), F32)
        dg = jnp.zeros((1, D_C), F32)
        db = jnp.zeros((1, D_C), F32)
        for r0 in range(0, ext, SUBROWS):
            in_tile = r0 < TILE
            src, dsrc, q0 = (p_ref, dy_ref, r0) if in_tile else (pn_ref, dyn_ref, r0 - TILE)
            u1 = _conv_conf(u0_scr, w, SUBROWS, HALO_C, r0) + cb_ref[...]
            n, rstd, u2 = _ln_fwd(u1, g_ref[...], b_ref[...])
            cz = src[q0:q0 + SUBROWS, 2 * D_C:3 * D_C]
            dyc = dsrc[q0:q0 + SUBROWS, :]
            if not in_tile:
                dyc = jnp.where(last, 0.0, dyc)
            du2 = dyc * _silu(cz) * _dsilu(u2)
            dn = du2 * g_ref[...]
            du1 = rstd * (dn - jnp.mean(dn, axis=-1, keepdims=True) - n * jnp.mean(dn * n, axis=-1, keepdims=True))
            du1_scr[r0:r0 + SUBROWS, :] = du1
            if in_tile:
                dp_ref[r0:r0 + SUBROWS, 2 * D_C:3 * D_C] = (dyc * _silu(u2) * _dsilu(cz)).astype(BF16)
                dg = dg + jnp.sum(du2 * n, axis=0, keepdims=True)
                db = db + jnp.sum(du2, axis=0, keepdims=True)
                dcb = dcb + jnp.sum(du1, axis=0, keepdims=True)
        dcb_ref[...] += dcb
        dg_ref[...] += dg
        db_ref[...] += db
        mask = _row_mask(i, tpe, TILE)
        for r0 in range(0, TILE, SUBROWS):
            acc = w[0:1] * du1_scr[pl.ds(r0 + CONF_K - 1, SUBROWS), :]
            for k in range(1, CONF_K):
                acc = acc + w[k:k + 1] * du1_scr[pl.ds(r0 + CONF_K - 1 - k, SUBROWS), :]
            ca = p_ref[r0:r0 + SUBROWS, 0:D_C]
            sg = jax.nn.sigmoid(p_ref[r0:r0 + SUBROWS, D_C:2 * D_C])
            m = mask[r0:r0 + SUBROWS]
            dp_ref[r0:r0 + SUBROWS, 0:D_C] = jnp.where(m, acc * sg, 0.0).astype(BF16)
            dp_ref[r0:r0 + SUBROWS, D_C:2 * D_C] = jnp.where(m, acc * ca * sg * (1.0 - sg), 0.0).astype(BF16)
        for k in range(CONF_K):
            part = jnp.zeros((SUB, D_C), F32)
            for r0 in range(0, TILE, SUBROWS):
                prod = du1_scr[r0:r0 + SUBROWS, :] * u0_scr[pl.ds(HALO_C + r0 - (CONF_K - 1) + k, SUBROWS), :]
                for q in range(0, SUBROWS, SUB):
                    part = part + prod[q:q + SUB]
            wacc_scr[k:k + 1, :] = jnp.sum(part, axis=0, keepdims=True)
        dw_ref[...] += wacc_scr[0:CONF_K, :]

    vec = pl.BlockSpec((1, D_C), lambda i: (0, 0))
    wspec = pl.BlockSpec((CONF_K, D_C), lambda i: (0, 0))
    return pl.pallas_call(
        body, name=name, grid=(t // TILE,),
        in_specs=[dcur, dnxt, cur, prev, nxt, wspec, vec, vec, vec],
        out_specs=[pl.BlockSpec((TILE, 3 * D_C), lambda i: (i, 0)), wspec, vec, vec, vec],
        out_shape=[jax.ShapeDtypeStruct((t, 3 * D_C), BF16), jax.ShapeDtypeStruct((CONF_K, D_C), F32),
                   jax.ShapeDtypeStruct((1, D_C), F32), jax.ShapeDtypeStruct((1, D_C), F32),
                   jax.ShapeDtypeStruct((1, D_C), F32)],
        scratch_shapes=[pltpu.VMEM((ext + HALO_C, D_C), F32), pltpu.VMEM((ext + SUB, D_C), F32),
                        pltpu.VMEM((HALO_C, D_C), F32)],
        compiler_params=_cparams(("arbitrary",)),
    )(dy, dy, proj, proj, proj, w, cb, g, b)


def _split_dot(x, m_bf16, terms):
    acc = None
    rem = x
    for _ in range(terms):
        hi = rem.astype(BF16)
        part = jnp.dot(hi, m_bf16, preferred_element_type=F32)
        acc = part if acc is None else acc + part
        rem = rem - hi.astype(F32)
    return acc


def _split_dot_left(m_bf16, x, terms):
    acc = None
    rem = x
    for _ in range(terms):
        hi = rem.astype(BF16)
        part = jnp.dot(m_bf16, hi, preferred_element_type=F32)
        acc = part if acc is None else acc + part
        rem = rem - hi.astype(F32)
    return acc


def _tri(rows_ge_cols):
    r = lax.broadcasted_iota(jnp.int32, (CHUNK, CHUNK), 0)
    c = lax.broadcasted_iota(jnp.int32, (CHUNK, CHUNK), 1)
    return (r >= c) if rows_ge_cols else (r <= c)


def _softplus(x):
    return jnp.maximum(x, 0.0) + jnp.log(1.0 + jnp.exp(-jnp.abs(x)))


def _ssd_common(dtraw, dtb, dtb_t, a_log, a_log_t, e_mat, valid_col, valid_row):
    a = -jnp.exp(a_log)
    lane = lax.broadcasted_iota(jnp.int32, (1, LANE), 1)
    a = jnp.where(lane < N_HEADS, a, 0.0)
    a_t = -jnp.exp(a_log_t)
    dt = jnp.where(valid_col, _softplus(dtraw + dtb), 0.0)
    dt = jnp.where(lane < N_HEADS, dt, 0.0)
    dt_t = jnp.where(valid_row, _softplus(dtraw.T[0:N_HEADS, :] + dtb_t), 0.0)
    ltri = _tri(True).astype(BF16)
    utri = _tri(False).astype(BF16)
    big_a = _split_dot_left(ltri, dt * a, 3)
    big_a_t = _split_dot(dt_t * a_t, utri, 3)
    e_a = jnp.exp(big_a)
    d_s = jnp.exp(big_a[CHUNK - 1:CHUNK, :] - big_a)
    dt_x = _split_dot(dt, e_mat, 2)
    e_a_x = _split_dot(e_a, e_mat, 2)
    d_s_x = _split_dot(d_s, e_mat, 2)
    cd_x = e_a_x[CHUNK - 1:CHUNK, :]
    return a, dt, big_a, big_a_t, e_a, d_s, dt_x, e_a_x, d_s_x, cd_x


def _decay(big_a, big_a_t, h, transposed):
    col = big_a[:, h:h + 1]
    row = big_a_t[h:h + 1, :]
    if not transposed:
        seg = col - row
        return jnp.where(_tri(True), jnp.exp(jnp.minimum(seg, 0.0)), 0.0)
    seg = row - col
    return jnp.where(_tri(False), jnp.exp(jnp.minimum(seg, 0.0)), 0.0)


NT_DIMS = (((1,), (1,)), ((), ()))
TN_DIMS = (((0,), (0,)), ((), ()))
HBM_SPEC = pl.BlockSpec(memory_space=pl.ANY)


class _Comm:
    def __init__(self, inputs, out_shapes, sem_shapes, copies, aliases=None):
        self.inputs, self.out_shapes, self.sem_shapes, self.copies = inputs, out_shapes, sem_shapes, copies
        self.aliases = aliases or {}

    def start(self, ins, outs, sems):
        local, sends, _ = self.copies(ins, outs, sems, False)
        for cp in local + sends:
            cp.start()

    def wait(self, ins, outs, sems):
        local, sends, recvs = self.copies(ins, outs, sems, True)
        for cp in recvs:
            cp.wait_recv()
        for cp in sends:
            cp.wait_send()
        for cp in local:
            cp.wait()


def _grid_call(body, name, grid, in_specs, out_specs, out_shape, scratch_shapes, operands, comm=None):
    if comm is None:
        return pl.pallas_call(
            body, name=name, grid=grid, in_specs=in_specs, out_specs=out_specs, out_shape=out_shape,
            scratch_shapes=scratch_shapes, compiler_params=_cparams(("arbitrary",) * len(grid)))(*operands)
    n_in, n_out, n_scr = len(in_specs), len(out_specs), len(scratch_shapes)
    nci, nco = len(comm.inputs), len(comm.out_shapes)

    def wrapped(*refs):
        ins, cins = refs[:n_in], refs[n_in:n_in + nci]
        o0 = n_in + nci
        outs, couts = refs[o0:o0 + n_out], refs[o0 + n_out:o0 + n_out + nco]
        s0 = o0 + n_out + nco
        scr, csems = refs[s0:s0 + n_scr], refs[s0 + n_scr:]
        first = jnp.logical_and(pl.program_id(0) == 0, pl.program_id(1) == 0)
        last = jnp.logical_and(pl.program_id(0) == grid[0] - 1, pl.program_id(1) == grid[1] - 1)

        @pl.when(first)
        def _():
            comm.start(cins, couts, csems)

        body(*ins, *outs, *scr)

        @pl.when(last)
        def _():
            comm.wait(cins, couts, csems)

    res = pl.pallas_call(
        wrapped, name=name, grid=grid,
        in_specs=list(in_specs) + [HBM_SPEC] * nci, out_specs=list(out_specs) + [HBM_SPEC] * nco,
        out_shape=list(out_shape) + list(comm.out_shapes),
        scratch_shapes=list(scratch_shapes) + list(comm.sem_shapes),
        input_output_aliases={n_in + k: n_out + v for k, v in comm.aliases.items()},
        compiler_params=_cparams(("arbitrary",) * len(grid)))(*operands, *comm.inputs)
    return res


def _ssd_fwd(xbc, proj, dtb, a_log, d_skip_x, norm_g, e_mat, lp, name, comm=None):
    t = xbc.shape[0]
    cpe = lp // CHUNK
    nb = t // lp
    dtb_p = jnp.pad(dtb.reshape(1, N_HEADS), ((0, 0), (0, LANE - N_HEADS)))
    alog_p = jnp.pad(a_log.reshape(1, N_HEADS), ((0, 0), (0, LANE - N_HEADS)))
    dtb_t = dtb.reshape(N_HEADS, 1)
    alog_t = a_log.reshape(N_HEADS, 1)

    def body(xbc_ref, bz_ref, dt_ref, dtb_ref, dtbt_ref, al_ref, alt_ref, dx_ref, g_ref, e_ref,
             yb_ref, ys_ref, st_ref, s_scr):
        c = pl.program_id(1)

        @pl.when(c == 0)
        def _():
            s_scr[...] = jnp.zeros_like(s_scr)

        rows = lax.broadcasted_iota(jnp.int32, (CHUNK, 1), 0)
        cols = lax.broadcasted_iota(jnp.int32, (1, CHUNK), 1)
        valid_col = jnp.logical_or(c > 0, rows >= PAD)
        valid_row = jnp.logical_or(c > 0, cols >= PAD)
        e_mat = e_ref[...]
        _, _, big_a, big_a_t, _, _, dt_x, e_a_x, d_s_x, cd_x = _ssd_common(
            dt_ref[...], dtb_ref[...], dtbt_ref[...], al_ref[...], alt_ref[...], e_mat, valid_col, valid_row)
        xs = xbc_ref[:, 0:D_B]
        bs = xbc_ref[:, D_B:D_B + N_GROUPS * N_STATE].astype(BF16)
        cs = xbc_ref[:, D_B + N_GROUPS * N_STATE:N_XBC].astype(BF16)
        xdt = xs * dt_x
        xdt_b = xdt.astype(BF16)
        st_prev = s_scr[...]
        st_ref[...] = st_prev.astype(BF16)
        st_b = st_prev.astype(BF16)
        u_b = (xdt * d_s_x).astype(BF16)
        y_parts = []
        for g in range(N_GROUPS):
            gs = slice(g * N_STATE, (g + 1) * N_STATE)
            gw = slice(g * GROUP_W, (g + 1) * GROUP_W)
            cb = lax.dot_general(cs[:, gs], bs[:, gs], NT_DIMS, preferred_element_type=F32)
            y_off = jnp.dot(cs[:, gs], st_b[:, gw], preferred_element_type=F32)
            diag = []
            for e in range(N_HEADS // N_GROUPS):
                h = g * (N_HEADS // N_GROUPS) + e
                m = (cb * _decay(big_a, big_a_t, h, False)).astype(BF16)
                diag.append(jnp.dot(m, xdt_b[:, h * HEAD_DIM:(h + 1) * HEAD_DIM], preferred_element_type=F32))
            y_parts.append(jnp.concatenate(diag, axis=1) + y_off * e_a_x[:, gw])
            new_st = lax.dot_general(bs[:, gs], u_b[:, gw], TN_DIMS, preferred_element_type=F32)
            s_scr[:, gw] = st_prev[:, gw] * cd_x[:, gw] + new_st
        y = jnp.concatenate(y_parts, axis=1) + xs * dx_ref[...]
        ys_ref[...] = y
        z = y * _silu(bz_ref[...])
        r = lax.rsqrt(jnp.mean(z * z, axis=-1, keepdims=True) + NORM_EPS)
        yb_ref[...] = (z * r * g_ref[...]).astype(BF16)

    def row(width, col):
        return pl.BlockSpec((CHUNK, width), lambda b, c: (b * cpe + c, col))

    def const(shape):
        return pl.BlockSpec(shape, lambda b, c: (0,) * len(shape))

    return _grid_call(
        body, name, (nb, cpe),
        [row(N_XBC, 0), row(D_B, COL_BZ // D_B), row(LANE, COL_DT // LANE),
         const((1, LANE)), const((N_HEADS, 1)), const((1, LANE)), const((N_HEADS, 1)),
         const((1, D_B)), const((1, D_B)), const((LANE, D_B))],
        [row(D_B, 0), row(D_B, 0), pl.BlockSpec((None, N_STATE, D_B), lambda b, c: (b * cpe + c, 0, 0))],
        [jax.ShapeDtypeStruct((t, D_B), BF16), jax.ShapeDtypeStruct((t, D_B), F32),
         jax.ShapeDtypeStruct((nb * cpe, N_STATE, D_B), BF16)],
        [pltpu.VMEM((N_STATE, D_B), F32)],
        (xbc, proj, proj, dtb_p, dtb_t, alog_p, alog_t, d_skip_x, norm_g, e_mat), comm)


def _ssd_bwd(dy, y_ssd, xbc, proj, states, dtb, a_log, d_skip_x, norm_g, e_mat, e_mat_t, lp, name, comm=None):
    t = xbc.shape[0]
    cpe = lp // CHUNK
    nb = t // lp
    hpg = N_HEADS // N_GROUPS
    dtb_p = jnp.pad(dtb.reshape(1, N_HEADS), ((0, 0), (0, LANE - N_HEADS)))
    alog_p = jnp.pad(a_log.reshape(1, N_HEADS), ((0, 0), (0, LANE - N_HEADS)))
    dtb_t = dtb.reshape(N_HEADS, 1)
    alog_t = a_log.reshape(N_HEADS, 1)

    def body(dy_ref, ys_ref, xbc_ref, bz_ref, dt_ref, st_ref, dtb_ref, dtbt_ref, al_ref, alt_ref, dx_ref, g_ref,
             e_ref, et_ref, dxbc_ref, dbz_ref, ddt_ref, dg_ref, ddtb_ref, dal_ref, dd_ref, ds_scr):
        c = pl.program_id(1)
        cc = cpe - 1 - c

        @pl.when(c == 0)
        def _():
            ds_scr[...] = jnp.zeros_like(ds_scr)

        @pl.when(jnp.logical_and(pl.program_id(0) == 0, c == 0))
        def _():
            dg_ref[...] = jnp.zeros_like(dg_ref)
            ddtb_ref[...] = jnp.zeros_like(ddtb_ref)
            dal_ref[...] = jnp.zeros_like(dal_ref)
            dd_ref[...] = jnp.zeros_like(dd_ref)

        rows = lax.broadcasted_iota(jnp.int32, (CHUNK, 1), 0)
        cols = lax.broadcasted_iota(jnp.int32, (1, CHUNK), 1)
        valid_col = jnp.logical_or(cc > 0, rows >= PAD)
        valid_row = jnp.logical_or(cc > 0, cols >= PAD)
        e_mat = e_ref[...]
        e_mat_t = et_ref[...]
        dtraw = dt_ref[...]
        a, dt, big_a, big_a_t, e_a, d_s, dt_x, e_a_x, d_s_x, cd_x = _ssd_common(
            dtraw, dtb_ref[...], dtbt_ref[...], al_ref[...], alt_ref[...], e_mat, valid_col, valid_row)
        xs = xbc_ref[:, 0:D_B]
        bs = xbc_ref[:, D_B:D_B + N_GROUPS * N_STATE].astype(BF16)
        cs = xbc_ref[:, D_B + N_GROUPS * N_STATE:N_XBC].astype(BF16)
        xdt = xs * dt_x
        xdt_b = xdt.astype(BF16)
        st_b = st_ref[...]
        dst = ds_scr[...]
        dst_b = dst.astype(BF16)

        ys = ys_ref[...]
        bz = bz_ref[...]
        sil = _silu(bz)
        z = ys * sil
        dz, dgt = _rms_bwd_math(z, g_ref[...], dy_ref[...])
        dg_ref[...] += jnp.sum(dgt, axis=0, keepdims=True)
        dbz_ref[...] = (dz * ys * _dsilu(bz)).astype(BF16)
        dys = dz * sil

        dd_lane = jnp.sum(dys * xs, axis=0, keepdims=True)
        dxs = dys * dx_ref[...]
        w_x = dys * e_a_x
        w_b = w_x.astype(BF16)
        dys_b = dys.astype(BF16)
        u_b = (xdt * d_s_x).astype(BF16)
        dxdt_parts, dbs_parts, dcs_parts, off_parts, g1_parts = [], [], [], [], []
        da_diag = jnp.zeros((CHUNK, LANE), F32)
        lane = lax.broadcasted_iota(jnp.int32, (1, LANE), 1)
        for g in range(N_GROUPS):
            gs = slice(g * N_STATE, (g + 1) * N_STATE)
            gw = slice(g * GROUP_W, (g + 1) * GROUP_W)
            cs_g, bs_g = cs[:, gs], bs[:, gs]
            dcs = lax.dot_general(w_b[:, gw], st_b[:, gw], NT_DIMS, preferred_element_type=F32)
            y_off = jnp.dot(cs_g, st_b[:, gw], preferred_element_type=F32)
            off_parts.append(y_off)
            dst_new = lax.dot_general(cs_g, w_b[:, gw], TN_DIMS, preferred_element_type=F32)
            g1 = jnp.dot(bs_g, dst_b[:, gw], preferred_element_type=F32)
            g1_parts.append(g1)
            dbs = lax.dot_general(u_b[:, gw], dst_b[:, gw], NT_DIMS, preferred_element_type=F32)
            cb = lax.dot_general(cs_g, bs_g, NT_DIMS, preferred_element_type=F32)
            cbt = lax.dot_general(bs_g, cs_g, NT_DIMS, preferred_element_type=F32)
            dcb = jnp.zeros((CHUNK, CHUNK), F32)
            dcbt = jnp.zeros((CHUNK, CHUNK), F32)
            dxdt_h = []
            for e in range(hpg):
                h = g * hpg + e
                hs = slice(h * HEAD_DIM, (h + 1) * HEAD_DIM)
                dec = _decay(big_a, big_a_t, h, False)
                dect = _decay(big_a, big_a_t, h, True)
                m = cb * dec
                mt = cbt * dect
                dxdt_h.append(jnp.dot(mt.astype(BF16), dys_b[:, hs], preferred_element_type=F32))
                dm = lax.dot_general(dys_b[:, hs], xdt_b[:, hs], NT_DIMS, preferred_element_type=F32)
                dmt = lax.dot_general(xdt_b[:, hs], dys_b[:, hs], NT_DIMS, preferred_element_type=F32)
                dcb = dcb + dm * dec
                dcbt = dcbt + dmt * dect
                da_h = jnp.sum(dm * m - dmt * mt, axis=1, keepdims=True)
                da_diag = da_diag + jnp.where(lane == h, da_h, 0.0)
            dcs = dcs + jnp.dot(dcb.astype(BF16), bs_g, preferred_element_type=F32)
            dbs = dbs + jnp.dot(dcbt.astype(BF16), cs_g, preferred_element_type=F32)
            dxdt_parts.append(jnp.concatenate(dxdt_h, axis=1) + g1 * d_s_x[:, gw])
            dbs_parts.append(dbs)
            dcs_parts.append(dcs)
            ds_scr[:, gw] = dst[:, gw] * cd_x[:, gw] + dst_new
        dxdt = jnp.concatenate(dxdt_parts, axis=1)
        y_off = jnp.concatenate(off_parts, axis=1)
        g1 = jnp.concatenate(g1_parts, axis=1)
        dds = _split_dot(g1 * xdt, e_mat_t, 2)
        da_off = _split_dot(w_x * y_off, e_mat_t, 2)
        ddt_x = _split_dot(dxdt * xs, e_mat_t, 2)
        dcd_lane = jnp.sum(dst * st_b.astype(F32), axis=0, keepdims=True)
        dcd = _split_dot(jnp.broadcast_to(dcd_lane, (SUB, D_B)), e_mat_t, 2)[0:1]
        dd_ref[...] += _split_dot(jnp.broadcast_to(dd_lane, (SUB, D_B)), e_mat_t, 2)[0:1]
        t_ds = dds * d_s
        d_a = da_diag + da_off - t_ds
        last_row = jnp.sum(t_ds, axis=0, keepdims=True) + dcd * e_a[CHUNK - 1:CHUNK, :]
        d_a = d_a + jnp.where(rows == CHUNK - 1, last_row, 0.0)
        dda = _split_dot_left(_tri(False).astype(BF16), d_a, 3)
        ddt = dda * a + ddt_x
        dal_ref[...] += jnp.sum(dda * dt * a, axis=0, keepdims=True)
        ddtraw = jnp.where(valid_col, ddt * jax.nn.sigmoid(dtraw + dtb_ref[...]), 0.0)
        ddtraw = jnp.where(lane < N_HEADS, ddtraw, 0.0)
        ddtb_ref[...] += jnp.sum(ddtraw, axis=0, keepdims=True)
        ddt_ref[...] = ddtraw.astype(BF16)
        dxs = dxs + dxdt * dt_x
        dxbc = jnp.concatenate([dxs] + dbs_parts + dcs_parts, axis=1)
        dxbc_ref[...] = jnp.where(valid_col, dxbc, 0.0)

    def row(width, col):
        return pl.BlockSpec((CHUNK, width), lambda b, c: (b * cpe + cpe - 1 - c, col))

    def const(shape):
        return pl.BlockSpec(shape, lambda b, c: (0,) * len(shape))

    return _grid_call(
        body, name, (nb, cpe),
        [row(D_B, 0), row(D_B, 0), row(N_XBC, 0), row(D_B, COL_BZ // D_B), row(LANE, COL_DT // LANE),
         pl.BlockSpec((None, N_STATE, D_B), lambda b, c: (b * cpe + cpe - 1 - c, 0, 0)),
         const((1, LANE)), const((N_HEADS, 1)), const((1, LANE)), const((N_HEADS, 1)),
         const((1, D_B)), const((1, D_B)), const((LANE, D_B)), const((D_B, LANE))],
        [row(N_XBC, 0), row(D_B, 0), row(LANE, 0),
         const((1, D_B)), const((1, LANE)), const((1, LANE)), const((1, LANE))],
        [jax.ShapeDtypeStruct((t, N_XBC), F32), jax.ShapeDtypeStruct((t, D_B), BF16),
         jax.ShapeDtypeStruct((t, LANE), BF16), jax.ShapeDtypeStruct((1, D_B), F32),
         jax.ShapeDtypeStruct((1, LANE), F32), jax.ShapeDtypeStruct((1, LANE), F32),
         jax.ShapeDtypeStruct((1, LANE), F32)],
        [pltpu.VMEM((N_STATE, D_B), F32)],
        (dy, y_ssd, xbc, proj, proj, states, dtb_p, dtb_t, alog_p, alog_t, d_skip_x, norm_g, e_mat, e_mat_t), comm)


HBM_SPEC = pl.BlockSpec(memory_space=pl.ANY)


def _mesh_pos():
    return lax.axis_index("x"), lax.axis_index("y"), lax.axis_index("c")


def _allgather(arrays, name):
    n = len(arrays)

    def body(*refs):
        xs, outs = refs[:n], refs[n:2 * n]
        send_sems, recv_sems, local_sems = refs[2 * n:]
        x, y, c = _mesh_pos()
        me, sibling = (x, y, c), (x, y, 1 - c)
        chips = [(1 - x, y), (x, 1 - y), (1 - x, 1 - y)]

        def slot(px, py, pc):
            return 4 * px + 2 * py + pc

        def copy(a, k, block, to, src=None):
            dst = outs[a].at[slot(*block)]
            return pltpu.make_async_remote_copy(
                src_ref=dst if src is None else src, dst_ref=dst, send_sem=send_sems.at[a, k], recv_sem=recv_sems.at[a, k],
                device_id=to, device_id_type=MESH)

        mine = [pltpu.make_async_copy(xs[a], outs[a].at[slot(*me)], local_sems.at[a]) for a in range(n)]
        for cp in mine:
            cp.start()
        first = []
        for a in range(n):
            first.append(copy(a, 0, me, sibling, src=xs[a]))
            first += [copy(a, 1 + j, me, (*chip, c), src=xs[a]) for j, chip in enumerate(chips)]
        for cp in first:
            cp.start()
        passed = []
        for j, chip in enumerate(chips):
            for a in range(n):
                copy(a, 1 + j, (*chip, c), me).wait_recv()
                cp = copy(a, 4 + j, (*chip, c), sibling)
                cp.start()
                passed.append(cp)
        for a in range(n):
            copy(a, 0, sibling, me).wait_recv()
        for j, chip in enumerate(chips):
            for a in range(n):
                copy(a, 4 + j, (*chip, 1 - c), me).wait_recv()
        for cp in first + passed:
            cp.wait_send()
        for cp in mine:
            cp.wait()

    return pl.pallas_call(
        body, name=name,
        in_specs=[HBM_SPEC] * n, out_specs=[HBM_SPEC] * n,
        out_shape=[jax.ShapeDtypeStruct((N_DEV,) + a.shape, a.dtype) for a in arrays],
        scratch_shapes=[pltpu.SemaphoreType.DMA((n, 7)), pltpu.SemaphoreType.DMA((n, 7)), pltpu.SemaphoreType.DMA((n,))],
    )(*arrays)


def _remote(src, dst, send_sem, recv_sem, to):
    return pltpu.make_async_remote_copy(src_ref=src, dst_ref=dst, send_sem=send_sem, recv_sem=recv_sem,
                                        device_id=to, device_id_type=MESH)


def _slot_plain(d):
    return d


def _slot_mix_rows(d):
    return jnp.where(d < 2, d + 4, jnp.where(d < 6, d - 2, d))


def _ag_ici(pieces, slots):
    n = len(pieces)

    def copies(ins, outs, sems, with_recvs):
        send_sems, recv_sems, local_sems = sems
        x, y, c = _mesh_pos()
        local, sends, recvs = [], [], []
        for a in range(n):
            mine = outs[a].at[slots[a](4 * x + 2 * y + c)]
            local.append(pltpu.make_async_copy(ins[a], mine, local_sems.at[a]))
            for k, (px, py) in enumerate([(1 - x, y), (x, 1 - y), (1 - x, 1 - y)]):
                sends.append(_remote(ins[a], mine, send_sems.at[a, k], recv_sems.at[a, k], (px, py, c)))
                if with_recvs:
                    theirs = outs[a].at[slots[a](4 * px + 2 * py + c)]
                    recvs.append(_remote(ins[a], theirs, send_sems.at[a, k], recv_sems.at[a, k], (px, py, c)))
        return local, sends, recvs

    return _Comm(pieces, [jax.ShapeDtypeStruct((N_DEV,) + p.shape, p.dtype) for p in pieces],
                 [pltpu.SemaphoreType.DMA((n, 3)), pltpu.SemaphoreType.DMA((n, 3)), pltpu.SemaphoreType.DMA((n,))], copies)


def _ag_d2d(bufs, slots):
    n = len(bufs)

    def copies(ins, outs, sems, with_recvs):
        send_sems, recv_sems = sems
        x, y, c = _mesh_pos()
        chips = [(x, y), (1 - x, y), (x, 1 - y), (1 - x, 1 - y)]
        sends, recvs = [], []
        for a in range(n):
            for k, (px, py) in enumerate(chips):
                held = slots[a](4 * px + 2 * py + c)
                sends.append(_remote(ins[a].at[held], outs[a].at[held], send_sems.at[a, k], recv_sems.at[a, k], (x, y, 1 - c)))
                if with_recvs:
                    got = slots[a](4 * px + 2 * py + 1 - c)
                    recvs.append(_remote(ins[a].at[got], outs[a].at[got], send_sems.at[a, k], recv_sems.at[a, k], (x, y, 1 - c)))
        return [], sends, recvs

    return _Comm(bufs, [jax.ShapeDtypeStruct(b.shape, b.dtype) for b in bufs],
                 [pltpu.SemaphoreType.DMA((n, 4)), pltpu.SemaphoreType.DMA((n, 4))], copies,
                 aliases={a: a for a in range(n)})


def _rs_d2d(blocks, slots):
    n = len(blocks)

    def copies(ins, outs, sems, with_recvs):
        send_sems, recv_sems = sems
        x, y, c = _mesh_pos()
        sends, recvs = [], []
        for a in range(n):
            for j in range(4):
                src = ins[a].at[slots[a](2 * j + 1 - c)]
                sends.append(_remote(src, outs[a].at[j], send_sems.at[a, j], recv_sems.at[a, j], (x, y, 1 - c)))
                if with_recvs:
                    recvs.append(_remote(src, outs[a].at[j], send_sems.at[a, j], recv_sems.at[a, j], (x, y, 1 - c)))
        return [], sends, recvs

    return _Comm(blocks, [jax.ShapeDtypeStruct((4,) + b.shape[1:], b.dtype) for b in blocks],
                 [pltpu.SemaphoreType.DMA((n, 4)), pltpu.SemaphoreType.DMA((n, 4))], copies)


def _rs_ici(blocks):
    n = len(blocks)

    def copies(ins, outs, sems, with_recvs):
        send_sems, recv_sems = sems
        x, y, c = _mesh_pos()
        sends, recvs = [], []
        for a in range(n):
            for k, (px, py) in enumerate([(1 - x, y), (x, 1 - y), (1 - x, 1 - y)]):
                src = ins[a].at[2 * px + py]
                sends.append(_remote(src, outs[a].at[k], send_sems.at[a, k], recv_sems.at[a, k], (px, py, c)))
                if with_recvs:
                    recvs.append(_remote(src, outs[a].at[k], send_sems.at[a, k], recv_sems.at[a, k], (px, py, c)))
        return [], sends, recvs

    return _Comm(blocks, [jax.ShapeDtypeStruct((3,) + b.shape[1:], b.dtype) for b in blocks],
                 [pltpu.SemaphoreType.DMA((n, 3)), pltpu.SemaphoreType.DMA((n, 3))], copies)


def _run_comm(comm, name):
    n_in, n_out = len(comm.inputs), len(comm.out_shapes)

    def body(*refs):
        ins, outs, sems = refs[:n_in], refs[n_in:n_in + n_out], refs[n_in + n_out:]
        comm.start(ins, outs, sems)
        comm.wait(ins, outs, sems)

    return pl.pallas_call(
        body, name=name, in_specs=[HBM_SPEC] * n_in, out_specs=[HBM_SPEC] * n_out, out_shape=comm.out_shapes,
        scratch_shapes=comm.sem_shapes, input_output_aliases=comm.aliases,
    )(*comm.inputs)


W_ROWS = 784


def _w_segments():
    per = N_IN // N_DEV
    out = []
    for d in range(N_DEV):
        lo, hi = per * d, per * (d + 1)
        for a, b, off in ((0, COL_C, 0), (COL_C, COL_C + N_HEADS, COL_DT - COL_C), (COL_C + N_HEADS, N_IN, -N_HEADS)):
            s, e = max(lo, a), min(hi, b)
            if s < e:
                out.append((d, s - lo, e - s, s + off))
    return out


def _w_gather_rows(g, name):
    tk = D_MODEL // 2
    u32 = jnp.uint32

    def body(g_ref, o_ref, scr):
        for d in range(N_DEV):
            x32 = pltpu.bitcast(g_ref[d], u32)
            for dd, src, rows, dst in _w_segments():
                if dd == d:
                    scr[dst // 2:(dst + rows) // 2, :] = x32[src // 2:(src + rows) // 2]
        scr[N_IN // 2:N_INP // 2, :] = jnp.zeros(((N_INP - N_IN) // 2, tk), u32)
        o_ref[...] = pltpu.bitcast(scr[...], BF16)

    return pl.pallas_call(
        body, name=name, grid=(D_MODEL // tk,),
        in_specs=[pl.BlockSpec((N_DEV, W_ROWS, tk), lambda j: (0, 0, j))],
        out_specs=pl.BlockSpec((N_INP, tk), lambda j: (0, j)),
        out_shape=jax.ShapeDtypeStruct((N_INP, D_MODEL), BF16),
        scratch_shapes=[pltpu.VMEM((N_INP // 2, tk), u32)],
        compiler_params=_cparams(("parallel",)),
    )(g)


def _w_split_rows(dwt, name):
    tk = D_MODEL // 4
    per = N_IN // N_DEV

    def body(w_ref, o_ref):
        for d, dst, rows, src in _w_segments():
            o_ref[d, dst:dst + rows, :] = w_ref[src:src + rows, :]
        for d in range(N_DEV):
            o_ref[d, per:W_ROWS, :] = jnp.zeros((W_ROWS - per, tk), F32)

    return pl.pallas_call(
        body, name=name, grid=(D_MODEL // tk,),
        in_specs=[pl.BlockSpec((N_INP, tk), lambda j: (0, j))],
        out_specs=pl.BlockSpec((N_DEV, W_ROWS, tk), lambda j: (0, 0, j)),
        out_shape=jax.ShapeDtypeStruct((N_DEV, W_ROWS, D_MODEL), F32),
        compiler_params=_cparams(("parallel",)),
    )(dwt)


def _rs_pair_sum(g, ra, own_slots, name):
    _, rows, cols = g.shape

    def body(s_ref, g_ref, ra_ref, o_ref):
        o_ref[...] = (g_ref[...] + ra_ref[...]).astype(BF16)

    return pl.pallas_call(
        body, name=name,
        grid_spec=pltpu.PrefetchScalarGridSpec(
            num_scalar_prefetch=1, grid=(4,),
            in_specs=[pl.BlockSpec((None, rows, cols), lambda j, s: (s[j], 0, 0)),
                      pl.BlockSpec((None, rows, cols), lambda j, s: (j, 0, 0))],
            out_specs=pl.BlockSpec((None, rows, cols), lambda j, s: (j, 0, 0))),
        out_shape=jax.ShapeDtypeStruct((4, rows, cols), BF16),
        compiler_params=_cparams(("parallel",)),
    )(own_slots, g, ra)


def _rs_final_sum(h, rb, chip_idx, name):
    _, rows, cols = h.shape

    def body(j_ref, h_ref, rb_ref, o_ref):
        o_ref[...] = ((h_ref[...].astype(F32) + rb_ref[0].astype(F32)) + rb_ref[1].astype(F32)) + rb_ref[2].astype(F32)

    return pl.pallas_call(
        body, name=name,
        grid_spec=pltpu.PrefetchScalarGridSpec(
            num_scalar_prefetch=1, grid=(1,),
            in_specs=[pl.BlockSpec((None, rows, cols), lambda i, j: (j[0], 0, 0)),
                      pl.BlockSpec((3, rows, cols), lambda i, j: (0, 0, 0))],
            out_specs=pl.BlockSpec((rows, cols), lambda i, j: (0, 0))),
        out_shape=jax.ShapeDtypeStruct((rows, cols), F32),
        compiler_params=_cparams(("arbitrary",)),
    )(chip_idx, h, rb)


def _adamw_math(w, g, m, v):
    m = ADAM_B1 * m + (1.0 - ADAM_B1) * g
    v = ADAM_B2 * v + (1.0 - ADAM_B2) * (g * g)
    m_hat = m / (1.0 - ADAM_B1 ** ADAM_STEP)
    v_hat = v / (1.0 - ADAM_B2 ** ADAM_STEP)
    delta = -ADAM_LR * (m_hat / (jnp.sqrt(v_hat) + ADAM_EPS) + ADAM_WD * w)
    return delta, m, v


def _adamw_rows(g, w, m, v, tr, name):
    rows, cols = w.shape

    def body(g_ref, w_ref, m_ref, v_ref, d_out, m_out, v_out):
        delta, m_new, v_new = _adamw_math(w_ref[...], g_ref[...], m_ref[...], v_ref[...])
        d_out[...] = delta
        m_out[...] = m_new
        v_out[...] = v_new

    flat = pl.BlockSpec((tr, cols), lambda r: (r, 0))
    return pl.pallas_call(
        body, name=name, grid=(rows // tr,),
        in_specs=[flat] * 4, out_specs=[flat] * 3,
        out_shape=[jax.ShapeDtypeStruct((rows, cols), F32)] * 3,
        compiler_params=_cparams(("parallel",)),
    )(g, w, m, v)


def _sum_devices(parts, name):
    _, p, _ = parts.shape

    def body(x_ref, o_ref):
        acc = x_ref[0]
        for d in range(1, N_DEV):
            acc = acc + x_ref[d]
        o_ref[...] = acc

    return pl.pallas_call(
        body, name=name, grid=(1,),
        in_specs=[pl.BlockSpec((N_DEV, p, LANE), lambda i: (0, 0, 0))],
        out_specs=pl.BlockSpec((p, LANE), lambda i: (0, 0)),
        out_shape=jax.ShapeDtypeStruct((p, LANE), F32),
        compiler_params=_cparams(("arbitrary",)),
    )(parts)


def _adamw_small(g, w, m, v, name):
    p = g.shape[0]

    def body(g_ref, w_ref, m_ref, v_ref, d_out, m_out, v_out):
        delta, m_new, v_new = _adamw_math(w_ref[...], g_ref[...], m_ref[...], v_ref[...])
        d_out[...] = delta
        m_out[...] = m_new
        v_out[...] = v_new

    spec = pl.BlockSpec((p, LANE), lambda i: (0, 0))
    return pl.pallas_call(
        body, name=name, grid=(1,),
        in_specs=[spec] * 4, out_specs=[spec] * 3,
        out_shape=[jax.ShapeDtypeStruct((p, LANE), F32)] * 3,
        compiler_params=_cparams(("arbitrary",)),
    )(g, w, m, v)


PACK_ALIGN = SUB * LANE

SMALL_PARAMS = (
    ("meta", (N_META, D_MODEL), 1),
    ("pre_g", (DEPTH, D_MODEL), None),
    ("post_g", (DEPTH, D_MODEL), None),
    ("conv_a_w", (DEPTH, CONV_A_K, D_A), 2),
    ("ssm_conv_w", (DEPTH, SSM_K, N_XBC), 2),
    ("ssm_conv_b", (DEPTH, N_XBC), None),
    ("dt_bias", (DEPTH, N_HEADS), None),
    ("a_log", (DEPTH, N_HEADS), None),
    ("d_skip", (DEPTH, N_HEADS), None),
    ("ssm_norm_g", (DEPTH, D_B), None),
    ("conf_conv_w", (DEPTH, CONF_K, D_C), 2),
    ("conf_conv_b", (DEPTH, D_C), None),
    ("conf_ln_g", (DEPTH, D_C), None),
    ("conf_ln_b", (DEPTH, D_C), None),
)


def _local_shape(shape, axis):
    if axis is None:
        return shape
    return tuple(s // N_DEV if k == axis else s for k, s in enumerate(shape))


def _pack(arrays):
    flat = []
    for a in arrays:
        v = a.reshape(-1).astype(F32)
        flat.append(jnp.pad(v, (0, (-v.shape[0]) % PACK_ALIGN)))
    return jnp.concatenate(flat).reshape(-1, LANE)


def _unpack(buf, shapes):
    flat = buf.reshape(-1)
    out, off = [], 0
    for s in shapes:
        size = 1
        for k in s:
            size *= k
        out.append(flat[off:off + size].reshape(s))
        off += size + (-size) % PACK_ALIGN
    return out


def _perm_cols(w):
    pad = jnp.zeros(w.shape[:-1] + (N_INP - N_IN,), w.dtype)
    return jnp.concatenate([w[..., :COL_C], w[..., COL_C + N_HEADS:N_IN], w[..., COL_C:COL_C + N_HEADS], pad], axis=-1)


def _unperm_cols(w):
    return jnp.concatenate([w[..., :COL_C], w[..., COL_DT:COL_DT + N_HEADS], w[..., COL_C:COL_DT]], axis=-1)


def kernel(x, meta, pre_g, post_g, w_in, w_out, conv_a_w, ssm_conv_w, ssm_conv_b, dt_bias, a_log, d_skip, ssm_norm_g, conf_conv_w, conf_conv_b, conf_ln_g, conf_ln_b, loss_target, m_meta, m_pre_g, m_post_g, m_w_in, m_w_out, m_conv_a_w, m_ssm_conv_w, m_ssm_conv_b, m_dt_bias, m_a_log, m_d_skip, m_ssm_norm_g, m_conf_conv_w, m_conf_conv_b, m_conf_ln_g, m_conf_ln_b, v_meta, v_pre_g, v_post_g, v_w_in, v_w_out, v_conv_a_w, v_ssm_conv_w, v_ssm_conv_b, v_dt_bias, v_a_log, v_d_skip, v_ssm_norm_g, v_conf_conv_w, v_conf_conv_b, v_conf_ln_g, v_conf_ln_b):
    weights = dict(meta=meta, pre_g=pre_g, post_g=post_g, conv_a_w=conv_a_w, ssm_conv_w=ssm_conv_w, ssm_conv_b=ssm_conv_b,
                   dt_bias=dt_bias, a_log=a_log, d_skip=d_skip, ssm_norm_g=ssm_norm_g, conf_conv_w=conf_conv_w,
                   conf_conv_b=conf_conv_b, conf_ln_g=conf_ln_g, conf_ln_b=conf_ln_b)
    mom1 = dict(meta=m_meta, pre_g=m_pre_g, post_g=m_post_g, conv_a_w=m_conv_a_w, ssm_conv_w=m_ssm_conv_w,
                ssm_conv_b=m_ssm_conv_b, dt_bias=m_dt_bias, a_log=m_a_log, d_skip=m_d_skip, ssm_norm_g=m_ssm_norm_g,
                conf_conv_w=m_conf_conv_w, conf_conv_b=m_conf_conv_b, conf_ln_g=m_conf_ln_g, conf_ln_b=m_conf_ln_b)
    mom2 = dict(meta=v_meta, pre_g=v_pre_g, post_g=v_post_g, conv_a_w=v_conv_a_w, ssm_conv_w=v_ssm_conv_w,
                ssm_conv_b=v_ssm_conv_b, dt_bias=v_dt_bias, a_log=v_a_log, d_skip=v_d_skip, ssm_norm_g=v_ssm_norm_g,
                conf_conv_w=v_conf_conv_w, conf_conv_b=v_conf_conv_b, conf_ln_g=v_conf_ln_g, conf_ln_b=v_conf_ln_b)
    nb, seq, d = x.shape
    lp = PAD + N_META + seq
    t = nb * lp
    assert lp % TILE == 0 and t % MM_TM == 0 and d == D_MODEL
    xi, yi, ci = _mesh_pos()
    dev = 4 * xi + 2 * yi + ci
    ci32 = ci.astype(jnp.int32)
    chip_idx = (2 * xi + yi).astype(jnp.int32).reshape(1)
    own_plain = jnp.stack([2 * j + ci32 for j in range(4)])
    own_mix = jnp.stack([_slot_mix_rows(2 * j + ci32) for j in range(4)]).astype(jnp.int32)
    n_in_loc = N_IN // N_DEV
    n_out_loc = 2 * D_MODEL // N_DEV
    slots = (_slot_plain, _slot_mix_rows)

    sharded_small = [n for n, _, ax in SMALL_PARAMS if ax is not None]
    small_shapes = {n: s for n, s, _ in SMALL_PARAMS}
    small_axis = {n: ax for n, _, ax in SMALL_PARAMS}
    sw_pack = _pack([weights[n] for n in sharded_small])
    (sw_g,) = _allgather([sw_pack], "ag_small_weights")
    wt_loc = jnp.pad(jnp.swapaxes(w_in, 1, 2).astype(BF16), ((0, 0), (0, W_ROWS - n_in_loc), (0, 0)))
    wo_loc = w_out.astype(BF16)

    def gather_finish(bufs):
        g_in_t, g_out_rows = _run_comm(_ag_d2d(list(bufs), slots), "ag_d2d")
        return _w_gather_rows(g_in_t, "w_gather_rows"), g_out_rows.reshape(2 * D_MODEL, D_MODEL)

    full = dict(weights)
    per_dev = [_unpack(sw_g[k], [_local_shape(small_shapes[n], small_axis[n]) for n in sharded_small]) for k in range(N_DEV)]
    for q, n in enumerate(sharded_small):
        full[n] = jnp.concatenate([per_dev[k][q] for k in range(N_DEV)], axis=small_axis[n])

    e_mat = (lax.broadcasted_iota(jnp.int32, (LANE, D_B), 0) == lax.broadcasted_iota(jnp.int32, (LANE, D_B), 1) // HEAD_DIM)
    e_mat = e_mat.astype(BF16)
    e_mat_t = e_mat.T

    front = jnp.concatenate([jnp.zeros((PAD, d), F32), full["meta"]], axis=0)
    h = jnp.concatenate([jnp.concatenate([front, x[b]], axis=0) for b in range(nb)], axis=0)
    saved = []
    w_t, w_o = gather_finish(_run_comm(_ag_ici([wt_loc[0], wo_loc[0]], slots), "ag_ici"))
    for i in range(DEPTH):
        row = lambda a: a[i].reshape(1, -1)
        hn = _rms_fwd(h, row(pre_g), "rms_fwd")
        proj = _mm(hn, w_t, "nt", F32, TILE, N_INP, D_MODEL, "mm_proj")
        ya = _a_fwd(proj, full["conv_a_w"][i], lp, "a_fwd")
        xbc = _xbc_fwd(proj, full["ssm_conv_w"][i], row(ssm_conv_b), lp, "xbc_fwd")
        yc = _c_fwd(proj, full["conf_conv_w"][i], row(conf_conv_b), row(conf_ln_g), row(conf_ln_b), lp, "c_fwd")
        d_skip_x = jnp.repeat(d_skip[i], HEAD_DIM).reshape(1, D_B)
        nxt = _ag_ici([wt_loc[i + 1], wo_loc[i + 1]], slots) if i + 1 < DEPTH else None
        yb, yssd, states, *bufs = _ssd_fwd(xbc, proj, dt_bias[i], a_log[i], d_skip_x, row(ssm_norm_g), e_mat, lp,
                                           "ssd_fwd", comm=nxt)
        ycat = jnp.concatenate([yb, ya, yc], axis=1)
        m = _mm(ycat, w_o, "nn", F32, MM_TM, D_MODEL, 2 * D_MODEL, "mm_out")
        saved.append((h, hn, proj, ycat, yssd, states, xbc, m, d_skip_x, w_t, w_o))
        h = _post_fwd(h, m, row(post_g), "post_fwd")
        if bufs:
            w_t, w_o = gather_finish(bufs)

    dh, loss_blk = _loss_kernel(h, loss_target.reshape(nb * seq, d), lp, "loss")

    grads = {n: [None] * DEPTH for n, _, _ in SMALL_PARAMS if n != "meta"}
    gt_in, gr_out = [None] * DEPTH, [None] * DEPTH
    dmeta = None
    pending = None

    def rs_pair_sums(blocks):
        ra_in, ra_out = _run_comm(_rs_d2d(list(blocks), slots), "rs_d2d")
        return [_rs_pair_sum(blocks[0], ra_in, own_plain, "rs_pair_sum_in"),
                _rs_pair_sum(blocks[1], ra_out, own_mix, "rs_pair_sum_out")]

    def rs_finish(layer, hs, rbs):
        gt_in[layer] = _rs_final_sum(hs[0], rbs[0], chip_idx, "rs_final_sum_in")
        gr_out[layer] = _rs_final_sum(hs[1], rbs[1], chip_idx, "rs_final_sum_out")

    for i in reversed(range(DEPTH)):
        row = lambda a: a[i].reshape(1, -1)
        h_i, hn, proj, ycat, yssd, states, xbc, m, d_skip_x, w_t, w_o = saved[i]
        dm, grads["post_g"][i] = _post_bwd(dh, m, row(post_g), "post_bwd")
        dy = _mm(dm, w_o, "nt", F32, MM_TM, 2 * D_MODEL, D_MODEL, "mm_dy")
        dw_out = _mm(ycat, dm, "tn", F32, D_MODEL, D_MODEL, t // 3, "mm_dwout")
        hs = rs_pair_sums(pending) if pending is not None else None
        dxbc, dbz, ddt, grads["ssm_norm_g"][i], ddtb, dal, dds, *rbs = _ssd_bwd(
            dy, yssd, xbc, proj, states, dt_bias[i], a_log[i], d_skip_x, row(ssm_norm_g), e_mat, e_mat_t, lp, "ssd_bwd",
            comm=_rs_ici(hs) if hs is not None else None)
        if hs is not None:
            rs_finish(i + 1, hs, rbs)
        grads["dt_bias"][i] = ddtb[:, :N_HEADS]
        grads["a_log"][i] = dal[:, :N_HEADS]
        grads["d_skip"][i] = dds[:, :N_HEADS]
        dpa, grads["conv_a_w"][i] = _a_bwd(dy, proj, full["conv_a_w"][i], lp, 2, "a_bwd")
        dpx, grads["ssm_conv_w"][i], grads["ssm_conv_b"][i] = _xbc_bwd(
            dxbc, proj, full["ssm_conv_w"][i], row(ssm_conv_b), lp, "xbc_bwd")
        dpc, grads["conf_conv_w"][i], grads["conf_conv_b"][i], grads["conf_ln_g"][i], grads["conf_ln_b"][i] = _c_bwd(
            dy, proj, full["conf_conv_w"][i], row(conf_conv_b), row(conf_ln_g), row(conf_ln_b), lp, 3, "c_bwd")
        dproj = jnp.concatenate([dpa, dbz, dpx, dpc, ddt], axis=1)
        dhn = _mm(dproj, w_t, "nn", F32, MM_TM, D_MODEL, N_INP, "mm_dhn")
        dw_in_t = _mm(dproj, hn, "tn", F32, MM_TN, D_MODEL, t // 3, "mm_dwin")
        dh, grads["pre_g"][i], dmeta = _rms_bwd(dh, dhn, h_i, row(pre_g), lp, "rms_bwd")
        pending = (_w_split_rows(dw_in_t, "w_split_rows"), dw_out.reshape(N_DEV, n_out_loc, D_MODEL))
    grad_x = dh.reshape(nb, lp, d)[:, PAD + N_META:]
    hs = rs_pair_sums(pending)
    rs_finish(0, hs, _run_comm(_rs_ici(hs), "rs_ici"))

    g_w_in = jnp.swapaxes(jnp.stack(gt_in)[:, :n_in_loc, :], 1, 2).reshape(DEPTH * D_MODEL, n_in_loc)
    g_w_out = jnp.stack(gr_out).reshape(DEPTH * n_out_loc, D_MODEL)
    big = {}
    big["w_in"] = [g_w_in, *_adamw_rows(g_w_in, w_in.reshape(-1, n_in_loc), m_w_in.reshape(-1, n_in_loc),
                                        v_w_in.reshape(-1, n_in_loc), 512, "adamw_w_in")]
    big["w_out"] = [g_w_out, *_adamw_rows(g_w_out, w_out.reshape(-1, D_MODEL), m_w_out.reshape(-1, D_MODEL),
                                          v_w_out.reshape(-1, D_MODEL), 256, "adamw_w_out")]
    big = {n: [a.reshape(s) for a in outs] for (n, outs), s in zip(big.items(), (w_in.shape, w_out.shape))}

    names = [n for n, _, _ in SMALL_PARAMS]
    partial = [loss_blk[0:1, 0:1], dmeta] + [jnp.concatenate(grads[n], axis=0) for n in names[1:]]
    part_pack = _pack(partial)
    (parts_g,) = _allgather([part_pack], "ag_small_grads")
    total = _unpack(_sum_devices(parts_g, "sum_small_grads"), [(1,)] + [small_shapes[n] for n in names])
    loss = total[0][0]
    g_small = {}
    for n, g in zip(names, total[1:]):
        ax = small_axis[n]
        if ax is not None:
            g = lax.dynamic_slice_in_dim(g, dev * (small_shapes[n][ax] // N_DEV), small_shapes[n][ax] // N_DEV, axis=ax)
        g_small[n] = g
    loc_shapes = [_local_shape(small_shapes[n], small_axis[n]) for n in names]
    d_pack, m_pack, v_pack = _adamw_small(_pack([g_small[n] for n in names]), _pack([weights[n] for n in names]),
                                          _pack([mom1[n] for n in names]), _pack([mom2[n] for n in names]), "adamw_small")
    d_small = dict(zip(names, _unpack(d_pack, loc_shapes)))
    m_small = dict(zip(names, _unpack(m_pack, loc_shapes)))
    v_small = dict(zip(names, _unpack(v_pack, loc_shapes)))

    order = ["meta", "pre_g", "post_g", "w_in", "w_out", "conv_a_w", "ssm_conv_w", "ssm_conv_b", "dt_bias", "a_log",
             "d_skip", "ssm_norm_g", "conf_conv_w", "conf_conv_b", "conf_ln_g", "conf_ln_b"]

    def pick(k, small):
        return [big[n][k] if n in big else small[n] for n in order]

    return (loss, grad_x, *pick(0, g_small), *pick(1, d_small), *pick(2, m_small), *pick(3, v_small))
```

```python
import functools

import jax
import jax.numpy as jnp
from jax import lax
from jax.experimental import pallas as pl
from jax.experimental.pallas import tpu as pltpu

F32 = jnp.float32
BF16 = jnp.bfloat16

D_MODEL = 1024
DEPTH = 4
SEQ = 2048
CHUNK = 64
N_META = 16
PAD = 48
LP = PAD + N_META + SEQ
D_A = 512
D_B = 1024
D_C = 512
N_HEADS = 16
HEAD_DIM = 64
N_STATE = 128
N_GROUPS = 2
GROUP_W = D_B // N_GROUPS
N_XBC = D_B + 2 * N_GROUPS * N_STATE
CONV_A_K = 3
SSM_K = 4
CONF_K = 31
NORM_EPS = 1e-6
LN_EPS = 1e-5
N_IN = 6160
N_INP = 6272
COL_BZ = 2048
COL_XBC = 3072
COL_C = 4608
COL_DT = 6144
LANE = 128
SUB = 8

ADAM_LR = 0.001
ADAM_B1 = 0.9
ADAM_B2 = 0.999
ADAM_EPS = 1e-08
ADAM_WD = 0.01
ADAM_STEP = 10

TILE = 192
HALO_A = 8
HALO_C = 32
MM_TM = 384
MM_TN = 896
VMEM_LIMIT = 56 * 1024 * 1024

MESH = pl.DeviceIdType.MESH
N_DEV = 8


def _silu(x):
    return x * jax.nn.sigmoid(x)


def _dsilu(x):
    s = jax.nn.sigmoid(x)
    return s * (1.0 + x * (1.0 - s))


def _cparams(sem=None):
    return pltpu.CompilerParams(dimension_semantics=sem, vmem_limit_bytes=VMEM_LIMIT)


def _mm(a, b, mode, out_dtype, tm, tn, tk, name, cols_outer=False):
    def ix(f):
        if cols_outer:
            return lambda j, i, q: f(i, j, q)
        return f

    if mode == "nn":
        (m, k), (_, n) = a.shape, b.shape
        a_spec = pl.BlockSpec((tm, tk), ix(lambda i, j, q: (i, q)))
        b_spec = pl.BlockSpec((tk, tn), ix(lambda i, j, q: (q, j)))
        dims = (((1,), (0,)), ((), ()))
    elif mode == "nt":
        (m, k), (n, _) = a.shape, b.shape
        a_spec = pl.BlockSpec((tm, tk), ix(lambda i, j, q: (i, q)))
        b_spec = pl.BlockSpec((tn, tk), ix(lambda i, j, q: (j, q)))
        dims = (((1,), (1,)), ((), ()))
    else:
        (k, m), (_, n) = a.shape, b.shape
        a_spec = pl.BlockSpec((tk, tm), ix(lambda i, j, q: (q, i)))
        b_spec = pl.BlockSpec((tk, tn), ix(lambda i, j, q: (q, j)))
        dims = (((0,), (0,)), ((), ()))
    assert m % tm == 0 and n % tn == 0 and k % tk == 0, (name, a.shape, b.shape)
    nk = k // tk
    grid = (n // tn, m // tm, nk) if cols_outer else (m // tm, n // tn, nk)

    def body(a_ref, b_ref, o_ref, acc_ref):
        part = lax.dot_general(a_ref[...].astype(BF16), b_ref[...].astype(BF16), dims, preferred_element_type=F32)
        if nk == 1:
            o_ref[...] = part.astype(o_ref.dtype)
        else:
            q = pl.program_id(2)

            @pl.when(q == 0)
            def _():
                acc_ref[...] = part

            @pl.when(q > 0)
            def _():
                acc_ref[...] += part

            @pl.when(q == nk - 1)
            def _():
                o_ref[...] = acc_ref[...].astype(o_ref.dtype)

    return pl.pallas_call(
        body, name=name, grid=grid,
        in_specs=[a_spec, b_spec], out_specs=pl.BlockSpec((tm, tn), ix(lambda i, j, q: (i, j))),
        out_shape=jax.ShapeDtypeStruct((m, n), out_dtype),
        scratch_shapes=[pltpu.VMEM((tm, tn) if nk > 1 else (SUB, LANE), F32)],
        compiler_params=_cparams(("parallel", "parallel", "arbitrary")),
    )(a, b)


def _row_mask(i, tpe, rows):
    r = lax.broadcasted_iota(jnp.int32, (rows, 1), 0)
    return jnp.logical_or((i % tpe) != 0, r >= PAD)


def _rms_fwd(h, g, name):
    t, d = h.shape

    def body(h_ref, g_ref, o_ref):
        x = h_ref[...]
        r = lax.rsqrt(jnp.mean(x * x, axis=-1, keepdims=True) + NORM_EPS)
        o_ref[...] = (x * r * g_ref[...]).astype(BF16)

    return pl.pallas_call(
        body, name=name, grid=(t // TILE,),
        in_specs=[pl.BlockSpec((TILE, d), lambda i: (i, 0)), pl.BlockSpec((1, d), lambda i: (0, 0))],
        out_specs=pl.BlockSpec((TILE, d), lambda i: (i, 0)),
        out_shape=jax.ShapeDtypeStruct((t, d), BF16),
        compiler_params=_cparams(("parallel",)),
    )(h, g)


def _post_fwd(h, m, g, name):
    t, d = h.shape

    def body(h_ref, m_ref, g_ref, o_ref):
        x = m_ref[...]
        r = lax.rsqrt(jnp.mean(x * x, axis=-1, keepdims=True) + NORM_EPS)
        o_ref[...] = h_ref[...] + x * r * g_ref[...]

    row = pl.BlockSpec((TILE, d), lambda i: (i, 0))
    return pl.pallas_call(
        body, name=name, grid=(t // TILE,),
        in_specs=[row, row, pl.BlockSpec((1, d), lambda i: (0, 0))], out_specs=row,
        out_shape=jax.ShapeDtypeStruct((t, d), F32),
        compiler_params=_cparams(("parallel",)),
    )(h, m, g)


def _rms_bwd_math(x, g, dy):
    r = lax.rsqrt(jnp.mean(x * x, axis=-1, keepdims=True) + NORM_EPS)
    gdy = dy * g
    dx = r * gdy - x * (r * r * r) * jnp.mean(gdy * x, axis=-1, keepdims=True)
    return dx, dy * x * r


def _post_bwd(dh, m, g, name):
    t, d = dh.shape

    def body(dh_ref, m_ref, g_ref, dm_ref, dg_ref):
        dm, dgt = _rms_bwd_math(m_ref[...], g_ref[...], dh_ref[...])
        dm_ref[...] = dm

        @pl.when(pl.program_id(0) == 0)
        def _():
            dg_ref[...] = jnp.zeros_like(dg_ref)

        dg_ref[...] += jnp.sum(dgt, axis=0, keepdims=True)

    row = pl.BlockSpec((TILE, d), lambda i: (i, 0))
    vec = pl.BlockSpec((1, d), lambda i: (0, 0))
    return pl.pallas_call(
        body, name=name, grid=(t // TILE,),
        in_specs=[row, row, vec], out_specs=[row, vec],
        out_shape=[jax.ShapeDtypeStruct((t, d), F32), jax.ShapeDtypeStruct((1, d), F32)],
        compiler_params=_cparams(("arbitrary",)),
    )(dh, m, g)


def _rms_bwd(dh_res, dhn, h, g, lp, name):
    t, d = h.shape
    tpe = lp // TILE

    def body(dr_ref, dn_ref, h_ref, g_ref, dh_ref, dg_ref, dmeta_ref):
        i = pl.program_id(0)
        dx, dgt = _rms_bwd_math(h_ref[...], g_ref[...], dn_ref[...])
        dh = dr_ref[...] + dx
        dh_ref[...] = dh

        @pl.when(i == 0)
        def _():
            dg_ref[...] = jnp.zeros_like(dg_ref)
            dmeta_ref[...] = jnp.zeros_like(dmeta_ref)

        dg_ref[...] += jnp.sum(dgt, axis=0, keepdims=True)

        @pl.when((i % tpe) == 0)
        def _():
            dmeta_ref[...] += dh[PAD:PAD + N_META, :]

    row = pl.BlockSpec((TILE, d), lambda i: (i, 0))
    vec = pl.BlockSpec((1, d), lambda i: (0, 0))
    return pl.pallas_call(
        body, name=name, grid=(t // TILE,),
        in_specs=[row, row, row, vec],
        out_specs=[row, vec, pl.BlockSpec((N_META, d), lambda i: (0, 0))],
        out_shape=[jax.ShapeDtypeStruct((t, d), F32), jax.ShapeDtypeStruct((1, d), F32),
                   jax.ShapeDtypeStruct((N_META, d), F32)],
        compiler_params=_cparams(("arbitrary",)),
    )(dh_res, dhn, h, g)


def _loss_kernel(h, target, lp, name):
    t, d = h.shape
    cpe = lp // CHUNK
    nb = t // lp

    def body(h_ref, t_ref, dh_ref, loss_ref):
        b, j = pl.program_id(0), pl.program_id(1)

        @pl.when(jnp.logical_and(b == 0, j == 0))
        def _():
            loss_ref[...] = jnp.zeros_like(loss_ref)

        @pl.when(j == 0)
        def _():
            dh_ref[...] = jnp.zeros_like(dh_ref)

        @pl.when(j > 0)
        def _():
            err = h_ref[...] - t_ref[...]
            dh_ref[...] = err * (1.0 / d)
            loss_ref[...] += (0.5 / d) * jnp.sum(err * err)

    return pl.pallas_call(
        body, name=name, grid=(nb, cpe),
        in_specs=[pl.BlockSpec((CHUNK, d), lambda b, j: (b * cpe + j, 0)),
                  pl.BlockSpec((CHUNK, d), lambda b, j: (b * (cpe - 1) + jnp.maximum(j - 1, 0), 0))],
        out_specs=[pl.BlockSpec((CHUNK, d), lambda b, j: (b * cpe + j, 0)),
                   pl.BlockSpec((SUB, LANE), lambda b, j: (0, 0))],
        out_shape=[jax.ShapeDtypeStruct((t, d), F32), jax.ShapeDtypeStruct((SUB, LANE), F32)],
        compiler_params=_cparams(("arbitrary", "arbitrary")),
    )(h, target)


def _halo_specs(t, lp, width, col, halo):
    per = TILE // halo
    last = t // halo - 1
    cur = pl.BlockSpec((TILE, width), lambda i: (i, col))
    prev = pl.BlockSpec((halo, width), lambda i: (jnp.maximum(i * per - 1, 0), col))
    nxt = pl.BlockSpec((halo, width), lambda i: (jnp.minimum((i + 1) * per, last), col))
    return cur, prev, nxt


def _rows_from(x, start, rows):
    s = start % SUB
    if s == 0:
        return x[start:start + rows]
    return pltpu.roll(x, x.shape[0] - s, axis=0)[start - s:start - s + rows]


def _conv_a(ve, w):
    rows = ve.shape[0] - HALO_A
    return (w[0:1] * _rows_from(ve, HALO_A - 2, rows) + w[1:2] * _rows_from(ve, HALO_A - 1, rows)
            + w[2:3] * ve[HALO_A:HALO_A + rows])


def _a_fwd(proj, w, lp, name):
    t = proj.shape[0]
    tpe = lp // TILE
    cur, prev, _ = _halo_specs(t, lp, 4 * D_A, 0, HALO_A)

    def body(p_ref, ph_ref, w_ref, y_ref):
        first = (pl.program_id(0) % tpe) == 0
        v = p_ref[:, D_A:2 * D_A] * p_ref[:, 2 * D_A:3 * D_A]
        vh = jnp.where(first, 0.0, ph_ref[:, D_A:2 * D_A] * ph_ref[:, 2 * D_A:3 * D_A])
        cv = _conv_a(jnp.concatenate([vh, v], axis=0), w_ref[...])
        y_ref[...] = (p_ref[:, 0:D_A] * cv * _silu(p_ref[:, 3 * D_A:4 * D_A])).astype(BF16)

    return pl.pallas_call(
        body, name=name, grid=(t // TILE,),
        in_specs=[cur, prev, pl.BlockSpec((CONV_A_K, D_A), lambda i: (0, 0))],
        out_specs=pl.BlockSpec((TILE, D_A), lambda i: (i, 0)),
        out_shape=jax.ShapeDtypeStruct((t, D_A), BF16),
        compiler_params=_cparams(("parallel",)),
    )(proj, proj, w)


def _a_bwd(dy, proj, w, lp, dy_col, name):
    t = proj.shape[0]
    tpe = lp // TILE
    cur, prev, nxt = _halo_specs(t, lp, 4 * D_A, 0, HALO_A)
    dcur, _, dnxt = _halo_specs(t, lp, D_A, dy_col, HALO_A)

    def body(dy_ref, dyn_ref, p_ref, ph_ref, pn_ref, w_ref, dp_ref, dw_ref):
        i = pl.program_id(0)
        first = (i % tpe) == 0
        last = (i % tpe) == tpe - 1
        w = w_ref[...]
        ab, ac, ax, az = (p_ref[:, k * D_A:(k + 1) * D_A] for k in range(4))
        v = ac * ax
        vh = jnp.where(first, 0.0, ph_ref[:, D_A:2 * D_A] * ph_ref[:, 2 * D_A:3 * D_A])
        ve = jnp.concatenate([vh, v], axis=0)
        taps = [_rows_from(ve, HALO_A - 2 + k, TILE) for k in range(CONV_A_K)]
        cv = w[0:1] * taps[0] + w[1:2] * taps[1] + w[2:3] * taps[2]
        s = _silu(az)
        dy_ = dy_ref[...]
        dcv = dy_ * ab * s
        dcvn = jnp.where(last, 0.0, dyn_ref[...] * pn_ref[:, 0:D_A] * _silu(pn_ref[:, 3 * D_A:4 * D_A]))
        dce = jnp.concatenate([dcv, dcvn], axis=0)
        dv = w[2:3] * dce[0:TILE] + w[1:2] * _rows_from(dce, 1, TILE) + w[0:1] * _rows_from(dce, 2, TILE)
        dp = jnp.concatenate([dy_ * cv * s, dv * ax, dv * ac, dy_ * ab * cv * _dsilu(az)], axis=1)
        dp_ref[...] = jnp.where(_row_mask(i, tpe, TILE), dp, 0.0).astype(BF16)
        dw = jnp.concatenate(
            [jnp.sum(dcv * taps[k], axis=0, keepdims=True) for k in range(CONV_A_K)], axis=0)

        @pl.when(i == 0)
        def _():
            dw_ref[...] = jnp.zeros_like(dw_ref)

        dw_ref[...] += dw

    wspec = pl.BlockSpec((CONV_A_K, D_A), lambda i: (0, 0))
    return pl.pallas_call(
        body, name=name, grid=(t // TILE,),
        in_specs=[dcur, dnxt, cur, prev, nxt, wspec],
        out_specs=[pl.BlockSpec((TILE, 4 * D_A), lambda i: (i, 0)), wspec],
        out_shape=[jax.ShapeDtypeStruct((t, 4 * D_A), BF16), jax.ShapeDtypeStruct((CONV_A_K, D_A), F32)],
        compiler_params=_cparams(("arbitrary",)),
    )(dy, dy, proj, proj, proj, w)


def _conv_ssm(xe, w, rows, off):
    acc = w[0:1] * _rows_from(xe, off - 3, rows)
    for k in range(1, SSM_K):
        acc = acc + w[k:k + 1] * _rows_from(xe, off - 3 + k, rows)
    return acc


def _xbc_fwd(proj, w, b, lp, name):
    t = proj.shape[0]
    tpe = lp // TILE
    cur, prev, _ = _halo_specs(t, lp, N_XBC, COL_XBC // N_XBC, HALO_A)

    def body(x_ref, xh_ref, w_ref, b_ref, o_ref):
        first = (pl.program_id(0) % tpe) == 0
        xh = jnp.where(first, 0.0, xh_ref[...])
        xe = jnp.concatenate([xh, x_ref[...]], axis=0)
        o_ref[...] = _silu(_conv_ssm(xe, w_ref[...], TILE, HALO_A) + b_ref[...])

    return pl.pallas_call(
        body, name=name, grid=(t // TILE,),
        in_specs=[cur, prev, pl.BlockSpec((SSM_K, N_XBC), lambda i: (0, 0)), pl.BlockSpec((1, N_XBC), lambda i: (0, 0))],
        out_specs=pl.BlockSpec((TILE, N_XBC), lambda i: (i, 0)),
        out_shape=jax.ShapeDtypeStruct((t, N_XBC), F32),
        compiler_params=_cparams(("parallel",)),
    )(proj, proj, w, b)


def _xbc_bwd(dxbc, proj, w, b, lp, name):
    t = proj.shape[0]
    tpe = lp // TILE
    cur, prev, nxt = _halo_specs(t, lp, N_XBC, COL_XBC // N_XBC, HALO_A)
    dcur, _, dnxt = _halo_specs(t, lp, N_XBC, 0, HALO_A)

    def body(d_ref, dn_ref, x_ref, xh_ref, xn_ref, w_ref, b_ref, dx_ref, dw_ref, db_ref):
        i = pl.program_id(0)
        first = (i % tpe) == 0
        last = (i % tpe) == tpe - 1
        w = w_ref[...]
        xh = jnp.where(first, 0.0, xh_ref[...])
        xe = jnp.concatenate([xh, x_ref[...], xn_ref[...]], axis=0)
        taps = [_rows_from(xe, HALO_A - 3 + k, TILE + HALO_A) for k in range(SSM_K)]
        pre = b_ref[...] + w[0:1] * taps[0]
        for k in range(1, SSM_K):
            pre = pre + w[k:k + 1] * taps[k]
        de = jnp.concatenate([d_ref[...], jnp.where(last, 0.0, dn_ref[...])], axis=0)
        dpre = de * _dsilu(pre)
        dx = w[3:4] * dpre[0:TILE]
        for k in range(SSM_K - 1):
            dx = dx + w[k:k + 1] * _rows_from(dpre, 3 - k, TILE)
        dx_ref[...] = jnp.where(_row_mask(i, tpe, TILE), dx, 0.0).astype(BF16)
        dpc = dpre[0:TILE]
        dw = jnp.concatenate(
            [jnp.sum(dpc * taps[k][0:TILE], axis=0, keepdims=True) for k in range(SSM_K)], axis=0)

        @pl.when(i == 0)
        def _():
            dw_ref[...] = jnp.zeros_like(dw_ref)
            db_ref[...] = jnp.zeros_like(db_ref)

        dw_ref[...] += dw
        db_ref[...] += jnp.sum(dpc, axis=0, keepdims=True)

    wspec = pl.BlockSpec((SSM_K, N_XBC), lambda i: (0, 0))
    bspec = pl.BlockSpec((1, N_XBC), lambda i: (0, 0))
    return pl.pallas_call(
        body, name=name, grid=(t // TILE,),
        in_specs=[dcur, dnxt, cur, prev, nxt, wspec, bspec],
        out_specs=[pl.BlockSpec((TILE, N_XBC), lambda i: (i, 0)), wspec, bspec],
        out_shape=[jax.ShapeDtypeStruct((t, N_XBC), BF16), jax.ShapeDtypeStruct((SSM_K, N_XBC), F32),
                   jax.ShapeDtypeStruct((1, N_XBC), F32)],
        compiler_params=_cparams(("arbitrary",)),
    )(dxbc, dxbc, proj, proj, proj, w, b)


SUBROWS = 32


def _fill_shifted(scr, x):
    scr[0] = x
    for s in range(1, SUB):
        scr[s] = pltpu.roll(x, x.shape[0] - s, axis=0)


def _window(scr, start, rows):
    s = start % SUB
    return scr[s, start - s:start - s + rows, :]


def _conv_conf(scr, w, rows, off, base):
    acc = w[0:1] * _window(scr, base + off - (CONF_K - 1), rows)
    for k in range(1, CONF_K):
        acc = acc + w[k:k + 1] * _window(scr, base + off - (CONF_K - 1) + k, rows)
    return acc


def _ln_fwd(u1, g, b):
    mu = jnp.mean(u1, axis=-1, keepdims=True)
    xc = u1 - mu
    rstd = lax.rsqrt(jnp.mean(xc * xc, axis=-1, keepdims=True) + LN_EPS)
    n = xc * rstd
    return n, rstd, n * g + b


def _c_fwd(proj, w, cb, g, b, lp, name):
    t = proj.shape[0]
    tpe = lp // TILE
    cur, prev, _ = _halo_specs(t, lp, 3 * D_C, COL_C // (3 * D_C), HALO_C)

    def body(p_ref, ph_ref, w_ref, cb_ref, g_ref, b_ref, y_ref, u0_scr):
        first = (pl.program_id(0) % tpe) == 0
        u0h = jnp.where(first, 0.0, ph_ref[:, 0:D_C] * jax.nn.sigmoid(ph_ref[:, D_C:2 * D_C]))
        _fill_shifted(u0_scr, jnp.concatenate([u0h, p_ref[:, 0:D_C] * jax.nn.sigmoid(p_ref[:, D_C:2 * D_C])], axis=0))
        w = w_ref[...]
        for r0 in range(0, TILE, SUBROWS):
            u1 = _conv_conf(u0_scr, w, SUBROWS, HALO_C, r0) + cb_ref[...]
            _, _, u2 = _ln_fwd(u1, g_ref[...], b_ref[...])
            y_ref[r0:r0 + SUBROWS, :] = (_silu(u2) * _silu(p_ref[r0:r0 + SUBROWS, 2 * D_C:3 * D_C])).astype(BF16)

    vec = pl.BlockSpec((1, D_C), lambda i: (0, 0))
    return pl.pallas_call(
        body, name=name, grid=(t // TILE,),
        in_specs=[cur, prev, pl.BlockSpec((CONF_K, D_C), lambda i: (0, 0)), vec, vec, vec],
        out_specs=pl.BlockSpec((TILE, D_C), lambda i: (i, 0)),
        out_shape=jax.ShapeDtypeStruct((t, D_C), BF16),
        scratch_shapes=[pltpu.VMEM((SUB, HALO_C + TILE, D_C), F32)],
        compiler_params=_cparams(("parallel",)),
    )(proj, proj, w, cb, g, b)


def _c_bwd(dy, proj, w, cb, g, b, lp, dy_col, name):
    t = proj.shape[0]
    tpe = lp // TILE
    cur, prev, nxt = _halo_specs(t, lp, 3 * D_C, COL_C // (3 * D_C), HALO_C)
    dcur, _, dnxt = _halo_specs(t, lp, D_C, dy_col, HALO_C)
    ext = TILE + HALO_C

    def body(dy_ref, dyn_ref, p_ref, ph_ref, pn_ref, w_ref, cb_ref, g_ref, b_ref,
             dp_ref, dw_ref, dcb_ref, dg_ref, db_ref, u0_scr, du1_scr, wacc_scr):
        i = pl.program_id(0)
        first = (i % tpe) == 0
        last = (i % tpe) == tpe - 1
        w = w_ref[...]

        @pl.when(i == 0)
        def _():
            dw_ref[...] = jnp.zeros_like(dw_ref)
            dcb_ref[...] = jnp.zeros_like(dcb_ref)
            dg_ref[...] = jnp.zeros_like(dg_ref)
            db_ref[...] = jnp.zeros_like(db_ref)

        u0h = jnp.where(first, 0.0, ph_ref[:, 0:D_C] * jax.nn.sigmoid(ph_ref[:, D_C:2 * D_C]))
        _fill_shifted(u0_scr, jnp.concatenate(
            [u0h, p_ref[:, 0:D_C] * jax.nn.sigmoid(p_ref[:, D_C:2 * D_C]),
             pn_ref[:, 0:D_C] * jax.nn.sigmoid(pn_ref[:, D_C:2 * D_C])], axis=0))
        dcb = jnp.zeros((1, D_C), F32)
        dg = jnp.zeros((1, D_C), F32)
        db = jnp.zeros((1, D_C), F32)
        for r0 in range(0, ext, SUBROWS):
            in_tile = r0 < TILE
            src, dsrc, q0 = (p_ref, dy_ref, r0) if in_tile else (pn_ref, dyn_ref, r0 - TILE)
            u1 = _conv_conf(u0_scr, w, SUBROWS, HALO_C, r0) + cb_ref[...]
            n, rstd, u2 = _ln_fwd(u1, g_ref[...], b_ref[...])
            cz = src[q0:q0 + SUBROWS, 2 * D_C:3 * D_C]
            dyc = dsrc[q0:q0 + SUBROWS, :]
            if not in_tile:
                dyc = jnp.where(last, 0.0, dyc)
            du2 = dyc * _silu(cz) * _dsilu(u2)
            dn = du2 * g_ref[...]
            du1 = rstd * (dn - jnp.mean(dn, axis=-1, keepdims=True) - n * jnp.mean(dn * n, axis=-1, keepdims=True))
            du1_scr[0, r0:r0 + SUBROWS, :] = du1
            if in_tile:
                dp_ref[r0:r0 + SUBROWS, 2 * D_C:3 * D_C] = (dyc * _silu(u2) * _dsilu(cz)).astype(BF16)
                dg = dg + jnp.sum(du2 * n, axis=0, keepdims=True)
                db = db + jnp.sum(du2, axis=0, keepdims=True)
                dcb = dcb + jnp.sum(du1, axis=0, keepdims=True)
        dcb_ref[...] += dcb
        dg_ref[...] += dg
        db_ref[...] += db
        mask = _row_mask(i, tpe, TILE)
        _fill_shifted(du1_scr, du1_scr[0])
        for r0 in range(0, TILE, SUBROWS):
            acc = w[0:1] * _window(du1_scr, r0 + CONF_K - 1, SUBROWS)
            for k in range(1, CONF_K):
                acc = acc + w[k:k + 1] * _window(du1_scr, r0 + CONF_K - 1 - k, SUBROWS)
            ca = p_ref[r0:r0 + SUBROWS, 0:D_C]
            sg = jax.nn.sigmoid(p_ref[r0:r0 + SUBROWS, D_C:2 * D_C])
            m = mask[r0:r0 + SUBROWS]
            dp_ref[r0:r0 + SUBROWS, 0:D_C] = jnp.where(m, acc * sg, 0.0).astype(BF16)
            dp_ref[r0:r0 + SUBROWS, D_C:2 * D_C] = jnp.where(m, acc * ca * sg * (1.0 - sg), 0.0).astype(BF16)
        for k in range(CONF_K):
            part = jnp.zeros((SUB, D_C), F32)
            for r0 in range(0, TILE, SUBROWS):
                prod = du1_scr[0, r0:r0 + SUBROWS, :] * _window(u0_scr, HALO_C + r0 - (CONF_K - 1) + k, SUBROWS)
                for q in range(0, SUBROWS, SUB):
                    part = part + prod[q:q + SUB]
            wacc_scr[k:k + 1, :] = jnp.sum(part, axis=0, keepdims=True)
        dw_ref[...] += wacc_scr[0:CONF_K, :]

    vec = pl.BlockSpec((1, D_C), lambda i: (0, 0))
    wspec = pl.BlockSpec((CONF_K, D_C), lambda i: (0, 0))
    return pl.pallas_call(
        body, name=name, grid=(t // TILE,),
        in_specs=[dcur, dnxt, cur, prev, nxt, wspec, vec, vec, vec],
        out_specs=[pl.BlockSpec((TILE, 3 * D_C), lambda i: (i, 0)), wspec, vec, vec, vec],
        out_shape=[jax.ShapeDtypeStruct((t, 3 * D_C), BF16), jax.ShapeDtypeStruct((CONF_K, D_C), F32),
                   jax.ShapeDtypeStruct((1, D_C), F32), jax.ShapeDtypeStruct((1, D_C), F32),
                   jax.ShapeDtypeStruct((1, D_C), F32)],
        scratch_shapes=[pltpu.VMEM((SUB, ext + HALO_C, D_C), F32), pltpu.VMEM((SUB, ext, D_C), F32),
                        pltpu.VMEM((HALO_C, D_C), F32)],
        compiler_params=_cparams(("arbitrary",)),
    )(dy, dy, proj, proj, proj, w, cb, g, b)


def _split_dot(x, m_bf16, terms):
    acc = None
    rem = x
    for _ in range(terms):
        hi = rem.astype(BF16)
        part = jnp.dot(hi, m_bf16, preferred_element_type=F32)
        acc = part if acc is None else acc + part
        rem = rem - hi.astype(F32)
    return acc


def _split_dot_left(m_bf16, x, terms):
    acc = None
    rem = x
    for _ in range(terms):
        hi = rem.astype(BF16)
        part = jnp.dot(m_bf16, hi, preferred_element_type=F32)
        acc = part if acc is None else acc + part
        rem = rem - hi.astype(F32)
    return acc


def _tri(rows_ge_cols):
    r = lax.broadcasted_iota(jnp.int32, (CHUNK, CHUNK), 0)
    c = lax.broadcasted_iota(jnp.int32, (CHUNK, CHUNK), 1)
    return (r >= c) if rows_ge_cols else (r <= c)


def _softplus(x):
    return jnp.maximum(x, 0.0) + jnp.log(1.0 + jnp.exp(-jnp.abs(x)))


def _ssd_common(dtraw, dtb, dtb_t, a_log, a_log_t, e_mat, valid_col, valid_row):
    a = -jnp.exp(a_log)
    lane = lax.broadcasted_iota(jnp.int32, (1, LANE), 1)
    a = jnp.where(lane < N_HEADS, a, 0.0)
    a_t = -jnp.exp(a_log_t)
    dt = jnp.where(valid_col, _softplus(dtraw + dtb), 0.0)
    dt = jnp.where(lane < N_HEADS, dt, 0.0)
    dt_t = jnp.where(valid_row, _softplus(dtraw.T[0:N_HEADS, :] + dtb_t), 0.0)
    ltri = _tri(True).astype(BF16)
    utri = _tri(False).astype(BF16)
    big_a = _split_dot_left(ltri, dt * a, 3)
    big_a_t = _split_dot(dt_t * a_t, utri, 3)
    e_a = jnp.exp(big_a)
    d_s = jnp.exp(big_a[CHUNK - 1:CHUNK, :] - big_a)
    dt_x = _split_dot(dt, e_mat, 2)
    e_a_x = _split_dot(e_a, e_mat, 2)
    d_s_x = _split_dot(d_s, e_mat, 2)
    cd_x = e_a_x[CHUNK - 1:CHUNK, :]
    return a, dt, big_a, big_a_t, e_a, d_s, dt_x, e_a_x, d_s_x, cd_x


def _decay(big_a, big_a_t, h, transposed):
    col = big_a[:, h:h + 1]
    row = big_a_t[h:h + 1, :]
    if not transposed:
        seg = col - row
        return jnp.where(_tri(True), jnp.exp(jnp.minimum(seg, 0.0)), 0.0)
    seg = row - col
    return jnp.where(_tri(False), jnp.exp(jnp.minimum(seg, 0.0)), 0.0)


NT_DIMS = (((1,), (1,)), ((), ()))
TN_DIMS = (((0,), (0,)), ((), ()))
HBM_SPEC = pl.BlockSpec(memory_space=pl.ANY)


class _Comm:
    def __init__(self, inputs, out_shapes, sem_shapes, copies, aliases=None):
        self.inputs, self.out_shapes, self.sem_shapes, self.copies = inputs, out_shapes, sem_shapes, copies
        self.aliases = aliases or {}

    def start(self, ins, outs, sems):
        local, sends, _ = self.copies(ins, outs, sems, False)
        for cp in local + sends:
            cp.start()

    def wait(self, ins, outs, sems):
        local, sends, recvs = self.copies(ins, outs, sems, True)
        for cp in recvs:
            cp.wait_recv()
        for cp in sends:
            cp.wait_send()
        for cp in local:
            cp.wait()


def _grid_call(body, name, grid, in_specs, out_specs, out_shape, scratch_shapes, operands, comm=None):
    if comm is None:
        return pl.pallas_call(
            body, name=name, grid=grid, in_specs=in_specs, out_specs=out_specs, out_shape=out_shape,
            scratch_shapes=scratch_shapes, compiler_params=_cparams(("arbitrary",) * len(grid)))(*operands)
    n_in, n_out, n_scr = len(in_specs), len(out_specs), len(scratch_shapes)
    nci, nco = len(comm.inputs), len(comm.out_shapes)

    def wrapped(*refs):
        ins, cins = refs[:n_in], refs[n_in:n_in + nci]
        o0 = n_in + nci
        outs, couts = refs[o0:o0 + n_out], refs[o0 + n_out:o0 + n_out + nco]
        s0 = o0 + n_out + nco
        scr, csems = refs[s0:s0 + n_scr], refs[s0 + n_scr:]
        first = jnp.logical_and(pl.program_id(0) == 0, pl.program_id(1) == 0)
        last = jnp.logical_and(pl.program_id(0) == grid[0] - 1, pl.program_id(1) == grid[1] - 1)

        @pl.when(first)
        def _():
            comm.start(cins, couts, csems)

        body(*ins, *outs, *scr)

        @pl.when(last)
        def _():
            comm.wait(cins, couts, csems)

    res = pl.pallas_call(
        wrapped, name=name, grid=grid,
        in_specs=list(in_specs) + [HBM_SPEC] * nci, out_specs=list(out_specs) + [HBM_SPEC] * nco,
        out_shape=list(out_shape) + list(comm.out_shapes),
        scratch_shapes=list(scratch_shapes) + list(comm.sem_shapes),
        input_output_aliases={n_in + k: n_out + v for k, v in comm.aliases.items()},
        compiler_params=_cparams(("arbitrary",) * len(grid)))(*operands, *comm.inputs)
    return res


def _ssd_fwd(xbc, proj, dtb, a_log, d_skip_x, norm_g, e_mat, lp, name, comm=None):
    t = xbc.shape[0]
    cpe = lp // CHUNK
    nb = t // lp
    dtb_p = jnp.pad(dtb.reshape(1, N_HEADS), ((0, 0), (0, LANE - N_HEADS)))
    alog_p = jnp.pad(a_log.reshape(1, N_HEADS), ((0, 0), (0, LANE - N_HEADS)))
    dtb_t = dtb.reshape(N_HEADS, 1)
    alog_t = a_log.reshape(N_HEADS, 1)

    def body(xbc_ref, bz_ref, dt_ref, dtb_ref, dtbt_ref, al_ref, alt_ref, dx_ref, g_ref, e_ref,
             yb_ref, ys_ref, st_ref, s_scr):
        c = pl.program_id(1)

        @pl.when(c == 0)
        def _():
            s_scr[...] = jnp.zeros_like(s_scr)

        rows = lax.broadcasted_iota(jnp.int32, (CHUNK, 1), 0)
        cols = lax.broadcasted_iota(jnp.int32, (1, CHUNK), 1)
        valid_col = jnp.logical_or(c > 0, rows >= PAD)
        valid_row = jnp.logical_or(c > 0, cols >= PAD)
        e_mat = e_ref[...]
        _, _, big_a, big_a_t, _, _, dt_x, e_a_x, d_s_x, cd_x = _ssd_common(
            dt_ref[...], dtb_ref[...], dtbt_ref[...], al_ref[...], alt_ref[...], e_mat, valid_col, valid_row)
        xs = xbc_ref[:, 0:D_B]
        bs = xbc_ref[:, D_B:D_B + N_GROUPS * N_STATE].astype(BF16)
        cs = xbc_ref[:, D_B + N_GROUPS * N_STATE:N_XBC].astype(BF16)
        xdt = xs * dt_x
        xdt_b = xdt.astype(BF16)
        st_prev = s_scr[...]
        st_ref[...] = st_prev.astype(BF16)
        st_b = st_prev.astype(BF16)
        u_b = (xdt * d_s_x).astype(BF16)
        y_parts = []
        for g in range(N_GROUPS):
            gs = slice(g * N_STATE, (g + 1) * N_STATE)
            gw = slice(g * GROUP_W, (g + 1) * GROUP_W)
            cb = lax.dot_general(cs[:, gs], bs[:, gs], NT_DIMS, preferred_element_type=F32)
            y_off = jnp.dot(cs[:, gs], st_b[:, gw], preferred_element_type=F32)
            diag = []
            for e in range(N_HEADS // N_GROUPS):
                h = g * (N_HEADS // N_GROUPS) + e
                m = (cb * _decay(big_a, big_a_t, h, False)).astype(BF16)
                diag.append(jnp.dot(m, xdt_b[:, h * HEAD_DIM:(h + 1) * HEAD_DIM], preferred_element_type=F32))
            y_parts.append(jnp.concatenate(diag, axis=1) + y_off * e_a_x[:, gw])
            new_st = lax.dot_general(bs[:, gs], u_b[:, gw], TN_DIMS, preferred_element_type=F32)
            s_scr[:, gw] = st_prev[:, gw] * cd_x[:, gw] + new_st
        y = jnp.concatenate(y_parts, axis=1) + xs * dx_ref[...]
        ys_ref[...] = y
        z = y * _silu(bz_ref[...])
        r = lax.rsqrt(jnp.mean(z * z, axis=-1, keepdims=True) + NORM_EPS)
        yb_ref[...] = (z * r * g_ref[...]).astype(BF16)

    def row(width, col):
        return pl.BlockSpec((CHUNK, width), lambda b, c: (b * cpe + c, col))

    def const(shape):
        return pl.BlockSpec(shape, lambda b, c: (0,) * len(shape))

    return _grid_call(
        body, name, (nb, cpe),
        [row(N_XBC, 0), row(D_B, COL_BZ // D_B), row(LANE, COL_DT // LANE),
         const((1, LANE)), const((N_HEADS, 1)), const((1, LANE)), const((N_HEADS, 1)),
         const((1, D_B)), const((1, D_B)), const((LANE, D_B))],
        [row(D_B, 0), row(D_B, 0), pl.BlockSpec((None, N_STATE, D_B), lambda b, c: (b * cpe + c, 0, 0))],
        [jax.ShapeDtypeStruct((t, D_B), BF16), jax.ShapeDtypeStruct((t, D_B), F32),
         jax.ShapeDtypeStruct((nb * cpe, N_STATE, D_B), BF16)],
        [pltpu.VMEM((N_STATE, D_B), F32)],
        (xbc, proj, proj, dtb_p, dtb_t, alog_p, alog_t, d_skip_x, norm_g, e_mat), comm)


def _ssd_bwd(dy, y_ssd, xbc, proj, states, dtb, a_log, d_skip_x, norm_g, e_mat, e_mat_t, lp, name, comm=None):
    t = xbc.shape[0]
    cpe = lp // CHUNK
    nb = t // lp
    hpg = N_HEADS // N_GROUPS
    dtb_p = jnp.pad(dtb.reshape(1, N_HEADS), ((0, 0), (0, LANE - N_HEADS)))
    alog_p = jnp.pad(a_log.reshape(1, N_HEADS), ((0, 0), (0, LANE - N_HEADS)))
    dtb_t = dtb.reshape(N_HEADS, 1)
    alog_t = a_log.reshape(N_HEADS, 1)

    def body(dy_ref, ys_ref, xbc_ref, bz_ref, dt_ref, st_ref, dtb_ref, dtbt_ref, al_ref, alt_ref, dx_ref, g_ref,
             e_ref, et_ref, dxbc_ref, dbz_ref, ddt_ref, dg_ref, ddtb_ref, dal_ref, dd_ref, ds_scr):
        c = pl.program_id(1)
        cc = cpe - 1 - c

        @pl.when(c == 0)
        def _():
            ds_scr[...] = jnp.zeros_like(ds_scr)

        @pl.when(jnp.logical_and(pl.program_id(0) == 0, c == 0))
        def _():
            dg_ref[...] = jnp.zeros_like(dg_ref)
            ddtb_ref[...] = jnp.zeros_like(ddtb_ref)
            dal_ref[...] = jnp.zeros_like(dal_ref)
            dd_ref[...] = jnp.zeros_like(dd_ref)

        rows = lax.broadcasted_iota(jnp.int32, (CHUNK, 1), 0)
        cols = lax.broadcasted_iota(jnp.int32, (1, CHUNK), 1)
        valid_col = jnp.logical_or(cc > 0, rows >= PAD)
        valid_row = jnp.logical_or(cc > 0, cols >= PAD)
        e_mat = e_ref[...]
        e_mat_t = et_ref[...]
        dtraw = dt_ref[...]
        a, dt, big_a, big_a_t, e_a, d_s, dt_x, e_a_x, d_s_x, cd_x = _ssd_common(
            dtraw, dtb_ref[...], dtbt_ref[...], al_ref[...], alt_ref[...], e_mat, valid_col, valid_row)
        xs = xbc_ref[:, 0:D_B]
        bs = xbc_ref[:, D_B:D_B + N_GROUPS * N_STATE].astype(BF16)
        cs = xbc_ref[:, D_B + N_GROUPS * N_STATE:N_XBC].astype(BF16)
        xdt = xs * dt_x
        xdt_b = xdt.astype(BF16)
        st_b = st_ref[...]
        dst = ds_scr[...]
        dst_b = dst.astype(BF16)

        ys = ys_ref[...]
        bz = bz_ref[...]
        sil = _silu(bz)
        z = ys * sil
        dz, dgt = _rms_bwd_math(z, g_ref[...], dy_ref[...])
        dg_ref[...] += jnp.sum(dgt, axis=0, keepdims=True)
        dbz_ref[...] = (dz * ys * _dsilu(bz)).astype(BF16)
        dys = dz * sil

        dd_lane = jnp.sum(dys * xs, axis=0, keepdims=True)
        dxs = dys * dx_ref[...]
        w_x = dys * e_a_x
        w_b = w_x.astype(BF16)
        dys_b = dys.astype(BF16)
        u_b = (xdt * d_s_x).astype(BF16)
        dxdt_parts, dbs_parts, dcs_parts, off_parts, g1_parts = [], [], [], [], []
        da_diag = jnp.zeros((CHUNK, LANE), F32)
        lane = lax.broadcasted_iota(jnp.int32, (1, LANE), 1)
        for g in range(N_GROUPS):
            gs = slice(g * N_STATE, (g + 1) * N_STATE)
            gw = slice(g * GROUP_W, (g + 1) * GROUP_W)
            cs_g, bs_g = cs[:, gs], bs[:, gs]
            dcs = lax.dot_general(w_b[:, gw], st_b[:, gw], NT_DIMS, preferred_element_type=F32)
            y_off = jnp.dot(cs_g, st_b[:, gw], preferred_element_type=F32)
            off_parts.append(y_off)
            dst_new = lax.dot_general(cs_g, w_b[:, gw], TN_DIMS, preferred_element_type=F32)
            g1 = jnp.dot(bs_g, dst_b[:, gw], preferred_element_type=F32)
            g1_parts.append(g1)
            dbs = lax.dot_general(u_b[:, gw], dst_b[:, gw], NT_DIMS, preferred_element_type=F32)
            cb = lax.dot_general(cs_g, bs_g, NT_DIMS, preferred_element_type=F32)
            cbt = lax.dot_general(bs_g, cs_g, NT_DIMS, preferred_element_type=F32)
            dcb = jnp.zeros((CHUNK, CHUNK), F32)
            dcbt = jnp.zeros((CHUNK, CHUNK), F32)
            dxdt_h = []
            for e in range(hpg):
                h = g * hpg + e
                hs = slice(h * HEAD_DIM, (h + 1) * HEAD_DIM)
                dec = _decay(big_a, big_a_t, h, False)
                dect = _decay(big_a, big_a_t, h, True)
                m = cb * dec
                mt = cbt * dect
                dxdt_h.append(jnp.dot(mt.astype(BF16), dys_b[:, hs], preferred_element_type=F32))
                dm = lax.dot_general(dys_b[:, hs], xdt_b[:, hs], NT_DIMS, preferred_element_type=F32)
                dmt = lax.dot_general(xdt_b[:, hs], dys_b[:, hs], NT_DIMS, preferred_element_type=F32)
                dcb = dcb + dm * dec
                dcbt = dcbt + dmt * dect
                da_h = jnp.sum(dm * m - dmt * mt, axis=1, keepdims=True)
                da_diag = da_diag + jnp.where(lane == h, da_h, 0.0)
            dcs = dcs + jnp.dot(dcb.astype(BF16), bs_g, preferred_element_type=F32)
            dbs = dbs + jnp.dot(dcbt.astype(BF16), cs_g, preferred_element_type=F32)
            dxdt_parts.append(jnp.concatenate(dxdt_h, axis=1) + g1 * d_s_x[:, gw])
            dbs_parts.append(dbs)
            dcs_parts.append(dcs)
            ds_scr[:, gw] = dst[:, gw] * cd_x[:, gw] + dst_new
        dxdt = jnp.concatenate(dxdt_parts, axis=1)
        y_off = jnp.concatenate(off_parts, axis=1)
        g1 = jnp.concatenate(g1_parts, axis=1)
        dds = _split_dot(g1 * xdt, e_mat_t, 2)
        da_off = _split_dot(w_x * y_off, e_mat_t, 2)
        ddt_x = _split_dot(dxdt * xs, e_mat_t, 2)
        dcd_lane = jnp.sum(dst * st_b.astype(F32), axis=0, keepdims=True)
        dcd = _split_dot(jnp.broadcast_to(dcd_lane, (SUB, D_B)), e_mat_t, 2)[0:1]
        dd_ref[...] += _split_dot(jnp.broadcast_to(dd_lane, (SUB, D_B)), e_mat_t, 2)[0:1]
        t_ds = dds * d_s
        d_a = da_diag + da_off - t_ds
        last_row = jnp.sum(t_ds, axis=0, keepdims=True) + dcd * e_a[CHUNK - 1:CHUNK, :]
        d_a = d_a + jnp.where(rows == CHUNK - 1, last_row, 0.0)
        dda = _split_dot_left(_tri(False).astype(BF16), d_a, 3)
        ddt = dda * a + ddt_x
        dal_ref[...] += jnp.sum(dda * dt * a, axis=0, keepdims=True)
        ddtraw = jnp.where(valid_col, ddt * jax.nn.sigmoid(dtraw + dtb_ref[...]), 0.0)
        ddtraw = jnp.where(lane < N_HEADS, ddtraw, 0.0)
        ddtb_ref[...] += jnp.sum(ddtraw, axis=0, keepdims=True)
        ddt_ref[...] = ddtraw.astype(BF16)
        dxs = dxs + dxdt * dt_x
        dxbc = jnp.concatenate([dxs] + dbs_parts + dcs_parts, axis=1)
        dxbc_ref[...] = jnp.where(valid_col, dxbc, 0.0)

    def row(width, col):
        return pl.BlockSpec((CHUNK, width), lambda b, c: (b * cpe + cpe - 1 - c, col))

    def const(shape):
        return pl.BlockSpec(shape, lambda b, c: (0,) * len(shape))

    return _grid_call(
        body, name, (nb, cpe),
        [row(D_B, 0), row(D_B, 0), row(N_XBC, 0), row(D_B, COL_BZ // D_B), row(LANE, COL_DT // LANE),
         pl.BlockSpec((None, N_STATE, D_B), lambda b, c: (b * cpe + cpe - 1 - c, 0, 0)),
         const((1, LANE)), const((N_HEADS, 1)), const((1, LANE)), const((N_HEADS, 1)),
         const((1, D_B)), const((1, D_B)), const((LANE, D_B)), const((D_B, LANE))],
        [row(N_XBC, 0), row(D_B, 0), row(LANE, 0),
         const((1, D_B)), const((1, LANE)), const((1, LANE)), const((1, LANE))],
        [jax.ShapeDtypeStruct((t, N_XBC), F32), jax.ShapeDtypeStruct((t, D_B), BF16),
         jax.ShapeDtypeStruct((t, LANE), BF16), jax.ShapeDtypeStruct((1, D_B), F32),
         jax.ShapeDtypeStruct((1, LANE), F32), jax.ShapeDtypeStruct((1, LANE), F32),
         jax.ShapeDtypeStruct((1, LANE), F32)],
        [pltpu.VMEM((N_STATE, D_B), F32)],
        (dy, y_ssd, xbc, proj, proj, states, dtb_p, dtb_t, alog_p, alog_t, d_skip_x, norm_g, e_mat, e_mat_t), comm)


HBM_SPEC = pl.BlockSpec(memory_space=pl.ANY)


def _mesh_pos():
    return lax.axis_index("x"), lax.axis_index("y"), lax.axis_index("c")


def _allgather(arrays, name):
    n = len(arrays)

    def body(*refs):
        xs, outs = refs[:n], refs[n:2 * n]
        send_sems, recv_sems, local_sems = refs[2 * n:]
        x, y, c = _mesh_pos()
        me, sibling = (x, y, c), (x, y, 1 - c)
        chips = [(1 - x, y), (x, 1 - y), (1 - x, 1 - y)]

        def slot(px, py, pc):
            return 4 * px + 2 * py + pc

        def copy(a, k, block, to, src=None):
            dst = outs[a].at[slot(*block)]
            return pltpu.make_async_remote_copy(
                src_ref=dst if src is None else src, dst_ref=dst, send_sem=send_sems.at[a, k], recv_sem=recv_sems.at[a, k],
                device_id=to, device_id_type=MESH)

        mine = [pltpu.make_async_copy(xs[a], outs[a].at[slot(*me)], local_sems.at[a]) for a in range(n)]
        for cp in mine:
            cp.start()
        first = []
        for a in range(n):
            first.append(copy(a, 0, me, sibling, src=xs[a]))
            first += [copy(a, 1 + j, me, (*chip, c), src=xs[a]) for j, chip in enumerate(chips)]
        for cp in first:
            cp.start()
        passed = []
        for j, chip in enumerate(chips):
            for a in range(n):
                copy(a, 1 + j, (*chip, c), me).wait_recv()
                cp = copy(a, 4 + j, (*chip, c), sibling)
                cp.start()
                passed.append(cp)
        for a in range(n):
            copy(a, 0, sibling, me).wait_recv()
        for j, chip in enumerate(chips):
            for a in range(n):
                copy(a, 4 + j, (*chip, 1 - c), me).wait_recv()
        for cp in first + passed:
            cp.wait_send()
        for cp in mine:
            cp.wait()

    return pl.pallas_call(
        body, name=name,
        in_specs=[HBM_SPEC] * n, out_specs=[HBM_SPEC] * n,
        out_shape=[jax.ShapeDtypeStruct((N_DEV,) + a.shape, a.dtype) for a in arrays],
        scratch_shapes=[pltpu.SemaphoreType.DMA((n, 7)), pltpu.SemaphoreType.DMA((n, 7)), pltpu.SemaphoreType.DMA((n,))],
    )(*arrays)


def _remote(src, dst, send_sem, recv_sem, to):
    return pltpu.make_async_remote_copy(src_ref=src, dst_ref=dst, send_sem=send_sem, recv_sem=recv_sem,
                                        device_id=to, device_id_type=MESH)


def _slot_plain(d):
    return d


def _slot_mix_rows(d):
    return jnp.where(d < 2, d + 4, jnp.where(d < 6, d - 2, d))


def _ag_ici(pieces, slots):
    n = len(pieces)

    def copies(ins, outs, sems, with_recvs):
        send_sems, recv_sems, local_sems = sems
        x, y, c = _mesh_pos()
        local, sends, recvs = [], [], []
        for a in range(n):
            mine = outs[a].at[slots[a](4 * x + 2 * y + c)]
            local.append(pltpu.make_async_copy(ins[a], mine, local_sems.at[a]))
            for k, (px, py) in enumerate([(1 - x, y), (x, 1 - y), (1 - x, 1 - y)]):
                sends.append(_remote(ins[a], mine, send_sems.at[a, k], recv_sems.at[a, k], (px, py, c)))
                if with_recvs:
                    theirs = outs[a].at[slots[a](4 * px + 2 * py + c)]
                    recvs.append(_remote(ins[a], theirs, send_sems.at[a, k], recv_sems.at[a, k], (px, py, c)))
        return local, sends, recvs

    return _Comm(pieces, [jax.ShapeDtypeStruct((N_DEV,) + p.shape, p.dtype) for p in pieces],
                 [pltpu.SemaphoreType.DMA((n, 3)), pltpu.SemaphoreType.DMA((n, 3)), pltpu.SemaphoreType.DMA((n,))], copies)


def _ag_d2d(bufs, slots):
    n = len(bufs)

    def copies(ins, outs, sems, with_recvs):
        send_sems, recv_sems = sems
        x, y, c = _mesh_pos()
        chips = [(x, y), (1 - x, y), (x, 1 - y), (1 - x, 1 - y)]
        sends, recvs = [], []
        for a in range(n):
            for k, (px, py) in enumerate(chips):
                held = slots[a](4 * px + 2 * py + c)
                sends.append(_remote(ins[a].at[held], outs[a].at[held], send_sems.at[a, k], recv_sems.at[a, k], (x, y, 1 - c)))
                if with_recvs:
                    got = slots[a](4 * px + 2 * py + 1 - c)
                    recvs.append(_remote(ins[a].at[got], outs[a].at[got], send_sems.at[a, k], recv_sems.at[a, k], (x, y, 1 - c)))
        return [], sends, recvs

    return _Comm(bufs, [jax.ShapeDtypeStruct(b.shape, b.dtype) for b in bufs],
                 [pltpu.SemaphoreType.DMA((n, 4)), pltpu.SemaphoreType.DMA((n, 4))], copies,
                 aliases={a: a for a in range(n)})


def _rs_d2d(blocks, slots):
    n = len(blocks)

    def copies(ins, outs, sems, with_recvs):
        send_sems, recv_sems = sems
        x, y, c = _mesh_pos()
        sends, recvs = [], []
        for a in range(n):
            for j in range(4):
                src = ins[a].at[slots[a](2 * j + 1 - c)]
                sends.append(_remote(src, outs[a].at[j], send_sems.at[a, j], recv_sems.at[a, j], (x, y, 1 - c)))
                if with_recvs:
                    recvs.append(_remote(src, outs[a].at[j], send_sems.at[a, j], recv_sems.at[a, j], (x, y, 1 - c)))
        return [], sends, recvs

    return _Comm(blocks, [jax.ShapeDtypeStruct((4,) + b.shape[1:], b.dtype) for b in blocks],
                 [pltpu.SemaphoreType.DMA((n, 4)), pltpu.SemaphoreType.DMA((n, 4))], copies)


def _rs_ici(blocks):
    n = len(blocks)

    def copies(ins, outs, sems, with_recvs):
        send_sems, recv_sems = sems
        x, y, c = _mesh_pos()
        sends, recvs = [], []
        for a in range(n):
            for k, (px, py) in enumerate([(1 - x, y), (x, 1 - y), (1 - x, 1 - y)]):
                src = ins[a].at[2 * px + py]
                sends.append(_remote(src, outs[a].at[k], send_sems.at[a, k], recv_sems.at[a, k], (px, py, c)))
                if with_recvs:
                    recvs.append(_remote(src, outs[a].at[k], send_sems.at[a, k], recv_sems.at[a, k], (px, py, c)))
        return [], sends, recvs

    return _Comm(blocks, [jax.ShapeDtypeStruct((3,) + b.shape[1:], b.dtype) for b in blocks],
                 [pltpu.SemaphoreType.DMA((n, 3)), pltpu.SemaphoreType.DMA((n, 3))], copies)


def _run_comm(comm, name):
    n_in, n_out = len(comm.inputs), len(comm.out_shapes)

    def body(*refs):
        ins, outs, sems = refs[:n_in], refs[n_in:n_in + n_out], refs[n_in + n_out:]
        comm.start(ins, outs, sems)
        comm.wait(ins, outs, sems)

    return pl.pallas_call(
        body, name=name, in_specs=[HBM_SPEC] * n_in, out_specs=[HBM_SPEC] * n_out, out_shape=comm.out_shapes,
        scratch_shapes=comm.sem_shapes, input_output_aliases=comm.aliases,
    )(*comm.inputs)


W_ROWS = 784


def _w_segments():
    per = N_IN // N_DEV
    out = []
    for d in range(N_DEV):
        lo, hi = per * d, per * (d + 1)
        for a, b, off in ((0, COL_C, 0), (COL_C, COL_C + N_HEADS, COL_DT - COL_C), (COL_C + N_HEADS, N_IN, -N_HEADS)):
            s, e = max(lo, a), min(hi, b)
            if s < e:
                out.append((d, s - lo, e - s, s + off))
    return out


def _w_gather_rows(g, name):
    tk = D_MODEL // 2
    u32 = jnp.uint32

    def body(g_ref, o_ref, scr):
        for d in range(N_DEV):
            x32 = pltpu.bitcast(g_ref[d], u32)
            for dd, src, rows, dst in _w_segments():
                if dd == d:
                    scr[dst // 2:(dst + rows) // 2, :] = x32[src // 2:(src + rows) // 2]
        scr[N_IN // 2:N_INP // 2, :] = jnp.zeros(((N_INP - N_IN) // 2, tk), u32)
        o_ref[...] = pltpu.bitcast(scr[...], BF16)

    return pl.pallas_call(
        body, name=name, grid=(D_MODEL // tk,),
        in_specs=[pl.BlockSpec((N_DEV, W_ROWS, tk), lambda j: (0, 0, j))],
        out_specs=pl.BlockSpec((N_INP, tk), lambda j: (0, j)),
        out_shape=jax.ShapeDtypeStruct((N_INP, D_MODEL), BF16),
        scratch_shapes=[pltpu.VMEM((N_INP // 2, tk), u32)],
        compiler_params=_cparams(("parallel",)),
    )(g)


def _w_split_rows(dwt, name):
    tk = D_MODEL // 4
    per = N_IN // N_DEV

    def body(w_ref, o_ref):
        for d, dst, rows, src in _w_segments():
            o_ref[d, dst:dst + rows, :] = w_ref[src:src + rows, :]
        for d in range(N_DEV):
            o_ref[d, per:W_ROWS, :] = jnp.zeros((W_ROWS - per, tk), F32)

    return pl.pallas_call(
        body, name=name, grid=(D_MODEL // tk,),
        in_specs=[pl.BlockSpec((N_INP, tk), lambda j: (0, j))],
        out_specs=pl.BlockSpec((N_DEV, W_ROWS, tk), lambda j: (0, 0, j)),
        out_shape=jax.ShapeDtypeStruct((N_DEV, W_ROWS, D_MODEL), F32),
        compiler_params=_cparams(("parallel",)),
    )(dwt)


def _rs_pair_sum(g, ra, own_slots, name):
    _, rows, cols = g.shape

    def body(s_ref, g_ref, ra_ref, o_ref):
        o_ref[...] = (g_ref[...] + ra_ref[...]).astype(BF16)

    return pl.pallas_call(
        body, name=name,
        grid_spec=pltpu.PrefetchScalarGridSpec(
            num_scalar_prefetch=1, grid=(4,),
            in_specs=[pl.BlockSpec((None, rows, cols), lambda j, s: (s[j], 0, 0)),
                      pl.BlockSpec((None, rows, cols), lambda j, s: (j, 0, 0))],
            out_specs=pl.BlockSpec((None, rows, cols), lambda j, s: (j, 0, 0))),
        out_shape=jax.ShapeDtypeStruct((4, rows, cols), BF16),
        compiler_params=_cparams(("parallel",)),
    )(own_slots, g, ra)


def _rs_final_sum(h, rb, chip_idx, name):
    _, rows, cols = h.shape

    def body(j_ref, h_ref, rb_ref, o_ref):
        o_ref[...] = ((h_ref[...].astype(F32) + rb_ref[0].astype(F32)) + rb_ref[1].astype(F32)) + rb_ref[2].astype(F32)

    return pl.pallas_call(
        body, name=name,
        grid_spec=pltpu.PrefetchScalarGridSpec(
            num_scalar_prefetch=1, grid=(1,),
            in_specs=[pl.BlockSpec((None, rows, cols), lambda i, j: (j[0], 0, 0)),
                      pl.BlockSpec((3, rows, cols), lambda i, j: (0, 0, 0))],
            out_specs=pl.BlockSpec((rows, cols), lambda i, j: (0, 0))),
        out_shape=jax.ShapeDtypeStruct((rows, cols), F32),
        compiler_params=_cparams(("arbitrary",)),
    )(chip_idx, h, rb)


def _adamw_math(w, g, m, v):
    m = ADAM_B1 * m + (1.0 - ADAM_B1) * g
    v = ADAM_B2 * v + (1.0 - ADAM_B2) * (g * g)
    m_hat = m / (1.0 - ADAM_B1 ** ADAM_STEP)
    v_hat = v / (1.0 - ADAM_B2 ** ADAM_STEP)
    delta = -ADAM_LR * (m_hat / (jnp.sqrt(v_hat) + ADAM_EPS) + ADAM_WD * w)
    return delta, m, v


def _adamw_rows(g, w, m, v, tr, name):
    rows, cols = w.shape

    def body(g_ref, w_ref, m_ref, v_ref, d_out, m_out, v_out):
        delta, m_new, v_new = _adamw_math(w_ref[...], g_ref[...], m_ref[...], v_ref[...])
        d_out[...] = delta
        m_out[...] = m_new
        v_out[...] = v_new

    flat = pl.BlockSpec((tr, cols), lambda r: (r, 0))
    return pl.pallas_call(
        body, name=name, grid=(rows // tr,),
        in_specs=[flat] * 4, out_specs=[flat] * 3,
        out_shape=[jax.ShapeDtypeStruct((rows, cols), F32)] * 3,
        compiler_params=_cparams(("parallel",)),
    )(g, w, m, v)


def _sum_devices(parts, name):
    _, p, _ = parts.shape

    def body(x_ref, o_ref):
        acc = x_ref[0]
        for d in range(1, N_DEV):
            acc = acc + x_ref[d]
        o_ref[...] = acc

    return pl.pallas_call(
        body, name=name, grid=(1,),
        in_specs=[pl.BlockSpec((N_DEV, p, LANE), lambda i: (0, 0, 0))],
        out_specs=pl.BlockSpec((p, LANE), lambda i: (0, 0)),
        out_shape=jax.ShapeDtypeStruct((p, LANE), F32),
        compiler_params=_cparams(("arbitrary",)),
    )(parts)


def _adamw_small(g, w, m, v, name):
    p = g.shape[0]

    def body(g_ref, w_ref, m_ref, v_ref, d_out, m_out, v_out):
        delta, m_new, v_new = _adamw_math(w_ref[...], g_ref[...], m_ref[...], v_ref[...])
        d_out[...] = delta
        m_out[...] = m_new
        v_out[...] = v_new

    spec = pl.BlockSpec((p, LANE), lambda i: (0, 0))
    return pl.pallas_call(
        body, name=name, grid=(1,),
        in_specs=[spec] * 4, out_specs=[spec] * 3,
        out_shape=[jax.ShapeDtypeStruct((p, LANE), F32)] * 3,
        compiler_params=_cparams(("arbitrary",)),
    )(g, w, m, v)


PACK_ALIGN = SUB * LANE

SMALL_PARAMS = (
    ("meta", (N_META, D_MODEL), 1),
    ("pre_g", (DEPTH, D_MODEL), None),
    ("post_g", (DEPTH, D_MODEL), None),
    ("conv_a_w", (DEPTH, CONV_A_K, D_A), 2),
    ("ssm_conv_w", (DEPTH, SSM_K, N_XBC), 2),
    ("ssm_conv_b", (DEPTH, N_XBC), None),
    ("dt_bias", (DEPTH, N_HEADS), None),
    ("a_log", (DEPTH, N_HEADS), None),
    ("d_skip", (DEPTH, N_HEADS), None),
    ("ssm_norm_g", (DEPTH, D_B), None),
    ("conf_conv_w", (DEPTH, CONF_K, D_C), 2),
    ("conf_conv_b", (DEPTH, D_C), None),
    ("conf_ln_g", (DEPTH, D_C), None),
    ("conf_ln_b", (DEPTH, D_C), None),
)


def _local_shape(shape, axis):
    if axis is None:
        return shape
    return tuple(s // N_DEV if k == axis else s for k, s in enumerate(shape))


def _pack(arrays):
    flat = []
    for a in arrays:
        v = a.reshape(-1).astype(F32)
        flat.append(v)
        if v.shape[0] % PACK_ALIGN:
            flat.append(jnp.zeros(((-v.shape[0]) % PACK_ALIGN,), F32))
    return jnp.concatenate(flat).reshape(-1, LANE)


def _unpack(buf, shapes):
    flat = buf.reshape(-1)
    out, off = [], 0
    for s in shapes:
        size = 1
        for k in s:
            size *= k
        out.append(flat[off:off + size].reshape(s))
        off += size + (-size) % PACK_ALIGN
    return out


def _perm_cols(w):
    pad = jnp.zeros(w.shape[:-1] + (N_INP - N_IN,), w.dtype)
    return jnp.concatenate([w[..., :COL_C], w[..., COL_C + N_HEADS:N_IN], w[..., COL_C:COL_C + N_HEADS], pad], axis=-1)


def _unperm_cols(w):
    return jnp.concatenate([w[..., :COL_C], w[..., COL_DT:COL_DT + N_HEADS], w[..., COL_C:COL_DT]], axis=-1)


def kernel(x, meta, pre_g, post_g, w_in, w_out, conv_a_w, ssm_conv_w, ssm_conv_b, dt_bias, a_log, d_skip, ssm_norm_g, conf_conv_w, conf_conv_b, conf_ln_g, conf_ln_b, loss_target, m_meta, m_pre_g, m_post_g, m_w_in, m_w_out, m_conv_a_w, m_ssm_conv_w, m_ssm_conv_b, m_dt_bias, m_a_log, m_d_skip, m_ssm_norm_g, m_conf_conv_w, m_conf_conv_b, m_conf_ln_g, m_conf_ln_b, v_meta, v_pre_g, v_post_g, v_w_in, v_w_out, v_conv_a_w, v_ssm_conv_w, v_ssm_conv_b, v_dt_bias, v_a_log, v_d_skip, v_ssm_norm_g, v_conf_conv_w, v_conf_conv_b, v_conf_ln_g, v_conf_ln_b):
    weights = dict(meta=meta, pre_g=pre_g, post_g=post_g, conv_a_w=conv_a_w, ssm_conv_w=ssm_conv_w, ssm_conv_b=ssm_conv_b,
                   dt_bias=dt_bias, a_log=a_log, d_skip=d_skip, ssm_norm_g=ssm_norm_g, conf_conv_w=conf_conv_w,
                   conf_conv_b=conf_conv_b, conf_ln_g=conf_ln_g, conf_ln_b=conf_ln_b)
    mom1 = dict(meta=m_meta, pre_g=m_pre_g, post_g=m_post_g, conv_a_w=m_conv_a_w, ssm_conv_w=m_ssm_conv_w,
                ssm_conv_b=m_ssm_conv_b, dt_bias=m_dt_bias, a_log=m_a_log, d_skip=m_d_skip, ssm_norm_g=m_ssm_norm_g,
                conf_conv_w=m_conf_conv_w, conf_conv_b=m_conf_conv_b, conf_ln_g=m_conf_ln_g, conf_ln_b=m_conf_ln_b)
    mom2 = dict(meta=v_meta, pre_g=v_pre_g, post_g=v_post_g, conv_a_w=v_conv_a_w, ssm_conv_w=v_ssm_conv_w,
                ssm_conv_b=v_ssm_conv_b, dt_bias=v_dt_bias, a_log=v_a_log, d_skip=v_d_skip, ssm_norm_g=v_ssm_norm_g,
                conf_conv_w=v_conf_conv_w, conf_conv_b=v_conf_conv_b, conf_ln_g=v_conf_ln_g, conf_ln_b=v_conf_ln_b)
    nb, seq, d = x.shape
    lp = PAD + N_META + seq
    t = nb * lp
    assert lp % TILE == 0 and t % (3 * LANE) == 0 and d == D_MODEL
    xi, yi, ci = _mesh_pos()
    dev = 4 * xi + 2 * yi + ci
    ci32 = ci.astype(jnp.int32)
    chip_idx = (2 * xi + yi).astype(jnp.int32).reshape(1)
    own_plain = jnp.stack([2 * j + ci32 for j in range(4)])
    own_mix = jnp.stack([_slot_mix_rows(2 * j + ci32) for j in range(4)]).astype(jnp.int32)
    n_in_loc = N_IN // N_DEV
    n_out_loc = 2 * D_MODEL // N_DEV
    slots = (_slot_plain, _slot_mix_rows)

    sharded_small = [n for n, _, ax in SMALL_PARAMS if ax is not None]
    small_shapes = {n: s for n, s, _ in SMALL_PARAMS}
    small_axis = {n: ax for n, _, ax in SMALL_PARAMS}
    sw_pack = _pack([weights[n] for n in sharded_small])
    (sw_g,) = _allgather([sw_pack], "ag_small_weights")
    wt_loc = jnp.pad(jnp.swapaxes(w_in, 1, 2).astype(BF16), ((0, 0), (0, W_ROWS - n_in_loc), (0, 0)))
    wo_loc = w_out.astype(BF16)

    def gather_finish(bufs):
        g_in_t, g_out_rows = _run_comm(_ag_d2d(list(bufs), slots), "ag_d2d")
        return _w_gather_rows(g_in_t, "w_gather_rows"), g_out_rows.reshape(2 * D_MODEL, D_MODEL)

    full = dict(weights)
    per_dev = [_unpack(sw_g[k], [_local_shape(small_shapes[n], small_axis[n]) for n in sharded_small]) for k in range(N_DEV)]
    for q, n in enumerate(sharded_small):
        full[n] = jnp.concatenate([per_dev[k][q] for k in range(N_DEV)], axis=small_axis[n])

    e_mat = (lax.broadcasted_iota(jnp.int32, (LANE, D_B), 0) == lax.broadcasted_iota(jnp.int32, (LANE, D_B), 1) // HEAD_DIM)
    e_mat = e_mat.astype(BF16)
    e_mat_t = e_mat.T

    front = jnp.concatenate([jnp.zeros((PAD, d), F32), full["meta"]], axis=0)
    h = jnp.concatenate([jnp.concatenate([front, x[b]], axis=0) for b in range(nb)], axis=0)
    saved = []
    w_t, w_o = gather_finish(_run_comm(_ag_ici([wt_loc[0], wo_loc[0]], slots), "ag_ici"))
    for i in range(DEPTH):
        row = lambda a: a[i].reshape(1, -1)
        hn = _rms_fwd(h, row(pre_g), "rms_fwd")
        proj = _mm(hn, w_t, "nt", F32, t // 3, MM_TN, D_MODEL, "mm_proj", cols_outer=True)
        ya = _a_fwd(proj, full["conv_a_w"][i], lp, "a_fwd")
        xbc = _xbc_fwd(proj, full["ssm_conv_w"][i], row(ssm_conv_b), lp, "xbc_fwd")
        yc = _c_fwd(proj, full["conf_conv_w"][i], row(conf_conv_b), row(conf_ln_g), row(conf_ln_b), lp, "c_fwd")
        d_skip_x = jnp.repeat(d_skip[i], HEAD_DIM).reshape(1, D_B)
        nxt = _ag_ici([wt_loc[i + 1], wo_loc[i + 1]], slots) if i + 1 < DEPTH else None
        yb, yssd, states, *bufs = _ssd_fwd(xbc, proj, dt_bias[i], a_log[i], d_skip_x, row(ssm_norm_g), e_mat, lp,
                                           "ssd_fwd", comm=nxt)
        ycat = jnp.concatenate([yb, ya, yc], axis=1)
        m = _mm(ycat, w_o, "nn", F32, t // 3, D_MODEL, 2 * D_MODEL, "mm_out")
        saved.append((h, hn, proj, ycat, yssd, states, xbc, m, d_skip_x, w_t, w_o))
        h = _post_fwd(h, m, row(post_g), "post_fwd")
        if bufs:
            w_t, w_o = gather_finish(bufs)

    dh, loss_blk = _loss_kernel(h, loss_target.reshape(nb * seq, d), lp, "loss")

    grads = {n: [None] * DEPTH for n, _, _ in SMALL_PARAMS if n != "meta"}
    gt_in, gr_out = [None] * DEPTH, [None] * DEPTH
    dmeta = None
    pending = None

    def rs_pair_sums(blocks):
        ra_in, ra_out = _run_comm(_rs_d2d(list(blocks), slots), "rs_d2d")
        return [_rs_pair_sum(blocks[0], ra_in, own_plain, "rs_pair_sum_in"),
                _rs_pair_sum(blocks[1], ra_out, own_mix, "rs_pair_sum_out")]

    def rs_finish(layer, hs, rbs):
        gt_in[layer] = _rs_final_sum(hs[0], rbs[0], chip_idx, "rs_final_sum_in")
        gr_out[layer] = _rs_final_sum(hs[1], rbs[1], chip_idx, "rs_final_sum_out")

    for i in reversed(range(DEPTH)):
        row = lambda a: a[i].reshape(1, -1)
        h_i, hn, proj, ycat, yssd, states, xbc, m, d_skip_x, w_t, w_o = saved[i]
        dm, grads["post_g"][i] = _post_bwd(dh, m, row(post_g), "post_bwd")
        dy = _mm(dm, w_o, "nt", F32, t // 3, D_MODEL, D_MODEL, "mm_dy", cols_outer=True)
        dw_out = _mm(ycat, dm, "tn", F32, D_MODEL, D_MODEL, t // 3, "mm_dwout")
        hs = rs_pair_sums(pending) if pending is not None else None
        dxbc, dbz, ddt, grads["ssm_norm_g"][i], ddtb, dal, dds, *rbs = _ssd_bwd(
            dy, yssd, xbc, proj, states, dt_bias[i], a_log[i], d_skip_x, row(ssm_norm_g), e_mat, e_mat_t, lp, "ssd_bwd",
            comm=_rs_ici(hs) if hs is not None else None)
        if hs is not None:
            rs_finish(i + 1, hs, rbs)
        grads["dt_bias"][i] = ddtb[:, :N_HEADS]
        grads["a_log"][i] = dal[:, :N_HEADS]
        grads["d_skip"][i] = dds[:, :N_HEADS]
        dpa, grads["conv_a_w"][i] = _a_bwd(dy, proj, full["conv_a_w"][i], lp, 2, "a_bwd")
        dpx, grads["ssm_conv_w"][i], grads["ssm_conv_b"][i] = _xbc_bwd(
            dxbc, proj, full["ssm_conv_w"][i], row(ssm_conv_b), lp, "xbc_bwd")
        dpc, grads["conf_conv_w"][i], grads["conf_conv_b"][i], grads["conf_ln_g"][i], grads["conf_ln_b"][i] = _c_bwd(
            dy, proj, full["conf_conv_w"][i], row(conf_conv_b), row(conf_ln_g), row(conf_ln_b), lp, 3, "c_bwd")
        dproj = jnp.concatenate([dpa, dbz, dpx, dpc, ddt], axis=1)
        dhn = _mm(dproj, w_t, "nn", F32, MM_TM, D_MODEL, N_INP, "mm_dhn")
        dw_in_t = _mm(dproj, hn, "tn", F32, MM_TN, D_MODEL, t, "mm_dwin")
        dh, grads["pre_g"][i], dmeta = _rms_bwd(dh, dhn, h_i, row(pre_g), lp, "rms_bwd")
        pending = (_w_split_rows(dw_in_t, "w_split_rows"), dw_out.reshape(N_DEV, n_out_loc, D_MODEL))
    grad_x = dh.reshape(nb, lp, d)[:, PAD + N_META:]
    hs = rs_pair_sums(pending)
    rs_finish(0, hs, _run_comm(_rs_ici(hs), "rs_ici"))

    g_w_in = jnp.swapaxes(jnp.stack(gt_in)[:, :n_in_loc, :], 1, 2).reshape(DEPTH * D_MODEL, n_in_loc)
    g_w_out = jnp.stack(gr_out).reshape(DEPTH * n_out_loc, D_MODEL)
    big = {}
    big["w_in"] = [g_w_in, *_adamw_rows(g_w_in, w_in.reshape(-1, n_in_loc), m_w_in.reshape(-1, n_in_loc),
                                        v_w_in.reshape(-1, n_in_loc), 512, "adamw_w_in")]
    big["w_out"] = [g_w_out, *_adamw_rows(g_w_out, w_out.reshape(-1, D_MODEL), m_w_out.reshape(-1, D_MODEL),
                                          v_w_out.reshape(-1, D_MODEL), 256, "adamw_w_out")]
    big = {n: [a.reshape(s) for a in outs] for (n, outs), s in zip(big.items(), (w_in.shape, w_out.shape))}

    names = [n for n, _, _ in SMALL_PARAMS]
    partial = [loss_blk[0:1, 0:1], dmeta] + [jnp.concatenate(grads[n], axis=0) for n in names[1:]]
    part_pack = _pack(partial)
    (parts_g,) = _allgather([part_pack], "ag_small_grads")
    total = _unpack(_sum_devices(parts_g, "sum_small_grads"), [(1,)] + [small_shapes[n] for n in names])
    loss = total[0][0]
    g_small = {}
    for n, g in zip(names, total[1:]):
        ax = small_axis[n]
        if ax is not None:
            g = lax.dynamic_slice_in_dim(g, dev * (small_shapes[n][ax] // N_DEV), small_shapes[n][ax] // N_DEV, axis=ax)
        g_small[n] = g
    loc_shapes = [_local_shape(small_shapes[n], small_axis[n]) for n in names]
    d_pack, m_pack, v_pack = _adamw_small(_pack([g_small[n] for n in names]), _pack([weights[n] for n in names]),
                                          _pack([mom1[n] for n in names]), _pack([mom2[n] for n in names]), "adamw_small")
    d_small = dict(zip(names, _unpack(d_pack, loc_shapes)))
    m_small = dict(zip(names, _unpack(m_pack, loc_shapes)))
    v_small = dict(zip(names, _unpack(v_pack, loc_shapes)))

    order = ["meta", "pre_g", "post_g", "w_in", "w_out", "conv_a_w", "ssm_conv_w", "ssm_conv_b", "dt_bias", "a_log",
             "d_skip", "ssm_norm_g", "conf_conv_w", "conf_conv_b", "conf_ln_g", "conf_ln_b"]

    def pick(k, small):
        return [big[n][k] if n in big else small[n] for n in order]

    return (loss, grad_x, *pick(0, g_small), *pick(1, d_small), *pick(2, m_small), *pick(3, v_small))
```

```python
import functools

import jax
import jax.numpy as jnp
from jax import lax
from jax.experimental import pallas as pl
from jax.experimental.pallas import tpu as pltpu

F32 = jnp.float32
BF16 = jnp.bfloat16

D_MODEL = 1024
DEPTH = 4
SEQ = 2048
CHUNK = 64
N_META = 16
PAD = 48
LP = PAD + N_META + SEQ
D_A = 512
D_B = 1024
D_C = 512
N_HEADS = 16
HEAD_DIM = 64
N_STATE = 128
N_GROUPS = 2
GROUP_W = D_B // N_GROUPS
N_XBC = D_B + 2 * N_GROUPS * N_STATE
CONV_A_K = 3
SSM_K = 4
CONF_K = 31
NORM_EPS = 1e-6
LN_EPS = 1e-5
N_IN = 6160
N_INP = 6272
COL_BZ = 2048
COL_DT = 3072
COL_XBC = 3200
COL_C = 4736
COL_MAP = ((0, 3072, 0), (3072, 4608, COL_XBC), (4608, 4624, COL_DT), (4624, 6160, COL_C))
LANE = 128
SUB = 8

ADAM_LR = 0.001
ADAM_B1 = 0.9
ADAM_B2 = 0.999
ADAM_EPS = 1e-08
ADAM_WD = 0.01
ADAM_STEP = 10

TILE = 192
HALO_A = 16
HALO_C = 32
MM_TM = 384
MM_TN = 896
VMEM_LIMIT = 56 * 1024 * 1024

MESH = pl.DeviceIdType.MESH
N_DEV = 8


def _silu(x):
    return x * jax.nn.sigmoid(x)


def _dsilu(x):
    s = jax.nn.sigmoid(x)
    return s * (1.0 + x * (1.0 - s))


def _cparams(sem=None):
    return pltpu.CompilerParams(dimension_semantics=sem, vmem_limit_bytes=VMEM_LIMIT)


def _mm(a, b, mode, out_dtype, tm, tn, tk, name, cols_outer=False):
    def ix(f):
        if cols_outer:
            return lambda j, i, q: f(i, j, q)
        return f

    if mode == "nn":
        (m, k), (_, n) = a.shape, b.shape
        a_spec = pl.BlockSpec((tm, tk), ix(lambda i, j, q: (i, q)))
        b_spec = pl.BlockSpec((tk, tn), ix(lambda i, j, q: (q, j)))
        dims = (((1,), (0,)), ((), ()))
    elif mode == "nt":
        (m, k), (n, _) = a.shape, b.shape
        a_spec = pl.BlockSpec((tm, tk), ix(lambda i, j, q: (i, q)))
        b_spec = pl.BlockSpec((tn, tk), ix(lambda i, j, q: (j, q)))
        dims = (((1,), (1,)), ((), ()))
    else:
        (k, m), (_, n) = a.shape, b.shape
        a_spec = pl.BlockSpec((tk, tm), ix(lambda i, j, q: (q, i)))
        b_spec = pl.BlockSpec((tk, tn), ix(lambda i, j, q: (q, j)))
        dims = (((0,), (0,)), ((), ()))
    assert m % tm == 0 and n % tn == 0 and k % tk == 0, (name, a.shape, b.shape)
    nk = k // tk
    grid = (n // tn, m // tm, nk) if cols_outer else (m // tm, n // tn, nk)

    def body(a_ref, b_ref, o_ref, acc_ref):
        part = lax.dot_general(a_ref[...].astype(BF16), b_ref[...].astype(BF16), dims, preferred_element_type=F32)
        if nk == 1:
            o_ref[...] = part.astype(o_ref.dtype)
        else:
            q = pl.program_id(2)

            @pl.when(q == 0)
            def _():
                acc_ref[...] = part

            @pl.when(q > 0)
            def _():
                acc_ref[...] += part

            @pl.when(q == nk - 1)
            def _():
                o_ref[...] = acc_ref[...].astype(o_ref.dtype)

    return pl.pallas_call(
        body, name=name, grid=grid,
        in_specs=[a_spec, b_spec], out_specs=pl.BlockSpec((tm, tn), ix(lambda i, j, q: (i, j))),
        out_shape=jax.ShapeDtypeStruct((m, n), out_dtype),
        scratch_shapes=[pltpu.VMEM((tm, tn) if nk > 1 else (SUB, LANE), F32)],
        compiler_params=_cparams(("parallel", "parallel", "arbitrary")),
    )(a, b)


def _row_mask(i, tpe, rows):
    r = lax.broadcasted_iota(jnp.int32, (rows, 1), 0)
    return jnp.logical_or((i % tpe) != 0, r >= PAD)


def _rms_fwd(h, g, rt, name):
    t, d = h.shape

    def body(h_ref, g_ref, o_ref):
        x = h_ref[...]
        r = lax.rsqrt(jnp.mean(x * x, axis=-1, keepdims=True) + NORM_EPS)
        o_ref[...] = (x * r * g_ref[...]).astype(BF16)

    return pl.pallas_call(
        body, name=name, grid=(t // rt,),
        in_specs=[pl.BlockSpec((rt, d), lambda i: (i, 0)), pl.BlockSpec((1, d), lambda i: (0, 0))],
        out_specs=pl.BlockSpec((rt, d), lambda i: (i, 0)),
        out_shape=jax.ShapeDtypeStruct((t, d), BF16),
        compiler_params=_cparams(("parallel",)),
    )(h, g)


def _post_fwd(h, m, g, rt, name):
    t, d = h.shape

    def body(h_ref, m_ref, g_ref, o_ref):
        x = m_ref[...]
        r = lax.rsqrt(jnp.mean(x * x, axis=-1, keepdims=True) + NORM_EPS)
        o_ref[...] = h_ref[...] + x * r * g_ref[...]

    row = pl.BlockSpec((rt, d), lambda i: (i, 0))
    return pl.pallas_call(
        body, name=name, grid=(t // rt,),
        in_specs=[row, row, pl.BlockSpec((1, d), lambda i: (0, 0))], out_specs=row,
        out_shape=jax.ShapeDtypeStruct((t, d), F32),
        compiler_params=_cparams(("parallel",)),
    )(h, m, g)


def _rms_bwd_math(x, g, dy):
    r = lax.rsqrt(jnp.mean(x * x, axis=-1, keepdims=True) + NORM_EPS)
    gdy = dy * g
    dx = r * gdy - x * (r * r * r) * jnp.mean(gdy * x, axis=-1, keepdims=True)
    return dx, dy * x * r


def _post_bwd(dh, m, g, rt, name):
    t, d = dh.shape

    def body(dh_ref, m_ref, g_ref, dm_ref, dg_ref):
        dm, dgt = _rms_bwd_math(m_ref[...], g_ref[...], dh_ref[...])
        dm_ref[...] = dm

        @pl.when(pl.program_id(0) == 0)
        def _():
            dg_ref[...] = jnp.zeros_like(dg_ref)

        dg_ref[...] += jnp.sum(dgt, axis=0, keepdims=True)

    row = pl.BlockSpec((rt, d), lambda i: (i, 0))
    vec = pl.BlockSpec((1, d), lambda i: (0, 0))
    return pl.pallas_call(
        body, name=name, grid=(t // rt,),
        in_specs=[row, row, vec], out_specs=[row, vec],
        out_shape=[jax.ShapeDtypeStruct((t, d), F32), jax.ShapeDtypeStruct((1, d), F32)],
        compiler_params=_cparams(("arbitrary",)),
    )(dh, m, g)


def _rms_bwd(dh_res, dhn, h, g, lp, rt, name):
    t, d = h.shape
    tpe = lp // rt

    def body(dr_ref, dn_ref, h_ref, g_ref, dh_ref, dg_ref, dmeta_ref):
        i = pl.program_id(0)
        dx, dgt = _rms_bwd_math(h_ref[...], g_ref[...], dn_ref[...])
        dh = dr_ref[...] + dx
        dh_ref[...] = dh

        @pl.when(i == 0)
        def _():
            dg_ref[...] = jnp.zeros_like(dg_ref)
            dmeta_ref[...] = jnp.zeros_like(dmeta_ref)

        dg_ref[...] += jnp.sum(dgt, axis=0, keepdims=True)

        @pl.when((i % tpe) == 0)
        def _():
            dmeta_ref[...] += dh[PAD:PAD + N_META, :]

    row = pl.BlockSpec((rt, d), lambda i: (i, 0))
    vec = pl.BlockSpec((1, d), lambda i: (0, 0))
    return pl.pallas_call(
        body, name=name, grid=(t // rt,),
        in_specs=[row, row, row, vec],
        out_specs=[row, vec, pl.BlockSpec((N_META, d), lambda i: (0, 0))],
        out_shape=[jax.ShapeDtypeStruct((t, d), F32), jax.ShapeDtypeStruct((1, d), F32),
                   jax.ShapeDtypeStruct((N_META, d), F32)],
        compiler_params=_cparams(("arbitrary",)),
    )(dh_res, dhn, h, g)


def _loss_kernel(h, target, lp, name):
    t, d = h.shape
    cpe = lp // CHUNK
    nb = t // lp

    def body(h_ref, t_ref, dh_ref, loss_ref):
        b, j = pl.program_id(0), pl.program_id(1)

        @pl.when(jnp.logical_and(b == 0, j == 0))
        def _():
            loss_ref[...] = jnp.zeros_like(loss_ref)

        @pl.when(j == 0)
        def _():
            dh_ref[...] = jnp.zeros_like(dh_ref)

        @pl.when(j > 0)
        def _():
            err = h_ref[...] - t_ref[...]
            dh_ref[...] = err * (1.0 / d)
            loss_ref[...] += (0.5 / d) * jnp.sum(err * err)

    return pl.pallas_call(
        body, name=name, grid=(nb, cpe),
        in_specs=[pl.BlockSpec((CHUNK, d), lambda b, j: (b * cpe + j, 0)),
                  pl.BlockSpec((CHUNK, d), lambda b, j: (b * (cpe - 1) + jnp.maximum(j - 1, 0), 0))],
        out_specs=[pl.BlockSpec((CHUNK, d), lambda b, j: (b * cpe + j, 0)),
                   pl.BlockSpec((SUB, LANE), lambda b, j: (0, 0))],
        out_shape=[jax.ShapeDtypeStruct((t, d), F32), jax.ShapeDtypeStruct((SUB, LANE), F32)],
        compiler_params=_cparams(("arbitrary", "arbitrary")),
    )(h, target)


def _window_spec(rows, width, col):
    return pl.BlockSpec((pl.Element(rows), pl.Element(width)), lambda i: (i * rows, col))


def _halo_specs(t, width, col, halo):
    cur = _window_spec(TILE, width, col)
    prev = pl.BlockSpec((pl.Element(halo), pl.Element(width)),
                        lambda i: (pl.multiple_of(jnp.maximum(i * TILE - halo, 0), halo), col))
    nxt = pl.BlockSpec((pl.Element(halo), pl.Element(width)),
                       lambda i: (pl.multiple_of(jnp.minimum((i + 1) * TILE, t - halo), halo), col))
    return cur, prev, nxt


def _f32(ref, lo, hi):
    return ref[:, lo:hi].astype(F32)


def _rows_from(x, start, rows):
    s = start % SUB
    if s == 0:
        return x[start:start + rows]
    return pltpu.roll(x, x.shape[0] - s, axis=0)[start - s:start - s + rows]


def _conv_a(ve, w):
    rows = ve.shape[0] - HALO_A
    return (w[0:1] * _rows_from(ve, HALO_A - 2, rows) + w[1:2] * _rows_from(ve, HALO_A - 1, rows)
            + w[2:3] * ve[HALO_A:HALO_A + rows])


def _a_fwd(proj, w, lp, name):
    t = proj.shape[0]
    tpe = lp // TILE
    cur, prev, _ = _halo_specs(t, 4 * D_A, 0, HALO_A)

    def body(p_ref, ph_ref, w_ref, y_ref):
        first = (pl.program_id(0) % tpe) == 0
        v = _f32(p_ref, D_A, 2 * D_A) * _f32(p_ref, 2 * D_A, 3 * D_A)
        vh = jnp.where(first, 0.0, _f32(ph_ref, D_A, 2 * D_A) * _f32(ph_ref, 2 * D_A, 3 * D_A))
        cv = _conv_a(jnp.concatenate([vh, v], axis=0), w_ref[...])
        y_ref[...] = (_f32(p_ref, 0, D_A) * cv * _silu(_f32(p_ref, 3 * D_A, 4 * D_A))).astype(BF16)

    return pl.pallas_call(
        body, name=name, grid=(t // TILE,),
        in_specs=[cur, prev, pl.BlockSpec((CONV_A_K, D_A), lambda i: (0, 0))],
        out_specs=_window_spec(TILE, D_A, D_B),
        out_shape=jax.ShapeDtypeStruct((t, D_B + D_A + D_C), BF16),
        compiler_params=_cparams(("parallel",)),
    )(proj, proj, w)


def _a_bwd(dy, proj, w, dproj, lp, name):
    t = proj.shape[0]
    tpe = lp // TILE
    cur, prev, nxt = _halo_specs(t, 4 * D_A, 0, HALO_A)
    dcur, _, dnxt = _halo_specs(t, D_A, D_B, HALO_A)

    def body(dy_ref, dyn_ref, p_ref, ph_ref, pn_ref, w_ref, dproj_ref, dp_ref, dw_ref):
        i = pl.program_id(0)
        first = (i % tpe) == 0
        last = (i % tpe) == tpe - 1
        w = w_ref[...]
        ab, ac, ax, az = (_f32(p_ref, k * D_A, (k + 1) * D_A) for k in range(4))
        v = ac * ax
        vh = jnp.where(first, 0.0, _f32(ph_ref, D_A, 2 * D_A) * _f32(ph_ref, 2 * D_A, 3 * D_A))
        ve = jnp.concatenate([vh, v], axis=0)
        taps = [_rows_from(ve, HALO_A - 2 + k, TILE) for k in range(CONV_A_K)]
        cv = w[0:1] * taps[0] + w[1:2] * taps[1] + w[2:3] * taps[2]
        s = _silu(az)
        dy_ = dy_ref[...].astype(F32)
        dcv = dy_ * ab * s
        dcvn = jnp.where(last, 0.0, dyn_ref[...].astype(F32) * _f32(pn_ref, 0, D_A) * _silu(_f32(pn_ref, 3 * D_A, 4 * D_A)))
        dce = jnp.concatenate([dcv, dcvn], axis=0)
        dv = w[2:3] * dce[0:TILE] + w[1:2] * _rows_from(dce, 1, TILE) + w[0:1] * _rows_from(dce, 2, TILE)
        dp = jnp.concatenate([dy_ * cv * s, dv * ax, dv * ac, dy_ * ab * cv * _dsilu(az)], axis=1)
        dp_ref[...] = jnp.where(_row_mask(i, tpe, TILE), dp, 0.0).astype(BF16)
        dw = jnp.concatenate(
            [jnp.sum(dcv * taps[k], axis=0, keepdims=True) for k in range(CONV_A_K)], axis=0)

        @pl.when(i == 0)
        def _():
            dw_ref[...] = jnp.zeros_like(dw_ref)

        dw_ref[...] += dw

    wspec = pl.BlockSpec((CONV_A_K, D_A), lambda i: (0, 0))
    return pl.pallas_call(
        body, name=name, grid=(t // TILE,),
        in_specs=[dcur, dnxt, cur, prev, nxt, wspec, HBM_SPEC],
        out_specs=[_window_spec(TILE, 4 * D_A, 0), wspec],
        out_shape=[jax.ShapeDtypeStruct(dproj.shape, dproj.dtype), jax.ShapeDtypeStruct((CONV_A_K, D_A), F32)],
        input_output_aliases={6: 0},
        compiler_params=_cparams(("arbitrary",)),
    )(dy, dy, proj, proj, proj, w, dproj)


def _conv_ssm(xe, w, rows, off):
    acc = w[0:1] * _rows_from(xe, off - 3, rows)
    for k in range(1, SSM_K):
        acc = acc + w[k:k + 1] * _rows_from(xe, off - 3 + k, rows)
    return acc


def _xbc_fwd(proj, w, b, lp, name):
    t = proj.shape[0]
    tpe = lp // TILE
    cur, prev, _ = _halo_specs(t, N_XBC, COL_XBC, HALO_A)

    def body(x_ref, xh_ref, w_ref, b_ref, o_ref):
        first = (pl.program_id(0) % tpe) == 0
        xh = jnp.where(first, 0.0, xh_ref[...].astype(F32))
        xe = jnp.concatenate([xh, x_ref[...].astype(F32)], axis=0)
        o_ref[...] = _silu(_conv_ssm(xe, w_ref[...], TILE, HALO_A) + b_ref[...]).astype(BF16)

    return pl.pallas_call(
        body, name=name, grid=(t // TILE,),
        in_specs=[cur, prev, pl.BlockSpec((SSM_K, N_XBC), lambda i: (0, 0)), pl.BlockSpec((1, N_XBC), lambda i: (0, 0))],
        out_specs=pl.BlockSpec((TILE, N_XBC), lambda i: (i, 0)),
        out_shape=jax.ShapeDtypeStruct((t, N_XBC), BF16),
        compiler_params=_cparams(("parallel",)),
    )(proj, proj, w, b)


def _xbc_bwd(dxbc, proj, w, b, dproj, lp, name):
    t = proj.shape[0]
    tpe = lp // TILE
    cur, prev, nxt = _halo_specs(t, N_XBC, COL_XBC, HALO_A)
    dcur, _, dnxt = _halo_specs(t, N_XBC, 0, HALO_A)

    def body(d_ref, dn_ref, x_ref, xh_ref, xn_ref, w_ref, b_ref, dproj_ref, dx_ref, dw_ref, db_ref):
        i = pl.program_id(0)
        first = (i % tpe) == 0
        last = (i % tpe) == tpe - 1
        w = w_ref[...]
        xh = jnp.where(first, 0.0, xh_ref[...].astype(F32))
        xe = jnp.concatenate([xh, x_ref[...].astype(F32), xn_ref[...].astype(F32)], axis=0)
        taps = [_rows_from(xe, HALO_A - 3 + k, TILE + HALO_A) for k in range(SSM_K)]
        pre = b_ref[...] + w[0:1] * taps[0]
        for k in range(1, SSM_K):
            pre = pre + w[k:k + 1] * taps[k]
        de = jnp.concatenate([d_ref[...].astype(F32), jnp.where(last, 0.0, dn_ref[...].astype(F32))], axis=0)
        dpre = de * _dsilu(pre)
        dx = w[3:4] * dpre[0:TILE]
        for k in range(SSM_K - 1):
            dx = dx + w[k:k + 1] * _rows_from(dpre, 3 - k, TILE)
        dx_ref[...] = jnp.where(_row_mask(i, tpe, TILE), dx, 0.0).astype(BF16)
        dpc = dpre[0:TILE]
        dw = jnp.concatenate(
            [jnp.sum(dpc * taps[k][0:TILE], axis=0, keepdims=True) for k in range(SSM_K)], axis=0)

        @pl.when(i == 0)
        def _():
            dw_ref[...] = jnp.zeros_like(dw_ref)
            db_ref[...] = jnp.zeros_like(db_ref)

        dw_ref[...] += dw
        db_ref[...] += jnp.sum(dpc, axis=0, keepdims=True)

    wspec = pl.BlockSpec((SSM_K, N_XBC), lambda i: (0, 0))
    bspec = pl.BlockSpec((1, N_XBC), lambda i: (0, 0))
    return pl.pallas_call(
        body, name=name, grid=(t // TILE,),
        in_specs=[dcur, dnxt, cur, prev, nxt, wspec, bspec, HBM_SPEC],
        out_specs=[_window_spec(TILE, N_XBC, COL_XBC), wspec, bspec],
        out_shape=[jax.ShapeDtypeStruct(dproj.shape, dproj.dtype), jax.ShapeDtypeStruct((SSM_K, N_XBC), F32),
                   jax.ShapeDtypeStruct((1, N_XBC), F32)],
        input_output_aliases={7: 0},
        compiler_params=_cparams(("arbitrary",)),
    )(dxbc, dxbc, proj, proj, proj, w, b, dproj)


SUBROWS = 32


def _fill_shifted(scr, x):
    scr[0] = x
    for s in range(1, SUB):
        scr[s] = pltpu.roll(x, x.shape[0] - s, axis=0)


def _window(scr, start, rows):
    s = start % SUB
    return scr[s, start - s:start - s + rows, :]


def _conv_conf(scr, w, rows, off, base):
    acc = w[0:1] * _window(scr, base + off - (CONF_K - 1), rows)
    for k in range(1, CONF_K):
        acc = acc + w[k:k + 1] * _window(scr, base + off - (CONF_K - 1) + k, rows)
    return acc


def _ln_fwd(u1, g, b):
    mu = jnp.mean(u1, axis=-1, keepdims=True)
    xc = u1 - mu
    rstd = lax.rsqrt(jnp.mean(xc * xc, axis=-1, keepdims=True) + LN_EPS)
    n = xc * rstd
    return n, rstd, n * g + b


def _c_fwd(proj, w, cb, g, b, ybuf, lp, name):
    t = proj.shape[0]
    tpe = lp // TILE
    cur, prev, _ = _halo_specs(t, 3 * D_C, COL_C, HALO_C)

    def body(p_ref, ph_ref, w_ref, cb_ref, g_ref, b_ref, ybuf_ref, y_ref, u0_scr):
        first = (pl.program_id(0) % tpe) == 0
        u0h = jnp.where(first, 0.0, _f32(ph_ref, 0, D_C) * jax.nn.sigmoid(_f32(ph_ref, D_C, 2 * D_C)))
        _fill_shifted(u0_scr, jnp.concatenate([u0h, _f32(p_ref, 0, D_C) * jax.nn.sigmoid(_f32(p_ref, D_C, 2 * D_C))], axis=0))
        w = w_ref[...]
        for r0 in range(0, TILE, SUBROWS):
            u1 = _conv_conf(u0_scr, w, SUBROWS, HALO_C, r0) + cb_ref[...]
            _, _, u2 = _ln_fwd(u1, g_ref[...], b_ref[...])
            cz = p_ref[r0:r0 + SUBROWS, 2 * D_C:3 * D_C].astype(F32)
            y_ref[r0:r0 + SUBROWS, :] = (_silu(u2) * _silu(cz)).astype(BF16)

    vec = pl.BlockSpec((1, D_C), lambda i: (0, 0))
    return pl.pallas_call(
        body, name=name, grid=(t // TILE,),
        in_specs=[cur, prev, pl.BlockSpec((CONF_K, D_C), lambda i: (0, 0)), vec, vec, vec, HBM_SPEC],
        out_specs=_window_spec(TILE, D_C, D_B + D_A),
        out_shape=jax.ShapeDtypeStruct(ybuf.shape, ybuf.dtype),
        scratch_shapes=[pltpu.VMEM((SUB, HALO_C + TILE, D_C), F32)],
        input_output_aliases={6: 0},
        compiler_params=_cparams(("parallel",)),
    )(proj, proj, w, cb, g, b, ybuf)


def _c_bwd(dy, proj, w, cb, g, b, dproj, lp, name):
    t = proj.shape[0]
    tpe = lp // TILE
    cur, prev, nxt = _halo_specs(t, 3 * D_C, COL_C, HALO_C)
    dcur, _, dnxt = _halo_specs(t, D_C, D_B + D_A, HALO_C)
    ext = TILE + HALO_C

    def body(dy_ref, dyn_ref, p_ref, ph_ref, pn_ref, w_ref, cb_ref, g_ref, b_ref, dproj_ref,
             dp_ref, dw_ref, dcb_ref, dg_ref, db_ref, u0_scr, du1_scr, wacc_scr):
        i = pl.program_id(0)
        first = (i % tpe) == 0
        last = (i % tpe) == tpe - 1
        w = w_ref[...]

        @pl.when(i == 0)
        def _():
            dw_ref[...] = jnp.zeros_like(dw_ref)
            dcb_ref[...] = jnp.zeros_like(dcb_ref)
            dg_ref[...] = jnp.zeros_like(dg_ref)
            db_ref[...] = jnp.zeros_like(db_ref)

        u0h = jnp.where(first, 0.0, _f32(ph_ref, 0, D_C) * jax.nn.sigmoid(_f32(ph_ref, D_C, 2 * D_C)))
        _fill_shifted(u0_scr, jnp.concatenate(
            [u0h, _f32(p_ref, 0, D_C) * jax.nn.sigmoid(_f32(p_ref, D_C, 2 * D_C)),
             _f32(pn_ref, 0, D_C) * jax.nn.sigmoid(_f32(pn_ref, D_C, 2 * D_C))], axis=0))
        dcb = jnp.zeros((1, D_C), F32)
        dg = jnp.zeros((1, D_C), F32)
        db = jnp.zeros((1, D_C), F32)
        for r0 in range(0, ext, SUBROWS):
            in_tile = r0 < TILE
            src, dsrc, q0 = (p_ref, dy_ref, r0) if in_tile else (pn_ref, dyn_ref, r0 - TILE)
            u1 = _conv_conf(u0_scr, w, SUBROWS, HALO_C, r0) + cb_ref[...]
            n, rstd, u2 = _ln_fwd(u1, g_ref[...], b_ref[...])
            cz = src[q0:q0 + SUBROWS, 2 * D_C:3 * D_C].astype(F32)
            dyc = dsrc[q0:q0 + SUBROWS, :].astype(F32)
            if not in_tile:
                dyc = jnp.where(last, 0.0, dyc)
            du2 = dyc * _silu(cz) * _dsilu(u2)
            dn = du2 * g_ref[...]
            du1 = rstd * (dn - jnp.mean(dn, axis=-1, keepdims=True) - n * jnp.mean(dn * n, axis=-1, keepdims=True))
            du1_scr[0, r0:r0 + SUBROWS, :] = du1
            if in_tile:
                dp_ref[r0:r0 + SUBROWS, 2 * D_C:3 * D_C] = (dyc * _silu(u2) * _dsilu(cz)).astype(BF16)
                dg = dg + jnp.sum(du2 * n, axis=0, keepdims=True)
                db = db + jnp.sum(du2, axis=0, keepdims=True)
                dcb = dcb + jnp.sum(du1, axis=0, keepdims=True)
        dcb_ref[...] += dcb
        dg_ref[...] += dg
        db_ref[...] += db
        mask = _row_mask(i, tpe, TILE)
        _fill_shifted(du1_scr, du1_scr[0])
        for r0 in range(0, TILE, SUBROWS):
            acc = w[0:1] * _window(du1_scr, r0 + CONF_K - 1, SUBROWS)
            for k in range(1, CONF_K):
                acc = acc + w[k:k + 1] * _window(du1_scr, r0 + CONF_K - 1 - k, SUBROWS)
            ca = p_ref[r0:r0 + SUBROWS, 0:D_C].astype(F32)
            sg = jax.nn.sigmoid(p_ref[r0:r0 + SUBROWS, D_C:2 * D_C].astype(F32))
            m = mask[r0:r0 + SUBROWS]
            dp_ref[r0:r0 + SUBROWS, 0:D_C] = jnp.where(m, acc * sg, 0.0).astype(BF16)
            dp_ref[r0:r0 + SUBROWS, D_C:2 * D_C] = jnp.where(m, acc * ca * sg * (1.0 - sg), 0.0).astype(BF16)
        for k in range(CONF_K):
            part = jnp.zeros((SUB, D_C), F32)
            for r0 in range(0, TILE, SUBROWS):
                prod = du1_scr[0, r0:r0 + SUBROWS, :] * _window(u0_scr, HALO_C + r0 - (CONF_K - 1) + k, SUBROWS)
                for q in range(0, SUBROWS, SUB):
                    part = part + prod[q:q + SUB]
            wacc_scr[k:k + 1, :] = jnp.sum(part, axis=0, keepdims=True)
        dw_ref[...] += wacc_scr[0:CONF_K, :]

    vec = pl.BlockSpec((1, D_C), lambda i: (0, 0))
    wspec = pl.BlockSpec((CONF_K, D_C), lambda i: (0, 0))
    return pl.pallas_call(
        body, name=name, grid=(t // TILE,),
        in_specs=[dcur, dnxt, cur, prev, nxt, wspec, vec, vec, vec, HBM_SPEC],
        out_specs=[_window_spec(TILE, 3 * D_C, COL_C), wspec, vec, vec, vec],
        out_shape=[jax.ShapeDtypeStruct(dproj.shape, dproj.dtype), jax.ShapeDtypeStruct((CONF_K, D_C), F32),
                   jax.ShapeDtypeStruct((1, D_C), F32), jax.ShapeDtypeStruct((1, D_C), F32),
                   jax.ShapeDtypeStruct((1, D_C), F32)],
        scratch_shapes=[pltpu.VMEM((SUB, ext + HALO_C, D_C), F32), pltpu.VMEM((SUB, ext, D_C), F32),
                        pltpu.VMEM((HALO_C, D_C), F32)],
        input_output_aliases={9: 0},
        compiler_params=_cparams(("arbitrary",)),
    )(dy, dy, proj, proj, proj, w, cb, g, b, dproj)


def _split_dot(x, m_bf16, terms):
    acc = None
    rem = x
    for _ in range(terms):
        hi = rem.astype(BF16)
        part = jnp.dot(hi, m_bf16, preferred_element_type=F32)
        acc = part if acc is None else acc + part
        rem = rem - hi.astype(F32)
    return acc


def _split_dot_left(m_bf16, x, terms):
    acc = None
    rem = x
    for _ in range(terms):
        hi = rem.astype(BF16)
        part = jnp.dot(m_bf16, hi, preferred_element_type=F32)
        acc = part if acc is None else acc + part
        rem = rem - hi.astype(F32)
    return acc


def _tri(rows_ge_cols):
    r = lax.broadcasted_iota(jnp.int32, (CHUNK, CHUNK), 0)
    c = lax.broadcasted_iota(jnp.int32, (CHUNK, CHUNK), 1)
    return (r >= c) if rows_ge_cols else (r <= c)


def _softplus(x):
    return jnp.maximum(x, 0.0) + jnp.log(1.0 + jnp.exp(-jnp.abs(x)))


def _ssd_common(dtraw, dtb, dtb_t, a_log, a_log_t, e_mat, valid_col, valid_row):
    a = -jnp.exp(a_log)
    lane = lax.broadcasted_iota(jnp.int32, (1, LANE), 1)
    a = jnp.where(lane < N_HEADS, a, 0.0)
    a_t = -jnp.exp(a_log_t)
    dt = jnp.where(valid_col, _softplus(dtraw + dtb), 0.0)
    dt = jnp.where(lane < N_HEADS, dt, 0.0)
    dt_t = jnp.where(valid_row, _softplus(dtraw.T[0:N_HEADS, :] + dtb_t), 0.0)
    ltri = _tri(True).astype(BF16)
    utri = _tri(False).astype(BF16)
    big_a = _split_dot_left(ltri, dt * a, 3)
    big_a_t = _split_dot(dt_t * a_t, utri, 3)
    e_a = jnp.exp(big_a)
    d_s = jnp.exp(big_a[CHUNK - 1:CHUNK, :] - big_a)
    dt_x = _split_dot(dt, e_mat, 2)
    e_a_x = _split_dot(e_a, e_mat, 2)
    d_s_x = _split_dot(d_s, e_mat, 2)
    cd_x = e_a_x[CHUNK - 1:CHUNK, :]
    return a, dt, big_a, big_a_t, e_a, d_s, dt_x, e_a_x, d_s_x, cd_x


def _decay(big_a, big_a_t, h, transposed):
    col = big_a[:, h:h + 1]
    row = big_a_t[h:h + 1, :]
    if not transposed:
        seg = col - row
        return jnp.where(_tri(True), jnp.exp(jnp.minimum(seg, 0.0)), 0.0)
    seg = row - col
    return jnp.where(_tri(False), jnp.exp(jnp.minimum(seg, 0.0)), 0.0)


NT_DIMS = (((1,), (1,)), ((), ()))
TN_DIMS = (((0,), (0,)), ((), ()))
HBM_SPEC = pl.BlockSpec(memory_space=pl.ANY)


class _Comm:
    def __init__(self, inputs, out_shapes, sem_shapes, copies, aliases=None):
        self.inputs, self.out_shapes, self.sem_shapes, self.copies = inputs, out_shapes, sem_shapes, copies
        self.aliases = aliases or {}

    def start(self, ins, outs, sems):
        local, sends, _ = self.copies(ins, outs, sems, False)
        for cp in local + sends:
            cp.start()

    def wait(self, ins, outs, sems):
        local, sends, recvs = self.copies(ins, outs, sems, True)
        for cp in recvs:
            cp.wait_recv()
        for cp in sends:
            cp.wait_send()
        for cp in local:
            cp.wait()


def _grid_call(body, name, grid, in_specs, out_specs, out_shape, scratch_shapes, operands, comm=None, aliases=None):
    aliases = dict(aliases or {})
    if comm is None:
        return pl.pallas_call(
            body, name=name, grid=grid, in_specs=in_specs, out_specs=out_specs, out_shape=out_shape,
            scratch_shapes=scratch_shapes, input_output_aliases=aliases,
            compiler_params=_cparams(("arbitrary",) * len(grid)))(*operands)
    n_in, n_out, n_scr = len(in_specs), len(out_specs), len(scratch_shapes)
    nci, nco = len(comm.inputs), len(comm.out_shapes)

    def wrapped(*refs):
        ins, cins = refs[:n_in], refs[n_in:n_in + nci]
        o0 = n_in + nci
        outs, couts = refs[o0:o0 + n_out], refs[o0 + n_out:o0 + n_out + nco]
        s0 = o0 + n_out + nco
        scr, csems = refs[s0:s0 + n_scr], refs[s0 + n_scr:]
        first = jnp.logical_and(pl.program_id(0) == 0, pl.program_id(1) == 0)
        last = jnp.logical_and(pl.program_id(0) == grid[0] - 1, pl.program_id(1) == grid[1] - 1)

        @pl.when(first)
        def _():
            comm.start(cins, couts, csems)

        body(*ins, *outs, *scr)

        @pl.when(last)
        def _():
            comm.wait(cins, couts, csems)

    res = pl.pallas_call(
        wrapped, name=name, grid=grid,
        in_specs=list(in_specs) + [HBM_SPEC] * nci, out_specs=list(out_specs) + [HBM_SPEC] * nco,
        out_shape=list(out_shape) + list(comm.out_shapes),
        scratch_shapes=list(scratch_shapes) + list(comm.sem_shapes),
        input_output_aliases={**aliases, **{n_in + k: n_out + v for k, v in comm.aliases.items()}},
        compiler_params=_cparams(("arbitrary",) * len(grid)))(*operands, *comm.inputs)
    return res


def _ssd_fwd(xbc, proj, dtb, a_log, d_skip_x, norm_g, e_mat, ybuf, lp, name, comm=None):
    t = xbc.shape[0]
    cpe = lp // CHUNK
    nb = t // lp
    dtb_p = jnp.pad(dtb.reshape(1, N_HEADS), ((0, 0), (0, LANE - N_HEADS)))
    alog_p = jnp.pad(a_log.reshape(1, N_HEADS), ((0, 0), (0, LANE - N_HEADS)))
    dtb_t = dtb.reshape(N_HEADS, 1)
    alog_t = a_log.reshape(N_HEADS, 1)

    def body(xbc_ref, bz_ref, dt_ref, dtb_ref, dtbt_ref, al_ref, alt_ref, dx_ref, g_ref, e_ref, ybuf_ref,
             yb_ref, ys_ref, st_ref, s_scr):
        c = pl.program_id(1)

        @pl.when(c == 0)
        def _():
            s_scr[...] = jnp.zeros_like(s_scr)

        rows = lax.broadcasted_iota(jnp.int32, (CHUNK, 1), 0)
        cols = lax.broadcasted_iota(jnp.int32, (1, CHUNK), 1)
        valid_col = jnp.logical_or(c > 0, rows >= PAD)
        valid_row = jnp.logical_or(c > 0, cols >= PAD)
        e_mat = e_ref[...]
        _, _, big_a, big_a_t, _, _, dt_x, e_a_x, d_s_x, cd_x = _ssd_common(
            dt_ref[...].astype(F32), dtb_ref[...], dtbt_ref[...], al_ref[...], alt_ref[...], e_mat, valid_col, valid_row)
        xs = xbc_ref[:, 0:D_B].astype(F32)
        bs = xbc_ref[:, D_B:D_B + N_GROUPS * N_STATE]
        cs = xbc_ref[:, D_B + N_GROUPS * N_STATE:N_XBC]
        xdt = xs * dt_x
        xdt_b = xdt.astype(BF16)
        st_prev = s_scr[...]
        st_ref[...] = st_prev.astype(BF16)
        st_b = st_prev.astype(BF16)
        u_b = (xdt * d_s_x).astype(BF16)
        y_parts = []
        for g in range(N_GROUPS):
            gs = slice(g * N_STATE, (g + 1) * N_STATE)
            gw = slice(g * GROUP_W, (g + 1) * GROUP_W)
            cb = lax.dot_general(cs[:, gs], bs[:, gs], NT_DIMS, preferred_element_type=F32)
            y_off = jnp.dot(cs[:, gs], st_b[:, gw], preferred_element_type=F32)
            diag = []
            for e in range(N_HEADS // N_GROUPS):
                h = g * (N_HEADS // N_GROUPS) + e
                m = (cb * _decay(big_a, big_a_t, h, False)).astype(BF16)
                diag.append(jnp.dot(m, xdt_b[:, h * HEAD_DIM:(h + 1) * HEAD_DIM], preferred_element_type=F32))
            y_parts.append(jnp.concatenate(diag, axis=1) + y_off * e_a_x[:, gw])
            new_st = lax.dot_general(bs[:, gs], u_b[:, gw], TN_DIMS, preferred_element_type=F32)
            s_scr[:, gw] = st_prev[:, gw] * cd_x[:, gw] + new_st
        y = jnp.concatenate(y_parts, axis=1) + xs * dx_ref[...]
        ys_ref[...] = y
        z = y * _silu(bz_ref[...].astype(F32))
        r = lax.rsqrt(jnp.mean(z * z, axis=-1, keepdims=True) + NORM_EPS)
        yb_ref[...] = (z * r * g_ref[...]).astype(BF16)

    def row(width, col):
        return pl.BlockSpec((CHUNK, width), lambda b, c: (b * cpe + c, col))

    def const(shape):
        return pl.BlockSpec(shape, lambda b, c: (0,) * len(shape))

    return _grid_call(
        body, name, (nb, cpe),
        [row(N_XBC, 0), row(D_B, COL_BZ // D_B), row(LANE, COL_DT // LANE),
         const((1, LANE)), const((N_HEADS, 1)), const((1, LANE)), const((N_HEADS, 1)),
         const((1, D_B)), const((1, D_B)), const((LANE, D_B)), HBM_SPEC],
        [row(D_B, 0), row(D_B, 0), pl.BlockSpec((None, N_STATE, D_B), lambda b, c: (b * cpe + c, 0, 0))],
        [jax.ShapeDtypeStruct(ybuf.shape, ybuf.dtype), jax.ShapeDtypeStruct((t, D_B), F32),
         jax.ShapeDtypeStruct((nb * cpe, N_STATE, D_B), BF16)],
        [pltpu.VMEM((N_STATE, D_B), F32)],
        (xbc, proj, proj, dtb_p, dtb_t, alog_p, alog_t, d_skip_x, norm_g, e_mat, ybuf), comm, aliases={10: 0})


def _ssd_bwd(dy, y_ssd, xbc, proj, states, dtb, a_log, d_skip_x, norm_g, e_mat, e_mat_t, lp, name, comm=None):
    t = xbc.shape[0]
    cpe = lp // CHUNK
    nb = t // lp
    hpg = N_HEADS // N_GROUPS
    dtb_p = jnp.pad(dtb.reshape(1, N_HEADS), ((0, 0), (0, LANE - N_HEADS)))
    alog_p = jnp.pad(a_log.reshape(1, N_HEADS), ((0, 0), (0, LANE - N_HEADS)))
    dtb_t = dtb.reshape(N_HEADS, 1)
    alog_t = a_log.reshape(N_HEADS, 1)

    def body(dy_ref, ys_ref, xbc_ref, bz_ref, dt_ref, st_ref, dtb_ref, dtbt_ref, al_ref, alt_ref, dx_ref, g_ref,
             e_ref, et_ref, dxbc_ref, dpw_ref, dg_ref, ddtb_ref, dal_ref, dd_ref, ds_scr):
        c = pl.program_id(1)
        cc = cpe - 1 - c

        @pl.when(c == 0)
        def _():
            ds_scr[...] = jnp.zeros_like(ds_scr)

        @pl.when(jnp.logical_and(pl.program_id(0) == 0, c == 0))
        def _():
            dg_ref[...] = jnp.zeros_like(dg_ref)
            ddtb_ref[...] = jnp.zeros_like(ddtb_ref)
            dal_ref[...] = jnp.zeros_like(dal_ref)
            dd_ref[...] = jnp.zeros_like(dd_ref)

        rows = lax.broadcasted_iota(jnp.int32, (CHUNK, 1), 0)
        cols = lax.broadcasted_iota(jnp.int32, (1, CHUNK), 1)
        valid_col = jnp.logical_or(cc > 0, rows >= PAD)
        valid_row = jnp.logical_or(cc > 0, cols >= PAD)
        e_mat = e_ref[...]
        e_mat_t = et_ref[...]
        dtraw = dt_ref[...].astype(F32)
        a, dt, big_a, big_a_t, e_a, d_s, dt_x, e_a_x, d_s_x, cd_x = _ssd_common(
            dtraw, dtb_ref[...], dtbt_ref[...], al_ref[...], alt_ref[...], e_mat, valid_col, valid_row)
        xs = xbc_ref[:, 0:D_B].astype(F32)
        bs = xbc_ref[:, D_B:D_B + N_GROUPS * N_STATE]
        cs = xbc_ref[:, D_B + N_GROUPS * N_STATE:N_XBC]
        xdt = xs * dt_x
        xdt_b = xdt.astype(BF16)
        st_b = st_ref[...]
        dst = ds_scr[...]
        dst_b = dst.astype(BF16)

        ys = ys_ref[...]
        bz = bz_ref[...].astype(F32)
        sil = _silu(bz)
        z = ys * sil
        dz, dgt = _rms_bwd_math(z, g_ref[...], dy_ref[...].astype(F32))
        dg_ref[...] += jnp.sum(dgt, axis=0, keepdims=True)
        dpw_ref[:, 0:D_B] = (dz * ys * _dsilu(bz)).astype(BF16)
        dys = dz * sil

        dd_lane = jnp.sum(dys * xs, axis=0, keepdims=True)
        dxs = dys * dx_ref[...]
        w_x = dys * e_a_x
        w_b = w_x.astype(BF16)
        dys_b = dys.astype(BF16)
        u_b = (xdt * d_s_x).astype(BF16)
        dxdt_parts, dbs_parts, dcs_parts, off_parts, g1_parts = [], [], [], [], []
        da_diag = jnp.zeros((CHUNK, LANE), F32)
        lane = lax.broadcasted_iota(jnp.int32, (1, LANE), 1)
        for g in range(N_GROUPS):
            gs = slice(g * N_STATE, (g + 1) * N_STATE)
            gw = slice(g * GROUP_W, (g + 1) * GROUP_W)
            cs_g, bs_g = cs[:, gs], bs[:, gs]
            dcs = lax.dot_general(w_b[:, gw], st_b[:, gw], NT_DIMS, preferred_element_type=F32)
            y_off = jnp.dot(cs_g, st_b[:, gw], preferred_element_type=F32)
            off_parts.append(y_off)
            dst_new = lax.dot_general(cs_g, w_b[:, gw], TN_DIMS, preferred_element_type=F32)
            g1 = jnp.dot(bs_g, dst_b[:, gw], preferred_element_type=F32)
            g1_parts.append(g1)
            dbs = lax.dot_general(u_b[:, gw], dst_b[:, gw], NT_DIMS, preferred_element_type=F32)
            cb = lax.dot_general(cs_g, bs_g, NT_DIMS, preferred_element_type=F32)
            cbt = lax.dot_general(bs_g, cs_g, NT_DIMS, preferred_element_type=F32)
            dcb = jnp.zeros((CHUNK, CHUNK), F32)
            dcbt = jnp.zeros((CHUNK, CHUNK), F32)
            dxdt_h = []
            for e in range(hpg):
                h = g * hpg + e
                hs = slice(h * HEAD_DIM, (h + 1) * HEAD_DIM)
                dec = _decay(big_a, big_a_t, h, False)
                dect = _decay(big_a, big_a_t, h, True)
                m = cb * dec
                mt = cbt * dect
                dxdt_h.append(jnp.dot(mt.astype(BF16), dys_b[:, hs], preferred_element_type=F32))
                dm = lax.dot_general(dys_b[:, hs], xdt_b[:, hs], NT_DIMS, preferred_element_type=F32)
                dmt = lax.dot_general(xdt_b[:, hs], dys_b[:, hs], NT_DIMS, preferred_element_type=F32)
                dcb = dcb + dm * dec
                dcbt = dcbt + dmt * dect
                da_h = jnp.sum(dm * m - dmt * mt, axis=1, keepdims=True)
                da_diag = da_diag + jnp.where(lane == h, da_h, 0.0)
            dcs = dcs + jnp.dot(dcb.astype(BF16), bs_g, preferred_element_type=F32)
            dbs = dbs + jnp.dot(dcbt.astype(BF16), cs_g, preferred_element_type=F32)
            dxdt_parts.append(jnp.concatenate(dxdt_h, axis=1) + g1 * d_s_x[:, gw])
            dbs_parts.append(dbs)
            dcs_parts.append(dcs)
            ds_scr[:, gw] = dst[:, gw] * cd_x[:, gw] + dst_new
        dxdt = jnp.concatenate(dxdt_parts, axis=1)
        y_off = jnp.concatenate(off_parts, axis=1)
        g1 = jnp.concatenate(g1_parts, axis=1)
        dds = _split_dot(g1 * xdt, e_mat_t, 2)
        da_off = _split_dot(w_x * y_off, e_mat_t, 2)
        ddt_x = _split_dot(dxdt * xs, e_mat_t, 2)
        dcd_lane = jnp.sum(dst * st_b.astype(F32), axis=0, keepdims=True)
        dcd = _split_dot(jnp.broadcast_to(dcd_lane, (SUB, D_B)), e_mat_t, 2)[0:1]
        dd_ref[...] += _split_dot(jnp.broadcast_to(dd_lane, (SUB, D_B)), e_mat_t, 2)[0:1]
        t_ds = dds * d_s
        d_a = da_diag + da_off - t_ds
        last_row = jnp.sum(t_ds, axis=0, keepdims=True) + dcd * e_a[CHUNK - 1:CHUNK, :]
        d_a = d_a + jnp.where(rows == CHUNK - 1, last_row, 0.0)
        dda = _split_dot_left(_tri(False).astype(BF16), d_a, 3)
        ddt = dda * a + ddt_x
        dal_ref[...] += jnp.sum(dda * dt * a, axis=0, keepdims=True)
        ddtraw = jnp.where(valid_col, ddt * jax.nn.sigmoid(dtraw + dtb_ref[...]), 0.0)
        ddtraw = jnp.where(lane < N_HEADS, ddtraw, 0.0)
        ddtb_ref[...] += jnp.sum(ddtraw, axis=0, keepdims=True)
        dpw_ref[:, D_B:D_B + LANE] = ddtraw.astype(BF16)
        dxs = dxs + dxdt * dt_x
        dxbc = jnp.concatenate([dxs] + dbs_parts + dcs_parts, axis=1)
        dxbc_ref[...] = jnp.where(valid_col, dxbc, 0.0).astype(BF16)

    def row(width, col):
        return pl.BlockSpec((CHUNK, width), lambda b, c: (b * cpe + cpe - 1 - c, col))

    def const(shape):
        return pl.BlockSpec(shape, lambda b, c: (0,) * len(shape))

    return _grid_call(
        body, name, (nb, cpe),
        [row(D_B, 0), row(D_B, 0), row(N_XBC, 0), row(D_B, COL_BZ // D_B), row(LANE, COL_DT // LANE),
         pl.BlockSpec((None, N_STATE, D_B), lambda b, c: (b * cpe + cpe - 1 - c, 0, 0)),
         const((1, LANE)), const((N_HEADS, 1)), const((1, LANE)), const((N_HEADS, 1)),
         const((1, D_B)), const((1, D_B)), const((LANE, D_B)), const((D_B, LANE))],
        [row(N_XBC, 0),
         pl.BlockSpec((pl.Element(CHUNK), pl.Element(D_B + LANE)), lambda b, c: ((b * cpe + cpe - 1 - c) * CHUNK, COL_BZ)),
         const((1, D_B)), const((1, LANE)), const((1, LANE)), const((1, LANE))],
        [jax.ShapeDtypeStruct((t, N_XBC), BF16), jax.ShapeDtypeStruct((t, N_INP), BF16),
         jax.ShapeDtypeStruct((1, D_B), F32),
         jax.ShapeDtypeStruct((1, LANE), F32), jax.ShapeDtypeStruct((1, LANE), F32),
         jax.ShapeDtypeStruct((1, LANE), F32)],
        [pltpu.VMEM((N_STATE, D_B), F32)],
        (dy, y_ssd, xbc, proj, proj, states, dtb_p, dtb_t, alog_p, alog_t, d_skip_x, norm_g, e_mat, e_mat_t), comm)


HBM_SPEC = pl.BlockSpec(memory_space=pl.ANY)


def _mesh_pos():
    return lax.axis_index("x"), lax.axis_index("y"), lax.axis_index("c")


def _allgather(arrays, name):
    n = len(arrays)

    def body(*refs):
        xs, outs = refs[:n], refs[n:2 * n]
        send_sems, recv_sems, local_sems = refs[2 * n:]
        x, y, c = _mesh_pos()
        me, sibling = (x, y, c), (x, y, 1 - c)
        chips = [(1 - x, y), (x, 1 - y), (1 - x, 1 - y)]

        def slot(px, py, pc):
            return 4 * px + 2 * py + pc

        def copy(a, k, block, to, src=None):
            dst = outs[a].at[slot(*block)]
            return pltpu.make_async_remote_copy(
                src_ref=dst if src is None else src, dst_ref=dst, send_sem=send_sems.at[a, k], recv_sem=recv_sems.at[a, k],
                device_id=to, device_id_type=MESH)

        mine = [pltpu.make_async_copy(xs[a], outs[a].at[slot(*me)], local_sems.at[a]) for a in range(n)]
        for cp in mine:
            cp.start()
        first = []
        for a in range(n):
            first.append(copy(a, 0, me, sibling, src=xs[a]))
            first += [copy(a, 1 + j, me, (*chip, c), src=xs[a]) for j, chip in enumerate(chips)]
        for cp in first:
            cp.start()
        passed = []
        for j, chip in enumerate(chips):
            for a in range(n):
                copy(a, 1 + j, (*chip, c), me).wait_recv()
                cp = copy(a, 4 + j, (*chip, c), sibling)
                cp.start()
                passed.append(cp)
        for a in range(n):
            copy(a, 0, sibling, me).wait_recv()
        for j, chip in enumerate(chips):
            for a in range(n):
                copy(a, 4 + j, (*chip, 1 - c), me).wait_recv()
        for cp in first + passed:
            cp.wait_send()
        for cp in mine:
            cp.wait()

    return pl.pallas_call(
        body, name=name,
        in_specs=[HBM_SPEC] * n, out_specs=[HBM_SPEC] * n,
        out_shape=[jax.ShapeDtypeStruct((N_DEV,) + a.shape, a.dtype) for a in arrays],
        scratch_shapes=[pltpu.SemaphoreType.DMA((n, 7)), pltpu.SemaphoreType.DMA((n, 7)), pltpu.SemaphoreType.DMA((n,))],
    )(*arrays)


def _remote(src, dst, send_sem, recv_sem, to):
    return pltpu.make_async_remote_copy(src_ref=src, dst_ref=dst, send_sem=send_sem, recv_sem=recv_sem,
                                        device_id=to, device_id_type=MESH)


def _slot_plain(d):
    return d


def _slot_mix_rows(d):
    return jnp.where(d < 2, d + 4, jnp.where(d < 6, d - 2, d))


def _ag_ici(pieces, slots):
    n = len(pieces)

    def copies(ins, outs, sems, with_recvs):
        send_sems, recv_sems, local_sems = sems
        x, y, c = _mesh_pos()
        local, sends, recvs = [], [], []
        for a in range(n):
            mine = outs[a].at[slots[a](4 * x + 2 * y + c)]
            local.append(pltpu.make_async_copy(ins[a], mine, local_sems.at[a]))
            for k, (px, py) in enumerate([(1 - x, y), (x, 1 - y), (1 - x, 1 - y)]):
                sends.append(_remote(ins[a], mine, send_sems.at[a, k], recv_sems.at[a, k], (px, py, c)))
                if with_recvs:
                    theirs = outs[a].at[slots[a](4 * px + 2 * py + c)]
                    recvs.append(_remote(ins[a], theirs, send_sems.at[a, k], recv_sems.at[a, k], (px, py, c)))
        return local, sends, recvs

    return _Comm(pieces, [jax.ShapeDtypeStruct((N_DEV,) + p.shape, p.dtype) for p in pieces],
                 [pltpu.SemaphoreType.DMA((n, 3)), pltpu.SemaphoreType.DMA((n, 3)), pltpu.SemaphoreType.DMA((n,))], copies)


def _ag_d2d(bufs, slots):
    n = len(bufs)

    def copies(ins, outs, sems, with_recvs):
        send_sems, recv_sems = sems
        x, y, c = _mesh_pos()
        chips = [(x, y), (1 - x, y), (x, 1 - y), (1 - x, 1 - y)]
        sends, recvs = [], []
        for a in range(n):
            for k, (px, py) in enumerate(chips):
                held = slots[a](4 * px + 2 * py + c)
                sends.append(_remote(ins[a].at[held], outs[a].at[held], send_sems.at[a, k], recv_sems.at[a, k], (x, y, 1 - c)))
                if with_recvs:
                    got = slots[a](4 * px + 2 * py + 1 - c)
                    recvs.append(_remote(ins[a].at[got], outs[a].at[got], send_sems.at[a, k], recv_sems.at[a, k], (x, y, 1 - c)))
        return [], sends, recvs

    return _Comm(bufs, [jax.ShapeDtypeStruct(b.shape, b.dtype) for b in bufs],
                 [pltpu.SemaphoreType.DMA((n, 4)), pltpu.SemaphoreType.DMA((n, 4))], copies,
                 aliases={a: a for a in range(n)})


def _rs_d2d(blocks, slots):
    n = len(blocks)

    def copies(ins, outs, sems, with_recvs):
        send_sems, recv_sems = sems
        x, y, c = _mesh_pos()
        sends, recvs = [], []
        for a in range(n):
            for j in range(4):
                src = ins[a].at[slots[a](2 * j + 1 - c)]
                sends.append(_remote(src, outs[a].at[j], send_sems.at[a, j], recv_sems.at[a, j], (x, y, 1 - c)))
                if with_recvs:
                    recvs.append(_remote(src, outs[a].at[j], send_sems.at[a, j], recv_sems.at[a, j], (x, y, 1 - c)))
        return [], sends, recvs

    return _Comm(blocks, [jax.ShapeDtypeStruct((4,) + b.shape[1:], b.dtype) for b in blocks],
                 [pltpu.SemaphoreType.DMA((n, 4)), pltpu.SemaphoreType.DMA((n, 4))], copies)


def _rs_ici(blocks):
    n = len(blocks)

    def copies(ins, outs, sems, with_recvs):
        send_sems, recv_sems = sems
        x, y, c = _mesh_pos()
        sends, recvs = [], []
        for a in range(n):
            for k, (px, py) in enumerate([(1 - x, y), (x, 1 - y), (1 - x, 1 - y)]):
                src = ins[a].at[2 * px + py]
                sends.append(_remote(src, outs[a].at[k], send_sems.at[a, k], recv_sems.at[a, k], (px, py, c)))
                if with_recvs:
                    recvs.append(_remote(src, outs[a].at[k], send_sems.at[a, k], recv_sems.at[a, k], (px, py, c)))
        return [], sends, recvs

    return _Comm(blocks, [jax.ShapeDtypeStruct((3,) + b.shape[1:], b.dtype) for b in blocks],
                 [pltpu.SemaphoreType.DMA((n, 3)), pltpu.SemaphoreType.DMA((n, 3))], copies)


def _run_comm(comm, name):
    n_in, n_out = len(comm.inputs), len(comm.out_shapes)

    def body(*refs):
        ins, outs, sems = refs[:n_in], refs[n_in:n_in + n_out], refs[n_in + n_out:]
        comm.start(ins, outs, sems)
        comm.wait(ins, outs, sems)

    return pl.pallas_call(
        body, name=name, in_specs=[HBM_SPEC] * n_in, out_specs=[HBM_SPEC] * n_out, out_shape=comm.out_shapes,
        scratch_shapes=comm.sem_shapes, input_output_aliases=comm.aliases,
    )(*comm.inputs)


W_ROWS = 784


def _w_segments():
    per = N_IN // N_DEV
    out = []
    for d in range(N_DEV):
        lo, hi = per * d, per * (d + 1)
        for a, b, start in COL_MAP:
            s, e = max(lo, a), min(hi, b)
            if s < e:
                out.append((d, s - lo, e - s, s - a + start))
    return out


def _w_gather_rows(g, name):
    tk = D_MODEL // 2
    u32 = jnp.uint32

    def body(g_ref, o_ref, scr):
        for d in range(N_DEV):
            x32 = pltpu.bitcast(g_ref[d], u32)
            for dd, src, rows, dst in _w_segments():
                if dd == d:
                    scr[dst // 2:(dst + rows) // 2, :] = x32[src // 2:(src + rows) // 2]
        scr[(COL_DT + N_HEADS) // 2:COL_XBC // 2, :] = jnp.zeros(((COL_XBC - COL_DT - N_HEADS) // 2, tk), u32)
        o_ref[...] = pltpu.bitcast(scr[...], BF16)

    return pl.pallas_call(
        body, name=name, grid=(D_MODEL // tk,),
        in_specs=[pl.BlockSpec((N_DEV, W_ROWS, tk), lambda j: (0, 0, j))],
        out_specs=pl.BlockSpec((N_INP, tk), lambda j: (0, j)),
        out_shape=jax.ShapeDtypeStruct((N_INP, D_MODEL), BF16),
        scratch_shapes=[pltpu.VMEM((N_INP // 2, tk), u32)],
        compiler_params=_cparams(("parallel",)),
    )(g)


def _w_split_rows(dwt, name):
    tk = D_MODEL // 4
    per = N_IN // N_DEV

    def body(w_ref, o_ref):
        for d, dst, rows, src in _w_segments():
            o_ref[d, dst:dst + rows, :] = w_ref[src:src + rows, :]
        for d in range(N_DEV):
            o_ref[d, per:W_ROWS, :] = jnp.zeros((W_ROWS - per, tk), F32)

    return pl.pallas_call(
        body, name=name, grid=(D_MODEL // tk,),
        in_specs=[pl.BlockSpec((N_INP, tk), lambda j: (0, j))],
        out_specs=pl.BlockSpec((N_DEV, W_ROWS, tk), lambda j: (0, 0, j)),
        out_shape=jax.ShapeDtypeStruct((N_DEV, W_ROWS, D_MODEL), F32),
        compiler_params=_cparams(("parallel",)),
    )(dwt)


def _rs_pair_sum(g, ra, own_slots, name):
    _, rows, cols = g.shape

    def body(s_ref, g_ref, ra_ref, o_ref):
        o_ref[...] = (g_ref[...] + ra_ref[...]).astype(BF16)

    return pl.pallas_call(
        body, name=name,
        grid_spec=pltpu.PrefetchScalarGridSpec(
            num_scalar_prefetch=1, grid=(4,),
            in_specs=[pl.BlockSpec((None, rows, cols), lambda j, s: (s[j], 0, 0)),
                      pl.BlockSpec((None, rows, cols), lambda j, s: (j, 0, 0))],
            out_specs=pl.BlockSpec((None, rows, cols), lambda j, s: (j, 0, 0))),
        out_shape=jax.ShapeDtypeStruct((4, rows, cols), BF16),
        compiler_params=_cparams(("parallel",)),
    )(own_slots, g, ra)


def _rs_final_sum(h, rb, chip_idx, name):
    _, rows, cols = h.shape

    def body(j_ref, h_ref, rb_ref, o_ref):
        o_ref[...] = ((h_ref[...].astype(F32) + rb_ref[0].astype(F32)) + rb_ref[1].astype(F32)) + rb_ref[2].astype(F32)

    return pl.pallas_call(
        body, name=name,
        grid_spec=pltpu.PrefetchScalarGridSpec(
            num_scalar_prefetch=1, grid=(1,),
            in_specs=[pl.BlockSpec((None, rows, cols), lambda i, j: (j[0], 0, 0)),
                      pl.BlockSpec((3, rows, cols), lambda i, j: (0, 0, 0))],
            out_specs=pl.BlockSpec((rows, cols), lambda i, j: (0, 0))),
        out_shape=jax.ShapeDtypeStruct((rows, cols), F32),
        compiler_params=_cparams(("arbitrary",)),
    )(chip_idx, h, rb)


def _adamw_math(w, g, m, v):
    m = ADAM_B1 * m + (1.0 - ADAM_B1) * g
    v = ADAM_B2 * v + (1.0 - ADAM_B2) * (g * g)
    m_hat = m / (1.0 - ADAM_B1 ** ADAM_STEP)
    v_hat = v / (1.0 - ADAM_B2 ** ADAM_STEP)
    delta = -ADAM_LR * (m_hat / (jnp.sqrt(v_hat) + ADAM_EPS) + ADAM_WD * w)
    return delta, m, v


def _adamw_rows(g, w, m, v, tr, name):
    rows, cols = w.shape

    def body(g_ref, w_ref, m_ref, v_ref, d_out, m_out, v_out):
        delta, m_new, v_new = _adamw_math(w_ref[...], g_ref[...], m_ref[...], v_ref[...])
        d_out[...] = delta
        m_out[...] = m_new
        v_out[...] = v_new

    flat = pl.BlockSpec((tr, cols), lambda r: (r, 0))
    return pl.pallas_call(
        body, name=name, grid=(rows // tr,),
        in_specs=[flat] * 4, out_specs=[flat] * 3,
        out_shape=[jax.ShapeDtypeStruct((rows, cols), F32)] * 3,
        compiler_params=_cparams(("parallel",)),
    )(g, w, m, v)


def _sum_devices(parts, name):
    _, p, _ = parts.shape

    def body(x_ref, o_ref):
        acc = x_ref[0]
        for d in range(1, N_DEV):
            acc = acc + x_ref[d]
        o_ref[...] = acc

    return pl.pallas_call(
        body, name=name, grid=(1,),
        in_specs=[pl.BlockSpec((N_DEV, p, LANE), lambda i: (0, 0, 0))],
        out_specs=pl.BlockSpec((p, LANE), lambda i: (0, 0)),
        out_shape=jax.ShapeDtypeStruct((p, LANE), F32),
        compiler_params=_cparams(("arbitrary",)),
    )(parts)


def _adamw_small(g, w, m, v, name):
    p = g.shape[0]

    def body(g_ref, w_ref, m_ref, v_ref, d_out, m_out, v_out):
        delta, m_new, v_new = _adamw_math(w_ref[...], g_ref[...], m_ref[...], v_ref[...])
        d_out[...] = delta
        m_out[...] = m_new
        v_out[...] = v_new

    spec = pl.BlockSpec((p, LANE), lambda i: (0, 0))
    return pl.pallas_call(
        body, name=name, grid=(1,),
        in_specs=[spec] * 4, out_specs=[spec] * 3,
        out_shape=[jax.ShapeDtypeStruct((p, LANE), F32)] * 3,
        compiler_params=_cparams(("arbitrary",)),
    )(g, w, m, v)


PACK_ALIGN = SUB * LANE

SMALL_PARAMS = (
    ("meta", (N_META, D_MODEL), 1),
    ("pre_g", (DEPTH, D_MODEL), None),
    ("post_g", (DEPTH, D_MODEL), None),
    ("conv_a_w", (DEPTH, CONV_A_K, D_A), 2),
    ("ssm_conv_w", (DEPTH, SSM_K, N_XBC), 2),
    ("ssm_conv_b", (DEPTH, N_XBC), None),
    ("dt_bias", (DEPTH, N_HEADS), None),
    ("a_log", (DEPTH, N_HEADS), None),
    ("d_skip", (DEPTH, N_HEADS), None),
    ("ssm_norm_g", (DEPTH, D_B), None),
    ("conf_conv_w", (DEPTH, CONF_K, D_C), 2),
    ("conf_conv_b", (DEPTH, D_C), None),
    ("conf_ln_g", (DEPTH, D_C), None),
    ("conf_ln_b", (DEPTH, D_C), None),
)


def _local_shape(shape, axis):
    if axis is None:
        return shape
    return tuple(s // N_DEV if k == axis else s for k, s in enumerate(shape))


def _pack(arrays):
    flat = []
    for a in arrays:
        v = a.reshape(-1).astype(F32)
        flat.append(v)
        if v.shape[0] % PACK_ALIGN:
            flat.append(jnp.zeros(((-v.shape[0]) % PACK_ALIGN,), F32))
    return jnp.concatenate(flat).reshape(-1, LANE)


def _unpack(buf, shapes):
    flat = buf.reshape(-1)
    out, off = [], 0
    for s in shapes:
        size = 1
        for k in s:
            size *= k
        out.append(flat[off:off + size].reshape(s))
        off += size + (-size) % PACK_ALIGN
    return out


def kernel(x, meta, pre_g, post_g, w_in, w_out, conv_a_w, ssm_conv_w, ssm_conv_b, dt_bias, a_log, d_skip, ssm_norm_g, conf_conv_w, conf_conv_b, conf_ln_g, conf_ln_b, loss_target, m_meta, m_pre_g, m_post_g, m_w_in, m_w_out, m_conv_a_w, m_ssm_conv_w, m_ssm_conv_b, m_dt_bias, m_a_log, m_d_skip, m_ssm_norm_g, m_conf_conv_w, m_conf_conv_b, m_conf_ln_g, m_conf_ln_b, v_meta, v_pre_g, v_post_g, v_w_in, v_w_out, v_conv_a_w, v_ssm_conv_w, v_ssm_conv_b, v_dt_bias, v_a_log, v_d_skip, v_ssm_norm_g, v_conf_conv_w, v_conf_conv_b, v_conf_ln_g, v_conf_ln_b):
    weights = dict(meta=meta, pre_g=pre_g, post_g=post_g, conv_a_w=conv_a_w, ssm_conv_w=ssm_conv_w, ssm_conv_b=ssm_conv_b,
                   dt_bias=dt_bias, a_log=a_log, d_skip=d_skip, ssm_norm_g=ssm_norm_g, conf_conv_w=conf_conv_w,
                   conf_conv_b=conf_conv_b, conf_ln_g=conf_ln_g, conf_ln_b=conf_ln_b)
    mom1 = dict(meta=m_meta, pre_g=m_pre_g, post_g=m_post_g, conv_a_w=m_conv_a_w, ssm_conv_w=m_ssm_conv_w,
                ssm_conv_b=m_ssm_conv_b, dt_bias=m_dt_bias, a_log=m_a_log, d_skip=m_d_skip, ssm_norm_g=m_ssm_norm_g,
                conf_conv_w=m_conf_conv_w, conf_conv_b=m_conf_conv_b, conf_ln_g=m_conf_ln_g, conf_ln_b=m_conf_ln_b)
    mom2 = dict(meta=v_meta, pre_g=v_pre_g, post_g=v_post_g, conv_a_w=v_conv_a_w, ssm_conv_w=v_ssm_conv_w,
                ssm_conv_b=v_ssm_conv_b, dt_bias=v_dt_bias, a_log=v_a_log, d_skip=v_d_skip, ssm_norm_g=v_ssm_norm_g,
                conf_conv_w=v_conf_conv_w, conf_conv_b=v_conf_conv_b, conf_ln_g=v_conf_ln_g, conf_ln_b=v_conf_ln_b)
    nb, seq, d = x.shape
    lp = PAD + N_META + seq
    t = nb * lp
    assert lp % TILE == 0 and t % (3 * LANE) == 0 and d == D_MODEL
    rt = lp // 3
    xi, yi, ci = _mesh_pos()
    dev = 4 * xi + 2 * yi + ci
    ci32 = ci.astype(jnp.int32)
    chip_idx = (2 * xi + yi).astype(jnp.int32).reshape(1)
    own_plain = jnp.stack([2 * j + ci32 for j in range(4)])
    own_mix = jnp.stack([_slot_mix_rows(2 * j + ci32) for j in range(4)]).astype(jnp.int32)
    n_in_loc = N_IN // N_DEV
    n_out_loc = 2 * D_MODEL // N_DEV
    slots = (_slot_plain, _slot_mix_rows)

    sharded_small = [n for n, _, ax in SMALL_PARAMS if ax is not None]
    small_shapes = {n: s for n, s, _ in SMALL_PARAMS}
    small_axis = {n: ax for n, _, ax in SMALL_PARAMS}
    sw_pack = _pack([weights[n] for n in sharded_small])
    (sw_g,) = _allgather([sw_pack], "ag_small_weights")
    wt_loc = jnp.pad(jnp.swapaxes(w_in, 1, 2).astype(BF16), ((0, 0), (0, W_ROWS - n_in_loc), (0, 0)))
    wo_loc = w_out.astype(BF16)

    def gather_finish(bufs):
        g_in_t, g_out_rows = _run_comm(_ag_d2d(list(bufs), slots), "ag_d2d")
        return _w_gather_rows(g_in_t, "w_gather_rows"), g_out_rows.reshape(2 * D_MODEL, D_MODEL)

    full = dict(weights)
    per_dev = [_unpack(sw_g[k], [_local_shape(small_shapes[n], small_axis[n]) for n in sharded_small]) for k in range(N_DEV)]
    for q, n in enumerate(sharded_small):
        full[n] = jnp.concatenate([per_dev[k][q] for k in range(N_DEV)], axis=small_axis[n])

    e_mat = (lax.broadcasted_iota(jnp.int32, (LANE, D_B), 0) == lax.broadcasted_iota(jnp.int32, (LANE, D_B), 1) // HEAD_DIM)
    e_mat = e_mat.astype(BF16)
    e_mat_t = e_mat.T

    front = jnp.concatenate([jnp.zeros((PAD, d), F32), full["meta"]], axis=0)
    h = jnp.concatenate([jnp.concatenate([front, x[b]], axis=0) for b in range(nb)], axis=0)
    saved = []
    w_t, w_o = gather_finish(_run_comm(_ag_ici([wt_loc[0], wo_loc[0]], slots), "ag_ici"))
    for i in range(DEPTH):
        row = lambda a: a[i].reshape(1, -1)
        hn = _rms_fwd(h, row(pre_g), rt, "rms_fwd")
        proj = _mm(hn, w_t, "nt", BF16, t // 3, MM_TN, D_MODEL, "mm_proj", cols_outer=True)
        ycat = _a_fwd(proj, full["conv_a_w"][i], lp, "a_fwd")
        xbc = _xbc_fwd(proj, full["ssm_conv_w"][i], row(ssm_conv_b), lp, "xbc_fwd")
        ycat = _c_fwd(proj, full["conf_conv_w"][i], row(conf_conv_b), row(conf_ln_g), row(conf_ln_b), ycat, lp, "c_fwd")
        d_skip_x = jnp.repeat(d_skip[i], HEAD_DIM).reshape(1, D_B)
        nxt = _ag_ici([wt_loc[i + 1], wo_loc[i + 1]], slots) if i + 1 < DEPTH else None
        ycat, yssd, states, *bufs = _ssd_fwd(xbc, proj, dt_bias[i], a_log[i], d_skip_x, row(ssm_norm_g), e_mat, ycat, lp,
                                             "ssd_fwd", comm=nxt)
        m = _mm(ycat, w_o, "nn", F32, t // 3, D_MODEL, 2 * D_MODEL, "mm_out")
        saved.append((h, hn, proj, ycat, yssd, states, xbc, m, d_skip_x, w_t, w_o))
        h = _post_fwd(h, m, row(post_g), rt, "post_fwd")
        if bufs:
            w_t, w_o = gather_finish(bufs)

    dh, loss_blk = _loss_kernel(h, loss_target.reshape(nb * seq, d), lp, "loss")

    grads = {n: [None] * DEPTH for n, _, _ in SMALL_PARAMS if n != "meta"}
    gt_in, gr_out = [None] * DEPTH, [None] * DEPTH
    dmeta = None
    pending = None

    def rs_pair_sums(blocks):
        ra_in, ra_out = _run_comm(_rs_d2d(list(blocks), slots), "rs_d2d")
        return [_rs_pair_sum(blocks[0], ra_in, own_plain, "rs_pair_sum_in"),
                _rs_pair_sum(blocks[1], ra_out, own_mix, "rs_pair_sum_out")]

    def rs_finish(layer, hs, rbs):
        gt_in[layer] = _rs_final_sum(hs[0], rbs[0], chip_idx, "rs_final_sum_in")
        gr_out[layer] = _rs_final_sum(hs[1], rbs[1], chip_idx, "rs_final_sum_out")

    for i in reversed(range(DEPTH)):
        row = lambda a: a[i].reshape(1, -1)
        h_i, hn, proj, ycat, yssd, states, xbc, m, d_skip_x, w_t, w_o = saved[i]
        dm, grads["post_g"][i] = _post_bwd(dh, m, row(post_g), rt, "post_bwd")
        dy = _mm(dm, w_o, "nt", BF16, t // 3, D_MODEL, D_MODEL, "mm_dy", cols_outer=True)
        dw_out = _mm(ycat, dm, "tn", F32, D_MODEL, D_MODEL, t // 3, "mm_dwout")
        hs = rs_pair_sums(pending) if pending is not None else None
        dxbc, dproj, grads["ssm_norm_g"][i], ddtb, dal, dds, *rbs = _ssd_bwd(
            dy, yssd, xbc, proj, states, dt_bias[i], a_log[i], d_skip_x, row(ssm_norm_g), e_mat, e_mat_t, lp, "ssd_bwd",
            comm=_rs_ici(hs) if hs is not None else None)
        if hs is not None:
            rs_finish(i + 1, hs, rbs)
        grads["dt_bias"][i] = ddtb[:, :N_HEADS]
        grads["a_log"][i] = dal[:, :N_HEADS]
        grads["d_skip"][i] = dds[:, :N_HEADS]
        dproj, grads["conv_a_w"][i] = _a_bwd(dy, proj, full["conv_a_w"][i], dproj, lp, "a_bwd")
        dproj, grads["ssm_conv_w"][i], grads["ssm_conv_b"][i] = _xbc_bwd(
            dxbc, proj, full["ssm_conv_w"][i], row(ssm_conv_b), dproj, lp, "xbc_bwd")
        dproj, grads["conf_conv_w"][i], grads["conf_conv_b"][i], grads["conf_ln_g"][i], grads["conf_ln_b"][i] = _c_bwd(
            dy, proj, full["conf_conv_w"][i], row(conf_conv_b), row(conf_ln_g), row(conf_ln_b), dproj, lp, "c_bwd")
        dhn = _mm(dproj, w_t, "nn", F32, MM_TM, D_MODEL, N_INP, "mm_dhn")
        dw_in_t = _mm(dproj, hn, "tn", F32, MM_TN, D_MODEL, t, "mm_dwin")
        dh, grads["pre_g"][i], dmeta = _rms_bwd(dh, dhn, h_i, row(pre_g), lp, rt, "rms_bwd")
        pending = (_w_split_rows(dw_in_t, "w_split_rows"), dw_out.reshape(N_DEV, n_out_loc, D_MODEL))
    grad_x = dh.reshape(nb, lp, d)[:, PAD + N_META:]
    hs = rs_pair_sums(pending)
    rs_finish(0, hs, _run_comm(_rs_ici(hs), "rs_ici"))

    g_w_in = jnp.swapaxes(jnp.stack(gt_in)[:, :n_in_loc, :], 1, 2).reshape(DEPTH * D_MODEL, n_in_loc)
    g_w_out = jnp.stack(gr_out).reshape(DEPTH * n_out_loc, D_MODEL)
    big = {}
    big["w_in"] = [g_w_in, *_adamw_rows(g_w_in, w_in.reshape(-1, n_in_loc), m_w_in.reshape(-1, n_in_loc),
                                        v_w_in.reshape(-1, n_in_loc), 512, "adamw_w_in")]
    big["w_out"] = [g_w_out, *_adamw_rows(g_w_out, w_out.reshape(-1, D_MODEL), m_w_out.reshape(-1, D_MODEL),
                                          v_w_out.reshape(-1, D_MODEL), 256, "adamw_w_out")]
    big = {n: [a.reshape(s) for a in outs] for (n, outs), s in zip(big.items(), (w_in.shape, w_out.shape))}

    names = [n for n, _, _ in SMALL_PARAMS]
    partial = [loss_blk[0:1, 0:1], dmeta] + [jnp.concatenate(grads[n], axis=0) for n in names[1:]]
    part_pack = _pack(partial)
    (parts_g,) = _allgather([part_pack], "ag_small_grads")
    total = _unpack(_sum_devices(parts_g, "sum_small_grads"), [(1,)] + [small_shapes[n] for n in names])
    loss = total[0][0]
    g_small = {}
    for n, g in zip(names, total[1:]):
        ax = small_axis[n]
        if ax is not None:
            g = lax.dynamic_slice_in_dim(g, dev * (small_shapes[n][ax] // N_DEV), small_shapes[n][ax] // N_DEV, axis=ax)
        g_small[n] = g
    loc_shapes = [_local_shape(small_shapes[n], small_axis[n]) for n in names]
    d_pack, m_pack, v_pack = _adamw_small(_pack([g_small[n] for n in names]), _pack([weights[n] for n in names]),
                                          _pack([mom1[n] for n in names]), _pack([mom2[n] for n in names]), "adamw_small")
    d_small = dict(zip(names, _unpack(d_pack, loc_shapes)))
    m_small = dict(zip(names, _unpack(m_pack, loc_shapes)))
    v_small = dict(zip(names, _unpack(v_pack, loc_shapes)))

    order = ["meta", "pre_g", "post_g", "w_in", "w_out", "conv_a_w", "ssm_conv_w", "ssm_conv_b", "dt_bias", "a_log",
             "d_skip", "ssm_norm_g", "conf_conv_w", "conf_conv_b", "conf_ln_g", "conf_ln_b"]

    def pick(k, small):
        return [big[n][k] if n in big else small[n] for n in order]

    return (loss, grad_x, *pick(0, g_small), *pick(1, d_small), *pick(2, m_small), *pick(3, v_small))
```

```python
import functools

import jax
import jax.numpy as jnp
from jax import lax
from jax.experimental import pallas as pl
from jax.experimental.pallas import tpu as pltpu

F32 = jnp.float32
BF16 = jnp.bfloat16

D_MODEL = 1024
DEPTH = 4
SEQ = 2048
CHUNK = 64
N_META = 16
PAD = 48
LP = PAD + N_META + SEQ
D_A = 512
D_B = 1024
D_C = 512
N_HEADS = 16
HEAD_DIM = 64
N_STATE = 128
N_GROUPS = 2
GROUP_W = D_B // N_GROUPS
N_XBC = D_B + 2 * N_GROUPS * N_STATE
CONV_A_K = 3
SSM_K = 4
CONF_K = 31
NORM_EPS = 1e-6
LN_EPS = 1e-5
N_IN = 6160
N_INP = 6272
COL_BZ = 2048
COL_DT = 3072
COL_XBC = 3200
COL_C = 4736
COL_MAP = ((0, 3072, 0), (3072, 4608, COL_XBC), (4608, 4624, COL_DT), (4624, 6160, COL_C))
LANE = 128
SUB = 8

ADAM_LR = 0.001
ADAM_B1 = 0.9
ADAM_B2 = 0.999
ADAM_EPS = 1e-08
ADAM_WD = 0.01
ADAM_STEP = 10

TILE = 192
HALO_A = 16
HALO_C = 32
MM_TM = 384
MM_TN = 896
VMEM_LIMIT = 56 * 1024 * 1024

MESH = pl.DeviceIdType.MESH
N_DEV = 8


def _silu(x):
    return x * jax.nn.sigmoid(x)


def _dsilu(x):
    s = jax.nn.sigmoid(x)
    return s * (1.0 + x * (1.0 - s))


def _cparams(sem=None):
    return pltpu.CompilerParams(dimension_semantics=sem, vmem_limit_bytes=VMEM_LIMIT)


def _mm(a, b, mode, out_dtype, tm, tn, tk, name, cols_outer=False, comm=None):
    def ix(f):
        if cols_outer:
            return lambda j, i, q: f(i, j, q)
        return f

    if mode == "nn":
        (m, k), (_, n) = a.shape, b.shape
        a_spec = pl.BlockSpec((tm, tk), ix(lambda i, j, q: (i, q)))
        b_spec = pl.BlockSpec((tk, tn), ix(lambda i, j, q: (q, j)))
        dims = (((1,), (0,)), ((), ()))
    elif mode == "nt":
        (m, k), (n, _) = a.shape, b.shape
        a_spec = pl.BlockSpec((tm, tk), ix(lambda i, j, q: (i, q)))
        b_spec = pl.BlockSpec((tn, tk), ix(lambda i, j, q: (j, q)))
        dims = (((1,), (1,)), ((), ()))
    else:
        (k, m), (_, n) = a.shape, b.shape
        a_spec = pl.BlockSpec((tk, tm), ix(lambda i, j, q: (q, i)))
        b_spec = pl.BlockSpec((tk, tn), ix(lambda i, j, q: (q, j)))
        dims = (((0,), (0,)), ((), ()))
    assert m % tm == 0 and n % tn == 0 and k % tk == 0, (name, a.shape, b.shape)
    nk = k // tk
    grid = (n // tn, m // tm, nk) if cols_outer else (m // tm, n // tn, nk)

    def body(a_ref, b_ref, o_ref, acc_ref):
        part = lax.dot_general(a_ref[...].astype(BF16), b_ref[...].astype(BF16), dims, preferred_element_type=F32)
        if nk == 1:
            o_ref[...] = part.astype(o_ref.dtype)
        else:
            q = pl.program_id(2)

            @pl.when(q == 0)
            def _():
                acc_ref[...] = part

            @pl.when(q > 0)
            def _():
                acc_ref[...] += part

            @pl.when(q == nk - 1)
            def _():
                o_ref[...] = acc_ref[...].astype(o_ref.dtype)

    res = _grid_call(
        body, name, grid, [a_spec, b_spec], [pl.BlockSpec((tm, tn), ix(lambda i, j, q: (i, j)))],
        [jax.ShapeDtypeStruct((m, n), out_dtype)], [pltpu.VMEM((tm, tn) if nk > 1 else (SUB, LANE), F32)], (a, b), comm)
    return res[0] if comm is None else res


def _row_mask(i, tpe, rows):
    r = lax.broadcasted_iota(jnp.int32, (rows, 1), 0)
    return jnp.logical_or((i % tpe) != 0, r >= PAD)


def _rms_fwd(h, g, rt, name):
    t, d = h.shape

    def body(h_ref, g_ref, o_ref):
        x = h_ref[...]
        r = lax.rsqrt(jnp.mean(x * x, axis=-1, keepdims=True) + NORM_EPS)
        o_ref[...] = (x * r * g_ref[...]).astype(BF16)

    return pl.pallas_call(
        body, name=name, grid=(t // rt,),
        in_specs=[pl.BlockSpec((rt, d), lambda i: (i, 0)), pl.BlockSpec((1, d), lambda i: (0, 0))],
        out_specs=pl.BlockSpec((rt, d), lambda i: (i, 0)),
        out_shape=jax.ShapeDtypeStruct((t, d), BF16),
        compiler_params=_cparams(("parallel",)),
    )(h, g)


def _post_fwd(h, m, g, rt, name):
    t, d = h.shape

    def body(h_ref, m_ref, g_ref, o_ref):
        x = m_ref[...]
        r = lax.rsqrt(jnp.mean(x * x, axis=-1, keepdims=True) + NORM_EPS)
        o_ref[...] = h_ref[...] + x * r * g_ref[...]

    row = pl.BlockSpec((rt, d), lambda i: (i, 0))
    return pl.pallas_call(
        body, name=name, grid=(t // rt,),
        in_specs=[row, row, pl.BlockSpec((1, d), lambda i: (0, 0))], out_specs=row,
        out_shape=jax.ShapeDtypeStruct((t, d), F32),
        compiler_params=_cparams(("parallel",)),
    )(h, m, g)


def _rms_bwd_math(x, g, dy):
    r = lax.rsqrt(jnp.mean(x * x, axis=-1, keepdims=True) + NORM_EPS)
    gdy = dy * g
    dx = r * gdy - x * (r * r * r) * jnp.mean(gdy * x, axis=-1, keepdims=True)
    return dx, dy * x * r


def _post_bwd(dh, m, g, rt, name):
    t, d = dh.shape

    def body(dh_ref, m_ref, g_ref, dm_ref, dg_ref):
        dm, dgt = _rms_bwd_math(m_ref[...], g_ref[...], dh_ref[...])
        dm_ref[...] = dm

        @pl.when(pl.program_id(0) == 0)
        def _():
            dg_ref[...] = jnp.zeros_like(dg_ref)

        dg_ref[...] += jnp.sum(dgt, axis=0, keepdims=True)

    row = pl.BlockSpec((rt, d), lambda i: (i, 0))
    vec = pl.BlockSpec((1, d), lambda i: (0, 0))
    return pl.pallas_call(
        body, name=name, grid=(t // rt,),
        in_specs=[row, row, vec], out_specs=[row, vec],
        out_shape=[jax.ShapeDtypeStruct((t, d), F32), jax.ShapeDtypeStruct((1, d), F32)],
        compiler_params=_cparams(("arbitrary",)),
    )(dh, m, g)


def _rms_bwd(dh_res, dhn, h, g, lp, rt, name):
    t, d = h.shape
    tpe = lp // rt

    def body(dr_ref, dn_ref, h_ref, g_ref, dh_ref, dg_ref, dmeta_ref):
        i = pl.program_id(0)
        dx, dgt = _rms_bwd_math(h_ref[...], g_ref[...], dn_ref[...])
        dh = dr_ref[...] + dx
        dh_ref[...] = dh

        @pl.when(i == 0)
        def _():
            dg_ref[...] = jnp.zeros_like(dg_ref)
            dmeta_ref[...] = jnp.zeros_like(dmeta_ref)

        dg_ref[...] += jnp.sum(dgt, axis=0, keepdims=True)

        @pl.when((i % tpe) == 0)
        def _():
            dmeta_ref[...] += dh[PAD:PAD + N_META, :]

    row = pl.BlockSpec((rt, d), lambda i: (i, 0))
    vec = pl.BlockSpec((1, d), lambda i: (0, 0))
    return pl.pallas_call(
        body, name=name, grid=(t // rt,),
        in_specs=[row, row, row, vec],
        out_specs=[row, vec, pl.BlockSpec((N_META, d), lambda i: (0, 0))],
        out_shape=[jax.ShapeDtypeStruct((t, d), F32), jax.ShapeDtypeStruct((1, d), F32),
                   jax.ShapeDtypeStruct((N_META, d), F32)],
        compiler_params=_cparams(("arbitrary",)),
    )(dh_res, dhn, h, g)


def _loss_kernel(h, target, lp, name):
    t, d = h.shape
    cpe = lp // CHUNK
    nb = t // lp

    def body(h_ref, t_ref, dh_ref, loss_ref):
        b, j = pl.program_id(0), pl.program_id(1)

        @pl.when(jnp.logical_and(b == 0, j == 0))
        def _():
            loss_ref[...] = jnp.zeros_like(loss_ref)

        @pl.when(j == 0)
        def _():
            dh_ref[...] = jnp.zeros_like(dh_ref)

        @pl.when(j > 0)
        def _():
            err = h_ref[...] - t_ref[...]
            dh_ref[...] = err * (1.0 / d)
            loss_ref[...] += (0.5 / d) * jnp.sum(err * err)

    return pl.pallas_call(
        body, name=name, grid=(nb, cpe),
        in_specs=[pl.BlockSpec((CHUNK, d), lambda b, j: (b * cpe + j, 0)),
                  pl.BlockSpec((CHUNK, d), lambda b, j: (b * (cpe - 1) + jnp.maximum(j - 1, 0), 0))],
        out_specs=[pl.BlockSpec((CHUNK, d), lambda b, j: (b * cpe + j, 0)),
                   pl.BlockSpec((SUB, LANE), lambda b, j: (0, 0))],
        out_shape=[jax.ShapeDtypeStruct((t, d), F32), jax.ShapeDtypeStruct((SUB, LANE), F32)],
        compiler_params=_cparams(("arbitrary", "arbitrary")),
    )(h, target)


def _window_spec(rows, width, col):
    return pl.BlockSpec((pl.Element(rows), pl.Element(width)), lambda i: (i * rows, col))


def _halo_specs(t, width, col, halo):
    cur = _window_spec(TILE, width, col)
    prev = pl.BlockSpec((pl.Element(halo), pl.Element(width)),
                        lambda i: (pl.multiple_of(jnp.maximum(i * TILE - halo, 0), halo), col))
    nxt = pl.BlockSpec((pl.Element(halo), pl.Element(width)),
                       lambda i: (pl.multiple_of(jnp.minimum((i + 1) * TILE, t - halo), halo), col))
    return cur, prev, nxt


def _f32(ref, lo, hi):
    return ref[:, lo:hi].astype(F32)


def _rows_from(x, start, rows):
    s = start % SUB
    if s == 0:
        return x[start:start + rows]
    return pltpu.roll(x, x.shape[0] - s, axis=0)[start - s:start - s + rows]


def _conv_a(ve, w):
    rows = ve.shape[0] - HALO_A
    return (w[0:1] * _rows_from(ve, HALO_A - 2, rows) + w[1:2] * _rows_from(ve, HALO_A - 1, rows)
            + w[2:3] * ve[HALO_A:HALO_A + rows])


def _a_fwd(proj, w, lp, name):
    t = proj.shape[0]
    tpe = lp // TILE
    cur, prev, _ = _halo_specs(t, 4 * D_A, 0, HALO_A)

    def body(p_ref, ph_ref, w_ref, y_ref):
        first = (pl.program_id(0) % tpe) == 0
        v = _f32(p_ref, D_A, 2 * D_A) * _f32(p_ref, 2 * D_A, 3 * D_A)
        vh = jnp.where(first, 0.0, _f32(ph_ref, D_A, 2 * D_A) * _f32(ph_ref, 2 * D_A, 3 * D_A))
        cv = _conv_a(jnp.concatenate([vh, v], axis=0), w_ref[...])
        y_ref[...] = (_f32(p_ref, 0, D_A) * cv * _silu(_f32(p_ref, 3 * D_A, 4 * D_A))).astype(BF16)

    return pl.pallas_call(
        body, name=name, grid=(t // TILE,),
        in_specs=[cur, prev, pl.BlockSpec((CONV_A_K, D_A), lambda i: (0, 0))],
        out_specs=_window_spec(TILE, D_A, D_B),
        out_shape=jax.ShapeDtypeStruct((t, D_B + D_A + D_C), BF16),
        compiler_params=_cparams(("parallel",)),
    )(proj, proj, w)


def _a_bwd(dy, proj, w, dproj, lp, name):
    t = proj.shape[0]
    tpe = lp // TILE
    cur, prev, nxt = _halo_specs(t, 4 * D_A, 0, HALO_A)
    dcur, _, dnxt = _halo_specs(t, D_A, D_B, HALO_A)

    def body(dy_ref, dyn_ref, p_ref, ph_ref, pn_ref, w_ref, dproj_ref, dp_ref, dw_ref):
        i = pl.program_id(0)
        first = (i % tpe) == 0
        last = (i % tpe) == tpe - 1
        w = w_ref[...]
        ab, ac, ax, az = (_f32(p_ref, k * D_A, (k + 1) * D_A) for k in range(4))
        v = ac * ax
        vh = jnp.where(first, 0.0, _f32(ph_ref, D_A, 2 * D_A) * _f32(ph_ref, 2 * D_A, 3 * D_A))
        ve = jnp.concatenate([vh, v], axis=0)
        taps = [_rows_from(ve, HALO_A - 2 + k, TILE) for k in range(CONV_A_K)]
        cv = w[0:1] * taps[0] + w[1:2] * taps[1] + w[2:3] * taps[2]
        s = _silu(az)
        dy_ = dy_ref[...].astype(F32)
        dcv = dy_ * ab * s
        dcvn = jnp.where(last, 0.0, dyn_ref[...].astype(F32) * _f32(pn_ref, 0, D_A) * _silu(_f32(pn_ref, 3 * D_A, 4 * D_A)))
        dce = jnp.concatenate([dcv, dcvn], axis=0)
        dv = w[2:3] * dce[0:TILE] + w[1:2] * _rows_from(dce, 1, TILE) + w[0:1] * _rows_from(dce, 2, TILE)
        dp = jnp.concatenate([dy_ * cv * s, dv * ax, dv * ac, dy_ * ab * cv * _dsilu(az)], axis=1)
        dp_ref[...] = jnp.where(_row_mask(i, tpe, TILE), dp, 0.0).astype(BF16)
        dw = jnp.concatenate(
            [jnp.sum(dcv * taps[k], axis=0, keepdims=True) for k in range(CONV_A_K)], axis=0)

        @pl.when(i == 0)
        def _():
            dw_ref[...] = jnp.zeros_like(dw_ref)

        dw_ref[...] += dw

    wspec = pl.BlockSpec((CONV_A_K, D_A), lambda i: (0, 0))
    return pl.pallas_call(
        body, name=name, grid=(t // TILE,),
        in_specs=[dcur, dnxt, cur, prev, nxt, wspec, HBM_SPEC],
        out_specs=[_window_spec(TILE, 4 * D_A, 0), wspec],
        out_shape=[jax.ShapeDtypeStruct(dproj.shape, dproj.dtype), jax.ShapeDtypeStruct((CONV_A_K, D_A), F32)],
        input_output_aliases={6: 0},
        compiler_params=_cparams(("arbitrary",)),
    )(dy, dy, proj, proj, proj, w, dproj)


def _conv_ssm(xe, w, rows, off):
    acc = w[0:1] * _rows_from(xe, off - 3, rows)
    for k in range(1, SSM_K):
        acc = acc + w[k:k + 1] * _rows_from(xe, off - 3 + k, rows)
    return acc


def _xbc_fwd(proj, w, b, lp, name):
    t = proj.shape[0]
    tpe = lp // TILE
    cur, prev, _ = _halo_specs(t, N_XBC, COL_XBC, HALO_A)

    def body(x_ref, xh_ref, w_ref, b_ref, o_ref):
        first = (pl.program_id(0) % tpe) == 0
        xh = jnp.where(first, 0.0, xh_ref[...].astype(F32))
        xe = jnp.concatenate([xh, x_ref[...].astype(F32)], axis=0)
        o_ref[...] = _silu(_conv_ssm(xe, w_ref[...], TILE, HALO_A) + b_ref[...]).astype(BF16)

    return pl.pallas_call(
        body, name=name, grid=(t // TILE,),
        in_specs=[cur, prev, pl.BlockSpec((SSM_K, N_XBC), lambda i: (0, 0)), pl.BlockSpec((1, N_XBC), lambda i: (0, 0))],
        out_specs=pl.BlockSpec((TILE, N_XBC), lambda i: (i, 0)),
        out_shape=jax.ShapeDtypeStruct((t, N_XBC), BF16),
        compiler_params=_cparams(("parallel",)),
    )(proj, proj, w, b)


def _xbc_bwd(dxbc, proj, w, b, dproj, lp, name):
    t = proj.shape[0]
    tpe = lp // TILE
    cur, prev, nxt = _halo_specs(t, N_XBC, COL_XBC, HALO_A)
    dcur, _, dnxt = _halo_specs(t, N_XBC, 0, HALO_A)

    def body(d_ref, dn_ref, x_ref, xh_ref, xn_ref, w_ref, b_ref, dproj_ref, dx_ref, dw_ref, db_ref):
        i = pl.program_id(0)
        first = (i % tpe) == 0
        last = (i % tpe) == tpe - 1
        w = w_ref[...]
        xh = jnp.where(first, 0.0, xh_ref[...].astype(F32))
        xe = jnp.concatenate([xh, x_ref[...].astype(F32), xn_ref[...].astype(F32)], axis=0)
        taps = [_rows_from(xe, HALO_A - 3 + k, TILE + HALO_A) for k in range(SSM_K)]
        pre = b_ref[...] + w[0:1] * taps[0]
        for k in range(1, SSM_K):
            pre = pre + w[k:k + 1] * taps[k]
        de = jnp.concatenate([d_ref[...].astype(F32), jnp.where(last, 0.0, dn_ref[...].astype(F32))], axis=0)
        dpre = de * _dsilu(pre)
        dx = w[3:4] * dpre[0:TILE]
        for k in range(SSM_K - 1):
            dx = dx + w[k:k + 1] * _rows_from(dpre, 3 - k, TILE)
        dx_ref[...] = jnp.where(_row_mask(i, tpe, TILE), dx, 0.0).astype(BF16)
        dpc = dpre[0:TILE]
        dw = jnp.concatenate(
            [jnp.sum(dpc * taps[k][0:TILE], axis=0, keepdims=True) for k in range(SSM_K)], axis=0)

        @pl.when(i == 0)
        def _():
            dw_ref[...] = jnp.zeros_like(dw_ref)
            db_ref[...] = jnp.zeros_like(db_ref)

        dw_ref[...] += dw
        db_ref[...] += jnp.sum(dpc, axis=0, keepdims=True)

    wspec = pl.BlockSpec((SSM_K, N_XBC), lambda i: (0, 0))
    bspec = pl.BlockSpec((1, N_XBC), lambda i: (0, 0))
    return pl.pallas_call(
        body, name=name, grid=(t // TILE,),
        in_specs=[dcur, dnxt, cur, prev, nxt, wspec, bspec, HBM_SPEC],
        out_specs=[_window_spec(TILE, N_XBC, COL_XBC), wspec, bspec],
        out_shape=[jax.ShapeDtypeStruct(dproj.shape, dproj.dtype), jax.ShapeDtypeStruct((SSM_K, N_XBC), F32),
                   jax.ShapeDtypeStruct((1, N_XBC), F32)],
        input_output_aliases={7: 0},
        compiler_params=_cparams(("arbitrary",)),
    )(dxbc, dxbc, proj, proj, proj, w, b, dproj)


SUBROWS = 32


def _fill_shifted(scr, x):
    scr[0] = x
    for s in range(1, SUB):
        scr[s] = pltpu.roll(x, x.shape[0] - s, axis=0)


def _window(scr, start, rows):
    s = start % SUB
    return scr[s, start - s:start - s + rows, :]


def _conv_conf(scr, w, rows, off, base):
    acc = w[0:1] * _window(scr, base + off - (CONF_K - 1), rows)
    for k in range(1, CONF_K):
        acc = acc + w[k:k + 1] * _window(scr, base + off - (CONF_K - 1) + k, rows)
    return acc


def _ln_fwd(u1, g, b):
    mu = jnp.mean(u1, axis=-1, keepdims=True)
    xc = u1 - mu
    rstd = lax.rsqrt(jnp.mean(xc * xc, axis=-1, keepdims=True) + LN_EPS)
    n = xc * rstd
    return n, rstd, n * g + b


def _c_fwd(proj, w, cb, g, b, ybuf, lp, name):
    t = proj.shape[0]
    tpe = lp // TILE
    cur, prev, _ = _halo_specs(t, 3 * D_C, COL_C, HALO_C)

    def body(p_ref, ph_ref, w_ref, cb_ref, g_ref, b_ref, ybuf_ref, y_ref, u0_scr):
        first = (pl.program_id(0) % tpe) == 0
        u0h = jnp.where(first, 0.0, _f32(ph_ref, 0, D_C) * jax.nn.sigmoid(_f32(ph_ref, D_C, 2 * D_C)))
        _fill_shifted(u0_scr, jnp.concatenate([u0h, _f32(p_ref, 0, D_C) * jax.nn.sigmoid(_f32(p_ref, D_C, 2 * D_C))], axis=0))
        w = w_ref[...]
        for r0 in range(0, TILE, SUBROWS):
            u1 = _conv_conf(u0_scr, w, SUBROWS, HALO_C, r0) + cb_ref[...]
            _, _, u2 = _ln_fwd(u1, g_ref[...], b_ref[...])
            cz = p_ref[r0:r0 + SUBROWS, 2 * D_C:3 * D_C].astype(F32)
            y_ref[r0:r0 + SUBROWS, :] = (_silu(u2) * _silu(cz)).astype(BF16)

    vec = pl.BlockSpec((1, D_C), lambda i: (0, 0))
    return pl.pallas_call(
        body, name=name, grid=(t // TILE,),
        in_specs=[cur, prev, pl.BlockSpec((CONF_K, D_C), lambda i: (0, 0)), vec, vec, vec, HBM_SPEC],
        out_specs=_window_spec(TILE, D_C, D_B + D_A),
        out_shape=jax.ShapeDtypeStruct(ybuf.shape, ybuf.dtype),
        scratch_shapes=[pltpu.VMEM((SUB, HALO_C + TILE, D_C), F32)],
        input_output_aliases={6: 0},
        compiler_params=_cparams(("parallel",)),
    )(proj, proj, w, cb, g, b, ybuf)


def _c_bwd(dy, proj, w, cb, g, b, dproj, lp, name):
    t = proj.shape[0]
    tpe = lp // TILE
    cur, prev, nxt = _halo_specs(t, 3 * D_C, COL_C, HALO_C)
    dcur, _, dnxt = _halo_specs(t, D_C, D_B + D_A, HALO_C)
    ext = TILE + HALO_C

    def body(dy_ref, dyn_ref, p_ref, ph_ref, pn_ref, w_ref, cb_ref, g_ref, b_ref, dproj_ref,
             dp_ref, dw_ref, dcb_ref, dg_ref, db_ref, u0_scr, du1_scr, wacc_scr):
        i = pl.program_id(0)
        first = (i % tpe) == 0
        last = (i % tpe) == tpe - 1
        w = w_ref[...]

        @pl.when(i == 0)
        def _():
            dw_ref[...] = jnp.zeros_like(dw_ref)
            dcb_ref[...] = jnp.zeros_like(dcb_ref)
            dg_ref[...] = jnp.zeros_like(dg_ref)
            db_ref[...] = jnp.zeros_like(db_ref)

        u0h = jnp.where(first, 0.0, _f32(ph_ref, 0, D_C) * jax.nn.sigmoid(_f32(ph_ref, D_C, 2 * D_C)))
        _fill_shifted(u0_scr, jnp.concatenate(
            [u0h, _f32(p_ref, 0, D_C) * jax.nn.sigmoid(_f32(p_ref, D_C, 2 * D_C)),
             _f32(pn_ref, 0, D_C) * jax.nn.sigmoid(_f32(pn_ref, D_C, 2 * D_C))], axis=0))
        dcb = jnp.zeros((1, D_C), F32)
        dg = jnp.zeros((1, D_C), F32)
        db = jnp.zeros((1, D_C), F32)
        for r0 in range(0, ext, SUBROWS):
            in_tile = r0 < TILE
            src, dsrc, q0 = (p_ref, dy_ref, r0) if in_tile else (pn_ref, dyn_ref, r0 - TILE)
            u1 = _conv_conf(u0_scr, w, SUBROWS, HALO_C, r0) + cb_ref[...]
            n, rstd, u2 = _ln_fwd(u1, g_ref[...], b_ref[...])
            cz = src[q0:q0 + SUBROWS, 2 * D_C:3 * D_C].astype(F32)
            dyc = dsrc[q0:q0 + SUBROWS, :].astype(F32)
            if not in_tile:
                dyc = jnp.where(last, 0.0, dyc)
            du2 = dyc * _silu(cz) * _dsilu(u2)
            dn = du2 * g_ref[...]
            du1 = rstd * (dn - jnp.mean(dn, axis=-1, keepdims=True) - n * jnp.mean(dn * n, axis=-1, keepdims=True))
            du1_scr[0, r0:r0 + SUBROWS, :] = du1
            if in_tile:
                dp_ref[r0:r0 + SUBROWS, 2 * D_C:3 * D_C] = (dyc * _silu(u2) * _dsilu(cz)).astype(BF16)
                dg = dg + jnp.sum(du2 * n, axis=0, keepdims=True)
                db = db + jnp.sum(du2, axis=0, keepdims=True)
                dcb = dcb + jnp.sum(du1, axis=0, keepdims=True)
        dcb_ref[...] += dcb
        dg_ref[...] += dg
        db_ref[...] += db
        mask = _row_mask(i, tpe, TILE)
        _fill_shifted(du1_scr, du1_scr[0])
        for r0 in range(0, TILE, SUBROWS):
            acc = w[0:1] * _window(du1_scr, r0 + CONF_K - 1, SUBROWS)
            for k in range(1, CONF_K):
                acc = acc + w[k:k + 1] * _window(du1_scr, r0 + CONF_K - 1 - k, SUBROWS)
            ca = p_ref[r0:r0 + SUBROWS, 0:D_C].astype(F32)
            sg = jax.nn.sigmoid(p_ref[r0:r0 + SUBROWS, D_C:2 * D_C].astype(F32))
            m = mask[r0:r0 + SUBROWS]
            dp_ref[r0:r0 + SUBROWS, 0:D_C] = jnp.where(m, acc * sg, 0.0).astype(BF16)
            dp_ref[r0:r0 + SUBROWS, D_C:2 * D_C] = jnp.where(m, acc * ca * sg * (1.0 - sg), 0.0).astype(BF16)
        for k in range(CONF_K):
            part = jnp.zeros((SUB, D_C), F32)
            for r0 in range(0, TILE, SUBROWS):
                prod = du1_scr[0, r0:r0 + SUBROWS, :] * _window(u0_scr, HALO_C + r0 - (CONF_K - 1) + k, SUBROWS)
                for q in range(0, SUBROWS, SUB):
                    part = part + prod[q:q + SUB]
            wacc_scr[k:k + 1, :] = jnp.sum(part, axis=0, keepdims=True)
        dw_ref[...] += wacc_scr[0:CONF_K, :]

    vec = pl.BlockSpec((1, D_C), lambda i: (0, 0))
    wspec = pl.BlockSpec((CONF_K, D_C), lambda i: (0, 0))
    return pl.pallas_call(
        body, name=name, grid=(t // TILE,),
        in_specs=[dcur, dnxt, cur, prev, nxt, wspec, vec, vec, vec, HBM_SPEC],
        out_specs=[_window_spec(TILE, 3 * D_C, COL_C), wspec, vec, vec, vec],
        out_shape=[jax.ShapeDtypeStruct(dproj.shape, dproj.dtype), jax.ShapeDtypeStruct((CONF_K, D_C), F32),
                   jax.ShapeDtypeStruct((1, D_C), F32), jax.ShapeDtypeStruct((1, D_C), F32),
                   jax.ShapeDtypeStruct((1, D_C), F32)],
        scratch_shapes=[pltpu.VMEM((SUB, ext + HALO_C, D_C), F32), pltpu.VMEM((SUB, ext, D_C), F32),
                        pltpu.VMEM((HALO_C, D_C), F32)],
        input_output_aliases={9: 0},
        compiler_params=_cparams(("arbitrary",)),
    )(dy, dy, proj, proj, proj, w, cb, g, b, dproj)


def _split_dot(x, m_bf16, terms):
    acc = None
    rem = x
    for _ in range(terms):
        hi = rem.astype(BF16)
        part = jnp.dot(hi, m_bf16, preferred_element_type=F32)
        acc = part if acc is None else acc + part
        rem = rem - hi.astype(F32)
    return acc


def _split_dot_left(m_bf16, x, terms):
    acc = None
    rem = x
    for _ in range(terms):
        hi = rem.astype(BF16)
        part = jnp.dot(m_bf16, hi, preferred_element_type=F32)
        acc = part if acc is None else acc + part
        rem = rem - hi.astype(F32)
    return acc


def _tri(rows_ge_cols):
    r = lax.broadcasted_iota(jnp.int32, (CHUNK, CHUNK), 0)
    c = lax.broadcasted_iota(jnp.int32, (CHUNK, CHUNK), 1)
    return (r >= c) if rows_ge_cols else (r <= c)


def _softplus(x):
    return jnp.maximum(x, 0.0) + jnp.log(1.0 + jnp.exp(-jnp.abs(x)))


def _ssd_common(dtraw, dtb, dtb_t, a_log, a_log_t, e_mat, valid_col, valid_row):
    a = -jnp.exp(a_log)
    lane = lax.broadcasted_iota(jnp.int32, (1, LANE), 1)
    a = jnp.where(lane < N_HEADS, a, 0.0)
    a_t = -jnp.exp(a_log_t)
    dt = jnp.where(valid_col, _softplus(dtraw + dtb), 0.0)
    dt = jnp.where(lane < N_HEADS, dt, 0.0)
    dt_t = jnp.where(valid_row, _softplus(dtraw.T[0:N_HEADS, :] + dtb_t), 0.0)
    ltri = _tri(True).astype(BF16)
    utri = _tri(False).astype(BF16)
    big_a = _split_dot_left(ltri, dt * a, 3)
    big_a_t = _split_dot(dt_t * a_t, utri, 3)
    e_a = jnp.exp(big_a)
    d_s = jnp.exp(big_a[CHUNK - 1:CHUNK, :] - big_a)
    dt_x = _split_dot(dt, e_mat, 2)
    e_a_x = _split_dot(e_a, e_mat, 2)
    d_s_x = _split_dot(d_s, e_mat, 2)
    cd_x = e_a_x[CHUNK - 1:CHUNK, :]
    return a, dt, big_a, big_a_t, e_a, d_s, dt_x, e_a_x, d_s_x, cd_x


def _decay(big_a, big_a_t, h, transposed):
    col = big_a[:, h:h + 1]
    row = big_a_t[h:h + 1, :]
    if not transposed:
        seg = col - row
        return jnp.where(_tri(True), jnp.exp(jnp.minimum(seg, 0.0)), 0.0)
    seg = row - col
    return jnp.where(_tri(False), jnp.exp(jnp.minimum(seg, 0.0)), 0.0)


NT_DIMS = (((1,), (1,)), ((), ()))
TN_DIMS = (((0,), (0,)), ((), ()))
HBM_SPEC = pl.BlockSpec(memory_space=pl.ANY)


class _Comm:
    def __init__(self, inputs, out_shapes, sem_shapes, copies, aliases=None):
        self.inputs, self.out_shapes, self.sem_shapes, self.copies = inputs, out_shapes, sem_shapes, copies
        self.aliases = aliases or {}

    def start(self, ins, outs, sems):
        local, sends, _ = self.copies(ins, outs, sems, False)
        for cp in local + sends:
            cp.start()

    def wait(self, ins, outs, sems):
        local, sends, recvs = self.copies(ins, outs, sems, True)
        for cp in recvs:
            cp.wait_recv()
        for cp in sends:
            cp.wait_send()
        for cp in local:
            cp.wait()


def _grid_call(body, name, grid, in_specs, out_specs, out_shape, scratch_shapes, operands, comm=None, aliases=None):
    aliases = dict(aliases or {})
    if comm is None:
        return pl.pallas_call(
            body, name=name, grid=grid, in_specs=in_specs, out_specs=out_specs, out_shape=out_shape,
            scratch_shapes=scratch_shapes, input_output_aliases=aliases,
            compiler_params=_cparams(("arbitrary",) * len(grid)))(*operands)
    n_in, n_out, n_scr = len(in_specs), len(out_specs), len(scratch_shapes)
    nci, nco = len(comm.inputs), len(comm.out_shapes)

    def wrapped(*refs):
        ins, cins = refs[:n_in], refs[n_in:n_in + nci]
        o0 = n_in + nci
        outs, couts = refs[o0:o0 + n_out], refs[o0 + n_out:o0 + n_out + nco]
        s0 = o0 + n_out + nco
        scr, csems = refs[s0:s0 + n_scr], refs[s0 + n_scr:]
        first = pl.program_id(0) == 0
        last = pl.program_id(0) == grid[0] - 1
        for k in range(1, len(grid)):
            first = jnp.logical_and(first, pl.program_id(k) == 0)
            last = jnp.logical_and(last, pl.program_id(k) == grid[k] - 1)

        @pl.when(first)
        def _():
            comm.start(cins, couts, csems)

        body(*ins, *outs, *scr)

        @pl.when(last)
        def _():
            comm.wait(cins, couts, csems)

    res = pl.pallas_call(
        wrapped, name=name, grid=grid,
        in_specs=list(in_specs) + [HBM_SPEC] * nci, out_specs=list(out_specs) + [HBM_SPEC] * nco,
        out_shape=list(out_shape) + list(comm.out_shapes),
        scratch_shapes=list(scratch_shapes) + list(comm.sem_shapes),
        input_output_aliases={**aliases, **{n_in + k: n_out + v for k, v in comm.aliases.items()}},
        compiler_params=_cparams(("arbitrary",) * len(grid)))(*operands, *comm.inputs)
    return res


def _ssd_fwd(xbc, proj, dtb, a_log, d_skip_x, norm_g, e_mat, ybuf, lp, name, comm=None):
    t = xbc.shape[0]
    cpe = lp // CHUNK
    nb = t // lp
    dtb_p = jnp.pad(dtb.reshape(1, N_HEADS), ((0, 0), (0, LANE - N_HEADS)))
    alog_p = jnp.pad(a_log.reshape(1, N_HEADS), ((0, 0), (0, LANE - N_HEADS)))
    dtb_t = dtb.reshape(N_HEADS, 1)
    alog_t = a_log.reshape(N_HEADS, 1)

    def body(xbc_ref, bz_ref, dt_ref, dtb_ref, dtbt_ref, al_ref, alt_ref, dx_ref, g_ref, e_ref, ybuf_ref,
             yb_ref, ys_ref, st_ref, s_scr):
        c = pl.program_id(0)

        @pl.when(c == 0)
        def _():
            s_scr[...] = jnp.zeros_like(s_scr)

        rows = lax.broadcasted_iota(jnp.int32, (CHUNK, 1), 0)
        cols = lax.broadcasted_iota(jnp.int32, (1, CHUNK), 1)
        valid_col = jnp.logical_or(c > 0, rows >= PAD)
        valid_row = jnp.logical_or(c > 0, cols >= PAD)
        e_mat = e_ref[...]
        for b in range(nb):
            _, _, big_a, big_a_t, _, _, dt_x, e_a_x, d_s_x, cd_x = _ssd_common(
                dt_ref[b].astype(F32), dtb_ref[...], dtbt_ref[...], al_ref[...], alt_ref[...], e_mat, valid_col, valid_row)
            xs = xbc_ref[b, :, 0:D_B].astype(F32)
            bs = xbc_ref[b, :, D_B:D_B + N_GROUPS * N_STATE]
            cs = xbc_ref[b, :, D_B + N_GROUPS * N_STATE:N_XBC]
            xdt = xs * dt_x
            xdt_b = xdt.astype(BF16)
            st_prev = s_scr[b]
            st_ref[b] = st_prev.astype(BF16)
            st_b = st_prev.astype(BF16)
            u_b = (xdt * d_s_x).astype(BF16)
            y_parts = []
            for g in range(N_GROUPS):
                gs = slice(g * N_STATE, (g + 1) * N_STATE)
                gw = slice(g * GROUP_W, (g + 1) * GROUP_W)
                cb = lax.dot_general(cs[:, gs], bs[:, gs], NT_DIMS, preferred_element_type=F32)
                y_off = jnp.dot(cs[:, gs], st_b[:, gw], preferred_element_type=F32)
                diag = []
                for e in range(N_HEADS // N_GROUPS):
                    h = g * (N_HEADS // N_GROUPS) + e
                    m = (cb * _decay(big_a, big_a_t, h, False)).astype(BF16)
                    diag.append(jnp.dot(m, xdt_b[:, h * HEAD_DIM:(h + 1) * HEAD_DIM], preferred_element_type=F32))
                y_parts.append(jnp.concatenate(diag, axis=1) + y_off * e_a_x[:, gw])
                new_st = lax.dot_general(bs[:, gs], u_b[:, gw], TN_DIMS, preferred_element_type=F32)
                s_scr[b, :, gw] = st_prev[:, gw] * cd_x[:, gw] + new_st
            y = jnp.concatenate(y_parts, axis=1) + xs * dx_ref[...]
            ys_ref[b] = y
            z = y * _silu(bz_ref[b].astype(F32))
            r = lax.rsqrt(jnp.mean(z * z, axis=-1, keepdims=True) + NORM_EPS)
            yb_ref[b] = (z * r * g_ref[...]).astype(BF16)

    def row(width, col):
        return pl.BlockSpec((nb, CHUNK, width), lambda c: (0, c, col))

    def const(shape):
        return pl.BlockSpec(shape, lambda c: (0,) * len(shape))

    proj3 = proj.reshape(nb, lp, N_INP)
    ybuf, yssd, states, *rest = _grid_call(
        body, name, (cpe,),
        [row(N_XBC, 0), row(D_B, COL_BZ // D_B), row(LANE, COL_DT // LANE),
         const((1, LANE)), const((N_HEADS, 1)), const((1, LANE)), const((N_HEADS, 1)),
         const((1, D_B)), const((1, D_B)), const((LANE, D_B)), HBM_SPEC],
        [row(D_B, 0), row(D_B, 0), pl.BlockSpec((nb, None, N_STATE, D_B), lambda c: (0, c, 0, 0))],
        [jax.ShapeDtypeStruct((nb, lp, ybuf.shape[1]), ybuf.dtype), jax.ShapeDtypeStruct((nb, lp, D_B), F32),
         jax.ShapeDtypeStruct((nb, cpe, N_STATE, D_B), BF16)],
        [pltpu.VMEM((nb, N_STATE, D_B), F32)],
        (xbc.reshape(nb, lp, N_XBC), proj3, proj3, dtb_p, dtb_t, alog_p, alog_t, d_skip_x, norm_g, e_mat,
         ybuf.reshape(nb, lp, ybuf.shape[1])), comm, aliases={10: 0})
    return (ybuf.reshape(t, -1), yssd.reshape(t, D_B), states, *rest)


def _ssd_bwd(dy, y_ssd, xbc, proj, states, dtb, a_log, d_skip_x, norm_g, e_mat, e_mat_t, lp, name, comm=None):
    t = xbc.shape[0]
    cpe = lp // CHUNK
    nb = t // lp
    hpg = N_HEADS // N_GROUPS
    dtb_p = jnp.pad(dtb.reshape(1, N_HEADS), ((0, 0), (0, LANE - N_HEADS)))
    alog_p = jnp.pad(a_log.reshape(1, N_HEADS), ((0, 0), (0, LANE - N_HEADS)))
    dtb_t = dtb.reshape(N_HEADS, 1)
    alog_t = a_log.reshape(N_HEADS, 1)

    def body(dy_ref, ys_ref, xbc_ref, bz_ref, dt_ref, st_ref, dtb_ref, dtbt_ref, al_ref, alt_ref, dx_ref, g_ref,
             e_ref, et_ref, dxbc_ref, dpw_ref, dg_ref, ddtb_ref, dal_ref, dd_ref, ds_scr):
        @pl.when(pl.program_id(0) == 0)
        def _():
            ds_scr[...] = jnp.zeros_like(ds_scr)
            dg_ref[...] = jnp.zeros_like(dg_ref)
            ddtb_ref[...] = jnp.zeros_like(ddtb_ref)
            dal_ref[...] = jnp.zeros_like(dal_ref)
            dd_ref[...] = jnp.zeros_like(dd_ref)

        for b in range(nb):
            one_example(dy_ref.at[b], ys_ref.at[b], xbc_ref.at[b], bz_ref.at[b], dt_ref.at[b], st_ref.at[b], dtb_ref,
                        dtbt_ref, al_ref, alt_ref, dx_ref, g_ref, e_ref, et_ref, dxbc_ref.at[b], dpw_ref.at[b], dg_ref,
                        ddtb_ref, dal_ref, dd_ref, ds_scr.at[b])

    def one_example(dy_ref, ys_ref, xbc_ref, bz_ref, dt_ref, st_ref, dtb_ref, dtbt_ref, al_ref, alt_ref, dx_ref, g_ref,
                    e_ref, et_ref, dxbc_ref, dpw_ref, dg_ref, ddtb_ref, dal_ref, dd_ref, ds_scr):
        cc = cpe - 1 - pl.program_id(0)
        rows = lax.broadcasted_iota(jnp.int32, (CHUNK, 1), 0)
        cols = lax.broadcasted_iota(jnp.int32, (1, CHUNK), 1)
        valid_col = jnp.logical_or(cc > 0, rows >= PAD)
        valid_row = jnp.logical_or(cc > 0, cols >= PAD)
        e_mat = e_ref[...]
        e_mat_t = et_ref[...]
        dtraw = dt_ref[...].astype(F32)
        a, dt, big_a, big_a_t, e_a, d_s, dt_x, e_a_x, d_s_x, cd_x = _ssd_common(
            dtraw, dtb_ref[...], dtbt_ref[...], al_ref[...], alt_ref[...], e_mat, valid_col, valid_row)
        xs = xbc_ref[:, 0:D_B].astype(F32)
        bs = xbc_ref[:, D_B:D_B + N_GROUPS * N_STATE]
        cs = xbc_ref[:, D_B + N_GROUPS * N_STATE:N_XBC]
        xdt = xs * dt_x
        xdt_b = xdt.astype(BF16)
        st_b = st_ref[...]
        dst = ds_scr[...]
        dst_b = dst.astype(BF16)

        ys = ys_ref[...]
        bz = bz_ref[...].astype(F32)
        sil = _silu(bz)
        z = ys * sil
        dz, dgt = _rms_bwd_math(z, g_ref[...], dy_ref[...].astype(F32))
        dg_ref[...] += jnp.sum(dgt, axis=0, keepdims=True)
        dpw_ref[:, 0:D_B] = (dz * ys * _dsilu(bz)).astype(BF16)
        dys = dz * sil

        dd_lane = jnp.sum(dys * xs, axis=0, keepdims=True)
        dxs = dys * dx_ref[...]
        w_x = dys * e_a_x
        w_b = w_x.astype(BF16)
        dys_b = dys.astype(BF16)
        u_b = (xdt * d_s_x).astype(BF16)
        dxdt_parts, dbs_parts, dcs_parts, off_parts, g1_parts = [], [], [], [], []
        da_diag = jnp.zeros((CHUNK, LANE), F32)
        lane = lax.broadcasted_iota(jnp.int32, (1, LANE), 1)
        for g in range(N_GROUPS):
            gs = slice(g * N_STATE, (g + 1) * N_STATE)
            gw = slice(g * GROUP_W, (g + 1) * GROUP_W)
            cs_g, bs_g = cs[:, gs], bs[:, gs]
            dcs = lax.dot_general(w_b[:, gw], st_b[:, gw], NT_DIMS, preferred_element_type=F32)
            y_off = jnp.dot(cs_g, st_b[:, gw], preferred_element_type=F32)
            off_parts.append(y_off)
            dst_new = lax.dot_general(cs_g, w_b[:, gw], TN_DIMS, preferred_element_type=F32)
            g1 = jnp.dot(bs_g, dst_b[:, gw], preferred_element_type=F32)
            g1_parts.append(g1)
            dbs = lax.dot_general(u_b[:, gw], dst_b[:, gw], NT_DIMS, preferred_element_type=F32)
            cb = lax.dot_general(cs_g, bs_g, NT_DIMS, preferred_element_type=F32)
            cbt = lax.dot_general(bs_g, cs_g, NT_DIMS, preferred_element_type=F32)
            dcb = jnp.zeros((CHUNK, CHUNK), F32)
            dcbt = jnp.zeros((CHUNK, CHUNK), F32)
            dxdt_h = []
            for e in range(hpg):
                h = g * hpg + e
                hs = slice(h * HEAD_DIM, (h + 1) * HEAD_DIM)
                dec = _decay(big_a, big_a_t, h, False)
                dect = _decay(big_a, big_a_t, h, True)
                m = cb * dec
                mt = cbt * dect
                dxdt_h.append(jnp.dot(mt.astype(BF16), dys_b[:, hs], preferred_element_type=F32))
                dm = lax.dot_general(dys_b[:, hs], xdt_b[:, hs], NT_DIMS, preferred_element_type=F32)
                dmt = lax.dot_general(xdt_b[:, hs], dys_b[:, hs], NT_DIMS, preferred_element_type=F32)
                dcb = dcb + dm * dec
                dcbt = dcbt + dmt * dect
                da_h = jnp.sum(dm * m - dmt * mt, axis=1, keepdims=True)
                da_diag = da_diag + jnp.where(lane == h, da_h, 0.0)
            dcs = dcs + jnp.dot(dcb.astype(BF16), bs_g, preferred_element_type=F32)
            dbs = dbs + jnp.dot(dcbt.astype(BF16), cs_g, preferred_element_type=F32)
            dxdt_parts.append(jnp.concatenate(dxdt_h, axis=1) + g1 * d_s_x[:, gw])
            dbs_parts.append(dbs)
            dcs_parts.append(dcs)
            ds_scr[:, gw] = dst[:, gw] * cd_x[:, gw] + dst_new
        dxdt = jnp.concatenate(dxdt_parts, axis=1)
        y_off = jnp.concatenate(off_parts, axis=1)
        g1 = jnp.concatenate(g1_parts, axis=1)
        dds = _split_dot(g1 * xdt, e_mat_t, 2)
        da_off = _split_dot(w_x * y_off, e_mat_t, 2)
        ddt_x = _split_dot(dxdt * xs, e_mat_t, 2)
        dcd_lane = jnp.sum(dst * st_b.astype(F32), axis=0, keepdims=True)
        dcd = _split_dot(jnp.broadcast_to(dcd_lane, (SUB, D_B)), e_mat_t, 2)[0:1]
        dd_ref[...] += _split_dot(jnp.broadcast_to(dd_lane, (SUB, D_B)), e_mat_t, 2)[0:1]
        t_ds = dds * d_s
        d_a = da_diag + da_off - t_ds
        last_row = jnp.sum(t_ds, axis=0, keepdims=True) + dcd * e_a[CHUNK - 1:CHUNK, :]
        d_a = d_a + jnp.where(rows == CHUNK - 1, last_row, 0.0)
        dda = _split_dot_left(_tri(False).astype(BF16), d_a, 3)
        ddt = dda * a + ddt_x
        dal_ref[...] += jnp.sum(dda * dt * a, axis=0, keepdims=True)
        ddtraw = jnp.where(valid_col, ddt * jax.nn.sigmoid(dtraw + dtb_ref[...]), 0.0)
        ddtraw = jnp.where(lane < N_HEADS, ddtraw, 0.0)
        ddtb_ref[...] += jnp.sum(ddtraw, axis=0, keepdims=True)
        dpw_ref[:, D_B:D_B + LANE] = ddtraw.astype(BF16)
        dxs = dxs + dxdt * dt_x
        dxbc = jnp.concatenate([dxs] + dbs_parts + dcs_parts, axis=1)
        dxbc_ref[...] = jnp.where(valid_col, dxbc, 0.0).astype(BF16)

    def row(width, col):
        return pl.BlockSpec((nb, CHUNK, width), lambda c: (0, cpe - 1 - c, col))

    def const(shape):
        return pl.BlockSpec(shape, lambda c: (0,) * len(shape))

    proj3 = proj.reshape(nb, lp, N_INP)
    dxbc, dproj, *rest = _grid_call(
        body, name, (cpe,),
        [row(D_B, 0), row(D_B, 0), row(N_XBC, 0), row(D_B, COL_BZ // D_B), row(LANE, COL_DT // LANE),
         pl.BlockSpec((nb, None, N_STATE, D_B), lambda c: (0, cpe - 1 - c, 0, 0)),
         const((1, LANE)), const((N_HEADS, 1)), const((1, LANE)), const((N_HEADS, 1)),
         const((1, D_B)), const((1, D_B)), const((LANE, D_B)), const((D_B, LANE))],
        [row(N_XBC, 0),
         pl.BlockSpec((pl.Element(nb), pl.Element(CHUNK), pl.Element(D_B + LANE)),
                      lambda c: (0, (cpe - 1 - c) * CHUNK, COL_BZ)),
         const((1, D_B)), const((1, LANE)), const((1, LANE)), const((1, LANE))],
        [jax.ShapeDtypeStruct((nb, lp, N_XBC), BF16), jax.ShapeDtypeStruct((nb, lp, N_INP), BF16),
         jax.ShapeDtypeStruct((1, D_B), F32),
         jax.ShapeDtypeStruct((1, LANE), F32), jax.ShapeDtypeStruct((1, LANE), F32),
         jax.ShapeDtypeStruct((1, LANE), F32)],
        [pltpu.VMEM((nb, N_STATE, D_B), F32)],
        (dy.reshape(nb, lp, -1), y_ssd.reshape(nb, lp, D_B), xbc.reshape(nb, lp, N_XBC), proj3, proj3, states,
         dtb_p, dtb_t, alog_p, alog_t, d_skip_x, norm_g, e_mat, e_mat_t), comm)
    return (dxbc.reshape(t, N_XBC), dproj.reshape(t, N_INP), *rest)


HBM_SPEC = pl.BlockSpec(memory_space=pl.ANY)


def _mesh_pos():
    return lax.axis_index("x"), lax.axis_index("y"), lax.axis_index("c")


def _allgather(arrays, name):
    n = len(arrays)

    def body(*refs):
        xs, outs = refs[:n], refs[n:2 * n]
        send_sems, recv_sems, local_sems = refs[2 * n:]
        x, y, c = _mesh_pos()
        me, sibling = (x, y, c), (x, y, 1 - c)
        chips = [(1 - x, y), (x, 1 - y), (1 - x, 1 - y)]

        def slot(px, py, pc):
            return 4 * px + 2 * py + pc

        def copy(a, k, block, to, src=None):
            dst = outs[a].at[slot(*block)]
            return pltpu.make_async_remote_copy(
                src_ref=dst if src is None else src, dst_ref=dst, send_sem=send_sems.at[a, k], recv_sem=recv_sems.at[a, k],
                device_id=to, device_id_type=MESH)

        mine = [pltpu.make_async_copy(xs[a], outs[a].at[slot(*me)], local_sems.at[a]) for a in range(n)]
        for cp in mine:
            cp.start()
        first = []
        for a in range(n):
            first.append(copy(a, 0, me, sibling, src=xs[a]))
            first += [copy(a, 1 + j, me, (*chip, c), src=xs[a]) for j, chip in enumerate(chips)]
        for cp in first:
            cp.start()
        passed = []
        for j, chip in enumerate(chips):
            for a in range(n):
                copy(a, 1 + j, (*chip, c), me).wait_recv()
                cp = copy(a, 4 + j, (*chip, c), sibling)
                cp.start()
                passed.append(cp)
        for a in range(n):
            copy(a, 0, sibling, me).wait_recv()
        for j, chip in enumerate(chips):
            for a in range(n):
                copy(a, 4 + j, (*chip, 1 - c), me).wait_recv()
        for cp in first + passed:
            cp.wait_send()
        for cp in mine:
            cp.wait()

    return pl.pallas_call(
        body, name=name,
        in_specs=[HBM_SPEC] * n, out_specs=[HBM_SPEC] * n,
        out_shape=[jax.ShapeDtypeStruct((N_DEV,) + a.shape, a.dtype) for a in arrays],
        scratch_shapes=[pltpu.SemaphoreType.DMA((n, 7)), pltpu.SemaphoreType.DMA((n, 7)), pltpu.SemaphoreType.DMA((n,))],
    )(*arrays)


def _remote(src, dst, send_sem, recv_sem, to):
    return pltpu.make_async_remote_copy(src_ref=src, dst_ref=dst, send_sem=send_sem, recv_sem=recv_sem,
                                        device_id=to, device_id_type=MESH)


def _slot_plain(d):
    return d


def _slot_mix_rows(d):
    return jnp.where(d < 2, d + 4, jnp.where(d < 6, d - 2, d))


def _ag_ici(pieces, slots):
    n = len(pieces)

    def copies(ins, outs, sems, with_recvs):
        send_sems, recv_sems, local_sems = sems
        x, y, c = _mesh_pos()
        local, sends, recvs = [], [], []
        for a in range(n):
            mine = outs[a].at[slots[a](4 * x + 2 * y + c)]
            local.append(pltpu.make_async_copy(ins[a], mine, local_sems.at[a]))
            for k, (px, py) in enumerate([(1 - x, y), (x, 1 - y), (1 - x, 1 - y)]):
                sends.append(_remote(ins[a], mine, send_sems.at[a, k], recv_sems.at[a, k], (px, py, c)))
                if with_recvs:
                    theirs = outs[a].at[slots[a](4 * px + 2 * py + c)]
                    recvs.append(_remote(ins[a], theirs, send_sems.at[a, k], recv_sems.at[a, k], (px, py, c)))
        return local, sends, recvs

    return _Comm(pieces, [jax.ShapeDtypeStruct((N_DEV,) + p.shape, p.dtype) for p in pieces],
                 [pltpu.SemaphoreType.DMA((n, 3)), pltpu.SemaphoreType.DMA((n, 3)), pltpu.SemaphoreType.DMA((n,))], copies)


def _ag_d2d(bufs, slots):
    n = len(bufs)

    def copies(ins, outs, sems, with_recvs):
        send_sems, recv_sems = sems
        x, y, c = _mesh_pos()
        chips = [(x, y), (1 - x, y), (x, 1 - y), (1 - x, 1 - y)]
        sends, recvs = [], []
        for a in range(n):
            for k, (px, py) in enumerate(chips):
                held = slots[a](4 * px + 2 * py + c)
                sends.append(_remote(ins[a].at[held], outs[a].at[held], send_sems.at[a, k], recv_sems.at[a, k], (x, y, 1 - c)))
                if with_recvs:
                    got = slots[a](4 * px + 2 * py + 1 - c)
                    recvs.append(_remote(ins[a].at[got], outs[a].at[got], send_sems.at[a, k], recv_sems.at[a, k], (x, y, 1 - c)))
        return [], sends, recvs

    return _Comm(bufs, [jax.ShapeDtypeStruct(b.shape, b.dtype) for b in bufs],
                 [pltpu.SemaphoreType.DMA((n, 4)), pltpu.SemaphoreType.DMA((n, 4))], copies,
                 aliases={a: a for a in range(n)})


def _rs_d2d(blocks, slots):
    n = len(blocks)

    def copies(ins, outs, sems, with_recvs):
        send_sems, recv_sems = sems
        x, y, c = _mesh_pos()
        sends, recvs = [], []
        for a in range(n):
            for j in range(4):
                src = ins[a].at[slots[a](2 * j + 1 - c)]
                sends.append(_remote(src, outs[a].at[j], send_sems.at[a, j], recv_sems.at[a, j], (x, y, 1 - c)))
                if with_recvs:
                    recvs.append(_remote(src, outs[a].at[j], send_sems.at[a, j], recv_sems.at[a, j], (x, y, 1 - c)))
        return [], sends, recvs

    return _Comm(blocks, [jax.ShapeDtypeStruct((4,) + b.shape[1:], b.dtype) for b in blocks],
                 [pltpu.SemaphoreType.DMA((n, 4)), pltpu.SemaphoreType.DMA((n, 4))], copies)


def _rs_ici(blocks):
    n = len(blocks)

    def copies(ins, outs, sems, with_recvs):
        send_sems, recv_sems = sems
        x, y, c = _mesh_pos()
        sends, recvs = [], []
        for a in range(n):
            for k, (px, py) in enumerate([(1 - x, y), (x, 1 - y), (1 - x, 1 - y)]):
                src = ins[a].at[2 * px + py]
                sends.append(_remote(src, outs[a].at[k], send_sems.at[a, k], recv_sems.at[a, k], (px, py, c)))
                if with_recvs:
                    recvs.append(_remote(src, outs[a].at[k], send_sems.at[a, k], recv_sems.at[a, k], (px, py, c)))
        return [], sends, recvs

    return _Comm(blocks, [jax.ShapeDtypeStruct((3,) + b.shape[1:], b.dtype) for b in blocks],
                 [pltpu.SemaphoreType.DMA((n, 3)), pltpu.SemaphoreType.DMA((n, 3))], copies)


def _run_comm(comm, name):
    n_in, n_out = len(comm.inputs), len(comm.out_shapes)

    def body(*refs):
        ins, outs, sems = refs[:n_in], refs[n_in:n_in + n_out], refs[n_in + n_out:]
        comm.start(ins, outs, sems)
        comm.wait(ins, outs, sems)

    return pl.pallas_call(
        body, name=name, in_specs=[HBM_SPEC] * n_in, out_specs=[HBM_SPEC] * n_out, out_shape=comm.out_shapes,
        scratch_shapes=comm.sem_shapes, input_output_aliases=comm.aliases,
    )(*comm.inputs)


W_ROWS = 784


def _w_segments():
    per = N_IN // N_DEV
    out = []
    for d in range(N_DEV):
        lo, hi = per * d, per * (d + 1)
        for a, b, start in COL_MAP:
            s, e = max(lo, a), min(hi, b)
            if s < e:
                out.append((d, s - lo, e - s, s - a + start))
    return out


def _w_gather_rows(g, name):
    tk = D_MODEL // 2
    u32 = jnp.uint32

    def body(g_ref, o_ref, scr):
        for d in range(N_DEV):
            x32 = pltpu.bitcast(g_ref[d], u32)
            for dd, src, rows, dst in _w_segments():
                if dd == d:
                    scr[dst // 2:(dst + rows) // 2, :] = x32[src // 2:(src + rows) // 2]
        scr[(COL_DT + N_HEADS) // 2:COL_XBC // 2, :] = jnp.zeros(((COL_XBC - COL_DT - N_HEADS) // 2, tk), u32)
        o_ref[...] = pltpu.bitcast(scr[...], BF16)

    return pl.pallas_call(
        body, name=name, grid=(D_MODEL // tk,),
        in_specs=[pl.BlockSpec((N_DEV, W_ROWS, tk), lambda j: (0, 0, j))],
        out_specs=pl.BlockSpec((N_INP, tk), lambda j: (0, j)),
        out_shape=jax.ShapeDtypeStruct((N_INP, D_MODEL), BF16),
        scratch_shapes=[pltpu.VMEM((N_INP // 2, tk), u32)],
        compiler_params=_cparams(("parallel",)),
    )(g)


def _w_split_rows(dwt, name):
    tk = D_MODEL // 4
    per = N_IN // N_DEV

    def body(w_ref, o_ref):
        for d, dst, rows, src in _w_segments():
            o_ref[d, dst:dst + rows, :] = w_ref[src:src + rows, :]
        for d in range(N_DEV):
            o_ref[d, per:W_ROWS, :] = jnp.zeros((W_ROWS - per, tk), F32)

    return pl.pallas_call(
        body, name=name, grid=(D_MODEL // tk,),
        in_specs=[pl.BlockSpec((N_INP, tk), lambda j: (0, j))],
        out_specs=pl.BlockSpec((N_DEV, W_ROWS, tk), lambda j: (0, 0, j)),
        out_shape=jax.ShapeDtypeStruct((N_DEV, W_ROWS, D_MODEL), F32),
        compiler_params=_cparams(("parallel",)),
    )(dwt)


def _rs_pair_sum(g, ra, own_slots, name):
    _, rows, cols = g.shape

    def body(s_ref, g_ref, ra_ref, o_ref):
        o_ref[...] = (g_ref[...] + ra_ref[...]).astype(BF16)

    return pl.pallas_call(
        body, name=name,
        grid_spec=pltpu.PrefetchScalarGridSpec(
            num_scalar_prefetch=1, grid=(4,),
            in_specs=[pl.BlockSpec((None, rows, cols), lambda j, s: (s[j], 0, 0)),
                      pl.BlockSpec((None, rows, cols), lambda j, s: (j, 0, 0))],
            out_specs=pl.BlockSpec((None, rows, cols), lambda j, s: (j, 0, 0))),
        out_shape=jax.ShapeDtypeStruct((4, rows, cols), BF16),
        compiler_params=_cparams(("parallel",)),
    )(own_slots, g, ra)


def _rs_final_sum(h, rb, chip_idx, name):
    _, rows, cols = h.shape

    def body(j_ref, h_ref, rb_ref, o_ref):
        o_ref[...] = ((h_ref[...].astype(F32) + rb_ref[0].astype(F32)) + rb_ref[1].astype(F32)) + rb_ref[2].astype(F32)

    return pl.pallas_call(
        body, name=name,
        grid_spec=pltpu.PrefetchScalarGridSpec(
            num_scalar_prefetch=1, grid=(1,),
            in_specs=[pl.BlockSpec((None, rows, cols), lambda i, j: (j[0], 0, 0)),
                      pl.BlockSpec((3, rows, cols), lambda i, j: (0, 0, 0))],
            out_specs=pl.BlockSpec((rows, cols), lambda i, j: (0, 0))),
        out_shape=jax.ShapeDtypeStruct((rows, cols), F32),
        compiler_params=_cparams(("arbitrary",)),
    )(chip_idx, h, rb)


def _adamw_math(w, g, m, v):
    m = ADAM_B1 * m + (1.0 - ADAM_B1) * g
    v = ADAM_B2 * v + (1.0 - ADAM_B2) * (g * g)
    m_hat = m / (1.0 - ADAM_B1 ** ADAM_STEP)
    v_hat = v / (1.0 - ADAM_B2 ** ADAM_STEP)
    delta = -ADAM_LR * (m_hat / (jnp.sqrt(v_hat) + ADAM_EPS) + ADAM_WD * w)
    return delta, m, v


def _adamw_rows(g, w, m, v, tr, name):
    rows, cols = w.shape

    def body(g_ref, w_ref, m_ref, v_ref, d_out, m_out, v_out):
        delta, m_new, v_new = _adamw_math(w_ref[...], g_ref[...], m_ref[...], v_ref[...])
        d_out[...] = delta
        m_out[...] = m_new
        v_out[...] = v_new

    flat = pl.BlockSpec((tr, cols), lambda r: (r, 0))
    return pl.pallas_call(
        body, name=name, grid=(rows // tr,),
        in_specs=[flat] * 4, out_specs=[flat] * 3,
        out_shape=[jax.ShapeDtypeStruct((rows, cols), F32)] * 3,
        compiler_params=_cparams(("parallel",)),
    )(g, w, m, v)


def _sum_devices(parts, name):
    _, p, _ = parts.shape

    def body(x_ref, o_ref):
        acc = x_ref[0]
        for d in range(1, N_DEV):
            acc = acc + x_ref[d]
        o_ref[...] = acc

    return pl.pallas_call(
        body, name=name, grid=(1,),
        in_specs=[pl.BlockSpec((N_DEV, p, LANE), lambda i: (0, 0, 0))],
        out_specs=pl.BlockSpec((p, LANE), lambda i: (0, 0)),
        out_shape=jax.ShapeDtypeStruct((p, LANE), F32),
        compiler_params=_cparams(("arbitrary",)),
    )(parts)


def _adamw_small(g, w, m, v, name):
    p = g.shape[0]

    def body(g_ref, w_ref, m_ref, v_ref, d_out, m_out, v_out):
        delta, m_new, v_new = _adamw_math(w_ref[...], g_ref[...], m_ref[...], v_ref[...])
        d_out[...] = delta
        m_out[...] = m_new
        v_out[...] = v_new

    spec = pl.BlockSpec((p, LANE), lambda i: (0, 0))
    return pl.pallas_call(
        body, name=name, grid=(1,),
        in_specs=[spec] * 4, out_specs=[spec] * 3,
        out_shape=[jax.ShapeDtypeStruct((p, LANE), F32)] * 3,
        compiler_params=_cparams(("arbitrary",)),
    )(g, w, m, v)


PACK_ALIGN = SUB * LANE

SMALL_PARAMS = (
    ("meta", (N_META, D_MODEL), 1),
    ("pre_g", (DEPTH, D_MODEL), None),
    ("post_g", (DEPTH, D_MODEL), None),
    ("conv_a_w", (DEPTH, CONV_A_K, D_A), 2),
    ("ssm_conv_w", (DEPTH, SSM_K, N_XBC), 2),
    ("ssm_conv_b", (DEPTH, N_XBC), None),
    ("dt_bias", (DEPTH, N_HEADS), None),
    ("a_log", (DEPTH, N_HEADS), None),
    ("d_skip", (DEPTH, N_HEADS), None),
    ("ssm_norm_g", (DEPTH, D_B), None),
    ("conf_conv_w", (DEPTH, CONF_K, D_C), 2),
    ("conf_conv_b", (DEPTH, D_C), None),
    ("conf_ln_g", (DEPTH, D_C), None),
    ("conf_ln_b", (DEPTH, D_C), None),
)


def _local_shape(shape, axis):
    if axis is None:
        return shape
    return tuple(s // N_DEV if k == axis else s for k, s in enumerate(shape))


def _pack(arrays):
    flat = []
    for a in arrays:
        v = a.reshape(-1).astype(F32)
        flat.append(v)
        if v.shape[0] % PACK_ALIGN:
            flat.append(jnp.zeros(((-v.shape[0]) % PACK_ALIGN,), F32))
    return jnp.concatenate(flat).reshape(-1, LANE)


def _unpack(buf, shapes):
    flat = buf.reshape(-1)
    out, off = [], 0
    for s in shapes:
        size = 1
        for k in s:
            size *= k
        out.append(flat[off:off + size].reshape(s))
        off += size + (-size) % PACK_ALIGN
    return out


def kernel(x, meta, pre_g, post_g, w_in, w_out, conv_a_w, ssm_conv_w, ssm_conv_b, dt_bias, a_log, d_skip, ssm_norm_g, conf_conv_w, conf_conv_b, conf_ln_g, conf_ln_b, loss_target, m_meta, m_pre_g, m_post_g, m_w_in, m_w_out, m_conv_a_w, m_ssm_conv_w, m_ssm_conv_b, m_dt_bias, m_a_log, m_d_skip, m_ssm_norm_g, m_conf_conv_w, m_conf_conv_b, m_conf_ln_g, m_conf_ln_b, v_meta, v_pre_g, v_post_g, v_w_in, v_w_out, v_conv_a_w, v_ssm_conv_w, v_ssm_conv_b, v_dt_bias, v_a_log, v_d_skip, v_ssm_norm_g, v_conf_conv_w, v_conf_conv_b, v_conf_ln_g, v_conf_ln_b):
    weights = dict(meta=meta, pre_g=pre_g, post_g=post_g, conv_a_w=conv_a_w, ssm_conv_w=ssm_conv_w, ssm_conv_b=ssm_conv_b,
                   dt_bias=dt_bias, a_log=a_log, d_skip=d_skip, ssm_norm_g=ssm_norm_g, conf_conv_w=conf_conv_w,
                   conf_conv_b=conf_conv_b, conf_ln_g=conf_ln_g, conf_ln_b=conf_ln_b)
    mom1 = dict(meta=m_meta, pre_g=m_pre_g, post_g=m_post_g, conv_a_w=m_conv_a_w, ssm_conv_w=m_ssm_conv_w,
                ssm_conv_b=m_ssm_conv_b, dt_bias=m_dt_bias, a_log=m_a_log, d_skip=m_d_skip, ssm_norm_g=m_ssm_norm_g,
                conf_conv_w=m_conf_conv_w, conf_conv_b=m_conf_conv_b, conf_ln_g=m_conf_ln_g, conf_ln_b=m_conf_ln_b)
    mom2 = dict(meta=v_meta, pre_g=v_pre_g, post_g=v_post_g, conv_a_w=v_conv_a_w, ssm_conv_w=v_ssm_conv_w,
                ssm_conv_b=v_ssm_conv_b, dt_bias=v_dt_bias, a_log=v_a_log, d_skip=v_d_skip, ssm_norm_g=v_ssm_norm_g,
                conf_conv_w=v_conf_conv_w, conf_conv_b=v_conf_conv_b, conf_ln_g=v_conf_ln_g, conf_ln_b=v_conf_ln_b)
    nb, seq, d = x.shape
    lp = PAD + N_META + seq
    t = nb * lp
    assert lp % TILE == 0 and t % (3 * LANE) == 0 and d == D_MODEL
    rt = lp // 3
    xi, yi, ci = _mesh_pos()
    dev = 4 * xi + 2 * yi + ci
    ci32 = ci.astype(jnp.int32)
    chip_idx = (2 * xi + yi).astype(jnp.int32).reshape(1)
    own_plain = jnp.stack([2 * j + ci32 for j in range(4)])
    own_mix = jnp.stack([_slot_mix_rows(2 * j + ci32) for j in range(4)]).astype(jnp.int32)
    n_in_loc = N_IN // N_DEV
    n_out_loc = 2 * D_MODEL // N_DEV
    slots = (_slot_plain, _slot_mix_rows)

    sharded_small = [n for n, _, ax in SMALL_PARAMS if ax is not None]
    small_shapes = {n: s for n, s, _ in SMALL_PARAMS}
    small_axis = {n: ax for n, _, ax in SMALL_PARAMS}
    sw_pack = _pack([weights[n] for n in sharded_small])
    (sw_g,) = _allgather([sw_pack], "ag_small_weights")
    wt_loc = jnp.pad(jnp.swapaxes(w_in, 1, 2).astype(BF16), ((0, 0), (0, W_ROWS - n_in_loc), (0, 0)))
    wo_loc = w_out.astype(BF16)

    def gathered(bufs):
        return _w_gather_rows(bufs[0], "w_gather_rows"), bufs[1].reshape(2 * D_MODEL, D_MODEL)

    full = dict(weights)
    per_dev = [_unpack(sw_g[k], [_local_shape(small_shapes[n], small_axis[n]) for n in sharded_small]) for k in range(N_DEV)]
    for q, n in enumerate(sharded_small):
        full[n] = jnp.concatenate([per_dev[k][q] for k in range(N_DEV)], axis=small_axis[n])

    e_mat = (lax.broadcasted_iota(jnp.int32, (LANE, D_B), 0) == lax.broadcasted_iota(jnp.int32, (LANE, D_B), 1) // HEAD_DIM)
    e_mat = e_mat.astype(BF16)
    e_mat_t = e_mat.T

    front = jnp.concatenate([jnp.zeros((PAD, d), F32), full["meta"]], axis=0)
    h = jnp.concatenate([jnp.concatenate([front, x[b]], axis=0) for b in range(nb)], axis=0)
    saved = []
    w_t, w_o = gathered(_run_comm(_ag_d2d(_run_comm(_ag_ici([wt_loc[0], wo_loc[0]], slots), "ag_ici"), slots), "ag_d2d"))
    for i in range(DEPTH):
        row = lambda a: a[i].reshape(1, -1)
        hn = _rms_fwd(h, row(pre_g), rt, "rms_fwd")
        proj = _mm(hn, w_t, "nt", BF16, t // 3, MM_TN, D_MODEL, "mm_proj", cols_outer=True)
        ycat = _a_fwd(proj, full["conv_a_w"][i], lp, "a_fwd")
        xbc = _xbc_fwd(proj, full["ssm_conv_w"][i], row(ssm_conv_b), lp, "xbc_fwd")
        ycat = _c_fwd(proj, full["conf_conv_w"][i], row(conf_conv_b), row(conf_ln_g), row(conf_ln_b), ycat, lp, "c_fwd")
        d_skip_x = jnp.repeat(d_skip[i], HEAD_DIM).reshape(1, D_B)
        nxt = _ag_ici([wt_loc[i + 1], wo_loc[i + 1]], slots) if i + 1 < DEPTH else None
        ycat, yssd, states, *bufs = _ssd_fwd(xbc, proj, dt_bias[i], a_log[i], d_skip_x, row(ssm_norm_g), e_mat, ycat, lp,
                                             "ssd_fwd", comm=nxt)
        if bufs:
            m, *bufs = _mm(ycat, w_o, "nn", F32, t // 3, D_MODEL, 2 * D_MODEL, "mm_out", comm=_ag_d2d(bufs, slots))
        else:
            m = _mm(ycat, w_o, "nn", F32, t // 3, D_MODEL, 2 * D_MODEL, "mm_out")
        saved.append((h, hn, proj, ycat, yssd, states, xbc, m, d_skip_x, w_t, w_o))
        h = _post_fwd(h, m, row(post_g), rt, "post_fwd")
        if bufs:
            w_t, w_o = gathered(bufs)

    dh, loss_blk = _loss_kernel(h, loss_target.reshape(nb * seq, d), lp, "loss")

    grads = {n: [None] * DEPTH for n, _, _ in SMALL_PARAMS if n != "meta"}
    gt_in, gr_out = [None] * DEPTH, [None] * DEPTH
    dmeta = None
    pending = None

    def rs_pair_sums(blocks, ras):
        return [_rs_pair_sum(blocks[0], ras[0], own_plain, "rs_pair_sum_in"),
                _rs_pair_sum(blocks[1], ras[1], own_mix, "rs_pair_sum_out")]

    def rs_finish(layer, hs, rbs):
        gt_in[layer] = _rs_final_sum(hs[0], rbs[0], chip_idx, "rs_final_sum_in")
        gr_out[layer] = _rs_final_sum(hs[1], rbs[1], chip_idx, "rs_final_sum_out")

    for i in reversed(range(DEPTH)):
        row = lambda a: a[i].reshape(1, -1)
        h_i, hn, proj, ycat, yssd, states, xbc, m, d_skip_x, w_t, w_o = saved[i]
        dm, grads["post_g"][i] = _post_bwd(dh, m, row(post_g), rt, "post_bwd")
        if pending is not None:
            dy, *ras = _mm(dm, w_o, "nt", BF16, t // 3, D_MODEL, D_MODEL, "mm_dy", cols_outer=True,
                           comm=_rs_d2d(list(pending), slots))
            hs = rs_pair_sums(pending, ras)
        else:
            dy = _mm(dm, w_o, "nt", BF16, t // 3, D_MODEL, D_MODEL, "mm_dy", cols_outer=True)
            hs = None
        dw_out = _mm(ycat, dm, "tn", F32, D_MODEL, D_MODEL, t // 3, "mm_dwout")
        dxbc, dproj, grads["ssm_norm_g"][i], ddtb, dal, dds, *rbs = _ssd_bwd(
            dy, yssd, xbc, proj, states, dt_bias[i], a_log[i], d_skip_x, row(ssm_norm_g), e_mat, e_mat_t, lp, "ssd_bwd",
            comm=_rs_ici(hs) if hs is not None else None)
        if hs is not None:
            rs_finish(i + 1, hs, rbs)
        grads["dt_bias"][i] = ddtb[:, :N_HEADS]
        grads["a_log"][i] = dal[:, :N_HEADS]
        grads["d_skip"][i] = dds[:, :N_HEADS]
        dproj, grads["conv_a_w"][i] = _a_bwd(dy, proj, full["conv_a_w"][i], dproj, lp, "a_bwd")
        dproj, grads["ssm_conv_w"][i], grads["ssm_conv_b"][i] = _xbc_bwd(
            dxbc, proj, full["ssm_conv_w"][i], row(ssm_conv_b), dproj, lp, "xbc_bwd")
        dproj, grads["conf_conv_w"][i], grads["conf_conv_b"][i], grads["conf_ln_g"][i], grads["conf_ln_b"][i] = _c_bwd(
            dy, proj, full["conf_conv_w"][i], row(conf_conv_b), row(conf_ln_g), row(conf_ln_b), dproj, lp, "c_bwd")
        dhn = _mm(dproj, w_t, "nn", F32, MM_TM, D_MODEL, N_INP, "mm_dhn")
        dw_in_t = _mm(dproj, hn, "tn", F32, MM_TN, D_MODEL, t, "mm_dwin")
        dh, grads["pre_g"][i], dmeta = _rms_bwd(dh, dhn, h_i, row(pre_g), lp, rt, "rms_bwd")
        pending = (_w_split_rows(dw_in_t, "w_split_rows"), dw_out.reshape(N_DEV, n_out_loc, D_MODEL))
    grad_x = dh.reshape(nb, lp, d)[:, PAD + N_META:]
    hs = rs_pair_sums(pending, _run_comm(_rs_d2d(list(pending), slots), "rs_d2d"))
    rs_finish(0, hs, _run_comm(_rs_ici(hs), "rs_ici"))

    g_w_in = jnp.swapaxes(jnp.stack(gt_in)[:, :n_in_loc, :], 1, 2).reshape(DEPTH * D_MODEL, n_in_loc)
    g_w_out = jnp.stack(gr_out).reshape(DEPTH * n_out_loc, D_MODEL)
    big = {}
    big["w_in"] = [g_w_in, *_adamw_rows(g_w_in, w_in.reshape(-1, n_in_loc), m_w_in.reshape(-1, n_in_loc),
                                        v_w_in.reshape(-1, n_in_loc), 512, "adamw_w_in")]
    big["w_out"] = [g_w_out, *_adamw_rows(g_w_out, w_out.reshape(-1, D_MODEL), m_w_out.reshape(-1, D_MODEL),
                                          v_w_out.reshape(-1, D_MODEL), 256, "adamw_w_out")]
    big = {n: [a.reshape(s) for a in outs] for (n, outs), s in zip(big.items(), (w_in.shape, w_out.shape))}

    names = [n for n, _, _ in SMALL_PARAMS]
    partial = [loss_blk[0:1, 0:1], dmeta] + [jnp.concatenate(grads[n], axis=0) for n in names[1:]]
    part_pack = _pack(partial)
    (parts_g,) = _allgather([part_pack], "ag_small_grads")
    total = _unpack(_sum_devices(parts_g, "sum_small_grads"), [(1,)] + [small_shapes[n] for n in names])
    loss = total[0][0]
    g_small = {}
    for n, g in zip(names, total[1:]):
        ax = small_axis[n]
        if ax is not None:
            g = lax.dynamic_slice_in_dim(g, dev * (small_shapes[n][ax] // N_DEV), small_shapes[n][ax] // N_DEV, axis=ax)
        g_small[n] = g
    loc_shapes = [_local_shape(small_shapes[n], small_axis[n]) for n in names]
    d_pack, m_pack, v_pack = _adamw_small(_pack([g_small[n] for n in names]), _pack([weights[n] for n in names]),
                                          _pack([mom1[n] for n in names]), _pack([mom2[n] for n in names]), "adamw_small")
    d_small = dict(zip(names, _unpack(d_pack, loc_shapes)))
    m_small = dict(zip(names, _unpack(m_pack, loc_shapes)))
    v_small = dict(zip(names, _unpack(v_pack, loc_shapes)))

    order = ["meta", "pre_g", "post_g", "w_in", "w_out", "conv_a_w", "ssm_conv_w", "ssm_conv_b", "dt_bias", "a_log",
             "d_skip", "ssm_norm_g", "conf_conv_w", "conf_conv_b", "conf_ln_g", "conf_ln_b"]

    def pick(k, small):
        return [big[n][k] if n in big else small[n] for n in order]

    return (loss, grad_x, *pick(0, g_small), *pick(1, d_small), *pick(2, m_small), *pick(3, v_small))
```

```python
import functools

import jax
import jax.numpy as jnp
from jax import lax
from jax.experimental import pallas as pl
from jax.experimental.pallas import tpu as pltpu

F32 = jnp.float32
BF16 = jnp.bfloat16

D_MODEL = 1024
DEPTH = 4
SEQ = 2048
CHUNK = 64
N_META = 16
PAD = 48
LP = PAD + N_META + SEQ
D_A = 512
D_B = 1024
D_C = 512
N_HEADS = 16
HEAD_DIM = 64
N_STATE = 128
N_GROUPS = 2
GROUP_W = D_B // N_GROUPS
N_XBC = D_B + 2 * N_GROUPS * N_STATE
CONV_A_K = 3
SSM_K = 4
CONF_K = 31
NORM_EPS = 1e-6
LN_EPS = 1e-5
N_IN = 6160
N_INP = 6272
COL_BZ = 2048
COL_DT = 3072
COL_XBC = 3200
COL_C = 4736
COL_MAP = ((0, 3072, 0), (3072, 4608, COL_XBC), (4608, 4624, COL_DT), (4624, 6160, COL_C))
LANE = 128
SUB = 8

ADAM_LR = 0.001
ADAM_B1 = 0.9
ADAM_B2 = 0.999
ADAM_EPS = 1e-08
ADAM_WD = 0.01
ADAM_STEP = 10

TILE = 192
HALO_A = 16
HALO_C = 32
MM_TM = 384
MM_TN = 896
VMEM_LIMIT = 56 * 1024 * 1024

MESH = pl.DeviceIdType.MESH
N_DEV = 8


def _silu(x):
    return x * jax.nn.sigmoid(x)


def _dsilu(x):
    s = jax.nn.sigmoid(x)
    return s * (1.0 + x * (1.0 - s))


def _cparams(sem=None):
    return pltpu.CompilerParams(dimension_semantics=sem, vmem_limit_bytes=VMEM_LIMIT)


def _mm(a, b, mode, out_dtype, tm, tn, tk, name, cols_outer=False, comm=None):
    def ix(f):
        if cols_outer:
            return lambda j, i, q: f(i, j, q)
        return f

    if mode == "nn":
        (m, k), (_, n) = a.shape, b.shape
        a_spec = pl.BlockSpec((tm, tk), ix(lambda i, j, q: (i, q)))
        b_spec = pl.BlockSpec((tk, tn), ix(lambda i, j, q: (q, j)))
        dims = (((1,), (0,)), ((), ()))
    elif mode == "nt":
        (m, k), (n, _) = a.shape, b.shape
        a_spec = pl.BlockSpec((tm, tk), ix(lambda i, j, q: (i, q)))
        b_spec = pl.BlockSpec((tn, tk), ix(lambda i, j, q: (j, q)))
        dims = (((1,), (1,)), ((), ()))
    else:
        (k, m), (_, n) = a.shape, b.shape
        a_spec = pl.BlockSpec((tk, tm), ix(lambda i, j, q: (q, i)))
        b_spec = pl.BlockSpec((tk, tn), ix(lambda i, j, q: (q, j)))
        dims = (((0,), (0,)), ((), ()))
    assert m % tm == 0 and n % tn == 0 and k % tk == 0, (name, a.shape, b.shape)
    nk = k // tk
    grid = (n // tn, m // tm, nk) if cols_outer else (m // tm, n // tn, nk)

    def body(a_ref, b_ref, o_ref, acc_ref):
        part = lax.dot_general(a_ref[...].astype(BF16), b_ref[...].astype(BF16), dims, preferred_element_type=F32)
        if nk == 1:
            o_ref[...] = part.astype(o_ref.dtype)
        else:
            q = pl.program_id(2)

            @pl.when(q == 0)
            def _():
                acc_ref[...] = part

            @pl.when(q > 0)
            def _():
                acc_ref[...] += part

            @pl.when(q == nk - 1)
            def _():
                o_ref[...] = acc_ref[...].astype(o_ref.dtype)

    res = _grid_call(
        body, name, grid, [a_spec, b_spec], [pl.BlockSpec((tm, tn), ix(lambda i, j, q: (i, j)))],
        [jax.ShapeDtypeStruct((m, n), out_dtype)], [pltpu.VMEM((tm, tn) if nk > 1 else (SUB, LANE), F32)], (a, b), comm)
    return res[0] if comm is None else res


def _row_mask(i, tpe, rows):
    r = lax.broadcasted_iota(jnp.int32, (rows, 1), 0)
    return jnp.logical_or((i % tpe) != 0, r >= PAD)


def _rms_fwd(h, g, rt, name):
    t, d = h.shape

    def body(h_ref, g_ref, o_ref):
        x = h_ref[...]
        r = lax.rsqrt(jnp.mean(x * x, axis=-1, keepdims=True) + NORM_EPS)
        o_ref[...] = (x * r * g_ref[...]).astype(BF16)

    return pl.pallas_call(
        body, name=name, grid=(t // rt,),
        in_specs=[pl.BlockSpec((rt, d), lambda i: (i, 0)), pl.BlockSpec((1, d), lambda i: (0, 0))],
        out_specs=pl.BlockSpec((rt, d), lambda i: (i, 0)),
        out_shape=jax.ShapeDtypeStruct((t, d), BF16),
        compiler_params=_cparams(("parallel",)),
    )(h, g)


def _post_fwd(h, m, g, rt, name):
    t, d = h.shape

    def body(h_ref, m_ref, g_ref, o_ref):
        x = m_ref[...]
        r = lax.rsqrt(jnp.mean(x * x, axis=-1, keepdims=True) + NORM_EPS)
        o_ref[...] = h_ref[...] + x * r * g_ref[...]

    row = pl.BlockSpec((rt, d), lambda i: (i, 0))
    return pl.pallas_call(
        body, name=name, grid=(t // rt,),
        in_specs=[row, row, pl.BlockSpec((1, d), lambda i: (0, 0))], out_specs=row,
        out_shape=jax.ShapeDtypeStruct((t, d), F32),
        compiler_params=_cparams(("parallel",)),
    )(h, m, g)


def _post_rms_fwd(h, m, g_post, g_next, rt, name):
    t, d = h.shape

    def body(h_ref, m_ref, gp_ref, gn_ref, o_ref, n_ref):
        x = m_ref[...]
        r = lax.rsqrt(jnp.mean(x * x, axis=-1, keepdims=True) + NORM_EPS)
        y = h_ref[...] + x * r * gp_ref[...]
        o_ref[...] = y
        r2 = lax.rsqrt(jnp.mean(y * y, axis=-1, keepdims=True) + NORM_EPS)
        n_ref[...] = (y * r2 * gn_ref[...]).astype(BF16)

    row = pl.BlockSpec((rt, d), lambda i: (i, 0))
    vec = pl.BlockSpec((1, d), lambda i: (0, 0))
    return pl.pallas_call(
        body, name=name, grid=(t // rt,),
        in_specs=[row, row, vec, vec], out_specs=[row, row],
        out_shape=[jax.ShapeDtypeStruct((t, d), F32), jax.ShapeDtypeStruct((t, d), BF16)],
        compiler_params=_cparams(("parallel",)),
    )(h, m, g_post, g_next)


def _rms_bwd_math(x, g, dy):
    r = lax.rsqrt(jnp.mean(x * x, axis=-1, keepdims=True) + NORM_EPS)
    gdy = dy * g
    dx = r * gdy - x * (r * r * r) * jnp.mean(gdy * x, axis=-1, keepdims=True)
    return dx, dy * x * r


def _post_bwd(dh, m, g, rt, name):
    t, d = dh.shape

    def body(dh_ref, m_ref, g_ref, dm_ref, dg_ref):
        dm, dgt = _rms_bwd_math(m_ref[...], g_ref[...], dh_ref[...])
        dm_ref[...] = dm

        @pl.when(pl.program_id(0) == 0)
        def _():
            dg_ref[...] = jnp.zeros_like(dg_ref)

        dg_ref[...] += jnp.sum(dgt, axis=0, keepdims=True)

    row = pl.BlockSpec((rt, d), lambda i: (i, 0))
    vec = pl.BlockSpec((1, d), lambda i: (0, 0))
    return pl.pallas_call(
        body, name=name, grid=(t // rt,),
        in_specs=[row, row, vec], out_specs=[row, vec],
        out_shape=[jax.ShapeDtypeStruct((t, d), F32), jax.ShapeDtypeStruct((1, d), F32)],
        compiler_params=_cparams(("arbitrary",)),
    )(dh, m, g)


def _rms_bwd(dh_res, dhn, h, g, lp, rt, name):
    t, d = h.shape
    tpe = lp // rt

    def body(dr_ref, dn_ref, h_ref, g_ref, dh_ref, dg_ref, dmeta_ref):
        i = pl.program_id(0)
        dx, dgt = _rms_bwd_math(h_ref[...], g_ref[...], dn_ref[...])
        dh = dr_ref[...] + dx
        dh_ref[...] = dh

        @pl.when(i == 0)
        def _():
            dg_ref[...] = jnp.zeros_like(dg_ref)
            dmeta_ref[...] = jnp.zeros_like(dmeta_ref)

        dg_ref[...] += jnp.sum(dgt, axis=0, keepdims=True)

        @pl.when((i % tpe) == 0)
        def _():
            dmeta_ref[...] += dh[PAD:PAD + N_META, :]

    row = pl.BlockSpec((rt, d), lambda i: (i, 0))
    vec = pl.BlockSpec((1, d), lambda i: (0, 0))
    return pl.pallas_call(
        body, name=name, grid=(t // rt,),
        in_specs=[row, row, row, vec],
        out_specs=[row, vec, pl.BlockSpec((N_META, d), lambda i: (0, 0))],
        out_shape=[jax.ShapeDtypeStruct((t, d), F32), jax.ShapeDtypeStruct((1, d), F32),
                   jax.ShapeDtypeStruct((N_META, d), F32)],
        compiler_params=_cparams(("arbitrary",)),
    )(dh_res, dhn, h, g)


def _loss_kernel(h, target, lp, name):
    t, d = h.shape
    nb = t // lp
    seq = lp - PAD - N_META
    rows = next(r for r in (1024, 512, 256, 128, CHUNK) if seq % r == 0)
    per = seq // rows

    def body(h_ref, t_ref, zero_ref, dh_ref, loss_ref):
        @pl.when(jnp.logical_and(pl.program_id(0) == 0, pl.program_id(1) == 0))
        def _():
            loss_ref[...] = jnp.zeros_like(loss_ref)

        err = h_ref[...] - t_ref[...]
        dh_ref[...] = err * (1.0 / d)
        loss_ref[...] += (0.5 / d) * jnp.sum(err * err)

    win = pl.BlockSpec((pl.Element(rows), pl.Element(d)),
                       lambda b, j: (pl.multiple_of(b * lp + PAD + N_META + j * rows, CHUNK), 0))
    return pl.pallas_call(
        body, name=name, grid=(nb, per),
        in_specs=[win, pl.BlockSpec((rows, d), lambda b, j: (b * per + j, 0)), HBM_SPEC],
        out_specs=[win, pl.BlockSpec((SUB, LANE), lambda b, j: (0, 0))],
        out_shape=[jax.ShapeDtypeStruct((t, d), F32), jax.ShapeDtypeStruct((SUB, LANE), F32)],
        input_output_aliases={2: 0},
        compiler_params=_cparams(("arbitrary", "arbitrary")),
    )(h, target, jnp.zeros((t, d), F32))


def _window_spec(rows, width, col):
    return pl.BlockSpec((pl.Element(rows), pl.Element(width)), lambda i: (i * rows, col))


def _halo_specs(t, width, col, halo):
    cur = _window_spec(TILE, width, col)
    prev = pl.BlockSpec((pl.Element(halo), pl.Element(width)),
                        lambda i: (pl.multiple_of(jnp.maximum(i * TILE - halo, 0), halo), col))
    nxt = pl.BlockSpec((pl.Element(halo), pl.Element(width)),
                       lambda i: (pl.multiple_of(jnp.minimum((i + 1) * TILE, t - halo), halo), col))
    return cur, prev, nxt


def _f32(ref, lo, hi):
    return ref[:, lo:hi].astype(F32)


def _rows_from(x, start, rows):
    s = start % SUB
    if s == 0:
        return x[start:start + rows]
    return pltpu.roll(x, x.shape[0] - s, axis=0)[start - s:start - s + rows]


def _conv_a(ve, w):
    rows = ve.shape[0] - HALO_A
    return (w[0:1] * _rows_from(ve, HALO_A - 2, rows) + w[1:2] * _rows_from(ve, HALO_A - 1, rows)
            + w[2:3] * ve[HALO_A:HALO_A + rows])


def _a_fwd(proj, w, lp, name):
    t = proj.shape[0]
    tpe = lp // TILE
    cur, prev, _ = _halo_specs(t, 4 * D_A, 0, HALO_A)

    def body(p_ref, ph_ref, w_ref, y_ref):
        first = (pl.program_id(0) % tpe) == 0
        v = _f32(p_ref, D_A, 2 * D_A) * _f32(p_ref, 2 * D_A, 3 * D_A)
        vh = jnp.where(first, 0.0, _f32(ph_ref, D_A, 2 * D_A) * _f32(ph_ref, 2 * D_A, 3 * D_A))
        cv = _conv_a(jnp.concatenate([vh, v], axis=0), w_ref[...])
        y_ref[...] = (_f32(p_ref, 0, D_A) * cv * _silu(_f32(p_ref, 3 * D_A, 4 * D_A))).astype(BF16)

    return pl.pallas_call(
        body, name=name, grid=(t // TILE,),
        in_specs=[cur, prev, pl.BlockSpec((CONV_A_K, D_A), lambda i: (0, 0))],
        out_specs=_window_spec(TILE, D_A, D_B),
        out_shape=jax.ShapeDtypeStruct((t, D_B + D_A + D_C), BF16),
        compiler_params=_cparams(("parallel",)),
    )(proj, proj, w)


def _a_bwd(dy, proj, w, dproj, lp, name):
    t = proj.shape[0]
    tpe = lp // TILE
    cur, prev, nxt = _halo_specs(t, 4 * D_A, 0, HALO_A)
    dcur, _, dnxt = _halo_specs(t, D_A, D_B, HALO_A)

    def body(dy_ref, dyn_ref, p_ref, ph_ref, pn_ref, w_ref, dproj_ref, dp_ref, dw_ref):
        i = pl.program_id(0)
        first = (i % tpe) == 0
        last = (i % tpe) == tpe - 1
        w = w_ref[...]
        ab, ac, ax, az = (_f32(p_ref, k * D_A, (k + 1) * D_A) for k in range(4))
        v = ac * ax
        vh = jnp.where(first, 0.0, _f32(ph_ref, D_A, 2 * D_A) * _f32(ph_ref, 2 * D_A, 3 * D_A))
        ve = jnp.concatenate([vh, v], axis=0)
        taps = [_rows_from(ve, HALO_A - 2 + k, TILE) for k in range(CONV_A_K)]
        cv = w[0:1] * taps[0] + w[1:2] * taps[1] + w[2:3] * taps[2]
        s = _silu(az)
        dy_ = dy_ref[...].astype(F32)
        dcv = dy_ * ab * s
        dcvn = jnp.where(last, 0.0, dyn_ref[...].astype(F32) * _f32(pn_ref, 0, D_A) * _silu(_f32(pn_ref, 3 * D_A, 4 * D_A)))
        dce = jnp.concatenate([dcv, dcvn], axis=0)
        dv = w[2:3] * dce[0:TILE] + w[1:2] * _rows_from(dce, 1, TILE) + w[0:1] * _rows_from(dce, 2, TILE)
        dp = jnp.concatenate([dy_ * cv * s, dv * ax, dv * ac, dy_ * ab * cv * _dsilu(az)], axis=1)
        dp_ref[...] = jnp.where(_row_mask(i, tpe, TILE), dp, 0.0).astype(BF16)
        dw = jnp.concatenate(
            [jnp.sum(dcv * taps[k], axis=0, keepdims=True) for k in range(CONV_A_K)], axis=0)

        @pl.when(i == 0)
        def _():
            dw_ref[...] = jnp.zeros_like(dw_ref)

        dw_ref[...] += dw

    wspec = pl.BlockSpec((CONV_A_K, D_A), lambda i: (0, 0))
    return pl.pallas_call(
        body, name=name, grid=(t // TILE,),
        in_specs=[dcur, dnxt, cur, prev, nxt, wspec, HBM_SPEC],
        out_specs=[_window_spec(TILE, 4 * D_A, 0), wspec],
        out_shape=[jax.ShapeDtypeStruct(dproj.shape, dproj.dtype), jax.ShapeDtypeStruct((CONV_A_K, D_A), F32)],
        input_output_aliases={6: 0},
        compiler_params=_cparams(("arbitrary",)),
    )(dy, dy, proj, proj, proj, w, dproj)


def _conv_ssm(xe, w, rows, off):
    acc = w[0:1] * _rows_from(xe, off - 3, rows)
    for k in range(1, SSM_K):
        acc = acc + w[k:k + 1] * _rows_from(xe, off - 3 + k, rows)
    return acc


def _xbc_fwd(proj, w, b, lp, name):
    t = proj.shape[0]
    tpe = lp // TILE
    cur, prev, _ = _halo_specs(t, N_XBC, COL_XBC, HALO_A)

    def body(x_ref, xh_ref, w_ref, b_ref, o_ref):
        first = (pl.program_id(0) % tpe) == 0
        xh = jnp.where(first, 0.0, xh_ref[...].astype(F32))
        xe = jnp.concatenate([xh, x_ref[...].astype(F32)], axis=0)
        o_ref[...] = _silu(_conv_ssm(xe, w_ref[...], TILE, HALO_A) + b_ref[...]).astype(BF16)

    return pl.pallas_call(
        body, name=name, grid=(t // TILE,),
        in_specs=[cur, prev, pl.BlockSpec((SSM_K, N_XBC), lambda i: (0, 0)), pl.BlockSpec((1, N_XBC), lambda i: (0, 0))],
        out_specs=pl.BlockSpec((TILE, N_XBC), lambda i: (i, 0)),
        out_shape=jax.ShapeDtypeStruct((t, N_XBC), BF16),
        compiler_params=_cparams(("parallel",)),
    )(proj, proj, w, b)


def _xbc_bwd(dxbc, proj, w, b, dproj, lp, name):
    t = proj.shape[0]
    tpe = lp // TILE
    cur, prev, nxt = _halo_specs(t, N_XBC, COL_XBC, HALO_A)
    dcur, _, dnxt = _halo_specs(t, N_XBC, 0, HALO_A)

    def body(d_ref, dn_ref, x_ref, xh_ref, xn_ref, w_ref, b_ref, dproj_ref, dx_ref, dw_ref, db_ref):
        i = pl.program_id(0)
        first = (i % tpe) == 0
        last = (i % tpe) == tpe - 1
        w = w_ref[...]
        xh = jnp.where(first, 0.0, xh_ref[...].astype(F32))
        xe = jnp.concatenate([xh, x_ref[...].astype(F32), xn_ref[...].astype(F32)], axis=0)
        taps = [_rows_from(xe, HALO_A - 3 + k, TILE + HALO_A) for k in range(SSM_K)]
        pre = b_ref[...] + w[0:1] * taps[0]
        for k in range(1, SSM_K):
            pre = pre + w[k:k + 1] * taps[k]
        de = jnp.concatenate([d_ref[...].astype(F32), jnp.where(last, 0.0, dn_ref[...].astype(F32))], axis=0)
        dpre = de * _dsilu(pre)
        dx = w[3:4] * dpre[0:TILE]
        for k in range(SSM_K - 1):
            dx = dx + w[k:k + 1] * _rows_from(dpre, 3 - k, TILE)
        dx_ref[...] = jnp.where(_row_mask(i, tpe, TILE), dx, 0.0).astype(BF16)
        dpc = dpre[0:TILE]
        dw = jnp.concatenate(
            [jnp.sum(dpc * taps[k][0:TILE], axis=0, keepdims=True) for k in range(SSM_K)], axis=0)

        @pl.when(i == 0)
        def _():
            dw_ref[...] = jnp.zeros_like(dw_ref)
            db_ref[...] = jnp.zeros_like(db_ref)

        dw_ref[...] += dw
        db_ref[...] += jnp.sum(dpc, axis=0, keepdims=True)

    wspec = pl.BlockSpec((SSM_K, N_XBC), lambda i: (0, 0))
    bspec = pl.BlockSpec((1, N_XBC), lambda i: (0, 0))
    return pl.pallas_call(
        body, name=name, grid=(t // TILE,),
        in_specs=[dcur, dnxt, cur, prev, nxt, wspec, bspec, HBM_SPEC],
        out_specs=[_window_spec(TILE, N_XBC, COL_XBC), wspec, bspec],
        out_shape=[jax.ShapeDtypeStruct(dproj.shape, dproj.dtype), jax.ShapeDtypeStruct((SSM_K, N_XBC), F32),
                   jax.ShapeDtypeStruct((1, N_XBC), F32)],
        input_output_aliases={7: 0},
        compiler_params=_cparams(("arbitrary",)),
    )(dxbc, dxbc, proj, proj, proj, w, b, dproj)


SUBROWS = 32


def _fill_shifted(scr, x):
    scr[0] = x
    for s in range(1, SUB):
        scr[s] = pltpu.roll(x, x.shape[0] - s, axis=0)


def _window(scr, start, rows):
    s = start % SUB
    return scr[s, start - s:start - s + rows, :]


def _conv_conf(scr, w, rows, off, base):
    acc = w[0:1] * _window(scr, base + off - (CONF_K - 1), rows)
    for k in range(1, CONF_K):
        acc = acc + w[k:k + 1] * _window(scr, base + off - (CONF_K - 1) + k, rows)
    return acc


def _ln_fwd(u1, g, b):
    mu = jnp.mean(u1, axis=-1, keepdims=True)
    xc = u1 - mu
    rstd = lax.rsqrt(jnp.mean(xc * xc, axis=-1, keepdims=True) + LN_EPS)
    n = xc * rstd
    return n, rstd, n * g + b


def _c_fwd(proj, w, cb, g, b, ybuf, lp, name):
    t = proj.shape[0]
    tpe = lp // TILE
    cur, prev, _ = _halo_specs(t, 3 * D_C, COL_C, HALO_C)

    def body(p_ref, ph_ref, w_ref, cb_ref, g_ref, b_ref, ybuf_ref, y_ref, u1_ref, u0_scr):
        first = (pl.program_id(0) % tpe) == 0
        u0h = jnp.where(first, 0.0, _f32(ph_ref, 0, D_C) * jax.nn.sigmoid(_f32(ph_ref, D_C, 2 * D_C)))
        _fill_shifted(u0_scr, jnp.concatenate([u0h, _f32(p_ref, 0, D_C) * jax.nn.sigmoid(_f32(p_ref, D_C, 2 * D_C))], axis=0))
        w = w_ref[...]
        for r0 in range(0, TILE, SUBROWS):
            u1 = _conv_conf(u0_scr, w, SUBROWS, HALO_C, r0) + cb_ref[...]
            u1_ref[r0:r0 + SUBROWS, :] = u1.astype(BF16)
            _, _, u2 = _ln_fwd(u1, g_ref[...], b_ref[...])
            cz = p_ref[r0:r0 + SUBROWS, 2 * D_C:3 * D_C].astype(F32)
            y_ref[r0:r0 + SUBROWS, :] = (_silu(u2) * _silu(cz)).astype(BF16)

    vec = pl.BlockSpec((1, D_C), lambda i: (0, 0))
    return pl.pallas_call(
        body, name=name, grid=(t // TILE,),
        in_specs=[cur, prev, pl.BlockSpec((CONF_K, D_C), lambda i: (0, 0)), vec, vec, vec, HBM_SPEC],
        out_specs=[_window_spec(TILE, D_C, D_B + D_A), pl.BlockSpec((TILE, D_C), lambda i: (i, 0))],
        out_shape=[jax.ShapeDtypeStruct(ybuf.shape, ybuf.dtype), jax.ShapeDtypeStruct((t, D_C), BF16)],
        scratch_shapes=[pltpu.VMEM((SUB, HALO_C + TILE, D_C), F32)],
        input_output_aliases={6: 0},
        compiler_params=_cparams(("parallel",)),
    )(proj, proj, w, cb, g, b, ybuf)


def _c_bwd(dy, proj, u1, w, cb, g, b, dproj, lp, name):
    t = proj.shape[0]
    tpe = lp // TILE
    cur, prev, nxt = _halo_specs(t, 3 * D_C, COL_C, HALO_C)
    dcur, _, dnxt = _halo_specs(t, D_C, D_B + D_A, HALO_C)
    ucur, _, unxt = _halo_specs(t, D_C, 0, HALO_C)
    ext = TILE + HALO_C

    def body(dy_ref, dyn_ref, p_ref, ph_ref, pn_ref, u1_ref, u1n_ref, w_ref, cb_ref, g_ref, b_ref, dproj_ref,
             dp_ref, dw_ref, dcb_ref, dg_ref, db_ref, u0_scr, du1_scr, wacc_scr):
        i = pl.program_id(0)
        first = (i % tpe) == 0
        last = (i % tpe) == tpe - 1
        w = w_ref[...]

        @pl.when(i == 0)
        def _():
            dw_ref[...] = jnp.zeros_like(dw_ref)
            dcb_ref[...] = jnp.zeros_like(dcb_ref)
            dg_ref[...] = jnp.zeros_like(dg_ref)
            db_ref[...] = jnp.zeros_like(db_ref)

        u0h = jnp.where(first, 0.0, _f32(ph_ref, 0, D_C) * jax.nn.sigmoid(_f32(ph_ref, D_C, 2 * D_C)))
        _fill_shifted(u0_scr, jnp.concatenate(
            [u0h, _f32(p_ref, 0, D_C) * jax.nn.sigmoid(_f32(p_ref, D_C, 2 * D_C))], axis=0))
        dcb = jnp.zeros((1, D_C), F32)
        dg = jnp.zeros((1, D_C), F32)
        db = jnp.zeros((1, D_C), F32)
        for r0 in range(0, ext, SUBROWS):
            in_tile = r0 < TILE
            src, dsrc, usrc, q0 = (p_ref, dy_ref, u1_ref, r0) if in_tile else (pn_ref, dyn_ref, u1n_ref, r0 - TILE)
            u1 = usrc[q0:q0 + SUBROWS, :].astype(F32)
            n, rstd, u2 = _ln_fwd(u1, g_ref[...], b_ref[...])
            cz = src[q0:q0 + SUBROWS, 2 * D_C:3 * D_C].astype(F32)
            dyc = dsrc[q0:q0 + SUBROWS, :].astype(F32)
            if not in_tile:
                dyc = jnp.where(last, 0.0, dyc)
            du2 = dyc * _silu(cz) * _dsilu(u2)
            dn = du2 * g_ref[...]
            du1 = rstd * (dn - jnp.mean(dn, axis=-1, keepdims=True) - n * jnp.mean(dn * n, axis=-1, keepdims=True))
            du1_scr[0, r0:r0 + SUBROWS, :] = du1
            if in_tile:
                dp_ref[r0:r0 + SUBROWS, 2 * D_C:3 * D_C] = (dyc * _silu(u2) * _dsilu(cz)).astype(BF16)
                dg = dg + jnp.sum(du2 * n, axis=0, keepdims=True)
                db = db + jnp.sum(du2, axis=0, keepdims=True)
                dcb = dcb + jnp.sum(du1, axis=0, keepdims=True)
        dcb_ref[...] += dcb
        dg_ref[...] += dg
        db_ref[...] += db
        mask = _row_mask(i, tpe, TILE)
        _fill_shifted(du1_scr, du1_scr[0])
        for r0 in range(0, TILE, SUBROWS):
            acc = w[0:1] * _window(du1_scr, r0 + CONF_K - 1, SUBROWS)
            for k in range(1, CONF_K):
                acc = acc + w[k:k + 1] * _window(du1_scr, r0 + CONF_K - 1 - k, SUBROWS)
            ca = p_ref[r0:r0 + SUBROWS, 0:D_C].astype(F32)
            sg = jax.nn.sigmoid(p_ref[r0:r0 + SUBROWS, D_C:2 * D_C].astype(F32))
            m = mask[r0:r0 + SUBROWS]
            dp_ref[r0:r0 + SUBROWS, 0:D_C] = jnp.where(m, acc * sg, 0.0).astype(BF16)
            dp_ref[r0:r0 + SUBROWS, D_C:2 * D_C] = jnp.where(m, acc * ca * sg * (1.0 - sg), 0.0).astype(BF16)
        for k in range(CONF_K):
            part = jnp.zeros((SUB, D_C), F32)
            for r0 in range(0, TILE, SUBROWS):
                prod = du1_scr[0, r0:r0 + SUBROWS, :] * _window(u0_scr, HALO_C + r0 - (CONF_K - 1) + k, SUBROWS)
                for q in range(0, SUBROWS, SUB):
                    part = part + prod[q:q + SUB]
            wacc_scr[k:k + 1, :] = jnp.sum(part, axis=0, keepdims=True)
        dw_ref[...] += wacc_scr[0:CONF_K, :]

    vec = pl.BlockSpec((1, D_C), lambda i: (0, 0))
    wspec = pl.BlockSpec((CONF_K, D_C), lambda i: (0, 0))
    return pl.pallas_call(
        body, name=name, grid=(t // TILE,),
        in_specs=[dcur, dnxt, cur, prev, nxt, ucur, unxt, wspec, vec, vec, vec, HBM_SPEC],
        out_specs=[_window_spec(TILE, 3 * D_C, COL_C), wspec, vec, vec, vec],
        out_shape=[jax.ShapeDtypeStruct(dproj.shape, dproj.dtype), jax.ShapeDtypeStruct((CONF_K, D_C), F32),
                   jax.ShapeDtypeStruct((1, D_C), F32), jax.ShapeDtypeStruct((1, D_C), F32),
                   jax.ShapeDtypeStruct((1, D_C), F32)],
        scratch_shapes=[pltpu.VMEM((SUB, ext, D_C), F32), pltpu.VMEM((SUB, ext, D_C), F32),
                        pltpu.VMEM((HALO_C, D_C), F32)],
        input_output_aliases={11: 0},
        compiler_params=_cparams(("arbitrary",)),
    )(dy, dy, proj, proj, proj, u1, u1, w, cb, g, b, dproj)


def _split_dot(x, m_bf16, terms):
    acc = None
    rem = x
    for _ in range(terms):
        hi = rem.astype(BF16)
        part = jnp.dot(hi, m_bf16, preferred_element_type=F32)
        acc = part if acc is None else acc + part
        rem = rem - hi.astype(F32)
    return acc


def _split_dot_left(m_bf16, x, terms):
    acc = None
    rem = x
    for _ in range(terms):
        hi = rem.astype(BF16)
        part = jnp.dot(m_bf16, hi, preferred_element_type=F32)
        acc = part if acc is None else acc + part
        rem = rem - hi.astype(F32)
    return acc


def _tri(rows_ge_cols):
    r = lax.broadcasted_iota(jnp.int32, (CHUNK, CHUNK), 0)
    c = lax.broadcasted_iota(jnp.int32, (CHUNK, CHUNK), 1)
    return (r >= c) if rows_ge_cols else (r <= c)


def _softplus(x):
    return jnp.maximum(x, 0.0) + jnp.log(1.0 + jnp.exp(-jnp.abs(x)))


def _ssd_common(dtraw, dtb, dtb_t, a_log, a_log_t, e_mat, valid_col, valid_row, x_terms=2):
    a = -jnp.exp(a_log)
    lane = lax.broadcasted_iota(jnp.int32, (1, LANE), 1)
    a = jnp.where(lane < N_HEADS, a, 0.0)
    a_t = -jnp.exp(a_log_t)
    dt = jnp.where(valid_col, _softplus(dtraw + dtb), 0.0)
    dt = jnp.where(lane < N_HEADS, dt, 0.0)
    dt_t = jnp.where(valid_row, _softplus(dtraw.T[0:N_HEADS, :] + dtb_t), 0.0)
    ltri = _tri(True).astype(BF16)
    utri = _tri(False).astype(BF16)
    big_a = _split_dot_left(ltri, dt * a, 3)
    big_a_t = _split_dot(dt_t * a_t, utri, 3)
    e_a = jnp.exp(big_a)
    d_s = jnp.exp(big_a[CHUNK - 1:CHUNK, :] - big_a)
    dt_x = _split_dot(dt, e_mat, x_terms)
    e_a_x = _split_dot(e_a, e_mat, 2)
    d_s_x = _split_dot(d_s, e_mat, x_terms)
    cd_x = e_a_x[CHUNK - 1:CHUNK, :]
    return a, dt, big_a, big_a_t, e_a, d_s, dt_x, e_a_x, d_s_x, cd_x


def _decay(big_a, big_a_t, h, transposed):
    col = big_a[:, h:h + 1]
    row = big_a_t[h:h + 1, :]
    if not transposed:
        seg = col - row
        return jnp.where(_tri(True), jnp.exp(jnp.minimum(seg, 0.0)), 0.0)
    seg = row - col
    return jnp.where(_tri(False), jnp.exp(jnp.minimum(seg, 0.0)), 0.0)


NT_DIMS = (((1,), (1,)), ((), ()))
TN_DIMS = (((0,), (0,)), ((), ()))
HBM_SPEC = pl.BlockSpec(memory_space=pl.ANY)


class _Comm:
    def __init__(self, inputs, out_shapes, sem_shapes, copies, aliases=None):
        self.inputs, self.out_shapes, self.sem_shapes, self.copies = inputs, out_shapes, sem_shapes, copies
        self.aliases = aliases or {}

    def start(self, ins, outs, sems):
        local, sends, _ = self.copies(ins, outs, sems, False)
        for cp in local + sends:
            cp.start()

    def wait(self, ins, outs, sems):
        local, sends, recvs = self.copies(ins, outs, sems, True)
        for cp in recvs:
            cp.wait_recv()
        for cp in sends:
            cp.wait_send()
        for cp in local:
            cp.wait()


def _grid_call(body, name, grid, in_specs, out_specs, out_shape, scratch_shapes, operands, comm=None, aliases=None):
    aliases = dict(aliases or {})
    if comm is None:
        return pl.pallas_call(
            body, name=name, grid=grid, in_specs=in_specs, out_specs=out_specs, out_shape=out_shape,
            scratch_shapes=scratch_shapes, input_output_aliases=aliases,
            compiler_params=_cparams(("arbitrary",) * len(grid)))(*operands)
    n_in, n_out, n_scr = len(in_specs), len(out_specs), len(scratch_shapes)
    nci, nco = len(comm.inputs), len(comm.out_shapes)

    def wrapped(*refs):
        ins, cins = refs[:n_in], refs[n_in:n_in + nci]
        o0 = n_in + nci
        outs, couts = refs[o0:o0 + n_out], refs[o0 + n_out:o0 + n_out + nco]
        s0 = o0 + n_out + nco
        scr, csems = refs[s0:s0 + n_scr], refs[s0 + n_scr:]
        first = pl.program_id(0) == 0
        last = pl.program_id(0) == grid[0] - 1
        for k in range(1, len(grid)):
            first = jnp.logical_and(first, pl.program_id(k) == 0)
            last = jnp.logical_and(last, pl.program_id(k) == grid[k] - 1)

        @pl.when(first)
        def _():
            comm.start(cins, couts, csems)

        body(*ins, *outs, *scr)

        @pl.when(last)
        def _():
            comm.wait(cins, couts, csems)

    res = pl.pallas_call(
        wrapped, name=name, grid=grid,
        in_specs=list(in_specs) + [HBM_SPEC] * nci, out_specs=list(out_specs) + [HBM_SPEC] * nco,
        out_shape=list(out_shape) + list(comm.out_shapes),
        scratch_shapes=list(scratch_shapes) + list(comm.sem_shapes),
        input_output_aliases={**aliases, **{n_in + k: n_out + v for k, v in comm.aliases.items()}},
        compiler_params=_cparams(("arbitrary",) * len(grid)))(*operands, *comm.inputs)
    return res


def _ssd_fwd(xbc, proj, dtb, a_log, d_skip_x, norm_g, e_mat, ybuf, lp, name, comm=None):
    t = xbc.shape[0]
    cpe = lp // CHUNK
    nb = t // lp
    dtb_p = jnp.pad(dtb.reshape(1, N_HEADS), ((0, 0), (0, LANE - N_HEADS)))
    alog_p = jnp.pad(a_log.reshape(1, N_HEADS), ((0, 0), (0, LANE - N_HEADS)))
    dtb_t = dtb.reshape(N_HEADS, 1)
    alog_t = a_log.reshape(N_HEADS, 1)

    def body(xbc_ref, bz_ref, dt_ref, dtb_ref, dtbt_ref, al_ref, alt_ref, dx_ref, g_ref, e_ref, ybuf_ref,
             yb_ref, ys_ref, st_ref, s_scr):
        c = pl.program_id(0)

        @pl.when(c == 0)
        def _():
            s_scr[...] = jnp.zeros_like(s_scr)

        rows = lax.broadcasted_iota(jnp.int32, (CHUNK, 1), 0)
        cols = lax.broadcasted_iota(jnp.int32, (1, CHUNK), 1)
        valid_col = jnp.logical_or(c > 0, rows >= PAD)
        valid_row = jnp.logical_or(c > 0, cols >= PAD)
        e_mat = e_ref[...]
        for b in range(nb):
            _, _, big_a, big_a_t, _, _, dt_x, e_a_x, d_s_x, cd_x = _ssd_common(
                dt_ref[b].astype(F32), dtb_ref[...], dtbt_ref[...], al_ref[...], alt_ref[...], e_mat, valid_col, valid_row)
            xs = xbc_ref[b, :, 0:D_B].astype(F32)
            bs = xbc_ref[b, :, D_B:D_B + N_GROUPS * N_STATE]
            cs = xbc_ref[b, :, D_B + N_GROUPS * N_STATE:N_XBC]
            xdt = xs * dt_x
            xdt_b = xdt.astype(BF16)
            st_prev = s_scr[b]
            st_ref[b] = st_prev.astype(BF16)
            st_b = st_prev.astype(BF16)
            u_b = (xdt * d_s_x).astype(BF16)
            y_parts = []
            for g in range(N_GROUPS):
                gs = slice(g * N_STATE, (g + 1) * N_STATE)
                gw = slice(g * GROUP_W, (g + 1) * GROUP_W)
                cb = lax.dot_general(cs[:, gs], bs[:, gs], NT_DIMS, preferred_element_type=F32)
                y_off = jnp.dot(cs[:, gs], st_b[:, gw], preferred_element_type=F32)
                diag = []
                for e in range(N_HEADS // N_GROUPS):
                    h = g * (N_HEADS // N_GROUPS) + e
                    m = (cb * _decay(big_a, big_a_t, h, False)).astype(BF16)
                    diag.append(jnp.dot(m, xdt_b[:, h * HEAD_DIM:(h + 1) * HEAD_DIM], preferred_element_type=F32))
                y_parts.append(jnp.concatenate(diag, axis=1) + y_off * e_a_x[:, gw])
                new_st = lax.dot_general(bs[:, gs], u_b[:, gw], TN_DIMS, preferred_element_type=F32)
                s_scr[b, :, gw] = st_prev[:, gw] * cd_x[:, gw] + new_st
            y = jnp.concatenate(y_parts, axis=1) + xs * dx_ref[...]
            ys_ref[b] = y
            z = y * _silu(bz_ref[b].astype(F32))
            r = lax.rsqrt(jnp.mean(z * z, axis=-1, keepdims=True) + NORM_EPS)
            yb_ref[b] = (z * r * g_ref[...]).astype(BF16)

    def row(width, col):
        return pl.BlockSpec((nb, CHUNK, width), lambda c: (0, c, col))

    def const(shape):
        return pl.BlockSpec(shape, lambda c: (0,) * len(shape))

    proj3 = proj.reshape(nb, lp, N_INP)
    ybuf, yssd, states, *rest = _grid_call(
        body, name, (cpe,),
        [row(N_XBC, 0), row(D_B, COL_BZ // D_B), row(LANE, COL_DT // LANE),
         const((1, LANE)), const((N_HEADS, 1)), const((1, LANE)), const((N_HEADS, 1)),
         const((1, D_B)), const((1, D_B)), const((LANE, D_B)), HBM_SPEC],
        [row(D_B, 0), row(D_B, 0), pl.BlockSpec((nb, None, N_STATE, D_B), lambda c: (0, c, 0, 0))],
        [jax.ShapeDtypeStruct((nb, lp, ybuf.shape[1]), ybuf.dtype), jax.ShapeDtypeStruct((nb, lp, D_B), F32),
         jax.ShapeDtypeStruct((nb, cpe, N_STATE, D_B), BF16)],
        [pltpu.VMEM((nb, N_STATE, D_B), F32)],
        (xbc.reshape(nb, lp, N_XBC), proj3, proj3, dtb_p, dtb_t, alog_p, alog_t, d_skip_x, norm_g, e_mat,
         ybuf.reshape(nb, lp, ybuf.shape[1])), comm, aliases={10: 0})
    return (ybuf.reshape(t, -1), yssd.reshape(t, D_B), states, *rest)


def _ssd_bwd(dy, y_ssd, xbc, proj, states, dtb, a_log, d_skip_x, norm_g, e_mat, e_mat_t, lp, name, comm=None):
    t = xbc.shape[0]
    cpe = lp // CHUNK
    nb = t // lp
    hpg = N_HEADS // N_GROUPS
    dtb_p = jnp.pad(dtb.reshape(1, N_HEADS), ((0, 0), (0, LANE - N_HEADS)))
    alog_p = jnp.pad(a_log.reshape(1, N_HEADS), ((0, 0), (0, LANE - N_HEADS)))
    dtb_t = dtb.reshape(N_HEADS, 1)
    alog_t = a_log.reshape(N_HEADS, 1)

    def body(dy_ref, ys_ref, xbc_ref, bz_ref, dt_ref, st_ref, dtb_ref, dtbt_ref, al_ref, alt_ref, dx_ref, g_ref,
             e_ref, et_ref, dxbc_ref, dpw_ref, dg_ref, ddtb_ref, dal_ref, dd_ref, ds_scr):
        @pl.when(pl.program_id(0) == 0)
        def _():
            ds_scr[...] = jnp.zeros_like(ds_scr)
            dg_ref[...] = jnp.zeros_like(dg_ref)
            ddtb_ref[...] = jnp.zeros_like(ddtb_ref)
            dal_ref[...] = jnp.zeros_like(dal_ref)
            dd_ref[...] = jnp.zeros_like(dd_ref)

        for b in range(nb):
            one_example(dy_ref.at[b], ys_ref.at[b], xbc_ref.at[b], bz_ref.at[b], dt_ref.at[b], st_ref.at[b], dtb_ref,
                        dtbt_ref, al_ref, alt_ref, dx_ref, g_ref, e_ref, et_ref, dxbc_ref.at[b], dpw_ref.at[b], dg_ref,
                        ddtb_ref, dal_ref, dd_ref, ds_scr.at[b])

    def one_example(dy_ref, ys_ref, xbc_ref, bz_ref, dt_ref, st_ref, dtb_ref, dtbt_ref, al_ref, alt_ref, dx_ref, g_ref,
                    e_ref, et_ref, dxbc_ref, dpw_ref, dg_ref, ddtb_ref, dal_ref, dd_ref, ds_scr):
        cc = cpe - 1 - pl.program_id(0)
        rows = lax.broadcasted_iota(jnp.int32, (CHUNK, 1), 0)
        cols = lax.broadcasted_iota(jnp.int32, (1, CHUNK), 1)
        valid_col = jnp.logical_or(cc > 0, rows >= PAD)
        valid_row = jnp.logical_or(cc > 0, cols >= PAD)
        e_mat = e_ref[...]
        e_mat_t = et_ref[...]
        dtraw = dt_ref[...].astype(F32)
        a, dt, big_a, big_a_t, e_a, d_s, dt_x, e_a_x, d_s_x, cd_x = _ssd_common(
            dtraw, dtb_ref[...], dtbt_ref[...], al_ref[...], alt_ref[...], e_mat, valid_col, valid_row, x_terms=1)
        xs = xbc_ref[:, 0:D_B].astype(F32)
        bs = xbc_ref[:, D_B:D_B + N_GROUPS * N_STATE]
        cs = xbc_ref[:, D_B + N_GROUPS * N_STATE:N_XBC]
        xdt = xs * dt_x
        xdt_b = xdt.astype(BF16)
        st_b = st_ref[...]
        dst = ds_scr[...]
        dst_b = dst.astype(BF16)

        ys = ys_ref[...]
        bz = bz_ref[...].astype(F32)
        sil = _silu(bz)
        z = ys * sil
        dz, dgt = _rms_bwd_math(z, g_ref[...], dy_ref[...].astype(F32))
        dg_ref[...] += jnp.sum(dgt, axis=0, keepdims=True)
        dpw_ref[:, 0:D_B] = (dz * ys * _dsilu(bz)).astype(BF16)
        dys = dz * sil

        dd_lane = jnp.sum(dys * xs, axis=0, keepdims=True)
        dxs = dys * dx_ref[...]
        w_x = dys * e_a_x
        w_b = w_x.astype(BF16)
        dys_b = dys.astype(BF16)
        u_b = (xdt * d_s_x).astype(BF16)
        dxdt_parts, dbs_parts, dcs_parts, off_parts, g1_parts = [], [], [], [], []
        da_diag = jnp.zeros((CHUNK, LANE), F32)
        lane = lax.broadcasted_iota(jnp.int32, (1, LANE), 1)
        for g in range(N_GROUPS):
            gs = slice(g * N_STATE, (g + 1) * N_STATE)
            gw = slice(g * GROUP_W, (g + 1) * GROUP_W)
            cs_g, bs_g = cs[:, gs], bs[:, gs]
            dcs = lax.dot_general(w_b[:, gw], st_b[:, gw], NT_DIMS, preferred_element_type=F32)
            y_off = jnp.dot(cs_g, st_b[:, gw], preferred_element_type=F32)
            off_parts.append(y_off)
            dst_new = lax.dot_general(cs_g, w_b[:, gw], TN_DIMS, preferred_element_type=F32)
            g1 = jnp.dot(bs_g, dst_b[:, gw], preferred_element_type=F32)
            g1_parts.append(g1)
            dbs = lax.dot_general(u_b[:, gw], dst_b[:, gw], NT_DIMS, preferred_element_type=F32)
            cb = lax.dot_general(cs_g, bs_g, NT_DIMS, preferred_element_type=F32)
            cbt = lax.dot_general(bs_g, cs_g, NT_DIMS, preferred_element_type=F32)
            dcb = jnp.zeros((CHUNK, CHUNK), F32)
            dcbt = jnp.zeros((CHUNK, CHUNK), F32)
            dxdt_h = []
            for e in range(hpg):
                h = g * hpg + e
                hs = slice(h * HEAD_DIM, (h + 1) * HEAD_DIM)
                dec = _decay(big_a, big_a_t, h, False)
                dect = _decay(big_a, big_a_t, h, True)
                m = cb * dec
                mt = cbt * dect
                dxdt_h.append(jnp.dot(mt.astype(BF16), dys_b[:, hs], preferred_element_type=F32))
                dm = lax.dot_general(dys_b[:, hs], xdt_b[:, hs], NT_DIMS, preferred_element_type=F32)
                dmt = lax.dot_general(xdt_b[:, hs], dys_b[:, hs], NT_DIMS, preferred_element_type=F32)
                dcb = dcb + dm * dec
                dcbt = dcbt + dmt * dect
                da_h = jnp.sum(dm * m - dmt * mt, axis=1, keepdims=True)
                da_diag = da_diag + jnp.where(lane == h, da_h, 0.0)
            dcs = dcs + jnp.dot(dcb.astype(BF16), bs_g, preferred_element_type=F32)
            dbs = dbs + jnp.dot(dcbt.astype(BF16), cs_g, preferred_element_type=F32)
            dxdt_parts.append(jnp.concatenate(dxdt_h, axis=1) + g1 * d_s_x[:, gw])
            dbs_parts.append(dbs)
            dcs_parts.append(dcs)
            ds_scr[:, gw] = dst[:, gw] * cd_x[:, gw] + dst_new
        dxdt = jnp.concatenate(dxdt_parts, axis=1)
        y_off = jnp.concatenate(off_parts, axis=1)
        g1 = jnp.concatenate(g1_parts, axis=1)
        dds = _split_dot(g1 * xdt, e_mat_t, 1)
        da_off = _split_dot(w_x * y_off, e_mat_t, 1)
        ddt_x = _split_dot(dxdt * xs, e_mat_t, 1)
        dcd_lane = jnp.sum(dst * st_b.astype(F32), axis=0, keepdims=True)
        dcd = _split_dot(jnp.broadcast_to(dcd_lane, (SUB, D_B)), e_mat_t, 1)[0:1]
        dd_ref[...] += _split_dot(jnp.broadcast_to(dd_lane, (SUB, D_B)), e_mat_t, 1)[0:1]
        t_ds = dds * d_s
        d_a = da_diag + da_off - t_ds
        last_row = jnp.sum(t_ds, axis=0, keepdims=True) + dcd * e_a[CHUNK - 1:CHUNK, :]
        d_a = d_a + jnp.where(rows == CHUNK - 1, last_row, 0.0)
        dda = _split_dot_left(_tri(False).astype(BF16), d_a, 3)
        ddt = dda * a + ddt_x
        dal_ref[...] += jnp.sum(dda * dt * a, axis=0, keepdims=True)
        ddtraw = jnp.where(valid_col, ddt * jax.nn.sigmoid(dtraw + dtb_ref[...]), 0.0)
        ddtraw = jnp.where(lane < N_HEADS, ddtraw, 0.0)
        ddtb_ref[...] += jnp.sum(ddtraw, axis=0, keepdims=True)
        dpw_ref[:, D_B:D_B + LANE] = ddtraw.astype(BF16)
        dxs = dxs + dxdt * dt_x
        dxbc = jnp.concatenate([dxs] + dbs_parts + dcs_parts, axis=1)
        dxbc_ref[...] = jnp.where(valid_col, dxbc, 0.0).astype(BF16)

    def row(width, col):
        return pl.BlockSpec((nb, CHUNK, width), lambda c: (0, cpe - 1 - c, col))

    def const(shape):
        return pl.BlockSpec(shape, lambda c: (0,) * len(shape))

    proj3 = proj.reshape(nb, lp, N_INP)
    dxbc, dproj, *rest = _grid_call(
        body, name, (cpe,),
        [row(D_B, 0), row(D_B, 0), row(N_XBC, 0), row(D_B, COL_BZ // D_B), row(LANE, COL_DT // LANE),
         pl.BlockSpec((nb, None, N_STATE, D_B), lambda c: (0, cpe - 1 - c, 0, 0)),
         const((1, LANE)), const((N_HEADS, 1)), const((1, LANE)), const((N_HEADS, 1)),
         const((1, D_B)), const((1, D_B)), const((LANE, D_B)), const((D_B, LANE))],
        [row(N_XBC, 0),
         pl.BlockSpec((pl.Element(nb), pl.Element(CHUNK), pl.Element(D_B + LANE)),
                      lambda c: (0, (cpe - 1 - c) * CHUNK, COL_BZ)),
         const((1, D_B)), const((1, LANE)), const((1, LANE)), const((1, LANE))],
        [jax.ShapeDtypeStruct((nb, lp, N_XBC), BF16), jax.ShapeDtypeStruct((nb, lp, N_INP), BF16),
         jax.ShapeDtypeStruct((1, D_B), F32),
         jax.ShapeDtypeStruct((1, LANE), F32), jax.ShapeDtypeStruct((1, LANE), F32),
         jax.ShapeDtypeStruct((1, LANE), F32)],
        [pltpu.VMEM((nb, N_STATE, D_B), F32)],
        (dy.reshape(nb, lp, -1), y_ssd.reshape(nb, lp, D_B), xbc.reshape(nb, lp, N_XBC), proj3, proj3, states,
         dtb_p, dtb_t, alog_p, alog_t, d_skip_x, norm_g, e_mat, e_mat_t), comm)
    return (dxbc.reshape(t, N_XBC), dproj.reshape(t, N_INP), *rest)


HBM_SPEC = pl.BlockSpec(memory_space=pl.ANY)


def _mesh_pos():
    return lax.axis_index("x"), lax.axis_index("y"), lax.axis_index("c")


def _allgather(arrays, name):
    n = len(arrays)

    def body(*refs):
        xs, outs = refs[:n], refs[n:2 * n]
        send_sems, recv_sems, local_sems = refs[2 * n:]
        x, y, c = _mesh_pos()
        me, sibling = (x, y, c), (x, y, 1 - c)
        chips = [(1 - x, y), (x, 1 - y), (1 - x, 1 - y)]

        def slot(px, py, pc):
            return 4 * px + 2 * py + pc

        def copy(a, k, block, to, src=None):
            dst = outs[a].at[slot(*block)]
            return pltpu.make_async_remote_copy(
                src_ref=dst if src is None else src, dst_ref=dst, send_sem=send_sems.at[a, k], recv_sem=recv_sems.at[a, k],
                device_id=to, device_id_type=MESH)

        mine = [pltpu.make_async_copy(xs[a], outs[a].at[slot(*me)], local_sems.at[a]) for a in range(n)]
        for cp in mine:
            cp.start()
        first = []
        for a in range(n):
            first.append(copy(a, 0, me, sibling, src=xs[a]))
            first += [copy(a, 1 + j, me, (*chip, c), src=xs[a]) for j, chip in enumerate(chips)]
        for cp in first:
            cp.start()
        passed = []
        for j, chip in enumerate(chips):
            for a in range(n):
                copy(a, 1 + j, (*chip, c), me).wait_recv()
                cp = copy(a, 4 + j, (*chip, c), sibling)
                cp.start()
                passed.append(cp)
        for a in range(n):
            copy(a, 0, sibling, me).wait_recv()
        for j, chip in enumerate(chips):
            for a in range(n):
                copy(a, 4 + j, (*chip, 1 - c), me).wait_recv()
        for cp in first + passed:
            cp.wait_send()
        for cp in mine:
            cp.wait()

    return pl.pallas_call(
        body, name=name,
        in_specs=[HBM_SPEC] * n, out_specs=[HBM_SPEC] * n,
        out_shape=[jax.ShapeDtypeStruct((N_DEV,) + a.shape, a.dtype) for a in arrays],
        scratch_shapes=[pltpu.SemaphoreType.DMA((n, 7)), pltpu.SemaphoreType.DMA((n, 7)), pltpu.SemaphoreType.DMA((n,))],
    )(*arrays)


def _remote(src, dst, send_sem, recv_sem, to):
    return pltpu.make_async_remote_copy(src_ref=src, dst_ref=dst, send_sem=send_sem, recv_sem=recv_sem,
                                        device_id=to, device_id_type=MESH)


def _slot_plain(d):
    return d


def _slot_mix_rows(d):
    return jnp.where(d < 2, d + 4, jnp.where(d < 6, d - 2, d))


def _ag_ici(pieces, slots):
    n = len(pieces)

    def copies(ins, outs, sems, with_recvs):
        send_sems, recv_sems, local_sems = sems
        x, y, c = _mesh_pos()
        local, sends, recvs = [], [], []
        for a in range(n):
            mine = outs[a].at[slots[a](4 * x + 2 * y + c)]
            local.append(pltpu.make_async_copy(ins[a], mine, local_sems.at[a]))
            for k, (px, py) in enumerate([(1 - x, y), (x, 1 - y), (1 - x, 1 - y)]):
                sends.append(_remote(ins[a], mine, send_sems.at[a, k], recv_sems.at[a, k], (px, py, c)))
                if with_recvs:
                    theirs = outs[a].at[slots[a](4 * px + 2 * py + c)]
                    recvs.append(_remote(ins[a], theirs, send_sems.at[a, k], recv_sems.at[a, k], (px, py, c)))
        return local, sends, recvs

    return _Comm(pieces, [jax.ShapeDtypeStruct((N_DEV,) + p.shape, p.dtype) for p in pieces],
                 [pltpu.SemaphoreType.DMA((n, 3)), pltpu.SemaphoreType.DMA((n, 3)), pltpu.SemaphoreType.DMA((n,))], copies)


def _ag_d2d(bufs, slots):
    n = len(bufs)

    def copies(ins, outs, sems, with_recvs):
        send_sems, recv_sems = sems
        x, y, c = _mesh_pos()
        chips = [(x, y), (1 - x, y), (x, 1 - y), (1 - x, 1 - y)]
        sends, recvs = [], []
        for a in range(n):
            for k, (px, py) in enumerate(chips):
                held = slots[a](4 * px + 2 * py + c)
                sends.append(_remote(ins[a].at[held], outs[a].at[held], send_sems.at[a, k], recv_sems.at[a, k], (x, y, 1 - c)))
                if with_recvs:
                    got = slots[a](4 * px + 2 * py + 1 - c)
                    recvs.append(_remote(ins[a].at[got], outs[a].at[got], send_sems.at[a, k], recv_sems.at[a, k], (x, y, 1 - c)))
        return [], sends, recvs

    return _Comm(bufs, [jax.ShapeDtypeStruct(b.shape, b.dtype) for b in bufs],
                 [pltpu.SemaphoreType.DMA((n, 4)), pltpu.SemaphoreType.DMA((n, 4))], copies,
                 aliases={a: a for a in range(n)})


def _rs_d2d(blocks, slots):
    n = len(blocks)

    def copies(ins, outs, sems, with_recvs):
        send_sems, recv_sems = sems
        x, y, c = _mesh_pos()
        sends, recvs = [], []
        for a in range(n):
            for j in range(4):
                src = ins[a].at[slots[a](2 * j + 1 - c)]
                sends.append(_remote(src, outs[a].at[j], send_sems.at[a, j], recv_sems.at[a, j], (x, y, 1 - c)))
                if with_recvs:
                    recvs.append(_remote(src, outs[a].at[j], send_sems.at[a, j], recv_sems.at[a, j], (x, y, 1 - c)))
        return [], sends, recvs

    return _Comm(blocks, [jax.ShapeDtypeStruct((4,) + b.shape[1:], b.dtype) for b in blocks],
                 [pltpu.SemaphoreType.DMA((n, 4)), pltpu.SemaphoreType.DMA((n, 4))], copies)


def _rs_ici(blocks):
    n = len(blocks)

    def copies(ins, outs, sems, with_recvs):
        send_sems, recv_sems = sems
        x, y, c = _mesh_pos()
        sends, recvs = [], []
        for a in range(n):
            for k, (px, py) in enumerate([(1 - x, y), (x, 1 - y), (1 - x, 1 - y)]):
                src = ins[a].at[2 * px + py]
                sends.append(_remote(src, outs[a].at[k], send_sems.at[a, k], recv_sems.at[a, k], (px, py, c)))
                if with_recvs:
                    recvs.append(_remote(src, outs[a].at[k], send_sems.at[a, k], recv_sems.at[a, k], (px, py, c)))
        return [], sends, recvs

    return _Comm(blocks, [jax.ShapeDtypeStruct((3,) + b.shape[1:], b.dtype) for b in blocks],
                 [pltpu.SemaphoreType.DMA((n, 3)), pltpu.SemaphoreType.DMA((n, 3))], copies)


def _run_comm(comm, name):
    n_in, n_out = len(comm.inputs), len(comm.out_shapes)

    def body(*refs):
        ins, outs, sems = refs[:n_in], refs[n_in:n_in + n_out], refs[n_in + n_out:]
        comm.start(ins, outs, sems)
        comm.wait(ins, outs, sems)

    return pl.pallas_call(
        body, name=name, in_specs=[HBM_SPEC] * n_in, out_specs=[HBM_SPEC] * n_out, out_shape=comm.out_shapes,
        scratch_shapes=comm.sem_shapes, input_output_aliases=comm.aliases,
    )(*comm.inputs)


W_ROWS = 784


def _w_segments():
    per = N_IN // N_DEV
    out = []
    for d in range(N_DEV):
        lo, hi = per * d, per * (d + 1)
        for a, b, start in COL_MAP:
            s, e = max(lo, a), min(hi, b)
            if s < e:
                out.append((d, s - lo, e - s, s - a + start))
    return out


def _w_gather_rows(g, name):
    tk = D_MODEL // 2
    u32 = jnp.uint32

    def body(g_ref, o_ref, scr):
        for d in range(N_DEV):
            x32 = pltpu.bitcast(g_ref[d], u32)
            for dd, src, rows, dst in _w_segments():
                if dd == d:
                    scr[dst // 2:(dst + rows) // 2, :] = x32[src // 2:(src + rows) // 2]
        scr[(COL_DT + N_HEADS) // 2:COL_XBC // 2, :] = jnp.zeros(((COL_XBC - COL_DT - N_HEADS) // 2, tk), u32)
        o_ref[...] = pltpu.bitcast(scr[...], BF16)

    return pl.pallas_call(
        body, name=name, grid=(D_MODEL // tk,),
        in_specs=[pl.BlockSpec((N_DEV, W_ROWS, tk), lambda j: (0, 0, j))],
        out_specs=pl.BlockSpec((N_INP, tk), lambda j: (0, j)),
        out_shape=jax.ShapeDtypeStruct((N_INP, D_MODEL), BF16),
        scratch_shapes=[pltpu.VMEM((N_INP // 2, tk), u32)],
        compiler_params=_cparams(("parallel",)),
    )(g)


def _w_split_rows(dwt, name):
    tk = D_MODEL // 4
    per = N_IN // N_DEV

    def body(w_ref, o_ref):
        for d, dst, rows, src in _w_segments():
            o_ref[d, dst // 2:(dst + rows) // 2, :] = pltpu.bitcast(w_ref[src:src + rows, :].astype(BF16), F32)
        for d in range(N_DEV):
            o_ref[d, per // 2:W_ROWS // 2, :] = jnp.zeros(((W_ROWS - per) // 2, tk), F32)

    return pl.pallas_call(
        body, name=name, grid=(D_MODEL // tk,),
        in_specs=[pl.BlockSpec((N_INP, tk), lambda j: (0, j))],
        out_specs=pl.BlockSpec((N_DEV, W_ROWS // 2, tk), lambda j: (0, 0, j)),
        out_shape=jax.ShapeDtypeStruct((N_DEV, W_ROWS // 2, D_MODEL), F32),
        compiler_params=_cparams(("parallel",)),
    )(dwt)


def _rs_pair_sum(g, ra, own_slots, name):
    packed = g.dtype == F32
    _, rows, cols = g.shape
    out_rows = 2 * rows if packed else rows

    def body(s_ref, g_ref, ra_ref, o_ref):
        a, b = g_ref[...], ra_ref[...]
        if packed:
            a, b = pltpu.bitcast(a, BF16), pltpu.bitcast(b, BF16)
        o_ref[...] = (a.astype(F32) + b.astype(F32)).astype(BF16)

    return pl.pallas_call(
        body, name=name,
        grid_spec=pltpu.PrefetchScalarGridSpec(
            num_scalar_prefetch=1, grid=(4,),
            in_specs=[pl.BlockSpec((None, rows, cols), lambda j, s: (s[j], 0, 0)),
                      pl.BlockSpec((None, rows, cols), lambda j, s: (j, 0, 0))],
            out_specs=pl.BlockSpec((None, out_rows, cols), lambda j, s: (j, 0, 0))),
        out_shape=jax.ShapeDtypeStruct((4, out_rows, cols), BF16),
        compiler_params=_cparams(("parallel",)),
    )(own_slots, g, ra)


def _rs_final_sum(h, rb, chip_idx, name):
    _, rows, cols = h.shape

    def body(j_ref, h_ref, rb_ref, o_ref):
        o_ref[...] = ((h_ref[...].astype(F32) + rb_ref[0].astype(F32)) + rb_ref[1].astype(F32)) + rb_ref[2].astype(F32)

    return pl.pallas_call(
        body, name=name,
        grid_spec=pltpu.PrefetchScalarGridSpec(
            num_scalar_prefetch=1, grid=(1,),
            in_specs=[pl.BlockSpec((None, rows, cols), lambda i, j: (j[0], 0, 0)),
                      pl.BlockSpec((3, rows, cols), lambda i, j: (0, 0, 0))],
            out_specs=pl.BlockSpec((rows, cols), lambda i, j: (0, 0))),
        out_shape=jax.ShapeDtypeStruct((rows, cols), F32),
        compiler_params=_cparams(("arbitrary",)),
    )(chip_idx, h, rb)


def _adamw_math(w, g, m, v):
    m = ADAM_B1 * m + (1.0 - ADAM_B1) * g
    v = ADAM_B2 * v + (1.0 - ADAM_B2) * (g * g)
    m_hat = m / (1.0 - ADAM_B1 ** ADAM_STEP)
    v_hat = v / (1.0 - ADAM_B2 ** ADAM_STEP)
    delta = -ADAM_LR * (m_hat / (jnp.sqrt(v_hat) + ADAM_EPS) + ADAM_WD * w)
    return delta, m, v


def _adamw_rows(g, w, m, v, tr, name):
    layers, rows, cols = w.shape

    def body(g_ref, w_ref, m_ref, v_ref, d_out, m_out, v_out):
        delta, m_new, v_new = _adamw_math(w_ref[...], g_ref[...], m_ref[...], v_ref[...])
        d_out[...] = delta
        m_out[...] = m_new
        v_out[...] = v_new

    blk = pl.BlockSpec((None, tr, cols), lambda a, r: (a, r, 0))
    return pl.pallas_call(
        body, name=name, grid=(layers, rows // tr),
        in_specs=[blk] * 4, out_specs=[blk] * 3,
        out_shape=[jax.ShapeDtypeStruct((layers, rows, cols), F32)] * 3,
        compiler_params=_cparams(("parallel", "parallel")),
    )(g, w, m, v)


def _sum_devices(parts, name):
    _, p, _ = parts.shape

    def body(x_ref, o_ref):
        acc = x_ref[0]
        for d in range(1, N_DEV):
            acc = acc + x_ref[d]
        o_ref[...] = acc

    return pl.pallas_call(
        body, name=name, grid=(1,),
        in_specs=[pl.BlockSpec((N_DEV, p, LANE), lambda i: (0, 0, 0))],
        out_specs=pl.BlockSpec((p, LANE), lambda i: (0, 0)),
        out_shape=jax.ShapeDtypeStruct((p, LANE), F32),
        compiler_params=_cparams(("arbitrary",)),
    )(parts)


def _adamw_small(g, w, m, v, name):
    p = g.shape[0]

    def body(g_ref, w_ref, m_ref, v_ref, d_out, m_out, v_out):
        delta, m_new, v_new = _adamw_math(w_ref[...], g_ref[...], m_ref[...], v_ref[...])
        d_out[...] = delta
        m_out[...] = m_new
        v_out[...] = v_new

    spec = pl.BlockSpec((p, LANE), lambda i: (0, 0))
    return pl.pallas_call(
        body, name=name, grid=(1,),
        in_specs=[spec] * 4, out_specs=[spec] * 3,
        out_shape=[jax.ShapeDtypeStruct((p, LANE), F32)] * 3,
        compiler_params=_cparams(("arbitrary",)),
    )(g, w, m, v)


PACK_ALIGN = SUB * LANE

SMALL_PARAMS = (
    ("meta", (N_META, D_MODEL), 1),
    ("pre_g", (DEPTH, D_MODEL), None),
    ("post_g", (DEPTH, D_MODEL), None),
    ("conv_a_w", (DEPTH, CONV_A_K, D_A), 2),
    ("ssm_conv_w", (DEPTH, SSM_K, N_XBC), 2),
    ("ssm_conv_b", (DEPTH, N_XBC), None),
    ("dt_bias", (DEPTH, N_HEADS), None),
    ("a_log", (DEPTH, N_HEADS), None),
    ("d_skip", (DEPTH, N_HEADS), None),
    ("ssm_norm_g", (DEPTH, D_B), None),
    ("conf_conv_w", (DEPTH, CONF_K, D_C), 2),
    ("conf_conv_b", (DEPTH, D_C), None),
    ("conf_ln_g", (DEPTH, D_C), None),
    ("conf_ln_b", (DEPTH, D_C), None),
)


def _local_shape(shape, axis):
    if axis is None:
        return shape
    return tuple(s // N_DEV if k == axis else s for k, s in enumerate(shape))


def _pack(arrays):
    flat = []
    for a in arrays:
        v = a.reshape(-1).astype(F32)
        flat.append(v)
        if v.shape[0] % PACK_ALIGN:
            flat.append(jnp.zeros(((-v.shape[0]) % PACK_ALIGN,), F32))
    return jnp.concatenate(flat).reshape(-1, LANE)


def _unpack(buf, shapes):
    flat = buf.reshape(-1)
    out, off = [], 0
    for s in shapes:
        size = 1
        for k in s:
            size *= k
        out.append(flat[off:off + size].reshape(s))
        off += size + (-size) % PACK_ALIGN
    return out


def kernel(x, meta, pre_g, post_g, w_in, w_out, conv_a_w, ssm_conv_w, ssm_conv_b, dt_bias, a_log, d_skip, ssm_norm_g, conf_conv_w, conf_conv_b, conf_ln_g, conf_ln_b, loss_target, m_meta, m_pre_g, m_post_g, m_w_in, m_w_out, m_conv_a_w, m_ssm_conv_w, m_ssm_conv_b, m_dt_bias, m_a_log, m_d_skip, m_ssm_norm_g, m_conf_conv_w, m_conf_conv_b, m_conf_ln_g, m_conf_ln_b, v_meta, v_pre_g, v_post_g, v_w_in, v_w_out, v_conv_a_w, v_ssm_conv_w, v_ssm_conv_b, v_dt_bias, v_a_log, v_d_skip, v_ssm_norm_g, v_conf_conv_w, v_conf_conv_b, v_conf_ln_g, v_conf_ln_b):
    weights = dict(meta=meta, pre_g=pre_g, post_g=post_g, conv_a_w=conv_a_w, ssm_conv_w=ssm_conv_w, ssm_conv_b=ssm_conv_b,
                   dt_bias=dt_bias, a_log=a_log, d_skip=d_skip, ssm_norm_g=ssm_norm_g, conf_conv_w=conf_conv_w,
                   conf_conv_b=conf_conv_b, conf_ln_g=conf_ln_g, conf_ln_b=conf_ln_b)
    mom1 = dict(meta=m_meta, pre_g=m_pre_g, post_g=m_post_g, conv_a_w=m_conv_a_w, ssm_conv_w=m_ssm_conv_w,
                ssm_conv_b=m_ssm_conv_b, dt_bias=m_dt_bias, a_log=m_a_log, d_skip=m_d_skip, ssm_norm_g=m_ssm_norm_g,
                conf_conv_w=m_conf_conv_w, conf_conv_b=m_conf_conv_b, conf_ln_g=m_conf_ln_g, conf_ln_b=m_conf_ln_b)
    mom2 = dict(meta=v_meta, pre_g=v_pre_g, post_g=v_post_g, conv_a_w=v_conv_a_w, ssm_conv_w=v_ssm_conv_w,
                ssm_conv_b=v_ssm_conv_b, dt_bias=v_dt_bias, a_log=v_a_log, d_skip=v_d_skip, ssm_norm_g=v_ssm_norm_g,
                conf_conv_w=v_conf_conv_w, conf_conv_b=v_conf_conv_b, conf_ln_g=v_conf_ln_g, conf_ln_b=v_conf_ln_b)
    nb, seq, d = x.shape
    lp = PAD + N_META + seq
    t = nb * lp
    assert lp % TILE == 0 and t % (3 * LANE) == 0 and d == D_MODEL
    rt = lp // 3
    xi, yi, ci = _mesh_pos()
    dev = 4 * xi + 2 * yi + ci
    ci32 = ci.astype(jnp.int32)
    chip_idx = (2 * xi + yi).astype(jnp.int32).reshape(1)
    own_plain = jnp.stack([2 * j + ci32 for j in range(4)])
    own_mix = jnp.stack([_slot_mix_rows(2 * j + ci32) for j in range(4)]).astype(jnp.int32)
    n_in_loc = N_IN // N_DEV
    n_out_loc = 2 * D_MODEL // N_DEV
    slots = (_slot_plain, _slot_mix_rows)

    sharded_small = [n for n, _, ax in SMALL_PARAMS if ax is not None]
    small_shapes = {n: s for n, s, _ in SMALL_PARAMS}
    small_axis = {n: ax for n, _, ax in SMALL_PARAMS}
    sw_pack = _pack([weights[n] for n in sharded_small])
    (sw_g,) = _allgather([sw_pack], "ag_small_weights")
    wt_loc = jnp.pad(jnp.swapaxes(w_in, 1, 2).astype(BF16), ((0, 0), (0, W_ROWS - n_in_loc), (0, 0)))
    wo_loc = w_out.astype(BF16)

    def gathered(bufs):
        return _w_gather_rows(bufs[0], "w_gather_rows"), bufs[1].reshape(2 * D_MODEL, D_MODEL)

    full = dict(weights)
    per_dev = [_unpack(sw_g[k], [_local_shape(small_shapes[n], small_axis[n]) for n in sharded_small]) for k in range(N_DEV)]
    for q, n in enumerate(sharded_small):
        full[n] = jnp.concatenate([per_dev[k][q] for k in range(N_DEV)], axis=small_axis[n])

    e_mat = (lax.broadcasted_iota(jnp.int32, (LANE, D_B), 0) == lax.broadcasted_iota(jnp.int32, (LANE, D_B), 1) // HEAD_DIM)
    e_mat = e_mat.astype(BF16)
    e_mat_t = e_mat.T

    front = jnp.concatenate([jnp.zeros((PAD, d), F32), full["meta"]], axis=0)
    h = jnp.concatenate([jnp.concatenate([front, x[b]], axis=0) for b in range(nb)], axis=0)
    saved = []
    w_t, w_o = gathered(_run_comm(_ag_d2d(_run_comm(_ag_ici([wt_loc[0], wo_loc[0]], slots), "ag_ici"), slots), "ag_d2d"))
    hn = _rms_fwd(h, pre_g[0].reshape(1, -1), rt, "rms_fwd")
    for i in range(DEPTH):
        row = lambda a: a[i].reshape(1, -1)
        proj = _mm(hn, w_t, "nt", BF16, t // 3, MM_TN, D_MODEL, "mm_proj", cols_outer=True)
        ycat = _a_fwd(proj, full["conv_a_w"][i], lp, "a_fwd")
        xbc = _xbc_fwd(proj, full["ssm_conv_w"][i], row(ssm_conv_b), lp, "xbc_fwd")
        ycat, u1 = _c_fwd(proj, full["conf_conv_w"][i], row(conf_conv_b), row(conf_ln_g), row(conf_ln_b), ycat, lp, "c_fwd")
        d_skip_x = jnp.repeat(d_skip[i], HEAD_DIM).reshape(1, D_B)
        nxt = _ag_ici([wt_loc[i + 1], wo_loc[i + 1]], slots) if i + 1 < DEPTH else None
        ycat, yssd, states, *bufs = _ssd_fwd(xbc, proj, dt_bias[i], a_log[i], d_skip_x, row(ssm_norm_g), e_mat, ycat, lp,
                                             "ssd_fwd", comm=nxt)
        if bufs:
            m, *bufs = _mm(ycat, w_o, "nn", F32, t // 3, D_MODEL, 2 * D_MODEL, "mm_out", comm=_ag_d2d(bufs, slots))
        else:
            m = _mm(ycat, w_o, "nn", F32, t // 3, D_MODEL, 2 * D_MODEL, "mm_out")
        saved.append((h, hn, proj, ycat, yssd, states, xbc, m, d_skip_x, w_t, w_o, u1))
        if i + 1 < DEPTH:
            h, hn = _post_rms_fwd(h, m, row(post_g), pre_g[i + 1].reshape(1, -1), rt, "post_rms_fwd")
        else:
            h = _post_fwd(h, m, row(post_g), rt, "post_fwd")
        if bufs:
            w_t, w_o = gathered(bufs)

    dh, loss_blk = _loss_kernel(h, loss_target.reshape(nb * seq, d), lp, "loss")

    grads = {n: [None] * DEPTH for n, _, _ in SMALL_PARAMS if n != "meta"}
    gt_in, gr_out = [None] * DEPTH, [None] * DEPTH
    dmeta = None
    pending = None

    def rs_pair_sums(blocks, ras):
        return [_rs_pair_sum(blocks[0], ras[0], own_plain, "rs_pair_sum_in"),
                _rs_pair_sum(blocks[1], ras[1], own_mix, "rs_pair_sum_out")]

    def rs_finish(layer, hs, rbs):
        gt_in[layer] = _rs_final_sum(hs[0], rbs[0], chip_idx, "rs_final_sum_in")
        gr_out[layer] = _rs_final_sum(hs[1], rbs[1], chip_idx, "rs_final_sum_out")

    for i in reversed(range(DEPTH)):
        row = lambda a: a[i].reshape(1, -1)
        h_i, hn, proj, ycat, yssd, states, xbc, m, d_skip_x, w_t, w_o, u1 = saved[i]
        dm, grads["post_g"][i] = _post_bwd(dh, m, row(post_g), rt, "post_bwd")
        if pending is not None:
            dy, *ras = _mm(dm, w_o, "nt", BF16, t // 3, D_MODEL, D_MODEL, "mm_dy", cols_outer=True,
                           comm=_rs_d2d(list(pending), slots))
            hs = rs_pair_sums(pending, ras)
        else:
            dy = _mm(dm, w_o, "nt", BF16, t // 3, D_MODEL, D_MODEL, "mm_dy", cols_outer=True)
            hs = None
        dw_out = _mm(ycat, dm, "tn", BF16, D_MODEL, D_MODEL, t // 3, "mm_dwout")
        dxbc, dproj, grads["ssm_norm_g"][i], ddtb, dal, dds, *rbs = _ssd_bwd(
            dy, yssd, xbc, proj, states, dt_bias[i], a_log[i], d_skip_x, row(ssm_norm_g), e_mat, e_mat_t, lp, "ssd_bwd",
            comm=_rs_ici(hs) if hs is not None else None)
        if hs is not None:
            rs_finish(i + 1, hs, rbs)
        grads["dt_bias"][i] = ddtb[:, :N_HEADS]
        grads["a_log"][i] = dal[:, :N_HEADS]
        grads["d_skip"][i] = dds[:, :N_HEADS]
        dproj, grads["conv_a_w"][i] = _a_bwd(dy, proj, full["conv_a_w"][i], dproj, lp, "a_bwd")
        dproj, grads["ssm_conv_w"][i], grads["ssm_conv_b"][i] = _xbc_bwd(
            dxbc, proj, full["ssm_conv_w"][i], row(ssm_conv_b), dproj, lp, "xbc_bwd")
        dproj, grads["conf_conv_w"][i], grads["conf_conv_b"][i], grads["conf_ln_g"][i], grads["conf_ln_b"][i] = _c_bwd(
            dy, proj, u1, full["conf_conv_w"][i], row(conf_conv_b), row(conf_ln_g), row(conf_ln_b), dproj, lp, "c_bwd")
        dhn = _mm(dproj, w_t, "nn", F32, MM_TM, D_MODEL, N_INP, "mm_dhn")
        dw_in_t = _mm(dproj, hn, "tn", F32, MM_TN, D_MODEL, t, "mm_dwin")
        dh, grads["pre_g"][i], dmeta = _rms_bwd(dh, dhn, h_i, row(pre_g), lp, rt, "rms_bwd")
        pending = (_w_split_rows(dw_in_t, "w_split_rows"), dw_out.reshape(N_DEV, n_out_loc, D_MODEL))
    grad_x = dh.reshape(nb, lp, d)[:, PAD + N_META:]
    hs = rs_pair_sums(pending, _run_comm(_rs_d2d(list(pending), slots), "rs_d2d"))
    rs_finish(0, hs, _run_comm(_rs_ici(hs), "rs_ici"))

    g_w_in = jnp.swapaxes(jnp.stack(gt_in)[:, :n_in_loc, :], 1, 2)
    g_w_out = jnp.stack(gr_out)
    big = {"w_in": [g_w_in, *_adamw_rows(g_w_in, w_in, m_w_in, v_w_in, 512, "adamw_w_in")],
           "w_out": [g_w_out, *_adamw_rows(g_w_out, w_out, m_w_out, v_w_out, n_out_loc, "adamw_w_out")]}

    names = [n for n, _, _ in SMALL_PARAMS]
    partial = [loss_blk[0:1, 0:1], dmeta] + [jnp.concatenate(grads[n], axis=0) for n in names[1:]]
    part_pack = _pack(partial)
    (parts_g,) = _allgather([part_pack], "ag_small_grads")
    total = _unpack(_sum_devices(parts_g, "sum_small_grads"), [(1,)] + [small_shapes[n] for n in names])
    loss = total[0][0]
    g_small = {}
    for n, g in zip(names, total[1:]):
        ax = small_axis[n]
        if ax is not None:
            g = lax.dynamic_slice_in_dim(g, dev * (small_shapes[n][ax] // N_DEV), small_shapes[n][ax] // N_DEV, axis=ax)
        g_small[n] = g
    loc_shapes = [_local_shape(small_shapes[n], small_axis[n]) for n in names]
    d_pack, m_pack, v_pack = _adamw_small(_pack([g_small[n] for n in names]), _pack([weights[n] for n in names]),
                                          _pack([mom1[n] for n in names]), _pack([mom2[n] for n in names]), "adamw_small")
    d_small = dict(zip(names, _unpack(d_pack, loc_shapes)))
    m_small = dict(zip(names, _unpack(m_pack, loc_shapes)))
    v_small = dict(zip(names, _unpack(v_pack, loc_shapes)))

    order = ["meta", "pre_g", "post_g", "w_in", "w_out", "conv_a_w", "ssm_conv_w", "ssm_conv_b", "dt_bias", "a_log",
             "d_skip", "ssm_norm_g", "conf_conv_w", "conf_conv_b", "conf_ln_g", "conf_ln_b"]

    def pick(k, small):
        return [big[n][k] if n in big else small[n] for n in order]

    return (loss, grad_x, *pick(0, g_small), *pick(1, d_small), *pick(2, m_small), *pick(3, v_small))
```

```python
import functools

import jax
import jax.numpy as jnp
from jax import lax
from jax.experimental import pallas as pl
from jax.experimental.pallas import tpu as pltpu

F32 = jnp.float32
BF16 = jnp.bfloat16

D_MODEL = 1024
DEPTH = 4
SEQ = 2048
CHUNK = 64
N_META = 16
PAD = 48
LP = PAD + N_META + SEQ
D_A = 512
D_B = 1024
D_C = 512
N_HEADS = 16
HEAD_DIM = 64
N_STATE = 128
N_GROUPS = 2
GROUP_W = D_B // N_GROUPS
N_XBC = D_B + 2 * N_GROUPS * N_STATE
CONV_A_K = 3
SSM_K = 4
CONF_K = 31
NORM_EPS = 1e-6
LN_EPS = 1e-5
N_IN = 6160
N_INP = 6272
COL_BZ = 2048
COL_DT = 3072
COL_XBC = 3200
COL_C = 4736
COL_MAP = ((0, 3072, 0), (3072, 4608, COL_XBC), (4608, 4624, COL_DT), (4624, 6160, COL_C))
LANE = 128
SUB = 8

ADAM_LR = 0.001
ADAM_B1 = 0.9
ADAM_B2 = 0.999
ADAM_EPS = 1e-08
ADAM_WD = 0.01
ADAM_STEP = 10

TILE = 192
HALO_A = 16
HALO_C = 32
MM_TM = 384
MM_TN = 896
VMEM_LIMIT = 56 * 1024 * 1024

MESH = pl.DeviceIdType.MESH
N_DEV = 8


def _silu(x):
    return x * jax.nn.sigmoid(x)


def _dsilu(x):
    s = jax.nn.sigmoid(x)
    return s * (1.0 + x * (1.0 - s))


def _cparams(sem=None):
    return pltpu.CompilerParams(dimension_semantics=sem, vmem_limit_bytes=VMEM_LIMIT)


def _mm(a, b, mode, out_dtype, tm, tn, tk, name, cols_outer=False, comm=None):
    def ix(f):
        if cols_outer:
            return lambda j, i, q: f(i, j, q)
        return f

    if mode == "nn":
        (m, k), (_, n) = a.shape, b.shape
        a_spec = pl.BlockSpec((tm, tk), ix(lambda i, j, q: (i, q)))
        b_spec = pl.BlockSpec((tk, tn), ix(lambda i, j, q: (q, j)))
        dims = (((1,), (0,)), ((), ()))
    elif mode == "nt":
        (m, k), (n, _) = a.shape, b.shape
        a_spec = pl.BlockSpec((tm, tk), ix(lambda i, j, q: (i, q)))
        b_spec = pl.BlockSpec((tn, tk), ix(lambda i, j, q: (j, q)))
        dims = (((1,), (1,)), ((), ()))
    else:
        (k, m), (_, n) = a.shape, b.shape
        a_spec = pl.BlockSpec((tk, tm), ix(lambda i, j, q: (q, i)))
        b_spec = pl.BlockSpec((tk, tn), ix(lambda i, j, q: (q, j)))
        dims = (((0,), (0,)), ((), ()))
    assert m % tm == 0 and n % tn == 0 and k % tk == 0, (name, a.shape, b.shape)
    nk = k // tk
    grid = (n // tn, m // tm, nk) if cols_outer else (m // tm, n // tn, nk)

    def body(a_ref, b_ref, o_ref, acc_ref):
        part = lax.dot_general(a_ref[...].astype(BF16), b_ref[...].astype(BF16), dims, preferred_element_type=F32)
        if nk == 1:
            o_ref[...] = part.astype(o_ref.dtype)
        else:
            q = pl.program_id(2)

            @pl.when(q == 0)
            def _():
                acc_ref[...] = part

            @pl.when(q > 0)
            def _():
                acc_ref[...] += part

            @pl.when(q == nk - 1)
            def _():
                o_ref[...] = acc_ref[...].astype(o_ref.dtype)

    res = _grid_call(
        body, name, grid, [a_spec, b_spec], [pl.BlockSpec((tm, tn), ix(lambda i, j, q: (i, j)))],
        [jax.ShapeDtypeStruct((m, n), out_dtype)], [pltpu.VMEM((tm, tn) if nk > 1 else (SUB, LANE), F32)], (a, b), comm)
    return res[0] if comm is None else res


def _row_mask(i, tpe, rows):
    r = lax.broadcasted_iota(jnp.int32, (rows, 1), 0)
    return jnp.logical_or((i % tpe) != 0, r >= PAD)


def _rms_fwd(h, g, rt, name):
    t, d = h.shape

    def body(h_ref, g_ref, o_ref):
        x = h_ref[...]
        r = lax.rsqrt(jnp.mean(x * x, axis=-1, keepdims=True) + NORM_EPS)
        o_ref[...] = (x * r * g_ref[...]).astype(BF16)

    return pl.pallas_call(
        body, name=name, grid=(t // rt,),
        in_specs=[pl.BlockSpec((rt, d), lambda i: (i, 0)), pl.BlockSpec((1, d), lambda i: (0, 0))],
        out_specs=pl.BlockSpec((rt, d), lambda i: (i, 0)),
        out_shape=jax.ShapeDtypeStruct((t, d), BF16),
        compiler_params=_cparams(("parallel",)),
    )(h, g)


def _post_fwd(h, m, g, rt, name):
    t, d = h.shape

    def body(h_ref, m_ref, g_ref, o_ref):
        x = m_ref[...]
        r = lax.rsqrt(jnp.mean(x * x, axis=-1, keepdims=True) + NORM_EPS)
        o_ref[...] = h_ref[...] + x * r * g_ref[...]

    row = pl.BlockSpec((rt, d), lambda i: (i, 0))
    return pl.pallas_call(
        body, name=name, grid=(t // rt,),
        in_specs=[row, row, pl.BlockSpec((1, d), lambda i: (0, 0))], out_specs=row,
        out_shape=jax.ShapeDtypeStruct((t, d), F32),
        compiler_params=_cparams(("parallel",)),
    )(h, m, g)


def _post_rms_fwd(h, m, g_post, g_next, rt, name):
    t, d = h.shape

    def body(h_ref, m_ref, gp_ref, gn_ref, o_ref, n_ref):
        x = m_ref[...]
        r = lax.rsqrt(jnp.mean(x * x, axis=-1, keepdims=True) + NORM_EPS)
        y = h_ref[...] + x * r * gp_ref[...]
        o_ref[...] = y
        r2 = lax.rsqrt(jnp.mean(y * y, axis=-1, keepdims=True) + NORM_EPS)
        n_ref[...] = (y * r2 * gn_ref[...]).astype(BF16)

    row = pl.BlockSpec((rt, d), lambda i: (i, 0))
    vec = pl.BlockSpec((1, d), lambda i: (0, 0))
    return pl.pallas_call(
        body, name=name, grid=(t // rt,),
        in_specs=[row, row, vec, vec], out_specs=[row, row],
        out_shape=[jax.ShapeDtypeStruct((t, d), F32), jax.ShapeDtypeStruct((t, d), BF16)],
        compiler_params=_cparams(("parallel",)),
    )(h, m, g_post, g_next)


def _rms_bwd_math(x, g, dy):
    r = lax.rsqrt(jnp.mean(x * x, axis=-1, keepdims=True) + NORM_EPS)
    gdy = dy * g
    dx = r * gdy - x * (r * r * r) * jnp.mean(gdy * x, axis=-1, keepdims=True)
    return dx, dy * x * r


def _post_bwd(dh, m, g, rt, name):
    t, d = dh.shape

    def body(dh_ref, m_ref, g_ref, dm_ref, dg_ref):
        dm, dgt = _rms_bwd_math(m_ref[...], g_ref[...], dh_ref[...])
        dm_ref[...] = dm

        @pl.when(pl.program_id(0) == 0)
        def _():
            dg_ref[...] = jnp.zeros_like(dg_ref)

        dg_ref[...] += jnp.sum(dgt, axis=0, keepdims=True)

    row = pl.BlockSpec((rt, d), lambda i: (i, 0))
    vec = pl.BlockSpec((1, d), lambda i: (0, 0))
    return pl.pallas_call(
        body, name=name, grid=(t // rt,),
        in_specs=[row, row, vec], out_specs=[row, vec],
        out_shape=[jax.ShapeDtypeStruct((t, d), F32), jax.ShapeDtypeStruct((1, d), F32)],
        compiler_params=_cparams(("arbitrary",)),
    )(dh, m, g)


def _rms_bwd(dh_res, dhn, h, g, lp, rt, name):
    t, d = h.shape
    tpe = lp // rt

    def body(dr_ref, dn_ref, h_ref, g_ref, dh_ref, dg_ref, dmeta_ref):
        i = pl.program_id(0)
        dx, dgt = _rms_bwd_math(h_ref[...], g_ref[...], dn_ref[...])
        dh = dr_ref[...] + dx
        dh_ref[...] = dh

        @pl.when(i == 0)
        def _():
            dg_ref[...] = jnp.zeros_like(dg_ref)
            dmeta_ref[...] = jnp.zeros_like(dmeta_ref)

        dg_ref[...] += jnp.sum(dgt, axis=0, keepdims=True)

        @pl.when((i % tpe) == 0)
        def _():
            dmeta_ref[...] += dh[PAD:PAD + N_META, :]

    row = pl.BlockSpec((rt, d), lambda i: (i, 0))
    vec = pl.BlockSpec((1, d), lambda i: (0, 0))
    return pl.pallas_call(
        body, name=name, grid=(t // rt,),
        in_specs=[row, row, row, vec],
        out_specs=[row, vec, pl.BlockSpec((N_META, d), lambda i: (0, 0))],
        out_shape=[jax.ShapeDtypeStruct((t, d), F32), jax.ShapeDtypeStruct((1, d), F32),
                   jax.ShapeDtypeStruct((N_META, d), F32)],
        compiler_params=_cparams(("arbitrary",)),
    )(dh_res, dhn, h, g)


def _loss_kernel(h, target, lp, name):
    t, d = h.shape
    nb = t // lp
    seq = lp - PAD - N_META
    rows = next(r for r in (1024, 512, 256, 128, CHUNK) if seq % r == 0)
    per = seq // rows

    def body(h_ref, t_ref, zero_ref, dh_ref, loss_ref):
        @pl.when(jnp.logical_and(pl.program_id(0) == 0, pl.program_id(1) == 0))
        def _():
            loss_ref[...] = jnp.zeros_like(loss_ref)

        err = h_ref[...] - t_ref[...]
        dh_ref[...] = err * (1.0 / d)
        loss_ref[...] += (0.5 / d) * jnp.sum(err * err)

    win = pl.BlockSpec((pl.Element(rows), pl.Element(d)),
                       lambda b, j: (pl.multiple_of(b * lp + PAD + N_META + j * rows, CHUNK), 0))
    return pl.pallas_call(
        body, name=name, grid=(nb, per),
        in_specs=[win, pl.BlockSpec((rows, d), lambda b, j: (b * per + j, 0)), HBM_SPEC],
        out_specs=[win, pl.BlockSpec((SUB, LANE), lambda b, j: (0, 0))],
        out_shape=[jax.ShapeDtypeStruct((t, d), F32), jax.ShapeDtypeStruct((SUB, LANE), F32)],
        input_output_aliases={2: 0},
        compiler_params=_cparams(("arbitrary", "arbitrary")),
    )(h, target, jnp.zeros((t, d), F32))


def _window_spec(rows, width, col):
    return pl.BlockSpec((pl.Element(rows), pl.Element(width)), lambda i: (i * rows, col))


def _halo_specs(t, width, col, halo):
    cur = _window_spec(TILE, width, col)
    prev = pl.BlockSpec((pl.Element(halo), pl.Element(width)),
                        lambda i: (pl.multiple_of(jnp.maximum(i * TILE - halo, 0), halo), col))
    nxt = pl.BlockSpec((pl.Element(halo), pl.Element(width)),
                       lambda i: (pl.multiple_of(jnp.minimum((i + 1) * TILE, t - halo), halo), col))
    return cur, prev, nxt


def _f32(ref, lo, hi):
    return ref[:, lo:hi].astype(F32)


def _rows_from(x, start, rows):
    s = start % SUB
    if s == 0:
        return x[start:start + rows]
    return pltpu.roll(x, x.shape[0] - s, axis=0)[start - s:start - s + rows]


def _conv_a(ve, w):
    rows = ve.shape[0] - HALO_A
    return (w[0:1] * _rows_from(ve, HALO_A - 2, rows) + w[1:2] * _rows_from(ve, HALO_A - 1, rows)
            + w[2:3] * ve[HALO_A:HALO_A + rows])


def _a_fwd(proj, w, lp, name):
    t = proj.shape[0]
    tpe = lp // TILE
    cur, prev, _ = _halo_specs(t, 4 * D_A, 0, HALO_A)

    def body(p_ref, ph_ref, w_ref, y_ref):
        first = (pl.program_id(0) % tpe) == 0
        v = _f32(p_ref, D_A, 2 * D_A) * _f32(p_ref, 2 * D_A, 3 * D_A)
        vh = jnp.where(first, 0.0, _f32(ph_ref, D_A, 2 * D_A) * _f32(ph_ref, 2 * D_A, 3 * D_A))
        cv = _conv_a(jnp.concatenate([vh, v], axis=0), w_ref[...])
        y_ref[...] = (_f32(p_ref, 0, D_A) * cv * _silu(_f32(p_ref, 3 * D_A, 4 * D_A))).astype(BF16)

    return pl.pallas_call(
        body, name=name, grid=(t // TILE,),
        in_specs=[cur, prev, pl.BlockSpec((CONV_A_K, D_A), lambda i: (0, 0))],
        out_specs=_window_spec(TILE, D_A, D_B),
        out_shape=jax.ShapeDtypeStruct((t, D_B + D_A + D_C), BF16),
        compiler_params=_cparams(("parallel",)),
    )(proj, proj, w)


def _a_bwd(dy, proj, w, dproj, lp, name):
    t = proj.shape[0]
    tpe = lp // TILE
    cur, prev, nxt = _halo_specs(t, 4 * D_A, 0, HALO_A)
    dcur, _, dnxt = _halo_specs(t, D_A, D_B, HALO_A)

    def body(dy_ref, dyn_ref, p_ref, ph_ref, pn_ref, w_ref, dproj_ref, dp_ref, dw_ref):
        i = pl.program_id(0)
        first = (i % tpe) == 0
        last = (i % tpe) == tpe - 1
        w = w_ref[...]
        ab, ac, ax, az = (_f32(p_ref, k * D_A, (k + 1) * D_A) for k in range(4))
        v = ac * ax
        vh = jnp.where(first, 0.0, _f32(ph_ref, D_A, 2 * D_A) * _f32(ph_ref, 2 * D_A, 3 * D_A))
        ve = jnp.concatenate([vh, v], axis=0)
        taps = [_rows_from(ve, HALO_A - 2 + k, TILE) for k in range(CONV_A_K)]
        cv = w[0:1] * taps[0] + w[1:2] * taps[1] + w[2:3] * taps[2]
        s = _silu(az)
        dy_ = dy_ref[...].astype(F32)
        dcv = dy_ * ab * s
        dcvn = jnp.where(last, 0.0, dyn_ref[...].astype(F32) * _f32(pn_ref, 0, D_A) * _silu(_f32(pn_ref, 3 * D_A, 4 * D_A)))
        dce = jnp.concatenate([dcv, dcvn], axis=0)
        dv = w[2:3] * dce[0:TILE] + w[1:2] * _rows_from(dce, 1, TILE) + w[0:1] * _rows_from(dce, 2, TILE)
        dp = jnp.concatenate([dy_ * cv * s, dv * ax, dv * ac, dy_ * ab * cv * _dsilu(az)], axis=1)
        dp_ref[...] = jnp.where(_row_mask(i, tpe, TILE), dp, 0.0).astype(BF16)
        dw = jnp.concatenate(
            [jnp.sum(dcv * taps[k], axis=0, keepdims=True) for k in range(CONV_A_K)], axis=0)

        @pl.when(i == 0)
        def _():
            dw_ref[...] = jnp.zeros_like(dw_ref)

        dw_ref[...] += dw

    wspec = pl.BlockSpec((CONV_A_K, D_A), lambda i: (0, 0))
    return pl.pallas_call(
        body, name=name, grid=(t // TILE,),
        in_specs=[dcur, dnxt, cur, prev, nxt, wspec, HBM_SPEC],
        out_specs=[_window_spec(TILE, 4 * D_A, 0), wspec],
        out_shape=[jax.ShapeDtypeStruct(dproj.shape, dproj.dtype), jax.ShapeDtypeStruct((CONV_A_K, D_A), F32)],
        input_output_aliases={6: 0},
        compiler_params=_cparams(("arbitrary",)),
    )(dy, dy, proj, proj, proj, w, dproj)


def _conv_ssm(xe, w, rows, off):
    acc = w[0:1] * _rows_from(xe, off - 3, rows)
    for k in range(1, SSM_K):
        acc = acc + w[k:k + 1] * _rows_from(xe, off - 3 + k, rows)
    return acc


def _xbc_fwd(proj, w, b, lp, name):
    t = proj.shape[0]
    tpe = lp // TILE
    cur, prev, _ = _halo_specs(t, N_XBC, COL_XBC, HALO_A)

    def body(x_ref, xh_ref, w_ref, b_ref, o_ref):
        first = (pl.program_id(0) % tpe) == 0
        xh = jnp.where(first, 0.0, xh_ref[...].astype(F32))
        xe = jnp.concatenate([xh, x_ref[...].astype(F32)], axis=0)
        o_ref[...] = _silu(_conv_ssm(xe, w_ref[...], TILE, HALO_A) + b_ref[...]).astype(BF16)

    return pl.pallas_call(
        body, name=name, grid=(t // TILE,),
        in_specs=[cur, prev, pl.BlockSpec((SSM_K, N_XBC), lambda i: (0, 0)), pl.BlockSpec((1, N_XBC), lambda i: (0, 0))],
        out_specs=pl.BlockSpec((TILE, N_XBC), lambda i: (i, 0)),
        out_shape=jax.ShapeDtypeStruct((t, N_XBC), BF16),
        compiler_params=_cparams(("parallel",)),
    )(proj, proj, w, b)


def _xbc_bwd(dxbc, proj, w, b, dproj, lp, name):
    t = proj.shape[0]
    tpe = lp // TILE
    cur, prev, nxt = _halo_specs(t, N_XBC, COL_XBC, HALO_A)
    dcur, _, dnxt = _halo_specs(t, N_XBC, 0, HALO_A)

    def body(d_ref, dn_ref, x_ref, xh_ref, xn_ref, w_ref, b_ref, dproj_ref, dx_ref, dw_ref, db_ref):
        i = pl.program_id(0)
        first = (i % tpe) == 0
        last = (i % tpe) == tpe - 1
        w = w_ref[...]
        xh = jnp.where(first, 0.0, xh_ref[...].astype(F32))
        xe = jnp.concatenate([xh, x_ref[...].astype(F32), xn_ref[...].astype(F32)], axis=0)
        taps = [_rows_from(xe, HALO_A - 3 + k, TILE + HALO_A) for k in range(SSM_K)]
        pre = b_ref[...] + w[0:1] * taps[0]
        for k in range(1, SSM_K):
            pre = pre + w[k:k + 1] * taps[k]
        de = jnp.concatenate([d_ref[...].astype(F32), jnp.where(last, 0.0, dn_ref[...].astype(F32))], axis=0)
        dpre = de * _dsilu(pre)
        dx = w[3:4] * dpre[0:TILE]
        for k in range(SSM_K - 1):
            dx = dx + w[k:k + 1] * _rows_from(dpre, 3 - k, TILE)
        dx_ref[...] = jnp.where(_row_mask(i, tpe, TILE), dx, 0.0).astype(BF16)
        dpc = dpre[0:TILE]
        dw = jnp.concatenate(
            [jnp.sum(dpc * taps[k][0:TILE], axis=0, keepdims=True) for k in range(SSM_K)], axis=0)

        @pl.when(i == 0)
        def _():
            dw_ref[...] = jnp.zeros_like(dw_ref)
            db_ref[...] = jnp.zeros_like(db_ref)

        dw_ref[...] += dw
        db_ref[...] += jnp.sum(dpc, axis=0, keepdims=True)

    wspec = pl.BlockSpec((SSM_K, N_XBC), lambda i: (0, 0))
    bspec = pl.BlockSpec((1, N_XBC), lambda i: (0, 0))
    return pl.pallas_call(
        body, name=name, grid=(t // TILE,),
        in_specs=[dcur, dnxt, cur, prev, nxt, wspec, bspec, HBM_SPEC],
        out_specs=[_window_spec(TILE, N_XBC, COL_XBC), wspec, bspec],
        out_shape=[jax.ShapeDtypeStruct(dproj.shape, dproj.dtype), jax.ShapeDtypeStruct((SSM_K, N_XBC), F32),
                   jax.ShapeDtypeStruct((1, N_XBC), F32)],
        input_output_aliases={7: 0},
        compiler_params=_cparams(("arbitrary",)),
    )(dxbc, dxbc, proj, proj, proj, w, b, dproj)


SUBROWS = 32


def _fill_shifted(scr, x):
    scr[0] = x
    for s in range(1, SUB):
        scr[s] = pltpu.roll(x, x.shape[0] - s, axis=0)


def _window(scr, start, rows):
    s = start % SUB
    return scr[s, start - s:start - s + rows, :]


def _conv_conf(scr, w, rows, off, base):
    acc = w[0:1] * _window(scr, base + off - (CONF_K - 1), rows)
    for k in range(1, CONF_K):
        acc = acc + w[k:k + 1] * _window(scr, base + off - (CONF_K - 1) + k, rows)
    return acc


def _ln_fwd(u1, g, b):
    mu = jnp.mean(u1, axis=-1, keepdims=True)
    xc = u1 - mu
    rstd = lax.rsqrt(jnp.mean(xc * xc, axis=-1, keepdims=True) + LN_EPS)
    n = xc * rstd
    return n, rstd, n * g + b


def _c_fwd(proj, w, cb, g, b, ybuf, lp, name):
    t = proj.shape[0]
    tpe = lp // TILE
    cur, prev, _ = _halo_specs(t, 3 * D_C, COL_C, HALO_C)

    def body(p_ref, ph_ref, w_ref, cb_ref, g_ref, b_ref, ybuf_ref, y_ref, u1_ref, u0_scr):
        first = (pl.program_id(0) % tpe) == 0
        u0h = jnp.where(first, 0.0, _f32(ph_ref, 0, D_C) * jax.nn.sigmoid(_f32(ph_ref, D_C, 2 * D_C)))
        _fill_shifted(u0_scr, jnp.concatenate([u0h, _f32(p_ref, 0, D_C) * jax.nn.sigmoid(_f32(p_ref, D_C, 2 * D_C))], axis=0))
        w = w_ref[...]
        for r0 in range(0, TILE, SUBROWS):
            u1 = _conv_conf(u0_scr, w, SUBROWS, HALO_C, r0) + cb_ref[...]
            u1_ref[r0:r0 + SUBROWS, :] = u1.astype(BF16)
            _, _, u2 = _ln_fwd(u1, g_ref[...], b_ref[...])
            cz = p_ref[r0:r0 + SUBROWS, 2 * D_C:3 * D_C].astype(F32)
            y_ref[r0:r0 + SUBROWS, :] = (_silu(u2) * _silu(cz)).astype(BF16)

    vec = pl.BlockSpec((1, D_C), lambda i: (0, 0))
    return pl.pallas_call(
        body, name=name, grid=(t // TILE,),
        in_specs=[cur, prev, pl.BlockSpec((CONF_K, D_C), lambda i: (0, 0)), vec, vec, vec, HBM_SPEC],
        out_specs=[_window_spec(TILE, D_C, D_B + D_A), pl.BlockSpec((TILE, D_C), lambda i: (i, 0))],
        out_shape=[jax.ShapeDtypeStruct(ybuf.shape, ybuf.dtype), jax.ShapeDtypeStruct((t, D_C), BF16)],
        scratch_shapes=[pltpu.VMEM((SUB, HALO_C + TILE, D_C), F32)],
        input_output_aliases={6: 0},
        compiler_params=_cparams(("parallel",)),
    )(proj, proj, w, cb, g, b, ybuf)


def _c_bwd(dy, proj, u1, w, cb, g, b, dproj, lp, name):
    t = proj.shape[0]
    tpe = lp // TILE
    cur, prev, nxt = _halo_specs(t, 3 * D_C, COL_C, HALO_C)
    dcur, _, dnxt = _halo_specs(t, D_C, D_B + D_A, HALO_C)
    ucur, _, unxt = _halo_specs(t, D_C, 0, HALO_C)
    ext = TILE + HALO_C

    def body(dy_ref, dyn_ref, p_ref, ph_ref, pn_ref, u1_ref, u1n_ref, w_ref, cb_ref, g_ref, b_ref, dproj_ref,
             dp_ref, dw_ref, dcb_ref, dg_ref, db_ref, u0_scr, du1_scr, wacc_scr):
        i = pl.program_id(0)
        first = (i % tpe) == 0
        last = (i % tpe) == tpe - 1
        w = w_ref[...]

        @pl.when(i == 0)
        def _():
            dw_ref[...] = jnp.zeros_like(dw_ref)
            dcb_ref[...] = jnp.zeros_like(dcb_ref)
            dg_ref[...] = jnp.zeros_like(dg_ref)
            db_ref[...] = jnp.zeros_like(db_ref)

        u0h = jnp.where(first, 0.0, _f32(ph_ref, 0, D_C) * jax.nn.sigmoid(_f32(ph_ref, D_C, 2 * D_C)))
        _fill_shifted(u0_scr, jnp.concatenate(
            [u0h, _f32(p_ref, 0, D_C) * jax.nn.sigmoid(_f32(p_ref, D_C, 2 * D_C))], axis=0))
        dcb = jnp.zeros((1, D_C), F32)
        dg = jnp.zeros((1, D_C), F32)
        db = jnp.zeros((1, D_C), F32)
        for r0 in range(0, ext, SUBROWS):
            in_tile = r0 < TILE
            src, dsrc, usrc, q0 = (p_ref, dy_ref, u1_ref, r0) if in_tile else (pn_ref, dyn_ref, u1n_ref, r0 - TILE)
            u1 = usrc[q0:q0 + SUBROWS, :].astype(F32)
            n, rstd, u2 = _ln_fwd(u1, g_ref[...], b_ref[...])
            cz = src[q0:q0 + SUBROWS, 2 * D_C:3 * D_C].astype(F32)
            dyc = dsrc[q0:q0 + SUBROWS, :].astype(F32)
            if not in_tile:
                dyc = jnp.where(last, 0.0, dyc)
            du2 = dyc * _silu(cz) * _dsilu(u2)
            dn = du2 * g_ref[...]
            du1 = rstd * (dn - jnp.mean(dn, axis=-1, keepdims=True) - n * jnp.mean(dn * n, axis=-1, keepdims=True))
            du1_scr[0, r0:r0 + SUBROWS, :] = du1
            if in_tile:
                dp_ref[r0:r0 + SUBROWS, 2 * D_C:3 * D_C] = (dyc * _silu(u2) * _dsilu(cz)).astype(BF16)
                dg = dg + jnp.sum(du2 * n, axis=0, keepdims=True)
                db = db + jnp.sum(du2, axis=0, keepdims=True)
                dcb = dcb + jnp.sum(du1, axis=0, keepdims=True)
        dcb_ref[...] += dcb
        dg_ref[...] += dg
        db_ref[...] += db
        mask = _row_mask(i, tpe, TILE)
        _fill_shifted(du1_scr, du1_scr[0])
        for r0 in range(0, TILE, SUBROWS):
            acc = w[0:1] * _window(du1_scr, r0 + CONF_K - 1, SUBROWS)
            for k in range(1, CONF_K):
                acc = acc + w[k:k + 1] * _window(du1_scr, r0 + CONF_K - 1 - k, SUBROWS)
            ca = p_ref[r0:r0 + SUBROWS, 0:D_C].astype(F32)
            sg = jax.nn.sigmoid(p_ref[r0:r0 + SUBROWS, D_C:2 * D_C].astype(F32))
            m = mask[r0:r0 + SUBROWS]
            dp_ref[r0:r0 + SUBROWS, 0:D_C] = jnp.where(m, acc * sg, 0.0).astype(BF16)
            dp_ref[r0:r0 + SUBROWS, D_C:2 * D_C] = jnp.where(m, acc * ca * sg * (1.0 - sg), 0.0).astype(BF16)
        for k in range(CONF_K):
            part = jnp.zeros((SUB, D_C), F32)
            for r0 in range(0, TILE, SUBROWS):
                prod = du1_scr[0, r0:r0 + SUBROWS, :] * _window(u0_scr, HALO_C + r0 - (CONF_K - 1) + k, SUBROWS)
                for q in range(0, SUBROWS, SUB):
                    part = part + prod[q:q + SUB]
            wacc_scr[k:k + 1, :] = jnp.sum(part, axis=0, keepdims=True)
        dw_ref[...] += wacc_scr[0:CONF_K, :]

    vec = pl.BlockSpec((1, D_C), lambda i: (0, 0))
    wspec = pl.BlockSpec((CONF_K, D_C), lambda i: (0, 0))
    return pl.pallas_call(
        body, name=name, grid=(t // TILE,),
        in_specs=[dcur, dnxt, cur, prev, nxt, ucur, unxt, wspec, vec, vec, vec, HBM_SPEC],
        out_specs=[_window_spec(TILE, 3 * D_C, COL_C), wspec, vec, vec, vec],
        out_shape=[jax.ShapeDtypeStruct(dproj.shape, dproj.dtype), jax.ShapeDtypeStruct((CONF_K, D_C), F32),
                   jax.ShapeDtypeStruct((1, D_C), F32), jax.ShapeDtypeStruct((1, D_C), F32),
                   jax.ShapeDtypeStruct((1, D_C), F32)],
        scratch_shapes=[pltpu.VMEM((SUB, ext, D_C), F32), pltpu.VMEM((SUB, ext, D_C), F32),
                        pltpu.VMEM((HALO_C, D_C), F32)],
        input_output_aliases={11: 0},
        compiler_params=_cparams(("arbitrary",)),
    )(dy, dy, proj, proj, proj, u1, u1, w, cb, g, b, dproj)


def _split_dot(x, m_bf16, terms):
    acc = None
    rem = x
    for _ in range(terms):
        hi = rem.astype(BF16)
        part = jnp.dot(hi, m_bf16, preferred_element_type=F32)
        acc = part if acc is None else acc + part
        rem = rem - hi.astype(F32)
    return acc


def _split_pieces(x, terms):
    out, rem = [], x
    for _ in range(terms):
        hi = rem.astype(BF16)
        out.append(hi)
        rem = rem - hi.astype(F32)
    return out


def _split_dot_many(xs, terms, m_bf16):
    pieces = [p for x, n in zip(xs, terms) for p in _split_pieces(x, n)]
    prod = jnp.dot(jnp.concatenate(pieces, axis=0), m_bf16, preferred_element_type=F32)
    out, off = [], 0
    for x, n in zip(xs, terms):
        rows = x.shape[0]
        acc = prod[off:off + rows]
        for q in range(1, n):
            acc = acc + prod[off + q * rows:off + (q + 1) * rows]
        out.append(acc)
        off += n * rows
    return out


def _split_dot_left(m_bf16, x, terms):
    cols = x.shape[1]
    prod = jnp.dot(m_bf16, jnp.concatenate(_split_pieces(x, terms), axis=1), preferred_element_type=F32)
    acc = prod[:, 0:cols]
    for q in range(1, terms):
        acc = acc + prod[:, q * cols:(q + 1) * cols]
    return acc


def _tri(rows_ge_cols):
    r = lax.broadcasted_iota(jnp.int32, (CHUNK, CHUNK), 0)
    c = lax.broadcasted_iota(jnp.int32, (CHUNK, CHUNK), 1)
    return (r >= c) if rows_ge_cols else (r <= c)


def _softplus(x):
    return jnp.maximum(x, 0.0) + jnp.log(1.0 + jnp.exp(-jnp.abs(x)))


def _ssd_common(dtraw, dtb, dtb_t, a_log, a_log_t, e_mat, valid_col, valid_row, x_terms=2):
    a = -jnp.exp(a_log)
    lane = lax.broadcasted_iota(jnp.int32, (1, LANE), 1)
    a = jnp.where(lane < N_HEADS, a, 0.0)
    a_t = -jnp.exp(a_log_t)
    dt = jnp.where(valid_col, _softplus(dtraw + dtb), 0.0)
    dt = jnp.where(lane < N_HEADS, dt, 0.0)
    dt_t = jnp.where(valid_row, _softplus(dtraw.T[0:N_HEADS, :] + dtb_t), 0.0)
    ltri = _tri(True).astype(BF16)
    utri = _tri(False).astype(BF16)
    big_a = _split_dot_left(ltri, dt * a, 3)
    big_a_t = _split_dot_many([dt_t * a_t], [3], utri)[0]
    e_a = jnp.exp(big_a)
    d_s = jnp.exp(big_a[CHUNK - 1:CHUNK, :] - big_a)
    if x_terms == 1:
        dt_x, e_a_x, d_s_x = _split_dot_many([dt, e_a, d_s], [1, 2, 1], e_mat)
    else:
        dt_x, e_a_x, d_s_x = (_split_dot(q, e_mat, x_terms) for q in (dt, e_a, d_s))
    cd_x = e_a_x[CHUNK - 1:CHUNK, :]
    return a, dt, big_a, big_a_t, e_a, d_s, dt_x, e_a_x, d_s_x, cd_x


def _decay(big_a, big_a_t, h, transposed):
    col = big_a[:, h:h + 1]
    row = big_a_t[h:h + 1, :]
    if not transposed:
        seg = col - row
        return jnp.where(_tri(True), jnp.exp(jnp.minimum(seg, 0.0)), 0.0)
    seg = row - col
    return jnp.where(_tri(False), jnp.exp(jnp.minimum(seg, 0.0)), 0.0)


NT_DIMS = (((1,), (1,)), ((), ()))
TN_DIMS = (((0,), (0,)), ((), ()))
HBM_SPEC = pl.BlockSpec(memory_space=pl.ANY)


class _Comm:
    def __init__(self, inputs, out_shapes, sem_shapes, copies, aliases=None):
        self.inputs, self.out_shapes, self.sem_shapes, self.copies = inputs, out_shapes, sem_shapes, copies
        self.aliases = aliases or {}

    def start(self, ins, outs, sems):
        local, sends, _ = self.copies(ins, outs, sems, False)
        for cp in local + sends:
            cp.start()

    def wait(self, ins, outs, sems):
        local, sends, recvs = self.copies(ins, outs, sems, True)
        for cp in recvs:
            cp.wait_recv()
        for cp in sends:
            cp.wait_send()
        for cp in local:
            cp.wait()


def _grid_call(body, name, grid, in_specs, out_specs, out_shape, scratch_shapes, operands, comm=None, aliases=None):
    aliases = dict(aliases or {})
    if comm is None:
        return pl.pallas_call(
            body, name=name, grid=grid, in_specs=in_specs, out_specs=out_specs, out_shape=out_shape,
            scratch_shapes=scratch_shapes, input_output_aliases=aliases,
            compiler_params=_cparams(("arbitrary",) * len(grid)))(*operands)
    n_in, n_out, n_scr = len(in_specs), len(out_specs), len(scratch_shapes)
    nci, nco = len(comm.inputs), len(comm.out_shapes)

    def wrapped(*refs):
        ins, cins = refs[:n_in], refs[n_in:n_in + nci]
        o0 = n_in + nci
        outs, couts = refs[o0:o0 + n_out], refs[o0 + n_out:o0 + n_out + nco]
        s0 = o0 + n_out + nco
        scr, csems = refs[s0:s0 + n_scr], refs[s0 + n_scr:]
        first = pl.program_id(0) == 0
        last = pl.program_id(0) == grid[0] - 1
        for k in range(1, len(grid)):
            first = jnp.logical_and(first, pl.program_id(k) == 0)
            last = jnp.logical_and(last, pl.program_id(k) == grid[k] - 1)

        @pl.when(first)
        def _():
            comm.start(cins, couts, csems)

        body(*ins, *outs, *scr)

        @pl.when(last)
        def _():
            comm.wait(cins, couts, csems)

    res = pl.pallas_call(
        wrapped, name=name, grid=grid,
        in_specs=list(in_specs) + [HBM_SPEC] * nci, out_specs=list(out_specs) + [HBM_SPEC] * nco,
        out_shape=list(out_shape) + list(comm.out_shapes),
        scratch_shapes=list(scratch_shapes) + list(comm.sem_shapes),
        input_output_aliases={**aliases, **{n_in + k: n_out + v for k, v in comm.aliases.items()}},
        compiler_params=_cparams(("arbitrary",) * len(grid)))(*operands, *comm.inputs)
    return res


def _ssd_fwd(xbc, proj, dtb, a_log, d_skip_x, norm_g, e_mat, ybuf, lp, name, comm=None):
    t = xbc.shape[0]
    cpe = lp // CHUNK
    nb = t // lp
    dtb_p = jnp.pad(dtb.reshape(1, N_HEADS), ((0, 0), (0, LANE - N_HEADS)))
    alog_p = jnp.pad(a_log.reshape(1, N_HEADS), ((0, 0), (0, LANE - N_HEADS)))
    dtb_t = dtb.reshape(N_HEADS, 1)
    alog_t = a_log.reshape(N_HEADS, 1)

    def body(xbc_ref, bz_ref, dt_ref, dtb_ref, dtbt_ref, al_ref, alt_ref, dx_ref, g_ref, e_ref, ybuf_ref,
             yb_ref, ys_ref, st_ref, s_scr):
        c = pl.program_id(0)

        @pl.when(c == 0)
        def _():
            s_scr[...] = jnp.zeros_like(s_scr)

        rows = lax.broadcasted_iota(jnp.int32, (CHUNK, 1), 0)
        cols = lax.broadcasted_iota(jnp.int32, (1, CHUNK), 1)
        valid_col = jnp.logical_or(c > 0, rows >= PAD)
        valid_row = jnp.logical_or(c > 0, cols >= PAD)
        e_mat = e_ref[...]
        for b in range(nb):
            _, _, big_a, big_a_t, _, _, dt_x, e_a_x, d_s_x, cd_x = _ssd_common(
                dt_ref[b].astype(F32), dtb_ref[...], dtbt_ref[...], al_ref[...], alt_ref[...], e_mat, valid_col, valid_row)
            xs = xbc_ref[b, :, 0:D_B].astype(F32)
            bs = xbc_ref[b, :, D_B:D_B + N_GROUPS * N_STATE]
            cs = xbc_ref[b, :, D_B + N_GROUPS * N_STATE:N_XBC]
            xdt = xs * dt_x
            xdt_b = xdt.astype(BF16)
            st_prev = s_scr[b]
            st_ref[b] = st_prev.astype(BF16)
            st_b = st_prev.astype(BF16)
            u_b = (xdt * d_s_x).astype(BF16)
            y_parts = []
            for g in range(N_GROUPS):
                gs = slice(g * N_STATE, (g + 1) * N_STATE)
                gw = slice(g * GROUP_W, (g + 1) * GROUP_W)
                cb = lax.dot_general(cs[:, gs], bs[:, gs], NT_DIMS, preferred_element_type=F32)
                y_off = jnp.dot(cs[:, gs], st_b[:, gw], preferred_element_type=F32)
                diag = []
                for e in range(N_HEADS // N_GROUPS):
                    h = g * (N_HEADS // N_GROUPS) + e
                    m = (cb * _decay(big_a, big_a_t, h, False)).astype(BF16)
                    diag.append(jnp.dot(m, xdt_b[:, h * HEAD_DIM:(h + 1) * HEAD_DIM], preferred_element_type=F32))
                y_parts.append(jnp.concatenate(diag, axis=1) + y_off * e_a_x[:, gw])
                new_st = lax.dot_general(bs[:, gs], u_b[:, gw], TN_DIMS, preferred_element_type=F32)
                s_scr[b, :, gw] = st_prev[:, gw] * cd_x[:, gw] + new_st
            y = jnp.concatenate(y_parts, axis=1) + xs * dx_ref[...]
            ys_ref[b] = y
            z = y * _silu(bz_ref[b].astype(F32))
            r = lax.rsqrt(jnp.mean(z * z, axis=-1, keepdims=True) + NORM_EPS)
            yb_ref[b] = (z * r * g_ref[...]).astype(BF16)

    def row(width, col):
        return pl.BlockSpec((nb, CHUNK, width), lambda c: (0, c, col))

    def const(shape):
        return pl.BlockSpec(shape, lambda c: (0,) * len(shape))

    proj3 = proj.reshape(nb, lp, N_INP)
    ybuf, yssd, states, *rest = _grid_call(
        body, name, (cpe,),
        [row(N_XBC, 0), row(D_B, COL_BZ // D_B), row(LANE, COL_DT // LANE),
         const((1, LANE)), const((N_HEADS, 1)), const((1, LANE)), const((N_HEADS, 1)),
         const((1, D_B)), const((1, D_B)), const((LANE, D_B)), HBM_SPEC],
        [row(D_B, 0), row(D_B, 0), pl.BlockSpec((nb, None, N_STATE, D_B), lambda c: (0, c, 0, 0))],
        [jax.ShapeDtypeStruct((nb, lp, ybuf.shape[1]), ybuf.dtype), jax.ShapeDtypeStruct((nb, lp, D_B), F32),
         jax.ShapeDtypeStruct((nb, cpe, N_STATE, D_B), BF16)],
        [pltpu.VMEM((nb, N_STATE, D_B), F32)],
        (xbc.reshape(nb, lp, N_XBC), proj3, proj3, dtb_p, dtb_t, alog_p, alog_t, d_skip_x, norm_g, e_mat,
         ybuf.reshape(nb, lp, ybuf.shape[1])), comm, aliases={10: 0})
    return (ybuf.reshape(t, -1), yssd.reshape(t, D_B), states, *rest)


def _ssd_bwd(dy, y_ssd, xbc, proj, states, dtb, a_log, d_skip_x, norm_g, e_mat, e_mat_t, lp, name, comm=None):
    t = xbc.shape[0]
    cpe = lp // CHUNK
    nb = t // lp
    hpg = N_HEADS // N_GROUPS
    dtb_p = jnp.pad(dtb.reshape(1, N_HEADS), ((0, 0), (0, LANE - N_HEADS)))
    alog_p = jnp.pad(a_log.reshape(1, N_HEADS), ((0, 0), (0, LANE - N_HEADS)))
    dtb_t = dtb.reshape(N_HEADS, 1)
    alog_t = a_log.reshape(N_HEADS, 1)

    def body(dy_ref, ys_ref, xbc_ref, bz_ref, dt_ref, st_ref, dtb_ref, dtbt_ref, al_ref, alt_ref, dx_ref, g_ref,
             e_ref, et_ref, dxbc_ref, dpw_ref, dg_ref, ddtb_ref, dal_ref, dd_ref, ds_scr):
        @pl.when(pl.program_id(0) == 0)
        def _():
            ds_scr[...] = jnp.zeros_like(ds_scr)
            dg_ref[...] = jnp.zeros_like(dg_ref)
            ddtb_ref[...] = jnp.zeros_like(ddtb_ref)
            dal_ref[...] = jnp.zeros_like(dal_ref)
            dd_ref[...] = jnp.zeros_like(dd_ref)

        for b in range(nb):
            one_example(dy_ref.at[b], ys_ref.at[b], xbc_ref.at[b], bz_ref.at[b], dt_ref.at[b], st_ref.at[b], dtb_ref,
                        dtbt_ref, al_ref, alt_ref, dx_ref, g_ref, e_ref, et_ref, dxbc_ref.at[b], dpw_ref.at[b], dg_ref,
                        ddtb_ref, dal_ref, dd_ref, ds_scr.at[b])

    def one_example(dy_ref, ys_ref, xbc_ref, bz_ref, dt_ref, st_ref, dtb_ref, dtbt_ref, al_ref, alt_ref, dx_ref, g_ref,
                    e_ref, et_ref, dxbc_ref, dpw_ref, dg_ref, ddtb_ref, dal_ref, dd_ref, ds_scr):
        cc = cpe - 1 - pl.program_id(0)
        rows = lax.broadcasted_iota(jnp.int32, (CHUNK, 1), 0)
        cols = lax.broadcasted_iota(jnp.int32, (1, CHUNK), 1)
        valid_col = jnp.logical_or(cc > 0, rows >= PAD)
        valid_row = jnp.logical_or(cc > 0, cols >= PAD)
        e_mat = e_ref[...]
        e_mat_t = et_ref[...]
        dtraw = dt_ref[...].astype(F32)
        a, dt, big_a, big_a_t, e_a, d_s, dt_x, e_a_x, d_s_x, cd_x = _ssd_common(
            dtraw, dtb_ref[...], dtbt_ref[...], al_ref[...], alt_ref[...], e_mat, valid_col, valid_row, x_terms=1)
        xs = xbc_ref[:, 0:D_B].astype(F32)
        bs = xbc_ref[:, D_B:D_B + N_GROUPS * N_STATE]
        cs = xbc_ref[:, D_B + N_GROUPS * N_STATE:N_XBC]
        xdt = xs * dt_x
        xdt_b = xdt.astype(BF16)
        st_b = st_ref[...]
        dst = ds_scr[...]
        dst_b = dst.astype(BF16)

        ys = ys_ref[...]
        bz = bz_ref[...].astype(F32)
        sil = _silu(bz)
        z = ys * sil
        dz, dgt = _rms_bwd_math(z, g_ref[...], dy_ref[...].astype(F32))
        dg_ref[...] += jnp.sum(dgt, axis=0, keepdims=True)
        dpw_ref[:, 0:D_B] = (dz * ys * _dsilu(bz)).astype(BF16)
        dys = dz * sil

        dd_lane = jnp.sum(dys * xs, axis=0, keepdims=True)
        dxs = dys * dx_ref[...]
        w_x = dys * e_a_x
        w_b = w_x.astype(BF16)
        dys_b = dys.astype(BF16)
        u_b = (xdt * d_s_x).astype(BF16)
        dxdt_parts, dbs_parts, dcs_parts, off_parts, g1_parts = [], [], [], [], []
        da_diag = jnp.zeros((CHUNK, LANE), F32)
        lane = lax.broadcasted_iota(jnp.int32, (1, LANE), 1)
        for g in range(N_GROUPS):
            gs = slice(g * N_STATE, (g + 1) * N_STATE)
            gw = slice(g * GROUP_W, (g + 1) * GROUP_W)
            cs_g, bs_g = cs[:, gs], bs[:, gs]
            dcs = lax.dot_general(w_b[:, gw], st_b[:, gw], NT_DIMS, preferred_element_type=F32)
            y_off = jnp.dot(cs_g, st_b[:, gw], preferred_element_type=F32)
            off_parts.append(y_off)
            dst_new = lax.dot_general(cs_g, w_b[:, gw], TN_DIMS, preferred_element_type=F32)
            g1 = jnp.dot(bs_g, dst_b[:, gw], preferred_element_type=F32)
            g1_parts.append(g1)
            dbs = lax.dot_general(u_b[:, gw], dst_b[:, gw], NT_DIMS, preferred_element_type=F32)
            cb = lax.dot_general(cs_g, bs_g, NT_DIMS, preferred_element_type=F32)
            cbt = lax.dot_general(bs_g, cs_g, NT_DIMS, preferred_element_type=F32)
            dcb = jnp.zeros((CHUNK, CHUNK), F32)
            dcbt = jnp.zeros((CHUNK, CHUNK), F32)
            dxdt_h = []
            for e in range(hpg):
                h = g * hpg + e
                hs = slice(h * HEAD_DIM, (h + 1) * HEAD_DIM)
                dec = _decay(big_a, big_a_t, h, False)
                dect = _decay(big_a, big_a_t, h, True)
                m = cb * dec
                mt = cbt * dect
                dxdt_h.append(jnp.dot(mt.astype(BF16), dys_b[:, hs], preferred_element_type=F32))
                dm = lax.dot_general(dys_b[:, hs], xdt_b[:, hs], NT_DIMS, preferred_element_type=F32)
                dmt = lax.dot_general(xdt_b[:, hs], dys_b[:, hs], NT_DIMS, preferred_element_type=F32)
                dcb = dcb + dm * dec
                dcbt = dcbt + dmt * dect
                da_h = jnp.sum(dm * m - dmt * mt, axis=1, keepdims=True)
                da_diag = da_diag + jnp.where(lane == h, da_h, 0.0)
            dcs = dcs + jnp.dot(dcb.astype(BF16), bs_g, preferred_element_type=F32)
            dbs = dbs + jnp.dot(dcbt.astype(BF16), cs_g, preferred_element_type=F32)
            dxdt_parts.append(jnp.concatenate(dxdt_h, axis=1) + g1 * d_s_x[:, gw])
            dbs_parts.append(dbs)
            dcs_parts.append(dcs)
            ds_scr[:, gw] = dst[:, gw] * cd_x[:, gw] + dst_new
        dxdt = jnp.concatenate(dxdt_parts, axis=1)
        y_off = jnp.concatenate(off_parts, axis=1)
        g1 = jnp.concatenate(g1_parts, axis=1)
        dcd_lane = jnp.sum(dst * st_b.astype(F32), axis=0, keepdims=True)
        vecs = jnp.concatenate([jnp.broadcast_to(dcd_lane, (SUB, D_B)), jnp.broadcast_to(dd_lane, (SUB, D_B))], axis=0)
        dds, da_off, ddt_x, vec_sums = _split_dot_many(
            [g1 * xdt, w_x * y_off, dxdt * xs, vecs], [1, 1, 1, 1], e_mat_t)
        dcd = vec_sums[0:1]
        dd_ref[...] += vec_sums[SUB:SUB + 1]
        t_ds = dds * d_s
        d_a = da_diag + da_off - t_ds
        last_row = jnp.sum(t_ds, axis=0, keepdims=True) + dcd * e_a[CHUNK - 1:CHUNK, :]
        d_a = d_a + jnp.where(rows == CHUNK - 1, last_row, 0.0)
        dda = _split_dot_left(_tri(False).astype(BF16), d_a, 3)
        ddt = dda * a + ddt_x
        dal_ref[...] += jnp.sum(dda * dt * a, axis=0, keepdims=True)
        ddtraw = jnp.where(valid_col, ddt * jax.nn.sigmoid(dtraw + dtb_ref[...]), 0.0)
        ddtraw = jnp.where(lane < N_HEADS, ddtraw, 0.0)
        ddtb_ref[...] += jnp.sum(ddtraw, axis=0, keepdims=True)
        dpw_ref[:, D_B:D_B + LANE] = ddtraw.astype(BF16)
        dxs = dxs + dxdt * dt_x
        dxbc = jnp.concatenate([dxs] + dbs_parts + dcs_parts, axis=1)
        dxbc_ref[...] = jnp.where(valid_col, dxbc, 0.0).astype(BF16)

    def row(width, col):
        return pl.BlockSpec((nb, CHUNK, width), lambda c: (0, cpe - 1 - c, col))

    def const(shape):
        return pl.BlockSpec(shape, lambda c: (0,) * len(shape))

    proj3 = proj.reshape(nb, lp, N_INP)
    dxbc, dproj, *rest = _grid_call(
        body, name, (cpe,),
        [row(D_B, 0), row(D_B, 0), row(N_XBC, 0), row(D_B, COL_BZ // D_B), row(LANE, COL_DT // LANE),
         pl.BlockSpec((nb, None, N_STATE, D_B), lambda c: (0, cpe - 1 - c, 0, 0)),
         const((1, LANE)), const((N_HEADS, 1)), const((1, LANE)), const((N_HEADS, 1)),
         const((1, D_B)), const((1, D_B)), const((LANE, D_B)), const((D_B, LANE))],
        [row(N_XBC, 0),
         pl.BlockSpec((pl.Element(nb), pl.Element(CHUNK), pl.Element(D_B + LANE)),
                      lambda c: (0, (cpe - 1 - c) * CHUNK, COL_BZ)),
         const((1, D_B)), const((1, LANE)), const((1, LANE)), const((1, LANE))],
        [jax.ShapeDtypeStruct((nb, lp, N_XBC), BF16), jax.ShapeDtypeStruct((nb, lp, N_INP), BF16),
         jax.ShapeDtypeStruct((1, D_B), F32),
         jax.ShapeDtypeStruct((1, LANE), F32), jax.ShapeDtypeStruct((1, LANE), F32),
         jax.ShapeDtypeStruct((1, LANE), F32)],
        [pltpu.VMEM((nb, N_STATE, D_B), F32)],
        (dy.reshape(nb, lp, -1), y_ssd.reshape(nb, lp, D_B), xbc.reshape(nb, lp, N_XBC), proj3, proj3, states,
         dtb_p, dtb_t, alog_p, alog_t, d_skip_x, norm_g, e_mat, e_mat_t), comm)
    return (dxbc.reshape(t, N_XBC), dproj.reshape(t, N_INP), *rest)


HBM_SPEC = pl.BlockSpec(memory_space=pl.ANY)


def _mesh_pos():
    return lax.axis_index("x"), lax.axis_index("y"), lax.axis_index("c")


def _allgather(arrays, name):
    n = len(arrays)

    def body(*refs):
        xs, outs = refs[:n], refs[n:2 * n]
        send_sems, recv_sems, local_sems = refs[2 * n:]
        x, y, c = _mesh_pos()
        me, sibling = (x, y, c), (x, y, 1 - c)
        chips = [(1 - x, y), (x, 1 - y), (1 - x, 1 - y)]

        def slot(px, py, pc):
            return 4 * px + 2 * py + pc

        def copy(a, k, block, to, src=None):
            dst = outs[a].at[slot(*block)]
            return pltpu.make_async_remote_copy(
                src_ref=dst if src is None else src, dst_ref=dst, send_sem=send_sems.at[a, k], recv_sem=recv_sems.at[a, k],
                device_id=to, device_id_type=MESH)

        mine = [pltpu.make_async_copy(xs[a], outs[a].at[slot(*me)], local_sems.at[a]) for a in range(n)]
        for cp in mine:
            cp.start()
        first = []
        for a in range(n):
            first.append(copy(a, 0, me, sibling, src=xs[a]))
            first += [copy(a, 1 + j, me, (*chip, c), src=xs[a]) for j, chip in enumerate(chips)]
        for cp in first:
            cp.start()
        passed = []
        for j, chip in enumerate(chips):
            for a in range(n):
                copy(a, 1 + j, (*chip, c), me).wait_recv()
                cp = copy(a, 4 + j, (*chip, c), sibling)
                cp.start()
                passed.append(cp)
        for a in range(n):
            copy(a, 0, sibling, me).wait_recv()
        for j, chip in enumerate(chips):
            for a in range(n):
                copy(a, 4 + j, (*chip, 1 - c), me).wait_recv()
        for cp in first + passed:
            cp.wait_send()
        for cp in mine:
            cp.wait()

    return pl.pallas_call(
        body, name=name,
        in_specs=[HBM_SPEC] * n, out_specs=[HBM_SPEC] * n,
        out_shape=[jax.ShapeDtypeStruct((N_DEV,) + a.shape, a.dtype) for a in arrays],
        scratch_shapes=[pltpu.SemaphoreType.DMA((n, 7)), pltpu.SemaphoreType.DMA((n, 7)), pltpu.SemaphoreType.DMA((n,))],
    )(*arrays)


def _remote(src, dst, send_sem, recv_sem, to):
    return pltpu.make_async_remote_copy(src_ref=src, dst_ref=dst, send_sem=send_sem, recv_sem=recv_sem,
                                        device_id=to, device_id_type=MESH)


def _slot_plain(d):
    return d


def _slot_mix_rows(d):
    return jnp.where(d < 2, d + 4, jnp.where(d < 6, d - 2, d))


def _ag_ici(pieces, slots):
    n = len(pieces)

    def copies(ins, outs, sems, with_recvs):
        send_sems, recv_sems, local_sems = sems
        x, y, c = _mesh_pos()
        local, sends, recvs = [], [], []
        for a in range(n):
            mine = outs[a].at[slots[a](4 * x + 2 * y + c)]
            local.append(pltpu.make_async_copy(ins[a], mine, local_sems.at[a]))
            for k, (px, py) in enumerate([(1 - x, y), (x, 1 - y), (1 - x, 1 - y)]):
                sends.append(_remote(ins[a], mine, send_sems.at[a, k], recv_sems.at[a, k], (px, py, c)))
                if with_recvs:
                    theirs = outs[a].at[slots[a](4 * px + 2 * py + c)]
                    recvs.append(_remote(ins[a], theirs, send_sems.at[a, k], recv_sems.at[a, k], (px, py, c)))
        return local, sends, recvs

    return _Comm(pieces, [jax.ShapeDtypeStruct((N_DEV,) + p.shape, p.dtype) for p in pieces],
                 [pltpu.SemaphoreType.DMA((n, 3)), pltpu.SemaphoreType.DMA((n, 3)), pltpu.SemaphoreType.DMA((n,))], copies)


def _ag_d2d(bufs, slots):
    n = len(bufs)

    def copies(ins, outs, sems, with_recvs):
        send_sems, recv_sems = sems
        x, y, c = _mesh_pos()
        chips = [(x, y), (1 - x, y), (x, 1 - y), (1 - x, 1 - y)]
        sends, recvs = [], []
        for a in range(n):
            for k, (px, py) in enumerate(chips):
                held = slots[a](4 * px + 2 * py + c)
                sends.append(_remote(ins[a].at[held], outs[a].at[held], send_sems.at[a, k], recv_sems.at[a, k], (x, y, 1 - c)))
                if with_recvs:
                    got = slots[a](4 * px + 2 * py + 1 - c)
                    recvs.append(_remote(ins[a].at[got], outs[a].at[got], send_sems.at[a, k], recv_sems.at[a, k], (x, y, 1 - c)))
        return [], sends, recvs

    return _Comm(bufs, [jax.ShapeDtypeStruct(b.shape, b.dtype) for b in bufs],
                 [pltpu.SemaphoreType.DMA((n, 4)), pltpu.SemaphoreType.DMA((n, 4))], copies,
                 aliases={a: a for a in range(n)})


def _rs_d2d(blocks, slots):
    n = len(blocks)

    def copies(ins, outs, sems, with_recvs):
        send_sems, recv_sems = sems
        x, y, c = _mesh_pos()
        sends, recvs = [], []
        for a in range(n):
            for j in range(4):
                src = ins[a].at[slots[a](2 * j + 1 - c)]
                sends.append(_remote(src, outs[a].at[j], send_sems.at[a, j], recv_sems.at[a, j], (x, y, 1 - c)))
                if with_recvs:
                    recvs.append(_remote(src, outs[a].at[j], send_sems.at[a, j], recv_sems.at[a, j], (x, y, 1 - c)))
        return [], sends, recvs

    return _Comm(blocks, [jax.ShapeDtypeStruct((4,) + b.shape[1:], b.dtype) for b in blocks],
                 [pltpu.SemaphoreType.DMA((n, 4)), pltpu.SemaphoreType.DMA((n, 4))], copies)


def _rs_ici(blocks):
    n = len(blocks)

    def copies(ins, outs, sems, with_recvs):
        send_sems, recv_sems = sems
        x, y, c = _mesh_pos()
        sends, recvs = [], []
        for a in range(n):
            for k, (px, py) in enumerate([(1 - x, y), (x, 1 - y), (1 - x, 1 - y)]):
                src = ins[a].at[2 * px + py]
                sends.append(_remote(src, outs[a].at[k], send_sems.at[a, k], recv_sems.at[a, k], (px, py, c)))
                if with_recvs:
                    recvs.append(_remote(src, outs[a].at[k], send_sems.at[a, k], recv_sems.at[a, k], (px, py, c)))
        return [], sends, recvs

    return _Comm(blocks, [jax.ShapeDtypeStruct((3,) + b.shape[1:], b.dtype) for b in blocks],
                 [pltpu.SemaphoreType.DMA((n, 3)), pltpu.SemaphoreType.DMA((n, 3))], copies)


def _run_comm(comm, name):
    n_in, n_out = len(comm.inputs), len(comm.out_shapes)

    def body(*refs):
        ins, outs, sems = refs[:n_in], refs[n_in:n_in + n_out], refs[n_in + n_out:]
        comm.start(ins, outs, sems)
        comm.wait(ins, outs, sems)

    return pl.pallas_call(
        body, name=name, in_specs=[HBM_SPEC] * n_in, out_specs=[HBM_SPEC] * n_out, out_shape=comm.out_shapes,
        scratch_shapes=comm.sem_shapes, input_output_aliases=comm.aliases,
    )(*comm.inputs)


W_ROWS = 784


def _w_segments():
    per = N_IN // N_DEV
    out = []
    for d in range(N_DEV):
        lo, hi = per * d, per * (d + 1)
        for a, b, start in COL_MAP:
            s, e = max(lo, a), min(hi, b)
            if s < e:
                out.append((d, s - lo, e - s, s - a + start))
    return out


def _w_gather_rows(g, name):
    tk = D_MODEL // 2
    u32 = jnp.uint32

    def body(g_ref, o_ref, scr):
        for d in range(N_DEV):
            x32 = pltpu.bitcast(g_ref[d], u32)
            for dd, src, rows, dst in _w_segments():
                if dd == d:
                    scr[dst // 2:(dst + rows) // 2, :] = x32[src // 2:(src + rows) // 2]
        scr[(COL_DT + N_HEADS) // 2:COL_XBC // 2, :] = jnp.zeros(((COL_XBC - COL_DT - N_HEADS) // 2, tk), u32)
        o_ref[...] = pltpu.bitcast(scr[...], BF16)

    return pl.pallas_call(
        body, name=name, grid=(D_MODEL // tk,),
        in_specs=[pl.BlockSpec((N_DEV, W_ROWS, tk), lambda j: (0, 0, j))],
        out_specs=pl.BlockSpec((N_INP, tk), lambda j: (0, j)),
        out_shape=jax.ShapeDtypeStruct((N_INP, D_MODEL), BF16),
        scratch_shapes=[pltpu.VMEM((N_INP // 2, tk), u32)],
        compiler_params=_cparams(("parallel",)),
    )(g)


def _w_split_rows(dwt, name):
    tk = D_MODEL // 4
    per = N_IN // N_DEV

    def body(w_ref, o_ref):
        for d, dst, rows, src in _w_segments():
            o_ref[d, dst // 2:(dst + rows) // 2, :] = pltpu.bitcast(w_ref[src:src + rows, :].astype(BF16), F32)
        for d in range(N_DEV):
            o_ref[d, per // 2:W_ROWS // 2, :] = jnp.zeros(((W_ROWS - per) // 2, tk), F32)

    return pl.pallas_call(
        body, name=name, grid=(D_MODEL // tk,),
        in_specs=[pl.BlockSpec((N_INP, tk), lambda j: (0, j))],
        out_specs=pl.BlockSpec((N_DEV, W_ROWS // 2, tk), lambda j: (0, 0, j)),
        out_shape=jax.ShapeDtypeStruct((N_DEV, W_ROWS // 2, D_MODEL), F32),
        compiler_params=_cparams(("parallel",)),
    )(dwt)


def _rs_pair_sum(g, ra, own_slots, name):
    packed = g.dtype == F32
    _, rows, cols = g.shape
    out_rows = 2 * rows if packed else rows

    def body(s_ref, g_ref, ra_ref, o_ref):
        a, b = g_ref[...], ra_ref[...]
        if packed:
            a, b = pltpu.bitcast(a, BF16), pltpu.bitcast(b, BF16)
        o_ref[...] = (a.astype(F32) + b.astype(F32)).astype(BF16)

    return pl.pallas_call(
        body, name=name,
        grid_spec=pltpu.PrefetchScalarGridSpec(
            num_scalar_prefetch=1, grid=(4,),
            in_specs=[pl.BlockSpec((None, rows, cols), lambda j, s: (s[j], 0, 0)),
                      pl.BlockSpec((None, rows, cols), lambda j, s: (j, 0, 0))],
            out_specs=pl.BlockSpec((None, out_rows, cols), lambda j, s: (j, 0, 0))),
        out_shape=jax.ShapeDtypeStruct((4, out_rows, cols), BF16),
        compiler_params=_cparams(("parallel",)),
    )(own_slots, g, ra)


def _rs_final_sum(h, rb, chip_idx, name):
    _, rows, cols = h.shape

    def body(j_ref, h_ref, rb_ref, o_ref):
        o_ref[...] = ((h_ref[...].astype(F32) + rb_ref[0].astype(F32)) + rb_ref[1].astype(F32)) + rb_ref[2].astype(F32)

    return pl.pallas_call(
        body, name=name,
        grid_spec=pltpu.PrefetchScalarGridSpec(
            num_scalar_prefetch=1, grid=(1,),
            in_specs=[pl.BlockSpec((None, rows, cols), lambda i, j: (j[0], 0, 0)),
                      pl.BlockSpec((3, rows, cols), lambda i, j: (0, 0, 0))],
            out_specs=pl.BlockSpec((rows, cols), lambda i, j: (0, 0))),
        out_shape=jax.ShapeDtypeStruct((rows, cols), F32),
        compiler_params=_cparams(("arbitrary",)),
    )(chip_idx, h, rb)


def _adamw_math(w, g, m, v):
    m = ADAM_B1 * m + (1.0 - ADAM_B1) * g
    v = ADAM_B2 * v + (1.0 - ADAM_B2) * (g * g)
    m_hat = m / (1.0 - ADAM_B1 ** ADAM_STEP)
    v_hat = v / (1.0 - ADAM_B2 ** ADAM_STEP)
    delta = -ADAM_LR * (m_hat / (jnp.sqrt(v_hat) + ADAM_EPS) + ADAM_WD * w)
    return delta, m, v


def _adamw_rows(g, w, m, v, tr, name):
    layers, rows, cols = w.shape

    def body(g_ref, w_ref, m_ref, v_ref, d_out, m_out, v_out):
        delta, m_new, v_new = _adamw_math(w_ref[...], g_ref[...], m_ref[...], v_ref[...])
        d_out[...] = delta
        m_out[...] = m_new
        v_out[...] = v_new

    blk = pl.BlockSpec((None, tr, cols), lambda a, r: (a, r, 0))
    return pl.pallas_call(
        body, name=name, grid=(layers, rows // tr),
        in_specs=[blk] * 4, out_specs=[blk] * 3,
        out_shape=[jax.ShapeDtypeStruct((layers, rows, cols), F32)] * 3,
        compiler_params=_cparams(("parallel", "parallel")),
    )(g, w, m, v)


def _adamw_cols(g_t, w, m, v, name):
    layers, k, cols = w.shape
    tr = 56
    assert g_t.shape[1] % tr == 0
    views = [jnp.transpose(a, (2, 0, 1)) for a in (w, m, v)]

    def body(g_ref, w_ref, m_ref, v_ref, g_out, d_out, m_out, v_out):
        for a in range(layers):
            g = g_ref[a]
            delta, m_new, v_new = _adamw_math(w_ref[:, a, :], g, m_ref[:, a, :], v_ref[:, a, :])
            g_out[:, a, :] = g
            d_out[:, a, :] = delta
            m_out[:, a, :] = m_new
            v_out[:, a, :] = v_new

    col = pl.BlockSpec((tr, layers, k), lambda r: (r, 0, 0))
    outs = pl.pallas_call(
        body, name=name, grid=(pl.cdiv(cols, tr),),
        in_specs=[pl.BlockSpec((layers, tr, k), lambda r: (0, r, 0)), col, col, col], out_specs=[col] * 4,
        out_shape=[jax.ShapeDtypeStruct((cols, layers, k), F32)] * 4,
        compiler_params=_cparams(("parallel",)),
    )(g_t, *views)
    return [jnp.transpose(o, (1, 2, 0)) for o in outs]


def _sum_devices(parts, name):
    _, p, _ = parts.shape

    def body(x_ref, o_ref):
        acc = x_ref[0]
        for d in range(1, N_DEV):
            acc = acc + x_ref[d]
        o_ref[...] = acc

    return pl.pallas_call(
        body, name=name, grid=(1,),
        in_specs=[pl.BlockSpec((N_DEV, p, LANE), lambda i: (0, 0, 0))],
        out_specs=pl.BlockSpec((p, LANE), lambda i: (0, 0)),
        out_shape=jax.ShapeDtypeStruct((p, LANE), F32),
        compiler_params=_cparams(("arbitrary",)),
    )(parts)


def _adamw_small(g, w, m, v, name):
    p = g.shape[0]

    def body(g_ref, w_ref, m_ref, v_ref, d_out, m_out, v_out):
        delta, m_new, v_new = _adamw_math(w_ref[...], g_ref[...], m_ref[...], v_ref[...])
        d_out[...] = delta
        m_out[...] = m_new
        v_out[...] = v_new

    spec = pl.BlockSpec((p, LANE), lambda i: (0, 0))
    return pl.pallas_call(
        body, name=name, grid=(1,),
        in_specs=[spec] * 4, out_specs=[spec] * 3,
        out_shape=[jax.ShapeDtypeStruct((p, LANE), F32)] * 3,
        compiler_params=_cparams(("arbitrary",)),
    )(g, w, m, v)


PACK_ALIGN = SUB * LANE

SMALL_PARAMS = (
    ("meta", (N_META, D_MODEL), 1),
    ("pre_g", (DEPTH, D_MODEL), None),
    ("post_g", (DEPTH, D_MODEL), None),
    ("conv_a_w", (DEPTH, CONV_A_K, D_A), 2),
    ("ssm_conv_w", (DEPTH, SSM_K, N_XBC), 2),
    ("ssm_conv_b", (DEPTH, N_XBC), None),
    ("dt_bias", (DEPTH, N_HEADS), None),
    ("a_log", (DEPTH, N_HEADS), None),
    ("d_skip", (DEPTH, N_HEADS), None),
    ("ssm_norm_g", (DEPTH, D_B), None),
    ("conf_conv_w", (DEPTH, CONF_K, D_C), 2),
    ("conf_conv_b", (DEPTH, D_C), None),
    ("conf_ln_g", (DEPTH, D_C), None),
    ("conf_ln_b", (DEPTH, D_C), None),
)


def _local_shape(shape, axis):
    if axis is None:
        return shape
    return tuple(s // N_DEV if k == axis else s for k, s in enumerate(shape))


def _pack(arrays):
    flat = []
    for a in arrays:
        v = a.reshape(-1).astype(F32)
        flat.append(v)
        if v.shape[0] % PACK_ALIGN:
            flat.append(jnp.zeros(((-v.shape[0]) % PACK_ALIGN,), F32))
    return jnp.concatenate(flat).reshape(-1, LANE)


def _unpack(buf, shapes):
    flat = buf.reshape(-1)
    out, off = [], 0
    for s in shapes:
        size = 1
        for k in s:
            size *= k
        out.append(flat[off:off + size].reshape(s))
        off += size + (-size) % PACK_ALIGN
    return out


def kernel(x, meta, pre_g, post_g, w_in, w_out, conv_a_w, ssm_conv_w, ssm_conv_b, dt_bias, a_log, d_skip, ssm_norm_g, conf_conv_w, conf_conv_b, conf_ln_g, conf_ln_b, loss_target, m_meta, m_pre_g, m_post_g, m_w_in, m_w_out, m_conv_a_w, m_ssm_conv_w, m_ssm_conv_b, m_dt_bias, m_a_log, m_d_skip, m_ssm_norm_g, m_conf_conv_w, m_conf_conv_b, m_conf_ln_g, m_conf_ln_b, v_meta, v_pre_g, v_post_g, v_w_in, v_w_out, v_conv_a_w, v_ssm_conv_w, v_ssm_conv_b, v_dt_bias, v_a_log, v_d_skip, v_ssm_norm_g, v_conf_conv_w, v_conf_conv_b, v_conf_ln_g, v_conf_ln_b):
    weights = dict(meta=meta, pre_g=pre_g, post_g=post_g, conv_a_w=conv_a_w, ssm_conv_w=ssm_conv_w, ssm_conv_b=ssm_conv_b,
                   dt_bias=dt_bias, a_log=a_log, d_skip=d_skip, ssm_norm_g=ssm_norm_g, conf_conv_w=conf_conv_w,
                   conf_conv_b=conf_conv_b, conf_ln_g=conf_ln_g, conf_ln_b=conf_ln_b)
    mom1 = dict(meta=m_meta, pre_g=m_pre_g, post_g=m_post_g, conv_a_w=m_conv_a_w, ssm_conv_w=m_ssm_conv_w,
                ssm_conv_b=m_ssm_conv_b, dt_bias=m_dt_bias, a_log=m_a_log, d_skip=m_d_skip, ssm_norm_g=m_ssm_norm_g,
                conf_conv_w=m_conf_conv_w, conf_conv_b=m_conf_conv_b, conf_ln_g=m_conf_ln_g, conf_ln_b=m_conf_ln_b)
    mom2 = dict(meta=v_meta, pre_g=v_pre_g, post_g=v_post_g, conv_a_w=v_conv_a_w, ssm_conv_w=v_ssm_conv_w,
                ssm_conv_b=v_ssm_conv_b, dt_bias=v_dt_bias, a_log=v_a_log, d_skip=v_d_skip, ssm_norm_g=v_ssm_norm_g,
                conf_conv_w=v_conf_conv_w, conf_conv_b=v_conf_conv_b, conf_ln_g=v_conf_ln_g, conf_ln_b=v_conf_ln_b)
    nb, seq, d = x.shape
    lp = PAD + N_META + seq
    t = nb * lp
    assert lp % TILE == 0 and t % (3 * LANE) == 0 and d == D_MODEL
    rt = lp // 3
    xi, yi, ci = _mesh_pos()
    dev = 4 * xi + 2 * yi + ci
    ci32 = ci.astype(jnp.int32)
    chip_idx = (2 * xi + yi).astype(jnp.int32).reshape(1)
    own_plain = jnp.stack([2 * j + ci32 for j in range(4)])
    own_mix = jnp.stack([_slot_mix_rows(2 * j + ci32) for j in range(4)]).astype(jnp.int32)
    n_in_loc = N_IN // N_DEV
    n_out_loc = 2 * D_MODEL // N_DEV
    slots = (_slot_plain, _slot_mix_rows)

    sharded_small = [n for n, _, ax in SMALL_PARAMS if ax is not None]
    small_shapes = {n: s for n, s, _ in SMALL_PARAMS}
    small_axis = {n: ax for n, _, ax in SMALL_PARAMS}
    sw_pack = _pack([weights[n] for n in sharded_small])
    (sw_g,) = _allgather([sw_pack], "ag_small_weights")
    wt_loc = jnp.pad(jnp.swapaxes(w_in, 1, 2).astype(BF16), ((0, 0), (0, W_ROWS - n_in_loc), (0, 0)))
    wo_loc = w_out.astype(BF16)

    def gathered(bufs):
        return _w_gather_rows(bufs[0], "w_gather_rows"), bufs[1].reshape(2 * D_MODEL, D_MODEL)

    full = dict(weights)
    per_dev = [_unpack(sw_g[k], [_local_shape(small_shapes[n], small_axis[n]) for n in sharded_small]) for k in range(N_DEV)]
    for q, n in enumerate(sharded_small):
        full[n] = jnp.concatenate([per_dev[k][q] for k in range(N_DEV)], axis=small_axis[n])

    e_mat = (lax.broadcasted_iota(jnp.int32, (LANE, D_B), 0) == lax.broadcasted_iota(jnp.int32, (LANE, D_B), 1) // HEAD_DIM)
    e_mat = e_mat.astype(BF16)
    e_mat_t = e_mat.T

    front = jnp.concatenate([jnp.zeros((PAD, d), F32), full["meta"]], axis=0)
    h = jnp.concatenate([jnp.concatenate([front, x[b]], axis=0) for b in range(nb)], axis=0)
    saved = []
    w_t, w_o = gathered(_run_comm(_ag_d2d(_run_comm(_ag_ici([wt_loc[0], wo_loc[0]], slots), "ag_ici"), slots), "ag_d2d"))
    hn = _rms_fwd(h, pre_g[0].reshape(1, -1), rt, "rms_fwd")
    for i in range(DEPTH):
        row = lambda a: a[i].reshape(1, -1)
        proj = _mm(hn, w_t, "nt", BF16, t // 3, MM_TN, D_MODEL, "mm_proj", cols_outer=True)
        ycat = _a_fwd(proj, full["conv_a_w"][i], lp, "a_fwd")
        xbc = _xbc_fwd(proj, full["ssm_conv_w"][i], row(ssm_conv_b), lp, "xbc_fwd")
        ycat, u1 = _c_fwd(proj, full["conf_conv_w"][i], row(conf_conv_b), row(conf_ln_g), row(conf_ln_b), ycat, lp, "c_fwd")
        d_skip_x = jnp.repeat(d_skip[i], HEAD_DIM).reshape(1, D_B)
        nxt = _ag_ici([wt_loc[i + 1], wo_loc[i + 1]], slots) if i + 1 < DEPTH else None
        ycat, yssd, states, *bufs = _ssd_fwd(xbc, proj, dt_bias[i], a_log[i], d_skip_x, row(ssm_norm_g), e_mat, ycat, lp,
                                             "ssd_fwd", comm=nxt)
        if bufs:
            m, *bufs = _mm(ycat, w_o, "nn", F32, t // 3, D_MODEL, 2 * D_MODEL, "mm_out", comm=_ag_d2d(bufs, slots))
        else:
            m = _mm(ycat, w_o, "nn", F32, t // 3, D_MODEL, 2 * D_MODEL, "mm_out")
        saved.append((h, hn, proj, ycat, yssd, states, xbc, m, d_skip_x, w_t, w_o, u1))
        if i + 1 < DEPTH:
            h, hn = _post_rms_fwd(h, m, row(post_g), pre_g[i + 1].reshape(1, -1), rt, "post_rms_fwd")
        else:
            h = _post_fwd(h, m, row(post_g), rt, "post_fwd")
        if bufs:
            w_t, w_o = gathered(bufs)

    dh, loss_blk = _loss_kernel(h, loss_target.reshape(nb * seq, d), lp, "loss")

    grads = {n: [None] * DEPTH for n, _, _ in SMALL_PARAMS if n != "meta"}
    gt_in, gr_out = [None] * DEPTH, [None] * DEPTH
    dmeta = None
    pending = None

    def rs_pair_sums(blocks, ras):
        return [_rs_pair_sum(blocks[0], ras[0], own_plain, "rs_pair_sum_in"),
                _rs_pair_sum(blocks[1], ras[1], own_mix, "rs_pair_sum_out")]

    def rs_finish(layer, hs, rbs):
        gt_in[layer] = _rs_final_sum(hs[0], rbs[0], chip_idx, "rs_final_sum_in")
        gr_out[layer] = _rs_final_sum(hs[1], rbs[1], chip_idx, "rs_final_sum_out")

    for i in reversed(range(DEPTH)):
        row = lambda a: a[i].reshape(1, -1)
        h_i, hn, proj, ycat, yssd, states, xbc, m, d_skip_x, w_t, w_o, u1 = saved[i]
        dm, grads["post_g"][i] = _post_bwd(dh, m, row(post_g), rt, "post_bwd")
        if pending is not None:
            dy, *ras = _mm(dm, w_o, "nt", BF16, t // 3, D_MODEL, D_MODEL, "mm_dy", cols_outer=True,
                           comm=_rs_d2d(list(pending), slots))
            hs = rs_pair_sums(pending, ras)
        else:
            dy = _mm(dm, w_o, "nt", BF16, t // 3, D_MODEL, D_MODEL, "mm_dy", cols_outer=True)
            hs = None
        dw_out = _mm(ycat, dm, "tn", BF16, D_MODEL, D_MODEL, t // 3, "mm_dwout")
        dxbc, dproj, grads["ssm_norm_g"][i], ddtb, dal, dds, *rbs = _ssd_bwd(
            dy, yssd, xbc, proj, states, dt_bias[i], a_log[i], d_skip_x, row(ssm_norm_g), e_mat, e_mat_t, lp, "ssd_bwd",
            comm=_rs_ici(hs) if hs is not None else None)
        if hs is not None:
            rs_finish(i + 1, hs, rbs)
        grads["dt_bias"][i] = ddtb[:, :N_HEADS]
        grads["a_log"][i] = dal[:, :N_HEADS]
        grads["d_skip"][i] = dds[:, :N_HEADS]
        dproj, grads["conv_a_w"][i] = _a_bwd(dy, proj, full["conv_a_w"][i], dproj, lp, "a_bwd")
        dproj, grads["ssm_conv_w"][i], grads["ssm_conv_b"][i] = _xbc_bwd(
            dxbc, proj, full["ssm_conv_w"][i], row(ssm_conv_b), dproj, lp, "xbc_bwd")
        dproj, grads["conf_conv_w"][i], grads["conf_conv_b"][i], grads["conf_ln_g"][i], grads["conf_ln_b"][i] = _c_bwd(
            dy, proj, u1, full["conf_conv_w"][i], row(conf_conv_b), row(conf_ln_g), row(conf_ln_b), dproj, lp, "c_bwd")
        dhn = _mm(dproj, w_t, "nn", F32, MM_TM, D_MODEL, N_INP, "mm_dhn")
        dw_in_t = _mm(dproj, hn, "tn", F32, MM_TN, D_MODEL, t, "mm_dwin")
        dh, grads["pre_g"][i], dmeta = _rms_bwd(dh, dhn, h_i, row(pre_g), lp, rt, "rms_bwd")
        pending = (_w_split_rows(dw_in_t, "w_split_rows"), dw_out.reshape(N_DEV, n_out_loc, D_MODEL))
    grad_x = dh.reshape(nb, lp, d)[:, PAD + N_META:]
    hs = rs_pair_sums(pending, _run_comm(_rs_d2d(list(pending), slots), "rs_d2d"))
    rs_finish(0, hs, _run_comm(_rs_ici(hs), "rs_ici"))

    g_w_out = jnp.stack(gr_out)
    big = {"w_in": _adamw_cols(jnp.stack(gt_in), w_in, m_w_in, v_w_in, "adamw_w_in"),
           "w_out": [g_w_out, *_adamw_rows(g_w_out, w_out, m_w_out, v_w_out, n_out_loc, "adamw_w_out")]}

    names = [n for n, _, _ in SMALL_PARAMS]
    partial = [loss_blk[0:1, 0:1], dmeta] + [jnp.concatenate(grads[n], axis=0) for n in names[1:]]
    part_pack = _pack(partial)
    (parts_g,) = _allgather([part_pack], "ag_small_grads")
    total = _unpack(_sum_devices(parts_g, "sum_small_grads"), [(1,)] + [small_shapes[n] for n in names])
    loss = total[0][0]
    g_small = {}
    for n, g in zip(names, total[1:]):
        ax = small_axis[n]
        if ax is not None:
            g = lax.dynamic_slice_in_dim(g, dev * (small_shapes[n][ax] // N_DEV), small_shapes[n][ax] // N_DEV, axis=ax)
        g_small[n] = g
    loc_shapes = [_local_shape(small_shapes[n], small_axis[n]) for n in names]
    d_pack, m_pack, v_pack = _adamw_small(_pack([g_small[n] for n in names]), _pack([weights[n] for n in names]),
                                          _pack([mom1[n] for n in names]), _pack([mom2[n] for n in names]), "adamw_small")
    d_small = dict(zip(names, _unpack(d_pack, loc_shapes)))
    m_small = dict(zip(names, _unpack(m_pack, loc_shapes)))
    v_small = dict(zip(names, _unpack(v_pack, loc_shapes)))

    order = ["meta", "pre_g", "post_g", "w_in", "w_out", "conv_a_w", "ssm_conv_w", "ssm_conv_b", "dt_bias", "a_log",
             "d_skip", "ssm_norm_g", "conf_conv_w", "conf_conv_b", "conf_ln_g", "conf_ln_b"]

    def pick(k, small):
        return [big[n][k] if n in big else small[n] for n in order]

    return (loss, grad_x, *pick(0, g_small), *pick(1, d_small), *pick(2, m_small), *pick(3, v_small))
```

```python
import functools

import jax
import jax.numpy as jnp
from jax import lax
from jax.experimental import pallas as pl
from jax.experimental.pallas import tpu as pltpu

F32 = jnp.float32
BF16 = jnp.bfloat16

D_MODEL = 1024
DEPTH = 4
SEQ = 2048
CHUNK = 64
N_META = 16
PAD = 48
LP = PAD + N_META + SEQ
D_A = 512
D_B = 1024
D_C = 512
N_HEADS = 16
HEAD_DIM = 64
N_STATE = 128
N_GROUPS = 2
GROUP_W = D_B // N_GROUPS
N_XBC = D_B + 2 * N_GROUPS * N_STATE
CONV_A_K = 3
SSM_K = 4
CONF_K = 31
NORM_EPS = 1e-6
LN_EPS = 1e-5
N_IN = 6160
N_INP = 6272
COL_BZ = 2048
COL_DT = 3072
COL_XBC = 3200
COL_C = 4736
COL_MAP = ((0, 3072, 0), (3072, 4608, COL_XBC), (4608, 4624, COL_DT), (4624, 6160, COL_C))
LANE = 128
SUB = 8

ADAM_LR = 0.001
ADAM_B1 = 0.9
ADAM_B2 = 0.999
ADAM_EPS = 1e-08
ADAM_WD = 0.01
ADAM_STEP = 10

TILE = 192
HALO_A = 16
HALO_C = 32
MM_TM = 384
MM_TN = 896
VMEM_LIMIT = 56 * 1024 * 1024

MESH = pl.DeviceIdType.MESH
N_DEV = 8


def _silu(x):
    return x * jax.nn.sigmoid(x)


def _dsilu(x):
    s = jax.nn.sigmoid(x)
    return s * (1.0 + x * (1.0 - s))


def _cparams(sem=None):
    return pltpu.CompilerParams(dimension_semantics=sem, vmem_limit_bytes=VMEM_LIMIT)


def _mm(a, b, mode, out_dtype, tm, tn, tk, name, cols_outer=False, comm=None):
    def ix(f):
        if cols_outer:
            return lambda j, i, q: f(i, j, q)
        return f

    if mode == "nn":
        (m, k), (_, n) = a.shape, b.shape
        a_spec = pl.BlockSpec((tm, tk), ix(lambda i, j, q: (i, q)))
        b_spec = pl.BlockSpec((tk, tn), ix(lambda i, j, q: (q, j)))
        dims = (((1,), (0,)), ((), ()))
    elif mode == "nt":
        (m, k), (n, _) = a.shape, b.shape
        a_spec = pl.BlockSpec((tm, tk), ix(lambda i, j, q: (i, q)))
        b_spec = pl.BlockSpec((tn, tk), ix(lambda i, j, q: (j, q)))
        dims = (((1,), (1,)), ((), ()))
    else:
        (k, m), (_, n) = a.shape, b.shape
        a_spec = pl.BlockSpec((tk, tm), ix(lambda i, j, q: (q, i)))
        b_spec = pl.BlockSpec((tk, tn), ix(lambda i, j, q: (q, j)))
        dims = (((0,), (0,)), ((), ()))
    assert m % tm == 0 and n % tn == 0 and k % tk == 0, (name, a.shape, b.shape)
    nk = k // tk
    grid = (n // tn, m // tm, nk) if cols_outer else (m // tm, n // tn, nk)

    def body(a_ref, b_ref, o_ref, acc_ref):
        part = lax.dot_general(a_ref[...].astype(BF16), b_ref[...].astype(BF16), dims, preferred_element_type=F32)
        if nk == 1:
            o_ref[...] = part.astype(o_ref.dtype)
        else:
            q = pl.program_id(2)

            @pl.when(q == 0)
            def _():
                acc_ref[...] = part

            @pl.when(q > 0)
            def _():
                acc_ref[...] += part

            @pl.when(q == nk - 1)
            def _():
                o_ref[...] = acc_ref[...].astype(o_ref.dtype)

    res = _grid_call(
        body, name, grid, [a_spec, b_spec], [pl.BlockSpec((tm, tn), ix(lambda i, j, q: (i, j)))],
        [jax.ShapeDtypeStruct((m, n), out_dtype)], [pltpu.VMEM((tm, tn) if nk > 1 else (SUB, LANE), F32)], (a, b), comm)
    return res[0] if comm is None else res


def _row_mask(i, tpe, rows):
    r = lax.broadcasted_iota(jnp.int32, (rows, 1), 0)
    return jnp.logical_or((i % tpe) != 0, r >= PAD)


def _rms_fwd(h, g, rt, name):
    t, d = h.shape

    def body(h_ref, g_ref, o_ref):
        x = h_ref[...]
        r = lax.rsqrt(jnp.mean(x * x, axis=-1, keepdims=True) + NORM_EPS)
        o_ref[...] = (x * r * g_ref[...]).astype(BF16)

    return pl.pallas_call(
        body, name=name, grid=(t // rt,),
        in_specs=[pl.BlockSpec((rt, d), lambda i: (i, 0)), pl.BlockSpec((1, d), lambda i: (0, 0))],
        out_specs=pl.BlockSpec((rt, d), lambda i: (i, 0)),
        out_shape=jax.ShapeDtypeStruct((t, d), BF16),
        compiler_params=_cparams(("parallel",)),
    )(h, g)


def _post_fwd(h, m, g, rt, name):
    t, d = h.shape

    def body(h_ref, m_ref, g_ref, o_ref):
        x = m_ref[...]
        r = lax.rsqrt(jnp.mean(x * x, axis=-1, keepdims=True) + NORM_EPS)
        o_ref[...] = h_ref[...] + x * r * g_ref[...]

    row = pl.BlockSpec((rt, d), lambda i: (i, 0))
    return pl.pallas_call(
        body, name=name, grid=(t // rt,),
        in_specs=[row, row, pl.BlockSpec((1, d), lambda i: (0, 0))], out_specs=row,
        out_shape=jax.ShapeDtypeStruct((t, d), F32),
        compiler_params=_cparams(("parallel",)),
    )(h, m, g)


def _post_rms_fwd(h, m, g_post, g_next, rt, name):
    t, d = h.shape

    def body(h_ref, m_ref, gp_ref, gn_ref, o_ref, n_ref):
        x = m_ref[...]
        r = lax.rsqrt(jnp.mean(x * x, axis=-1, keepdims=True) + NORM_EPS)
        y = h_ref[...] + x * r * gp_ref[...]
        o_ref[...] = y
        r2 = lax.rsqrt(jnp.mean(y * y, axis=-1, keepdims=True) + NORM_EPS)
        n_ref[...] = (y * r2 * gn_ref[...]).astype(BF16)

    row = pl.BlockSpec((rt, d), lambda i: (i, 0))
    vec = pl.BlockSpec((1, d), lambda i: (0, 0))
    return pl.pallas_call(
        body, name=name, grid=(t // rt,),
        in_specs=[row, row, vec, vec], out_specs=[row, row],
        out_shape=[jax.ShapeDtypeStruct((t, d), F32), jax.ShapeDtypeStruct((t, d), BF16)],
        compiler_params=_cparams(("parallel",)),
    )(h, m, g_post, g_next)


def _rms_bwd_math(x, g, dy):
    r = lax.rsqrt(jnp.mean(x * x, axis=-1, keepdims=True) + NORM_EPS)
    gdy = dy * g
    dx = r * gdy - x * (r * r * r) * jnp.mean(gdy * x, axis=-1, keepdims=True)
    return dx, dy * x * r


def _post_bwd(dh, m, g, rt, name):
    t, d = dh.shape

    def body(dh_ref, m_ref, g_ref, dm_ref, dg_ref):
        dm, dgt = _rms_bwd_math(m_ref[...], g_ref[...], dh_ref[...])
        dm_ref[...] = dm.astype(BF16)

        @pl.when(pl.program_id(0) == 0)
        def _():
            dg_ref[...] = jnp.zeros_like(dg_ref)

        dg_ref[...] += jnp.sum(dgt, axis=0, keepdims=True)

    row = pl.BlockSpec((rt, d), lambda i: (i, 0))
    vec = pl.BlockSpec((1, d), lambda i: (0, 0))
    return pl.pallas_call(
        body, name=name, grid=(t // rt,),
        in_specs=[row, row, vec], out_specs=[row, vec],
        out_shape=[jax.ShapeDtypeStruct((t, d), BF16), jax.ShapeDtypeStruct((1, d), F32)],
        compiler_params=_cparams(("arbitrary",)),
    )(dh, m, g)


def _rms_bwd(dh_res, dhn, h, g, lp, rt, name):
    t, d = h.shape
    tpe = lp // rt

    def body(dr_ref, dn_ref, h_ref, g_ref, dh_ref, dg_ref, dmeta_ref):
        i = pl.program_id(0)
        dx, dgt = _rms_bwd_math(h_ref[...], g_ref[...], dn_ref[...])
        dh = dr_ref[...] + dx
        dh_ref[...] = dh

        @pl.when(i == 0)
        def _():
            dg_ref[...] = jnp.zeros_like(dg_ref)
            dmeta_ref[...] = jnp.zeros_like(dmeta_ref)

        dg_ref[...] += jnp.sum(dgt, axis=0, keepdims=True)

        @pl.when((i % tpe) == 0)
        def _():
            dmeta_ref[...] += dh[PAD:PAD + N_META, :]

    row = pl.BlockSpec((rt, d), lambda i: (i, 0))
    vec = pl.BlockSpec((1, d), lambda i: (0, 0))
    return pl.pallas_call(
        body, name=name, grid=(t // rt,),
        in_specs=[row, row, row, vec],
        out_specs=[row, vec, pl.BlockSpec((N_META, d), lambda i: (0, 0))],
        out_shape=[jax.ShapeDtypeStruct((t, d), F32), jax.ShapeDtypeStruct((1, d), F32),
                   jax.ShapeDtypeStruct((N_META, d), F32)],
        compiler_params=_cparams(("arbitrary",)),
    )(dh_res, dhn, h, g)


def _loss_kernel(h, target, lp, name):
    t, d = h.shape
    nb = t // lp
    seq = lp - PAD - N_META
    rows = next(r for r in (1024, 512, 256, 128, CHUNK) if seq % r == 0)
    per = seq // rows

    def body(h_ref, t_ref, zero_ref, dh_ref, loss_ref):
        @pl.when(jnp.logical_and(pl.program_id(0) == 0, pl.program_id(1) == 0))
        def _():
            loss_ref[...] = jnp.zeros_like(loss_ref)

        err = h_ref[...] - t_ref[...]
        dh_ref[...] = err * (1.0 / d)
        loss_ref[...] += (0.5 / d) * jnp.sum(err * err)

    win = pl.BlockSpec((pl.Element(rows), pl.Element(d)),
                       lambda b, j: (pl.multiple_of(b * lp + PAD + N_META + j * rows, CHUNK), 0))
    return pl.pallas_call(
        body, name=name, grid=(nb, per),
        in_specs=[win, pl.BlockSpec((rows, d), lambda b, j: (b * per + j, 0)), HBM_SPEC],
        out_specs=[win, pl.BlockSpec((SUB, LANE), lambda b, j: (0, 0))],
        out_shape=[jax.ShapeDtypeStruct((t, d), F32), jax.ShapeDtypeStruct((SUB, LANE), F32)],
        input_output_aliases={2: 0},
        compiler_params=_cparams(("arbitrary", "arbitrary")),
    )(h, target, jnp.zeros((t, d), F32))


def _window_spec(rows, width, col):
    return pl.BlockSpec((pl.Element(rows), pl.Element(width)), lambda i: (i * rows, col))


def _halo_specs(t, width, col, halo):
    cur = _window_spec(TILE, width, col)
    prev = pl.BlockSpec((pl.Element(halo), pl.Element(width)),
                        lambda i: (pl.multiple_of(jnp.maximum(i * TILE - halo, 0), halo), col))
    nxt = pl.BlockSpec((pl.Element(halo), pl.Element(width)),
                       lambda i: (pl.multiple_of(jnp.minimum((i + 1) * TILE, t - halo), halo), col))
    return cur, prev, nxt


def _f32(ref, lo, hi):
    return ref[:, lo:hi].astype(F32)


def _rows_from(x, start, rows):
    s = start % SUB
    if s == 0:
        return x[start:start + rows]
    return pltpu.roll(x, x.shape[0] - s, axis=0)[start - s:start - s + rows]


def _conv_a(ve, w):
    rows = ve.shape[0] - HALO_A
    return (w[0:1] * _rows_from(ve, HALO_A - 2, rows) + w[1:2] * _rows_from(ve, HALO_A - 1, rows)
            + w[2:3] * ve[HALO_A:HALO_A + rows])


def _a_fwd(proj, w, lp, name):
    t = proj.shape[0]
    tpe = lp // TILE
    cur, prev, _ = _halo_specs(t, 4 * D_A, 0, HALO_A)

    def body(p_ref, ph_ref, w_ref, y_ref):
        first = (pl.program_id(0) % tpe) == 0
        v = _f32(p_ref, D_A, 2 * D_A) * _f32(p_ref, 2 * D_A, 3 * D_A)
        vh = jnp.where(first, 0.0, _f32(ph_ref, D_A, 2 * D_A) * _f32(ph_ref, 2 * D_A, 3 * D_A))
        cv = _conv_a(jnp.concatenate([vh, v], axis=0), w_ref[...])
        y_ref[...] = (_f32(p_ref, 0, D_A) * cv * _silu(_f32(p_ref, 3 * D_A, 4 * D_A))).astype(BF16)

    return pl.pallas_call(
        body, name=name, grid=(t // TILE,),
        in_specs=[cur, prev, pl.BlockSpec((CONV_A_K, D_A), lambda i: (0, 0))],
        out_specs=_window_spec(TILE, D_A, D_B),
        out_shape=jax.ShapeDtypeStruct((t, D_B + D_A + D_C), BF16),
        compiler_params=_cparams(("parallel",)),
    )(proj, proj, w)


def _a_bwd(dy, proj, w, dproj, lp, name):
    t = proj.shape[0]
    tpe = lp // TILE
    cur, prev, nxt = _halo_specs(t, 4 * D_A, 0, HALO_A)
    dcur, _, dnxt = _halo_specs(t, D_A, D_B, HALO_A)

    def body(dy_ref, dyn_ref, p_ref, ph_ref, pn_ref, w_ref, dproj_ref, dp_ref, dw_ref):
        i = pl.program_id(0)
        first = (i % tpe) == 0
        last = (i % tpe) == tpe - 1
        w = w_ref[...]
        ab, ac, ax, az = (_f32(p_ref, k * D_A, (k + 1) * D_A) for k in range(4))
        v = ac * ax
        vh = jnp.where(first, 0.0, _f32(ph_ref, D_A, 2 * D_A) * _f32(ph_ref, 2 * D_A, 3 * D_A))
        ve = jnp.concatenate([vh, v], axis=0)
        taps = [_rows_from(ve, HALO_A - 2 + k, TILE) for k in range(CONV_A_K)]
        cv = w[0:1] * taps[0] + w[1:2] * taps[1] + w[2:3] * taps[2]
        s = _silu(az)
        dy_ = dy_ref[...].astype(F32)
        dcv = dy_ * ab * s
        dcvn = jnp.where(last, 0.0, dyn_ref[...].astype(F32) * _f32(pn_ref, 0, D_A) * _silu(_f32(pn_ref, 3 * D_A, 4 * D_A)))
        dce = jnp.concatenate([dcv, dcvn], axis=0)
        dv = w[2:3] * dce[0:TILE] + w[1:2] * _rows_from(dce, 1, TILE) + w[0:1] * _rows_from(dce, 2, TILE)
        dp = jnp.concatenate([dy_ * cv * s, dv * ax, dv * ac, dy_ * ab * cv * _dsilu(az)], axis=1)
        dp_ref[...] = jnp.where(_row_mask(i, tpe, TILE), dp, 0.0).astype(BF16)
        dw = jnp.concatenate(
            [jnp.sum(dcv * taps[k], axis=0, keepdims=True) for k in range(CONV_A_K)], axis=0)

        @pl.when(i == 0)
        def _():
            dw_ref[...] = jnp.zeros_like(dw_ref)

        dw_ref[...] += dw

    wspec = pl.BlockSpec((CONV_A_K, D_A), lambda i: (0, 0))
    return pl.pallas_call(
        body, name=name, grid=(t // TILE,),
        in_specs=[dcur, dnxt, cur, prev, nxt, wspec, HBM_SPEC],
        out_specs=[_window_spec(TILE, 4 * D_A, 0), wspec],
        out_shape=[jax.ShapeDtypeStruct(dproj.shape, dproj.dtype), jax.ShapeDtypeStruct((CONV_A_K, D_A), F32)],
        input_output_aliases={6: 0},
        compiler_params=_cparams(("arbitrary",)),
    )(dy, dy, proj, proj, proj, w, dproj)


def _conv_ssm(xe, w, rows, off):
    acc = w[0:1] * _rows_from(xe, off - 3, rows)
    for k in range(1, SSM_K):
        acc = acc + w[k:k + 1] * _rows_from(xe, off - 3 + k, rows)
    return acc


def _xbc_fwd(proj, w, b, lp, name):
    t = proj.shape[0]
    tpe = lp // TILE
    cur, prev, _ = _halo_specs(t, N_XBC, COL_XBC, HALO_A)

    def body(x_ref, xh_ref, w_ref, b_ref, o_ref):
        first = (pl.program_id(0) % tpe) == 0
        xh = jnp.where(first, 0.0, xh_ref[...].astype(F32))
        xe = jnp.concatenate([xh, x_ref[...].astype(F32)], axis=0)
        o_ref[...] = _silu(_conv_ssm(xe, w_ref[...], TILE, HALO_A) + b_ref[...]).astype(BF16)

    return pl.pallas_call(
        body, name=name, grid=(t // TILE,),
        in_specs=[cur, prev, pl.BlockSpec((SSM_K, N_XBC), lambda i: (0, 0)), pl.BlockSpec((1, N_XBC), lambda i: (0, 0))],
        out_specs=pl.BlockSpec((TILE, N_XBC), lambda i: (i, 0)),
        out_shape=jax.ShapeDtypeStruct((t, N_XBC), BF16),
        compiler_params=_cparams(("parallel",)),
    )(proj, proj, w, b)


def _xbc_bwd(dxbc, proj, w, b, dproj, lp, name):
    t = proj.shape[0]
    tpe = lp // TILE
    cur, prev, nxt = _halo_specs(t, N_XBC, COL_XBC, HALO_A)
    dcur, _, dnxt = _halo_specs(t, N_XBC, 0, HALO_A)

    def body(d_ref, dn_ref, x_ref, xh_ref, xn_ref, w_ref, b_ref, dproj_ref, dx_ref, dw_ref, db_ref):
        i = pl.program_id(0)
        first = (i % tpe) == 0
        last = (i % tpe) == tpe - 1
        w = w_ref[...]
        xh = jnp.where(first, 0.0, xh_ref[...].astype(F32))
        xe = jnp.concatenate([xh, x_ref[...].astype(F32), xn_ref[...].astype(F32)], axis=0)
        taps = [_rows_from(xe, HALO_A - 3 + k, TILE + HALO_A) for k in range(SSM_K)]
        pre = b_ref[...] + w[0:1] * taps[0]
        for k in range(1, SSM_K):
            pre = pre + w[k:k + 1] * taps[k]
        de = jnp.concatenate([d_ref[...].astype(F32), jnp.where(last, 0.0, dn_ref[...].astype(F32))], axis=0)
        dpre = de * _dsilu(pre)
        dx = w[3:4] * dpre[0:TILE]
        for k in range(SSM_K - 1):
            dx = dx + w[k:k + 1] * _rows_from(dpre, 3 - k, TILE)
        dx_ref[...] = jnp.where(_row_mask(i, tpe, TILE), dx, 0.0).astype(BF16)
        dpc = dpre[0:TILE]
        dw = jnp.concatenate(
            [jnp.sum(dpc * taps[k][0:TILE], axis=0, keepdims=True) for k in range(SSM_K)], axis=0)

        @pl.when(i == 0)
        def _():
            dw_ref[...] = jnp.zeros_like(dw_ref)
            db_ref[...] = jnp.zeros_like(db_ref)

        dw_ref[...] += dw
        db_ref[...] += jnp.sum(dpc, axis=0, keepdims=True)

    wspec = pl.BlockSpec((SSM_K, N_XBC), lambda i: (0, 0))
    bspec = pl.BlockSpec((1, N_XBC), lambda i: (0, 0))
    return pl.pallas_call(
        body, name=name, grid=(t // TILE,),
        in_specs=[dcur, dnxt, cur, prev, nxt, wspec, bspec, HBM_SPEC],
        out_specs=[_window_spec(TILE, N_XBC, COL_XBC), wspec, bspec],
        out_shape=[jax.ShapeDtypeStruct(dproj.shape, dproj.dtype), jax.ShapeDtypeStruct((SSM_K, N_XBC), F32),
                   jax.ShapeDtypeStruct((1, N_XBC), F32)],
        input_output_aliases={7: 0},
        compiler_params=_cparams(("arbitrary",)),
    )(dxbc, dxbc, proj, proj, proj, w, b, dproj)


SUBROWS = 32


def _fill_shifted(scr, x):
    scr[0] = x
    for s in range(1, SUB):
        scr[s] = pltpu.roll(x, x.shape[0] - s, axis=0)


def _window(scr, start, rows):
    s = start % SUB
    return scr[s, start - s:start - s + rows, :]


def _conv_conf(scr, w, rows, off, base):
    acc = w[0:1] * _window(scr, base + off - (CONF_K - 1), rows)
    for k in range(1, CONF_K):
        acc = acc + w[k:k + 1] * _window(scr, base + off - (CONF_K - 1) + k, rows)
    return acc


def _ln_fwd(u1, g, b):
    mu = jnp.mean(u1, axis=-1, keepdims=True)
    xc = u1 - mu
    rstd = lax.rsqrt(jnp.mean(xc * xc, axis=-1, keepdims=True) + LN_EPS)
    n = xc * rstd
    return n, rstd, n * g + b


def _c_fwd(proj, w, cb, g, b, ybuf, lp, name):
    t = proj.shape[0]
    tpe = lp // TILE
    cur, prev, _ = _halo_specs(t, 3 * D_C, COL_C, HALO_C)

    def body(p_ref, ph_ref, w_ref, cb_ref, g_ref, b_ref, ybuf_ref, y_ref, u1_ref, u0_scr):
        first = (pl.program_id(0) % tpe) == 0
        u0h = jnp.where(first, 0.0, _f32(ph_ref, 0, D_C) * jax.nn.sigmoid(_f32(ph_ref, D_C, 2 * D_C)))
        _fill_shifted(u0_scr, jnp.concatenate([u0h, _f32(p_ref, 0, D_C) * jax.nn.sigmoid(_f32(p_ref, D_C, 2 * D_C))], axis=0))
        w = w_ref[...]
        for r0 in range(0, TILE, SUBROWS):
            u1 = _conv_conf(u0_scr, w, SUBROWS, HALO_C, r0) + cb_ref[...]
            u1_ref[r0:r0 + SUBROWS, :] = u1.astype(BF16)
            _, _, u2 = _ln_fwd(u1, g_ref[...], b_ref[...])
            cz = p_ref[r0:r0 + SUBROWS, 2 * D_C:3 * D_C].astype(F32)
            y_ref[r0:r0 + SUBROWS, :] = (_silu(u2) * _silu(cz)).astype(BF16)

    vec = pl.BlockSpec((1, D_C), lambda i: (0, 0))
    return pl.pallas_call(
        body, name=name, grid=(t // TILE,),
        in_specs=[cur, prev, pl.BlockSpec((CONF_K, D_C), lambda i: (0, 0)), vec, vec, vec, HBM_SPEC],
        out_specs=[_window_spec(TILE, D_C, D_B + D_A), pl.BlockSpec((TILE, D_C), lambda i: (i, 0))],
        out_shape=[jax.ShapeDtypeStruct(ybuf.shape, ybuf.dtype), jax.ShapeDtypeStruct((t, D_C), BF16)],
        scratch_shapes=[pltpu.VMEM((SUB, HALO_C + TILE, D_C), F32)],
        input_output_aliases={6: 0},
        compiler_params=_cparams(("parallel",)),
    )(proj, proj, w, cb, g, b, ybuf)


def _c_bwd(dy, proj, u1, w, cb, g, b, dproj, lp, name):
    t = proj.shape[0]
    tpe = lp // TILE
    cur, prev, nxt = _halo_specs(t, 3 * D_C, COL_C, HALO_C)
    dcur, _, dnxt = _halo_specs(t, D_C, D_B + D_A, HALO_C)
    ucur, _, unxt = _halo_specs(t, D_C, 0, HALO_C)
    ext = TILE + HALO_C

    def body(dy_ref, dyn_ref, p_ref, ph_ref, pn_ref, u1_ref, u1n_ref, w_ref, cb_ref, g_ref, b_ref, dproj_ref,
             dp_ref, dw_ref, dcb_ref, dg_ref, db_ref, u0_scr, du1_scr, wacc_scr):
        i = pl.program_id(0)
        first = (i % tpe) == 0
        last = (i % tpe) == tpe - 1
        w = w_ref[...]

        @pl.when(i == 0)
        def _():
            dw_ref[...] = jnp.zeros_like(dw_ref)
            dcb_ref[...] = jnp.zeros_like(dcb_ref)
            dg_ref[...] = jnp.zeros_like(dg_ref)
            db_ref[...] = jnp.zeros_like(db_ref)

        u0h = jnp.where(first, 0.0, _f32(ph_ref, 0, D_C) * jax.nn.sigmoid(_f32(ph_ref, D_C, 2 * D_C)))
        _fill_shifted(u0_scr, jnp.concatenate(
            [u0h, _f32(p_ref, 0, D_C) * jax.nn.sigmoid(_f32(p_ref, D_C, 2 * D_C))], axis=0))
        dcb = jnp.zeros((1, D_C), F32)
        dg = jnp.zeros((1, D_C), F32)
        db = jnp.zeros((1, D_C), F32)
        for r0 in range(0, ext, SUBROWS):
            in_tile = r0 < TILE
            src, dsrc, usrc, q0 = (p_ref, dy_ref, u1_ref, r0) if in_tile else (pn_ref, dyn_ref, u1n_ref, r0 - TILE)
            u1 = usrc[q0:q0 + SUBROWS, :].astype(F32)
            n, rstd, u2 = _ln_fwd(u1, g_ref[...], b_ref[...])
            cz = src[q0:q0 + SUBROWS, 2 * D_C:3 * D_C].astype(F32)
            dyc = dsrc[q0:q0 + SUBROWS, :].astype(F32)
            if not in_tile:
                dyc = jnp.where(last, 0.0, dyc)
            du2 = dyc * _silu(cz) * _dsilu(u2)
            dn = du2 * g_ref[...]
            du1 = rstd * (dn - jnp.mean(dn, axis=-1, keepdims=True) - n * jnp.mean(dn * n, axis=-1, keepdims=True))
            du1_scr[0, r0:r0 + SUBROWS, :] = du1
            if in_tile:
                dp_ref[r0:r0 + SUBROWS, 2 * D_C:3 * D_C] = (dyc * _silu(u2) * _dsilu(cz)).astype(BF16)
                dg = dg + jnp.sum(du2 * n, axis=0, keepdims=True)
                db = db + jnp.sum(du2, axis=0, keepdims=True)
                dcb = dcb + jnp.sum(du1, axis=0, keepdims=True)
        dcb_ref[...] += dcb
        dg_ref[...] += dg
        db_ref[...] += db
        mask = _row_mask(i, tpe, TILE)
        _fill_shifted(du1_scr, du1_scr[0])
        for r0 in range(0, TILE, SUBROWS):
            acc = w[0:1] * _window(du1_scr, r0 + CONF_K - 1, SUBROWS)
            for k in range(1, CONF_K):
                acc = acc + w[k:k + 1] * _window(du1_scr, r0 + CONF_K - 1 - k, SUBROWS)
            ca = p_ref[r0:r0 + SUBROWS, 0:D_C].astype(F32)
            sg = jax.nn.sigmoid(p_ref[r0:r0 + SUBROWS, D_C:2 * D_C].astype(F32))
            m = mask[r0:r0 + SUBROWS]
            dp_ref[r0:r0 + SUBROWS, 0:D_C] = jnp.where(m, acc * sg, 0.0).astype(BF16)
            dp_ref[r0:r0 + SUBROWS, D_C:2 * D_C] = jnp.where(m, acc * ca * sg * (1.0 - sg), 0.0).astype(BF16)
        for k in range(CONF_K):
            part = jnp.zeros((SUB, D_C), F32)
            for r0 in range(0, TILE, SUBROWS):
                prod = du1_scr[0, r0:r0 + SUBROWS, :] * _window(u0_scr, HALO_C + r0 - (CONF_K - 1) + k, SUBROWS)
                for q in range(0, SUBROWS, SUB):
                    part = part + prod[q:q + SUB]
            wacc_scr[k:k + 1, :] = jnp.sum(part, axis=0, keepdims=True)
        dw_ref[...] += wacc_scr[0:CONF_K, :]

    vec = pl.BlockSpec((1, D_C), lambda i: (0, 0))
    wspec = pl.BlockSpec((CONF_K, D_C), lambda i: (0, 0))
    return pl.pallas_call(
        body, name=name, grid=(t // TILE,),
        in_specs=[dcur, dnxt, cur, prev, nxt, ucur, unxt, wspec, vec, vec, vec, HBM_SPEC],
        out_specs=[_window_spec(TILE, 3 * D_C, COL_C), wspec, vec, vec, vec],
        out_shape=[jax.ShapeDtypeStruct(dproj.shape, dproj.dtype), jax.ShapeDtypeStruct((CONF_K, D_C), F32),
                   jax.ShapeDtypeStruct((1, D_C), F32), jax.ShapeDtypeStruct((1, D_C), F32),
                   jax.ShapeDtypeStruct((1, D_C), F32)],
        scratch_shapes=[pltpu.VMEM((SUB, ext, D_C), F32), pltpu.VMEM((SUB, ext, D_C), F32),
                        pltpu.VMEM((HALO_C, D_C), F32)],
        input_output_aliases={11: 0},
        compiler_params=_cparams(("arbitrary",)),
    )(dy, dy, proj, proj, proj, u1, u1, w, cb, g, b, dproj)


def _split_dot(x, m_bf16, terms):
    acc = None
    rem = x
    for _ in range(terms):
        hi = rem.astype(BF16)
        part = jnp.dot(hi, m_bf16, preferred_element_type=F32)
        acc = part if acc is None else acc + part
        rem = rem - hi.astype(F32)
    return acc


def _split_pieces(x, terms):
    out, rem = [], x
    for _ in range(terms):
        hi = rem.astype(BF16)
        out.append(hi)
        rem = rem - hi.astype(F32)
    return out


def _split_dot_many(xs, terms, m_bf16):
    pieces = [p for x, n in zip(xs, terms) for p in _split_pieces(x, n)]
    prod = jnp.dot(jnp.concatenate(pieces, axis=0), m_bf16, preferred_element_type=F32)
    out, off = [], 0
    for x, n in zip(xs, terms):
        rows = x.shape[0]
        acc = prod[off:off + rows]
        for q in range(1, n):
            acc = acc + prod[off + q * rows:off + (q + 1) * rows]
        out.append(acc)
        off += n * rows
    return out


def _split_dot_left(m_bf16, x, terms):
    cols = x.shape[1]
    prod = jnp.dot(m_bf16, jnp.concatenate(_split_pieces(x, terms), axis=1), preferred_element_type=F32)
    acc = prod[:, 0:cols]
    for q in range(1, terms):
        acc = acc + prod[:, q * cols:(q + 1) * cols]
    return acc


def _tri(rows_ge_cols):
    r = lax.broadcasted_iota(jnp.int32, (CHUNK, CHUNK), 0)
    c = lax.broadcasted_iota(jnp.int32, (CHUNK, CHUNK), 1)
    return (r >= c) if rows_ge_cols else (r <= c)


def _softplus(x):
    return jnp.maximum(x, 0.0) + jnp.log(1.0 + jnp.exp(-jnp.abs(x)))


def _ssd_common(dtraw, dtb, dtb_t, a_log, a_log_t, e_mat, valid_col, valid_row, x_terms=2):
    a = -jnp.exp(a_log)
    lane = lax.broadcasted_iota(jnp.int32, (1, LANE), 1)
    a = jnp.where(lane < N_HEADS, a, 0.0)
    a_t = -jnp.exp(a_log_t)
    dt = jnp.where(valid_col, _softplus(dtraw + dtb), 0.0)
    dt = jnp.where(lane < N_HEADS, dt, 0.0)
    dt_t = jnp.where(valid_row, _softplus(dtraw.T[0:N_HEADS, :] + dtb_t), 0.0)
    ltri = _tri(True).astype(BF16)
    utri = _tri(False).astype(BF16)
    big_a = _split_dot_left(ltri, dt * a, 3)
    big_a_t = _split_dot_many([dt_t * a_t], [3], utri)[0]
    e_a = jnp.exp(big_a)
    d_s = jnp.exp(big_a[CHUNK - 1:CHUNK, :] - big_a)
    if x_terms == 1:
        dt_x, e_a_x, d_s_x = _split_dot_many([dt, e_a, d_s], [1, 2, 1], e_mat)
    else:
        dt_x, e_a_x, d_s_x = (_split_dot(q, e_mat, x_terms) for q in (dt, e_a, d_s))
    cd_x = e_a_x[CHUNK - 1:CHUNK, :]
    return a, dt, big_a, big_a_t, e_a, d_s, dt_x, e_a_x, d_s_x, cd_x


def _decay(big_a, big_a_t, h, transposed):
    col = big_a[:, h:h + 1]
    row = big_a_t[h:h + 1, :]
    if not transposed:
        seg = col - row
        return jnp.where(_tri(True), jnp.exp(jnp.minimum(seg, 0.0)), 0.0)
    seg = row - col
    return jnp.where(_tri(False), jnp.exp(jnp.minimum(seg, 0.0)), 0.0)


NT_DIMS = (((1,), (1,)), ((), ()))
TN_DIMS = (((0,), (0,)), ((), ()))
HBM_SPEC = pl.BlockSpec(memory_space=pl.ANY)


class _Comm:
    def __init__(self, inputs, out_shapes, sem_shapes, copies, aliases=None):
        self.inputs, self.out_shapes, self.sem_shapes, self.copies = inputs, out_shapes, sem_shapes, copies
        self.aliases = aliases or {}

    def start(self, ins, outs, sems):
        local, sends, _ = self.copies(ins, outs, sems, False)
        for cp in local + sends:
            cp.start()

    def wait(self, ins, outs, sems):
        local, sends, recvs = self.copies(ins, outs, sems, True)
        for cp in recvs:
            cp.wait_recv()
        for cp in sends:
            cp.wait_send()
        for cp in local:
            cp.wait()


def _grid_call(body, name, grid, in_specs, out_specs, out_shape, scratch_shapes, operands, comm=None, aliases=None):
    aliases = dict(aliases or {})
    if comm is None:
        return pl.pallas_call(
            body, name=name, grid=grid, in_specs=in_specs, out_specs=out_specs, out_shape=out_shape,
            scratch_shapes=scratch_shapes, input_output_aliases=aliases,
            compiler_params=_cparams(("arbitrary",) * len(grid)))(*operands)
    n_in, n_out, n_scr = len(in_specs), len(out_specs), len(scratch_shapes)
    nci, nco = len(comm.inputs), len(comm.out_shapes)

    def wrapped(*refs):
        ins, cins = refs[:n_in], refs[n_in:n_in + nci]
        o0 = n_in + nci
        outs, couts = refs[o0:o0 + n_out], refs[o0 + n_out:o0 + n_out + nco]
        s0 = o0 + n_out + nco
        scr, csems = refs[s0:s0 + n_scr], refs[s0 + n_scr:]
        first = pl.program_id(0) == 0
        last = pl.program_id(0) == grid[0] - 1
        for k in range(1, len(grid)):
            first = jnp.logical_and(first, pl.program_id(k) == 0)
            last = jnp.logical_and(last, pl.program_id(k) == grid[k] - 1)

        @pl.when(first)
        def _():
            comm.start(cins, couts, csems)

        body(*ins, *outs, *scr)

        @pl.when(last)
        def _():
            comm.wait(cins, couts, csems)

    res = pl.pallas_call(
        wrapped, name=name, grid=grid,
        in_specs=list(in_specs) + [HBM_SPEC] * nci, out_specs=list(out_specs) + [HBM_SPEC] * nco,
        out_shape=list(out_shape) + list(comm.out_shapes),
        scratch_shapes=list(scratch_shapes) + list(comm.sem_shapes),
        input_output_aliases={**aliases, **{n_in + k: n_out + v for k, v in comm.aliases.items()}},
        compiler_params=_cparams(("arbitrary",) * len(grid)))(*operands, *comm.inputs)
    return res


def _ssd_fwd(xbc, proj, dtb, a_log, d_skip_x, norm_g, e_mat, ybuf, lp, name, comm=None):
    t = xbc.shape[0]
    cpe = lp // CHUNK
    nb = t // lp
    dtb_p = jnp.pad(dtb.reshape(1, N_HEADS), ((0, 0), (0, LANE - N_HEADS)))
    alog_p = jnp.pad(a_log.reshape(1, N_HEADS), ((0, 0), (0, LANE - N_HEADS)))
    dtb_t = dtb.reshape(N_HEADS, 1)
    alog_t = a_log.reshape(N_HEADS, 1)

    def body(xbc_ref, bz_ref, dt_ref, dtb_ref, dtbt_ref, al_ref, alt_ref, dx_ref, g_ref, e_ref, ybuf_ref,
             yb_ref, ys_ref, st_ref, s_scr):
        c = pl.program_id(0)

        @pl.when(c == 0)
        def _():
            s_scr[...] = jnp.zeros_like(s_scr)

        rows = lax.broadcasted_iota(jnp.int32, (CHUNK, 1), 0)
        cols = lax.broadcasted_iota(jnp.int32, (1, CHUNK), 1)
        valid_col = jnp.logical_or(c > 0, rows >= PAD)
        valid_row = jnp.logical_or(c > 0, cols >= PAD)
        e_mat = e_ref[...]
        ex = []
        for b in range(nb):
            _, _, big_a, big_a_t, _, _, dt_x, e_a_x, d_s_x, cd_x = _ssd_common(
                dt_ref[b].astype(F32), dtb_ref[...], dtbt_ref[...], al_ref[...], alt_ref[...], e_mat, valid_col, valid_row)
            xs = xbc_ref[b, :, 0:D_B].astype(F32)
            xdt = xs * dt_x
            st_prev = s_scr[b]
            ex.append(dict(big_a=big_a, big_a_t=big_a_t, e_a_x=e_a_x, cd_x=cd_x, xs=xs,
                           bs=xbc_ref[b, :, D_B:D_B + N_GROUPS * N_STATE], cs=xbc_ref[b, :, D_B + N_GROUPS * N_STATE:N_XBC],
                           xdt_b=xdt.astype(BF16), st_prev=st_prev, st_b=st_prev.astype(BF16),
                           u_b=(xdt * d_s_x).astype(BF16), bz=bz_ref[b].astype(F32), y_parts=[], new_st=[]))
        for g in range(N_GROUPS):
            gs = slice(g * N_STATE, (g + 1) * N_STATE)
            gw = slice(g * GROUP_W, (g + 1) * GROUP_W)
            for q in ex:
                cb = lax.dot_general(q["cs"][:, gs], q["bs"][:, gs], NT_DIMS, preferred_element_type=F32)
                q["y_off"] = jnp.dot(q["cs"][:, gs], q["st_b"][:, gw], preferred_element_type=F32)
                q["cb"] = cb
                q["diag"] = []
            for e in range(N_HEADS // N_GROUPS):
                h = g * (N_HEADS // N_GROUPS) + e
                for q in ex:
                    m = (q["cb"] * _decay(q["big_a"], q["big_a_t"], h, False)).astype(BF16)
                    q["diag"].append(jnp.dot(m, q["xdt_b"][:, h * HEAD_DIM:(h + 1) * HEAD_DIM], preferred_element_type=F32))
            for q in ex:
                q["y_parts"].append(jnp.concatenate(q["diag"], axis=1) + q["y_off"] * q["e_a_x"][:, gw])
                upd = lax.dot_general(q["bs"][:, gs], q["u_b"][:, gw], TN_DIMS, preferred_element_type=F32)
                q["new_st"].append(q["st_prev"][:, gw] * q["cd_x"][:, gw] + upd)
        for b, q in enumerate(ex):
            y = jnp.concatenate(q["y_parts"], axis=1) + q["xs"] * dx_ref[...]
            z = y * _silu(q["bz"])
            r = lax.rsqrt(jnp.mean(z * z, axis=-1, keepdims=True) + NORM_EPS)
            q["y"], q["yb"] = y, (z * r * g_ref[...]).astype(BF16)
        for b, q in enumerate(ex):
            st_ref[b] = q["st_b"]
            s_scr[b] = jnp.concatenate(q["new_st"], axis=1)
            ys_ref[b] = q["y"].astype(BF16)
            yb_ref[b] = q["yb"]

    def row(width, col):
        return pl.BlockSpec((nb, CHUNK, width), lambda c: (0, c, col))

    def const(shape):
        return pl.BlockSpec(shape, lambda c: (0,) * len(shape))

    proj3 = proj.reshape(nb, lp, N_INP)
    ybuf, yssd, states, *rest = _grid_call(
        body, name, (cpe,),
        [row(N_XBC, 0), row(D_B, COL_BZ // D_B), row(LANE, COL_DT // LANE),
         const((1, LANE)), const((N_HEADS, 1)), const((1, LANE)), const((N_HEADS, 1)),
         const((1, D_B)), const((1, D_B)), const((LANE, D_B)), HBM_SPEC],
        [row(D_B, 0), row(D_B, 0), pl.BlockSpec((nb, None, N_STATE, D_B), lambda c: (0, c, 0, 0))],
        [jax.ShapeDtypeStruct((nb, lp, ybuf.shape[1]), ybuf.dtype), jax.ShapeDtypeStruct((nb, lp, D_B), BF16),
         jax.ShapeDtypeStruct((nb, cpe, N_STATE, D_B), BF16)],
        [pltpu.VMEM((nb, N_STATE, D_B), F32)],
        (xbc.reshape(nb, lp, N_XBC), proj3, proj3, dtb_p, dtb_t, alog_p, alog_t, d_skip_x, norm_g, e_mat,
         ybuf.reshape(nb, lp, ybuf.shape[1])), comm, aliases={10: 0})
    return (ybuf.reshape(t, -1), yssd.reshape(t, D_B), states, *rest)


def _ssd_bwd(dy, y_ssd, xbc, proj, states, dtb, a_log, d_skip_x, norm_g, e_mat, e_mat_t, lp, name, comm=None):
    t = xbc.shape[0]
    cpe = lp // CHUNK
    nb = t // lp
    hpg = N_HEADS // N_GROUPS
    dtb_p = jnp.pad(dtb.reshape(1, N_HEADS), ((0, 0), (0, LANE - N_HEADS)))
    alog_p = jnp.pad(a_log.reshape(1, N_HEADS), ((0, 0), (0, LANE - N_HEADS)))
    dtb_t = dtb.reshape(N_HEADS, 1)
    alog_t = a_log.reshape(N_HEADS, 1)

    def body(dy_ref, ys_ref, xbc_ref, bz_ref, dt_ref, st_ref, dtb_ref, dtbt_ref, al_ref, alt_ref, dx_ref, g_ref,
             e_ref, et_ref, dxbc_ref, dpw_ref, dg_ref, ddtb_ref, dal_ref, dd_ref, ds_scr):
        @pl.when(pl.program_id(0) == 0)
        def _():
            ds_scr[...] = jnp.zeros_like(ds_scr)
            dg_ref[...] = jnp.zeros_like(dg_ref)
            ddtb_ref[...] = jnp.zeros_like(ddtb_ref)
            dal_ref[...] = jnp.zeros_like(dal_ref)
            dd_ref[...] = jnp.zeros_like(dd_ref)

        stores, sums = [], {}
        chains = [one_example(dy_ref.at[b], ys_ref.at[b], xbc_ref.at[b], bz_ref.at[b], dt_ref.at[b], st_ref.at[b], dtb_ref,
                              dtbt_ref, al_ref, alt_ref, dx_ref, g_ref, e_ref, et_ref, dxbc_ref.at[b], dpw_ref.at[b],
                              ds_scr.at[b], stores, sums) for b in range(nb)]
        live = list(chains)
        while live:
            live = [ch for ch in live if next(ch, "done") != "done"]
        for ref, idx, val in stores:
            ref[idx] = val
        for ref, name_ in ((dg_ref, "dg"), (ddtb_ref, "ddtb"), (dal_ref, "dal"), (dd_ref, "dd")):
            total = sums[name_][0]
            for part in sums[name_][1:]:
                total = total + part
            ref[...] += total

    def one_example(dy_ref, ys_ref, xbc_ref, bz_ref, dt_ref, st_ref, dtb_ref, dtbt_ref, al_ref, alt_ref, dx_ref, g_ref,
                    e_ref, et_ref, dxbc_ref, dpw_ref, ds_scr, stores, sums):
        everything = (slice(None), slice(None))
        cc = cpe - 1 - pl.program_id(0)
        rows = lax.broadcasted_iota(jnp.int32, (CHUNK, 1), 0)
        cols = lax.broadcasted_iota(jnp.int32, (1, CHUNK), 1)
        valid_col = jnp.logical_or(cc > 0, rows >= PAD)
        valid_row = jnp.logical_or(cc > 0, cols >= PAD)
        e_mat = e_ref[...]
        e_mat_t = et_ref[...]
        dtraw = dt_ref[...].astype(F32)
        a, dt, big_a, big_a_t, e_a, d_s, dt_x, e_a_x, d_s_x, cd_x = _ssd_common(
            dtraw, dtb_ref[...], dtbt_ref[...], al_ref[...], alt_ref[...], e_mat, valid_col, valid_row, x_terms=1)
        xs = xbc_ref[:, 0:D_B].astype(F32)
        bs = xbc_ref[:, D_B:D_B + N_GROUPS * N_STATE]
        cs = xbc_ref[:, D_B + N_GROUPS * N_STATE:N_XBC]
        xdt = xs * dt_x
        xdt_b = xdt.astype(BF16)
        st_b = st_ref[...]
        dst = ds_scr[...]
        dst_b = dst.astype(BF16)

        ys = ys_ref[...].astype(F32)
        bz = bz_ref[...].astype(F32)
        sil = _silu(bz)
        z = ys * sil
        dz, dgt = _rms_bwd_math(z, g_ref[...], dy_ref[...].astype(F32))
        sums.setdefault("dg", []).append(jnp.sum(dgt, axis=0, keepdims=True))
        stores.append((dpw_ref, (slice(None), slice(0, D_B)), (dz * ys * _dsilu(bz)).astype(BF16)))
        dys = dz * sil

        dd_lane = jnp.sum(dys * xs, axis=0, keepdims=True)
        dxs = dys * dx_ref[...]
        w_x = dys * e_a_x
        w_b = w_x.astype(BF16)
        dys_b = dys.astype(BF16)
        u_b = (xdt * d_s_x).astype(BF16)
        dxdt_parts, dbs_parts, dcs_parts, off_parts, g1_parts = [], [], [], [], []
        da_diag = jnp.zeros((CHUNK, LANE), F32)
        lane = lax.broadcasted_iota(jnp.int32, (1, LANE), 1)
        for g in range(N_GROUPS):
            gs = slice(g * N_STATE, (g + 1) * N_STATE)
            gw = slice(g * GROUP_W, (g + 1) * GROUP_W)
            cs_g, bs_g = cs[:, gs], bs[:, gs]
            dcs = lax.dot_general(w_b[:, gw], st_b[:, gw], NT_DIMS, preferred_element_type=F32)
            y_off = jnp.dot(cs_g, st_b[:, gw], preferred_element_type=F32)
            off_parts.append(y_off)
            dst_new = lax.dot_general(cs_g, w_b[:, gw], TN_DIMS, preferred_element_type=F32)
            g1 = jnp.dot(bs_g, dst_b[:, gw], preferred_element_type=F32)
            g1_parts.append(g1)
            dbs = lax.dot_general(u_b[:, gw], dst_b[:, gw], NT_DIMS, preferred_element_type=F32)
            cb = lax.dot_general(cs_g, bs_g, NT_DIMS, preferred_element_type=F32)
            cbt = lax.dot_general(bs_g, cs_g, NT_DIMS, preferred_element_type=F32)
            dcb = jnp.zeros((CHUNK, CHUNK), F32)
            dcbt = jnp.zeros((CHUNK, CHUNK), F32)
            dxdt_h = []
            for e in range(hpg):
                h = g * hpg + e
                hs = slice(h * HEAD_DIM, (h + 1) * HEAD_DIM)
                dec = _decay(big_a, big_a_t, h, False)
                dect = _decay(big_a, big_a_t, h, True)
                m = cb * dec
                mt = cbt * dect
                dxdt_h.append(jnp.dot(mt.astype(BF16), dys_b[:, hs], preferred_element_type=F32))
                dm = lax.dot_general(dys_b[:, hs], xdt_b[:, hs], NT_DIMS, preferred_element_type=F32)
                dmt = lax.dot_general(xdt_b[:, hs], dys_b[:, hs], NT_DIMS, preferred_element_type=F32)
                dcb = dcb + dm * dec
                dcbt = dcbt + dmt * dect
                da_h = jnp.sum(dm * m - dmt * mt, axis=1, keepdims=True)
                da_diag = da_diag + jnp.where(lane == h, da_h, 0.0)
                yield
            dcs = dcs + jnp.dot(dcb.astype(BF16), bs_g, preferred_element_type=F32)
            dbs = dbs + jnp.dot(dcbt.astype(BF16), cs_g, preferred_element_type=F32)
            dxdt_parts.append(jnp.concatenate(dxdt_h, axis=1) + g1 * d_s_x[:, gw])
            dbs_parts.append(dbs)
            dcs_parts.append(dcs)
            stores.append((ds_scr, (slice(None), gw), dst[:, gw] * cd_x[:, gw] + dst_new))
            yield
        dxdt = jnp.concatenate(dxdt_parts, axis=1)
        y_off = jnp.concatenate(off_parts, axis=1)
        g1 = jnp.concatenate(g1_parts, axis=1)
        dcd_lane = jnp.sum(dst * st_b.astype(F32), axis=0, keepdims=True)
        vecs = jnp.concatenate([jnp.broadcast_to(dcd_lane, (SUB, D_B)), jnp.broadcast_to(dd_lane, (SUB, D_B))], axis=0)
        dds, da_off, ddt_x, vec_sums = _split_dot_many(
            [g1 * xdt, w_x * y_off, dxdt * xs, vecs], [1, 1, 1, 1], e_mat_t)
        dcd = vec_sums[0:1]
        sums.setdefault("dd", []).append(vec_sums[SUB:SUB + 1])
        yield
        t_ds = dds * d_s
        d_a = da_diag + da_off - t_ds
        last_row = jnp.sum(t_ds, axis=0, keepdims=True) + dcd * e_a[CHUNK - 1:CHUNK, :]
        d_a = d_a + jnp.where(rows == CHUNK - 1, last_row, 0.0)
        dda = _split_dot_left(_tri(False).astype(BF16), d_a, 3)
        ddt = dda * a + ddt_x
        sums.setdefault("dal", []).append(jnp.sum(dda * dt * a, axis=0, keepdims=True))
        ddtraw = jnp.where(valid_col, ddt * jax.nn.sigmoid(dtraw + dtb_ref[...]), 0.0)
        ddtraw = jnp.where(lane < N_HEADS, ddtraw, 0.0)
        sums.setdefault("ddtb", []).append(jnp.sum(ddtraw, axis=0, keepdims=True))
        stores.append((dpw_ref, (slice(None), slice(D_B, D_B + LANE)), ddtraw.astype(BF16)))
        dxs = dxs + dxdt * dt_x
        dxbc = jnp.concatenate([dxs] + dbs_parts + dcs_parts, axis=1)
        stores.append((dxbc_ref, everything, jnp.where(valid_col, dxbc, 0.0).astype(BF16)))

    def row(width, col):
        return pl.BlockSpec((nb, CHUNK, width), lambda c: (0, cpe - 1 - c, col))

    def const(shape):
        return pl.BlockSpec(shape, lambda c: (0,) * len(shape))

    proj3 = proj.reshape(nb, lp, N_INP)
    dxbc, dproj, *rest = _grid_call(
        body, name, (cpe,),
        [row(D_B, 0), row(D_B, 0), row(N_XBC, 0), row(D_B, COL_BZ // D_B), row(LANE, COL_DT // LANE),
         pl.BlockSpec((nb, None, N_STATE, D_B), lambda c: (0, cpe - 1 - c, 0, 0)),
         const((1, LANE)), const((N_HEADS, 1)), const((1, LANE)), const((N_HEADS, 1)),
         const((1, D_B)), const((1, D_B)), const((LANE, D_B)), const((D_B, LANE))],
        [row(N_XBC, 0),
         pl.BlockSpec((pl.Element(nb), pl.Element(CHUNK), pl.Element(D_B + LANE)),
                      lambda c: (0, (cpe - 1 - c) * CHUNK, COL_BZ)),
         const((1, D_B)), const((1, LANE)), const((1, LANE)), const((1, LANE))],
        [jax.ShapeDtypeStruct((nb, lp, N_XBC), BF16), jax.ShapeDtypeStruct((nb, lp, N_INP), BF16),
         jax.ShapeDtypeStruct((1, D_B), F32),
         jax.ShapeDtypeStruct((1, LANE), F32), jax.ShapeDtypeStruct((1, LANE), F32),
         jax.ShapeDtypeStruct((1, LANE), F32)],
        [pltpu.VMEM((nb, N_STATE, D_B), F32)],
        (dy.reshape(nb, lp, -1), y_ssd.reshape(nb, lp, D_B), xbc.reshape(nb, lp, N_XBC), proj3, proj3, states,
         dtb_p, dtb_t, alog_p, alog_t, d_skip_x, norm_g, e_mat, e_mat_t), comm)
    return (dxbc.reshape(t, N_XBC), dproj.reshape(t, N_INP), *rest)


HBM_SPEC = pl.BlockSpec(memory_space=pl.ANY)


def _mesh_pos():
    return lax.axis_index("x"), lax.axis_index("y"), lax.axis_index("c")


def _allgather(arrays, name):
    n = len(arrays)

    def body(*refs):
        xs, outs = refs[:n], refs[n:2 * n]
        send_sems, recv_sems, local_sems = refs[2 * n:]
        x, y, c = _mesh_pos()
        me, sibling = (x, y, c), (x, y, 1 - c)
        chips = [(1 - x, y), (x, 1 - y), (1 - x, 1 - y)]

        def slot(px, py, pc):
            return 4 * px + 2 * py + pc

        def copy(a, k, block, to, src=None):
            dst = outs[a].at[slot(*block)]
            return pltpu.make_async_remote_copy(
                src_ref=dst if src is None else src, dst_ref=dst, send_sem=send_sems.at[a, k], recv_sem=recv_sems.at[a, k],
                device_id=to, device_id_type=MESH)

        mine = [pltpu.make_async_copy(xs[a], outs[a].at[slot(*me)], local_sems.at[a]) for a in range(n)]
        for cp in mine:
            cp.start()
        first = []
        for a in range(n):
            first.append(copy(a, 0, me, sibling, src=xs[a]))
            first += [copy(a, 1 + j, me, (*chip, c), src=xs[a]) for j, chip in enumerate(chips)]
        for cp in first:
            cp.start()
        passed = []
        for j, chip in enumerate(chips):
            for a in range(n):
                copy(a, 1 + j, (*chip, c), me).wait_recv()
                cp = copy(a, 4 + j, (*chip, c), sibling)
                cp.start()
                passed.append(cp)
        for a in range(n):
            copy(a, 0, sibling, me).wait_recv()
        for j, chip in enumerate(chips):
            for a in range(n):
                copy(a, 4 + j, (*chip, 1 - c), me).wait_recv()
        for cp in first + passed:
            cp.wait_send()
        for cp in mine:
            cp.wait()

    return pl.pallas_call(
        body, name=name,
        in_specs=[HBM_SPEC] * n, out_specs=[HBM_SPEC] * n,
        out_shape=[jax.ShapeDtypeStruct((N_DEV,) + a.shape, a.dtype) for a in arrays],
        scratch_shapes=[pltpu.SemaphoreType.DMA((n, 7)), pltpu.SemaphoreType.DMA((n, 7)), pltpu.SemaphoreType.DMA((n,))],
    )(*arrays)


def _remote(src, dst, send_sem, recv_sem, to):
    return pltpu.make_async_remote_copy(src_ref=src, dst_ref=dst, send_sem=send_sem, recv_sem=recv_sem,
                                        device_id=to, device_id_type=MESH)


def _slot_plain(d):
    return d


def _slot_mix_rows(d):
    return jnp.where(d < 2, d + 4, jnp.where(d < 6, d - 2, d))


def _ag_ici(pieces, slots):
    n = len(pieces)

    def copies(ins, outs, sems, with_recvs):
        send_sems, recv_sems, local_sems = sems
        x, y, c = _mesh_pos()
        local, sends, recvs = [], [], []
        for a in range(n):
            mine = outs[a].at[slots[a](4 * x + 2 * y + c)]
            local.append(pltpu.make_async_copy(ins[a], mine, local_sems.at[a]))
            for k, (px, py) in enumerate([(1 - x, y), (x, 1 - y), (1 - x, 1 - y)]):
                sends.append(_remote(ins[a], mine, send_sems.at[a, k], recv_sems.at[a, k], (px, py, c)))
                if with_recvs:
                    theirs = outs[a].at[slots[a](4 * px + 2 * py + c)]
                    recvs.append(_remote(ins[a], theirs, send_sems.at[a, k], recv_sems.at[a, k], (px, py, c)))
        return local, sends, recvs

    return _Comm(pieces, [jax.ShapeDtypeStruct((N_DEV,) + p.shape, p.dtype) for p in pieces],
                 [pltpu.SemaphoreType.DMA((n, 3)), pltpu.SemaphoreType.DMA((n, 3)), pltpu.SemaphoreType.DMA((n,))], copies)


def _ag_d2d(bufs, slots):
    n = len(bufs)

    def copies(ins, outs, sems, with_recvs):
        send_sems, recv_sems = sems
        x, y, c = _mesh_pos()
        chips = [(x, y), (1 - x, y), (x, 1 - y), (1 - x, 1 - y)]
        sends, recvs = [], []
        for a in range(n):
            for k, (px, py) in enumerate(chips):
                held = slots[a](4 * px + 2 * py + c)
                sends.append(_remote(ins[a].at[held], outs[a].at[held], send_sems.at[a, k], recv_sems.at[a, k], (x, y, 1 - c)))
                if with_recvs:
                    got = slots[a](4 * px + 2 * py + 1 - c)
                    recvs.append(_remote(ins[a].at[got], outs[a].at[got], send_sems.at[a, k], recv_sems.at[a, k], (x, y, 1 - c)))
        return [], sends, recvs

    return _Comm(bufs, [jax.ShapeDtypeStruct(b.shape, b.dtype) for b in bufs],
                 [pltpu.SemaphoreType.DMA((n, 4)), pltpu.SemaphoreType.DMA((n, 4))], copies,
                 aliases={a: a for a in range(n)})


def _rs_d2d(blocks, slots):
    n = len(blocks)

    def copies(ins, outs, sems, with_recvs):
        send_sems, recv_sems = sems
        x, y, c = _mesh_pos()
        sends, recvs = [], []
        for a in range(n):
            for j in range(4):
                src = ins[a].at[slots[a](2 * j + 1 - c)]
                sends.append(_remote(src, outs[a].at[j], send_sems.at[a, j], recv_sems.at[a, j], (x, y, 1 - c)))
                if with_recvs:
                    recvs.append(_remote(src, outs[a].at[j], send_sems.at[a, j], recv_sems.at[a, j], (x, y, 1 - c)))
        return [], sends, recvs

    return _Comm(blocks, [jax.ShapeDtypeStruct((4,) + b.shape[1:], b.dtype) for b in blocks],
                 [pltpu.SemaphoreType.DMA((n, 4)), pltpu.SemaphoreType.DMA((n, 4))], copies)


def _rs_ici(blocks):
    n = len(blocks)

    def copies(ins, outs, sems, with_recvs):
        send_sems, recv_sems = sems
        x, y, c = _mesh_pos()
        sends, recvs = [], []
        for a in range(n):
            for k, (px, py) in enumerate([(1 - x, y), (x, 1 - y), (1 - x, 1 - y)]):
                src = ins[a].at[2 * px + py]
                sends.append(_remote(src, outs[a].at[k], send_sems.at[a, k], recv_sems.at[a, k], (px, py, c)))
                if with_recvs:
                    recvs.append(_remote(src, outs[a].at[k], send_sems.at[a, k], recv_sems.at[a, k], (px, py, c)))
        return [], sends, recvs

    return _Comm(blocks, [jax.ShapeDtypeStruct((3,) + b.shape[1:], b.dtype) for b in blocks],
                 [pltpu.SemaphoreType.DMA((n, 3)), pltpu.SemaphoreType.DMA((n, 3))], copies)


def _run_comm(comm, name):
    n_in, n_out = len(comm.inputs), len(comm.out_shapes)

    def body(*refs):
        ins, outs, sems = refs[:n_in], refs[n_in:n_in + n_out], refs[n_in + n_out:]
        comm.start(ins, outs, sems)
        comm.wait(ins, outs, sems)

    return pl.pallas_call(
        body, name=name, in_specs=[HBM_SPEC] * n_in, out_specs=[HBM_SPEC] * n_out, out_shape=comm.out_shapes,
        scratch_shapes=comm.sem_shapes, input_output_aliases=comm.aliases,
    )(*comm.inputs)


W_ROWS = 784


def _w_segments():
    per = N_IN // N_DEV
    out = []
    for d in range(N_DEV):
        lo, hi = per * d, per * (d + 1)
        for a, b, start in COL_MAP:
            s, e = max(lo, a), min(hi, b)
            if s < e:
                out.append((d, s - lo, e - s, s - a + start))
    return out


def _w_gather_rows(g, name):
    tk = D_MODEL // 2
    u32 = jnp.uint32

    def body(g_ref, o_ref, scr):
        for d in range(N_DEV):
            x32 = pltpu.bitcast(g_ref[d], u32)
            for dd, src, rows, dst in _w_segments():
                if dd == d:
                    scr[dst // 2:(dst + rows) // 2, :] = x32[src // 2:(src + rows) // 2]
        scr[(COL_DT + N_HEADS) // 2:COL_XBC // 2, :] = jnp.zeros(((COL_XBC - COL_DT - N_HEADS) // 2, tk), u32)
        o_ref[...] = pltpu.bitcast(scr[...], BF16)

    return pl.pallas_call(
        body, name=name, grid=(D_MODEL // tk,),
        in_specs=[pl.BlockSpec((N_DEV, W_ROWS, tk), lambda j: (0, 0, j))],
        out_specs=pl.BlockSpec((N_INP, tk), lambda j: (0, j)),
        out_shape=jax.ShapeDtypeStruct((N_INP, D_MODEL), BF16),
        scratch_shapes=[pltpu.VMEM((N_INP // 2, tk), u32)],
        compiler_params=_cparams(("parallel",)),
    )(g)


def _w_split_rows(dwt, name):
    tk = D_MODEL // 4
    per = N_IN // N_DEV

    def body(w_ref, o_ref):
        for d, dst, rows, src in _w_segments():
            o_ref[d, dst // 2:(dst + rows) // 2, :] = pltpu.bitcast(w_ref[src:src + rows, :].astype(BF16), F32)
        for d in range(N_DEV):
            o_ref[d, per // 2:W_ROWS // 2, :] = jnp.zeros(((W_ROWS - per) // 2, tk), F32)

    return pl.pallas_call(
        body, name=name, grid=(D_MODEL // tk,),
        in_specs=[pl.BlockSpec((N_INP, tk), lambda j: (0, j))],
        out_specs=pl.BlockSpec((N_DEV, W_ROWS // 2, tk), lambda j: (0, 0, j)),
        out_shape=jax.ShapeDtypeStruct((N_DEV, W_ROWS // 2, D_MODEL), F32),
        compiler_params=_cparams(("parallel",)),
    )(dwt)


def _rs_pair_sum(g, ra, own_slots, name):
    packed = g.dtype == F32
    _, rows, cols = g.shape
    out_rows = 2 * rows if packed else rows

    def body(s_ref, g_ref, ra_ref, o_ref):
        a, b = g_ref[...], ra_ref[...]
        if packed:
            a, b = pltpu.bitcast(a, BF16), pltpu.bitcast(b, BF16)
        o_ref[...] = (a.astype(F32) + b.astype(F32)).astype(BF16)

    return pl.pallas_call(
        body, name=name,
        grid_spec=pltpu.PrefetchScalarGridSpec(
            num_scalar_prefetch=1, grid=(4,),
            in_specs=[pl.BlockSpec((None, rows, cols), lambda j, s: (s[j], 0, 0)),
                      pl.BlockSpec((None, rows, cols), lambda j, s: (j, 0, 0))],
            out_specs=pl.BlockSpec((None, out_rows, cols), lambda j, s: (j, 0, 0))),
        out_shape=jax.ShapeDtypeStruct((4, out_rows, cols), BF16),
        compiler_params=_cparams(("parallel",)),
    )(own_slots, g, ra)


def _rs_final_sum(h, rb, chip_idx, name):
    _, rows, cols = h.shape

    def body(j_ref, h_ref, rb_ref, o_ref):
        o_ref[...] = ((h_ref[...].astype(F32) + rb_ref[0].astype(F32)) + rb_ref[1].astype(F32)) + rb_ref[2].astype(F32)

    return pl.pallas_call(
        body, name=name,
        grid_spec=pltpu.PrefetchScalarGridSpec(
            num_scalar_prefetch=1, grid=(1,),
            in_specs=[pl.BlockSpec((None, rows, cols), lambda i, j: (j[0], 0, 0)),
                      pl.BlockSpec((3, rows, cols), lambda i, j: (0, 0, 0))],
            out_specs=pl.BlockSpec((rows, cols), lambda i, j: (0, 0))),
        out_shape=jax.ShapeDtypeStruct((rows, cols), F32),
        compiler_params=_cparams(("arbitrary",)),
    )(chip_idx, h, rb)


def _adamw_math(w, g, m, v):
    m = ADAM_B1 * m + (1.0 - ADAM_B1) * g
    v = ADAM_B2 * v + (1.0 - ADAM_B2) * (g * g)
    m_hat = m / (1.0 - ADAM_B1 ** ADAM_STEP)
    v_hat = v / (1.0 - ADAM_B2 ** ADAM_STEP)
    delta = -ADAM_LR * (m_hat / (jnp.sqrt(v_hat) + ADAM_EPS) + ADAM_WD * w)
    return delta, m, v


def _adamw_rows(g, w, m, v, tr, name):
    layers, rows, cols = w.shape

    def body(g_ref, w_ref, m_ref, v_ref, d_out, m_out, v_out):
        delta, m_new, v_new = _adamw_math(w_ref[...], g_ref[...], m_ref[...], v_ref[...])
        d_out[...] = delta
        m_out[...] = m_new
        v_out[...] = v_new

    blk = pl.BlockSpec((None, tr, cols), lambda a, r: (a, r, 0))
    return pl.pallas_call(
        body, name=name, grid=(layers, rows // tr),
        in_specs=[blk] * 4, out_specs=[blk] * 3,
        out_shape=[jax.ShapeDtypeStruct((layers, rows, cols), F32)] * 3,
        compiler_params=_cparams(("parallel", "parallel")),
    )(g, w, m, v)


def _adamw_cols(g_t, w, m, v, name):
    layers, k, cols = w.shape
    tr = 56
    assert g_t.shape[1] % tr == 0
    views = [jnp.transpose(a, (2, 0, 1)) for a in (w, m, v)]

    def body(g_ref, w_ref, m_ref, v_ref, g_out, d_out, m_out, v_out):
        for a in range(layers):
            g = g_ref[a]
            delta, m_new, v_new = _adamw_math(w_ref[:, a, :], g, m_ref[:, a, :], v_ref[:, a, :])
            g_out[:, a, :] = g
            d_out[:, a, :] = delta
            m_out[:, a, :] = m_new
            v_out[:, a, :] = v_new

    col = pl.BlockSpec((tr, layers, k), lambda r: (r, 0, 0))
    outs = pl.pallas_call(
        body, name=name, grid=(pl.cdiv(cols, tr),),
        in_specs=[pl.BlockSpec((layers, tr, k), lambda r: (0, r, 0)), col, col, col], out_specs=[col] * 4,
        out_shape=[jax.ShapeDtypeStruct((cols, layers, k), F32)] * 4,
        compiler_params=_cparams(("parallel",)),
    )(g_t, *views)
    return [jnp.transpose(o, (1, 2, 0)) for o in outs]


def _sum_devices(parts, name):
    _, p, _ = parts.shape

    def body(x_ref, o_ref):
        acc = x_ref[0]
        for d in range(1, N_DEV):
            acc = acc + x_ref[d]
        o_ref[...] = acc

    return pl.pallas_call(
        body, name=name, grid=(1,),
        in_specs=[pl.BlockSpec((N_DEV, p, LANE), lambda i: (0, 0, 0))],
        out_specs=pl.BlockSpec((p, LANE), lambda i: (0, 0)),
        out_shape=jax.ShapeDtypeStruct((p, LANE), F32),
        compiler_params=_cparams(("arbitrary",)),
    )(parts)


def _adamw_small(g, w, m, v, name):
    p = g.shape[0]

    def body(g_ref, w_ref, m_ref, v_ref, d_out, m_out, v_out):
        delta, m_new, v_new = _adamw_math(w_ref[...], g_ref[...], m_ref[...], v_ref[...])
        d_out[...] = delta
        m_out[...] = m_new
        v_out[...] = v_new

    spec = pl.BlockSpec((p, LANE), lambda i: (0, 0))
    return pl.pallas_call(
        body, name=name, grid=(1,),
        in_specs=[spec] * 4, out_specs=[spec] * 3,
        out_shape=[jax.ShapeDtypeStruct((p, LANE), F32)] * 3,
        compiler_params=_cparams(("arbitrary",)),
    )(g, w, m, v)


PACK_ALIGN = SUB * LANE

SMALL_PARAMS = (
    ("meta", (N_META, D_MODEL), 1),
    ("pre_g", (DEPTH, D_MODEL), None),
    ("post_g", (DEPTH, D_MODEL), None),
    ("conv_a_w", (DEPTH, CONV_A_K, D_A), 2),
    ("ssm_conv_w", (DEPTH, SSM_K, N_XBC), 2),
    ("ssm_conv_b", (DEPTH, N_XBC), None),
    ("dt_bias", (DEPTH, N_HEADS), None),
    ("a_log", (DEPTH, N_HEADS), None),
    ("d_skip", (DEPTH, N_HEADS), None),
    ("ssm_norm_g", (DEPTH, D_B), None),
    ("conf_conv_w", (DEPTH, CONF_K, D_C), 2),
    ("conf_conv_b", (DEPTH, D_C), None),
    ("conf_ln_g", (DEPTH, D_C), None),
    ("conf_ln_b", (DEPTH, D_C), None),
)


def _local_shape(shape, axis):
    if axis is None:
        return shape
    return tuple(s // N_DEV if k == axis else s for k, s in enumerate(shape))


def _pack(arrays):
    flat = []
    for a in arrays:
        v = a.reshape(-1).astype(F32)
        flat.append(v)
        if v.shape[0] % PACK_ALIGN:
            flat.append(jnp.zeros(((-v.shape[0]) % PACK_ALIGN,), F32))
    return jnp.concatenate(flat).reshape(-1, LANE)


def _unpack(buf, shapes):
    flat = buf.reshape(-1)
    out, off = [], 0
    for s in shapes:
        size = 1
        for k in s:
            size *= k
        out.append(flat[off:off + size].reshape(s))
        off += size + (-size) % PACK_ALIGN
    return out


def kernel(x, meta, pre_g, post_g, w_in, w_out, conv_a_w, ssm_conv_w, ssm_conv_b, dt_bias, a_log, d_skip, ssm_norm_g, conf_conv_w, conf_conv_b, conf_ln_g, conf_ln_b, loss_target, m_meta, m_pre_g, m_post_g, m_w_in, m_w_out, m_conv_a_w, m_ssm_conv_w, m_ssm_conv_b, m_dt_bias, m_a_log, m_d_skip, m_ssm_norm_g, m_conf_conv_w, m_conf_conv_b, m_conf_ln_g, m_conf_ln_b, v_meta, v_pre_g, v_post_g, v_w_in, v_w_out, v_conv_a_w, v_ssm_conv_w, v_ssm_conv_b, v_dt_bias, v_a_log, v_d_skip, v_ssm_norm_g, v_conf_conv_w, v_conf_conv_b, v_conf_ln_g, v_conf_ln_b):
    weights = dict(meta=meta, pre_g=pre_g, post_g=post_g, conv_a_w=conv_a_w, ssm_conv_w=ssm_conv_w, ssm_conv_b=ssm_conv_b,
                   dt_bias=dt_bias, a_log=a_log, d_skip=d_skip, ssm_norm_g=ssm_norm_g, conf_conv_w=conf_conv_w,
                   conf_conv_b=conf_conv_b, conf_ln_g=conf_ln_g, conf_ln_b=conf_ln_b)
    mom1 = dict(meta=m_meta, pre_g=m_pre_g, post_g=m_post_g, conv_a_w=m_conv_a_w, ssm_conv_w=m_ssm_conv_w,
                ssm_conv_b=m_ssm_conv_b, dt_bias=m_dt_bias, a_log=m_a_log, d_skip=m_d_skip, ssm_norm_g=m_ssm_norm_g,
                conf_conv_w=m_conf_conv_w, conf_conv_b=m_conf_conv_b, conf_ln_g=m_conf_ln_g, conf_ln_b=m_conf_ln_b)
    mom2 = dict(meta=v_meta, pre_g=v_pre_g, post_g=v_post_g, conv_a_w=v_conv_a_w, ssm_conv_w=v_ssm_conv_w,
                ssm_conv_b=v_ssm_conv_b, dt_bias=v_dt_bias, a_log=v_a_log, d_skip=v_d_skip, ssm_norm_g=v_ssm_norm_g,
                conf_conv_w=v_conf_conv_w, conf_conv_b=v_conf_conv_b, conf_ln_g=v_conf_ln_g, conf_ln_b=v_conf_ln_b)
    nb, seq, d = x.shape
    lp = PAD + N_META + seq
    t = nb * lp
    assert lp % TILE == 0 and t % (3 * LANE) == 0 and d == D_MODEL
    rt = lp // 3
    xi, yi, ci = _mesh_pos()
    dev = 4 * xi + 2 * yi + ci
    ci32 = ci.astype(jnp.int32)
    chip_idx = (2 * xi + yi).astype(jnp.int32).reshape(1)
    own_plain = jnp.stack([2 * j + ci32 for j in range(4)])
    own_mix = jnp.stack([_slot_mix_rows(2 * j + ci32) for j in range(4)]).astype(jnp.int32)
    n_in_loc = N_IN // N_DEV
    n_out_loc = 2 * D_MODEL // N_DEV
    slots = (_slot_plain, _slot_mix_rows)

    sharded_small = [n for n, _, ax in SMALL_PARAMS if ax is not None]
    small_shapes = {n: s for n, s, _ in SMALL_PARAMS}
    small_axis = {n: ax for n, _, ax in SMALL_PARAMS}
    sw_pack = _pack([weights[n] for n in sharded_small])
    (sw_g,) = _allgather([sw_pack], "ag_small_weights")
    wt_loc = jnp.pad(jnp.swapaxes(w_in, 1, 2).astype(BF16), ((0, 0), (0, W_ROWS - n_in_loc), (0, 0)))
    wo_loc = w_out.astype(BF16)

    def gathered(bufs):
        return _w_gather_rows(bufs[0], "w_gather_rows"), bufs[1].reshape(2 * D_MODEL, D_MODEL)

    full = dict(weights)
    per_dev = [_unpack(sw_g[k], [_local_shape(small_shapes[n], small_axis[n]) for n in sharded_small]) for k in range(N_DEV)]
    for q, n in enumerate(sharded_small):
        full[n] = jnp.concatenate([per_dev[k][q] for k in range(N_DEV)], axis=small_axis[n])

    e_mat = (lax.broadcasted_iota(jnp.int32, (LANE, D_B), 0) == lax.broadcasted_iota(jnp.int32, (LANE, D_B), 1) // HEAD_DIM)
    e_mat = e_mat.astype(BF16)
    e_mat_t = e_mat.T

    front = jnp.concatenate([jnp.zeros((PAD, d), F32), full["meta"]], axis=0)
    h = jnp.concatenate([jnp.concatenate([front, x[b]], axis=0) for b in range(nb)], axis=0)
    saved = []
    w_t, w_o = gathered(_run_comm(_ag_d2d(_run_comm(_ag_ici([wt_loc[0], wo_loc[0]], slots), "ag_ici"), slots), "ag_d2d"))
    hn = _rms_fwd(h, pre_g[0].reshape(1, -1), rt, "rms_fwd")
    for i in range(DEPTH):
        row = lambda a: a[i].reshape(1, -1)
        proj = _mm(hn, w_t, "nt", BF16, t // 3, MM_TN, D_MODEL, "mm_proj", cols_outer=True)
        ycat = _a_fwd(proj, full["conv_a_w"][i], lp, "a_fwd")
        xbc = _xbc_fwd(proj, full["ssm_conv_w"][i], row(ssm_conv_b), lp, "xbc_fwd")
        ycat, u1 = _c_fwd(proj, full["conf_conv_w"][i], row(conf_conv_b), row(conf_ln_g), row(conf_ln_b), ycat, lp, "c_fwd")
        d_skip_x = jnp.repeat(d_skip[i], HEAD_DIM).reshape(1, D_B)
        nxt = _ag_ici([wt_loc[i + 1], wo_loc[i + 1]], slots) if i + 1 < DEPTH else None
        ycat, yssd, states, *bufs = _ssd_fwd(xbc, proj, dt_bias[i], a_log[i], d_skip_x, row(ssm_norm_g), e_mat, ycat, lp,
                                             "ssd_fwd", comm=nxt)
        if bufs:
            m, *bufs = _mm(ycat, w_o, "nn", F32, t // 3, D_MODEL, 2 * D_MODEL, "mm_out", comm=_ag_d2d(bufs, slots))
        else:
            m = _mm(ycat, w_o, "nn", F32, t // 3, D_MODEL, 2 * D_MODEL, "mm_out")
        saved.append((h, hn, proj, ycat, yssd, states, xbc, m, d_skip_x, w_t, w_o, u1))
        if i + 1 < DEPTH:
            h, hn = _post_rms_fwd(h, m, row(post_g), pre_g[i + 1].reshape(1, -1), rt, "post_rms_fwd")
        else:
            h = _post_fwd(h, m, row(post_g), rt, "post_fwd")
        if bufs:
            w_t, w_o = gathered(bufs)

    dh, loss_blk = _loss_kernel(h, loss_target.reshape(nb * seq, d), lp, "loss")

    grads = {n: [None] * DEPTH for n, _, _ in SMALL_PARAMS if n != "meta"}
    gt_in, gr_out = [None] * DEPTH, [None] * DEPTH
    dmeta = None
    pending = None

    def rs_pair_sums(blocks, ras):
        return [_rs_pair_sum(blocks[0], ras[0], own_plain, "rs_pair_sum_in"),
                _rs_pair_sum(blocks[1], ras[1], own_mix, "rs_pair_sum_out")]

    def rs_finish(layer, hs, rbs):
        gt_in[layer] = _rs_final_sum(hs[0], rbs[0], chip_idx, "rs_final_sum_in")
        gr_out[layer] = _rs_final_sum(hs[1], rbs[1], chip_idx, "rs_final_sum_out")

    for i in reversed(range(DEPTH)):
        row = lambda a: a[i].reshape(1, -1)
        h_i, hn, proj, ycat, yssd, states, xbc, m, d_skip_x, w_t, w_o, u1 = saved[i]
        dm, grads["post_g"][i] = _post_bwd(dh, m, row(post_g), rt, "post_bwd")
        if pending is not None:
            dy, *ras = _mm(dm, w_o, "nt", BF16, t // 3, D_MODEL, D_MODEL, "mm_dy", cols_outer=True,
                           comm=_rs_d2d(list(pending), slots))
            hs = rs_pair_sums(pending, ras)
        else:
            dy = _mm(dm, w_o, "nt", BF16, t // 3, D_MODEL, D_MODEL, "mm_dy", cols_outer=True)
            hs = None
        dw_out = _mm(ycat, dm, "tn", BF16, D_MODEL, D_MODEL, t // 3, "mm_dwout")
        dxbc, dproj, grads["ssm_norm_g"][i], ddtb, dal, dds, *rbs = _ssd_bwd(
            dy, yssd, xbc, proj, states, dt_bias[i], a_log[i], d_skip_x, row(ssm_norm_g), e_mat, e_mat_t, lp, "ssd_bwd",
            comm=_rs_ici(hs) if hs is not None else None)
        if hs is not None:
            rs_finish(i + 1, hs, rbs)
        grads["dt_bias"][i] = ddtb[:, :N_HEADS]
        grads["a_log"][i] = dal[:, :N_HEADS]
        grads["d_skip"][i] = dds[:, :N_HEADS]
        dproj, grads["conv_a_w"][i] = _a_bwd(dy, proj, full["conv_a_w"][i], dproj, lp, "a_bwd")
        dproj, grads["ssm_conv_w"][i], grads["ssm_conv_b"][i] = _xbc_bwd(
            dxbc, proj, full["ssm_conv_w"][i], row(ssm_conv_b), dproj, lp, "xbc_bwd")
        dproj, grads["conf_conv_w"][i], grads["conf_conv_b"][i], grads["conf_ln_g"][i], grads["conf_ln_b"][i] = _c_bwd(
            dy, proj, u1, full["conf_conv_w"][i], row(conf_conv_b), row(conf_ln_g), row(conf_ln_b), dproj, lp, "c_bwd")
        dhn = _mm(dproj, w_t, "nn", F32, MM_TM, D_MODEL, N_INP, "mm_dhn")
        dw_in_t = _mm(dproj, hn, "tn", F32, MM_TN, D_MODEL, t, "mm_dwin")
        dh, grads["pre_g"][i], dmeta = _rms_bwd(dh, dhn, h_i, row(pre_g), lp, rt, "rms_bwd")
        pending = (_w_split_rows(dw_in_t, "w_split_rows"), dw_out.reshape(N_DEV, n_out_loc, D_MODEL))
    grad_x = dh.reshape(nb, lp, d)[:, PAD + N_META:]
    hs = rs_pair_sums(pending, _run_comm(_rs_d2d(list(pending), slots), "rs_d2d"))
    rs_finish(0, hs, _run_comm(_rs_ici(hs), "rs_ici"))

    g_w_out = jnp.stack(gr_out)
    big = {"w_in": _adamw_cols(jnp.stack(gt_in), w_in, m_w_in, v_w_in, "adamw_w_in"),
           "w_out": [g_w_out, *_adamw_rows(g_w_out, w_out, m_w_out, v_w_out, n_out_loc, "adamw_w_out")]}

    names = [n for n, _, _ in SMALL_PARAMS]
    partial = [loss_blk[0:1, 0:1], dmeta] + [jnp.concatenate(grads[n], axis=0) for n in names[1:]]
    part_pack = _pack(partial)
    (parts_g,) = _allgather([part_pack], "ag_small_grads")
    total = _unpack(_sum_devices(parts_g, "sum_small_grads"), [(1,)] + [small_shapes[n] for n in names])
    loss = total[0][0]
    g_small = {}
    for n, g in zip(names, total[1:]):
        ax = small_axis[n]
        if ax is not None:
            g = lax.dynamic_slice_in_dim(g, dev * (small_shapes[n][ax] // N_DEV), small_shapes[n][ax] // N_DEV, axis=ax)
        g_small[n] = g
    loc_shapes = [_local_shape(small_shapes[n], small_axis[n]) for n in names]
    d_pack, m_pack, v_pack = _adamw_small(_pack([g_small[n] for n in names]), _pack([weights[n] for n in names]),
                                          _pack([mom1[n] for n in names]), _pack([mom2[n] for n in names]), "adamw_small")
    d_small = dict(zip(names, _unpack(d_pack, loc_shapes)))
    m_small = dict(zip(names, _unpack(m_pack, loc_shapes)))
    v_small = dict(zip(names, _unpack(v_pack, loc_shapes)))

    order = ["meta", "pre_g", "post_g", "w_in", "w_out", "conv_a_w", "ssm_conv_w", "ssm_conv_b", "dt_bias", "a_log",
             "d_skip", "ssm_norm_g", "conf_conv_w", "conf_conv_b", "conf_ln_g", "conf_ln_b"]

    def pick(k, small):
        return [big[n][k] if n in big else small[n] for n in order]

    return (loss, grad_x, *pick(0, g_small), *pick(1, d_small), *pick(2, m_small), *pick(3, v_small))
```

```python
import functools

import jax
import jax.numpy as jnp
from jax import lax
from jax.experimental import pallas as pl
from jax.experimental.pallas import tpu as pltpu

F32 = jnp.float32
BF16 = jnp.bfloat16

D_MODEL = 1024
DEPTH = 4
SEQ = 2048
CHUNK = 64
N_META = 16
PAD = 48
LP = PAD + N_META + SEQ
D_A = 512
D_B = 1024
D_C = 512
N_HEADS = 16
HEAD_DIM = 64
N_STATE = 128
N_GROUPS = 2
GROUP_W = D_B // N_GROUPS
N_XBC = D_B + 2 * N_GROUPS * N_STATE
CONV_A_K = 3
SSM_K = 4
CONF_K = 31
NORM_EPS = 1e-6
LN_EPS = 1e-5
N_IN = 6160
N_INP = 6272
COL_BZ = 2048
COL_DT = 3072
COL_XBC = 3200
COL_C = 4736
COL_MAP = ((0, 3072, 0), (3072, 4608, COL_XBC), (4608, 4624, COL_DT), (4624, 6160, COL_C))
LANE = 128
SUB = 8

ADAM_LR = 0.001
ADAM_B1 = 0.9
ADAM_B2 = 0.999
ADAM_EPS = 1e-08
ADAM_WD = 0.01
ADAM_STEP = 10

TILE = 192
HALO_A = 16
HALO_C = 32
MM_TM = 384
MM_TN = 896
VMEM_LIMIT = 56 * 1024 * 1024

MESH = pl.DeviceIdType.MESH
N_DEV = 8


def _silu(x):
    return x * jax.nn.sigmoid(x)


def _dsilu(x):
    s = jax.nn.sigmoid(x)
    return s * (1.0 + x * (1.0 - s))


def _cparams(sem=None):
    return pltpu.CompilerParams(dimension_semantics=sem, vmem_limit_bytes=VMEM_LIMIT)


def _mm(a, b, mode, out_dtype, tm, tn, tk, name, cols_outer=False, comm=None):
    def ix(f):
        if cols_outer:
            return lambda j, i, q: f(i, j, q)
        return f

    if mode == "nn":
        (m, k), (_, n) = a.shape, b.shape
        a_spec = pl.BlockSpec((tm, tk), ix(lambda i, j, q: (i, q)))
        b_spec = pl.BlockSpec((tk, tn), ix(lambda i, j, q: (q, j)))
        dims = (((1,), (0,)), ((), ()))
    elif mode == "nt":
        (m, k), (n, _) = a.shape, b.shape
        a_spec = pl.BlockSpec((tm, tk), ix(lambda i, j, q: (i, q)))
        b_spec = pl.BlockSpec((tn, tk), ix(lambda i, j, q: (j, q)))
        dims = (((1,), (1,)), ((), ()))
    else:
        (k, m), (_, n) = a.shape, b.shape
        a_spec = pl.BlockSpec((tk, tm), ix(lambda i, j, q: (q, i)))
        b_spec = pl.BlockSpec((tk, tn), ix(lambda i, j, q: (q, j)))
        dims = (((0,), (0,)), ((), ()))
    assert m % tm == 0 and n % tn == 0 and k % tk == 0, (name, a.shape, b.shape)
    nk = k // tk
    grid = (n // tn, m // tm, nk) if cols_outer else (m // tm, n // tn, nk)

    def body(a_ref, b_ref, o_ref, acc_ref):
        part = lax.dot_general(a_ref[...].astype(BF16), b_ref[...].astype(BF16), dims, preferred_element_type=F32)
        if nk == 1:
            o_ref[...] = part.astype(o_ref.dtype)
        else:
            q = pl.program_id(2)

            @pl.when(q == 0)
            def _():
                acc_ref[...] = part

            @pl.when(q > 0)
            def _():
                acc_ref[...] += part

            @pl.when(q == nk - 1)
            def _():
                o_ref[...] = acc_ref[...].astype(o_ref.dtype)

    res = _grid_call(
        body, name, grid, [a_spec, b_spec], [pl.BlockSpec((tm, tn), ix(lambda i, j, q: (i, j)))],
        [jax.ShapeDtypeStruct((m, n), out_dtype)], [pltpu.VMEM((tm, tn) if nk > 1 else (SUB, LANE), F32)], (a, b), comm)
    return res[0] if comm is None else res


def _row_mask(i, tpe, rows):
    r = lax.broadcasted_iota(jnp.int32, (rows, 1), 0)
    return jnp.logical_or((i % tpe) != 0, r >= PAD)


def _rms_fwd(h, g, rt, name):
    t, d = h.shape

    def body(h_ref, g_ref, o_ref):
        x = h_ref[...]
        r = lax.rsqrt(jnp.mean(x * x, axis=-1, keepdims=True) + NORM_EPS)
        o_ref[...] = (x * r * g_ref[...]).astype(BF16)

    return pl.pallas_call(
        body, name=name, grid=(t // rt,),
        in_specs=[pl.BlockSpec((rt, d), lambda i: (i, 0)), pl.BlockSpec((1, d), lambda i: (0, 0))],
        out_specs=pl.BlockSpec((rt, d), lambda i: (i, 0)),
        out_shape=jax.ShapeDtypeStruct((t, d), BF16),
        compiler_params=_cparams(("parallel",)),
    )(h, g)


def _post_fwd(h, m, g, rt, name):
    t, d = h.shape

    def body(h_ref, m_ref, g_ref, o_ref):
        x = m_ref[...]
        r = lax.rsqrt(jnp.mean(x * x, axis=-1, keepdims=True) + NORM_EPS)
        o_ref[...] = h_ref[...] + x * r * g_ref[...]

    row = pl.BlockSpec((rt, d), lambda i: (i, 0))
    return pl.pallas_call(
        body, name=name, grid=(t // rt,),
        in_specs=[row, row, pl.BlockSpec((1, d), lambda i: (0, 0))], out_specs=row,
        out_shape=jax.ShapeDtypeStruct((t, d), F32),
        compiler_params=_cparams(("parallel",)),
    )(h, m, g)


def _post_rms_fwd(h, m, g_post, g_next, rt, name):
    t, d = h.shape

    def body(h_ref, m_ref, gp_ref, gn_ref, o_ref, n_ref):
        x = m_ref[...]
        r = lax.rsqrt(jnp.mean(x * x, axis=-1, keepdims=True) + NORM_EPS)
        y = h_ref[...] + x * r * gp_ref[...]
        o_ref[...] = y
        r2 = lax.rsqrt(jnp.mean(y * y, axis=-1, keepdims=True) + NORM_EPS)
        n_ref[...] = (y * r2 * gn_ref[...]).astype(BF16)

    row = pl.BlockSpec((rt, d), lambda i: (i, 0))
    vec = pl.BlockSpec((1, d), lambda i: (0, 0))
    return pl.pallas_call(
        body, name=name, grid=(t // rt,),
        in_specs=[row, row, vec, vec], out_specs=[row, row],
        out_shape=[jax.ShapeDtypeStruct((t, d), F32), jax.ShapeDtypeStruct((t, d), BF16)],
        compiler_params=_cparams(("parallel",)),
    )(h, m, g_post, g_next)


def _rms_bwd_math(x, g, dy):
    r = lax.rsqrt(jnp.mean(x * x, axis=-1, keepdims=True) + NORM_EPS)
    gdy = dy * g
    dx = r * gdy - x * (r * r * r) * jnp.mean(gdy * x, axis=-1, keepdims=True)
    return dx, dy * x * r


def _post_bwd(dh, m, g, rt, name):
    t, d = dh.shape

    def body(dh_ref, m_ref, g_ref, dm_ref, dg_ref):
        dm, dgt = _rms_bwd_math(m_ref[...], g_ref[...], dh_ref[...])
        dm_ref[...] = dm.astype(BF16)

        @pl.when(pl.program_id(0) == 0)
        def _():
            dg_ref[...] = jnp.zeros_like(dg_ref)

        dg_ref[...] += jnp.sum(dgt, axis=0, keepdims=True)

    row = pl.BlockSpec((rt, d), lambda i: (i, 0))
    vec = pl.BlockSpec((1, d), lambda i: (0, 0))
    return pl.pallas_call(
        body, name=name, grid=(t // rt,),
        in_specs=[row, row, vec], out_specs=[row, vec],
        out_shape=[jax.ShapeDtypeStruct((t, d), BF16), jax.ShapeDtypeStruct((1, d), F32)],
        compiler_params=_cparams(("arbitrary",)),
    )(dh, m, g)


def _rms_bwd(dh_res, dhn, h, g, lp, rt, name):
    t, d = h.shape
    tpe = lp // rt

    def body(dr_ref, dn_ref, h_ref, g_ref, dh_ref, dg_ref, dmeta_ref):
        i = pl.program_id(0)
        dx, dgt = _rms_bwd_math(h_ref[...], g_ref[...], dn_ref[...])
        dh = dr_ref[...] + dx
        dh_ref[...] = dh

        @pl.when(i == 0)
        def _():
            dg_ref[...] = jnp.zeros_like(dg_ref)
            dmeta_ref[...] = jnp.zeros_like(dmeta_ref)

        dg_ref[...] += jnp.sum(dgt, axis=0, keepdims=True)

        @pl.when((i % tpe) == 0)
        def _():
            dmeta_ref[...] += dh[PAD:PAD + N_META, :]

    row = pl.BlockSpec((rt, d), lambda i: (i, 0))
    vec = pl.BlockSpec((1, d), lambda i: (0, 0))
    return pl.pallas_call(
        body, name=name, grid=(t // rt,),
        in_specs=[row, row, row, vec],
        out_specs=[row, vec, pl.BlockSpec((N_META, d), lambda i: (0, 0))],
        out_shape=[jax.ShapeDtypeStruct((t, d), F32), jax.ShapeDtypeStruct((1, d), F32),
                   jax.ShapeDtypeStruct((N_META, d), F32)],
        compiler_params=_cparams(("arbitrary",)),
    )(dh_res, dhn, h, g)


def _loss_kernel(h, target, lp, name):
    t, d = h.shape
    nb = t // lp
    seq = lp - PAD - N_META
    rows = next(r for r in (1024, 512, 256, 128, CHUNK) if seq % r == 0)
    per = seq // rows

    def body(h_ref, t_ref, zero_ref, dh_ref, loss_ref):
        @pl.when(jnp.logical_and(pl.program_id(0) == 0, pl.program_id(1) == 0))
        def _():
            loss_ref[...] = jnp.zeros_like(loss_ref)

        err = h_ref[...] - t_ref[...]
        dh_ref[...] = err * (1.0 / d)
        loss_ref[...] += (0.5 / d) * jnp.sum(err * err)

    win = pl.BlockSpec((pl.Element(rows), pl.Element(d)),
                       lambda b, j: (pl.multiple_of(b * lp + PAD + N_META + j * rows, CHUNK), 0))
    return pl.pallas_call(
        body, name=name, grid=(nb, per),
        in_specs=[win, pl.BlockSpec((rows, d), lambda b, j: (b * per + j, 0)), HBM_SPEC],
        out_specs=[win, pl.BlockSpec((SUB, LANE), lambda b, j: (0, 0))],
        out_shape=[jax.ShapeDtypeStruct((t, d), F32), jax.ShapeDtypeStruct((SUB, LANE), F32)],
        input_output_aliases={2: 0},
        compiler_params=_cparams(("arbitrary", "arbitrary")),
    )(h, target, jnp.zeros((t, d), F32))


def _window_spec(rows, width, col):
    return pl.BlockSpec((pl.Element(rows), pl.Element(width)), lambda i: (i * rows, col))


def _halo_specs(t, width, col, halo):
    cur = _window_spec(TILE, width, col)
    prev = pl.BlockSpec((pl.Element(halo), pl.Element(width)),
                        lambda i: (pl.multiple_of(jnp.maximum(i * TILE - halo, 0), halo), col))
    nxt = pl.BlockSpec((pl.Element(halo), pl.Element(width)),
                       lambda i: (pl.multiple_of(jnp.minimum((i + 1) * TILE, t - halo), halo), col))
    return cur, prev, nxt


def _f32(ref, lo, hi):
    return ref[:, lo:hi].astype(F32)


def _rows_from(x, start, rows):
    s = start % SUB
    if s == 0:
        return x[start:start + rows]
    return pltpu.roll(x, x.shape[0] - s, axis=0)[start - s:start - s + rows]


def _conv_a(ve, w):
    rows = ve.shape[0] - HALO_A
    return (w[0:1] * _rows_from(ve, HALO_A - 2, rows) + w[1:2] * _rows_from(ve, HALO_A - 1, rows)
            + w[2:3] * ve[HALO_A:HALO_A + rows])


def _a_fwd(proj, w, lp, name):
    t = proj.shape[0]
    tpe = lp // TILE
    cur, prev, _ = _halo_specs(t, 4 * D_A, 0, HALO_A)

    def body(p_ref, ph_ref, w_ref, y_ref):
        first = (pl.program_id(0) % tpe) == 0
        v = _f32(p_ref, D_A, 2 * D_A) * _f32(p_ref, 2 * D_A, 3 * D_A)
        vh = jnp.where(first, 0.0, _f32(ph_ref, D_A, 2 * D_A) * _f32(ph_ref, 2 * D_A, 3 * D_A))
        cv = _conv_a(jnp.concatenate([vh, v], axis=0), w_ref[...])
        y_ref[...] = (_f32(p_ref, 0, D_A) * cv * _silu(_f32(p_ref, 3 * D_A, 4 * D_A))).astype(BF16)

    return pl.pallas_call(
        body, name=name, grid=(t // TILE,),
        in_specs=[cur, prev, pl.BlockSpec((CONV_A_K, D_A), lambda i: (0, 0))],
        out_specs=_window_spec(TILE, D_A, D_B),
        out_shape=jax.ShapeDtypeStruct((t, D_B + D_A + D_C), BF16),
        compiler_params=_cparams(("parallel",)),
    )(proj, proj, w)


def _a_bwd(dy, proj, w, dproj, lp, name):
    t = proj.shape[0]
    tpe = lp // TILE
    cur, prev, nxt = _halo_specs(t, 4 * D_A, 0, HALO_A)
    dcur, _, dnxt = _halo_specs(t, D_A, D_B, HALO_A)

    def body(dy_ref, dyn_ref, p_ref, ph_ref, pn_ref, w_ref, dproj_ref, dp_ref, dw_ref):
        i = pl.program_id(0)
        first = (i % tpe) == 0
        last = (i % tpe) == tpe - 1
        w = w_ref[...]
        ab, ac, ax, az = (_f32(p_ref, k * D_A, (k + 1) * D_A) for k in range(4))
        v = ac * ax
        vh = jnp.where(first, 0.0, _f32(ph_ref, D_A, 2 * D_A) * _f32(ph_ref, 2 * D_A, 3 * D_A))
        ve = jnp.concatenate([vh, v], axis=0)
        taps = [_rows_from(ve, HALO_A - 2 + k, TILE) for k in range(CONV_A_K)]
        cv = w[0:1] * taps[0] + w[1:2] * taps[1] + w[2:3] * taps[2]
        s = _silu(az)
        dy_ = dy_ref[...].astype(F32)
        dcv = dy_ * ab * s
        dcvn = jnp.where(last, 0.0, dyn_ref[...].astype(F32) * _f32(pn_ref, 0, D_A) * _silu(_f32(pn_ref, 3 * D_A, 4 * D_A)))
        dce = jnp.concatenate([dcv, dcvn], axis=0)
        dv = w[2:3] * dce[0:TILE] + w[1:2] * _rows_from(dce, 1, TILE) + w[0:1] * _rows_from(dce, 2, TILE)
        dp = jnp.concatenate([dy_ * cv * s, dv * ax, dv * ac, dy_ * ab * cv * _dsilu(az)], axis=1)
        dp_ref[...] = jnp.where(_row_mask(i, tpe, TILE), dp, 0.0).astype(BF16)
        dw = jnp.concatenate(
            [jnp.sum(dcv * taps[k], axis=0, keepdims=True) for k in range(CONV_A_K)], axis=0)

        @pl.when(i == 0)
        def _():
            dw_ref[...] = jnp.zeros_like(dw_ref)

        dw_ref[...] += dw

    wspec = pl.BlockSpec((CONV_A_K, D_A), lambda i: (0, 0))
    return pl.pallas_call(
        body, name=name, grid=(t // TILE,),
        in_specs=[dcur, dnxt, cur, prev, nxt, wspec, HBM_SPEC],
        out_specs=[_window_spec(TILE, 4 * D_A, 0), wspec],
        out_shape=[jax.ShapeDtypeStruct(dproj.shape, dproj.dtype), jax.ShapeDtypeStruct((CONV_A_K, D_A), F32)],
        input_output_aliases={6: 0},
        compiler_params=_cparams(("arbitrary",)),
    )(dy, dy, proj, proj, proj, w, dproj)


def _conv_ssm(xe, w, rows, off):
    acc = w[0:1] * _rows_from(xe, off - 3, rows)
    for k in range(1, SSM_K):
        acc = acc + w[k:k + 1] * _rows_from(xe, off - 3 + k, rows)
    return acc


def _xbc_fwd(proj, w, b, lp, name):
    t = proj.shape[0]
    tpe = lp // TILE
    cur, prev, _ = _halo_specs(t, N_XBC, COL_XBC, HALO_A)

    def body(x_ref, xh_ref, w_ref, b_ref, o_ref):
        first = (pl.program_id(0) % tpe) == 0
        xh = jnp.where(first, 0.0, xh_ref[...].astype(F32))
        xe = jnp.concatenate([xh, x_ref[...].astype(F32)], axis=0)
        o_ref[...] = _silu(_conv_ssm(xe, w_ref[...], TILE, HALO_A) + b_ref[...]).astype(BF16)

    return pl.pallas_call(
        body, name=name, grid=(t // TILE,),
        in_specs=[cur, prev, pl.BlockSpec((SSM_K, N_XBC), lambda i: (0, 0)), pl.BlockSpec((1, N_XBC), lambda i: (0, 0))],
        out_specs=pl.BlockSpec((TILE, N_XBC), lambda i: (i, 0)),
        out_shape=jax.ShapeDtypeStruct((t, N_XBC), BF16),
        compiler_params=_cparams(("parallel",)),
    )(proj, proj, w, b)


def _xbc_bwd(dxbc, proj, w, b, dproj, lp, name):
    t = proj.shape[0]
    tpe = lp // TILE
    cur, prev, nxt = _halo_specs(t, N_XBC, COL_XBC, HALO_A)
    dcur, _, dnxt = _halo_specs(t, N_XBC, 0, HALO_A)

    def body(d_ref, dn_ref, x_ref, xh_ref, xn_ref, w_ref, b_ref, dproj_ref, dx_ref, dw_ref, db_ref):
        i = pl.program_id(0)
        first = (i % tpe) == 0
        last = (i % tpe) == tpe - 1
        w = w_ref[...]
        xh = jnp.where(first, 0.0, xh_ref[...].astype(F32))
        xe = jnp.concatenate([xh, x_ref[...].astype(F32), xn_ref[...].astype(F32)], axis=0)
        taps = [_rows_from(xe, HALO_A - 3 + k, TILE + HALO_A) for k in range(SSM_K)]
        pre = b_ref[...] + w[0:1] * taps[0]
        for k in range(1, SSM_K):
            pre = pre + w[k:k + 1] * taps[k]
        de = jnp.concatenate([d_ref[...].astype(F32), jnp.where(last, 0.0, dn_ref[...].astype(F32))], axis=0)
        dpre = de * _dsilu(pre)
        dx = w[3:4] * dpre[0:TILE]
        for k in range(SSM_K - 1):
            dx = dx + w[k:k + 1] * _rows_from(dpre, 3 - k, TILE)
        dx_ref[...] = jnp.where(_row_mask(i, tpe, TILE), dx, 0.0).astype(BF16)
        dpc = dpre[0:TILE]
        dw = jnp.concatenate(
            [jnp.sum(dpc * taps[k][0:TILE], axis=0, keepdims=True) for k in range(SSM_K)], axis=0)

        @pl.when(i == 0)
        def _():
            dw_ref[...] = jnp.zeros_like(dw_ref)
            db_ref[...] = jnp.zeros_like(db_ref)

        dw_ref[...] += dw
        db_ref[...] += jnp.sum(dpc, axis=0, keepdims=True)

    wspec = pl.BlockSpec((SSM_K, N_XBC), lambda i: (0, 0))
    bspec = pl.BlockSpec((1, N_XBC), lambda i: (0, 0))
    return pl.pallas_call(
        body, name=name, grid=(t // TILE,),
        in_specs=[dcur, dnxt, cur, prev, nxt, wspec, bspec, HBM_SPEC],
        out_specs=[_window_spec(TILE, N_XBC, COL_XBC), wspec, bspec],
        out_shape=[jax.ShapeDtypeStruct(dproj.shape, dproj.dtype), jax.ShapeDtypeStruct((SSM_K, N_XBC), F32),
                   jax.ShapeDtypeStruct((1, N_XBC), F32)],
        input_output_aliases={7: 0},
        compiler_params=_cparams(("arbitrary",)),
    )(dxbc, dxbc, proj, proj, proj, w, b, dproj)


SUBROWS = 32


def _fill_shifted(scr, x):
    scr[0] = x
    for s in range(1, SUB):
        scr[s] = pltpu.roll(x, x.shape[0] - s, axis=0)


def _window(scr, start, rows):
    s = start % SUB
    return scr[s, start - s:start - s + rows, :]


def _conv_conf(scr, w, rows, off, base):
    acc = w[0:1] * _window(scr, base + off - (CONF_K - 1), rows)
    for k in range(1, CONF_K):
        acc = acc + w[k:k + 1] * _window(scr, base + off - (CONF_K - 1) + k, rows)
    return acc


def _ln_fwd(u1, g, b):
    mu = jnp.mean(u1, axis=-1, keepdims=True)
    xc = u1 - mu
    rstd = lax.rsqrt(jnp.mean(xc * xc, axis=-1, keepdims=True) + LN_EPS)
    n = xc * rstd
    return n, rstd, n * g + b


def _c_fwd(proj, w, cb, g, b, ybuf, lp, name):
    t = proj.shape[0]
    tpe = lp // TILE
    cur, prev, _ = _halo_specs(t, 3 * D_C, COL_C, HALO_C)

    def body(p_ref, ph_ref, w_ref, cb_ref, g_ref, b_ref, ybuf_ref, y_ref, u1_ref, u0_scr):
        first = (pl.program_id(0) % tpe) == 0
        u0h = jnp.where(first, 0.0, _f32(ph_ref, 0, D_C) * jax.nn.sigmoid(_f32(ph_ref, D_C, 2 * D_C)))
        _fill_shifted(u0_scr, jnp.concatenate([u0h, _f32(p_ref, 0, D_C) * jax.nn.sigmoid(_f32(p_ref, D_C, 2 * D_C))], axis=0))
        w = w_ref[...]
        for r0 in range(0, TILE, SUBROWS):
            u1 = _conv_conf(u0_scr, w, SUBROWS, HALO_C, r0) + cb_ref[...]
            u1_ref[r0:r0 + SUBROWS, :] = u1.astype(BF16)
            _, _, u2 = _ln_fwd(u1, g_ref[...], b_ref[...])
            cz = p_ref[r0:r0 + SUBROWS, 2 * D_C:3 * D_C].astype(F32)
            y_ref[r0:r0 + SUBROWS, :] = (_silu(u2) * _silu(cz)).astype(BF16)

    vec = pl.BlockSpec((1, D_C), lambda i: (0, 0))
    return pl.pallas_call(
        body, name=name, grid=(t // TILE,),
        in_specs=[cur, prev, pl.BlockSpec((CONF_K, D_C), lambda i: (0, 0)), vec, vec, vec, HBM_SPEC],
        out_specs=[_window_spec(TILE, D_C, D_B + D_A), pl.BlockSpec((TILE, D_C), lambda i: (i, 0))],
        out_shape=[jax.ShapeDtypeStruct(ybuf.shape, ybuf.dtype), jax.ShapeDtypeStruct((t, D_C), BF16)],
        scratch_shapes=[pltpu.VMEM((SUB, HALO_C + TILE, D_C), F32)],
        input_output_aliases={6: 0},
        compiler_params=_cparams(("parallel",)),
    )(proj, proj, w, cb, g, b, ybuf)


def _c_bwd(dy, proj, u1, w, cb, g, b, dproj, lp, name):
    t = proj.shape[0]
    tpe = lp // TILE
    cur, prev, nxt = _halo_specs(t, 3 * D_C, COL_C, HALO_C)
    dcur, _, dnxt = _halo_specs(t, D_C, D_B + D_A, HALO_C)
    ucur, _, unxt = _halo_specs(t, D_C, 0, HALO_C)
    ext = TILE + HALO_C

    def body(dy_ref, dyn_ref, p_ref, ph_ref, pn_ref, u1_ref, u1n_ref, w_ref, cb_ref, g_ref, b_ref, dproj_ref,
             dp_ref, dw_ref, dcb_ref, dg_ref, db_ref, u0_scr, du1_scr, wacc_scr):
        i = pl.program_id(0)
        first = (i % tpe) == 0
        last = (i % tpe) == tpe - 1
        w = w_ref[...]

        @pl.when(i == 0)
        def _():
            dw_ref[...] = jnp.zeros_like(dw_ref)
            dcb_ref[...] = jnp.zeros_like(dcb_ref)
            dg_ref[...] = jnp.zeros_like(dg_ref)
            db_ref[...] = jnp.zeros_like(db_ref)

        u0h = jnp.where(first, 0.0, _f32(ph_ref, 0, D_C) * jax.nn.sigmoid(_f32(ph_ref, D_C, 2 * D_C)))
        _fill_shifted(u0_scr, jnp.concatenate(
            [u0h, _f32(p_ref, 0, D_C) * jax.nn.sigmoid(_f32(p_ref, D_C, 2 * D_C))], axis=0))
        dcb = jnp.zeros((1, D_C), F32)
        dg = jnp.zeros((1, D_C), F32)
        db = jnp.zeros((1, D_C), F32)
        for r0 in range(0, ext, SUBROWS):
            in_tile = r0 < TILE
            src, dsrc, usrc, q0 = (p_ref, dy_ref, u1_ref, r0) if in_tile else (pn_ref, dyn_ref, u1n_ref, r0 - TILE)
            u1 = usrc[q0:q0 + SUBROWS, :].astype(F32)
            n, rstd, u2 = _ln_fwd(u1, g_ref[...], b_ref[...])
            cz = src[q0:q0 + SUBROWS, 2 * D_C:3 * D_C].astype(F32)
            dyc = dsrc[q0:q0 + SUBROWS, :].astype(F32)
            if not in_tile:
                dyc = jnp.where(last, 0.0, dyc)
            du2 = dyc * _silu(cz) * _dsilu(u2)
            dn = du2 * g_ref[...]
            du1 = rstd * (dn - jnp.mean(dn, axis=-1, keepdims=True) - n * jnp.mean(dn * n, axis=-1, keepdims=True))
            du1_scr[0, r0:r0 + SUBROWS, :] = du1
            if in_tile:
                dp_ref[r0:r0 + SUBROWS, 2 * D_C:3 * D_C] = (dyc * _silu(u2) * _dsilu(cz)).astype(BF16)
                dg = dg + jnp.sum(du2 * n, axis=0, keepdims=True)
                db = db + jnp.sum(du2, axis=0, keepdims=True)
                dcb = dcb + jnp.sum(du1, axis=0, keepdims=True)
        dcb_ref[...] += dcb
        dg_ref[...] += dg
        db_ref[...] += db
        mask = _row_mask(i, tpe, TILE)
        _fill_shifted(du1_scr, du1_scr[0])
        for r0 in range(0, TILE, SUBROWS):
            acc = w[0:1] * _window(du1_scr, r0 + CONF_K - 1, SUBROWS)
            for k in range(1, CONF_K):
                acc = acc + w[k:k + 1] * _window(du1_scr, r0 + CONF_K - 1 - k, SUBROWS)
            ca = p_ref[r0:r0 + SUBROWS, 0:D_C].astype(F32)
            sg = jax.nn.sigmoid(p_ref[r0:r0 + SUBROWS, D_C:2 * D_C].astype(F32))
            m = mask[r0:r0 + SUBROWS]
            dp_ref[r0:r0 + SUBROWS, 0:D_C] = jnp.where(m, acc * sg, 0.0).astype(BF16)
            dp_ref[r0:r0 + SUBROWS, D_C:2 * D_C] = jnp.where(m, acc * ca * sg * (1.0 - sg), 0.0).astype(BF16)
        for k in range(CONF_K):
            part = jnp.zeros((SUB, D_C), F32)
            for r0 in range(0, TILE, SUBROWS):
                prod = du1_scr[0, r0:r0 + SUBROWS, :] * _window(u0_scr, HALO_C + r0 - (CONF_K - 1) + k, SUBROWS)
                for q in range(0, SUBROWS, SUB):
                    part = part + prod[q:q + SUB]
            wacc_scr[k:k + 1, :] = jnp.sum(part, axis=0, keepdims=True)
        dw_ref[...] += wacc_scr[0:CONF_K, :]

    vec = pl.BlockSpec((1, D_C), lambda i: (0, 0))
    wspec = pl.BlockSpec((CONF_K, D_C), lambda i: (0, 0))
    return pl.pallas_call(
        body, name=name, grid=(t // TILE,),
        in_specs=[dcur, dnxt, cur, prev, nxt, ucur, unxt, wspec, vec, vec, vec, HBM_SPEC],
        out_specs=[_window_spec(TILE, 3 * D_C, COL_C), wspec, vec, vec, vec],
        out_shape=[jax.ShapeDtypeStruct(dproj.shape, dproj.dtype), jax.ShapeDtypeStruct((CONF_K, D_C), F32),
                   jax.ShapeDtypeStruct((1, D_C), F32), jax.ShapeDtypeStruct((1, D_C), F32),
                   jax.ShapeDtypeStruct((1, D_C), F32)],
        scratch_shapes=[pltpu.VMEM((SUB, ext, D_C), F32), pltpu.VMEM((SUB, ext, D_C), F32),
                        pltpu.VMEM((HALO_C, D_C), F32)],
        input_output_aliases={11: 0},
        compiler_params=_cparams(("arbitrary",)),
    )(dy, dy, proj, proj, proj, u1, u1, w, cb, g, b, dproj)


def _split_dot(x, m_bf16, terms):
    acc = None
    rem = x
    for _ in range(terms):
        hi = rem.astype(BF16)
        part = jnp.dot(hi, m_bf16, preferred_element_type=F32)
        acc = part if acc is None else acc + part
        rem = rem - hi.astype(F32)
    return acc


def _split_pieces(x, terms):
    out, rem = [], x
    for _ in range(terms):
        hi = rem.astype(BF16)
        out.append(hi)
        rem = rem - hi.astype(F32)
    return out


def _split_dot_many(xs, terms, m_bf16):
    pieces = [p for x, n in zip(xs, terms) for p in _split_pieces(x, n)]
    prod = jnp.dot(jnp.concatenate(pieces, axis=0), m_bf16, preferred_element_type=F32)
    out, off = [], 0
    for x, n in zip(xs, terms):
        rows = x.shape[0]
        acc = prod[off:off + rows]
        for q in range(1, n):
            acc = acc + prod[off + q * rows:off + (q + 1) * rows]
        out.append(acc)
        off += n * rows
    return out


def _split_dot_left(m_bf16, x, terms):
    cols = x.shape[1]
    prod = jnp.dot(m_bf16, jnp.concatenate(_split_pieces(x, terms), axis=1), preferred_element_type=F32)
    acc = prod[:, 0:cols]
    for q in range(1, terms):
        acc = acc + prod[:, q * cols:(q + 1) * cols]
    return acc


def _tri(rows_ge_cols):
    r = lax.broadcasted_iota(jnp.int32, (CHUNK, CHUNK), 0)
    c = lax.broadcasted_iota(jnp.int32, (CHUNK, CHUNK), 1)
    return (r >= c) if rows_ge_cols else (r <= c)


def _softplus(x):
    return jnp.maximum(x, 0.0) + jnp.log(1.0 + jnp.exp(-jnp.abs(x)))


def _ssd_common(dtraw, dtb, dtb_t, a_log, a_log_t, e_mat, valid_col, valid_row, x_terms=2):
    a = -jnp.exp(a_log)
    lane = lax.broadcasted_iota(jnp.int32, (1, LANE), 1)
    a = jnp.where(lane < N_HEADS, a, 0.0)
    a_t = -jnp.exp(a_log_t)
    dt = jnp.where(valid_col, _softplus(dtraw + dtb), 0.0)
    dt = jnp.where(lane < N_HEADS, dt, 0.0)
    dt_t = jnp.where(valid_row, _softplus(dtraw.T[0:N_HEADS, :] + dtb_t), 0.0)
    ltri = _tri(True).astype(BF16)
    utri = _tri(False).astype(BF16)
    big_a = _split_dot_left(ltri, dt * a, 3)
    big_a_t = _split_dot_many([dt_t * a_t], [3], utri)[0]
    e_a = jnp.exp(big_a)
    d_s = jnp.exp(big_a[CHUNK - 1:CHUNK, :] - big_a)
    if x_terms == 1:
        dt_x, e_a_x, d_s_x = _split_dot_many([dt, e_a, d_s], [1, 2, 1], e_mat)
    else:
        dt_x, e_a_x, d_s_x = (_split_dot(q, e_mat, x_terms) for q in (dt, e_a, d_s))
    cd_x = e_a_x[CHUNK - 1:CHUNK, :]
    return a, dt, big_a, big_a_t, e_a, d_s, dt_x, e_a_x, d_s_x, cd_x


def _decay(big_a, big_a_t, h, transposed):
    col = big_a[:, h:h + 1]
    row = big_a_t[h:h + 1, :]
    if not transposed:
        seg = col - row
        return jnp.where(_tri(True), jnp.exp(jnp.minimum(seg, 0.0)), 0.0)
    seg = row - col
    return jnp.where(_tri(False), jnp.exp(jnp.minimum(seg, 0.0)), 0.0)


NT_DIMS = (((1,), (1,)), ((), ()))
TN_DIMS = (((0,), (0,)), ((), ()))
HBM_SPEC = pl.BlockSpec(memory_space=pl.ANY)


class _Comm:
    def __init__(self, inputs, out_shapes, sem_shapes, copies, aliases=None):
        self.inputs, self.out_shapes, self.sem_shapes, self.copies = inputs, out_shapes, sem_shapes, copies
        self.aliases = aliases or {}

    def start(self, ins, outs, sems):
        local, sends, _ = self.copies(ins, outs, sems, False)
        for cp in local + sends:
            cp.start()

    def wait(self, ins, outs, sems):
        local, sends, recvs = self.copies(ins, outs, sems, True)
        for cp in recvs:
            cp.wait_recv()
        for cp in sends:
            cp.wait_send()
        for cp in local:
            cp.wait()


def _merge_comm(*comms):
    comms = [c for c in comms if c is not None]
    if len(comms) <= 1:
        return comms[0] if comms else None

    def copies(ins, outs, sems, with_recvs):
        local, sends, recvs = [], [], []
        i0 = o0 = s0 = 0
        for c in comms:
            ni, no, ns = len(c.inputs), len(c.out_shapes), len(c.sem_shapes)
            loc, snd, rcv = c.copies(ins[i0:i0 + ni], outs[o0:o0 + no], sems[s0:s0 + ns], with_recvs)
            local, sends, recvs = local + loc, sends + snd, recvs + rcv
            i0, o0, s0 = i0 + ni, o0 + no, s0 + ns
        return local, sends, recvs

    aliases, i0, o0 = {}, 0, 0
    for c in comms:
        aliases.update({i0 + k: o0 + v for k, v in c.aliases.items()})
        i0, o0 = i0 + len(c.inputs), o0 + len(c.out_shapes)
    return _Comm([a for c in comms for a in c.inputs], [s for c in comms for s in c.out_shapes],
                 [s for c in comms for s in c.sem_shapes], copies, aliases)


def _grid_call(body, name, grid, in_specs, out_specs, out_shape, scratch_shapes, operands, comm=None, aliases=None):
    aliases = dict(aliases or {})
    if comm is None:
        return pl.pallas_call(
            body, name=name, grid=grid, in_specs=in_specs, out_specs=out_specs, out_shape=out_shape,
            scratch_shapes=scratch_shapes, input_output_aliases=aliases,
            compiler_params=_cparams(("arbitrary",) * len(grid)))(*operands)
    n_in, n_out, n_scr = len(in_specs), len(out_specs), len(scratch_shapes)
    nci, nco = len(comm.inputs), len(comm.out_shapes)

    def wrapped(*refs):
        ins, cins = refs[:n_in], refs[n_in:n_in + nci]
        o0 = n_in + nci
        outs, couts = refs[o0:o0 + n_out], refs[o0 + n_out:o0 + n_out + nco]
        s0 = o0 + n_out + nco
        scr, csems = refs[s0:s0 + n_scr], refs[s0 + n_scr:]
        first = pl.program_id(0) == 0
        last = pl.program_id(0) == grid[0] - 1
        for k in range(1, len(grid)):
            first = jnp.logical_and(first, pl.program_id(k) == 0)
            last = jnp.logical_and(last, pl.program_id(k) == grid[k] - 1)

        @pl.when(first)
        def _():
            comm.start(cins, couts, csems)

        body(*ins, *outs, *scr)

        @pl.when(last)
        def _():
            comm.wait(cins, couts, csems)

    res = pl.pallas_call(
        wrapped, name=name, grid=grid,
        in_specs=list(in_specs) + [HBM_SPEC] * nci, out_specs=list(out_specs) + [HBM_SPEC] * nco,
        out_shape=list(out_shape) + list(comm.out_shapes),
        scratch_shapes=list(scratch_shapes) + list(comm.sem_shapes),
        input_output_aliases={**aliases, **{n_in + k: n_out + v for k, v in comm.aliases.items()}},
        compiler_params=_cparams(("arbitrary",) * len(grid)))(*operands, *comm.inputs)
    return res


def _ssd_fwd(xbc, proj, dtb, a_log, d_skip_x, norm_g, e_mat, ybuf, lp, name, comm=None):
    t = xbc.shape[0]
    cpe = lp // CHUNK
    nb = t // lp
    dtb_p = jnp.pad(dtb.reshape(1, N_HEADS), ((0, 0), (0, LANE - N_HEADS)))
    alog_p = jnp.pad(a_log.reshape(1, N_HEADS), ((0, 0), (0, LANE - N_HEADS)))
    dtb_t = dtb.reshape(N_HEADS, 1)
    alog_t = a_log.reshape(N_HEADS, 1)

    def body(xbc_ref, bz_ref, dt_ref, dtb_ref, dtbt_ref, al_ref, alt_ref, dx_ref, g_ref, e_ref, ybuf_ref,
             yb_ref, ys_ref, st_ref, s_scr):
        c = pl.program_id(0)

        @pl.when(c == 0)
        def _():
            s_scr[...] = jnp.zeros_like(s_scr)

        rows = lax.broadcasted_iota(jnp.int32, (CHUNK, 1), 0)
        cols = lax.broadcasted_iota(jnp.int32, (1, CHUNK), 1)
        valid_col = jnp.logical_or(c > 0, rows >= PAD)
        valid_row = jnp.logical_or(c > 0, cols >= PAD)
        e_mat = e_ref[...]
        ex = []
        for b in range(nb):
            _, _, big_a, big_a_t, _, _, dt_x, e_a_x, d_s_x, cd_x = _ssd_common(
                dt_ref[b].astype(F32), dtb_ref[...], dtbt_ref[...], al_ref[...], alt_ref[...], e_mat, valid_col, valid_row)
            xs = xbc_ref[b, :, 0:D_B].astype(F32)
            xdt = xs * dt_x
            st_prev = s_scr[b]
            ex.append(dict(big_a=big_a, big_a_t=big_a_t, e_a_x=e_a_x, cd_x=cd_x, xs=xs,
                           bs=xbc_ref[b, :, D_B:D_B + N_GROUPS * N_STATE], cs=xbc_ref[b, :, D_B + N_GROUPS * N_STATE:N_XBC],
                           xdt_b=xdt.astype(BF16), st_prev=st_prev, st_b=st_prev.astype(BF16),
                           u_b=(xdt * d_s_x).astype(BF16), bz=bz_ref[b].astype(F32), y_parts=[], new_st=[]))
        for g in range(N_GROUPS):
            gs = slice(g * N_STATE, (g + 1) * N_STATE)
            gw = slice(g * GROUP_W, (g + 1) * GROUP_W)
            for q in ex:
                cb = lax.dot_general(q["cs"][:, gs], q["bs"][:, gs], NT_DIMS, preferred_element_type=F32)
                q["y_off"] = jnp.dot(q["cs"][:, gs], q["st_b"][:, gw], preferred_element_type=F32)
                q["cb"] = cb
                q["diag"] = []
            for e in range(N_HEADS // N_GROUPS):
                h = g * (N_HEADS // N_GROUPS) + e
                for q in ex:
                    m = (q["cb"] * _decay(q["big_a"], q["big_a_t"], h, False)).astype(BF16)
                    q["diag"].append(jnp.dot(m, q["xdt_b"][:, h * HEAD_DIM:(h + 1) * HEAD_DIM], preferred_element_type=F32))
            for q in ex:
                q["y_parts"].append(jnp.concatenate(q["diag"], axis=1) + q["y_off"] * q["e_a_x"][:, gw])
                upd = lax.dot_general(q["bs"][:, gs], q["u_b"][:, gw], TN_DIMS, preferred_element_type=F32)
                q["new_st"].append(q["st_prev"][:, gw] * q["cd_x"][:, gw] + upd)
        for b, q in enumerate(ex):
            y = jnp.concatenate(q["y_parts"], axis=1) + q["xs"] * dx_ref[...]
            z = y * _silu(q["bz"])
            r = lax.rsqrt(jnp.mean(z * z, axis=-1, keepdims=True) + NORM_EPS)
            q["y"], q["yb"] = y, (z * r * g_ref[...]).astype(BF16)
        for b, q in enumerate(ex):
            st_ref[b] = q["st_b"]
            s_scr[b] = jnp.concatenate(q["new_st"], axis=1)
            ys_ref[b] = q["y"].astype(BF16)
            yb_ref[b] = q["yb"]

    def row(width, col):
        return pl.BlockSpec((nb, CHUNK, width), lambda c: (0, c, col))

    def const(shape):
        return pl.BlockSpec(shape, lambda c: (0,) * len(shape))

    proj3 = proj.reshape(nb, lp, N_INP)
    ybuf, yssd, states, *rest = _grid_call(
        body, name, (cpe,),
        [row(N_XBC, 0), row(D_B, COL_BZ // D_B), row(LANE, COL_DT // LANE),
         const((1, LANE)), const((N_HEADS, 1)), const((1, LANE)), const((N_HEADS, 1)),
         const((1, D_B)), const((1, D_B)), const((LANE, D_B)), HBM_SPEC],
        [row(D_B, 0), row(D_B, 0), pl.BlockSpec((nb, None, N_STATE, D_B), lambda c: (0, c, 0, 0))],
        [jax.ShapeDtypeStruct((nb, lp, ybuf.shape[1]), ybuf.dtype), jax.ShapeDtypeStruct((nb, lp, D_B), BF16),
         jax.ShapeDtypeStruct((nb, cpe, N_STATE, D_B), BF16)],
        [pltpu.VMEM((nb, N_STATE, D_B), F32)],
        (xbc.reshape(nb, lp, N_XBC), proj3, proj3, dtb_p, dtb_t, alog_p, alog_t, d_skip_x, norm_g, e_mat,
         ybuf.reshape(nb, lp, ybuf.shape[1])), comm, aliases={10: 0})
    return (ybuf.reshape(t, -1), yssd.reshape(t, D_B), states, *rest)


def _ssd_bwd(dy, y_ssd, xbc, proj, states, dtb, a_log, d_skip_x, norm_g, e_mat, e_mat_t, lp, name, comm=None):
    t = xbc.shape[0]
    cpe = lp // CHUNK
    nb = t // lp
    hpg = N_HEADS // N_GROUPS
    dtb_p = jnp.pad(dtb.reshape(1, N_HEADS), ((0, 0), (0, LANE - N_HEADS)))
    alog_p = jnp.pad(a_log.reshape(1, N_HEADS), ((0, 0), (0, LANE - N_HEADS)))
    dtb_t = dtb.reshape(N_HEADS, 1)
    alog_t = a_log.reshape(N_HEADS, 1)

    def body(dy_ref, ys_ref, xbc_ref, bz_ref, dt_ref, st_ref, dtb_ref, dtbt_ref, al_ref, alt_ref, dx_ref, g_ref,
             e_ref, et_ref, dxbc_ref, dpw_ref, dg_ref, ddtb_ref, dal_ref, dd_ref, ds_scr):
        @pl.when(pl.program_id(0) == 0)
        def _():
            ds_scr[...] = jnp.zeros_like(ds_scr)
            dg_ref[...] = jnp.zeros_like(dg_ref)
            ddtb_ref[...] = jnp.zeros_like(ddtb_ref)
            dal_ref[...] = jnp.zeros_like(dal_ref)
            dd_ref[...] = jnp.zeros_like(dd_ref)

        stores, sums = [], {}
        chains = [one_example(dy_ref.at[b], ys_ref.at[b], xbc_ref.at[b], bz_ref.at[b], dt_ref.at[b], st_ref.at[b], dtb_ref,
                              dtbt_ref, al_ref, alt_ref, dx_ref, g_ref, e_ref, et_ref, dxbc_ref.at[b], dpw_ref.at[b],
                              ds_scr.at[b], stores, sums) for b in range(nb)]
        live = list(chains)
        while live:
            live = [ch for ch in live if next(ch, "done") != "done"]
        for ref, idx, val in stores:
            ref[idx] = val
        for ref, name_ in ((dg_ref, "dg"), (ddtb_ref, "ddtb"), (dal_ref, "dal"), (dd_ref, "dd")):
            total = sums[name_][0]
            for part in sums[name_][1:]:
                total = total + part
            ref[...] += total

    def one_example(dy_ref, ys_ref, xbc_ref, bz_ref, dt_ref, st_ref, dtb_ref, dtbt_ref, al_ref, alt_ref, dx_ref, g_ref,
                    e_ref, et_ref, dxbc_ref, dpw_ref, ds_scr, stores, sums):
        everything = (slice(None), slice(None))
        cc = cpe - 1 - pl.program_id(0)
        rows = lax.broadcasted_iota(jnp.int32, (CHUNK, 1), 0)
        cols = lax.broadcasted_iota(jnp.int32, (1, CHUNK), 1)
        valid_col = jnp.logical_or(cc > 0, rows >= PAD)
        valid_row = jnp.logical_or(cc > 0, cols >= PAD)
        e_mat = e_ref[...]
        e_mat_t = et_ref[...]
        dtraw = dt_ref[...].astype(F32)
        a, dt, big_a, big_a_t, e_a, d_s, dt_x, e_a_x, d_s_x, cd_x = _ssd_common(
            dtraw, dtb_ref[...], dtbt_ref[...], al_ref[...], alt_ref[...], e_mat, valid_col, valid_row, x_terms=1)
        xs = xbc_ref[:, 0:D_B].astype(F32)
        bs = xbc_ref[:, D_B:D_B + N_GROUPS * N_STATE]
        cs = xbc_ref[:, D_B + N_GROUPS * N_STATE:N_XBC]
        xdt = xs * dt_x
        xdt_b = xdt.astype(BF16)
        st_b = st_ref[...]
        dst = ds_scr[...]
        dst_b = dst.astype(BF16)

        ys = ys_ref[...].astype(F32)
        bz = bz_ref[...].astype(F32)
        sil = _silu(bz)
        z = ys * sil
        dz, dgt = _rms_bwd_math(z, g_ref[...], dy_ref[...].astype(F32))
        sums.setdefault("dg", []).append(jnp.sum(dgt, axis=0, keepdims=True))
        stores.append((dpw_ref, (slice(None), slice(0, D_B)), (dz * ys * _dsilu(bz)).astype(BF16)))
        dys = dz * sil

        dd_lane = jnp.sum(dys * xs, axis=0, keepdims=True)
        dxs = dys * dx_ref[...]
        w_x = dys * e_a_x
        w_b = w_x.astype(BF16)
        dys_b = dys.astype(BF16)
        u_b = (xdt * d_s_x).astype(BF16)
        dxdt_parts, dbs_parts, dcs_parts, off_parts, g1_parts = [], [], [], [], []
        da_diag = jnp.zeros((CHUNK, LANE), F32)
        lane = lax.broadcasted_iota(jnp.int32, (1, LANE), 1)
        for g in range(N_GROUPS):
            gs = slice(g * N_STATE, (g + 1) * N_STATE)
            gw = slice(g * GROUP_W, (g + 1) * GROUP_W)
            cs_g, bs_g = cs[:, gs], bs[:, gs]
            dcs = lax.dot_general(w_b[:, gw], st_b[:, gw], NT_DIMS, preferred_element_type=F32)
            y_off = jnp.dot(cs_g, st_b[:, gw], preferred_element_type=F32)
            off_parts.append(y_off)
            dst_new = lax.dot_general(cs_g, w_b[:, gw], TN_DIMS, preferred_element_type=F32)
            g1 = jnp.dot(bs_g, dst_b[:, gw], preferred_element_type=F32)
            g1_parts.append(g1)
            dbs = lax.dot_general(u_b[:, gw], dst_b[:, gw], NT_DIMS, preferred_element_type=F32)
            cb = lax.dot_general(cs_g, bs_g, NT_DIMS, preferred_element_type=F32)
            cbt = lax.dot_general(bs_g, cs_g, NT_DIMS, preferred_element_type=F32)
            dcb = jnp.zeros((CHUNK, CHUNK), F32)
            dcbt = jnp.zeros((CHUNK, CHUNK), F32)
            dxdt_h = []
            for e in range(hpg):
                h = g * hpg + e
                hs = slice(h * HEAD_DIM, (h + 1) * HEAD_DIM)
                dec = _decay(big_a, big_a_t, h, False)
                dect = _decay(big_a, big_a_t, h, True)
                m = cb * dec
                mt = cbt * dect
                dxdt_h.append(jnp.dot(mt.astype(BF16), dys_b[:, hs], preferred_element_type=F32))
                dm = lax.dot_general(dys_b[:, hs], xdt_b[:, hs], NT_DIMS, preferred_element_type=F32)
                dmt = lax.dot_general(xdt_b[:, hs], dys_b[:, hs], NT_DIMS, preferred_element_type=F32)
                dcb = dcb + dm * dec
                dcbt = dcbt + dmt * dect
                da_h = jnp.sum(dm * m - dmt * mt, axis=1, keepdims=True)
                da_diag = da_diag + jnp.where(lane == h, da_h, 0.0)
                yield
            dcs = dcs + jnp.dot(dcb.astype(BF16), bs_g, preferred_element_type=F32)
            dbs = dbs + jnp.dot(dcbt.astype(BF16), cs_g, preferred_element_type=F32)
            dxdt_parts.append(jnp.concatenate(dxdt_h, axis=1) + g1 * d_s_x[:, gw])
            dbs_parts.append(dbs)
            dcs_parts.append(dcs)
            stores.append((ds_scr, (slice(None), gw), dst[:, gw] * cd_x[:, gw] + dst_new))
            yield
        dxdt = jnp.concatenate(dxdt_parts, axis=1)
        y_off = jnp.concatenate(off_parts, axis=1)
        g1 = jnp.concatenate(g1_parts, axis=1)
        dcd_lane = jnp.sum(dst * st_b.astype(F32), axis=0, keepdims=True)
        vecs = jnp.concatenate([jnp.broadcast_to(dcd_lane, (SUB, D_B)), jnp.broadcast_to(dd_lane, (SUB, D_B))], axis=0)
        dds, da_off, ddt_x, vec_sums = _split_dot_many(
            [g1 * xdt, w_x * y_off, dxdt * xs, vecs], [1, 1, 1, 1], e_mat_t)
        dcd = vec_sums[0:1]
        sums.setdefault("dd", []).append(vec_sums[SUB:SUB + 1])
        yield
        t_ds = dds * d_s
        d_a = da_diag + da_off - t_ds
        last_row = jnp.sum(t_ds, axis=0, keepdims=True) + dcd * e_a[CHUNK - 1:CHUNK, :]
        d_a = d_a + jnp.where(rows == CHUNK - 1, last_row, 0.0)
        dda = _split_dot_left(_tri(False).astype(BF16), d_a, 3)
        ddt = dda * a + ddt_x
        sums.setdefault("dal", []).append(jnp.sum(dda * dt * a, axis=0, keepdims=True))
        ddtraw = jnp.where(valid_col, ddt * jax.nn.sigmoid(dtraw + dtb_ref[...]), 0.0)
        ddtraw = jnp.where(lane < N_HEADS, ddtraw, 0.0)
        sums.setdefault("ddtb", []).append(jnp.sum(ddtraw, axis=0, keepdims=True))
        stores.append((dpw_ref, (slice(None), slice(D_B, D_B + LANE)), ddtraw.astype(BF16)))
        dxs = dxs + dxdt * dt_x
        dxbc = jnp.concatenate([dxs] + dbs_parts + dcs_parts, axis=1)
        stores.append((dxbc_ref, everything, jnp.where(valid_col, dxbc, 0.0).astype(BF16)))

    def row(width, col):
        return pl.BlockSpec((nb, CHUNK, width), lambda c: (0, cpe - 1 - c, col))

    def const(shape):
        return pl.BlockSpec(shape, lambda c: (0,) * len(shape))

    proj3 = proj.reshape(nb, lp, N_INP)
    dxbc, dproj, *rest = _grid_call(
        body, name, (cpe,),
        [row(D_B, 0), row(D_B, 0), row(N_XBC, 0), row(D_B, COL_BZ // D_B), row(LANE, COL_DT // LANE),
         pl.BlockSpec((nb, None, N_STATE, D_B), lambda c: (0, cpe - 1 - c, 0, 0)),
         const((1, LANE)), const((N_HEADS, 1)), const((1, LANE)), const((N_HEADS, 1)),
         const((1, D_B)), const((1, D_B)), const((LANE, D_B)), const((D_B, LANE))],
        [row(N_XBC, 0),
         pl.BlockSpec((pl.Element(nb), pl.Element(CHUNK), pl.Element(D_B + LANE)),
                      lambda c: (0, (cpe - 1 - c) * CHUNK, COL_BZ)),
         const((1, D_B)), const((1, LANE)), const((1, LANE)), const((1, LANE))],
        [jax.ShapeDtypeStruct((nb, lp, N_XBC), BF16), jax.ShapeDtypeStruct((nb, lp, N_INP), BF16),
         jax.ShapeDtypeStruct((1, D_B), F32),
         jax.ShapeDtypeStruct((1, LANE), F32), jax.ShapeDtypeStruct((1, LANE), F32),
         jax.ShapeDtypeStruct((1, LANE), F32)],
        [pltpu.VMEM((nb, N_STATE, D_B), F32)],
        (dy.reshape(nb, lp, -1), y_ssd.reshape(nb, lp, D_B), xbc.reshape(nb, lp, N_XBC), proj3, proj3, states,
         dtb_p, dtb_t, alog_p, alog_t, d_skip_x, norm_g, e_mat, e_mat_t), comm)
    return (dxbc.reshape(t, N_XBC), dproj.reshape(t, N_INP), *rest)


HBM_SPEC = pl.BlockSpec(memory_space=pl.ANY)


def _mesh_pos():
    return lax.axis_index("x"), lax.axis_index("y"), lax.axis_index("c")


def _remote(src, dst, send_sem, recv_sem, to):
    return pltpu.make_async_remote_copy(src_ref=src, dst_ref=dst, send_sem=send_sem, recv_sem=recv_sem,
                                        device_id=to, device_id_type=MESH)


def _slot_plain(d):
    return d


def _slot_mix_rows(d):
    return jnp.where(d < 2, d + 4, jnp.where(d < 6, d - 2, d))


def _ag_ici(pieces, slots):
    n = len(pieces)

    def copies(ins, outs, sems, with_recvs):
        send_sems, recv_sems, local_sems = sems
        x, y, c = _mesh_pos()
        local, sends, recvs = [], [], []
        for a in range(n):
            mine = outs[a].at[slots[a](4 * x + 2 * y + c)]
            local.append(pltpu.make_async_copy(ins[a], mine, local_sems.at[a]))
            for k, (px, py) in enumerate([(1 - x, y), (x, 1 - y), (1 - x, 1 - y)]):
                sends.append(_remote(ins[a], mine, send_sems.at[a, k], recv_sems.at[a, k], (px, py, c)))
                if with_recvs:
                    theirs = outs[a].at[slots[a](4 * px + 2 * py + c)]
                    recvs.append(_remote(ins[a], theirs, send_sems.at[a, k], recv_sems.at[a, k], (px, py, c)))
        return local, sends, recvs

    return _Comm(pieces, [jax.ShapeDtypeStruct((N_DEV,) + p.shape, p.dtype) for p in pieces],
                 [pltpu.SemaphoreType.DMA((n, 3)), pltpu.SemaphoreType.DMA((n, 3)), pltpu.SemaphoreType.DMA((n,))], copies)


def _ag_d2d(bufs, slots):
    n = len(bufs)

    def copies(ins, outs, sems, with_recvs):
        send_sems, recv_sems = sems
        x, y, c = _mesh_pos()
        chips = [(x, y), (1 - x, y), (x, 1 - y), (1 - x, 1 - y)]
        sends, recvs = [], []
        for a in range(n):
            for k, (px, py) in enumerate(chips):
                held = slots[a](4 * px + 2 * py + c)
                sends.append(_remote(ins[a].at[held], outs[a].at[held], send_sems.at[a, k], recv_sems.at[a, k], (x, y, 1 - c)))
                if with_recvs:
                    got = slots[a](4 * px + 2 * py + 1 - c)
                    recvs.append(_remote(ins[a].at[got], outs[a].at[got], send_sems.at[a, k], recv_sems.at[a, k], (x, y, 1 - c)))
        return [], sends, recvs

    return _Comm(bufs, [jax.ShapeDtypeStruct(b.shape, b.dtype) for b in bufs],
                 [pltpu.SemaphoreType.DMA((n, 4)), pltpu.SemaphoreType.DMA((n, 4))], copies,
                 aliases={a: a for a in range(n)})


def _rs_d2d(blocks, slots):
    n = len(blocks)

    def copies(ins, outs, sems, with_recvs):
        send_sems, recv_sems = sems
        x, y, c = _mesh_pos()
        sends, recvs = [], []
        for a in range(n):
            for j in range(4):
                src = ins[a].at[slots[a](2 * j + 1 - c)]
                sends.append(_remote(src, outs[a].at[j], send_sems.at[a, j], recv_sems.at[a, j], (x, y, 1 - c)))
                if with_recvs:
                    recvs.append(_remote(src, outs[a].at[j], send_sems.at[a, j], recv_sems.at[a, j], (x, y, 1 - c)))
        return [], sends, recvs

    return _Comm(blocks, [jax.ShapeDtypeStruct((4,) + b.shape[1:], b.dtype) for b in blocks],
                 [pltpu.SemaphoreType.DMA((n, 4)), pltpu.SemaphoreType.DMA((n, 4))], copies)


def _rs_ici(blocks):
    n = len(blocks)

    def copies(ins, outs, sems, with_recvs):
        send_sems, recv_sems = sems
        x, y, c = _mesh_pos()
        sends, recvs = [], []
        for a in range(n):
            for k, (px, py) in enumerate([(1 - x, y), (x, 1 - y), (1 - x, 1 - y)]):
                src = ins[a].at[2 * px + py]
                sends.append(_remote(src, outs[a].at[k], send_sems.at[a, k], recv_sems.at[a, k], (px, py, c)))
                if with_recvs:
                    recvs.append(_remote(src, outs[a].at[k], send_sems.at[a, k], recv_sems.at[a, k], (px, py, c)))
        return [], sends, recvs

    return _Comm(blocks, [jax.ShapeDtypeStruct((3,) + b.shape[1:], b.dtype) for b in blocks],
                 [pltpu.SemaphoreType.DMA((n, 3)), pltpu.SemaphoreType.DMA((n, 3))], copies)


def _run_comm(comm, name):
    n_in, n_out = len(comm.inputs), len(comm.out_shapes)

    def body(*refs):
        ins, outs, sems = refs[:n_in], refs[n_in:n_in + n_out], refs[n_in + n_out:]
        comm.start(ins, outs, sems)
        comm.wait(ins, outs, sems)

    return pl.pallas_call(
        body, name=name, in_specs=[HBM_SPEC] * n_in, out_specs=[HBM_SPEC] * n_out, out_shape=comm.out_shapes,
        scratch_shapes=comm.sem_shapes, input_output_aliases=comm.aliases,
    )(*comm.inputs)


W_ROWS = 784


def _w_segments():
    per = N_IN // N_DEV
    out = []
    for d in range(N_DEV):
        lo, hi = per * d, per * (d + 1)
        for a, b, start in COL_MAP:
            s, e = max(lo, a), min(hi, b)
            if s < e:
                out.append((d, s - lo, e - s, s - a + start))
    return out


def _w_gather_rows(g, name):
    tk = D_MODEL // 2
    u32 = jnp.uint32

    def body(g_ref, o_ref, scr):
        for d in range(N_DEV):
            x32 = pltpu.bitcast(g_ref[d], u32)
            for dd, src, rows, dst in _w_segments():
                if dd == d:
                    scr[dst // 2:(dst + rows) // 2, :] = x32[src // 2:(src + rows) // 2]
        scr[(COL_DT + N_HEADS) // 2:COL_XBC // 2, :] = jnp.zeros(((COL_XBC - COL_DT - N_HEADS) // 2, tk), u32)
        o_ref[...] = pltpu.bitcast(scr[...], BF16)

    return pl.pallas_call(
        body, name=name, grid=(D_MODEL // tk,),
        in_specs=[pl.BlockSpec((N_DEV, W_ROWS, tk), lambda j: (0, 0, j))],
        out_specs=pl.BlockSpec((N_INP, tk), lambda j: (0, j)),
        out_shape=jax.ShapeDtypeStruct((N_INP, D_MODEL), BF16),
        scratch_shapes=[pltpu.VMEM((N_INP // 2, tk), u32)],
        compiler_params=_cparams(("parallel",)),
    )(g)


def _w_split_rows(dwt, name):
    tk = D_MODEL // 4
    per = N_IN // N_DEV

    def body(w_ref, o_ref):
        for d, dst, rows, src in _w_segments():
            o_ref[d, dst // 2:(dst + rows) // 2, :] = pltpu.bitcast(w_ref[src:src + rows, :].astype(BF16), F32)
        for d in range(N_DEV):
            o_ref[d, per // 2:W_ROWS // 2, :] = jnp.zeros(((W_ROWS - per) // 2, tk), F32)

    return pl.pallas_call(
        body, name=name, grid=(D_MODEL // tk,),
        in_specs=[pl.BlockSpec((N_INP, tk), lambda j: (0, j))],
        out_specs=pl.BlockSpec((N_DEV, W_ROWS // 2, tk), lambda j: (0, 0, j)),
        out_shape=jax.ShapeDtypeStruct((N_DEV, W_ROWS // 2, D_MODEL), F32),
        compiler_params=_cparams(("parallel",)),
    )(dwt)


def _rs_pair_sum(g, ra, own_slots, name):
    packed = g.dtype == F32
    _, rows, cols = g.shape
    out_rows = 2 * rows if packed else rows

    def body(s_ref, g_ref, ra_ref, o_ref):
        a, b = g_ref[...], ra_ref[...]
        if packed:
            a, b = pltpu.bitcast(a, BF16), pltpu.bitcast(b, BF16)
        o_ref[...] = (a.astype(F32) + b.astype(F32)).astype(BF16)

    return pl.pallas_call(
        body, name=name,
        grid_spec=pltpu.PrefetchScalarGridSpec(
            num_scalar_prefetch=1, grid=(4,),
            in_specs=[pl.BlockSpec((None, rows, cols), lambda j, s: (s[j], 0, 0)),
                      pl.BlockSpec((None, rows, cols), lambda j, s: (j, 0, 0))],
            out_specs=pl.BlockSpec((None, out_rows, cols), lambda j, s: (j, 0, 0))),
        out_shape=jax.ShapeDtypeStruct((4, out_rows, cols), BF16),
        compiler_params=_cparams(("parallel",)),
    )(own_slots, g, ra)


def _rs_final_sum(h, rb, chip_idx, name):
    _, rows, cols = h.shape

    def body(j_ref, h_ref, rb_ref, o_ref):
        o_ref[...] = ((h_ref[...].astype(F32) + rb_ref[0].astype(F32)) + rb_ref[1].astype(F32)) + rb_ref[2].astype(F32)

    return pl.pallas_call(
        body, name=name,
        grid_spec=pltpu.PrefetchScalarGridSpec(
            num_scalar_prefetch=1, grid=(1,),
            in_specs=[pl.BlockSpec((None, rows, cols), lambda i, j: (j[0], 0, 0)),
                      pl.BlockSpec((3, rows, cols), lambda i, j: (0, 0, 0))],
            out_specs=pl.BlockSpec((rows, cols), lambda i, j: (0, 0))),
        out_shape=jax.ShapeDtypeStruct((rows, cols), F32),
        compiler_params=_cparams(("arbitrary",)),
    )(chip_idx, h, rb)


def _adamw_math(w, g, m, v):
    m = ADAM_B1 * m + (1.0 - ADAM_B1) * g
    v = ADAM_B2 * v + (1.0 - ADAM_B2) * (g * g)
    m_hat = m / (1.0 - ADAM_B1 ** ADAM_STEP)
    v_hat = v / (1.0 - ADAM_B2 ** ADAM_STEP)
    delta = -ADAM_LR * (m_hat / (jnp.sqrt(v_hat) + ADAM_EPS) + ADAM_WD * w)
    return delta, m, v


def _adamw_rows(g, w, m, v, tr, name):
    layers, rows, cols = w.shape

    def body(g_ref, w_ref, m_ref, v_ref, d_out, m_out, v_out):
        delta, m_new, v_new = _adamw_math(w_ref[...], g_ref[...], m_ref[...], v_ref[...])
        d_out[...] = delta
        m_out[...] = m_new
        v_out[...] = v_new

    blk = pl.BlockSpec((None, tr, cols), lambda a, r: (a, r, 0))
    return pl.pallas_call(
        body, name=name, grid=(layers, rows // tr),
        in_specs=[blk] * 4, out_specs=[blk] * 3,
        out_shape=[jax.ShapeDtypeStruct((layers, rows, cols), F32)] * 3,
        compiler_params=_cparams(("parallel", "parallel")),
    )(g, w, m, v)


def _adamw_cols(g_t, w, m, v, name):
    layers, k, cols = w.shape
    tr = 56
    assert g_t.shape[1] % tr == 0
    views = [jnp.transpose(a, (2, 0, 1)) for a in (w, m, v)]

    def body(g_ref, w_ref, m_ref, v_ref, g_out, d_out, m_out, v_out):
        for a in range(layers):
            g = g_ref[a]
            delta, m_new, v_new = _adamw_math(w_ref[:, a, :], g, m_ref[:, a, :], v_ref[:, a, :])
            g_out[:, a, :] = g
            d_out[:, a, :] = delta
            m_out[:, a, :] = m_new
            v_out[:, a, :] = v_new

    col = pl.BlockSpec((tr, layers, k), lambda r: (r, 0, 0))
    outs = pl.pallas_call(
        body, name=name, grid=(pl.cdiv(cols, tr),),
        in_specs=[pl.BlockSpec((layers, tr, k), lambda r: (0, r, 0)), col, col, col], out_specs=[col] * 4,
        out_shape=[jax.ShapeDtypeStruct((cols, layers, k), F32)] * 4,
        compiler_params=_cparams(("parallel",)),
    )(g_t, *views)
    return [jnp.transpose(o, (1, 2, 0)) for o in outs]


def _sum_devices(parts, name):
    _, p, _ = parts.shape

    def body(x_ref, o_ref):
        acc = x_ref[0]
        for d in range(1, N_DEV):
            acc = acc + x_ref[d]
        o_ref[...] = acc

    return pl.pallas_call(
        body, name=name, grid=(1,),
        in_specs=[pl.BlockSpec((N_DEV, p, LANE), lambda i: (0, 0, 0))],
        out_specs=pl.BlockSpec((p, LANE), lambda i: (0, 0)),
        out_shape=jax.ShapeDtypeStruct((p, LANE), F32),
        compiler_params=_cparams(("arbitrary",)),
    )(parts)


def _adamw_small(g, w, m, v, name):
    p = g.shape[0]

    def body(g_ref, w_ref, m_ref, v_ref, d_out, m_out, v_out):
        delta, m_new, v_new = _adamw_math(w_ref[...], g_ref[...], m_ref[...], v_ref[...])
        d_out[...] = delta
        m_out[...] = m_new
        v_out[...] = v_new

    spec = pl.BlockSpec((p, LANE), lambda i: (0, 0))
    return pl.pallas_call(
        body, name=name, grid=(1,),
        in_specs=[spec] * 4, out_specs=[spec] * 3,
        out_shape=[jax.ShapeDtypeStruct((p, LANE), F32)] * 3,
        compiler_params=_cparams(("arbitrary",)),
    )(g, w, m, v)


PACK_ALIGN = SUB * LANE

SMALL_PARAMS = (
    ("meta", (N_META, D_MODEL), 1),
    ("pre_g", (DEPTH, D_MODEL), None),
    ("post_g", (DEPTH, D_MODEL), None),
    ("conv_a_w", (DEPTH, CONV_A_K, D_A), 2),
    ("ssm_conv_w", (DEPTH, SSM_K, N_XBC), 2),
    ("ssm_conv_b", (DEPTH, N_XBC), None),
    ("dt_bias", (DEPTH, N_HEADS), None),
    ("a_log", (DEPTH, N_HEADS), None),
    ("d_skip", (DEPTH, N_HEADS), None),
    ("ssm_norm_g", (DEPTH, D_B), None),
    ("conf_conv_w", (DEPTH, CONF_K, D_C), 2),
    ("conf_conv_b", (DEPTH, D_C), None),
    ("conf_ln_g", (DEPTH, D_C), None),
    ("conf_ln_b", (DEPTH, D_C), None),
)


def _local_shape(shape, axis):
    if axis is None:
        return shape
    return tuple(s // N_DEV if k == axis else s for k, s in enumerate(shape))


def _pack(arrays):
    flat = []
    for a in arrays:
        v = a.reshape(-1).astype(F32)
        flat.append(v)
        if v.shape[0] % PACK_ALIGN:
            flat.append(jnp.zeros(((-v.shape[0]) % PACK_ALIGN,), F32))
    return jnp.concatenate(flat).reshape(-1, LANE)


def _unpack(buf, shapes):
    flat = buf.reshape(-1)
    out, off = [], 0
    for s in shapes:
        size = 1
        for k in s:
            size *= k
        out.append(flat[off:off + size].reshape(s))
        off += size + (-size) % PACK_ALIGN
    return out


def kernel(x, meta, pre_g, post_g, w_in, w_out, conv_a_w, ssm_conv_w, ssm_conv_b, dt_bias, a_log, d_skip, ssm_norm_g, conf_conv_w, conf_conv_b, conf_ln_g, conf_ln_b, loss_target, m_meta, m_pre_g, m_post_g, m_w_in, m_w_out, m_conv_a_w, m_ssm_conv_w, m_ssm_conv_b, m_dt_bias, m_a_log, m_d_skip, m_ssm_norm_g, m_conf_conv_w, m_conf_conv_b, m_conf_ln_g, m_conf_ln_b, v_meta, v_pre_g, v_post_g, v_w_in, v_w_out, v_conv_a_w, v_ssm_conv_w, v_ssm_conv_b, v_dt_bias, v_a_log, v_d_skip, v_ssm_norm_g, v_conf_conv_w, v_conf_conv_b, v_conf_ln_g, v_conf_ln_b):
    weights = dict(meta=meta, pre_g=pre_g, post_g=post_g, conv_a_w=conv_a_w, ssm_conv_w=ssm_conv_w, ssm_conv_b=ssm_conv_b,
                   dt_bias=dt_bias, a_log=a_log, d_skip=d_skip, ssm_norm_g=ssm_norm_g, conf_conv_w=conf_conv_w,
                   conf_conv_b=conf_conv_b, conf_ln_g=conf_ln_g, conf_ln_b=conf_ln_b)
    mom1 = dict(meta=m_meta, pre_g=m_pre_g, post_g=m_post_g, conv_a_w=m_conv_a_w, ssm_conv_w=m_ssm_conv_w,
                ssm_conv_b=m_ssm_conv_b, dt_bias=m_dt_bias, a_log=m_a_log, d_skip=m_d_skip, ssm_norm_g=m_ssm_norm_g,
                conf_conv_w=m_conf_conv_w, conf_conv_b=m_conf_conv_b, conf_ln_g=m_conf_ln_g, conf_ln_b=m_conf_ln_b)
    mom2 = dict(meta=v_meta, pre_g=v_pre_g, post_g=v_post_g, conv_a_w=v_conv_a_w, ssm_conv_w=v_ssm_conv_w,
                ssm_conv_b=v_ssm_conv_b, dt_bias=v_dt_bias, a_log=v_a_log, d_skip=v_d_skip, ssm_norm_g=v_ssm_norm_g,
                conf_conv_w=v_conf_conv_w, conf_conv_b=v_conf_conv_b, conf_ln_g=v_conf_ln_g, conf_ln_b=v_conf_ln_b)
    nb, seq, d = x.shape
    lp = PAD + N_META + seq
    t = nb * lp
    assert lp % TILE == 0 and t % (3 * LANE) == 0 and d == D_MODEL
    rt = lp // 3
    xi, yi, ci = _mesh_pos()
    dev = 4 * xi + 2 * yi + ci
    ci32 = ci.astype(jnp.int32)
    chip_idx = (2 * xi + yi).astype(jnp.int32).reshape(1)
    own_plain = jnp.stack([2 * j + ci32 for j in range(4)])
    own_mix = jnp.stack([_slot_mix_rows(2 * j + ci32) for j in range(4)]).astype(jnp.int32)
    n_in_loc = N_IN // N_DEV
    n_out_loc = 2 * D_MODEL // N_DEV
    slots = (_slot_plain, _slot_mix_rows)

    sharded_small = [n for n, _, ax in SMALL_PARAMS if ax is not None]
    small_shapes = {n: s for n, s, _ in SMALL_PARAMS}
    small_axis = {n: ax for n, _, ax in SMALL_PARAMS}
    sw_pack = _pack([weights[n] for n in sharded_small])
    wt_loc = jnp.pad(jnp.swapaxes(w_in, 1, 2).astype(BF16), ((0, 0), (0, W_ROWS - n_in_loc), (0, 0)))
    wo_loc = w_out.astype(BF16)
    plain2 = (_slot_plain, _slot_plain)
    head = _run_comm(_ag_ici([wt_loc[0], sw_pack], plain2), "ag_ici")
    g_in_t, sw_g = _run_comm(_ag_d2d(head, plain2), "ag_d2d")

    def gathered(bufs):
        return _w_gather_rows(bufs[0], "w_gather_rows"), bufs[1].reshape(2 * D_MODEL, D_MODEL)

    full = dict(weights)
    per_dev = [_unpack(sw_g[k], [_local_shape(small_shapes[n], small_axis[n]) for n in sharded_small]) for k in range(N_DEV)]
    for q, n in enumerate(sharded_small):
        full[n] = jnp.concatenate([per_dev[k][q] for k in range(N_DEV)], axis=small_axis[n])

    e_mat = (lax.broadcasted_iota(jnp.int32, (LANE, D_B), 0) == lax.broadcasted_iota(jnp.int32, (LANE, D_B), 1) // HEAD_DIM)
    e_mat = e_mat.astype(BF16)
    e_mat_t = e_mat.T

    front = jnp.concatenate([jnp.zeros((PAD, d), F32), full["meta"]], axis=0)
    h = jnp.concatenate([jnp.concatenate([front, x[b]], axis=0) for b in range(nb)], axis=0)
    saved = []
    w_t, w_o = _w_gather_rows(g_in_t, "w_gather_rows"), None
    hn = _rms_fwd(h, pre_g[0].reshape(1, -1), rt, "rms_fwd")
    for i in range(DEPTH):
        row = lambda a: a[i].reshape(1, -1)
        if i == 0:
            proj, wo_buf = _mm(hn, w_t, "nt", BF16, t // 3, MM_TN, D_MODEL, "mm_proj", cols_outer=True,
                               comm=_ag_ici([wo_loc[0]], (_slot_mix_rows,)))
        else:
            proj = _mm(hn, w_t, "nt", BF16, t // 3, MM_TN, D_MODEL, "mm_proj", cols_outer=True)
        ycat = _a_fwd(proj, full["conv_a_w"][i], lp, "a_fwd")
        xbc = _xbc_fwd(proj, full["ssm_conv_w"][i], row(ssm_conv_b), lp, "xbc_fwd")
        ycat, u1 = _c_fwd(proj, full["conf_conv_w"][i], row(conf_conv_b), row(conf_ln_g), row(conf_ln_b), ycat, lp, "c_fwd")
        d_skip_x = jnp.repeat(d_skip[i], HEAD_DIM).reshape(1, D_B)
        nxt = _ag_ici([wt_loc[i + 1], wo_loc[i + 1]], slots) if i + 1 < DEPTH else None
        if i == 0:
            nxt = _merge_comm(nxt, _ag_d2d([wo_buf], (_slot_mix_rows,)))
        ycat, yssd, states, *bufs = _ssd_fwd(xbc, proj, dt_bias[i], a_log[i], d_skip_x, row(ssm_norm_g), e_mat, ycat, lp,
                                             "ssd_fwd", comm=nxt)
        if i == 0:
            w_o = bufs.pop().reshape(2 * D_MODEL, D_MODEL)
        if bufs:
            m, *bufs = _mm(ycat, w_o, "nn", F32, t // 3, D_MODEL, 2 * D_MODEL, "mm_out", comm=_ag_d2d(bufs, slots))
        else:
            m = _mm(ycat, w_o, "nn", F32, t // 3, D_MODEL, 2 * D_MODEL, "mm_out")
        saved.append((h, hn, proj, ycat, yssd, states, xbc, m, d_skip_x, w_t, w_o, u1))
        if i + 1 < DEPTH:
            h, hn = _post_rms_fwd(h, m, row(post_g), pre_g[i + 1].reshape(1, -1), rt, "post_rms_fwd")
        else:
            h = _post_fwd(h, m, row(post_g), rt, "post_fwd")
        if bufs:
            w_t, w_o = gathered(bufs)

    dh, loss_blk = _loss_kernel(h, loss_target.reshape(nb * seq, d), lp, "loss")

    grads = {n: [None] * DEPTH for n, _, _ in SMALL_PARAMS if n != "meta"}
    gt_in, gr_out = [None] * DEPTH, [None] * DEPTH
    dmeta = None
    pending = None

    def rs_pair_sums(blocks, ras):
        return [_rs_pair_sum(blocks[0], ras[0], own_plain, "rs_pair_sum_in"),
                _rs_pair_sum(blocks[1], ras[1], own_mix, "rs_pair_sum_out")]

    def rs_finish(layer, hs, rbs):
        gt_in[layer] = _rs_final_sum(hs[0], rbs[0], chip_idx, "rs_final_sum_in")
        gr_out[layer] = _rs_final_sum(hs[1], rbs[1], chip_idx, "rs_final_sum_out")

    for i in reversed(range(DEPTH)):
        row = lambda a: a[i].reshape(1, -1)
        h_i, hn, proj, ycat, yssd, states, xbc, m, d_skip_x, w_t, w_o, u1 = saved[i]
        dm, grads["post_g"][i] = _post_bwd(dh, m, row(post_g), rt, "post_bwd")
        if pending is not None:
            dy, *ras = _mm(dm, w_o, "nt", BF16, t // 3, D_MODEL, D_MODEL, "mm_dy", cols_outer=True,
                           comm=_rs_d2d(list(pending), slots))
            hs = rs_pair_sums(pending, ras)
        else:
            dy = _mm(dm, w_o, "nt", BF16, t // 3, D_MODEL, D_MODEL, "mm_dy", cols_outer=True)
            hs = None
        dw_out = _mm(ycat, dm, "tn", BF16, D_MODEL, D_MODEL, t // 3, "mm_dwout")
        dxbc, dproj, grads["ssm_norm_g"][i], ddtb, dal, dds, *rbs = _ssd_bwd(
            dy, yssd, xbc, proj, states, dt_bias[i], a_log[i], d_skip_x, row(ssm_norm_g), e_mat, e_mat_t, lp, "ssd_bwd",
            comm=_rs_ici(hs) if hs is not None else None)
        if hs is not None:
            rs_finish(i + 1, hs, rbs)
        grads["dt_bias"][i] = ddtb[:, :N_HEADS]
        grads["a_log"][i] = dal[:, :N_HEADS]
        grads["d_skip"][i] = dds[:, :N_HEADS]
        dproj, grads["conv_a_w"][i] = _a_bwd(dy, proj, full["conv_a_w"][i], dproj, lp, "a_bwd")
        dproj, grads["ssm_conv_w"][i], grads["ssm_conv_b"][i] = _xbc_bwd(
            dxbc, proj, full["ssm_conv_w"][i], row(ssm_conv_b), dproj, lp, "xbc_bwd")
        dproj, grads["conf_conv_w"][i], grads["conf_conv_b"][i], grads["conf_ln_g"][i], grads["conf_ln_b"][i] = _c_bwd(
            dy, proj, u1, full["conf_conv_w"][i], row(conf_conv_b), row(conf_ln_g), row(conf_ln_b), dproj, lp, "c_bwd")
        p_out = dw_out.reshape(N_DEV, n_out_loc, D_MODEL)
        if i > 0:
            dhn = _mm(dproj, w_t, "nn", F32, MM_TM, D_MODEL, N_INP, "mm_dhn")
            dw_in_t = _mm(dproj, hn, "tn", F32, MM_TN, D_MODEL, t, "mm_dwin")
        else:
            dhn, ra_out = _mm(dproj, w_t, "nn", F32, MM_TM, D_MODEL, N_INP, "mm_dhn",
                              comm=_rs_d2d([p_out], (_slot_mix_rows,)))
            h_out = _rs_pair_sum(p_out, ra_out, own_mix, "rs_pair_sum_out")
            dw_in_t, rb_out = _mm(dproj, hn, "tn", F32, MM_TN, D_MODEL, t, "mm_dwin", comm=_rs_ici([h_out]))
            gr_out[0] = _rs_final_sum(h_out, rb_out, chip_idx, "rs_final_sum_out")
        dh, grads["pre_g"][i], dmeta = _rms_bwd(dh, dhn, h_i, row(pre_g), lp, rt, "rms_bwd")
        pending = (_w_split_rows(dw_in_t, "w_split_rows"), p_out)
    grad_x = dh.reshape(nb, lp, d)[:, PAD + N_META:]

    names = [n for n, _, _ in SMALL_PARAMS]
    partial = [loss_blk[0:1, 0:1], dmeta] + [jnp.concatenate(grads[n], axis=0) for n in names[1:]]
    part_pack = _pack(partial)
    p_in = pending[0]
    ra_in, part_buf = _run_comm(_merge_comm(_rs_d2d([p_in], (_slot_plain,)), _ag_ici([part_pack], (_slot_plain,))), "rs_d2d")
    h_in = _rs_pair_sum(p_in, ra_in, own_plain, "rs_pair_sum_in")
    rb_in, parts_g = _run_comm(_merge_comm(_rs_ici([h_in]), _ag_d2d([part_buf], (_slot_plain,))), "rs_ici")
    gt_in[0] = _rs_final_sum(h_in, rb_in, chip_idx, "rs_final_sum_in")

    g_w_out = jnp.stack(gr_out)
    big = {"w_in": _adamw_cols(jnp.stack(gt_in), w_in, m_w_in, v_w_in, "adamw_w_in"),
           "w_out": [g_w_out, *_adamw_rows(g_w_out, w_out, m_w_out, v_w_out, n_out_loc, "adamw_w_out")]}

    total = _unpack(_sum_devices(parts_g, "sum_small_grads"), [(1,)] + [small_shapes[n] for n in names])
    loss = total[0][0]
    g_small = {}
    for n, g in zip(names, total[1:]):
        ax = small_axis[n]
        if ax is not None:
            g = lax.dynamic_slice_in_dim(g, dev * (small_shapes[n][ax] // N_DEV), small_shapes[n][ax] // N_DEV, axis=ax)
        g_small[n] = g
    loc_shapes = [_local_shape(small_shapes[n], small_axis[n]) for n in names]
    d_pack, m_pack, v_pack = _adamw_small(_pack([g_small[n] for n in names]), _pack([weights[n] for n in names]),
                                          _pack([mom1[n] for n in names]), _pack([mom2[n] for n in names]), "adamw_small")
    d_small = dict(zip(names, _unpack(d_pack, loc_shapes)))
    m_small = dict(zip(names, _unpack(m_pack, loc_shapes)))
    v_small = dict(zip(names, _unpack(v_pack, loc_shapes)))

    order = ["meta", "pre_g", "post_g", "w_in", "w_out", "conv_a_w", "ssm_conv_w", "ssm_conv_b", "dt_bias", "a_log",
             "d_skip", "ssm_norm_g", "conf_conv_w", "conf_conv_b", "conf_ln_g", "conf_ln_b"]

    def pick(k, small):
        return [big[n][k] if n in big else small[n] for n in order]

    return (loss, grad_x, *pick(0, g_small), *pick(1, d_small), *pick(2, m_small), *pick(3, v_small))
```

```python
import functools

import jax
import jax.numpy as jnp
from jax import lax
from jax.experimental import pallas as pl
from jax.experimental.pallas import tpu as pltpu

F32 = jnp.float32
BF16 = jnp.bfloat16

D_MODEL = 1024
DEPTH = 4
SEQ = 2048
CHUNK = 64
N_META = 16
PAD = 48
LP = PAD + N_META + SEQ
D_A = 512
D_B = 1024
D_C = 512
N_HEADS = 16
HEAD_DIM = 64
N_STATE = 128
N_GROUPS = 2
GROUP_W = D_B // N_GROUPS
N_XBC = D_B + 2 * N_GROUPS * N_STATE
CONV_A_K = 3
SSM_K = 4
CONF_K = 31
NORM_EPS = 1e-6
LN_EPS = 1e-5
N_IN = 6160
N_INP = 6272
COL_BZ = 2048
COL_DT = 3072
COL_XBC = 3200
COL_C = 4736
COL_MAP = ((0, 3072, 0), (3072, 4608, COL_XBC), (4608, 4624, COL_DT), (4624, 6160, COL_C))
LANE = 128
SUB = 8

ADAM_LR = 0.001
ADAM_B1 = 0.9
ADAM_B2 = 0.999
ADAM_EPS = 1e-08
ADAM_WD = 0.01
ADAM_STEP = 10

TILE = 192
SHORT_TILES = 6
HALO_A = 16
HALO_C = 32
MM_TM = 384
MM_TN = 896
VMEM_LIMIT = 56 * 1024 * 1024

MESH = pl.DeviceIdType.MESH
N_DEV = 8


def _silu(x):
    return x * jax.nn.sigmoid(x)


def _dsilu(x):
    s = jax.nn.sigmoid(x)
    return s * (1.0 + x * (1.0 - s))


def _cparams(sem=None):
    return pltpu.CompilerParams(dimension_semantics=sem, vmem_limit_bytes=VMEM_LIMIT)


def _mm(a, b, mode, out_dtype, tm, tn, tk, name, cols_outer=False, comm=None):
    def ix(f):
        if cols_outer:
            return lambda j, i, q: f(i, j, q)
        return f

    if mode == "nn":
        (m, k), (_, n) = a.shape, b.shape
        a_spec = pl.BlockSpec((tm, tk), ix(lambda i, j, q: (i, q)))
        b_spec = pl.BlockSpec((tk, tn), ix(lambda i, j, q: (q, j)))
        dims = (((1,), (0,)), ((), ()))
    elif mode == "nt":
        (m, k), (n, _) = a.shape, b.shape
        a_spec = pl.BlockSpec((tm, tk), ix(lambda i, j, q: (i, q)))
        b_spec = pl.BlockSpec((tn, tk), ix(lambda i, j, q: (j, q)))
        dims = (((1,), (1,)), ((), ()))
    else:
        (k, m), (_, n) = a.shape, b.shape
        a_spec = pl.BlockSpec((tk, tm), ix(lambda i, j, q: (q, i)))
        b_spec = pl.BlockSpec((tk, tn), ix(lambda i, j, q: (q, j)))
        dims = (((0,), (0,)), ((), ()))
    assert m % tm == 0 and n % tn == 0 and k % tk == 0, (name, a.shape, b.shape)
    nk = k // tk
    grid = (n // tn, m // tm, nk) if cols_outer else (m // tm, n // tn, nk)

    def body(a_ref, b_ref, o_ref, acc_ref):
        part = lax.dot_general(a_ref[...].astype(BF16), b_ref[...].astype(BF16), dims, preferred_element_type=F32)
        if nk == 1:
            o_ref[...] = part.astype(o_ref.dtype)
        else:
            q = pl.program_id(2)

            @pl.when(q == 0)
            def _():
                acc_ref[...] = part

            @pl.when(q > 0)
            def _():
                acc_ref[...] += part

            @pl.when(q == nk - 1)
            def _():
                o_ref[...] = acc_ref[...].astype(o_ref.dtype)

    res = _grid_call(
        body, name, grid, [a_spec, b_spec], [pl.BlockSpec((tm, tn), ix(lambda i, j, q: (i, j)))],
        [jax.ShapeDtypeStruct((m, n), out_dtype)], [pltpu.VMEM((tm, tn) if nk > 1 else (SUB, LANE), F32)], (a, b), comm)
    return res[0] if comm is None else res


def _row_mask(i, tpe, rows):
    r = lax.broadcasted_iota(jnp.int32, (rows, 1), 0)
    return jnp.logical_or((i % tpe) != 0, r >= PAD)


def _rms_fwd(h, g, rt, name):
    t, d = h.shape

    def body(h_ref, g_ref, o_ref):
        x = h_ref[...]
        r = lax.rsqrt(jnp.mean(x * x, axis=-1, keepdims=True) + NORM_EPS)
        o_ref[...] = (x * r * g_ref[...]).astype(BF16)

    return pl.pallas_call(
        body, name=name, grid=(t // rt,),
        in_specs=[pl.BlockSpec((rt, d), lambda i: (i, 0)), pl.BlockSpec((1, d), lambda i: (0, 0))],
        out_specs=pl.BlockSpec((rt, d), lambda i: (i, 0)),
        out_shape=jax.ShapeDtypeStruct((t, d), BF16),
        compiler_params=_cparams(("parallel",)),
    )(h, g)


def _post_fwd(h, m, g, rt, name):
    t, d = h.shape

    def body(h_ref, m_ref, g_ref, o_ref):
        x = m_ref[...]
        r = lax.rsqrt(jnp.mean(x * x, axis=-1, keepdims=True) + NORM_EPS)
        o_ref[...] = h_ref[...] + x * r * g_ref[...]

    row = pl.BlockSpec((rt, d), lambda i: (i, 0))
    return pl.pallas_call(
        body, name=name, grid=(t // rt,),
        in_specs=[row, row, pl.BlockSpec((1, d), lambda i: (0, 0))], out_specs=row,
        out_shape=jax.ShapeDtypeStruct((t, d), F32),
        compiler_params=_cparams(("parallel",)),
    )(h, m, g)


def _post_rms_fwd(h, m, g_post, g_next, rt, name):
    t, d = h.shape

    def body(h_ref, m_ref, gp_ref, gn_ref, o_ref, n_ref):
        x = m_ref[...]
        r = lax.rsqrt(jnp.mean(x * x, axis=-1, keepdims=True) + NORM_EPS)
        y = h_ref[...] + x * r * gp_ref[...]
        o_ref[...] = y
        r2 = lax.rsqrt(jnp.mean(y * y, axis=-1, keepdims=True) + NORM_EPS)
        n_ref[...] = (y * r2 * gn_ref[...]).astype(BF16)

    row = pl.BlockSpec((rt, d), lambda i: (i, 0))
    vec = pl.BlockSpec((1, d), lambda i: (0, 0))
    return pl.pallas_call(
        body, name=name, grid=(t // rt,),
        in_specs=[row, row, vec, vec], out_specs=[row, row],
        out_shape=[jax.ShapeDtypeStruct((t, d), F32), jax.ShapeDtypeStruct((t, d), BF16)],
        compiler_params=_cparams(("parallel",)),
    )(h, m, g_post, g_next)


def _rms_bwd_math(x, g, dy):
    r = lax.rsqrt(jnp.mean(x * x, axis=-1, keepdims=True) + NORM_EPS)
    gdy = dy * g
    dx = r * gdy - x * (r * r * r) * jnp.mean(gdy * x, axis=-1, keepdims=True)
    return dx, dy * x * r


def _post_bwd(dh, m, g, rt, name):
    t, d = dh.shape

    def body(dh_ref, m_ref, g_ref, dm_ref, dg_ref):
        dm, dgt = _rms_bwd_math(m_ref[...], g_ref[...], dh_ref[...])
        dm_ref[...] = dm.astype(BF16)

        @pl.when(pl.program_id(0) == 0)
        def _():
            dg_ref[...] = jnp.zeros_like(dg_ref)

        dg_ref[...] += jnp.sum(dgt, axis=0, keepdims=True)

    row = pl.BlockSpec((rt, d), lambda i: (i, 0))
    vec = pl.BlockSpec((1, d), lambda i: (0, 0))
    return pl.pallas_call(
        body, name=name, grid=(t // rt,),
        in_specs=[row, row, vec], out_specs=[row, vec],
        out_shape=[jax.ShapeDtypeStruct((t, d), BF16), jax.ShapeDtypeStruct((1, d), F32)],
        compiler_params=_cparams(("arbitrary",)),
    )(dh, m, g)


def _rms_bwd(dh_res, dhn, h, g, lp, rt, name):
    t, d = h.shape
    tpe = lp // rt

    def body(dr_ref, dn_ref, h_ref, g_ref, dh_ref, dg_ref, dmeta_ref):
        i = pl.program_id(0)
        dx, dgt = _rms_bwd_math(h_ref[...], g_ref[...], dn_ref[...].astype(F32))
        dh = dr_ref[...] + dx
        dh_ref[...] = dh

        @pl.when(i == 0)
        def _():
            dg_ref[...] = jnp.zeros_like(dg_ref)
            dmeta_ref[...] = jnp.zeros_like(dmeta_ref)

        dg_ref[...] += jnp.sum(dgt, axis=0, keepdims=True)

        @pl.when((i % tpe) == 0)
        def _():
            dmeta_ref[...] += dh[PAD:PAD + N_META, :]

    row = pl.BlockSpec((rt, d), lambda i: (i, 0))
    vec = pl.BlockSpec((1, d), lambda i: (0, 0))
    return pl.pallas_call(
        body, name=name, grid=(t // rt,),
        in_specs=[row, row, row, vec],
        out_specs=[row, vec, pl.BlockSpec((N_META, d), lambda i: (0, 0))],
        out_shape=[jax.ShapeDtypeStruct((t, d), F32), jax.ShapeDtypeStruct((1, d), F32),
                   jax.ShapeDtypeStruct((N_META, d), F32)],
        compiler_params=_cparams(("arbitrary",)),
    )(dh_res, dhn, h, g)


def _loss_kernel(h, target, lp, name):
    t, d = h.shape
    nb = t // lp
    seq = lp - PAD - N_META
    rows = next(r for r in (1024, 512, 256, 128, CHUNK) if seq % r == 0)
    per = seq // rows

    def body(h_ref, t_ref, zero_ref, dh_ref, loss_ref):
        @pl.when(jnp.logical_and(pl.program_id(0) == 0, pl.program_id(1) == 0))
        def _():
            loss_ref[...] = jnp.zeros_like(loss_ref)

        err = h_ref[...] - t_ref[...]
        dh_ref[...] = err * (1.0 / d)
        loss_ref[...] += (0.5 / d) * jnp.sum(err * err)

    win = pl.BlockSpec((pl.Element(rows), pl.Element(d)),
                       lambda b, j: (pl.multiple_of(b * lp + PAD + N_META + j * rows, CHUNK), 0))
    return pl.pallas_call(
        body, name=name, grid=(nb, per),
        in_specs=[win, pl.BlockSpec((rows, d), lambda b, j: (b * per + j, 0)), HBM_SPEC],
        out_specs=[win, pl.BlockSpec((SUB, LANE), lambda b, j: (0, 0))],
        out_shape=[jax.ShapeDtypeStruct((t, d), F32), jax.ShapeDtypeStruct((SUB, LANE), F32)],
        input_output_aliases={2: 0},
        compiler_params=_cparams(("arbitrary", "arbitrary")),
    )(h, target, jnp.zeros((t, d), F32))


def _window_spec(rows, width, col):
    return pl.BlockSpec((pl.Element(rows), pl.Element(width)), lambda i: (i * rows, col))


def _halo_specs(t, width, col, halo, tile=TILE):
    cur = _window_spec(tile, width, col)
    prev = pl.BlockSpec((pl.Element(halo), pl.Element(width)),
                        lambda i: (pl.multiple_of(jnp.maximum(i * tile - halo, 0), halo), col))
    nxt = pl.BlockSpec((pl.Element(halo), pl.Element(width)),
                       lambda i: (pl.multiple_of(jnp.minimum((i + 1) * tile, t - halo), halo), col))
    return cur, prev, nxt


def _f32(ref, lo, hi):
    return ref[:, lo:hi].astype(F32)


def _rows_from(x, start, rows):
    s = start % SUB
    if s == 0:
        return x[start:start + rows]
    return pltpu.roll(x, x.shape[0] - s, axis=0)[start - s:start - s + rows]


def _conv_a(ve, w):
    rows = ve.shape[0] - HALO_A
    return (w[0:1] * _rows_from(ve, HALO_A - 2, rows) + w[1:2] * _rows_from(ve, HALO_A - 1, rows)
            + w[2:3] * ve[HALO_A:HALO_A + rows])


def _a_fwd(proj, w, lp, name):
    t = proj.shape[0]
    tile = lp // SHORT_TILES
    tpe = lp // tile
    cur, prev, _ = _halo_specs(t, 4 * D_A, 0, HALO_A, tile)

    def body(p_ref, ph_ref, w_ref, y_ref):
        first = (pl.program_id(0) % tpe) == 0
        v = _f32(p_ref, D_A, 2 * D_A) * _f32(p_ref, 2 * D_A, 3 * D_A)
        vh = jnp.where(first, 0.0, _f32(ph_ref, D_A, 2 * D_A) * _f32(ph_ref, 2 * D_A, 3 * D_A))
        cv = _conv_a(jnp.concatenate([vh, v], axis=0), w_ref[...])
        y_ref[...] = (_f32(p_ref, 0, D_A) * cv * _silu(_f32(p_ref, 3 * D_A, 4 * D_A))).astype(BF16)

    return pl.pallas_call(
        body, name=name, grid=(t // tile,),
        in_specs=[cur, prev, pl.BlockSpec((CONV_A_K, D_A), lambda i: (0, 0))],
        out_specs=_window_spec(tile, D_A, D_B),
        out_shape=jax.ShapeDtypeStruct((t, D_B + D_A + D_C), BF16),
        compiler_params=_cparams(("parallel",)),
    )(proj, proj, w)


def _a_bwd(dy, proj, w, dproj, lp, name):
    t = proj.shape[0]
    tile = lp // SHORT_TILES
    tpe = lp // tile
    cur, prev, nxt = _halo_specs(t, 4 * D_A, 0, HALO_A, tile)
    dcur, _, dnxt = _halo_specs(t, D_A, D_B, HALO_A, tile)

    def body(dy_ref, dyn_ref, p_ref, ph_ref, pn_ref, w_ref, dproj_ref, dp_ref, dw_ref):
        i = pl.program_id(0)
        first = (i % tpe) == 0
        last = (i % tpe) == tpe - 1
        w = w_ref[...]
        ab, ac, ax, az = (_f32(p_ref, k * D_A, (k + 1) * D_A) for k in range(4))
        v = ac * ax
        vh = jnp.where(first, 0.0, _f32(ph_ref, D_A, 2 * D_A) * _f32(ph_ref, 2 * D_A, 3 * D_A))
        ve = jnp.concatenate([vh, v], axis=0)
        taps = [_rows_from(ve, HALO_A - 2 + k, tile) for k in range(CONV_A_K)]
        cv = w[0:1] * taps[0] + w[1:2] * taps[1] + w[2:3] * taps[2]
        s = _silu(az)
        dy_ = dy_ref[...].astype(F32)
        dcv = dy_ * ab * s
        dcvn = jnp.where(last, 0.0, dyn_ref[...].astype(F32) * _f32(pn_ref, 0, D_A) * _silu(_f32(pn_ref, 3 * D_A, 4 * D_A)))
        dce = jnp.concatenate([dcv, dcvn], axis=0)
        dv = w[2:3] * dce[0:tile] + w[1:2] * _rows_from(dce, 1, tile) + w[0:1] * _rows_from(dce, 2, tile)
        dp = jnp.concatenate([dy_ * cv * s, dv * ax, dv * ac, dy_ * ab * cv * _dsilu(az)], axis=1)
        dp_ref[...] = jnp.where(_row_mask(i, tpe, tile), dp, 0.0).astype(BF16)
        dw = jnp.concatenate(
            [jnp.sum(dcv * taps[k], axis=0, keepdims=True) for k in range(CONV_A_K)], axis=0)

        @pl.when(i == 0)
        def _():
            dw_ref[...] = jnp.zeros_like(dw_ref)

        dw_ref[...] += dw

    wspec = pl.BlockSpec((CONV_A_K, D_A), lambda i: (0, 0))
    return pl.pallas_call(
        body, name=name, grid=(t // tile,),
        in_specs=[dcur, dnxt, cur, prev, nxt, wspec, HBM_SPEC],
        out_specs=[_window_spec(tile, 4 * D_A, 0), wspec],
        out_shape=[jax.ShapeDtypeStruct(dproj.shape, dproj.dtype), jax.ShapeDtypeStruct((CONV_A_K, D_A), F32)],
        input_output_aliases={6: 0},
        compiler_params=_cparams(("arbitrary",)),
    )(dy, dy, proj, proj, proj, w, dproj)


def _conv_ssm(xe, w, rows, off):
    acc = w[0:1] * _rows_from(xe, off - 3, rows)
    for k in range(1, SSM_K):
        acc = acc + w[k:k + 1] * _rows_from(xe, off - 3 + k, rows)
    return acc


def _xbc_fwd(proj, w, b, lp, name):
    t = proj.shape[0]
    tile = lp // SHORT_TILES
    tpe = lp // tile
    cur, prev, _ = _halo_specs(t, N_XBC, COL_XBC, HALO_A, tile)

    def body(x_ref, xh_ref, w_ref, b_ref, o_ref):
        first = (pl.program_id(0) % tpe) == 0
        xh = jnp.where(first, 0.0, xh_ref[...].astype(F32))
        xe = jnp.concatenate([xh, x_ref[...].astype(F32)], axis=0)
        o_ref[...] = _silu(_conv_ssm(xe, w_ref[...], tile, HALO_A) + b_ref[...]).astype(BF16)

    return pl.pallas_call(
        body, name=name, grid=(t // tile,),
        in_specs=[cur, prev, pl.BlockSpec((SSM_K, N_XBC), lambda i: (0, 0)), pl.BlockSpec((1, N_XBC), lambda i: (0, 0))],
        out_specs=pl.BlockSpec((tile, N_XBC), lambda i: (i, 0)),
        out_shape=jax.ShapeDtypeStruct((t, N_XBC), BF16),
        compiler_params=_cparams(("parallel",)),
    )(proj, proj, w, b)


def _xbc_bwd(dxbc, proj, w, b, dproj, lp, name):
    t = proj.shape[0]
    tile = lp // SHORT_TILES
    tpe = lp // tile
    cur, prev, nxt = _halo_specs(t, N_XBC, COL_XBC, HALO_A, tile)
    dcur, _, dnxt = _halo_specs(t, N_XBC, 0, HALO_A, tile)

    def body(d_ref, dn_ref, x_ref, xh_ref, xn_ref, w_ref, b_ref, dproj_ref, dx_ref, dw_ref, db_ref):
        i = pl.program_id(0)
        first = (i % tpe) == 0
        last = (i % tpe) == tpe - 1
        w = w_ref[...]
        xh = jnp.where(first, 0.0, xh_ref[...].astype(F32))
        xe = jnp.concatenate([xh, x_ref[...].astype(F32), xn_ref[...].astype(F32)], axis=0)
        taps = [_rows_from(xe, HALO_A - 3 + k, tile + HALO_A) for k in range(SSM_K)]
        pre = b_ref[...] + w[0:1] * taps[0]
        for k in range(1, SSM_K):
            pre = pre + w[k:k + 1] * taps[k]
        de = jnp.concatenate([d_ref[...].astype(F32), jnp.where(last, 0.0, dn_ref[...].astype(F32))], axis=0)
        dpre = de * _dsilu(pre)
        dx = w[3:4] * dpre[0:tile]
        for k in range(SSM_K - 1):
            dx = dx + w[k:k + 1] * _rows_from(dpre, 3 - k, tile)
        dx_ref[...] = jnp.where(_row_mask(i, tpe, tile), dx, 0.0).astype(BF16)
        dpc = dpre[0:tile]
        dw = jnp.concatenate(
            [jnp.sum(dpc * taps[k][0:tile], axis=0, keepdims=True) for k in range(SSM_K)], axis=0)

        @pl.when(i == 0)
        def _():
            dw_ref[...] = jnp.zeros_like(dw_ref)
            db_ref[...] = jnp.zeros_like(db_ref)

        dw_ref[...] += dw
        db_ref[...] += jnp.sum(dpc, axis=0, keepdims=True)

    wspec = pl.BlockSpec((SSM_K, N_XBC), lambda i: (0, 0))
    bspec = pl.BlockSpec((1, N_XBC), lambda i: (0, 0))
    return pl.pallas_call(
        body, name=name, grid=(t // tile,),
        in_specs=[dcur, dnxt, cur, prev, nxt, wspec, bspec, HBM_SPEC],
        out_specs=[_window_spec(tile, N_XBC, COL_XBC), wspec, bspec],
        out_shape=[jax.ShapeDtypeStruct(dproj.shape, dproj.dtype), jax.ShapeDtypeStruct((SSM_K, N_XBC), F32),
                   jax.ShapeDtypeStruct((1, N_XBC), F32)],
        input_output_aliases={7: 0},
        compiler_params=_cparams(("arbitrary",)),
    )(dxbc, dxbc, proj, proj, proj, w, b, dproj)


SUBROWS = 32


def _fill_shifted(scr, x):
    scr[0] = x
    for s in range(1, SUB):
        scr[s] = pltpu.roll(x, x.shape[0] - s, axis=0)


def _window(scr, start, rows):
    s = start % SUB
    return scr[s, start - s:start - s + rows, :]


def _conv_conf(scr, w, rows, off, base):
    acc = w[0:1] * _window(scr, base + off - (CONF_K - 1), rows)
    for k in range(1, CONF_K):
        acc = acc + w[k:k + 1] * _window(scr, base + off - (CONF_K - 1) + k, rows)
    return acc


def _ln_fwd(u1, g, b):
    mu = jnp.mean(u1, axis=-1, keepdims=True)
    xc = u1 - mu
    rstd = lax.rsqrt(jnp.mean(xc * xc, axis=-1, keepdims=True) + LN_EPS)
    n = xc * rstd
    return n, rstd, n * g + b


def _c_fwd(proj, w, cb, g, b, ybuf, lp, name):
    t = proj.shape[0]
    tpe = lp // TILE
    cur, prev, _ = _halo_specs(t, 3 * D_C, COL_C, HALO_C)

    def body(p_ref, ph_ref, w_ref, cb_ref, g_ref, b_ref, ybuf_ref, y_ref, u1_ref, u0_scr):
        first = (pl.program_id(0) % tpe) == 0
        u0h = jnp.where(first, 0.0, _f32(ph_ref, 0, D_C) * jax.nn.sigmoid(_f32(ph_ref, D_C, 2 * D_C)))
        _fill_shifted(u0_scr, jnp.concatenate([u0h, _f32(p_ref, 0, D_C) * jax.nn.sigmoid(_f32(p_ref, D_C, 2 * D_C))], axis=0))
        w = w_ref[...]
        for r0 in range(0, TILE, SUBROWS):
            u1 = _conv_conf(u0_scr, w, SUBROWS, HALO_C, r0) + cb_ref[...]
            u1_ref[r0:r0 + SUBROWS, :] = u1.astype(BF16)
            _, _, u2 = _ln_fwd(u1, g_ref[...], b_ref[...])
            cz = p_ref[r0:r0 + SUBROWS, 2 * D_C:3 * D_C].astype(F32)
            y_ref[r0:r0 + SUBROWS, :] = (_silu(u2) * _silu(cz)).astype(BF16)

    vec = pl.BlockSpec((1, D_C), lambda i: (0, 0))
    return pl.pallas_call(
        body, name=name, grid=(t // TILE,),
        in_specs=[cur, prev, pl.BlockSpec((CONF_K, D_C), lambda i: (0, 0)), vec, vec, vec, HBM_SPEC],
        out_specs=[_window_spec(TILE, D_C, D_B + D_A), pl.BlockSpec((TILE, D_C), lambda i: (i, 0))],
        out_shape=[jax.ShapeDtypeStruct(ybuf.shape, ybuf.dtype), jax.ShapeDtypeStruct((t, D_C), BF16)],
        scratch_shapes=[pltpu.VMEM((SUB, HALO_C + TILE, D_C), F32)],
        input_output_aliases={6: 0},
        compiler_params=_cparams(("parallel",)),
    )(proj, proj, w, cb, g, b, ybuf)


def _c_bwd(dy, proj, u1, w, cb, g, b, dproj, lp, name):
    t = proj.shape[0]
    tpe = lp // TILE
    cur, prev, nxt = _halo_specs(t, 3 * D_C, COL_C, HALO_C)
    dcur, _, dnxt = _halo_specs(t, D_C, D_B + D_A, HALO_C)
    ucur, _, unxt = _halo_specs(t, D_C, 0, HALO_C)
    ext = TILE + HALO_C

    def body(dy_ref, dyn_ref, p_ref, ph_ref, pn_ref, u1_ref, u1n_ref, w_ref, cb_ref, g_ref, b_ref, dproj_ref,
             dp_ref, dw_ref, dcb_ref, dg_ref, db_ref, u0_scr, du1_scr, wacc_scr):
        i = pl.program_id(0)
        first = (i % tpe) == 0
        last = (i % tpe) == tpe - 1
        w = w_ref[...]

        @pl.when(i == 0)
        def _():
            dw_ref[...] = jnp.zeros_like(dw_ref)
            dcb_ref[...] = jnp.zeros_like(dcb_ref)
            dg_ref[...] = jnp.zeros_like(dg_ref)
            db_ref[...] = jnp.zeros_like(db_ref)

        u0h = jnp.where(first, 0.0, _f32(ph_ref, 0, D_C) * jax.nn.sigmoid(_f32(ph_ref, D_C, 2 * D_C)))
        _fill_shifted(u0_scr, jnp.concatenate(
            [u0h, _f32(p_ref, 0, D_C) * jax.nn.sigmoid(_f32(p_ref, D_C, 2 * D_C))], axis=0))
        dcb = jnp.zeros((1, D_C), F32)
        dg = jnp.zeros((1, D_C), F32)
        db = jnp.zeros((1, D_C), F32)
        for r0 in range(0, ext, SUBROWS):
            in_tile = r0 < TILE
            src, dsrc, usrc, q0 = (p_ref, dy_ref, u1_ref, r0) if in_tile else (pn_ref, dyn_ref, u1n_ref, r0 - TILE)
            u1 = usrc[q0:q0 + SUBROWS, :].astype(F32)
            n, rstd, u2 = _ln_fwd(u1, g_ref[...], b_ref[...])
            cz = src[q0:q0 + SUBROWS, 2 * D_C:3 * D_C].astype(F32)
            dyc = dsrc[q0:q0 + SUBROWS, :].astype(F32)
            if not in_tile:
                dyc = jnp.where(last, 0.0, dyc)
            du2 = dyc * _silu(cz) * _dsilu(u2)
            dn = du2 * g_ref[...]
            du1 = rstd * (dn - jnp.mean(dn, axis=-1, keepdims=True) - n * jnp.mean(dn * n, axis=-1, keepdims=True))
            du1_scr[0, r0:r0 + SUBROWS, :] = du1
            if in_tile:
                dp_ref[r0:r0 + SUBROWS, 2 * D_C:3 * D_C] = (dyc * _silu(u2) * _dsilu(cz)).astype(BF16)
                dg = dg + jnp.sum(du2 * n, axis=0, keepdims=True)
                db = db + jnp.sum(du2, axis=0, keepdims=True)
                dcb = dcb + jnp.sum(du1, axis=0, keepdims=True)
        dcb_ref[...] += dcb
        dg_ref[...] += dg
        db_ref[...] += db
        mask = _row_mask(i, tpe, TILE)
        _fill_shifted(du1_scr, du1_scr[0])
        for r0 in range(0, TILE, SUBROWS):
            acc = w[0:1] * _window(du1_scr, r0 + CONF_K - 1, SUBROWS)
            for k in range(1, CONF_K):
                acc = acc + w[k:k + 1] * _window(du1_scr, r0 + CONF_K - 1 - k, SUBROWS)
            ca = p_ref[r0:r0 + SUBROWS, 0:D_C].astype(F32)
            sg = jax.nn.sigmoid(p_ref[r0:r0 + SUBROWS, D_C:2 * D_C].astype(F32))
            m = mask[r0:r0 + SUBROWS]
            dp_ref[r0:r0 + SUBROWS, 0:D_C] = jnp.where(m, acc * sg, 0.0).astype(BF16)
            dp_ref[r0:r0 + SUBROWS, D_C:2 * D_C] = jnp.where(m, acc * ca * sg * (1.0 - sg), 0.0).astype(BF16)
        for k in range(CONF_K):
            part = jnp.zeros((SUB, D_C), F32)
            for r0 in range(0, TILE, SUBROWS):
                prod = du1_scr[0, r0:r0 + SUBROWS, :] * _window(u0_scr, HALO_C + r0 - (CONF_K - 1) + k, SUBROWS)
                for q in range(0, SUBROWS, SUB):
                    part = part + prod[q:q + SUB]
            wacc_scr[k:k + 1, :] = jnp.sum(part, axis=0, keepdims=True)
        dw_ref[...] += wacc_scr[0:CONF_K, :]

    vec = pl.BlockSpec((1, D_C), lambda i: (0, 0))
    wspec = pl.BlockSpec((CONF_K, D_C), lambda i: (0, 0))
    return pl.pallas_call(
        body, name=name, grid=(t // TILE,),
        in_specs=[dcur, dnxt, cur, prev, nxt, ucur, unxt, wspec, vec, vec, vec, HBM_SPEC],
        out_specs=[_window_spec(TILE, 3 * D_C, COL_C), wspec, vec, vec, vec],
        out_shape=[jax.ShapeDtypeStruct(dproj.shape, dproj.dtype), jax.ShapeDtypeStruct((CONF_K, D_C), F32),
                   jax.ShapeDtypeStruct((1, D_C), F32), jax.ShapeDtypeStruct((1, D_C), F32),
                   jax.ShapeDtypeStruct((1, D_C), F32)],
        scratch_shapes=[pltpu.VMEM((SUB, ext, D_C), F32), pltpu.VMEM((SUB, ext, D_C), F32),
                        pltpu.VMEM((HALO_C, D_C), F32)],
        input_output_aliases={11: 0},
        compiler_params=_cparams(("arbitrary",)),
    )(dy, dy, proj, proj, proj, u1, u1, w, cb, g, b, dproj)


def _split_dot(x, m_bf16, terms):
    acc = None
    rem = x
    for _ in range(terms):
        hi = rem.astype(BF16)
        part = jnp.dot(hi, m_bf16, preferred_element_type=F32)
        acc = part if acc is None else acc + part
        rem = rem - hi.astype(F32)
    return acc


def _split_pieces(x, terms):
    out, rem = [], x
    for _ in range(terms):
        hi = rem.astype(BF16)
        out.append(hi)
        rem = rem - hi.astype(F32)
    return out


def _split_dot_many(xs, terms, m_bf16):
    pieces = [p for x, n in zip(xs, terms) for p in _split_pieces(x, n)]
    prod = jnp.dot(jnp.concatenate(pieces, axis=0), m_bf16, preferred_element_type=F32)
    out, off = [], 0
    for x, n in zip(xs, terms):
        rows = x.shape[0]
        acc = prod[off:off + rows]
        for q in range(1, n):
            acc = acc + prod[off + q * rows:off + (q + 1) * rows]
        out.append(acc)
        off += n * rows
    return out


def _split_dot_left(m_bf16, x, terms):
    cols = x.shape[1]
    prod = jnp.dot(m_bf16, jnp.concatenate(_split_pieces(x, terms), axis=1), preferred_element_type=F32)
    acc = prod[:, 0:cols]
    for q in range(1, terms):
        acc = acc + prod[:, q * cols:(q + 1) * cols]
    return acc


def _tri(rows_ge_cols):
    r = lax.broadcasted_iota(jnp.int32, (CHUNK, CHUNK), 0)
    c = lax.broadcasted_iota(jnp.int32, (CHUNK, CHUNK), 1)
    return (r >= c) if rows_ge_cols else (r <= c)


def _softplus(x):
    return jnp.maximum(x, 0.0) + jnp.log(1.0 + jnp.exp(-jnp.abs(x)))


def _ssd_common(dtraw, dtb, dtb_t, a_log, a_log_t, e_mat, valid_col, valid_row, x_terms=2):
    a = -jnp.exp(a_log)
    lane = lax.broadcasted_iota(jnp.int32, (1, LANE), 1)
    a = jnp.where(lane < N_HEADS, a, 0.0)
    a_t = -jnp.exp(a_log_t)
    dt = jnp.where(valid_col, _softplus(dtraw + dtb), 0.0)
    dt = jnp.where(lane < N_HEADS, dt, 0.0)
    dt_t = jnp.where(valid_row, _softplus(dtraw.T[0:N_HEADS, :] + dtb_t), 0.0)
    ltri = _tri(True).astype(BF16)
    utri = _tri(False).astype(BF16)
    big_a = _split_dot_left(ltri, dt * a, 3)
    big_a_t = _split_dot_many([dt_t * a_t], [3], utri)[0]
    e_a = jnp.exp(big_a)
    d_s = jnp.exp(big_a[CHUNK - 1:CHUNK, :] - big_a)
    if x_terms == 1:
        dt_x, e_a_x, d_s_x = _split_dot_many([dt, e_a, d_s], [1, 2, 1], e_mat)
    else:
        dt_x, e_a_x, d_s_x = (_split_dot(q, e_mat, x_terms) for q in (dt, e_a, d_s))
    cd_x = e_a_x[CHUNK - 1:CHUNK, :]
    return a, dt, big_a, big_a_t, e_a, d_s, dt_x, e_a_x, d_s_x, cd_x


def _decay(big_a, big_a_t, h, transposed):
    col = big_a[:, h:h + 1]
    row = big_a_t[h:h + 1, :]
    if not transposed:
        seg = col - row
        return jnp.where(_tri(True), jnp.exp(jnp.minimum(seg, 0.0)), 0.0)
    seg = row - col
    return jnp.where(_tri(False), jnp.exp(jnp.minimum(seg, 0.0)), 0.0)


NT_DIMS = (((1,), (1,)), ((), ()))
TN_DIMS = (((0,), (0,)), ((), ()))
HBM_SPEC = pl.BlockSpec(memory_space=pl.ANY)


class _Comm:
    def __init__(self, inputs, out_shapes, sem_shapes, copies, aliases=None):
        self.inputs, self.out_shapes, self.sem_shapes, self.copies = inputs, out_shapes, sem_shapes, copies
        self.aliases = aliases or {}

    def start(self, ins, outs, sems):
        local, sends, _ = self.copies(ins, outs, sems, False)
        for cp in local + sends:
            cp.start()

    def wait(self, ins, outs, sems):
        local, sends, recvs = self.copies(ins, outs, sems, True)
        for cp in recvs:
            cp.wait_recv()
        for cp in sends:
            cp.wait_send()
        for cp in local:
            cp.wait()


def _merge_comm(*comms):
    comms = [c for c in comms if c is not None]
    if len(comms) <= 1:
        return comms[0] if comms else None

    def copies(ins, outs, sems, with_recvs):
        local, sends, recvs = [], [], []
        i0 = o0 = s0 = 0
        for c in comms:
            ni, no, ns = len(c.inputs), len(c.out_shapes), len(c.sem_shapes)
            loc, snd, rcv = c.copies(ins[i0:i0 + ni], outs[o0:o0 + no], sems[s0:s0 + ns], with_recvs)
            local, sends, recvs = local + loc, sends + snd, recvs + rcv
            i0, o0, s0 = i0 + ni, o0 + no, s0 + ns
        return local, sends, recvs

    aliases, i0, o0 = {}, 0, 0
    for c in comms:
        aliases.update({i0 + k: o0 + v for k, v in c.aliases.items()})
        i0, o0 = i0 + len(c.inputs), o0 + len(c.out_shapes)
    return _Comm([a for c in comms for a in c.inputs], [s for c in comms for s in c.out_shapes],
                 [s for c in comms for s in c.sem_shapes], copies, aliases)


def _grid_call(body, name, grid, in_specs, out_specs, out_shape, scratch_shapes, operands, comm=None, aliases=None):
    aliases = dict(aliases or {})
    if comm is None:
        return pl.pallas_call(
            body, name=name, grid=grid, in_specs=in_specs, out_specs=out_specs, out_shape=out_shape,
            scratch_shapes=scratch_shapes, input_output_aliases=aliases,
            compiler_params=_cparams(("arbitrary",) * len(grid)))(*operands)
    n_in, n_out, n_scr = len(in_specs), len(out_specs), len(scratch_shapes)
    nci, nco = len(comm.inputs), len(comm.out_shapes)

    def wrapped(*refs):
        ins, cins = refs[:n_in], refs[n_in:n_in + nci]
        o0 = n_in + nci
        outs, couts = refs[o0:o0 + n_out], refs[o0 + n_out:o0 + n_out + nco]
        s0 = o0 + n_out + nco
        scr, csems = refs[s0:s0 + n_scr], refs[s0 + n_scr:]
        first = pl.program_id(0) == 0
        last = pl.program_id(0) == grid[0] - 1
        for k in range(1, len(grid)):
            first = jnp.logical_and(first, pl.program_id(k) == 0)
            last = jnp.logical_and(last, pl.program_id(k) == grid[k] - 1)

        @pl.when(first)
        def _():
            comm.start(cins, couts, csems)

        body(*ins, *outs, *scr)

        @pl.when(last)
        def _():
            comm.wait(cins, couts, csems)

    res = pl.pallas_call(
        wrapped, name=name, grid=grid,
        in_specs=list(in_specs) + [HBM_SPEC] * nci, out_specs=list(out_specs) + [HBM_SPEC] * nco,
        out_shape=list(out_shape) + list(comm.out_shapes),
        scratch_shapes=list(scratch_shapes) + list(comm.sem_shapes),
        input_output_aliases={**aliases, **{n_in + k: n_out + v for k, v in comm.aliases.items()}},
        compiler_params=_cparams(("arbitrary",) * len(grid)))(*operands, *comm.inputs)
    return res


def _ssd_fwd(xbc, proj, dtb, a_log, d_skip_x, norm_g, e_mat, ybuf, lp, name, comm=None):
    t = xbc.shape[0]
    cpe = lp // CHUNK
    nb = t // lp
    dtb_p = jnp.pad(dtb.reshape(1, N_HEADS), ((0, 0), (0, LANE - N_HEADS)))
    alog_p = jnp.pad(a_log.reshape(1, N_HEADS), ((0, 0), (0, LANE - N_HEADS)))
    dtb_t = dtb.reshape(N_HEADS, 1)
    alog_t = a_log.reshape(N_HEADS, 1)

    def body(xbc_ref, bz_ref, dt_ref, dtb_ref, dtbt_ref, al_ref, alt_ref, dx_ref, g_ref, e_ref, ybuf_ref,
             yb_ref, ys_ref, st_ref, s_scr):
        c = pl.program_id(0)

        @pl.when(c == 0)
        def _():
            s_scr[...] = jnp.zeros_like(s_scr)

        rows = lax.broadcasted_iota(jnp.int32, (CHUNK, 1), 0)
        cols = lax.broadcasted_iota(jnp.int32, (1, CHUNK), 1)
        valid_col = jnp.logical_or(c > 0, rows >= PAD)
        valid_row = jnp.logical_or(c > 0, cols >= PAD)
        e_mat = e_ref[...]
        ex = []
        for b in range(nb):
            _, _, big_a, big_a_t, _, _, dt_x, e_a_x, d_s_x, cd_x = _ssd_common(
                dt_ref[b].astype(F32), dtb_ref[...], dtbt_ref[...], al_ref[...], alt_ref[...], e_mat, valid_col, valid_row)
            xs = xbc_ref[b, :, 0:D_B].astype(F32)
            xdt = xs * dt_x
            st_prev = s_scr[b]
            ex.append(dict(big_a=big_a, big_a_t=big_a_t, e_a_x=e_a_x, cd_x=cd_x, xs=xs,
                           bs=xbc_ref[b, :, D_B:D_B + N_GROUPS * N_STATE], cs=xbc_ref[b, :, D_B + N_GROUPS * N_STATE:N_XBC],
                           xdt_b=xdt.astype(BF16), st_prev=st_prev, st_b=st_prev.astype(BF16),
                           u_b=(xdt * d_s_x).astype(BF16), bz=bz_ref[b].astype(F32), y_parts=[], new_st=[]))
        for g in range(N_GROUPS):
            gs = slice(g * N_STATE, (g + 1) * N_STATE)
            gw = slice(g * GROUP_W, (g + 1) * GROUP_W)
            for q in ex:
                cb = lax.dot_general(q["cs"][:, gs], q["bs"][:, gs], NT_DIMS, preferred_element_type=F32)
                q["y_off"] = jnp.dot(q["cs"][:, gs], q["st_b"][:, gw], preferred_element_type=F32)
                q["cb"] = cb
                q["diag"] = []
            for e in range(N_HEADS // N_GROUPS):
                h = g * (N_HEADS // N_GROUPS) + e
                for q in ex:
                    m = (q["cb"] * _decay(q["big_a"], q["big_a_t"], h, False)).astype(BF16)
                    q["diag"].append(jnp.dot(m, q["xdt_b"][:, h * HEAD_DIM:(h + 1) * HEAD_DIM], preferred_element_type=F32))
            for q in ex:
                q["y_parts"].append(jnp.concatenate(q["diag"], axis=1) + q["y_off"] * q["e_a_x"][:, gw])
                upd = lax.dot_general(q["bs"][:, gs], q["u_b"][:, gw], TN_DIMS, preferred_element_type=F32)
                q["new_st"].append(q["st_prev"][:, gw] * q["cd_x"][:, gw] + upd)
        for b, q in enumerate(ex):
            y = jnp.concatenate(q["y_parts"], axis=1) + q["xs"] * dx_ref[...]
            z = y * _silu(q["bz"])
            r = lax.rsqrt(jnp.mean(z * z, axis=-1, keepdims=True) + NORM_EPS)
            q["y"], q["yb"] = y, (z * r * g_ref[...]).astype(BF16)
        for b, q in enumerate(ex):
            st_ref[b] = q["st_b"]
            s_scr[b] = jnp.concatenate(q["new_st"], axis=1)
            ys_ref[b] = q["y"].astype(BF16)
            yb_ref[b] = q["yb"]

    def row(width, col):
        return pl.BlockSpec((nb, CHUNK, width), lambda c: (0, c, col))

    def const(shape):
        return pl.BlockSpec(shape, lambda c: (0,) * len(shape))

    proj3 = proj.reshape(nb, lp, N_INP)
    ybuf, yssd, states, *rest = _grid_call(
        body, name, (cpe,),
        [row(N_XBC, 0), row(D_B, COL_BZ // D_B), row(LANE, COL_DT // LANE),
         const((1, LANE)), const((N_HEADS, 1)), const((1, LANE)), const((N_HEADS, 1)),
         const((1, D_B)), const((1, D_B)), const((LANE, D_B)), HBM_SPEC],
        [row(D_B, 0), row(D_B, 0), pl.BlockSpec((nb, None, N_STATE, D_B), lambda c: (0, c, 0, 0))],
        [jax.ShapeDtypeStruct((nb, lp, ybuf.shape[1]), ybuf.dtype), jax.ShapeDtypeStruct((nb, lp, D_B), BF16),
         jax.ShapeDtypeStruct((nb, cpe, N_STATE, D_B), BF16)],
        [pltpu.VMEM((nb, N_STATE, D_B), F32)],
        (xbc.reshape(nb, lp, N_XBC), proj3, proj3, dtb_p, dtb_t, alog_p, alog_t, d_skip_x, norm_g, e_mat,
         ybuf.reshape(nb, lp, ybuf.shape[1])), comm, aliases={10: 0})
    return (ybuf.reshape(t, -1), yssd.reshape(t, D_B), states, *rest)


def _ssd_bwd(dy, y_ssd, xbc, proj, states, dtb, a_log, d_skip_x, norm_g, e_mat, e_mat_t, lp, name, comm=None):
    t = xbc.shape[0]
    cpe = lp // CHUNK
    nb = t // lp
    hpg = N_HEADS // N_GROUPS
    dtb_p = jnp.pad(dtb.reshape(1, N_HEADS), ((0, 0), (0, LANE - N_HEADS)))
    alog_p = jnp.pad(a_log.reshape(1, N_HEADS), ((0, 0), (0, LANE - N_HEADS)))
    dtb_t = dtb.reshape(N_HEADS, 1)
    alog_t = a_log.reshape(N_HEADS, 1)

    def body(dy_ref, ys_ref, xbc_ref, bz_ref, dt_ref, st_ref, dtb_ref, dtbt_ref, al_ref, alt_ref, dx_ref, g_ref,
             e_ref, et_ref, dxbc_ref, dpw_ref, dg_ref, ddtb_ref, dal_ref, dd_ref, ds_scr):
        @pl.when(pl.program_id(0) == 0)
        def _():
            ds_scr[...] = jnp.zeros_like(ds_scr)
            dg_ref[...] = jnp.zeros_like(dg_ref)
            ddtb_ref[...] = jnp.zeros_like(ddtb_ref)
            dal_ref[...] = jnp.zeros_like(dal_ref)
            dd_ref[...] = jnp.zeros_like(dd_ref)

        stores, sums = [], {}
        chains = [one_example(dy_ref.at[b], ys_ref.at[b], xbc_ref.at[b], bz_ref.at[b], dt_ref.at[b], st_ref.at[b], dtb_ref,
                              dtbt_ref, al_ref, alt_ref, dx_ref, g_ref, e_ref, et_ref, dxbc_ref.at[b], dpw_ref.at[b],
                              ds_scr.at[b], stores, sums) for b in range(nb)]
        live = list(chains)
        while live:
            live = [ch for ch in live if next(ch, "done") != "done"]
        for ref, idx, val in stores:
            ref[idx] = val
        for ref, name_ in ((dg_ref, "dg"), (ddtb_ref, "ddtb"), (dal_ref, "dal"), (dd_ref, "dd")):
            total = sums[name_][0]
            for part in sums[name_][1:]:
                total = total + part
            ref[...] += total

    def one_example(dy_ref, ys_ref, xbc_ref, bz_ref, dt_ref, st_ref, dtb_ref, dtbt_ref, al_ref, alt_ref, dx_ref, g_ref,
                    e_ref, et_ref, dxbc_ref, dpw_ref, ds_scr, stores, sums):
        everything = (slice(None), slice(None))
        cc = cpe - 1 - pl.program_id(0)
        rows = lax.broadcasted_iota(jnp.int32, (CHUNK, 1), 0)
        cols = lax.broadcasted_iota(jnp.int32, (1, CHUNK), 1)
        valid_col = jnp.logical_or(cc > 0, rows >= PAD)
        valid_row = jnp.logical_or(cc > 0, cols >= PAD)
        e_mat = e_ref[...]
        e_mat_t = et_ref[...]
        dtraw = dt_ref[...].astype(F32)
        a, dt, big_a, big_a_t, e_a, d_s, dt_x, e_a_x, d_s_x, cd_x = _ssd_common(
            dtraw, dtb_ref[...], dtbt_ref[...], al_ref[...], alt_ref[...], e_mat, valid_col, valid_row, x_terms=1)
        xs = xbc_ref[:, 0:D_B].astype(F32)
        bs = xbc_ref[:, D_B:D_B + N_GROUPS * N_STATE]
        cs = xbc_ref[:, D_B + N_GROUPS * N_STATE:N_XBC]
        xdt = xs * dt_x
        xdt_b = xdt.astype(BF16)
        st_b = st_ref[...]
        dst = ds_scr[...]
        dst_b = dst.astype(BF16)

        ys = ys_ref[...].astype(F32)
        bz = bz_ref[...].astype(F32)
        sil = _silu(bz)
        z = ys * sil
        dz, dgt = _rms_bwd_math(z, g_ref[...], dy_ref[...].astype(F32))
        sums.setdefault("dg", []).append(jnp.sum(dgt, axis=0, keepdims=True))
        stores.append((dpw_ref, (slice(None), slice(0, D_B)), (dz * ys * _dsilu(bz)).astype(BF16)))
        dys = dz * sil

        dd_lane = jnp.sum(dys * xs, axis=0, keepdims=True)
        dxs = dys * dx_ref[...]
        w_x = dys * e_a_x
        w_b = w_x.astype(BF16)
        dys_b = dys.astype(BF16)
        u_b = (xdt * d_s_x).astype(BF16)
        dxdt_parts, dbs_parts, dcs_parts, off_parts, g1_parts = [], [], [], [], []
        da_diag = jnp.zeros((CHUNK, LANE), F32)
        lane = lax.broadcasted_iota(jnp.int32, (1, LANE), 1)
        for g in range(N_GROUPS):
            gs = slice(g * N_STATE, (g + 1) * N_STATE)
            gw = slice(g * GROUP_W, (g + 1) * GROUP_W)
            cs_g, bs_g = cs[:, gs], bs[:, gs]
            dcs = lax.dot_general(w_b[:, gw], st_b[:, gw], NT_DIMS, preferred_element_type=F32)
            y_off = jnp.dot(cs_g, st_b[:, gw], preferred_element_type=F32)
            off_parts.append(y_off)
            dst_new = lax.dot_general(cs_g, w_b[:, gw], TN_DIMS, preferred_element_type=F32)
            g1 = jnp.dot(bs_g, dst_b[:, gw], preferred_element_type=F32)
            g1_parts.append(g1)
            dbs = lax.dot_general(u_b[:, gw], dst_b[:, gw], NT_DIMS, preferred_element_type=F32)
            cb = lax.dot_general(cs_g, bs_g, NT_DIMS, preferred_element_type=F32)
            cbt = lax.dot_general(bs_g, cs_g, NT_DIMS, preferred_element_type=F32)
            dcb = jnp.zeros((CHUNK, CHUNK), F32)
            dcbt = jnp.zeros((CHUNK, CHUNK), F32)
            dxdt_h = []
            for e in range(hpg):
                h = g * hpg + e
                hs = slice(h * HEAD_DIM, (h + 1) * HEAD_DIM)
                dec = _decay(big_a, big_a_t, h, False)
                dect = _decay(big_a, big_a_t, h, True)
                m = cb * dec
                mt = cbt * dect
                dxdt_h.append(jnp.dot(mt.astype(BF16), dys_b[:, hs], preferred_element_type=F32))
                dm = lax.dot_general(dys_b[:, hs], xdt_b[:, hs], NT_DIMS, preferred_element_type=F32)
                dmt = lax.dot_general(xdt_b[:, hs], dys_b[:, hs], NT_DIMS, preferred_element_type=F32)
                dcb = dcb + dm * dec
                dcbt = dcbt + dmt * dect
                da_h = jnp.sum(dm * m - dmt * mt, axis=1, keepdims=True)
                da_diag = da_diag + jnp.where(lane == h, da_h, 0.0)
                yield
            dcs = dcs + jnp.dot(dcb.astype(BF16), bs_g, preferred_element_type=F32)
            dbs = dbs + jnp.dot(dcbt.astype(BF16), cs_g, preferred_element_type=F32)
            dxdt_parts.append(jnp.concatenate(dxdt_h, axis=1) + g1 * d_s_x[:, gw])
            dbs_parts.append(dbs)
            dcs_parts.append(dcs)
            stores.append((ds_scr, (slice(None), gw), dst[:, gw] * cd_x[:, gw] + dst_new))
            yield
        dxdt = jnp.concatenate(dxdt_parts, axis=1)
        y_off = jnp.concatenate(off_parts, axis=1)
        g1 = jnp.concatenate(g1_parts, axis=1)
        dcd_lane = jnp.sum(dst * st_b.astype(F32), axis=0, keepdims=True)
        vecs = jnp.concatenate([jnp.broadcast_to(dcd_lane, (SUB, D_B)), jnp.broadcast_to(dd_lane, (SUB, D_B))], axis=0)
        dds, da_off, ddt_x, vec_sums = _split_dot_many(
            [g1 * xdt, w_x * y_off, dxdt * xs, vecs], [1, 1, 1, 1], e_mat_t)
        dcd = vec_sums[0:1]
        sums.setdefault("dd", []).append(vec_sums[SUB:SUB + 1])
        yield
        t_ds = dds * d_s
        d_a = da_diag + da_off - t_ds
        last_row = jnp.sum(t_ds, axis=0, keepdims=True) + dcd * e_a[CHUNK - 1:CHUNK, :]
        d_a = d_a + jnp.where(rows == CHUNK - 1, last_row, 0.0)
        dda = _split_dot_left(_tri(False).astype(BF16), d_a, 3)
        ddt = dda * a + ddt_x
        sums.setdefault("dal", []).append(jnp.sum(dda * dt * a, axis=0, keepdims=True))
        ddtraw = jnp.where(valid_col, ddt * jax.nn.sigmoid(dtraw + dtb_ref[...]), 0.0)
        ddtraw = jnp.where(lane < N_HEADS, ddtraw, 0.0)
        sums.setdefault("ddtb", []).append(jnp.sum(ddtraw, axis=0, keepdims=True))
        stores.append((dpw_ref, (slice(None), slice(D_B, D_B + LANE)), ddtraw.astype(BF16)))
        dxs = dxs + dxdt * dt_x
        dxbc = jnp.concatenate([dxs] + dbs_parts + dcs_parts, axis=1)
        stores.append((dxbc_ref, everything, jnp.where(valid_col, dxbc, 0.0).astype(BF16)))

    def row(width, col):
        return pl.BlockSpec((nb, CHUNK, width), lambda c: (0, cpe - 1 - c, col))

    def const(shape):
        return pl.BlockSpec(shape, lambda c: (0,) * len(shape))

    proj3 = proj.reshape(nb, lp, N_INP)
    dxbc, dproj, *rest = _grid_call(
        body, name, (cpe,),
        [row(D_B, 0), row(D_B, 0), row(N_XBC, 0), row(D_B, COL_BZ // D_B), row(LANE, COL_DT // LANE),
         pl.BlockSpec((nb, None, N_STATE, D_B), lambda c: (0, cpe - 1 - c, 0, 0)),
         const((1, LANE)), const((N_HEADS, 1)), const((1, LANE)), const((N_HEADS, 1)),
         const((1, D_B)), const((1, D_B)), const((LANE, D_B)), const((D_B, LANE))],
        [row(N_XBC, 0),
         pl.BlockSpec((pl.Element(nb), pl.Element(CHUNK), pl.Element(D_B + LANE)),
                      lambda c: (0, (cpe - 1 - c) * CHUNK, COL_BZ)),
         const((1, D_B)), const((1, LANE)), const((1, LANE)), const((1, LANE))],
        [jax.ShapeDtypeStruct((nb, lp, N_XBC), BF16), jax.ShapeDtypeStruct((nb, lp, N_INP), BF16),
         jax.ShapeDtypeStruct((1, D_B), F32),
         jax.ShapeDtypeStruct((1, LANE), F32), jax.ShapeDtypeStruct((1, LANE), F32),
         jax.ShapeDtypeStruct((1, LANE), F32)],
        [pltpu.VMEM((nb, N_STATE, D_B), F32)],
        (dy.reshape(nb, lp, -1), y_ssd.reshape(nb, lp, D_B), xbc.reshape(nb, lp, N_XBC), proj3, proj3, states,
         dtb_p, dtb_t, alog_p, alog_t, d_skip_x, norm_g, e_mat, e_mat_t), comm)
    return (dxbc.reshape(t, N_XBC), dproj.reshape(t, N_INP), *rest)


HBM_SPEC = pl.BlockSpec(memory_space=pl.ANY)


def _mesh_pos():
    return lax.axis_index("x"), lax.axis_index("y"), lax.axis_index("c")


def _remote(src, dst, send_sem, recv_sem, to):
    return pltpu.make_async_remote_copy(src_ref=src, dst_ref=dst, send_sem=send_sem, recv_sem=recv_sem,
                                        device_id=to, device_id_type=MESH)


def _slot_plain(d):
    return d


def _slot_mix_rows(d):
    return jnp.where(d < 2, d + 4, jnp.where(d < 6, d - 2, d))


def _ag_ici(pieces, slots):
    n = len(pieces)

    def copies(ins, outs, sems, with_recvs):
        send_sems, recv_sems, local_sems = sems
        x, y, c = _mesh_pos()
        local, sends, recvs = [], [], []
        for a in range(n):
            mine = outs[a].at[slots[a](4 * x + 2 * y + c)]
            local.append(pltpu.make_async_copy(ins[a], mine, local_sems.at[a]))
            for k, (px, py) in enumerate([(1 - x, y), (x, 1 - y), (1 - x, 1 - y)]):
                sends.append(_remote(ins[a], mine, send_sems.at[a, k], recv_sems.at[a, k], (px, py, c)))
                if with_recvs:
                    theirs = outs[a].at[slots[a](4 * px + 2 * py + c)]
                    recvs.append(_remote(ins[a], theirs, send_sems.at[a, k], recv_sems.at[a, k], (px, py, c)))
        return local, sends, recvs

    return _Comm(pieces, [jax.ShapeDtypeStruct((N_DEV,) + p.shape, p.dtype) for p in pieces],
                 [pltpu.SemaphoreType.DMA((n, 3)), pltpu.SemaphoreType.DMA((n, 3)), pltpu.SemaphoreType.DMA((n,))], copies)


def _ag_d2d(bufs, slots):
    n = len(bufs)

    def copies(ins, outs, sems, with_recvs):
        send_sems, recv_sems = sems
        x, y, c = _mesh_pos()
        chips = [(x, y), (1 - x, y), (x, 1 - y), (1 - x, 1 - y)]
        sends, recvs = [], []
        for a in range(n):
            for k, (px, py) in enumerate(chips):
                held = slots[a](4 * px + 2 * py + c)
                sends.append(_remote(ins[a].at[held], outs[a].at[held], send_sems.at[a, k], recv_sems.at[a, k], (x, y, 1 - c)))
                if with_recvs:
                    got = slots[a](4 * px + 2 * py + 1 - c)
                    recvs.append(_remote(ins[a].at[got], outs[a].at[got], send_sems.at[a, k], recv_sems.at[a, k], (x, y, 1 - c)))
        return [], sends, recvs

    return _Comm(bufs, [jax.ShapeDtypeStruct(b.shape, b.dtype) for b in bufs],
                 [pltpu.SemaphoreType.DMA((n, 4)), pltpu.SemaphoreType.DMA((n, 4))], copies,
                 aliases={a: a for a in range(n)})


def _rs_d2d(blocks, slots):
    n = len(blocks)

    def copies(ins, outs, sems, with_recvs):
        send_sems, recv_sems = sems
        x, y, c = _mesh_pos()
        sends, recvs = [], []
        for a in range(n):
            for j in range(4):
                src = ins[a].at[slots[a](2 * j + 1 - c)]
                sends.append(_remote(src, outs[a].at[j], send_sems.at[a, j], recv_sems.at[a, j], (x, y, 1 - c)))
                if with_recvs:
                    recvs.append(_remote(src, outs[a].at[j], send_sems.at[a, j], recv_sems.at[a, j], (x, y, 1 - c)))
        return [], sends, recvs

    return _Comm(blocks, [jax.ShapeDtypeStruct((4,) + b.shape[1:], b.dtype) for b in blocks],
                 [pltpu.SemaphoreType.DMA((n, 4)), pltpu.SemaphoreType.DMA((n, 4))], copies)


def _rs_ici(blocks):
    n = len(blocks)

    def copies(ins, outs, sems, with_recvs):
        send_sems, recv_sems = sems
        x, y, c = _mesh_pos()
        sends, recvs = [], []
        for a in range(n):
            for k, (px, py) in enumerate([(1 - x, y), (x, 1 - y), (1 - x, 1 - y)]):
                src = ins[a].at[2 * px + py]
                sends.append(_remote(src, outs[a].at[k], send_sems.at[a, k], recv_sems.at[a, k], (px, py, c)))
                if with_recvs:
                    recvs.append(_remote(src, outs[a].at[k], send_sems.at[a, k], recv_sems.at[a, k], (px, py, c)))
        return [], sends, recvs

    return _Comm(blocks, [jax.ShapeDtypeStruct((3,) + b.shape[1:], b.dtype) for b in blocks],
                 [pltpu.SemaphoreType.DMA((n, 3)), pltpu.SemaphoreType.DMA((n, 3))], copies)


def _run_comm(comm, name):
    n_in, n_out = len(comm.inputs), len(comm.out_shapes)

    def body(*refs):
        ins, outs, sems = refs[:n_in], refs[n_in:n_in + n_out], refs[n_in + n_out:]
        comm.start(ins, outs, sems)
        comm.wait(ins, outs, sems)

    return pl.pallas_call(
        body, name=name, in_specs=[HBM_SPEC] * n_in, out_specs=[HBM_SPEC] * n_out, out_shape=comm.out_shapes,
        scratch_shapes=comm.sem_shapes, input_output_aliases=comm.aliases,
    )(*comm.inputs)


W_ROWS = 784


def _w_segments():
    per = N_IN // N_DEV
    out = []
    for d in range(N_DEV):
        lo, hi = per * d, per * (d + 1)
        for a, b, start in COL_MAP:
            s, e = max(lo, a), min(hi, b)
            if s < e:
                out.append((d, s - lo, e - s, s - a + start))
    return out


def _w_gather_rows(g, name):
    tk = D_MODEL // 2
    u32 = jnp.uint32

    def body(g_ref, o_ref, scr):
        for d in range(N_DEV):
            x32 = pltpu.bitcast(g_ref[d], u32)
            for dd, src, rows, dst in _w_segments():
                if dd == d:
                    scr[dst // 2:(dst + rows) // 2, :] = x32[src // 2:(src + rows) // 2]
        scr[(COL_DT + N_HEADS) // 2:COL_XBC // 2, :] = jnp.zeros(((COL_XBC - COL_DT - N_HEADS) // 2, tk), u32)
        o_ref[...] = pltpu.bitcast(scr[...], BF16)

    return pl.pallas_call(
        body, name=name, grid=(D_MODEL // tk,),
        in_specs=[pl.BlockSpec((N_DEV, W_ROWS, tk), lambda j: (0, 0, j))],
        out_specs=pl.BlockSpec((N_INP, tk), lambda j: (0, j)),
        out_shape=jax.ShapeDtypeStruct((N_INP, D_MODEL), BF16),
        scratch_shapes=[pltpu.VMEM((N_INP // 2, tk), u32)],
        compiler_params=_cparams(("parallel",)),
    )(g)


def _w_split_rows(dwt, name):
    tk = D_MODEL // 4
    per = N_IN // N_DEV

    def body(w_ref, o_ref):
        for d, dst, rows, src in _w_segments():
            o_ref[d, dst // 2:(dst + rows) // 2, :] = pltpu.bitcast(w_ref[src:src + rows, :].astype(BF16), F32)
        for d in range(N_DEV):
            o_ref[d, per // 2:W_ROWS // 2, :] = jnp.zeros(((W_ROWS - per) // 2, tk), F32)

    return pl.pallas_call(
        body, name=name, grid=(D_MODEL // tk,),
        in_specs=[pl.BlockSpec((N_INP, tk), lambda j: (0, j))],
        out_specs=pl.BlockSpec((N_DEV, W_ROWS // 2, tk), lambda j: (0, 0, j)),
        out_shape=jax.ShapeDtypeStruct((N_DEV, W_ROWS // 2, D_MODEL), F32),
        compiler_params=_cparams(("parallel",)),
    )(dwt)


def _rs_pair_sum(g, ra, own_slots, name):
    packed = g.dtype == F32
    _, rows, cols = g.shape
    out_rows = 2 * rows if packed else rows

    def body(s_ref, g_ref, ra_ref, o_ref):
        a, b = g_ref[...], ra_ref[...]
        if packed:
            a, b = pltpu.bitcast(a, BF16), pltpu.bitcast(b, BF16)
        o_ref[...] = (a.astype(F32) + b.astype(F32)).astype(BF16)

    return pl.pallas_call(
        body, name=name,
        grid_spec=pltpu.PrefetchScalarGridSpec(
            num_scalar_prefetch=1, grid=(4,),
            in_specs=[pl.BlockSpec((None, rows, cols), lambda j, s: (s[j], 0, 0)),
                      pl.BlockSpec((None, rows, cols), lambda j, s: (j, 0, 0))],
            out_specs=pl.BlockSpec((None, out_rows, cols), lambda j, s: (j, 0, 0))),
        out_shape=jax.ShapeDtypeStruct((4, out_rows, cols), BF16),
        compiler_params=_cparams(("parallel",)),
    )(own_slots, g, ra)


def _rs_final_sum(h, rb, chip_idx, name):
    _, rows, cols = h.shape

    def body(j_ref, h_ref, rb_ref, o_ref):
        o_ref[...] = ((h_ref[...].astype(F32) + rb_ref[0].astype(F32)) + rb_ref[1].astype(F32)) + rb_ref[2].astype(F32)

    return pl.pallas_call(
        body, name=name,
        grid_spec=pltpu.PrefetchScalarGridSpec(
            num_scalar_prefetch=1, grid=(1,),
            in_specs=[pl.BlockSpec((None, rows, cols), lambda i, j: (j[0], 0, 0)),
                      pl.BlockSpec((3, rows, cols), lambda i, j: (0, 0, 0))],
            out_specs=pl.BlockSpec((rows, cols), lambda i, j: (0, 0))),
        out_shape=jax.ShapeDtypeStruct((rows, cols), F32),
        compiler_params=_cparams(("arbitrary",)),
    )(chip_idx, h, rb)


def _adamw_math(w, g, m, v):
    m = ADAM_B1 * m + (1.0 - ADAM_B1) * g
    v = ADAM_B2 * v + (1.0 - ADAM_B2) * (g * g)
    m_hat = m / (1.0 - ADAM_B1 ** ADAM_STEP)
    v_hat = v / (1.0 - ADAM_B2 ** ADAM_STEP)
    delta = -ADAM_LR * (m_hat / (jnp.sqrt(v_hat) + ADAM_EPS) + ADAM_WD * w)
    return delta, m, v


def _adamw_rows(g, w, m, v, tr, name):
    layers, rows, cols = w.shape

    def body(g_ref, w_ref, m_ref, v_ref, d_out, m_out, v_out):
        delta, m_new, v_new = _adamw_math(w_ref[...], g_ref[...], m_ref[...], v_ref[...])
        d_out[...] = delta
        m_out[...] = m_new
        v_out[...] = v_new

    blk = pl.BlockSpec((None, tr, cols), lambda a, r: (a, r, 0))
    return pl.pallas_call(
        body, name=name, grid=(layers, rows // tr),
        in_specs=[blk] * 4, out_specs=[blk] * 3,
        out_shape=[jax.ShapeDtypeStruct((layers, rows, cols), F32)] * 3,
        compiler_params=_cparams(("parallel", "parallel")),
    )(g, w, m, v)


def _adamw_cols(g_t, w, m, v, name):
    layers, k, cols = w.shape
    tr = 56
    assert g_t.shape[1] % tr == 0
    views = [jnp.transpose(a, (2, 0, 1)) for a in (w, m, v)]

    def body(g_ref, w_ref, m_ref, v_ref, g_out, d_out, m_out, v_out):
        for a in range(layers):
            g = g_ref[a]
            delta, m_new, v_new = _adamw_math(w_ref[:, a, :], g, m_ref[:, a, :], v_ref[:, a, :])
            g_out[:, a, :] = g
            d_out[:, a, :] = delta
            m_out[:, a, :] = m_new
            v_out[:, a, :] = v_new

    col = pl.BlockSpec((tr, layers, k), lambda r: (r, 0, 0))
    outs = pl.pallas_call(
        body, name=name, grid=(pl.cdiv(cols, tr),),
        in_specs=[pl.BlockSpec((layers, tr, k), lambda r: (0, r, 0)), col, col, col], out_specs=[col] * 4,
        out_shape=[jax.ShapeDtypeStruct((cols, layers, k), F32)] * 4,
        compiler_params=_cparams(("parallel",)),
    )(g_t, *views)
    return [jnp.transpose(o, (1, 2, 0)) for o in outs]


def _sum_devices(parts, name):
    _, p, _ = parts.shape

    def body(x_ref, o_ref):
        acc = x_ref[0]
        for d in range(1, N_DEV):
            acc = acc + x_ref[d]
        o_ref[...] = acc

    return pl.pallas_call(
        body, name=name, grid=(1,),
        in_specs=[pl.BlockSpec((N_DEV, p, LANE), lambda i: (0, 0, 0))],
        out_specs=pl.BlockSpec((p, LANE), lambda i: (0, 0)),
        out_shape=jax.ShapeDtypeStruct((p, LANE), F32),
        compiler_params=_cparams(("arbitrary",)),
    )(parts)


def _adamw_small(g, w, m, v, name):
    p = g.shape[0]

    def body(g_ref, w_ref, m_ref, v_ref, d_out, m_out, v_out):
        delta, m_new, v_new = _adamw_math(w_ref[...], g_ref[...], m_ref[...], v_ref[...])
        d_out[...] = delta
        m_out[...] = m_new
        v_out[...] = v_new

    spec = pl.BlockSpec((p, LANE), lambda i: (0, 0))
    return pl.pallas_call(
        body, name=name, grid=(1,),
        in_specs=[spec] * 4, out_specs=[spec] * 3,
        out_shape=[jax.ShapeDtypeStruct((p, LANE), F32)] * 3,
        compiler_params=_cparams(("arbitrary",)),
    )(g, w, m, v)


PACK_ALIGN = SUB * LANE

SMALL_PARAMS = (
    ("meta", (N_META, D_MODEL), 1),
    ("pre_g", (DEPTH, D_MODEL), None),
    ("post_g", (DEPTH, D_MODEL), None),
    ("conv_a_w", (DEPTH, CONV_A_K, D_A), 2),
    ("ssm_conv_w", (DEPTH, SSM_K, N_XBC), 2),
    ("ssm_conv_b", (DEPTH, N_XBC), None),
    ("dt_bias", (DEPTH, N_HEADS), None),
    ("a_log", (DEPTH, N_HEADS), None),
    ("d_skip", (DEPTH, N_HEADS), None),
    ("ssm_norm_g", (DEPTH, D_B), None),
    ("conf_conv_w", (DEPTH, CONF_K, D_C), 2),
    ("conf_conv_b", (DEPTH, D_C), None),
    ("conf_ln_g", (DEPTH, D_C), None),
    ("conf_ln_b", (DEPTH, D_C), None),
)


def _local_shape(shape, axis):
    if axis is None:
        return shape
    return tuple(s // N_DEV if k == axis else s for k, s in enumerate(shape))


def _pack(arrays):
    flat = []
    for a in arrays:
        v = a.reshape(-1).astype(F32)
        flat.append(v)
        if v.shape[0] % PACK_ALIGN:
            flat.append(jnp.zeros(((-v.shape[0]) % PACK_ALIGN,), F32))
    return jnp.concatenate(flat).reshape(-1, LANE)


def _unpack(buf, shapes):
    flat = buf.reshape(-1)
    out, off = [], 0
    for s in shapes:
        size = 1
        for k in s:
            size *= k
        out.append(flat[off:off + size].reshape(s))
        off += size + (-size) % PACK_ALIGN
    return out


def kernel(x, meta, pre_g, post_g, w_in, w_out, conv_a_w, ssm_conv_w, ssm_conv_b, dt_bias, a_log, d_skip, ssm_norm_g, conf_conv_w, conf_conv_b, conf_ln_g, conf_ln_b, loss_target, m_meta, m_pre_g, m_post_g, m_w_in, m_w_out, m_conv_a_w, m_ssm_conv_w, m_ssm_conv_b, m_dt_bias, m_a_log, m_d_skip, m_ssm_norm_g, m_conf_conv_w, m_conf_conv_b, m_conf_ln_g, m_conf_ln_b, v_meta, v_pre_g, v_post_g, v_w_in, v_w_out, v_conv_a_w, v_ssm_conv_w, v_ssm_conv_b, v_dt_bias, v_a_log, v_d_skip, v_ssm_norm_g, v_conf_conv_w, v_conf_conv_b, v_conf_ln_g, v_conf_ln_b):
    weights = dict(meta=meta, pre_g=pre_g, post_g=post_g, conv_a_w=conv_a_w, ssm_conv_w=ssm_conv_w, ssm_conv_b=ssm_conv_b,
                   dt_bias=dt_bias, a_log=a_log, d_skip=d_skip, ssm_norm_g=ssm_norm_g, conf_conv_w=conf_conv_w,
                   conf_conv_b=conf_conv_b, conf_ln_g=conf_ln_g, conf_ln_b=conf_ln_b)
    mom1 = dict(meta=m_meta, pre_g=m_pre_g, post_g=m_post_g, conv_a_w=m_conv_a_w, ssm_conv_w=m_ssm_conv_w,
                ssm_conv_b=m_ssm_conv_b, dt_bias=m_dt_bias, a_log=m_a_log, d_skip=m_d_skip, ssm_norm_g=m_ssm_norm_g,
                conf_conv_w=m_conf_conv_w, conf_conv_b=m_conf_conv_b, conf_ln_g=m_conf_ln_g, conf_ln_b=m_conf_ln_b)
    mom2 = dict(meta=v_meta, pre_g=v_pre_g, post_g=v_post_g, conv_a_w=v_conv_a_w, ssm_conv_w=v_ssm_conv_w,
                ssm_conv_b=v_ssm_conv_b, dt_bias=v_dt_bias, a_log=v_a_log, d_skip=v_d_skip, ssm_norm_g=v_ssm_norm_g,
                conf_conv_w=v_conf_conv_w, conf_conv_b=v_conf_conv_b, conf_ln_g=v_conf_ln_g, conf_ln_b=v_conf_ln_b)
    nb, seq, d = x.shape
    lp = PAD + N_META + seq
    t = nb * lp
    assert lp % TILE == 0 and t % (3 * LANE) == 0 and d == D_MODEL
    rt = lp // 3
    xi, yi, ci = _mesh_pos()
    dev = 4 * xi + 2 * yi + ci
    ci32 = ci.astype(jnp.int32)
    chip_idx = (2 * xi + yi).astype(jnp.int32).reshape(1)
    own_plain = jnp.stack([2 * j + ci32 for j in range(4)])
    own_mix = jnp.stack([_slot_mix_rows(2 * j + ci32) for j in range(4)]).astype(jnp.int32)
    n_in_loc = N_IN // N_DEV
    n_out_loc = 2 * D_MODEL // N_DEV
    slots = (_slot_plain, _slot_mix_rows)

    sharded_small = [n for n, _, ax in SMALL_PARAMS if ax is not None]
    small_shapes = {n: s for n, s, _ in SMALL_PARAMS}
    small_axis = {n: ax for n, _, ax in SMALL_PARAMS}
    sw_pack = _pack([weights[n] for n in sharded_small])
    wt_loc = jnp.pad(jnp.swapaxes(w_in, 1, 2).astype(BF16), ((0, 0), (0, W_ROWS - n_in_loc), (0, 0)))
    wo_loc = w_out.astype(BF16)
    plain2 = (_slot_plain, _slot_plain)
    head = _run_comm(_ag_ici([wt_loc[0], sw_pack], plain2), "ag_ici")
    g_in_t, sw_g = _run_comm(_ag_d2d(head, plain2), "ag_d2d")

    def gathered(bufs):
        return _w_gather_rows(bufs[0], "w_gather_rows"), bufs[1].reshape(2 * D_MODEL, D_MODEL)

    full = dict(weights)
    per_dev = [_unpack(sw_g[k], [_local_shape(small_shapes[n], small_axis[n]) for n in sharded_small]) for k in range(N_DEV)]
    for q, n in enumerate(sharded_small):
        full[n] = jnp.concatenate([per_dev[k][q] for k in range(N_DEV)], axis=small_axis[n])

    e_mat = (lax.broadcasted_iota(jnp.int32, (LANE, D_B), 0) == lax.broadcasted_iota(jnp.int32, (LANE, D_B), 1) // HEAD_DIM)
    e_mat = e_mat.astype(BF16)
    e_mat_t = e_mat.T

    front = jnp.concatenate([jnp.zeros((PAD, d), F32), full["meta"]], axis=0)
    h = jnp.concatenate([jnp.concatenate([front, x[b]], axis=0) for b in range(nb)], axis=0)
    saved = []
    w_t, w_o = _w_gather_rows(g_in_t, "w_gather_rows"), None
    hn = _rms_fwd(h, pre_g[0].reshape(1, -1), rt, "rms_fwd")
    for i in range(DEPTH):
        row = lambda a: a[i].reshape(1, -1)
        if i == 0:
            proj, wo_buf = _mm(hn, w_t, "nt", BF16, t // 3, MM_TN, D_MODEL, "mm_proj", cols_outer=True,
                               comm=_ag_ici([wo_loc[0]], (_slot_mix_rows,)))
        else:
            proj = _mm(hn, w_t, "nt", BF16, t // 3, MM_TN, D_MODEL, "mm_proj", cols_outer=True)
        ycat = _a_fwd(proj, full["conv_a_w"][i], lp, "a_fwd")
        xbc = _xbc_fwd(proj, full["ssm_conv_w"][i], row(ssm_conv_b), lp, "xbc_fwd")
        ycat, u1 = _c_fwd(proj, full["conf_conv_w"][i], row(conf_conv_b), row(conf_ln_g), row(conf_ln_b), ycat, lp, "c_fwd")
        d_skip_x = jnp.repeat(d_skip[i], HEAD_DIM).reshape(1, D_B)
        nxt = _ag_ici([wt_loc[i + 1], wo_loc[i + 1]], slots) if i + 1 < DEPTH else None
        if i == 0:
            nxt = _merge_comm(nxt, _ag_d2d([wo_buf], (_slot_mix_rows,)))
        ycat, yssd, states, *bufs = _ssd_fwd(xbc, proj, dt_bias[i], a_log[i], d_skip_x, row(ssm_norm_g), e_mat, ycat, lp,
                                             "ssd_fwd", comm=nxt)
        if i == 0:
            w_o = bufs.pop().reshape(2 * D_MODEL, D_MODEL)
        if bufs:
            m, *bufs = _mm(ycat, w_o, "nn", F32, t // 3, D_MODEL, 2 * D_MODEL, "mm_out", comm=_ag_d2d(bufs, slots))
        else:
            m = _mm(ycat, w_o, "nn", F32, t // 3, D_MODEL, 2 * D_MODEL, "mm_out")
        saved.append((h, hn, proj, ycat, yssd, states, xbc, m, d_skip_x, w_t, w_o, u1))
        if i + 1 < DEPTH:
            h, hn = _post_rms_fwd(h, m, row(post_g), pre_g[i + 1].reshape(1, -1), rt, "post_rms_fwd")
        else:
            h = _post_fwd(h, m, row(post_g), rt, "post_fwd")
        if bufs:
            w_t, w_o = gathered(bufs)

    dh, loss_blk = _loss_kernel(h, loss_target.reshape(nb * seq, d), lp, "loss")

    grads = {n: [None] * DEPTH for n, _, _ in SMALL_PARAMS if n != "meta"}
    gt_in, gr_out = [None] * DEPTH, [None] * DEPTH
    dmeta = None
    pending = None

    def rs_pair_sums(blocks, ras):
        return [_rs_pair_sum(blocks[0], ras[0], own_plain, "rs_pair_sum_in"),
                _rs_pair_sum(blocks[1], ras[1], own_mix, "rs_pair_sum_out")]

    def rs_finish(layer, hs, rbs):
        gt_in[layer] = _rs_final_sum(hs[0], rbs[0], chip_idx, "rs_final_sum_in")
        gr_out[layer] = _rs_final_sum(hs[1], rbs[1], chip_idx, "rs_final_sum_out")

    for i in reversed(range(DEPTH)):
        row = lambda a: a[i].reshape(1, -1)
        h_i, hn, proj, ycat, yssd, states, xbc, m, d_skip_x, w_t, w_o, u1 = saved[i]
        dm, grads["post_g"][i] = _post_bwd(dh, m, row(post_g), rt, "post_bwd")
        if pending is not None:
            dy, *ras = _mm(dm, w_o, "nt", BF16, t // 3, D_MODEL, D_MODEL, "mm_dy", cols_outer=True,
                           comm=_rs_d2d(list(pending), slots))
            hs = rs_pair_sums(pending, ras)
        else:
            dy = _mm(dm, w_o, "nt", BF16, t // 3, D_MODEL, D_MODEL, "mm_dy", cols_outer=True)
            hs = None
        dw_out = _mm(ycat, dm, "tn", BF16, D_MODEL, D_MODEL, t // 3, "mm_dwout")
        dxbc, dproj, grads["ssm_norm_g"][i], ddtb, dal, dds, *rbs = _ssd_bwd(
            dy, yssd, xbc, proj, states, dt_bias[i], a_log[i], d_skip_x, row(ssm_norm_g), e_mat, e_mat_t, lp, "ssd_bwd",
            comm=_rs_ici(hs) if hs is not None else None)
        if hs is not None:
            rs_finish(i + 1, hs, rbs)
        grads["dt_bias"][i] = ddtb[:, :N_HEADS]
        grads["a_log"][i] = dal[:, :N_HEADS]
        grads["d_skip"][i] = dds[:, :N_HEADS]
        dproj, grads["conv_a_w"][i] = _a_bwd(dy, proj, full["conv_a_w"][i], dproj, lp, "a_bwd")
        dproj, grads["ssm_conv_w"][i], grads["ssm_conv_b"][i] = _xbc_bwd(
            dxbc, proj, full["ssm_conv_w"][i], row(ssm_conv_b), dproj, lp, "xbc_bwd")
        dproj, grads["conf_conv_w"][i], grads["conf_conv_b"][i], grads["conf_ln_g"][i], grads["conf_ln_b"][i] = _c_bwd(
            dy, proj, u1, full["conf_conv_w"][i], row(conf_conv_b), row(conf_ln_g), row(conf_ln_b), dproj, lp, "c_bwd")
        p_out = dw_out.reshape(N_DEV, n_out_loc, D_MODEL)
        if i > 0:
            dhn = _mm(dproj, w_t, "nn", BF16, MM_TM, D_MODEL, N_INP, "mm_dhn")
            dw_in_t = _mm(dproj, hn, "tn", F32, MM_TN, D_MODEL, t, "mm_dwin")
        else:
            dhn, ra_out = _mm(dproj, w_t, "nn", BF16, MM_TM, D_MODEL, N_INP, "mm_dhn",
                              comm=_rs_d2d([p_out], (_slot_mix_rows,)))
            h_out = _rs_pair_sum(p_out, ra_out, own_mix, "rs_pair_sum_out")
            dw_in_t, rb_out = _mm(dproj, hn, "tn", F32, MM_TN, D_MODEL, t, "mm_dwin", comm=_rs_ici([h_out]))
            gr_out[0] = _rs_final_sum(h_out, rb_out, chip_idx, "rs_final_sum_out")
        dh, grads["pre_g"][i], dmeta = _rms_bwd(dh, dhn, h_i, row(pre_g), lp, rt, "rms_bwd")
        pending = (_w_split_rows(dw_in_t, "w_split_rows"), p_out)
    grad_x = dh.reshape(nb, lp, d)[:, PAD + N_META:]

    names = [n for n, _, _ in SMALL_PARAMS]
    partial = [loss_blk[0:1, 0:1], dmeta] + [jnp.concatenate(grads[n], axis=0) for n in names[1:]]
    part_pack = _pack(partial)
    p_in = pending[0]
    ra_in, part_buf = _run_comm(_merge_comm(_rs_d2d([p_in], (_slot_plain,)), _ag_ici([part_pack], (_slot_plain,))), "rs_d2d")
    h_in = _rs_pair_sum(p_in, ra_in, own_plain, "rs_pair_sum_in")
    rb_in, parts_g = _run_comm(_merge_comm(_rs_ici([h_in]), _ag_d2d([part_buf], (_slot_plain,))), "rs_ici")
    gt_in[0] = _rs_final_sum(h_in, rb_in, chip_idx, "rs_final_sum_in")

    g_w_out = jnp.stack(gr_out)
    big = {"w_in": _adamw_cols(jnp.stack(gt_in), w_in, m_w_in, v_w_in, "adamw_w_in"),
           "w_out": [g_w_out, *_adamw_rows(g_w_out, w_out, m_w_out, v_w_out, n_out_loc, "adamw_w_out")]}

    total = _unpack(_sum_devices(parts_g, "sum_small_grads"), [(1,)] + [small_shapes[n] for n in names])
    loss = total[0][0]
    g_small = {}
    for n, g in zip(names, total[1:]):
        ax = small_axis[n]
        if ax is not None:
            g = lax.dynamic_slice_in_dim(g, dev * (small_shapes[n][ax] // N_DEV), small_shapes[n][ax] // N_DEV, axis=ax)
        g_small[n] = g
    loc_shapes = [_local_shape(small_shapes[n], small_axis[n]) for n in names]
    d_pack, m_pack, v_pack = _adamw_small(_pack([g_small[n] for n in names]), _pack([weights[n] for n in names]),
                                          _pack([mom1[n] for n in names]), _pack([mom2[n] for n in names]), "adamw_small")
    d_small = dict(zip(names, _unpack(d_pack, loc_shapes)))
    m_small = dict(zip(names, _unpack(m_pack, loc_shapes)))
    v_small = dict(zip(names, _unpack(v_pack, loc_shapes)))

    order = ["meta", "pre_g", "post_g", "w_in", "w_out", "conv_a_w", "ssm_conv_w", "ssm_conv_b", "dt_bias", "a_log",
             "d_skip", "ssm_norm_g", "conf_conv_w", "conf_conv_b", "conf_ln_g", "conf_ln_b"]

    def pick(k, small):
        return [big[n][k] if n in big else small[n] for n in order]

    return (loss, grad_x, *pick(0, g_small), *pick(1, d_small), *pick(2, m_small), *pick(3, v_small))
```

```python
import jax
import jax.numpy as jnp
from jax import lax
from jax.experimental import pallas as pl
from jax.experimental.pallas import tpu as pltpu

F32 = jnp.float32
BF16 = jnp.bfloat16

D_MODEL = 1024
DEPTH = 4
SEQ = 2048
CHUNK = 64
N_META = 16
PAD = 48
LP = PAD + N_META + SEQ
D_A = 512
D_B = 1024
D_C = 512
N_HEADS = 16
HEAD_DIM = 64
N_STATE = 128
N_GROUPS = 2
GROUP_W = D_B // N_GROUPS
N_XBC = D_B + 2 * N_GROUPS * N_STATE
CONV_A_K = 3
SSM_K = 4
CONF_K = 31
NORM_EPS = 1e-6
LN_EPS = 1e-5
N_IN = 6160
N_INP = 6272
COL_BZ = 2048
COL_DT = 3072
COL_XBC = 3200
COL_C = 4736
COL_MAP = ((0, 3072, 0), (3072, 4608, COL_XBC), (4608, 4624, COL_DT), (4624, 6160, COL_C))
LANE = 128
SUB = 8

ADAM_LR = 0.001
ADAM_B1 = 0.9
ADAM_B2 = 0.999
ADAM_EPS = 1e-08
ADAM_WD = 0.01
ADAM_STEP = 10

SHORT_TILES = 6
HALO_A = 16
HALO_C = 32
MM_TM = 384
MM_TN = 896
VMEM_LIMIT = 56 * 1024 * 1024

MESH = pl.DeviceIdType.MESH
N_DEV = 8


def _silu(x):
    return x * jax.nn.sigmoid(x)


def _dsilu(x):
    s = jax.nn.sigmoid(x)
    return s * (1.0 + x * (1.0 - s))


def _cparams(sem=None):
    return pltpu.CompilerParams(dimension_semantics=sem, vmem_limit_bytes=VMEM_LIMIT)


def _mm(a, b, mode, out_dtype, tm, tn, tk, name, cols_outer=False, comm=None):
    def ix(f):
        if cols_outer:
            return lambda j, i, q: f(i, j, q)
        return f

    if mode == "nn":
        (m, k), (_, n) = a.shape, b.shape
        a_spec = pl.BlockSpec((tm, tk), ix(lambda i, j, q: (i, q)))
        b_spec = pl.BlockSpec((tk, tn), ix(lambda i, j, q: (q, j)))
        dims = (((1,), (0,)), ((), ()))
    elif mode == "nt":
        (m, k), (n, _) = a.shape, b.shape
        a_spec = pl.BlockSpec((tm, tk), ix(lambda i, j, q: (i, q)))
        b_spec = pl.BlockSpec((tn, tk), ix(lambda i, j, q: (j, q)))
        dims = (((1,), (1,)), ((), ()))
    else:
        (k, m), (_, n) = a.shape, b.shape
        a_spec = pl.BlockSpec((tk, tm), ix(lambda i, j, q: (q, i)))
        b_spec = pl.BlockSpec((tk, tn), ix(lambda i, j, q: (q, j)))
        dims = (((0,), (0,)), ((), ()))
    assert m % tm == 0 and n % tn == 0 and k % tk == 0, (name, a.shape, b.shape)
    nk = k // tk
    grid = (n // tn, m // tm, nk) if cols_outer else (m // tm, n // tn, nk)

    def body(a_ref, b_ref, o_ref, acc_ref):
        part = lax.dot_general(a_ref[...].astype(BF16), b_ref[...].astype(BF16), dims, preferred_element_type=F32)
        if nk == 1:
            o_ref[...] = part.astype(o_ref.dtype)
        else:
            q = pl.program_id(2)

            @pl.when(q == 0)
            def _():
                acc_ref[...] = part

            @pl.when(q > 0)
            def _():
                acc_ref[...] += part

            @pl.when(q == nk - 1)
            def _():
                o_ref[...] = acc_ref[...].astype(o_ref.dtype)

    res = _grid_call(
        body, name, grid, [a_spec, b_spec], [pl.BlockSpec((tm, tn), ix(lambda i, j, q: (i, j)))],
        [jax.ShapeDtypeStruct((m, n), out_dtype)], [pltpu.VMEM((tm, tn) if nk > 1 else (SUB, LANE), F32)], (a, b), comm)
    return res[0] if comm is None else res


def _row_mask(i, tpe, rows):
    r = lax.broadcasted_iota(jnp.int32, (rows, 1), 0)
    return jnp.logical_or((i % tpe) != 0, r >= PAD)


def _rms_fwd(h, g, rt, name):
    t, d = h.shape

    def body(h_ref, g_ref, o_ref):
        x = h_ref[...]
        r = lax.rsqrt(jnp.mean(x * x, axis=-1, keepdims=True) + NORM_EPS)
        o_ref[...] = (x * r * g_ref[...]).astype(BF16)

    return pl.pallas_call(
        body, name=name, grid=(t // rt,),
        in_specs=[pl.BlockSpec((rt, d), lambda i: (i, 0)), pl.BlockSpec((1, d), lambda i: (0, 0))],
        out_specs=pl.BlockSpec((rt, d), lambda i: (i, 0)),
        out_shape=jax.ShapeDtypeStruct((t, d), BF16),
        compiler_params=_cparams(("parallel",)),
    )(h, g)


def _post_fwd(h, m, g, rt, name):
    t, d = h.shape

    def body(h_ref, m_ref, g_ref, o_ref):
        x = m_ref[...]
        r = lax.rsqrt(jnp.mean(x * x, axis=-1, keepdims=True) + NORM_EPS)
        o_ref[...] = h_ref[...] + x * r * g_ref[...]

    row = pl.BlockSpec((rt, d), lambda i: (i, 0))
    return pl.pallas_call(
        body, name=name, grid=(t // rt,),
        in_specs=[row, row, pl.BlockSpec((1, d), lambda i: (0, 0))], out_specs=row,
        out_shape=jax.ShapeDtypeStruct((t, d), F32),
        compiler_params=_cparams(("parallel",)),
    )(h, m, g)


def _post_rms_fwd(h, m, g_post, g_next, rt, name):
    t, d = h.shape

    def body(h_ref, m_ref, gp_ref, gn_ref, o_ref, n_ref):
        x = m_ref[...]
        r = lax.rsqrt(jnp.mean(x * x, axis=-1, keepdims=True) + NORM_EPS)
        y = h_ref[...] + x * r * gp_ref[...]
        o_ref[...] = y
        r2 = lax.rsqrt(jnp.mean(y * y, axis=-1, keepdims=True) + NORM_EPS)
        n_ref[...] = (y * r2 * gn_ref[...]).astype(BF16)

    row = pl.BlockSpec((rt, d), lambda i: (i, 0))
    vec = pl.BlockSpec((1, d), lambda i: (0, 0))
    return pl.pallas_call(
        body, name=name, grid=(t // rt,),
        in_specs=[row, row, vec, vec], out_specs=[row, row],
        out_shape=[jax.ShapeDtypeStruct((t, d), F32), jax.ShapeDtypeStruct((t, d), BF16)],
        compiler_params=_cparams(("parallel",)),
    )(h, m, g_post, g_next)


def _rms_bwd_math(x, g, dy):
    r = lax.rsqrt(jnp.mean(x * x, axis=-1, keepdims=True) + NORM_EPS)
    gdy = dy * g
    dx = r * gdy - x * (r * r * r) * jnp.mean(gdy * x, axis=-1, keepdims=True)
    return dx, dy * x * r


def _post_bwd(dh, m, g, rt, name):
    t, d = dh.shape

    def body(dh_ref, m_ref, g_ref, dm_ref, dg_ref):
        dm, dgt = _rms_bwd_math(m_ref[...], g_ref[...], dh_ref[...])
        dm_ref[...] = dm.astype(BF16)

        @pl.when(pl.program_id(0) == 0)
        def _():
            dg_ref[...] = jnp.zeros_like(dg_ref)

        dg_ref[...] += jnp.sum(dgt, axis=0, keepdims=True)

    row = pl.BlockSpec((rt, d), lambda i: (i, 0))
    vec = pl.BlockSpec((1, d), lambda i: (0, 0))
    return pl.pallas_call(
        body, name=name, grid=(t // rt,),
        in_specs=[row, row, vec], out_specs=[row, vec],
        out_shape=[jax.ShapeDtypeStruct((t, d), BF16), jax.ShapeDtypeStruct((1, d), F32)],
        compiler_params=_cparams(("arbitrary",)),
    )(dh, m, g)


def _rms_bwd(dh_res, dhn, h, g, lp, rt, name):
    t, d = h.shape
    tpe = lp // rt

    def body(dr_ref, dn_ref, h_ref, g_ref, dh_ref, dg_ref, dmeta_ref):
        i = pl.program_id(0)
        dx, dgt = _rms_bwd_math(h_ref[...], g_ref[...], dn_ref[...].astype(F32))
        dh = dr_ref[...] + dx
        dh_ref[...] = dh

        @pl.when(i == 0)
        def _():
            dg_ref[...] = jnp.zeros_like(dg_ref)
            dmeta_ref[...] = jnp.zeros_like(dmeta_ref)

        dg_ref[...] += jnp.sum(dgt, axis=0, keepdims=True)

        @pl.when((i % tpe) == 0)
        def _():
            dmeta_ref[...] += dh[PAD:PAD + N_META, :]

    row = pl.BlockSpec((rt, d), lambda i: (i, 0))
    vec = pl.BlockSpec((1, d), lambda i: (0, 0))
    return pl.pallas_call(
        body, name=name, grid=(t // rt,),
        in_specs=[row, row, row, vec],
        out_specs=[row, vec, pl.BlockSpec((N_META, d), lambda i: (0, 0))],
        out_shape=[jax.ShapeDtypeStruct((t, d), F32), jax.ShapeDtypeStruct((1, d), F32),
                   jax.ShapeDtypeStruct((N_META, d), F32)],
        compiler_params=_cparams(("arbitrary",)),
    )(dh_res, dhn, h, g)


def _loss_kernel(h, target, lp, name):
    t, d = h.shape
    nb = t // lp
    seq = lp - PAD - N_META
    rows = next(r for r in (1024, 512, 256, 128, CHUNK) if seq % r == 0)
    per = seq // rows

    def body(h_ref, t_ref, zero_ref, dh_ref, loss_ref):
        @pl.when(jnp.logical_and(pl.program_id(0) == 0, pl.program_id(1) == 0))
        def _():
            loss_ref[...] = jnp.zeros_like(loss_ref)

        err = h_ref[...] - t_ref[...]
        dh_ref[...] = err * (1.0 / d)
        loss_ref[...] += (0.5 / d) * jnp.sum(err * err)

    win = pl.BlockSpec((pl.Element(rows), pl.Element(d)),
                       lambda b, j: (pl.multiple_of(b * lp + PAD + N_META + j * rows, CHUNK), 0))
    return pl.pallas_call(
        body, name=name, grid=(nb, per),
        in_specs=[win, pl.BlockSpec((rows, d), lambda b, j: (b * per + j, 0)), HBM_SPEC],
        out_specs=[win, pl.BlockSpec((SUB, LANE), lambda b, j: (0, 0))],
        out_shape=[jax.ShapeDtypeStruct((t, d), F32), jax.ShapeDtypeStruct((SUB, LANE), F32)],
        input_output_aliases={2: 0},
        compiler_params=_cparams(("arbitrary", "arbitrary")),
    )(h, target, jnp.zeros((t, d), F32))


def _window_spec(rows, width, col):
    return pl.BlockSpec((pl.Element(rows), pl.Element(width)), lambda i: (i * rows, col))


def _halo_specs(t, width, col, halo, tile):
    cur = _window_spec(tile, width, col)
    prev = pl.BlockSpec((pl.Element(halo), pl.Element(width)),
                        lambda i: (pl.multiple_of(jnp.maximum(i * tile - halo, 0), halo), col))
    nxt = pl.BlockSpec((pl.Element(halo), pl.Element(width)),
                       lambda i: (pl.multiple_of(jnp.minimum((i + 1) * tile, t - halo), halo), col))
    return cur, prev, nxt


def _f32(ref, lo, hi):
    return ref[:, lo:hi].astype(F32)


def _rows_from(x, start, rows):
    s = start % SUB
    if s == 0:
        return x[start:start + rows]
    return pltpu.roll(x, x.shape[0] - s, axis=0)[start - s:start - s + rows]


def _conv_a(ve, w):
    rows = ve.shape[0] - HALO_A
    return (w[0:1] * _rows_from(ve, HALO_A - 2, rows) + w[1:2] * _rows_from(ve, HALO_A - 1, rows)
            + w[2:3] * ve[HALO_A:HALO_A + rows])


def _a_fwd(proj, w, lp, name):
    t = proj.shape[0]
    tile = lp // SHORT_TILES
    tpe = lp // tile
    cur, prev, _ = _halo_specs(t, 4 * D_A, 0, HALO_A, tile)

    def body(p_ref, ph_ref, w_ref, y_ref):
        first = (pl.program_id(0) % tpe) == 0
        v = _f32(p_ref, D_A, 2 * D_A) * _f32(p_ref, 2 * D_A, 3 * D_A)
        vh = jnp.where(first, 0.0, _f32(ph_ref, D_A, 2 * D_A) * _f32(ph_ref, 2 * D_A, 3 * D_A))
        cv = _conv_a(jnp.concatenate([vh, v], axis=0), w_ref[...])
        y_ref[...] = (_f32(p_ref, 0, D_A) * cv * _silu(_f32(p_ref, 3 * D_A, 4 * D_A))).astype(BF16)

    return pl.pallas_call(
        body, name=name, grid=(t // tile,),
        in_specs=[cur, prev, pl.BlockSpec((CONV_A_K, D_A), lambda i: (0, 0))],
        out_specs=_window_spec(tile, D_A, D_B),
        out_shape=jax.ShapeDtypeStruct((t, D_B + D_A + D_C), BF16),
        compiler_params=_cparams(("parallel",)),
    )(proj, proj, w)


def _a_bwd(dy, proj, w, dproj, lp, name):
    t = proj.shape[0]
    tile = lp // SHORT_TILES
    tpe = lp // tile
    cur, prev, nxt = _halo_specs(t, 4 * D_A, 0, HALO_A, tile)
    dcur, _, dnxt = _halo_specs(t, D_A, D_B, HALO_A, tile)

    def body(dy_ref, dyn_ref, p_ref, ph_ref, pn_ref, w_ref, dproj_ref, dp_ref, dw_ref):
        i = pl.program_id(0)
        first = (i % tpe) == 0
        last = (i % tpe) == tpe - 1
        w = w_ref[...]
        ab, ac, ax, az = (_f32(p_ref, k * D_A, (k + 1) * D_A) for k in range(4))
        v = ac * ax
        vh = jnp.where(first, 0.0, _f32(ph_ref, D_A, 2 * D_A) * _f32(ph_ref, 2 * D_A, 3 * D_A))
        ve = jnp.concatenate([vh, v], axis=0)
        taps = [_rows_from(ve, HALO_A - 2 + k, tile) for k in range(CONV_A_K)]
        cv = w[0:1] * taps[0] + w[1:2] * taps[1] + w[2:3] * taps[2]
        s = _silu(az)
        dy_ = dy_ref[...].astype(F32)
        dcv = dy_ * ab * s
        dcvn = jnp.where(last, 0.0, dyn_ref[...].astype(F32) * _f32(pn_ref, 0, D_A) * _silu(_f32(pn_ref, 3 * D_A, 4 * D_A)))
        dce = jnp.concatenate([dcv, dcvn], axis=0)
        dv = w[2:3] * dce[0:tile] + w[1:2] * _rows_from(dce, 1, tile) + w[0:1] * _rows_from(dce, 2, tile)
        dp = jnp.concatenate([dy_ * cv * s, dv * ax, dv * ac, dy_ * ab * cv * _dsilu(az)], axis=1)
        dp_ref[...] = jnp.where(_row_mask(i, tpe, tile), dp, 0.0).astype(BF16)
        dw = jnp.concatenate(
            [jnp.sum(dcv * taps[k], axis=0, keepdims=True) for k in range(CONV_A_K)], axis=0)

        @pl.when(i == 0)
        def _():
            dw_ref[...] = jnp.zeros_like(dw_ref)

        dw_ref[...] += dw

    wspec = pl.BlockSpec((CONV_A_K, D_A), lambda i: (0, 0))
    return pl.pallas_call(
        body, name=name, grid=(t // tile,),
        in_specs=[dcur, dnxt, cur, prev, nxt, wspec, HBM_SPEC],
        out_specs=[_window_spec(tile, 4 * D_A, 0), wspec],
        out_shape=[jax.ShapeDtypeStruct(dproj.shape, dproj.dtype), jax.ShapeDtypeStruct((CONV_A_K, D_A), F32)],
        input_output_aliases={6: 0},
        compiler_params=_cparams(("arbitrary",)),
    )(dy, dy, proj, proj, proj, w, dproj)


def _conv_ssm(xe, w, rows, off):
    acc = w[0:1] * _rows_from(xe, off - 3, rows)
    for k in range(1, SSM_K):
        acc = acc + w[k:k + 1] * _rows_from(xe, off - 3 + k, rows)
    return acc


def _xbc_fwd(proj, w, b, lp, name):
    t = proj.shape[0]
    tile = lp // SHORT_TILES
    tpe = lp // tile
    cur, prev, _ = _halo_specs(t, N_XBC, COL_XBC, HALO_A, tile)

    def body(x_ref, xh_ref, w_ref, b_ref, o_ref):
        first = (pl.program_id(0) % tpe) == 0
        xh = jnp.where(first, 0.0, xh_ref[...].astype(F32))
        xe = jnp.concatenate([xh, x_ref[...].astype(F32)], axis=0)
        o_ref[...] = _silu(_conv_ssm(xe, w_ref[...], tile, HALO_A) + b_ref[...]).astype(BF16)

    return pl.pallas_call(
        body, name=name, grid=(t // tile,),
        in_specs=[cur, prev, pl.BlockSpec((SSM_K, N_XBC), lambda i: (0, 0)), pl.BlockSpec((1, N_XBC), lambda i: (0, 0))],
        out_specs=pl.BlockSpec((tile, N_XBC), lambda i: (i, 0)),
        out_shape=jax.ShapeDtypeStruct((t, N_XBC), BF16),
        compiler_params=_cparams(("parallel",)),
    )(proj, proj, w, b)


def _xbc_bwd(dxbc, proj, w, b, dproj, lp, name):
    t = proj.shape[0]
    tile = lp // SHORT_TILES
    tpe = lp // tile
    cur, prev, nxt = _halo_specs(t, N_XBC, COL_XBC, HALO_A, tile)
    dcur, _, dnxt = _halo_specs(t, N_XBC, 0, HALO_A, tile)

    def body(d_ref, dn_ref, x_ref, xh_ref, xn_ref, w_ref, b_ref, dproj_ref, dx_ref, dw_ref, db_ref):
        i = pl.program_id(0)
        first = (i % tpe) == 0
        last = (i % tpe) == tpe - 1
        w = w_ref[...]
        xh = jnp.where(first, 0.0, xh_ref[...].astype(F32))
        xe = jnp.concatenate([xh, x_ref[...].astype(F32), xn_ref[...].astype(F32)], axis=0)
        taps = [_rows_from(xe, HALO_A - 3 + k, tile + HALO_A) for k in range(SSM_K)]
        pre = b_ref[...] + w[0:1] * taps[0]
        for k in range(1, SSM_K):
            pre = pre + w[k:k + 1] * taps[k]
        de = jnp.concatenate([d_ref[...].astype(F32), jnp.where(last, 0.0, dn_ref[...].astype(F32))], axis=0)
        dpre = de * _dsilu(pre)
        dx = w[3:4] * dpre[0:tile]
        for k in range(SSM_K - 1):
            dx = dx + w[k:k + 1] * _rows_from(dpre, 3 - k, tile)
        dx_ref[...] = jnp.where(_row_mask(i, tpe, tile), dx, 0.0).astype(BF16)
        dpc = dpre[0:tile]
        dw = jnp.concatenate(
            [jnp.sum(dpc * taps[k][0:tile], axis=0, keepdims=True) for k in range(SSM_K)], axis=0)

        @pl.when(i == 0)
        def _():
            dw_ref[...] = jnp.zeros_like(dw_ref)
            db_ref[...] = jnp.zeros_like(db_ref)

        dw_ref[...] += dw
        db_ref[...] += jnp.sum(dpc, axis=0, keepdims=True)

    wspec = pl.BlockSpec((SSM_K, N_XBC), lambda i: (0, 0))
    bspec = pl.BlockSpec((1, N_XBC), lambda i: (0, 0))
    return pl.pallas_call(
        body, name=name, grid=(t // tile,),
        in_specs=[dcur, dnxt, cur, prev, nxt, wspec, bspec, HBM_SPEC],
        out_specs=[_window_spec(tile, N_XBC, COL_XBC), wspec, bspec],
        out_shape=[jax.ShapeDtypeStruct(dproj.shape, dproj.dtype), jax.ShapeDtypeStruct((SSM_K, N_XBC), F32),
                   jax.ShapeDtypeStruct((1, N_XBC), F32)],
        input_output_aliases={7: 0},
        compiler_params=_cparams(("arbitrary",)),
    )(dxbc, dxbc, proj, proj, proj, w, b, dproj)


SUBROWS = 32


def _fill_shifted(scr, x):
    scr[0] = x
    for s in range(1, SUB):
        scr[s] = pltpu.roll(x, x.shape[0] - s, axis=0)


def _window(scr, start, rows):
    s = start % SUB
    return scr[s, start - s:start - s + rows, :]


def _conv_conf(scr, w, rows, off, base):
    acc = w[0:1] * _window(scr, base + off - (CONF_K - 1), rows)
    for k in range(1, CONF_K):
        acc = acc + w[k:k + 1] * _window(scr, base + off - (CONF_K - 1) + k, rows)
    return acc


def _ln_fwd(u1, g, b):
    mu = jnp.mean(u1, axis=-1, keepdims=True)
    xc = u1 - mu
    rstd = lax.rsqrt(jnp.mean(xc * xc, axis=-1, keepdims=True) + LN_EPS)
    n = xc * rstd
    return n, rstd, n * g + b


def _c_fwd(proj, w, cb, g, b, ybuf, lp, name):
    t = proj.shape[0]
    tile = lp // SHORT_TILES
    tpe = lp // tile
    cur, prev, _ = _halo_specs(t, 3 * D_C, COL_C, HALO_C, tile)

    def body(p_ref, ph_ref, w_ref, cb_ref, g_ref, b_ref, ybuf_ref, y_ref, u1_ref, u0_scr):
        first = (pl.program_id(0) % tpe) == 0
        u0h = jnp.where(first, 0.0, _f32(ph_ref, 0, D_C) * jax.nn.sigmoid(_f32(ph_ref, D_C, 2 * D_C)))
        _fill_shifted(u0_scr, jnp.concatenate([u0h, _f32(p_ref, 0, D_C) * jax.nn.sigmoid(_f32(p_ref, D_C, 2 * D_C))], axis=0))
        w = w_ref[...]
        for r0 in range(0, tile, SUBROWS):
            u1 = _conv_conf(u0_scr, w, SUBROWS, HALO_C, r0) + cb_ref[...]
            u1_ref[r0:r0 + SUBROWS, :] = u1.astype(BF16)
            _, _, u2 = _ln_fwd(u1, g_ref[...], b_ref[...])
            cz = p_ref[r0:r0 + SUBROWS, 2 * D_C:3 * D_C].astype(F32)
            y_ref[r0:r0 + SUBROWS, :] = (_silu(u2) * _silu(cz)).astype(BF16)

    vec = pl.BlockSpec((1, D_C), lambda i: (0, 0))
    return pl.pallas_call(
        body, name=name, grid=(t // tile,),
        in_specs=[cur, prev, pl.BlockSpec((CONF_K, D_C), lambda i: (0, 0)), vec, vec, vec, HBM_SPEC],
        out_specs=[_window_spec(tile, D_C, D_B + D_A), pl.BlockSpec((tile, D_C), lambda i: (i, 0))],
        out_shape=[jax.ShapeDtypeStruct(ybuf.shape, ybuf.dtype), jax.ShapeDtypeStruct((t, D_C), BF16)],
        scratch_shapes=[pltpu.VMEM((SUB, HALO_C + tile, D_C), F32)],
        input_output_aliases={6: 0},
        compiler_params=_cparams(("parallel",)),
    )(proj, proj, w, cb, g, b, ybuf)


def _c_bwd(dy, proj, u1, w, cb, g, b, dproj, lp, name):
    t = proj.shape[0]
    tile = lp // SHORT_TILES
    tpe = lp // tile
    cur, prev, nxt = _halo_specs(t, 3 * D_C, COL_C, HALO_C, tile)
    dcur, _, dnxt = _halo_specs(t, D_C, D_B + D_A, HALO_C, tile)
    ucur, _, unxt = _halo_specs(t, D_C, 0, HALO_C, tile)
    ext = tile + HALO_C

    def body(dy_ref, dyn_ref, p_ref, ph_ref, pn_ref, u1_ref, u1n_ref, w_ref, cb_ref, g_ref, b_ref, dproj_ref,
             dp_ref, dw_ref, dcb_ref, dg_ref, db_ref, u0_scr, du1_scr, wacc_scr):
        i = pl.program_id(0)
        first = (i % tpe) == 0
        last = (i % tpe) == tpe - 1
        w = w_ref[...]

        @pl.when(i == 0)
        def _():
            dw_ref[...] = jnp.zeros_like(dw_ref)
            dcb_ref[...] = jnp.zeros_like(dcb_ref)
            dg_ref[...] = jnp.zeros_like(dg_ref)
            db_ref[...] = jnp.zeros_like(db_ref)

        u0h = jnp.where(first, 0.0, _f32(ph_ref, 0, D_C) * jax.nn.sigmoid(_f32(ph_ref, D_C, 2 * D_C)))
        _fill_shifted(u0_scr, jnp.concatenate(
            [u0h, _f32(p_ref, 0, D_C) * jax.nn.sigmoid(_f32(p_ref, D_C, 2 * D_C))], axis=0))
        dcb = jnp.zeros((1, D_C), F32)
        dg = jnp.zeros((1, D_C), F32)
        db = jnp.zeros((1, D_C), F32)
        for r0 in range(0, ext, SUBROWS):
            in_tile = r0 < tile
            src, dsrc, usrc, q0 = (p_ref, dy_ref, u1_ref, r0) if in_tile else (pn_ref, dyn_ref, u1n_ref, r0 - tile)
            u1 = usrc[q0:q0 + SUBROWS, :].astype(F32)
            n, rstd, u2 = _ln_fwd(u1, g_ref[...], b_ref[...])
            cz = src[q0:q0 + SUBROWS, 2 * D_C:3 * D_C].astype(F32)
            dyc = dsrc[q0:q0 + SUBROWS, :].astype(F32)
            if not in_tile:
                dyc = jnp.where(last, 0.0, dyc)
            du2 = dyc * _silu(cz) * _dsilu(u2)
            dn = du2 * g_ref[...]
            du1 = rstd * (dn - jnp.mean(dn, axis=-1, keepdims=True) - n * jnp.mean(dn * n, axis=-1, keepdims=True))
            du1_scr[0, r0:r0 + SUBROWS, :] = du1
            if in_tile:
                dp_ref[r0:r0 + SUBROWS, 2 * D_C:3 * D_C] = (dyc * _silu(u2) * _dsilu(cz)).astype(BF16)
                dg = dg + jnp.sum(du2 * n, axis=0, keepdims=True)
                db = db + jnp.sum(du2, axis=0, keepdims=True)
                dcb = dcb + jnp.sum(du1, axis=0, keepdims=True)
        dcb_ref[...] += dcb
        dg_ref[...] += dg
        db_ref[...] += db
        mask = _row_mask(i, tpe, tile)
        _fill_shifted(du1_scr, du1_scr[0])
        for r0 in range(0, tile, SUBROWS):
            acc = w[0:1] * _window(du1_scr, r0 + CONF_K - 1, SUBROWS)
            for k in range(1, CONF_K):
                acc = acc + w[k:k + 1] * _window(du1_scr, r0 + CONF_K - 1 - k, SUBROWS)
            ca = p_ref[r0:r0 + SUBROWS, 0:D_C].astype(F32)
            sg = jax.nn.sigmoid(p_ref[r0:r0 + SUBROWS, D_C:2 * D_C].astype(F32))
            m = mask[r0:r0 + SUBROWS]
            dp_ref[r0:r0 + SUBROWS, 0:D_C] = jnp.where(m, acc * sg, 0.0).astype(BF16)
            dp_ref[r0:r0 + SUBROWS, D_C:2 * D_C] = jnp.where(m, acc * ca * sg * (1.0 - sg), 0.0).astype(BF16)
        for k in range(CONF_K):
            part = jnp.zeros((SUB, D_C), F32)
            for r0 in range(0, tile, SUBROWS):
                prod = du1_scr[0, r0:r0 + SUBROWS, :] * _window(u0_scr, HALO_C + r0 - (CONF_K - 1) + k, SUBROWS)
                for q in range(0, SUBROWS, SUB):
                    part = part + prod[q:q + SUB]
            wacc_scr[k:k + 1, :] = jnp.sum(part, axis=0, keepdims=True)
        dw_ref[...] += wacc_scr[0:CONF_K, :]

    vec = pl.BlockSpec((1, D_C), lambda i: (0, 0))
    wspec = pl.BlockSpec((CONF_K, D_C), lambda i: (0, 0))
    return pl.pallas_call(
        body, name=name, grid=(t // tile,),
        in_specs=[dcur, dnxt, cur, prev, nxt, ucur, unxt, wspec, vec, vec, vec, HBM_SPEC],
        out_specs=[_window_spec(tile, 3 * D_C, COL_C), wspec, vec, vec, vec],
        out_shape=[jax.ShapeDtypeStruct(dproj.shape, dproj.dtype), jax.ShapeDtypeStruct((CONF_K, D_C), F32),
                   jax.ShapeDtypeStruct((1, D_C), F32), jax.ShapeDtypeStruct((1, D_C), F32),
                   jax.ShapeDtypeStruct((1, D_C), F32)],
        scratch_shapes=[pltpu.VMEM((SUB, ext, D_C), F32), pltpu.VMEM((SUB, ext, D_C), F32),
                        pltpu.VMEM((HALO_C, D_C), F32)],
        input_output_aliases={11: 0},
        compiler_params=_cparams(("arbitrary",)),
    )(dy, dy, proj, proj, proj, u1, u1, w, cb, g, b, dproj)


def _split_dot(x, m_bf16, terms):
    acc = None
    rem = x
    for _ in range(terms):
        hi = rem.astype(BF16)
        part = jnp.dot(hi, m_bf16, preferred_element_type=F32)
        acc = part if acc is None else acc + part
        rem = rem - hi.astype(F32)
    return acc


def _split_pieces(x, terms):
    out, rem = [], x
    for _ in range(terms):
        hi = rem.astype(BF16)
        out.append(hi)
        rem = rem - hi.astype(F32)
    return out


def _split_dot_many(xs, terms, m_bf16):
    pieces = [p for x, n in zip(xs, terms) for p in _split_pieces(x, n)]
    prod = jnp.dot(jnp.concatenate(pieces, axis=0), m_bf16, preferred_element_type=F32)
    out, off = [], 0
    for x, n in zip(xs, terms):
        rows = x.shape[0]
        acc = prod[off:off + rows]
        for q in range(1, n):
            acc = acc + prod[off + q * rows:off + (q + 1) * rows]
        out.append(acc)
        off += n * rows
    return out


def _split_dot_left(m_bf16, x, terms):
    cols = x.shape[1]
    prod = jnp.dot(m_bf16, jnp.concatenate(_split_pieces(x, terms), axis=1), preferred_element_type=F32)
    acc = prod[:, 0:cols]
    for q in range(1, terms):
        acc = acc + prod[:, q * cols:(q + 1) * cols]
    return acc


def _tri(rows_ge_cols):
    r = lax.broadcasted_iota(jnp.int32, (CHUNK, CHUNK), 0)
    c = lax.broadcasted_iota(jnp.int32, (CHUNK, CHUNK), 1)
    return (r >= c) if rows_ge_cols else (r <= c)


def _softplus(x):
    return jnp.maximum(x, 0.0) + jnp.log(1.0 + jnp.exp(-jnp.abs(x)))


def _ssd_common(dtraw, dtb, dtb_t, a_log, a_log_t, e_mat, valid_col, valid_row, x_terms=2):
    a = -jnp.exp(a_log)
    lane = lax.broadcasted_iota(jnp.int32, (1, LANE), 1)
    a = jnp.where(lane < N_HEADS, a, 0.0)
    a_t = -jnp.exp(a_log_t)
    dt = jnp.where(valid_col, _softplus(dtraw + dtb), 0.0)
    dt = jnp.where(lane < N_HEADS, dt, 0.0)
    dt_t = jnp.where(valid_row, _softplus(dtraw.T[0:N_HEADS, :] + dtb_t), 0.0)
    ltri = _tri(True).astype(BF16)
    utri = _tri(False).astype(BF16)
    big_a = _split_dot_left(ltri, dt * a, 3)
    big_a_t = _split_dot_many([dt_t * a_t], [3], utri)[0]
    e_a = jnp.exp(big_a)
    d_s = jnp.exp(big_a[CHUNK - 1:CHUNK, :] - big_a)
    if x_terms == 1:
        dt_x, e_a_x, d_s_x = _split_dot_many([dt, e_a, d_s], [1, 2, 1], e_mat)
    else:
        dt_x, e_a_x, d_s_x = (_split_dot(q, e_mat, x_terms) for q in (dt, e_a, d_s))
    cd_x = e_a_x[CHUNK - 1:CHUNK, :]
    return a, dt, big_a, big_a_t, e_a, d_s, dt_x, e_a_x, d_s_x, cd_x


def _decay(big_a, big_a_t, h, transposed):
    col = big_a[:, h:h + 1]
    row = big_a_t[h:h + 1, :]
    if not transposed:
        seg = col - row
        return jnp.where(_tri(True), jnp.exp(jnp.minimum(seg, 0.0)), 0.0)
    seg = row - col
    return jnp.where(_tri(False), jnp.exp(jnp.minimum(seg, 0.0)), 0.0)


NT_DIMS = (((1,), (1,)), ((), ()))
TN_DIMS = (((0,), (0,)), ((), ()))
HBM_SPEC = pl.BlockSpec(memory_space=pl.ANY)


class _Comm:
    def __init__(self, inputs, out_shapes, sem_shapes, copies, aliases=None):
        self.inputs, self.out_shapes, self.sem_shapes, self.copies = inputs, out_shapes, sem_shapes, copies
        self.aliases = aliases or {}

    def start(self, ins, outs, sems):
        local, sends, _ = self.copies(ins, outs, sems, False)
        for cp in local + sends:
            cp.start()

    def wait(self, ins, outs, sems):
        local, sends, recvs = self.copies(ins, outs, sems, True)
        for cp in recvs:
            cp.wait_recv()
        for cp in sends:
            cp.wait_send()
        for cp in local:
            cp.wait()


def _merge_comm(*comms):
    comms = [c for c in comms if c is not None]
    if len(comms) <= 1:
        return comms[0] if comms else None

    def copies(ins, outs, sems, with_recvs):
        local, sends, recvs = [], [], []
        i0 = o0 = s0 = 0
        for c in comms:
            ni, no, ns = len(c.inputs), len(c.out_shapes), len(c.sem_shapes)
            loc, snd, rcv = c.copies(ins[i0:i0 + ni], outs[o0:o0 + no], sems[s0:s0 + ns], with_recvs)
            local, sends, recvs = local + loc, sends + snd, recvs + rcv
            i0, o0, s0 = i0 + ni, o0 + no, s0 + ns
        return local, sends, recvs

    aliases, i0, o0 = {}, 0, 0
    for c in comms:
        aliases.update({i0 + k: o0 + v for k, v in c.aliases.items()})
        i0, o0 = i0 + len(c.inputs), o0 + len(c.out_shapes)
    return _Comm([a for c in comms for a in c.inputs], [s for c in comms for s in c.out_shapes],
                 [s for c in comms for s in c.sem_shapes], copies, aliases)


def _grid_call(body, name, grid, in_specs, out_specs, out_shape, scratch_shapes, operands, comm=None, aliases=None):
    aliases = dict(aliases or {})
    if comm is None:
        return pl.pallas_call(
            body, name=name, grid=grid, in_specs=in_specs, out_specs=out_specs, out_shape=out_shape,
            scratch_shapes=scratch_shapes, input_output_aliases=aliases,
            compiler_params=_cparams(("arbitrary",) * len(grid)))(*operands)
    n_in, n_out, n_scr = len(in_specs), len(out_specs), len(scratch_shapes)
    nci, nco = len(comm.inputs), len(comm.out_shapes)

    def wrapped(*refs):
        ins, cins = refs[:n_in], refs[n_in:n_in + nci]
        o0 = n_in + nci
        outs, couts = refs[o0:o0 + n_out], refs[o0 + n_out:o0 + n_out + nco]
        s0 = o0 + n_out + nco
        scr, csems = refs[s0:s0 + n_scr], refs[s0 + n_scr:]
        first = pl.program_id(0) == 0
        last = pl.program_id(0) == grid[0] - 1
        for k in range(1, len(grid)):
            first = jnp.logical_and(first, pl.program_id(k) == 0)
            last = jnp.logical_and(last, pl.program_id(k) == grid[k] - 1)

        @pl.when(first)
        def _():
            comm.start(cins, couts, csems)

        body(*ins, *outs, *scr)

        @pl.when(last)
        def _():
            comm.wait(cins, couts, csems)

    res = pl.pallas_call(
        wrapped, name=name, grid=grid,
        in_specs=list(in_specs) + [HBM_SPEC] * nci, out_specs=list(out_specs) + [HBM_SPEC] * nco,
        out_shape=list(out_shape) + list(comm.out_shapes),
        scratch_shapes=list(scratch_shapes) + list(comm.sem_shapes),
        input_output_aliases={**aliases, **{n_in + k: n_out + v for k, v in comm.aliases.items()}},
        compiler_params=_cparams(("arbitrary",) * len(grid)))(*operands, *comm.inputs)
    return res


def _ssd_fwd(xbc, proj, dtb, a_log, d_skip_x, norm_g, e_mat, ybuf, lp, name, comm=None):
    t = xbc.shape[0]
    cpe = lp // CHUNK
    nb = t // lp
    dtb_p = jnp.pad(dtb.reshape(1, N_HEADS), ((0, 0), (0, LANE - N_HEADS)))
    alog_p = jnp.pad(a_log.reshape(1, N_HEADS), ((0, 0), (0, LANE - N_HEADS)))
    dtb_t = dtb.reshape(N_HEADS, 1)
    alog_t = a_log.reshape(N_HEADS, 1)

    def body(xbc_ref, bz_ref, dt_ref, dtb_ref, dtbt_ref, al_ref, alt_ref, dx_ref, g_ref, e_ref, ybuf_ref,
             yb_ref, ys_ref, st_ref, s_scr):
        c = pl.program_id(0)

        @pl.when(c == 0)
        def _():
            s_scr[...] = jnp.zeros_like(s_scr)

        rows = lax.broadcasted_iota(jnp.int32, (CHUNK, 1), 0)
        cols = lax.broadcasted_iota(jnp.int32, (1, CHUNK), 1)
        valid_col = jnp.logical_or(c > 0, rows >= PAD)
        valid_row = jnp.logical_or(c > 0, cols >= PAD)
        e_mat = e_ref[...]
        ex = []
        for b in range(nb):
            _, _, big_a, big_a_t, _, _, dt_x, e_a_x, d_s_x, cd_x = _ssd_common(
                dt_ref[b].astype(F32), dtb_ref[...], dtbt_ref[...], al_ref[...], alt_ref[...], e_mat, valid_col, valid_row)
            xs = xbc_ref[b, :, 0:D_B].astype(F32)
            xdt = xs * dt_x
            st_prev = s_scr[b]
            ex.append(dict(big_a=big_a, big_a_t=big_a_t, e_a_x=e_a_x, cd_x=cd_x, xs=xs,
                           bs=xbc_ref[b, :, D_B:D_B + N_GROUPS * N_STATE], cs=xbc_ref[b, :, D_B + N_GROUPS * N_STATE:N_XBC],
                           xdt_b=xdt.astype(BF16), st_prev=st_prev, st_b=st_prev.astype(BF16),
                           u_b=(xdt * d_s_x).astype(BF16), bz=bz_ref[b].astype(F32), y_parts=[], new_st=[]))
        for g in range(N_GROUPS):
            gs = slice(g * N_STATE, (g + 1) * N_STATE)
            gw = slice(g * GROUP_W, (g + 1) * GROUP_W)
            for q in ex:
                cb = lax.dot_general(q["cs"][:, gs], q["bs"][:, gs], NT_DIMS, preferred_element_type=F32)
                q["y_off"] = jnp.dot(q["cs"][:, gs], q["st_b"][:, gw], preferred_element_type=F32)
                q["cb"] = cb
                q["diag"] = []
            for e in range(N_HEADS // N_GROUPS):
                h = g * (N_HEADS // N_GROUPS) + e
                for q in ex:
                    m = (q["cb"] * _decay(q["big_a"], q["big_a_t"], h, False)).astype(BF16)
                    q["diag"].append(jnp.dot(m, q["xdt_b"][:, h * HEAD_DIM:(h + 1) * HEAD_DIM], preferred_element_type=F32))
            for q in ex:
                q["y_parts"].append(jnp.concatenate(q["diag"], axis=1) + q["y_off"] * q["e_a_x"][:, gw])
                upd = lax.dot_general(q["bs"][:, gs], q["u_b"][:, gw], TN_DIMS, preferred_element_type=F32)
                q["new_st"].append(q["st_prev"][:, gw] * q["cd_x"][:, gw] + upd)
        for b, q in enumerate(ex):
            y = jnp.concatenate(q["y_parts"], axis=1) + q["xs"] * dx_ref[...]
            z = y * _silu(q["bz"])
            r = lax.rsqrt(jnp.mean(z * z, axis=-1, keepdims=True) + NORM_EPS)
            q["y"], q["yb"] = y, (z * r * g_ref[...]).astype(BF16)
        for b, q in enumerate(ex):
            st_ref[b] = q["st_b"]
            s_scr[b] = jnp.concatenate(q["new_st"], axis=1)
            ys_ref[b] = q["y"].astype(BF16)
            yb_ref[b] = q["yb"]

    def row(width, col):
        return pl.BlockSpec((nb, CHUNK, width), lambda c: (0, c, col))

    def const(shape):
        return pl.BlockSpec(shape, lambda c: (0,) * len(shape))

    proj3 = proj.reshape(nb, lp, N_INP)
    ybuf, yssd, states, *rest = _grid_call(
        body, name, (cpe,),
        [row(N_XBC, 0), row(D_B, COL_BZ // D_B), row(LANE, COL_DT // LANE),
         const((1, LANE)), const((N_HEADS, 1)), const((1, LANE)), const((N_HEADS, 1)),
         const((1, D_B)), const((1, D_B)), const((LANE, D_B)), HBM_SPEC],
        [row(D_B, 0), row(D_B, 0), pl.BlockSpec((nb, None, N_STATE, D_B), lambda c: (0, c, 0, 0))],
        [jax.ShapeDtypeStruct((nb, lp, ybuf.shape[1]), ybuf.dtype), jax.ShapeDtypeStruct((nb, lp, D_B), BF16),
         jax.ShapeDtypeStruct((nb, cpe, N_STATE, D_B), BF16)],
        [pltpu.VMEM((nb, N_STATE, D_B), F32)],
        (xbc.reshape(nb, lp, N_XBC), proj3, proj3, dtb_p, dtb_t, alog_p, alog_t, d_skip_x, norm_g, e_mat,
         ybuf.reshape(nb, lp, ybuf.shape[1])), comm, aliases={10: 0})
    return (ybuf.reshape(t, -1), yssd.reshape(t, D_B), states, *rest)


def _ssd_bwd(dy, y_ssd, xbc, proj, states, dtb, a_log, d_skip_x, norm_g, e_mat, e_mat_t, lp, name, comm=None):
    t = xbc.shape[0]
    cpe = lp // CHUNK
    nb = t // lp
    hpg = N_HEADS // N_GROUPS
    dtb_p = jnp.pad(dtb.reshape(1, N_HEADS), ((0, 0), (0, LANE - N_HEADS)))
    alog_p = jnp.pad(a_log.reshape(1, N_HEADS), ((0, 0), (0, LANE - N_HEADS)))
    dtb_t = dtb.reshape(N_HEADS, 1)
    alog_t = a_log.reshape(N_HEADS, 1)

    def body(dy_ref, ys_ref, xbc_ref, bz_ref, dt_ref, st_ref, dtb_ref, dtbt_ref, al_ref, alt_ref, dx_ref, g_ref,
             e_ref, et_ref, dxbc_ref, dpw_ref, dg_ref, ddtb_ref, dal_ref, dd_ref, ds_scr):
        @pl.when(pl.program_id(0) == 0)
        def _():
            ds_scr[...] = jnp.zeros_like(ds_scr)
            dg_ref[...] = jnp.zeros_like(dg_ref)
            ddtb_ref[...] = jnp.zeros_like(ddtb_ref)
            dal_ref[...] = jnp.zeros_like(dal_ref)
            dd_ref[...] = jnp.zeros_like(dd_ref)

        stores, sums = [], {}
        chains = [one_example(dy_ref.at[b], ys_ref.at[b], xbc_ref.at[b], bz_ref.at[b], dt_ref.at[b], st_ref.at[b], dtb_ref,
                              dtbt_ref, al_ref, alt_ref, dx_ref, g_ref, e_ref, et_ref, dxbc_ref.at[b], dpw_ref.at[b],
                              ds_scr.at[b], stores, sums) for b in range(nb)]
        live = list(chains)
        while live:
            live = [ch for ch in live if next(ch, "done") != "done"]
        for ref, idx, val in stores:
            ref[idx] = val
        for ref, name_ in ((dg_ref, "dg"), (ddtb_ref, "ddtb"), (dal_ref, "dal"), (dd_ref, "dd")):
            total = sums[name_][0]
            for part in sums[name_][1:]:
                total = total + part
            ref[...] += total

    def one_example(dy_ref, ys_ref, xbc_ref, bz_ref, dt_ref, st_ref, dtb_ref, dtbt_ref, al_ref, alt_ref, dx_ref, g_ref,
                    e_ref, et_ref, dxbc_ref, dpw_ref, ds_scr, stores, sums):
        everything = (slice(None), slice(None))
        cc = cpe - 1 - pl.program_id(0)
        rows = lax.broadcasted_iota(jnp.int32, (CHUNK, 1), 0)
        cols = lax.broadcasted_iota(jnp.int32, (1, CHUNK), 1)
        valid_col = jnp.logical_or(cc > 0, rows >= PAD)
        valid_row = jnp.logical_or(cc > 0, cols >= PAD)
        e_mat = e_ref[...]
        e_mat_t = et_ref[...]
        dtraw = dt_ref[...].astype(F32)
        a, dt, big_a, big_a_t, e_a, d_s, dt_x, e_a_x, d_s_x, cd_x = _ssd_common(
            dtraw, dtb_ref[...], dtbt_ref[...], al_ref[...], alt_ref[...], e_mat, valid_col, valid_row, x_terms=1)
        xs = xbc_ref[:, 0:D_B].astype(F32)
        bs = xbc_ref[:, D_B:D_B + N_GROUPS * N_STATE]
        cs = xbc_ref[:, D_B + N_GROUPS * N_STATE:N_XBC]
        xdt = xs * dt_x
        xdt_b = xdt.astype(BF16)
        st_b = st_ref[...]
        dst = ds_scr[...]
        dst_b = dst.astype(BF16)

        ys = ys_ref[...].astype(F32)
        bz = bz_ref[...].astype(F32)
        sil = _silu(bz)
        z = ys * sil
        dz, dgt = _rms_bwd_math(z, g_ref[...], dy_ref[...].astype(F32))
        sums.setdefault("dg", []).append(jnp.sum(dgt, axis=0, keepdims=True))
        stores.append((dpw_ref, (slice(None), slice(0, D_B)), (dz * ys * _dsilu(bz)).astype(BF16)))
        dys = dz * sil

        dd_lane = jnp.sum(dys * xs, axis=0, keepdims=True)
        dxs = dys * dx_ref[...]
        w_x = dys * e_a_x
        w_b = w_x.astype(BF16)
        dys_b = dys.astype(BF16)
        u_b = (xdt * d_s_x).astype(BF16)
        dxdt_parts, dbs_parts, dcs_parts, off_parts, g1_parts = [], [], [], [], []
        da_diag = jnp.zeros((CHUNK, LANE), F32)
        lane = lax.broadcasted_iota(jnp.int32, (1, LANE), 1)
        for g in range(N_GROUPS):
            gs = slice(g * N_STATE, (g + 1) * N_STATE)
            gw = slice(g * GROUP_W, (g + 1) * GROUP_W)
            cs_g, bs_g = cs[:, gs], bs[:, gs]
            dcs = lax.dot_general(w_b[:, gw], st_b[:, gw], NT_DIMS, preferred_element_type=F32)
            y_off = jnp.dot(cs_g, st_b[:, gw], preferred_element_type=F32)
            off_parts.append(y_off)
            dst_new = lax.dot_general(cs_g, w_b[:, gw], TN_DIMS, preferred_element_type=F32)
            g1 = jnp.dot(bs_g, dst_b[:, gw], preferred_element_type=F32)
            g1_parts.append(g1)
            dbs = lax.dot_general(u_b[:, gw], dst_b[:, gw], NT_DIMS, preferred_element_type=F32)
            cb = lax.dot_general(cs_g, bs_g, NT_DIMS, preferred_element_type=F32)
            cbt = lax.dot_general(bs_g, cs_g, NT_DIMS, preferred_element_type=F32)
            dcb = jnp.zeros((CHUNK, CHUNK), F32)
            dcbt = jnp.zeros((CHUNK, CHUNK), F32)
            dxdt_h = []
            for e in range(hpg):
                h = g * hpg + e
                hs = slice(h * HEAD_DIM, (h + 1) * HEAD_DIM)
                dec = _decay(big_a, big_a_t, h, False)
                dect = _decay(big_a, big_a_t, h, True)
                m = cb * dec
                mt = cbt * dect
                dxdt_h.append(jnp.dot(mt.astype(BF16), dys_b[:, hs], preferred_element_type=F32))
                dm = lax.dot_general(dys_b[:, hs], xdt_b[:, hs], NT_DIMS, preferred_element_type=F32)
                dmt = lax.dot_general(xdt_b[:, hs], dys_b[:, hs], NT_DIMS, preferred_element_type=F32)
                dcb = dcb + dm * dec
                dcbt = dcbt + dmt * dect
                da_h = jnp.sum(dm * m - dmt * mt, axis=1, keepdims=True)
                da_diag = da_diag + jnp.where(lane == h, da_h, 0.0)
                yield
            dcs = dcs + jnp.dot(dcb.astype(BF16), bs_g, preferred_element_type=F32)
            dbs = dbs + jnp.dot(dcbt.astype(BF16), cs_g, preferred_element_type=F32)
            dxdt_parts.append(jnp.concatenate(dxdt_h, axis=1) + g1 * d_s_x[:, gw])
            dbs_parts.append(dbs)
            dcs_parts.append(dcs)
            stores.append((ds_scr, (slice(None), gw), dst[:, gw] * cd_x[:, gw] + dst_new))
            yield
        dxdt = jnp.concatenate(dxdt_parts, axis=1)
        y_off = jnp.concatenate(off_parts, axis=1)
        g1 = jnp.concatenate(g1_parts, axis=1)
        dcd_lane = jnp.sum(dst * st_b.astype(F32), axis=0, keepdims=True)
        vecs = jnp.concatenate([jnp.broadcast_to(dcd_lane, (SUB, D_B)), jnp.broadcast_to(dd_lane, (SUB, D_B))], axis=0)
        dds, da_off, ddt_x, vec_sums = _split_dot_many(
            [g1 * xdt, w_x * y_off, dxdt * xs, vecs], [1, 1, 1, 1], e_mat_t)
        dcd = vec_sums[0:1]
        sums.setdefault("dd", []).append(vec_sums[SUB:SUB + 1])
        yield
        t_ds = dds * d_s
        d_a = da_diag + da_off - t_ds
        last_row = jnp.sum(t_ds, axis=0, keepdims=True) + dcd * e_a[CHUNK - 1:CHUNK, :]
        d_a = d_a + jnp.where(rows == CHUNK - 1, last_row, 0.0)
        dda = _split_dot_left(_tri(False).astype(BF16), d_a, 3)
        ddt = dda * a + ddt_x
        sums.setdefault("dal", []).append(jnp.sum(dda * dt * a, axis=0, keepdims=True))
        ddtraw = jnp.where(valid_col, ddt * jax.nn.sigmoid(dtraw + dtb_ref[...]), 0.0)
        ddtraw = jnp.where(lane < N_HEADS, ddtraw, 0.0)
        sums.setdefault("ddtb", []).append(jnp.sum(ddtraw, axis=0, keepdims=True))
        stores.append((dpw_ref, (slice(None), slice(D_B, D_B + LANE)), ddtraw.astype(BF16)))
        dxs = dxs + dxdt * dt_x
        dxbc = jnp.concatenate([dxs] + dbs_parts + dcs_parts, axis=1)
        stores.append((dxbc_ref, everything, jnp.where(valid_col, dxbc, 0.0).astype(BF16)))

    def row(width, col):
        return pl.BlockSpec((nb, CHUNK, width), lambda c: (0, cpe - 1 - c, col))

    def const(shape):
        return pl.BlockSpec(shape, lambda c: (0,) * len(shape))

    proj3 = proj.reshape(nb, lp, N_INP)
    dxbc, dproj, *rest = _grid_call(
        body, name, (cpe,),
        [row(D_B, 0), row(D_B, 0), row(N_XBC, 0), row(D_B, COL_BZ // D_B), row(LANE, COL_DT // LANE),
         pl.BlockSpec((nb, None, N_STATE, D_B), lambda c: (0, cpe - 1 - c, 0, 0)),
         const((1, LANE)), const((N_HEADS, 1)), const((1, LANE)), const((N_HEADS, 1)),
         const((1, D_B)), const((1, D_B)), const((LANE, D_B)), const((D_B, LANE))],
        [row(N_XBC, 0),
         pl.BlockSpec((pl.Element(nb), pl.Element(CHUNK), pl.Element(D_B + LANE)),
                      lambda c: (0, (cpe - 1 - c) * CHUNK, COL_BZ)),
         const((1, D_B)), const((1, LANE)), const((1, LANE)), const((1, LANE))],
        [jax.ShapeDtypeStruct((nb, lp, N_XBC), BF16), jax.ShapeDtypeStruct((nb, lp, N_INP), BF16),
         jax.ShapeDtypeStruct((1, D_B), F32),
         jax.ShapeDtypeStruct((1, LANE), F32), jax.ShapeDtypeStruct((1, LANE), F32),
         jax.ShapeDtypeStruct((1, LANE), F32)],
        [pltpu.VMEM((nb, N_STATE, D_B), F32)],
        (dy.reshape(nb, lp, -1), y_ssd.reshape(nb, lp, D_B), xbc.reshape(nb, lp, N_XBC), proj3, proj3, states,
         dtb_p, dtb_t, alog_p, alog_t, d_skip_x, norm_g, e_mat, e_mat_t), comm)
    return (dxbc.reshape(t, N_XBC), dproj.reshape(t, N_INP), *rest)


HBM_SPEC = pl.BlockSpec(memory_space=pl.ANY)


def _mesh_pos():
    return lax.axis_index("x"), lax.axis_index("y"), lax.axis_index("c")


def _remote(src, dst, send_sem, recv_sem, to):
    return pltpu.make_async_remote_copy(src_ref=src, dst_ref=dst, send_sem=send_sem, recv_sem=recv_sem,
                                        device_id=to, device_id_type=MESH)


def _slot_plain(d):
    return d


def _slot_mix_rows(d):
    return jnp.where(d < 2, d + 4, jnp.where(d < 6, d - 2, d))


def _ag_ici(pieces, slots):
    n = len(pieces)

    def copies(ins, outs, sems, with_recvs):
        send_sems, recv_sems, local_sems = sems
        x, y, c = _mesh_pos()
        local, sends, recvs = [], [], []
        for a in range(n):
            mine = outs[a].at[slots[a](4 * x + 2 * y + c)]
            local.append(pltpu.make_async_copy(ins[a], mine, local_sems.at[a]))
            for k, (px, py) in enumerate([(1 - x, y), (x, 1 - y), (1 - x, 1 - y)]):
                sends.append(_remote(ins[a], mine, send_sems.at[a, k], recv_sems.at[a, k], (px, py, c)))
                if with_recvs:
                    theirs = outs[a].at[slots[a](4 * px + 2 * py + c)]
                    recvs.append(_remote(ins[a], theirs, send_sems.at[a, k], recv_sems.at[a, k], (px, py, c)))
        return local, sends, recvs

    return _Comm(pieces, [jax.ShapeDtypeStruct((N_DEV,) + p.shape, p.dtype) for p in pieces],
                 [pltpu.SemaphoreType.DMA((n, 3)), pltpu.SemaphoreType.DMA((n, 3)), pltpu.SemaphoreType.DMA((n,))], copies)


def _ag_d2d(bufs, slots):
    n = len(bufs)

    def copies(ins, outs, sems, with_recvs):
        send_sems, recv_sems = sems
        x, y, c = _mesh_pos()
        chips = [(x, y), (1 - x, y), (x, 1 - y), (1 - x, 1 - y)]
        sends, recvs = [], []
        for a in range(n):
            for k, (px, py) in enumerate(chips):
                held = slots[a](4 * px + 2 * py + c)
                sends.append(_remote(ins[a].at[held], outs[a].at[held], send_sems.at[a, k], recv_sems.at[a, k], (x, y, 1 - c)))
                if with_recvs:
                    got = slots[a](4 * px + 2 * py + 1 - c)
                    recvs.append(_remote(ins[a].at[got], outs[a].at[got], send_sems.at[a, k], recv_sems.at[a, k], (x, y, 1 - c)))
        return [], sends, recvs

    return _Comm(bufs, [jax.ShapeDtypeStruct(b.shape, b.dtype) for b in bufs],
                 [pltpu.SemaphoreType.DMA((n, 4)), pltpu.SemaphoreType.DMA((n, 4))], copies,
                 aliases={a: a for a in range(n)})


def _rs_d2d(blocks, slots):
    n = len(blocks)

    def copies(ins, outs, sems, with_recvs):
        send_sems, recv_sems = sems
        x, y, c = _mesh_pos()
        sends, recvs = [], []
        for a in range(n):
            for j in range(4):
                src = ins[a].at[slots[a](2 * j + 1 - c)]
                sends.append(_remote(src, outs[a].at[j], send_sems.at[a, j], recv_sems.at[a, j], (x, y, 1 - c)))
                if with_recvs:
                    recvs.append(_remote(src, outs[a].at[j], send_sems.at[a, j], recv_sems.at[a, j], (x, y, 1 - c)))
        return [], sends, recvs

    return _Comm(blocks, [jax.ShapeDtypeStruct((4,) + b.shape[1:], b.dtype) for b in blocks],
                 [pltpu.SemaphoreType.DMA((n, 4)), pltpu.SemaphoreType.DMA((n, 4))], copies)


def _rs_ici(blocks):
    n = len(blocks)

    def copies(ins, outs, sems, with_recvs):
        send_sems, recv_sems = sems
        x, y, c = _mesh_pos()
        sends, recvs = [], []
        for a in range(n):
            for k, (px, py) in enumerate([(1 - x, y), (x, 1 - y), (1 - x, 1 - y)]):
                src = ins[a].at[2 * px + py]
                sends.append(_remote(src, outs[a].at[k], send_sems.at[a, k], recv_sems.at[a, k], (px, py, c)))
                if with_recvs:
                    recvs.append(_remote(src, outs[a].at[k], send_sems.at[a, k], recv_sems.at[a, k], (px, py, c)))
        return [], sends, recvs

    return _Comm(blocks, [jax.ShapeDtypeStruct((3,) + b.shape[1:], b.dtype) for b in blocks],
                 [pltpu.SemaphoreType.DMA((n, 3)), pltpu.SemaphoreType.DMA((n, 3))], copies)


def _run_comm(comm, name):
    n_in, n_out = len(comm.inputs), len(comm.out_shapes)

    def body(*refs):
        ins, outs, sems = refs[:n_in], refs[n_in:n_in + n_out], refs[n_in + n_out:]
        comm.start(ins, outs, sems)
        comm.wait(ins, outs, sems)

    return pl.pallas_call(
        body, name=name, in_specs=[HBM_SPEC] * n_in, out_specs=[HBM_SPEC] * n_out, out_shape=comm.out_shapes,
        scratch_shapes=comm.sem_shapes, input_output_aliases=comm.aliases,
    )(*comm.inputs)


W_ROWS = 784


def _w_segments():
    per = N_IN // N_DEV
    out = []
    for d in range(N_DEV):
        lo, hi = per * d, per * (d + 1)
        for a, b, start in COL_MAP:
            s, e = max(lo, a), min(hi, b)
            if s < e:
                out.append((d, s - lo, e - s, s - a + start))
    return out


def _w_gather_rows(g, name):
    tk = D_MODEL // 2
    u32 = jnp.uint32

    def body(g_ref, o_ref, scr):
        for d in range(N_DEV):
            x32 = pltpu.bitcast(g_ref[d], u32)
            for dd, src, rows, dst in _w_segments():
                if dd == d:
                    scr[dst // 2:(dst + rows) // 2, :] = x32[src // 2:(src + rows) // 2]
        scr[(COL_DT + N_HEADS) // 2:COL_XBC // 2, :] = jnp.zeros(((COL_XBC - COL_DT - N_HEADS) // 2, tk), u32)
        o_ref[...] = pltpu.bitcast(scr[...], BF16)

    return pl.pallas_call(
        body, name=name, grid=(D_MODEL // tk,),
        in_specs=[pl.BlockSpec((N_DEV, W_ROWS, tk), lambda j: (0, 0, j))],
        out_specs=pl.BlockSpec((N_INP, tk), lambda j: (0, j)),
        out_shape=jax.ShapeDtypeStruct((N_INP, D_MODEL), BF16),
        scratch_shapes=[pltpu.VMEM((N_INP // 2, tk), u32)],
        compiler_params=_cparams(("parallel",)),
    )(g)


def _w_split_rows(dwt, name):
    tk = D_MODEL // 4
    per = N_IN // N_DEV

    def body(w_ref, o_ref):
        for d, dst, rows, src in _w_segments():
            o_ref[d, dst // 2:(dst + rows) // 2, :] = pltpu.bitcast(w_ref[src:src + rows, :].astype(BF16), F32)
        for d in range(N_DEV):
            o_ref[d, per // 2:W_ROWS // 2, :] = jnp.zeros(((W_ROWS - per) // 2, tk), F32)

    return pl.pallas_call(
        body, name=name, grid=(D_MODEL // tk,),
        in_specs=[pl.BlockSpec((N_INP, tk), lambda j: (0, j))],
        out_specs=pl.BlockSpec((N_DEV, W_ROWS // 2, tk), lambda j: (0, 0, j)),
        out_shape=jax.ShapeDtypeStruct((N_DEV, W_ROWS // 2, D_MODEL), F32),
        compiler_params=_cparams(("parallel",)),
    )(dwt)


def _rs_pair_sum(g, ra, own_slots, name):
    packed = g.dtype == F32
    _, rows, cols = g.shape
    out_rows = 2 * rows if packed else rows

    def body(s_ref, g_ref, ra_ref, o_ref):
        a, b = g_ref[...], ra_ref[...]
        if packed:
            a, b = pltpu.bitcast(a, BF16), pltpu.bitcast(b, BF16)
        o_ref[...] = (a.astype(F32) + b.astype(F32)).astype(BF16)

    return pl.pallas_call(
        body, name=name,
        grid_spec=pltpu.PrefetchScalarGridSpec(
            num_scalar_prefetch=1, grid=(4,),
            in_specs=[pl.BlockSpec((None, rows, cols), lambda j, s: (s[j], 0, 0)),
                      pl.BlockSpec((None, rows, cols), lambda j, s: (j, 0, 0))],
            out_specs=pl.BlockSpec((None, out_rows, cols), lambda j, s: (j, 0, 0))),
        out_shape=jax.ShapeDtypeStruct((4, out_rows, cols), BF16),
        compiler_params=_cparams(("parallel",)),
    )(own_slots, g, ra)


def _rs_final_sum(h, rb, chip_idx, name):
    _, rows, cols = h.shape

    def body(j_ref, h_ref, rb_ref, o_ref):
        o_ref[...] = ((h_ref[...].astype(F32) + rb_ref[0].astype(F32)) + rb_ref[1].astype(F32)) + rb_ref[2].astype(F32)

    return pl.pallas_call(
        body, name=name,
        grid_spec=pltpu.PrefetchScalarGridSpec(
            num_scalar_prefetch=1, grid=(1,),
            in_specs=[pl.BlockSpec((None, rows, cols), lambda i, j: (j[0], 0, 0)),
                      pl.BlockSpec((3, rows, cols), lambda i, j: (0, 0, 0))],
            out_specs=pl.BlockSpec((rows, cols), lambda i, j: (0, 0))),
        out_shape=jax.ShapeDtypeStruct((rows, cols), F32),
        compiler_params=_cparams(("arbitrary",)),
    )(chip_idx, h, rb)


def _adamw_math(w, g, m, v):
    m = ADAM_B1 * m + (1.0 - ADAM_B1) * g
    v = ADAM_B2 * v + (1.0 - ADAM_B2) * (g * g)
    m_hat = m / (1.0 - ADAM_B1 ** ADAM_STEP)
    v_hat = v / (1.0 - ADAM_B2 ** ADAM_STEP)
    delta = -ADAM_LR * (m_hat / (jnp.sqrt(v_hat) + ADAM_EPS) + ADAM_WD * w)
    return delta, m, v


def _adamw_rows(g, w, m, v, tr, name):
    layers, rows, cols = w.shape

    def body(g_ref, w_ref, m_ref, v_ref, d_out, m_out, v_out):
        delta, m_new, v_new = _adamw_math(w_ref[...], g_ref[...], m_ref[...], v_ref[...])
        d_out[...] = delta
        m_out[...] = m_new
        v_out[...] = v_new

    blk = pl.BlockSpec((None, tr, cols), lambda a, r: (a, r, 0))
    return pl.pallas_call(
        body, name=name, grid=(layers, rows // tr),
        in_specs=[blk] * 4, out_specs=[blk] * 3,
        out_shape=[jax.ShapeDtypeStruct((layers, rows, cols), F32)] * 3,
        compiler_params=_cparams(("parallel", "parallel")),
    )(g, w, m, v)


def _adamw_cols(g_t, w, m, v, name):
    layers, k, cols = w.shape
    tr = 56
    assert g_t.shape[1] % tr == 0
    views = [jnp.transpose(a, (2, 0, 1)) for a in (w, m, v)]

    def body(g_ref, w_ref, m_ref, v_ref, g_out, d_out, m_out, v_out):
        for a in range(layers):
            g = g_ref[a]
            delta, m_new, v_new = _adamw_math(w_ref[:, a, :], g, m_ref[:, a, :], v_ref[:, a, :])
            g_out[:, a, :] = g
            d_out[:, a, :] = delta
            m_out[:, a, :] = m_new
            v_out[:, a, :] = v_new

    col = pl.BlockSpec((tr, layers, k), lambda r: (r, 0, 0))
    outs = pl.pallas_call(
        body, name=name, grid=(pl.cdiv(cols, tr),),
        in_specs=[pl.BlockSpec((layers, tr, k), lambda r: (0, r, 0)), col, col, col], out_specs=[col] * 4,
        out_shape=[jax.ShapeDtypeStruct((cols, layers, k), F32)] * 4,
        compiler_params=_cparams(("parallel",)),
    )(g_t, *views)
    return [jnp.transpose(o, (1, 2, 0)) for o in outs]


def _sum_devices(parts, name):
    _, p, _ = parts.shape

    def body(x_ref, o_ref):
        acc = x_ref[0]
        for d in range(1, N_DEV):
            acc = acc + x_ref[d]
        o_ref[...] = acc

    return pl.pallas_call(
        body, name=name, grid=(1,),
        in_specs=[pl.BlockSpec((N_DEV, p, LANE), lambda i: (0, 0, 0))],
        out_specs=pl.BlockSpec((p, LANE), lambda i: (0, 0)),
        out_shape=jax.ShapeDtypeStruct((p, LANE), F32),
        compiler_params=_cparams(("arbitrary",)),
    )(parts)


def _adamw_small(g, w, m, v, name):
    p = g.shape[0]

    def body(g_ref, w_ref, m_ref, v_ref, d_out, m_out, v_out):
        delta, m_new, v_new = _adamw_math(w_ref[...], g_ref[...], m_ref[...], v_ref[...])
        d_out[...] = delta
        m_out[...] = m_new
        v_out[...] = v_new

    spec = pl.BlockSpec((p, LANE), lambda i: (0, 0))
    return pl.pallas_call(
        body, name=name, grid=(1,),
        in_specs=[spec] * 4, out_specs=[spec] * 3,
        out_shape=[jax.ShapeDtypeStruct((p, LANE), F32)] * 3,
        compiler_params=_cparams(("arbitrary",)),
    )(g, w, m, v)


PACK_ALIGN = SUB * LANE

SMALL_PARAMS = (
    ("meta", (N_META, D_MODEL), 1),
    ("pre_g", (DEPTH, D_MODEL), None),
    ("post_g", (DEPTH, D_MODEL), None),
    ("conv_a_w", (DEPTH, CONV_A_K, D_A), 2),
    ("ssm_conv_w", (DEPTH, SSM_K, N_XBC), 2),
    ("ssm_conv_b", (DEPTH, N_XBC), None),
    ("dt_bias", (DEPTH, N_HEADS), None),
    ("a_log", (DEPTH, N_HEADS), None),
    ("d_skip", (DEPTH, N_HEADS), None),
    ("ssm_norm_g", (DEPTH, D_B), None),
    ("conf_conv_w", (DEPTH, CONF_K, D_C), 2),
    ("conf_conv_b", (DEPTH, D_C), None),
    ("conf_ln_g", (DEPTH, D_C), None),
    ("conf_ln_b", (DEPTH, D_C), None),
)


def _local_shape(shape, axis):
    if axis is None:
        return shape
    return tuple(s // N_DEV if k == axis else s for k, s in enumerate(shape))


def _pack(arrays):
    flat = []
    for a in arrays:
        v = a.reshape(-1).astype(F32)
        flat.append(v)
        if v.shape[0] % PACK_ALIGN:
            flat.append(jnp.zeros(((-v.shape[0]) % PACK_ALIGN,), F32))
    return jnp.concatenate(flat).reshape(-1, LANE)


def _unpack(buf, shapes):
    flat = buf.reshape(-1)
    out, off = [], 0
    for s in shapes:
        size = 1
        for k in s:
            size *= k
        out.append(flat[off:off + size].reshape(s))
        off += size + (-size) % PACK_ALIGN
    return out


def kernel(x, meta, pre_g, post_g, w_in, w_out, conv_a_w, ssm_conv_w, ssm_conv_b, dt_bias, a_log, d_skip, ssm_norm_g, conf_conv_w, conf_conv_b, conf_ln_g, conf_ln_b, loss_target, m_meta, m_pre_g, m_post_g, m_w_in, m_w_out, m_conv_a_w, m_ssm_conv_w, m_ssm_conv_b, m_dt_bias, m_a_log, m_d_skip, m_ssm_norm_g, m_conf_conv_w, m_conf_conv_b, m_conf_ln_g, m_conf_ln_b, v_meta, v_pre_g, v_post_g, v_w_in, v_w_out, v_conv_a_w, v_ssm_conv_w, v_ssm_conv_b, v_dt_bias, v_a_log, v_d_skip, v_ssm_norm_g, v_conf_conv_w, v_conf_conv_b, v_conf_ln_g, v_conf_ln_b):
    weights = dict(meta=meta, pre_g=pre_g, post_g=post_g, conv_a_w=conv_a_w, ssm_conv_w=ssm_conv_w, ssm_conv_b=ssm_conv_b,
                   dt_bias=dt_bias, a_log=a_log, d_skip=d_skip, ssm_norm_g=ssm_norm_g, conf_conv_w=conf_conv_w,
                   conf_conv_b=conf_conv_b, conf_ln_g=conf_ln_g, conf_ln_b=conf_ln_b)
    mom1 = dict(meta=m_meta, pre_g=m_pre_g, post_g=m_post_g, conv_a_w=m_conv_a_w, ssm_conv_w=m_ssm_conv_w,
                ssm_conv_b=m_ssm_conv_b, dt_bias=m_dt_bias, a_log=m_a_log, d_skip=m_d_skip, ssm_norm_g=m_ssm_norm_g,
                conf_conv_w=m_conf_conv_w, conf_conv_b=m_conf_conv_b, conf_ln_g=m_conf_ln_g, conf_ln_b=m_conf_ln_b)
    mom2 = dict(meta=v_meta, pre_g=v_pre_g, post_g=v_post_g, conv_a_w=v_conv_a_w, ssm_conv_w=v_ssm_conv_w,
                ssm_conv_b=v_ssm_conv_b, dt_bias=v_dt_bias, a_log=v_a_log, d_skip=v_d_skip, ssm_norm_g=v_ssm_norm_g,
                conf_conv_w=v_conf_conv_w, conf_conv_b=v_conf_conv_b, conf_ln_g=v_conf_ln_g, conf_ln_b=v_conf_ln_b)
    nb, seq, d = x.shape
    lp = PAD + N_META + seq
    t = nb * lp
    assert lp % (SHORT_TILES * HALO_C) == 0 and t % (3 * LANE) == 0 and d == D_MODEL
    rt = lp // 3
    xi, yi, ci = _mesh_pos()
    dev = 4 * xi + 2 * yi + ci
    ci32 = ci.astype(jnp.int32)
    chip_idx = (2 * xi + yi).astype(jnp.int32).reshape(1)
    own_plain = jnp.stack([2 * j + ci32 for j in range(4)])
    own_mix = jnp.stack([_slot_mix_rows(2 * j + ci32) for j in range(4)]).astype(jnp.int32)
    n_in_loc = N_IN // N_DEV
    n_out_loc = 2 * D_MODEL // N_DEV
    slots = (_slot_plain, _slot_mix_rows)

    sharded_small = [n for n, _, ax in SMALL_PARAMS if ax is not None]
    small_shapes = {n: s for n, s, _ in SMALL_PARAMS}
    small_axis = {n: ax for n, _, ax in SMALL_PARAMS}
    sw_pack = _pack([weights[n] for n in sharded_small])
    wt_loc = jnp.pad(jnp.swapaxes(w_in, 1, 2).astype(BF16), ((0, 0), (0, W_ROWS - n_in_loc), (0, 0)))
    wo_loc = w_out.astype(BF16)
    plain2 = (_slot_plain, _slot_plain)
    head = _run_comm(_ag_ici([wt_loc[0], sw_pack], plain2), "ag_ici")
    g_in_t, sw_g = _run_comm(_ag_d2d(head, plain2), "ag_d2d")

    def gathered(bufs):
        return _w_gather_rows(bufs[0], "w_gather_rows"), bufs[1].reshape(2 * D_MODEL, D_MODEL)

    full = dict(weights)
    per_dev = [_unpack(sw_g[k], [_local_shape(small_shapes[n], small_axis[n]) for n in sharded_small]) for k in range(N_DEV)]
    for q, n in enumerate(sharded_small):
        full[n] = jnp.concatenate([per_dev[k][q] for k in range(N_DEV)], axis=small_axis[n])

    e_mat = (lax.broadcasted_iota(jnp.int32, (LANE, D_B), 0) == lax.broadcasted_iota(jnp.int32, (LANE, D_B), 1) // HEAD_DIM)
    e_mat = e_mat.astype(BF16)
    e_mat_t = e_mat.T

    front = jnp.concatenate([jnp.zeros((PAD, d), F32), full["meta"]], axis=0)
    h = jnp.concatenate([jnp.concatenate([front, x[b]], axis=0) for b in range(nb)], axis=0)
    saved = []
    w_t, w_o = _w_gather_rows(g_in_t, "w_gather_rows"), None
    hn = _rms_fwd(h, pre_g[0].reshape(1, -1), rt, "rms_fwd")
    for i in range(DEPTH):
        row = lambda a: a[i].reshape(1, -1)
        if i == 0:
            proj, wo_buf = _mm(hn, w_t, "nt", BF16, t // 3, MM_TN, D_MODEL, "mm_proj", cols_outer=True,
                               comm=_ag_ici([wo_loc[0]], (_slot_mix_rows,)))
        else:
            proj = _mm(hn, w_t, "nt", BF16, t // 3, MM_TN, D_MODEL, "mm_proj", cols_outer=True)
        ycat = _a_fwd(proj, full["conv_a_w"][i], lp, "a_fwd")
        xbc = _xbc_fwd(proj, full["ssm_conv_w"][i], row(ssm_conv_b), lp, "xbc_fwd")
        ycat, u1 = _c_fwd(proj, full["conf_conv_w"][i], row(conf_conv_b), row(conf_ln_g), row(conf_ln_b), ycat, lp, "c_fwd")
        d_skip_x = jnp.repeat(d_skip[i], HEAD_DIM).reshape(1, D_B)
        nxt = _ag_ici([wt_loc[i + 1], wo_loc[i + 1]], slots) if i + 1 < DEPTH else None
        if i == 0:
            nxt = _merge_comm(nxt, _ag_d2d([wo_buf], (_slot_mix_rows,)))
        ycat, yssd, states, *bufs = _ssd_fwd(xbc, proj, dt_bias[i], a_log[i], d_skip_x, row(ssm_norm_g), e_mat, ycat, lp,
                                             "ssd_fwd", comm=nxt)
        if i == 0:
            w_o = bufs.pop().reshape(2 * D_MODEL, D_MODEL)
        if bufs:
            m, *bufs = _mm(ycat, w_o, "nn", F32, t // 3, D_MODEL, 2 * D_MODEL, "mm_out", comm=_ag_d2d(bufs, slots))
        else:
            m = _mm(ycat, w_o, "nn", F32, t // 3, D_MODEL, 2 * D_MODEL, "mm_out")
        saved.append((h, hn, proj, ycat, yssd, states, xbc, m, d_skip_x, w_t, w_o, u1))
        if i + 1 < DEPTH:
            h, hn = _post_rms_fwd(h, m, row(post_g), pre_g[i + 1].reshape(1, -1), rt, "post_rms_fwd")
        else:
            h = _post_fwd(h, m, row(post_g), rt, "post_fwd")
        if bufs:
            w_t, w_o = gathered(bufs)

    dh, loss_blk = _loss_kernel(h, loss_target.reshape(nb * seq, d), lp, "loss")

    grads = {n: [None] * DEPTH for n, _, _ in SMALL_PARAMS if n != "meta"}
    gt_in, gr_out = [None] * DEPTH, [None] * DEPTH
    dmeta = None
    pending = None

    def rs_pair_sums(blocks, ras):
        return [_rs_pair_sum(blocks[0], ras[0], own_plain, "rs_pair_sum_in"),
                _rs_pair_sum(blocks[1], ras[1], own_mix, "rs_pair_sum_out")]

    def rs_finish(layer, hs, rbs):
        gt_in[layer] = _rs_final_sum(hs[0], rbs[0], chip_idx, "rs_final_sum_in")
        gr_out[layer] = _rs_final_sum(hs[1], rbs[1], chip_idx, "rs_final_sum_out")

    for i in reversed(range(DEPTH)):
        row = lambda a: a[i].reshape(1, -1)
        h_i, hn, proj, ycat, yssd, states, xbc, m, d_skip_x, w_t, w_o, u1 = saved[i]
        dm, grads["post_g"][i] = _post_bwd(dh, m, row(post_g), rt, "post_bwd")
        if pending is not None:
            dy, *ras = _mm(dm, w_o, "nt", BF16, t // 3, D_MODEL, D_MODEL, "mm_dy", cols_outer=True,
                           comm=_rs_d2d(list(pending), slots))
            hs = rs_pair_sums(pending, ras)
        else:
            dy = _mm(dm, w_o, "nt", BF16, t // 3, D_MODEL, D_MODEL, "mm_dy", cols_outer=True)
            hs = None
        dw_out = _mm(ycat, dm, "tn", BF16, D_MODEL, D_MODEL, t // 3, "mm_dwout")
        dxbc, dproj, grads["ssm_norm_g"][i], ddtb, dal, dds, *rbs = _ssd_bwd(
            dy, yssd, xbc, proj, states, dt_bias[i], a_log[i], d_skip_x, row(ssm_norm_g), e_mat, e_mat_t, lp, "ssd_bwd",
            comm=_rs_ici(hs) if hs is not None else None)
        if hs is not None:
            rs_finish(i + 1, hs, rbs)
        grads["dt_bias"][i] = ddtb[:, :N_HEADS]
        grads["a_log"][i] = dal[:, :N_HEADS]
        grads["d_skip"][i] = dds[:, :N_HEADS]
        dproj, grads["conv_a_w"][i] = _a_bwd(dy, proj, full["conv_a_w"][i], dproj, lp, "a_bwd")
        dproj, grads["ssm_conv_w"][i], grads["ssm_conv_b"][i] = _xbc_bwd(
            dxbc, proj, full["ssm_conv_w"][i], row(ssm_conv_b), dproj, lp, "xbc_bwd")
        dproj, grads["conf_conv_w"][i], grads["conf_conv_b"][i], grads["conf_ln_g"][i], grads["conf_ln_b"][i] = _c_bwd(
            dy, proj, u1, full["conf_conv_w"][i], row(conf_conv_b), row(conf_ln_g), row(conf_ln_b), dproj, lp, "c_bwd")
        p_out = dw_out.reshape(N_DEV, n_out_loc, D_MODEL)
        if i > 0:
            dhn = _mm(dproj, w_t, "nn", BF16, MM_TM, D_MODEL, N_INP, "mm_dhn")
            dw_in_t = _mm(dproj, hn, "tn", F32, MM_TN, D_MODEL, t, "mm_dwin")
        else:
            dhn, ra_out = _mm(dproj, w_t, "nn", BF16, MM_TM, D_MODEL, N_INP, "mm_dhn",
                              comm=_rs_d2d([p_out], (_slot_mix_rows,)))
            h_out = _rs_pair_sum(p_out, ra_out, own_mix, "rs_pair_sum_out")
            dw_in_t, rb_out = _mm(dproj, hn, "tn", F32, MM_TN, D_MODEL, t, "mm_dwin", comm=_rs_ici([h_out]))
            gr_out[0] = _rs_final_sum(h_out, rb_out, chip_idx, "rs_final_sum_out")
        dh, grads["pre_g"][i], dmeta = _rms_bwd(dh, dhn, h_i, row(pre_g), lp, rt, "rms_bwd")
        pending = (_w_split_rows(dw_in_t, "w_split_rows"), p_out)
    grad_x = dh.reshape(nb, lp, d)[:, PAD + N_META:]

    names = [n for n, _, _ in SMALL_PARAMS]
    partial = [loss_blk[0:1, 0:1], dmeta] + [jnp.concatenate(grads[n], axis=0) for n in names[1:]]
    part_pack = _pack(partial)
    p_in = pending[0]
    ra_in, part_buf = _run_comm(_merge_comm(_rs_d2d([p_in], (_slot_plain,)), _ag_ici([part_pack], (_slot_plain,))), "rs_d2d")
    h_in = _rs_pair_sum(p_in, ra_in, own_plain, "rs_pair_sum_in")
    rb_in, parts_g = _run_comm(_merge_comm(_rs_ici([h_in]), _ag_d2d([part_buf], (_slot_plain,))), "rs_ici")
    gt_in[0] = _rs_final_sum(h_in, rb_in, chip_idx, "rs_final_sum_in")

    g_w_out = jnp.stack(gr_out)
    big = {"w_in": _adamw_cols(jnp.stack(gt_in), w_in, m_w_in, v_w_in, "adamw_w_in"),
           "w_out": [g_w_out, *_adamw_rows(g_w_out, w_out, m_w_out, v_w_out, n_out_loc, "adamw_w_out")]}

    total = _unpack(_sum_devices(parts_g, "sum_small_grads"), [(1,)] + [small_shapes[n] for n in names])
    loss = total[0][0]
    g_small = {}
    for n, g in zip(names, total[1:]):
        ax = small_axis[n]
        if ax is not None:
            g = lax.dynamic_slice_in_dim(g, dev * (small_shapes[n][ax] // N_DEV), small_shapes[n][ax] // N_DEV, axis=ax)
        g_small[n] = g
    loc_shapes = [_local_shape(small_shapes[n], small_axis[n]) for n in names]
    d_pack, m_pack, v_pack = _adamw_small(_pack([g_small[n] for n in names]), _pack([weights[n] for n in names]),
                                          _pack([mom1[n] for n in names]), _pack([mom2[n] for n in names]), "adamw_small")
    d_small = dict(zip(names, _unpack(d_pack, loc_shapes)))
    m_small = dict(zip(names, _unpack(m_pack, loc_shapes)))
    v_small = dict(zip(names, _unpack(v_pack, loc_shapes)))

    order = ["meta", "pre_g", "post_g", "w_in", "w_out", "conv_a_w", "ssm_conv_w", "ssm_conv_b", "dt_bias", "a_log",
             "d_skip", "ssm_norm_g", "conf_conv_w", "conf_conv_b", "conf_ln_g", "conf_ln_b"]

    def pick(k, small):
        return [big[n][k] if n in big else small[n] for n in order]

    return (loss, grad_x, *pick(0, g_small), *pick(1, d_small), *pick(2, m_small), *pick(3, v_small))
```

```python
import jax
import jax.numpy as jnp
from jax import lax
from jax.experimental import pallas as pl
from jax.experimental.pallas import tpu as pltpu

F32 = jnp.float32
BF16 = jnp.bfloat16

D_MODEL = 1024
DEPTH = 4
SEQ = 2048
CHUNK = 64
N_META = 16
PAD = 48
LP = PAD + N_META + SEQ
D_A = 512
D_B = 1024
D_C = 512
N_HEADS = 16
HEAD_DIM = 64
N_STATE = 128
N_GROUPS = 2
GROUP_W = D_B // N_GROUPS
N_XBC = D_B + 2 * N_GROUPS * N_STATE
CONV_A_K = 3
SSM_K = 4
CONF_K = 31
NORM_EPS = 1e-6
LN_EPS = 1e-5
N_IN = 6160
N_INP = 6272
COL_BZ = 2048
COL_DT = 3072
COL_XBC = 3200
COL_C = 4736
COL_MAP = ((0, 3072, 0), (3072, 4608, COL_XBC), (4608, 4624, COL_DT), (4624, 6160, COL_C))
LANE = 128
SUB = 8

ADAM_LR = 0.001
ADAM_B1 = 0.9
ADAM_B2 = 0.999
ADAM_EPS = 1e-08
ADAM_WD = 0.01
ADAM_STEP = 10

SHORT_TILES = 6
HALO_A = 16
HALO_C = 32
MM_TM = 384
MM_TN = 896
VMEM_LIMIT = 56 * 1024 * 1024

MESH = pl.DeviceIdType.MESH
N_DEV = 8


def _silu(x):
    return x * jax.nn.sigmoid(x)


def _dsilu(x):
    s = jax.nn.sigmoid(x)
    return s * (1.0 + x * (1.0 - s))


def _cparams(sem=None):
    return pltpu.CompilerParams(dimension_semantics=sem, vmem_limit_bytes=VMEM_LIMIT)


def _mm(a, b, mode, out_dtype, tm, tn, tk, name, cols_outer=False, comm=None):
    def ix(f):
        if cols_outer:
            return lambda j, i, q: f(i, j, q)
        return f

    if mode == "nn":
        (m, k), (_, n) = a.shape, b.shape
        a_spec = pl.BlockSpec((tm, tk), ix(lambda i, j, q: (i, q)))
        b_spec = pl.BlockSpec((tk, tn), ix(lambda i, j, q: (q, j)))
        dims = (((1,), (0,)), ((), ()))
    elif mode == "nt":
        (m, k), (n, _) = a.shape, b.shape
        a_spec = pl.BlockSpec((tm, tk), ix(lambda i, j, q: (i, q)))
        b_spec = pl.BlockSpec((tn, tk), ix(lambda i, j, q: (j, q)))
        dims = (((1,), (1,)), ((), ()))
    else:
        (k, m), (_, n) = a.shape, b.shape
        a_spec = pl.BlockSpec((tk, tm), ix(lambda i, j, q: (q, i)))
        b_spec = pl.BlockSpec((tk, tn), ix(lambda i, j, q: (q, j)))
        dims = (((0,), (0,)), ((), ()))
    assert m % tm == 0 and n % tn == 0 and k % tk == 0, (name, a.shape, b.shape)
    nk = k // tk
    grid = (n // tn, m // tm, nk) if cols_outer else (m // tm, n // tn, nk)

    def body(a_ref, b_ref, o_ref, acc_ref):
        part = lax.dot_general(a_ref[...].astype(BF16), b_ref[...].astype(BF16), dims, preferred_element_type=F32)
        if nk == 1:
            o_ref[...] = part.astype(o_ref.dtype)
        else:
            q = pl.program_id(2)

            @pl.when(q == 0)
            def _():
                acc_ref[...] = part

            @pl.when(q > 0)
            def _():
                acc_ref[...] += part

            @pl.when(q == nk - 1)
            def _():
                o_ref[...] = acc_ref[...].astype(o_ref.dtype)

    res = _grid_call(
        body, name, grid, [a_spec, b_spec], [pl.BlockSpec((tm, tn), ix(lambda i, j, q: (i, j)))],
        [jax.ShapeDtypeStruct((m, n), out_dtype)], [pltpu.VMEM((tm, tn) if nk > 1 else (SUB, LANE), F32)], (a, b), comm)
    return res[0] if comm is None else res


def _row_mask(i, tpe, rows):
    r = lax.broadcasted_iota(jnp.int32, (rows, 1), 0)
    return jnp.logical_or((i % tpe) != 0, r >= PAD)


def _rms_fwd(h, g, rt, name):
    t, d = h.shape

    def body(h_ref, g_ref, o_ref):
        x = h_ref[...]
        r = lax.rsqrt(jnp.mean(x * x, axis=-1, keepdims=True) + NORM_EPS)
        o_ref[...] = (x * r * g_ref[...]).astype(BF16)

    return pl.pallas_call(
        body, name=name, grid=(t // rt,),
        in_specs=[pl.BlockSpec((rt, d), lambda i: (i, 0)), pl.BlockSpec((1, d), lambda i: (0, 0))],
        out_specs=pl.BlockSpec((rt, d), lambda i: (i, 0)),
        out_shape=jax.ShapeDtypeStruct((t, d), BF16),
        compiler_params=_cparams(("parallel",)),
    )(h, g)


def _post_fwd(h, m, g, rt, name):
    t, d = h.shape

    def body(h_ref, m_ref, g_ref, o_ref):
        x = m_ref[...]
        r = lax.rsqrt(jnp.mean(x * x, axis=-1, keepdims=True) + NORM_EPS)
        o_ref[...] = h_ref[...] + x * r * g_ref[...]

    row = pl.BlockSpec((rt, d), lambda i: (i, 0))
    return pl.pallas_call(
        body, name=name, grid=(t // rt,),
        in_specs=[row, row, pl.BlockSpec((1, d), lambda i: (0, 0))], out_specs=row,
        out_shape=jax.ShapeDtypeStruct((t, d), F32),
        compiler_params=_cparams(("parallel",)),
    )(h, m, g)


def _post_rms_fwd(h, m, g_post, g_next, rt, name):
    t, d = h.shape

    def body(h_ref, m_ref, gp_ref, gn_ref, o_ref, n_ref):
        x = m_ref[...]
        r = lax.rsqrt(jnp.mean(x * x, axis=-1, keepdims=True) + NORM_EPS)
        y = h_ref[...] + x * r * gp_ref[...]
        o_ref[...] = y
        r2 = lax.rsqrt(jnp.mean(y * y, axis=-1, keepdims=True) + NORM_EPS)
        n_ref[...] = (y * r2 * gn_ref[...]).astype(BF16)

    row = pl.BlockSpec((rt, d), lambda i: (i, 0))
    vec = pl.BlockSpec((1, d), lambda i: (0, 0))
    return pl.pallas_call(
        body, name=name, grid=(t // rt,),
        in_specs=[row, row, vec, vec], out_specs=[row, row],
        out_shape=[jax.ShapeDtypeStruct((t, d), F32), jax.ShapeDtypeStruct((t, d), BF16)],
        compiler_params=_cparams(("parallel",)),
    )(h, m, g_post, g_next)


def _rms_bwd_math(x, g, dy):
    r = lax.rsqrt(jnp.mean(x * x, axis=-1, keepdims=True) + NORM_EPS)
    gdy = dy * g
    dx = r * gdy - x * (r * r * r) * jnp.mean(gdy * x, axis=-1, keepdims=True)
    return dx, dy * x * r


def _post_bwd(dh, m, g, rt, name):
    t, d = dh.shape

    def body(dh_ref, m_ref, g_ref, dm_ref, dg_ref):
        dm, dgt = _rms_bwd_math(m_ref[...], g_ref[...], dh_ref[...])
        dm_ref[...] = dm.astype(BF16)

        @pl.when(pl.program_id(0) == 0)
        def _():
            dg_ref[...] = jnp.zeros_like(dg_ref)

        dg_ref[...] += jnp.sum(dgt, axis=0, keepdims=True)

    row = pl.BlockSpec((rt, d), lambda i: (i, 0))
    vec = pl.BlockSpec((1, d), lambda i: (0, 0))
    return pl.pallas_call(
        body, name=name, grid=(t // rt,),
        in_specs=[row, row, vec], out_specs=[row, vec],
        out_shape=[jax.ShapeDtypeStruct((t, d), BF16), jax.ShapeDtypeStruct((1, d), F32)],
        compiler_params=_cparams(("arbitrary",)),
    )(dh, m, g)


def _rms_bwd(dh_res, dhn, h, g, lp, rt, name):
    t, d = h.shape
    tpe = lp // rt

    def body(dr_ref, dn_ref, h_ref, g_ref, dh_ref, dg_ref, dmeta_ref):
        i = pl.program_id(0)
        dx, dgt = _rms_bwd_math(h_ref[...], g_ref[...], dn_ref[...].astype(F32))
        dh = dr_ref[...] + dx
        dh_ref[...] = dh

        @pl.when(i == 0)
        def _():
            dg_ref[...] = jnp.zeros_like(dg_ref)
            dmeta_ref[...] = jnp.zeros_like(dmeta_ref)

        dg_ref[...] += jnp.sum(dgt, axis=0, keepdims=True)

        @pl.when((i % tpe) == 0)
        def _():
            dmeta_ref[...] += dh[PAD:PAD + N_META, :]

    row = pl.BlockSpec((rt, d), lambda i: (i, 0))
    vec = pl.BlockSpec((1, d), lambda i: (0, 0))
    return pl.pallas_call(
        body, name=name, grid=(t // rt,),
        in_specs=[row, row, row, vec],
        out_specs=[row, vec, pl.BlockSpec((N_META, d), lambda i: (0, 0))],
        out_shape=[jax.ShapeDtypeStruct((t, d), F32), jax.ShapeDtypeStruct((1, d), F32),
                   jax.ShapeDtypeStruct((N_META, d), F32)],
        compiler_params=_cparams(("arbitrary",)),
    )(dh_res, dhn, h, g)


def _rms_post_bwd(dh_res, dhn, h, g, m_prev, g_prev, rt, name):
    t, d = h.shape

    def body(dr_ref, dn_ref, h_ref, g_ref, m_ref, gp_ref, dh_ref, dg_ref, dm_ref, dgp_ref):
        dx, dgt = _rms_bwd_math(h_ref[...], g_ref[...], dn_ref[...].astype(F32))
        dh = dr_ref[...] + dx
        dh_ref[...] = dh
        dm, dgt_prev = _rms_bwd_math(m_ref[...], gp_ref[...], dh)
        dm_ref[...] = dm.astype(BF16)

        @pl.when(pl.program_id(0) == 0)
        def _():
            dg_ref[...] = jnp.zeros_like(dg_ref)
            dgp_ref[...] = jnp.zeros_like(dgp_ref)

        dg_ref[...] += jnp.sum(dgt, axis=0, keepdims=True)
        dgp_ref[...] += jnp.sum(dgt_prev, axis=0, keepdims=True)

    row = pl.BlockSpec((rt, d), lambda i: (i, 0))
    vec = pl.BlockSpec((1, d), lambda i: (0, 0))
    return pl.pallas_call(
        body, name=name, grid=(t // rt,),
        in_specs=[row, row, row, vec, row, vec], out_specs=[row, vec, row, vec],
        out_shape=[jax.ShapeDtypeStruct((t, d), F32), jax.ShapeDtypeStruct((1, d), F32),
                   jax.ShapeDtypeStruct((t, d), BF16), jax.ShapeDtypeStruct((1, d), F32)],
        compiler_params=_cparams(("arbitrary",)),
    )(dh_res, dhn, h, g, m_prev, g_prev)


def _loss_kernel(h, target, lp, name):
    t, d = h.shape
    nb = t // lp
    seq = lp - PAD - N_META
    rows = next(r for r in (1024, 512, 256, 128, CHUNK) if seq % r == 0)
    per = seq // rows

    def body(h_ref, t_ref, zero_ref, dh_ref, loss_ref):
        @pl.when(jnp.logical_and(pl.program_id(0) == 0, pl.program_id(1) == 0))
        def _():
            loss_ref[...] = jnp.zeros_like(loss_ref)

        err = h_ref[...] - t_ref[...]
        dh_ref[...] = err * (1.0 / d)
        loss_ref[...] += (0.5 / d) * jnp.sum(err * err)

    win = pl.BlockSpec((pl.Element(rows), pl.Element(d)),
                       lambda b, j: (pl.multiple_of(b * lp + PAD + N_META + j * rows, CHUNK), 0))
    return pl.pallas_call(
        body, name=name, grid=(nb, per),
        in_specs=[win, pl.BlockSpec((rows, d), lambda b, j: (b * per + j, 0)), HBM_SPEC],
        out_specs=[win, pl.BlockSpec((SUB, LANE), lambda b, j: (0, 0))],
        out_shape=[jax.ShapeDtypeStruct((t, d), F32), jax.ShapeDtypeStruct((SUB, LANE), F32)],
        input_output_aliases={2: 0},
        compiler_params=_cparams(("arbitrary", "arbitrary")),
    )(h, target, jnp.zeros((t, d), F32))


def _window_spec(rows, width, col):
    return pl.BlockSpec((pl.Element(rows), pl.Element(width)), lambda i: (i * rows, col))


def _halo_specs(t, width, col, halo, tile):
    cur = _window_spec(tile, width, col)
    prev = pl.BlockSpec((pl.Element(halo), pl.Element(width)),
                        lambda i: (pl.multiple_of(jnp.maximum(i * tile - halo, 0), halo), col))
    nxt = pl.BlockSpec((pl.Element(halo), pl.Element(width)),
                       lambda i: (pl.multiple_of(jnp.minimum((i + 1) * tile, t - halo), halo), col))
    return cur, prev, nxt


def _f32(ref, lo, hi):
    return ref[:, lo:hi].astype(F32)


def _rows_from(x, start, rows):
    s = start % SUB
    if s == 0:
        return x[start:start + rows]
    return pltpu.roll(x, x.shape[0] - s, axis=0)[start - s:start - s + rows]


def _conv_a(ve, w):
    rows = ve.shape[0] - HALO_A
    return (w[0:1] * _rows_from(ve, HALO_A - 2, rows) + w[1:2] * _rows_from(ve, HALO_A - 1, rows)
            + w[2:3] * ve[HALO_A:HALO_A + rows])


def _a_fwd(proj, w, lp, name):
    t = proj.shape[0]
    tile = lp // SHORT_TILES
    tpe = lp // tile
    cur, prev, _ = _halo_specs(t, 4 * D_A, 0, HALO_A, tile)

    def body(p_ref, ph_ref, w_ref, y_ref):
        first = (pl.program_id(0) % tpe) == 0
        v = _f32(p_ref, D_A, 2 * D_A) * _f32(p_ref, 2 * D_A, 3 * D_A)
        vh = jnp.where(first, 0.0, _f32(ph_ref, D_A, 2 * D_A) * _f32(ph_ref, 2 * D_A, 3 * D_A))
        cv = _conv_a(jnp.concatenate([vh, v], axis=0), w_ref[...])
        y_ref[...] = (_f32(p_ref, 0, D_A) * cv * _silu(_f32(p_ref, 3 * D_A, 4 * D_A))).astype(BF16)

    return pl.pallas_call(
        body, name=name, grid=(t // tile,),
        in_specs=[cur, prev, pl.BlockSpec((CONV_A_K, D_A), lambda i: (0, 0))],
        out_specs=_window_spec(tile, D_A, D_B),
        out_shape=jax.ShapeDtypeStruct((t, D_B + D_A + D_C), BF16),
        compiler_params=_cparams(("parallel",)),
    )(proj, proj, w)


def _a_bwd(dy, proj, w, dproj, lp, name):
    t = proj.shape[0]
    tile = lp // SHORT_TILES
    tpe = lp // tile
    cur, prev, nxt = _halo_specs(t, 4 * D_A, 0, HALO_A, tile)
    dcur, _, dnxt = _halo_specs(t, D_A, D_B, HALO_A, tile)

    def body(dy_ref, dyn_ref, p_ref, ph_ref, pn_ref, w_ref, dproj_ref, dp_ref, dw_ref):
        i = pl.program_id(0)
        first = (i % tpe) == 0
        last = (i % tpe) == tpe - 1
        w = w_ref[...]
        ab, ac, ax, az = (_f32(p_ref, k * D_A, (k + 1) * D_A) for k in range(4))
        v = ac * ax
        vh = jnp.where(first, 0.0, _f32(ph_ref, D_A, 2 * D_A) * _f32(ph_ref, 2 * D_A, 3 * D_A))
        ve = jnp.concatenate([vh, v], axis=0)
        taps = [_rows_from(ve, HALO_A - 2 + k, tile) for k in range(CONV_A_K)]
        cv = w[0:1] * taps[0] + w[1:2] * taps[1] + w[2:3] * taps[2]
        s = _silu(az)
        dy_ = dy_ref[...].astype(F32)
        dcv = dy_ * ab * s
        dcvn = jnp.where(last, 0.0, dyn_ref[...].astype(F32) * _f32(pn_ref, 0, D_A) * _silu(_f32(pn_ref, 3 * D_A, 4 * D_A)))
        dce = jnp.concatenate([dcv, dcvn], axis=0)
        dv = w[2:3] * dce[0:tile] + w[1:2] * _rows_from(dce, 1, tile) + w[0:1] * _rows_from(dce, 2, tile)
        dp = jnp.concatenate([dy_ * cv * s, dv * ax, dv * ac, dy_ * ab * cv * _dsilu(az)], axis=1)
        dp_ref[...] = jnp.where(_row_mask(i, tpe, tile), dp, 0.0).astype(BF16)
        dw = jnp.concatenate(
            [jnp.sum(dcv * taps[k], axis=0, keepdims=True) for k in range(CONV_A_K)], axis=0)

        @pl.when(i == 0)
        def _():
            dw_ref[...] = jnp.zeros_like(dw_ref)

        dw_ref[...] += dw

    wspec = pl.BlockSpec((CONV_A_K, D_A), lambda i: (0, 0))
    return pl.pallas_call(
        body, name=name, grid=(t // tile,),
        in_specs=[dcur, dnxt, cur, prev, nxt, wspec, HBM_SPEC],
        out_specs=[_window_spec(tile, 4 * D_A, 0), wspec],
        out_shape=[jax.ShapeDtypeStruct(dproj.shape, dproj.dtype), jax.ShapeDtypeStruct((CONV_A_K, D_A), F32)],
        input_output_aliases={6: 0},
        compiler_params=_cparams(("arbitrary",)),
    )(dy, dy, proj, proj, proj, w, dproj)


def _conv_ssm(xe, w, rows, off):
    acc = w[0:1] * _rows_from(xe, off - 3, rows)
    for k in range(1, SSM_K):
        acc = acc + w[k:k + 1] * _rows_from(xe, off - 3 + k, rows)
    return acc


def _xbc_fwd(proj, w, b, lp, name):
    t = proj.shape[0]
    tile = lp // SHORT_TILES
    tpe = lp // tile
    cur, prev, _ = _halo_specs(t, N_XBC, COL_XBC, HALO_A, tile)

    def body(x_ref, xh_ref, w_ref, b_ref, o_ref):
        first = (pl.program_id(0) % tpe) == 0
        xh = jnp.where(first, 0.0, xh_ref[...].astype(F32))
        xe = jnp.concatenate([xh, x_ref[...].astype(F32)], axis=0)
        o_ref[...] = _silu(_conv_ssm(xe, w_ref[...], tile, HALO_A) + b_ref[...]).astype(BF16)

    return pl.pallas_call(
        body, name=name, grid=(t // tile,),
        in_specs=[cur, prev, pl.BlockSpec((SSM_K, N_XBC), lambda i: (0, 0)), pl.BlockSpec((1, N_XBC), lambda i: (0, 0))],
        out_specs=pl.BlockSpec((tile, N_XBC), lambda i: (i, 0)),
        out_shape=jax.ShapeDtypeStruct((t, N_XBC), BF16),
        compiler_params=_cparams(("parallel",)),
    )(proj, proj, w, b)


def _xbc_bwd(dxbc, proj, w, b, dproj, lp, name):
    t = proj.shape[0]
    tile = lp // SHORT_TILES
    tpe = lp // tile
    cur, prev, nxt = _halo_specs(t, N_XBC, COL_XBC, HALO_A, tile)
    dcur, _, dnxt = _halo_specs(t, N_XBC, 0, HALO_A, tile)

    def body(d_ref, dn_ref, x_ref, xh_ref, xn_ref, w_ref, b_ref, dproj_ref, dx_ref, dw_ref, db_ref):
        i = pl.program_id(0)
        first = (i % tpe) == 0
        last = (i % tpe) == tpe - 1
        w = w_ref[...]
        xh = jnp.where(first, 0.0, xh_ref[...].astype(F32))
        xe = jnp.concatenate([xh, x_ref[...].astype(F32), xn_ref[...].astype(F32)], axis=0)
        taps = [_rows_from(xe, HALO_A - 3 + k, tile + HALO_A) for k in range(SSM_K)]
        pre = b_ref[...] + w[0:1] * taps[0]
        for k in range(1, SSM_K):
            pre = pre + w[k:k + 1] * taps[k]
        de = jnp.concatenate([d_ref[...].astype(F32), jnp.where(last, 0.0, dn_ref[...].astype(F32))], axis=0)
        dpre = de * _dsilu(pre)
        dx = w[3:4] * dpre[0:tile]
        for k in range(SSM_K - 1):
            dx = dx + w[k:k + 1] * _rows_from(dpre, 3 - k, tile)
        dx_ref[...] = jnp.where(_row_mask(i, tpe, tile), dx, 0.0).astype(BF16)
        dpc = dpre[0:tile]
        dw = jnp.concatenate(
            [jnp.sum(dpc * taps[k][0:tile], axis=0, keepdims=True) for k in range(SSM_K)], axis=0)

        @pl.when(i == 0)
        def _():
            dw_ref[...] = jnp.zeros_like(dw_ref)
            db_ref[...] = jnp.zeros_like(db_ref)

        dw_ref[...] += dw
        db_ref[...] += jnp.sum(dpc, axis=0, keepdims=True)

    wspec = pl.BlockSpec((SSM_K, N_XBC), lambda i: (0, 0))
    bspec = pl.BlockSpec((1, N_XBC), lambda i: (0, 0))
    return pl.pallas_call(
        body, name=name, grid=(t // tile,),
        in_specs=[dcur, dnxt, cur, prev, nxt, wspec, bspec, HBM_SPEC],
        out_specs=[_window_spec(tile, N_XBC, COL_XBC), wspec, bspec],
        out_shape=[jax.ShapeDtypeStruct(dproj.shape, dproj.dtype), jax.ShapeDtypeStruct((SSM_K, N_XBC), F32),
                   jax.ShapeDtypeStruct((1, N_XBC), F32)],
        input_output_aliases={7: 0},
        compiler_params=_cparams(("arbitrary",)),
    )(dxbc, dxbc, proj, proj, proj, w, b, dproj)


SUBROWS = 32


def _fill_shifted(scr, x):
    scr[0] = x
    for s in range(1, SUB):
        scr[s] = pltpu.roll(x, x.shape[0] - s, axis=0)


def _window(scr, start, rows):
    s = start % SUB
    return scr[s, start - s:start - s + rows, :]


def _conv_conf(scr, w, rows, off, base):
    acc = w[0:1] * _window(scr, base + off - (CONF_K - 1), rows)
    for k in range(1, CONF_K):
        acc = acc + w[k:k + 1] * _window(scr, base + off - (CONF_K - 1) + k, rows)
    return acc


def _ln_fwd(u1, g, b):
    mu = jnp.mean(u1, axis=-1, keepdims=True)
    xc = u1 - mu
    rstd = lax.rsqrt(jnp.mean(xc * xc, axis=-1, keepdims=True) + LN_EPS)
    n = xc * rstd
    return n, rstd, n * g + b


def _c_fwd(proj, w, cb, g, b, ybuf, lp, name):
    t = proj.shape[0]
    tile = lp // SHORT_TILES
    tpe = lp // tile
    cur, prev, _ = _halo_specs(t, 3 * D_C, COL_C, HALO_C, tile)

    def body(p_ref, ph_ref, w_ref, cb_ref, g_ref, b_ref, ybuf_ref, y_ref, u1_ref, u0_scr):
        first = (pl.program_id(0) % tpe) == 0
        u0h = jnp.where(first, 0.0, _f32(ph_ref, 0, D_C) * jax.nn.sigmoid(_f32(ph_ref, D_C, 2 * D_C)))
        _fill_shifted(u0_scr, jnp.concatenate([u0h, _f32(p_ref, 0, D_C) * jax.nn.sigmoid(_f32(p_ref, D_C, 2 * D_C))], axis=0))
        w = w_ref[...]
        for r0 in range(0, tile, SUBROWS):
            u1 = _conv_conf(u0_scr, w, SUBROWS, HALO_C, r0) + cb_ref[...]
            u1_ref[r0:r0 + SUBROWS, :] = u1.astype(BF16)
            _, _, u2 = _ln_fwd(u1, g_ref[...], b_ref[...])
            cz = p_ref[r0:r0 + SUBROWS, 2 * D_C:3 * D_C].astype(F32)
            y_ref[r0:r0 + SUBROWS, :] = (_silu(u2) * _silu(cz)).astype(BF16)

    vec = pl.BlockSpec((1, D_C), lambda i: (0, 0))
    return pl.pallas_call(
        body, name=name, grid=(t // tile,),
        in_specs=[cur, prev, pl.BlockSpec((CONF_K, D_C), lambda i: (0, 0)), vec, vec, vec, HBM_SPEC],
        out_specs=[_window_spec(tile, D_C, D_B + D_A), pl.BlockSpec((tile, D_C), lambda i: (i, 0))],
        out_shape=[jax.ShapeDtypeStruct(ybuf.shape, ybuf.dtype), jax.ShapeDtypeStruct((t, D_C), BF16)],
        scratch_shapes=[pltpu.VMEM((SUB, HALO_C + tile, D_C), F32)],
        input_output_aliases={6: 0},
        compiler_params=_cparams(("parallel",)),
    )(proj, proj, w, cb, g, b, ybuf)


def _c_bwd(dy, proj, u1, w, cb, g, b, dproj, lp, name):
    t = proj.shape[0]
    tile = lp // SHORT_TILES
    tpe = lp // tile
    cur, prev, nxt = _halo_specs(t, 3 * D_C, COL_C, HALO_C, tile)
    dcur, _, dnxt = _halo_specs(t, D_C, D_B + D_A, HALO_C, tile)
    ucur, _, unxt = _halo_specs(t, D_C, 0, HALO_C, tile)
    ext = tile + HALO_C

    def body(dy_ref, dyn_ref, p_ref, ph_ref, pn_ref, u1_ref, u1n_ref, w_ref, cb_ref, g_ref, b_ref, dproj_ref,
             dp_ref, dw_ref, dcb_ref, dg_ref, db_ref, u0_scr, du1_scr, wacc_scr):
        i = pl.program_id(0)
        first = (i % tpe) == 0
        last = (i % tpe) == tpe - 1
        w = w_ref[...]

        @pl.when(i == 0)
        def _():
            dw_ref[...] = jnp.zeros_like(dw_ref)
            dcb_ref[...] = jnp.zeros_like(dcb_ref)
            dg_ref[...] = jnp.zeros_like(dg_ref)
            db_ref[...] = jnp.zeros_like(db_ref)

        u0h = jnp.where(first, 0.0, _f32(ph_ref, 0, D_C) * jax.nn.sigmoid(_f32(ph_ref, D_C, 2 * D_C)))
        _fill_shifted(u0_scr, jnp.concatenate(
            [u0h, _f32(p_ref, 0, D_C) * jax.nn.sigmoid(_f32(p_ref, D_C, 2 * D_C))], axis=0))
        dcb = jnp.zeros((1, D_C), F32)
        dg = jnp.zeros((1, D_C), F32)
        db = jnp.zeros((1, D_C), F32)
        for r0 in range(0, ext, SUBROWS):
            in_tile = r0 < tile
            src, dsrc, usrc, q0 = (p_ref, dy_ref, u1_ref, r0) if in_tile else (pn_ref, dyn_ref, u1n_ref, r0 - tile)
            u1 = usrc[q0:q0 + SUBROWS, :].astype(F32)
            n, rstd, u2 = _ln_fwd(u1, g_ref[...], b_ref[...])
            cz = src[q0:q0 + SUBROWS, 2 * D_C:3 * D_C].astype(F32)
            dyc = dsrc[q0:q0 + SUBROWS, :].astype(F32)
            if not in_tile:
                dyc = jnp.where(last, 0.0, dyc)
            du2 = dyc * _silu(cz) * _dsilu(u2)
            dn = du2 * g_ref[...]
            du1 = rstd * (dn - jnp.mean(dn, axis=-1, keepdims=True) - n * jnp.mean(dn * n, axis=-1, keepdims=True))
            du1_scr[0, r0:r0 + SUBROWS, :] = du1
            if in_tile:
                dp_ref[r0:r0 + SUBROWS, 2 * D_C:3 * D_C] = (dyc * _silu(u2) * _dsilu(cz)).astype(BF16)
                dg = dg + jnp.sum(du2 * n, axis=0, keepdims=True)
                db = db + jnp.sum(du2, axis=0, keepdims=True)
                dcb = dcb + jnp.sum(du1, axis=0, keepdims=True)
        dcb_ref[...] += dcb
        dg_ref[...] += dg
        db_ref[...] += db
        mask = _row_mask(i, tpe, tile)
        _fill_shifted(du1_scr, du1_scr[0])
        for r0 in range(0, tile, SUBROWS):
            acc = w[0:1] * _window(du1_scr, r0 + CONF_K - 1, SUBROWS)
            for k in range(1, CONF_K):
                acc = acc + w[k:k + 1] * _window(du1_scr, r0 + CONF_K - 1 - k, SUBROWS)
            ca = p_ref[r0:r0 + SUBROWS, 0:D_C].astype(F32)
            sg = jax.nn.sigmoid(p_ref[r0:r0 + SUBROWS, D_C:2 * D_C].astype(F32))
            m = mask[r0:r0 + SUBROWS]
            dp_ref[r0:r0 + SUBROWS, 0:D_C] = jnp.where(m, acc * sg, 0.0).astype(BF16)
            dp_ref[r0:r0 + SUBROWS, D_C:2 * D_C] = jnp.where(m, acc * ca * sg * (1.0 - sg), 0.0).astype(BF16)
        for k in range(CONF_K):
            part = jnp.zeros((SUB, D_C), F32)
            for r0 in range(0, tile, SUBROWS):
                prod = du1_scr[0, r0:r0 + SUBROWS, :] * _window(u0_scr, HALO_C + r0 - (CONF_K - 1) + k, SUBROWS)
                for q in range(0, SUBROWS, SUB):
                    part = part + prod[q:q + SUB]
            wacc_scr[k:k + 1, :] = jnp.sum(part, axis=0, keepdims=True)
        dw_ref[...] += wacc_scr[0:CONF_K, :]

    vec = pl.BlockSpec((1, D_C), lambda i: (0, 0))
    wspec = pl.BlockSpec((CONF_K, D_C), lambda i: (0, 0))
    return pl.pallas_call(
        body, name=name, grid=(t // tile,),
        in_specs=[dcur, dnxt, cur, prev, nxt, ucur, unxt, wspec, vec, vec, vec, HBM_SPEC],
        out_specs=[_window_spec(tile, 3 * D_C, COL_C), wspec, vec, vec, vec],
        out_shape=[jax.ShapeDtypeStruct(dproj.shape, dproj.dtype), jax.ShapeDtypeStruct((CONF_K, D_C), F32),
                   jax.ShapeDtypeStruct((1, D_C), F32), jax.ShapeDtypeStruct((1, D_C), F32),
                   jax.ShapeDtypeStruct((1, D_C), F32)],
        scratch_shapes=[pltpu.VMEM((SUB, ext, D_C), F32), pltpu.VMEM((SUB, ext, D_C), F32),
                        pltpu.VMEM((HALO_C, D_C), F32)],
        input_output_aliases={11: 0},
        compiler_params=_cparams(("arbitrary",)),
    )(dy, dy, proj, proj, proj, u1, u1, w, cb, g, b, dproj)


def _split_dot(x, m_bf16, terms):
    acc = None
    rem = x
    for _ in range(terms):
        hi = rem.astype(BF16)
        part = jnp.dot(hi, m_bf16, preferred_element_type=F32)
        acc = part if acc is None else acc + part
        rem = rem - hi.astype(F32)
    return acc


def _split_pieces(x, terms):
    out, rem = [], x
    for _ in range(terms):
        hi = rem.astype(BF16)
        out.append(hi)
        rem = rem - hi.astype(F32)
    return out


def _split_dot_many(xs, terms, m_bf16):
    pieces = [p for x, n in zip(xs, terms) for p in _split_pieces(x, n)]
    prod = jnp.dot(jnp.concatenate(pieces, axis=0), m_bf16, preferred_element_type=F32)
    out, off = [], 0
    for x, n in zip(xs, terms):
        rows = x.shape[0]
        acc = prod[off:off + rows]
        for q in range(1, n):
            acc = acc + prod[off + q * rows:off + (q + 1) * rows]
        out.append(acc)
        off += n * rows
    return out


def _split_dot_left(m_bf16, x, terms):
    cols = x.shape[1]
    prod = jnp.dot(m_bf16, jnp.concatenate(_split_pieces(x, terms), axis=1), preferred_element_type=F32)
    acc = prod[:, 0:cols]
    for q in range(1, terms):
        acc = acc + prod[:, q * cols:(q + 1) * cols]
    return acc


def _tri(rows_ge_cols):
    r = lax.broadcasted_iota(jnp.int32, (CHUNK, CHUNK), 0)
    c = lax.broadcasted_iota(jnp.int32, (CHUNK, CHUNK), 1)
    return (r >= c) if rows_ge_cols else (r <= c)


def _softplus(x):
    return jnp.maximum(x, 0.0) + jnp.log(1.0 + jnp.exp(-jnp.abs(x)))


def _ssd_common(dtraw, dtb, dtb_t, a_log, a_log_t, e_mat, valid_col, valid_row, x_terms=2):
    a = -jnp.exp(a_log)
    lane = lax.broadcasted_iota(jnp.int32, (1, LANE), 1)
    a = jnp.where(lane < N_HEADS, a, 0.0)
    a_t = -jnp.exp(a_log_t)
    dt = jnp.where(valid_col, _softplus(dtraw + dtb), 0.0)
    dt = jnp.where(lane < N_HEADS, dt, 0.0)
    dt_t = jnp.where(valid_row, _softplus(dtraw.T[0:N_HEADS, :] + dtb_t), 0.0)
    ltri = _tri(True).astype(BF16)
    utri = _tri(False).astype(BF16)
    big_a = _split_dot_left(ltri, dt * a, 3)
    big_a_t = _split_dot_many([dt_t * a_t], [3], utri)[0]
    e_a = jnp.exp(big_a)
    d_s = jnp.exp(big_a[CHUNK - 1:CHUNK, :] - big_a)
    if x_terms == 1:
        dt_x, e_a_x, d_s_x = _split_dot_many([dt, e_a, d_s], [1, 2, 1], e_mat)
    else:
        dt_x, e_a_x, d_s_x = (_split_dot(q, e_mat, x_terms) for q in (dt, e_a, d_s))
    cd_x = e_a_x[CHUNK - 1:CHUNK, :]
    return a, dt, big_a, big_a_t, e_a, d_s, dt_x, e_a_x, d_s_x, cd_x


def _decay(big_a, big_a_t, h, transposed):
    col = big_a[:, h:h + 1]
    row = big_a_t[h:h + 1, :]
    if not transposed:
        seg = col - row
        return jnp.where(_tri(True), jnp.exp(jnp.minimum(seg, 0.0)), 0.0)
    seg = row - col
    return jnp.where(_tri(False), jnp.exp(jnp.minimum(seg, 0.0)), 0.0)


NT_DIMS = (((1,), (1,)), ((), ()))
TN_DIMS = (((0,), (0,)), ((), ()))
HBM_SPEC = pl.BlockSpec(memory_space=pl.ANY)


class _Comm:
    def __init__(self, inputs, out_shapes, sem_shapes, copies, aliases=None):
        self.inputs, self.out_shapes, self.sem_shapes, self.copies = inputs, out_shapes, sem_shapes, copies
        self.aliases = aliases or {}

    def start(self, ins, outs, sems):
        local, sends, _ = self.copies(ins, outs, sems, False)
        for cp in local + sends:
            cp.start()

    def wait(self, ins, outs, sems):
        local, sends, recvs = self.copies(ins, outs, sems, True)
        for cp in recvs:
            cp.wait_recv()
        for cp in sends:
            cp.wait_send()
        for cp in local:
            cp.wait()


def _merge_comm(*comms):
    comms = [c for c in comms if c is not None]
    if len(comms) <= 1:
        return comms[0] if comms else None

    def copies(ins, outs, sems, with_recvs):
        local, sends, recvs = [], [], []
        i0 = o0 = s0 = 0
        for c in comms:
            ni, no, ns = len(c.inputs), len(c.out_shapes), len(c.sem_shapes)
            loc, snd, rcv = c.copies(ins[i0:i0 + ni], outs[o0:o0 + no], sems[s0:s0 + ns], with_recvs)
            local, sends, recvs = local + loc, sends + snd, recvs + rcv
            i0, o0, s0 = i0 + ni, o0 + no, s0 + ns
        return local, sends, recvs

    aliases, i0, o0 = {}, 0, 0
    for c in comms:
        aliases.update({i0 + k: o0 + v for k, v in c.aliases.items()})
        i0, o0 = i0 + len(c.inputs), o0 + len(c.out_shapes)
    return _Comm([a for c in comms for a in c.inputs], [s for c in comms for s in c.out_shapes],
                 [s for c in comms for s in c.sem_shapes], copies, aliases)


def _grid_call(body, name, grid, in_specs, out_specs, out_shape, scratch_shapes, operands, comm=None, aliases=None):
    aliases = dict(aliases or {})
    if comm is None:
        return pl.pallas_call(
            body, name=name, grid=grid, in_specs=in_specs, out_specs=out_specs, out_shape=out_shape,
            scratch_shapes=scratch_shapes, input_output_aliases=aliases,
            compiler_params=_cparams(("arbitrary",) * len(grid)))(*operands)
    n_in, n_out, n_scr = len(in_specs), len(out_specs), len(scratch_shapes)
    nci, nco = len(comm.inputs), len(comm.out_shapes)

    def wrapped(*refs):
        ins, cins = refs[:n_in], refs[n_in:n_in + nci]
        o0 = n_in + nci
        outs, couts = refs[o0:o0 + n_out], refs[o0 + n_out:o0 + n_out + nco]
        s0 = o0 + n_out + nco
        scr, csems = refs[s0:s0 + n_scr], refs[s0 + n_scr:]
        first = pl.program_id(0) == 0
        last = pl.program_id(0) == grid[0] - 1
        for k in range(1, len(grid)):
            first = jnp.logical_and(first, pl.program_id(k) == 0)
            last = jnp.logical_and(last, pl.program_id(k) == grid[k] - 1)

        @pl.when(first)
        def _():
            comm.start(cins, couts, csems)

        body(*ins, *outs, *scr)

        @pl.when(last)
        def _():
            comm.wait(cins, couts, csems)

    res = pl.pallas_call(
        wrapped, name=name, grid=grid,
        in_specs=list(in_specs) + [HBM_SPEC] * nci, out_specs=list(out_specs) + [HBM_SPEC] * nco,
        out_shape=list(out_shape) + list(comm.out_shapes),
        scratch_shapes=list(scratch_shapes) + list(comm.sem_shapes),
        input_output_aliases={**aliases, **{n_in + k: n_out + v for k, v in comm.aliases.items()}},
        compiler_params=_cparams(("arbitrary",) * len(grid)))(*operands, *comm.inputs)
    return res


def _ssd_fwd(xbc, proj, dtb, a_log, d_skip_x, norm_g, e_mat, ybuf, lp, name, comm=None):
    t = xbc.shape[0]
    cpe = lp // CHUNK
    nb = t // lp
    dtb_p = jnp.pad(dtb.reshape(1, N_HEADS), ((0, 0), (0, LANE - N_HEADS)))
    alog_p = jnp.pad(a_log.reshape(1, N_HEADS), ((0, 0), (0, LANE - N_HEADS)))
    dtb_t = dtb.reshape(N_HEADS, 1)
    alog_t = a_log.reshape(N_HEADS, 1)

    def body(xbc_ref, bz_ref, dt_ref, dtb_ref, dtbt_ref, al_ref, alt_ref, dx_ref, g_ref, e_ref, ybuf_ref,
             yb_ref, ys_ref, st_ref, s_scr):
        c = pl.program_id(0)

        @pl.when(c == 0)
        def _():
            s_scr[...] = jnp.zeros_like(s_scr)

        rows = lax.broadcasted_iota(jnp.int32, (CHUNK, 1), 0)
        cols = lax.broadcasted_iota(jnp.int32, (1, CHUNK), 1)
        valid_col = jnp.logical_or(c > 0, rows >= PAD)
        valid_row = jnp.logical_or(c > 0, cols >= PAD)
        e_mat = e_ref[...]
        ex = []
        for b in range(nb):
            _, _, big_a, big_a_t, _, _, dt_x, e_a_x, d_s_x, cd_x = _ssd_common(
                dt_ref[b].astype(F32), dtb_ref[...], dtbt_ref[...], al_ref[...], alt_ref[...], e_mat, valid_col, valid_row)
            xs = xbc_ref[b, :, 0:D_B].astype(F32)
            xdt = xs * dt_x
            st_prev = s_scr[b]
            ex.append(dict(big_a=big_a, big_a_t=big_a_t, e_a_x=e_a_x, cd_x=cd_x, xs=xs,
                           bs=xbc_ref[b, :, D_B:D_B + N_GROUPS * N_STATE], cs=xbc_ref[b, :, D_B + N_GROUPS * N_STATE:N_XBC],
                           xdt_b=xdt.astype(BF16), st_prev=st_prev, st_b=st_prev.astype(BF16),
                           u_b=(xdt * d_s_x).astype(BF16), bz=bz_ref[b].astype(F32), y_parts=[], new_st=[]))
        for g in range(N_GROUPS):
            gs = slice(g * N_STATE, (g + 1) * N_STATE)
            gw = slice(g * GROUP_W, (g + 1) * GROUP_W)
            for q in ex:
                cb = lax.dot_general(q["cs"][:, gs], q["bs"][:, gs], NT_DIMS, preferred_element_type=F32)
                q["y_off"] = jnp.dot(q["cs"][:, gs], q["st_b"][:, gw], preferred_element_type=F32)
                q["cb"] = cb
                q["diag"] = []
            for e in range(N_HEADS // N_GROUPS):
                h = g * (N_HEADS // N_GROUPS) + e
                for q in ex:
                    m = (q["cb"] * _decay(q["big_a"], q["big_a_t"], h, False)).astype(BF16)
                    q["diag"].append(jnp.dot(m, q["xdt_b"][:, h * HEAD_DIM:(h + 1) * HEAD_DIM], preferred_element_type=F32))
            for q in ex:
                q["y_parts"].append(jnp.concatenate(q["diag"], axis=1) + q["y_off"] * q["e_a_x"][:, gw])
                upd = lax.dot_general(q["bs"][:, gs], q["u_b"][:, gw], TN_DIMS, preferred_element_type=F32)
                q["new_st"].append(q["st_prev"][:, gw] * q["cd_x"][:, gw] + upd)
        for b, q in enumerate(ex):
            y = jnp.concatenate(q["y_parts"], axis=1) + q["xs"] * dx_ref[...]
            z = y * _silu(q["bz"])
            r = lax.rsqrt(jnp.mean(z * z, axis=-1, keepdims=True) + NORM_EPS)
            q["y"], q["yb"] = y, (z * r * g_ref[...]).astype(BF16)
        for b, q in enumerate(ex):
            st_ref[b] = q["st_b"]
            s_scr[b] = jnp.concatenate(q["new_st"], axis=1)
            ys_ref[b] = q["y"].astype(BF16)
            yb_ref[b] = q["yb"]

    def row(width, col):
        return pl.BlockSpec((nb, CHUNK, width), lambda c: (0, c, col))

    def const(shape):
        return pl.BlockSpec(shape, lambda c: (0,) * len(shape))

    proj3 = proj.reshape(nb, lp, N_INP)
    ybuf, yssd, states, *rest = _grid_call(
        body, name, (cpe,),
        [row(N_XBC, 0), row(D_B, COL_BZ // D_B), row(LANE, COL_DT // LANE),
         const((1, LANE)), const((N_HEADS, 1)), const((1, LANE)), const((N_HEADS, 1)),
         const((1, D_B)), const((1, D_B)), const((LANE, D_B)), HBM_SPEC],
        [row(D_B, 0), row(D_B, 0), pl.BlockSpec((nb, None, N_STATE, D_B), lambda c: (0, c, 0, 0))],
        [jax.ShapeDtypeStruct((nb, lp, ybuf.shape[1]), ybuf.dtype), jax.ShapeDtypeStruct((nb, lp, D_B), BF16),
         jax.ShapeDtypeStruct((nb, cpe, N_STATE, D_B), BF16)],
        [pltpu.VMEM((nb, N_STATE, D_B), F32)],
        (xbc.reshape(nb, lp, N_XBC), proj3, proj3, dtb_p, dtb_t, alog_p, alog_t, d_skip_x, norm_g, e_mat,
         ybuf.reshape(nb, lp, ybuf.shape[1])), comm, aliases={10: 0})
    return (ybuf.reshape(t, -1), yssd.reshape(t, D_B), states, *rest)


def _ssd_bwd(dy, y_ssd, xbc, proj, states, dtb, a_log, d_skip_x, norm_g, e_mat, e_mat_t, lp, name, comm=None):
    t = xbc.shape[0]
    cpe = lp // CHUNK
    nb = t // lp
    hpg = N_HEADS // N_GROUPS
    dtb_p = jnp.pad(dtb.reshape(1, N_HEADS), ((0, 0), (0, LANE - N_HEADS)))
    alog_p = jnp.pad(a_log.reshape(1, N_HEADS), ((0, 0), (0, LANE - N_HEADS)))
    dtb_t = dtb.reshape(N_HEADS, 1)
    alog_t = a_log.reshape(N_HEADS, 1)

    def body(dy_ref, ys_ref, xbc_ref, bz_ref, dt_ref, st_ref, dtb_ref, dtbt_ref, al_ref, alt_ref, dx_ref, g_ref,
             e_ref, et_ref, dxbc_ref, dpw_ref, dg_ref, ddtb_ref, dal_ref, dd_ref, ds_scr):
        @pl.when(pl.program_id(0) == 0)
        def _():
            ds_scr[...] = jnp.zeros_like(ds_scr)
            dg_ref[...] = jnp.zeros_like(dg_ref)
            ddtb_ref[...] = jnp.zeros_like(ddtb_ref)
            dal_ref[...] = jnp.zeros_like(dal_ref)
            dd_ref[...] = jnp.zeros_like(dd_ref)

        stores, sums = [], {}
        chains = [one_example(dy_ref.at[b], ys_ref.at[b], xbc_ref.at[b], bz_ref.at[b], dt_ref.at[b], st_ref.at[b], dtb_ref,
                              dtbt_ref, al_ref, alt_ref, dx_ref, g_ref, e_ref, et_ref, dxbc_ref.at[b], dpw_ref.at[b],
                              ds_scr.at[b], stores, sums) for b in range(nb)]
        live = list(chains)
        while live:
            live = [ch for ch in live if next(ch, "done") != "done"]
        for ref, idx, val in stores:
            ref[idx] = val
        for ref, name_ in ((dg_ref, "dg"), (ddtb_ref, "ddtb"), (dal_ref, "dal"), (dd_ref, "dd")):
            total = sums[name_][0]
            for part in sums[name_][1:]:
                total = total + part
            ref[...] += total

    def one_example(dy_ref, ys_ref, xbc_ref, bz_ref, dt_ref, st_ref, dtb_ref, dtbt_ref, al_ref, alt_ref, dx_ref, g_ref,
                    e_ref, et_ref, dxbc_ref, dpw_ref, ds_scr, stores, sums):
        everything = (slice(None), slice(None))
        cc = cpe - 1 - pl.program_id(0)
        rows = lax.broadcasted_iota(jnp.int32, (CHUNK, 1), 0)
        cols = lax.broadcasted_iota(jnp.int32, (1, CHUNK), 1)
        valid_col = jnp.logical_or(cc > 0, rows >= PAD)
        valid_row = jnp.logical_or(cc > 0, cols >= PAD)
        e_mat = e_ref[...]
        e_mat_t = et_ref[...]
        dtraw = dt_ref[...].astype(F32)
        a, dt, big_a, big_a_t, e_a, d_s, dt_x, e_a_x, d_s_x, cd_x = _ssd_common(
            dtraw, dtb_ref[...], dtbt_ref[...], al_ref[...], alt_ref[...], e_mat, valid_col, valid_row, x_terms=1)
        xs = xbc_ref[:, 0:D_B].astype(F32)
        bs = xbc_ref[:, D_B:D_B + N_GROUPS * N_STATE]
        cs = xbc_ref[:, D_B + N_GROUPS * N_STATE:N_XBC]
        xdt = xs * dt_x
        xdt_b = xdt.astype(BF16)
        st_b = st_ref[...]
        dst = ds_scr[...]
        dst_b = dst.astype(BF16)

        ys = ys_ref[...].astype(F32)
        bz = bz_ref[...].astype(F32)
        sil = _silu(bz)
        z = ys * sil
        dz, dgt = _rms_bwd_math(z, g_ref[...], dy_ref[...].astype(F32))
        sums.setdefault("dg", []).append(jnp.sum(dgt, axis=0, keepdims=True))
        stores.append((dpw_ref, (slice(None), slice(0, D_B)), (dz * ys * _dsilu(bz)).astype(BF16)))
        dys = dz * sil

        dd_lane = jnp.sum(dys * xs, axis=0, keepdims=True)
        dxs = dys * dx_ref[...]
        w_x = dys * e_a_x
        w_b = w_x.astype(BF16)
        dys_b = dys.astype(BF16)
        u_b = (xdt * d_s_x).astype(BF16)
        dxdt_parts, dbs_parts, dcs_parts, off_parts, g1_parts = [], [], [], [], []
        da_diag = jnp.zeros((CHUNK, LANE), F32)
        lane = lax.broadcasted_iota(jnp.int32, (1, LANE), 1)
        for g in range(N_GROUPS):
            gs = slice(g * N_STATE, (g + 1) * N_STATE)
            gw = slice(g * GROUP_W, (g + 1) * GROUP_W)
            cs_g, bs_g = cs[:, gs], bs[:, gs]
            dcs = lax.dot_general(w_b[:, gw], st_b[:, gw], NT_DIMS, preferred_element_type=F32)
            y_off = jnp.dot(cs_g, st_b[:, gw], preferred_element_type=F32)
            off_parts.append(y_off)
            dst_new = lax.dot_general(cs_g, w_b[:, gw], TN_DIMS, preferred_element_type=F32)
            g1 = jnp.dot(bs_g, dst_b[:, gw], preferred_element_type=F32)
            g1_parts.append(g1)
            dbs = lax.dot_general(u_b[:, gw], dst_b[:, gw], NT_DIMS, preferred_element_type=F32)
            cb = lax.dot_general(cs_g, bs_g, NT_DIMS, preferred_element_type=F32)
            cbt = lax.dot_general(bs_g, cs_g, NT_DIMS, preferred_element_type=F32)
            dcb = jnp.zeros((CHUNK, CHUNK), F32)
            dcbt = jnp.zeros((CHUNK, CHUNK), F32)
            dxdt_h = []
            for e in range(hpg):
                h = g * hpg + e
                hs = slice(h * HEAD_DIM, (h + 1) * HEAD_DIM)
                dec = _decay(big_a, big_a_t, h, False)
                dect = _decay(big_a, big_a_t, h, True)
                m = cb * dec
                mt = cbt * dect
                dxdt_h.append(jnp.dot(mt.astype(BF16), dys_b[:, hs], preferred_element_type=F32))
                dm = lax.dot_general(dys_b[:, hs], xdt_b[:, hs], NT_DIMS, preferred_element_type=F32)
                dmt = lax.dot_general(xdt_b[:, hs], dys_b[:, hs], NT_DIMS, preferred_element_type=F32)
                dcb = dcb + dm * dec
                dcbt = dcbt + dmt * dect
                da_h = jnp.sum(dm * m - dmt * mt, axis=1, keepdims=True)
                da_diag = da_diag + jnp.where(lane == h, da_h, 0.0)
                yield
            dcs = dcs + jnp.dot(dcb.astype(BF16), bs_g, preferred_element_type=F32)
            dbs = dbs + jnp.dot(dcbt.astype(BF16), cs_g, preferred_element_type=F32)
            dxdt_parts.append(jnp.concatenate(dxdt_h, axis=1) + g1 * d_s_x[:, gw])
            dbs_parts.append(dbs)
            dcs_parts.append(dcs)
            stores.append((ds_scr, (slice(None), gw), dst[:, gw] * cd_x[:, gw] + dst_new))
            yield
        dxdt = jnp.concatenate(dxdt_parts, axis=1)
        y_off = jnp.concatenate(off_parts, axis=1)
        g1 = jnp.concatenate(g1_parts, axis=1)
        dcd_lane = jnp.sum(dst * st_b.astype(F32), axis=0, keepdims=True)
        vecs = jnp.concatenate([jnp.broadcast_to(dcd_lane, (SUB, D_B)), jnp.broadcast_to(dd_lane, (SUB, D_B))], axis=0)
        dds, da_off, ddt_x, vec_sums = _split_dot_many(
            [g1 * xdt, w_x * y_off, dxdt * xs, vecs], [1, 1, 1, 1], e_mat_t)
        dcd = vec_sums[0:1]
        sums.setdefault("dd", []).append(vec_sums[SUB:SUB + 1])
        yield
        t_ds = dds * d_s
        d_a = da_diag + da_off - t_ds
        last_row = jnp.sum(t_ds, axis=0, keepdims=True) + dcd * e_a[CHUNK - 1:CHUNK, :]
        d_a = d_a + jnp.where(rows == CHUNK - 1, last_row, 0.0)
        dda = _split_dot_left(_tri(False).astype(BF16), d_a, 3)
        ddt = dda * a + ddt_x
        sums.setdefault("dal", []).append(jnp.sum(dda * dt * a, axis=0, keepdims=True))
        ddtraw = jnp.where(valid_col, ddt * jax.nn.sigmoid(dtraw + dtb_ref[...]), 0.0)
        ddtraw = jnp.where(lane < N_HEADS, ddtraw, 0.0)
        sums.setdefault("ddtb", []).append(jnp.sum(ddtraw, axis=0, keepdims=True))
        stores.append((dpw_ref, (slice(None), slice(D_B, D_B + LANE)), ddtraw.astype(BF16)))
        dxs = dxs + dxdt * dt_x
        dxbc = jnp.concatenate([dxs] + dbs_parts + dcs_parts, axis=1)
        stores.append((dxbc_ref, everything, jnp.where(valid_col, dxbc, 0.0).astype(BF16)))

    def row(width, col):
        return pl.BlockSpec((nb, CHUNK, width), lambda c: (0, cpe - 1 - c, col))

    def const(shape):
        return pl.BlockSpec(shape, lambda c: (0,) * len(shape))

    proj3 = proj.reshape(nb, lp, N_INP)
    dxbc, dproj, *rest = _grid_call(
        body, name, (cpe,),
        [row(D_B, 0), row(D_B, 0), row(N_XBC, 0), row(D_B, COL_BZ // D_B), row(LANE, COL_DT // LANE),
         pl.BlockSpec((nb, None, N_STATE, D_B), lambda c: (0, cpe - 1 - c, 0, 0)),
         const((1, LANE)), const((N_HEADS, 1)), const((1, LANE)), const((N_HEADS, 1)),
         const((1, D_B)), const((1, D_B)), const((LANE, D_B)), const((D_B, LANE))],
        [row(N_XBC, 0),
         pl.BlockSpec((pl.Element(nb), pl.Element(CHUNK), pl.Element(D_B + LANE)),
                      lambda c: (0, (cpe - 1 - c) * CHUNK, COL_BZ)),
         const((1, D_B)), const((1, LANE)), const((1, LANE)), const((1, LANE))],
        [jax.ShapeDtypeStruct((nb, lp, N_XBC), BF16), jax.ShapeDtypeStruct((nb, lp, N_INP), BF16),
         jax.ShapeDtypeStruct((1, D_B), F32),
         jax.ShapeDtypeStruct((1, LANE), F32), jax.ShapeDtypeStruct((1, LANE), F32),
         jax.ShapeDtypeStruct((1, LANE), F32)],
        [pltpu.VMEM((nb, N_STATE, D_B), F32)],
        (dy.reshape(nb, lp, -1), y_ssd.reshape(nb, lp, D_B), xbc.reshape(nb, lp, N_XBC), proj3, proj3, states,
         dtb_p, dtb_t, alog_p, alog_t, d_skip_x, norm_g, e_mat, e_mat_t), comm)
    return (dxbc.reshape(t, N_XBC), dproj.reshape(t, N_INP), *rest)


HBM_SPEC = pl.BlockSpec(memory_space=pl.ANY)


def _mesh_pos():
    return lax.axis_index("x"), lax.axis_index("y"), lax.axis_index("c")


def _remote(src, dst, send_sem, recv_sem, to):
    return pltpu.make_async_remote_copy(src_ref=src, dst_ref=dst, send_sem=send_sem, recv_sem=recv_sem,
                                        device_id=to, device_id_type=MESH)


def _slot_plain(d):
    return d


def _slot_mix_rows(d):
    return jnp.where(d < 2, d + 4, jnp.where(d < 6, d - 2, d))


def _ag_ici(pieces, slots):
    n = len(pieces)

    def copies(ins, outs, sems, with_recvs):
        send_sems, recv_sems, local_sems = sems
        x, y, c = _mesh_pos()
        local, sends, recvs = [], [], []
        for a in range(n):
            mine = outs[a].at[slots[a](4 * x + 2 * y + c)]
            local.append(pltpu.make_async_copy(ins[a], mine, local_sems.at[a]))
            for k, (px, py) in enumerate([(1 - x, y), (x, 1 - y), (1 - x, 1 - y)]):
                sends.append(_remote(ins[a], mine, send_sems.at[a, k], recv_sems.at[a, k], (px, py, c)))
                if with_recvs:
                    theirs = outs[a].at[slots[a](4 * px + 2 * py + c)]
                    recvs.append(_remote(ins[a], theirs, send_sems.at[a, k], recv_sems.at[a, k], (px, py, c)))
        return local, sends, recvs

    return _Comm(pieces, [jax.ShapeDtypeStruct((N_DEV,) + p.shape, p.dtype) for p in pieces],
                 [pltpu.SemaphoreType.DMA((n, 3)), pltpu.SemaphoreType.DMA((n, 3)), pltpu.SemaphoreType.DMA((n,))], copies)


def _ag_d2d(bufs, slots):
    n = len(bufs)

    def copies(ins, outs, sems, with_recvs):
        send_sems, recv_sems = sems
        x, y, c = _mesh_pos()
        chips = [(x, y), (1 - x, y), (x, 1 - y), (1 - x, 1 - y)]
        sends, recvs = [], []
        for a in range(n):
            for k, (px, py) in enumerate(chips):
                held = slots[a](4 * px + 2 * py + c)
                sends.append(_remote(ins[a].at[held], outs[a].at[held], send_sems.at[a, k], recv_sems.at[a, k], (x, y, 1 - c)))
                if with_recvs:
                    got = slots[a](4 * px + 2 * py + 1 - c)
                    recvs.append(_remote(ins[a].at[got], outs[a].at[got], send_sems.at[a, k], recv_sems.at[a, k], (x, y, 1 - c)))
        return [], sends, recvs

    return _Comm(bufs, [jax.ShapeDtypeStruct(b.shape, b.dtype) for b in bufs],
                 [pltpu.SemaphoreType.DMA((n, 4)), pltpu.SemaphoreType.DMA((n, 4))], copies,
                 aliases={a: a for a in range(n)})


def _rs_d2d(blocks, slots):
    n = len(blocks)

    def copies(ins, outs, sems, with_recvs):
        send_sems, recv_sems = sems
        x, y, c = _mesh_pos()
        sends, recvs = [], []
        for a in range(n):
            for j in range(4):
                src = ins[a].at[slots[a](2 * j + 1 - c)]
                sends.append(_remote(src, outs[a].at[j], send_sems.at[a, j], recv_sems.at[a, j], (x, y, 1 - c)))
                if with_recvs:
                    recvs.append(_remote(src, outs[a].at[j], send_sems.at[a, j], recv_sems.at[a, j], (x, y, 1 - c)))
        return [], sends, recvs

    return _Comm(blocks, [jax.ShapeDtypeStruct((4,) + b.shape[1:], b.dtype) for b in blocks],
                 [pltpu.SemaphoreType.DMA((n, 4)), pltpu.SemaphoreType.DMA((n, 4))], copies)


def _rs_ici(blocks):
    n = len(blocks)

    def copies(ins, outs, sems, with_recvs):
        send_sems, recv_sems = sems
        x, y, c = _mesh_pos()
        sends, recvs = [], []
        for a in range(n):
            for k, (px, py) in enumerate([(1 - x, y), (x, 1 - y), (1 - x, 1 - y)]):
                src = ins[a].at[2 * px + py]
                sends.append(_remote(src, outs[a].at[k], send_sems.at[a, k], recv_sems.at[a, k], (px, py, c)))
                if with_recvs:
                    recvs.append(_remote(src, outs[a].at[k], send_sems.at[a, k], recv_sems.at[a, k], (px, py, c)))
        return [], sends, recvs

    return _Comm(blocks, [jax.ShapeDtypeStruct((3,) + b.shape[1:], b.dtype) for b in blocks],
                 [pltpu.SemaphoreType.DMA((n, 3)), pltpu.SemaphoreType.DMA((n, 3))], copies)


def _run_comm(comm, name):
    n_in, n_out = len(comm.inputs), len(comm.out_shapes)

    def body(*refs):
        ins, outs, sems = refs[:n_in], refs[n_in:n_in + n_out], refs[n_in + n_out:]
        comm.start(ins, outs, sems)
        comm.wait(ins, outs, sems)

    return pl.pallas_call(
        body, name=name, in_specs=[HBM_SPEC] * n_in, out_specs=[HBM_SPEC] * n_out, out_shape=comm.out_shapes,
        scratch_shapes=comm.sem_shapes, input_output_aliases=comm.aliases,
    )(*comm.inputs)


W_ROWS = 784


def _w_segments():
    per = N_IN // N_DEV
    out = []
    for d in range(N_DEV):
        lo, hi = per * d, per * (d + 1)
        for a, b, start in COL_MAP:
            s, e = max(lo, a), min(hi, b)
            if s < e:
                out.append((d, s - lo, e - s, s - a + start))
    return out


def _w_gather_rows(g, name):
    tk = D_MODEL // 2
    u32 = jnp.uint32

    def body(g_ref, o_ref, scr):
        for d in range(N_DEV):
            x32 = pltpu.bitcast(g_ref[d], u32)
            for dd, src, rows, dst in _w_segments():
                if dd == d:
                    scr[dst // 2:(dst + rows) // 2, :] = x32[src // 2:(src + rows) // 2]
        scr[(COL_DT + N_HEADS) // 2:COL_XBC // 2, :] = jnp.zeros(((COL_XBC - COL_DT - N_HEADS) // 2, tk), u32)
        o_ref[...] = pltpu.bitcast(scr[...], BF16)

    return pl.pallas_call(
        body, name=name, grid=(D_MODEL // tk,),
        in_specs=[pl.BlockSpec((N_DEV, W_ROWS, tk), lambda j: (0, 0, j))],
        out_specs=pl.BlockSpec((N_INP, tk), lambda j: (0, j)),
        out_shape=jax.ShapeDtypeStruct((N_INP, D_MODEL), BF16),
        scratch_shapes=[pltpu.VMEM((N_INP // 2, tk), u32)],
        compiler_params=_cparams(("parallel",)),
    )(g)


def _w_split_rows(dwt, name):
    tk = D_MODEL // 4
    per = N_IN // N_DEV

    def body(w_ref, o_ref):
        for d, dst, rows, src in _w_segments():
            o_ref[d, dst // 2:(dst + rows) // 2, :] = pltpu.bitcast(w_ref[src:src + rows, :].astype(BF16), F32)
        for d in range(N_DEV):
            o_ref[d, per // 2:W_ROWS // 2, :] = jnp.zeros(((W_ROWS - per) // 2, tk), F32)

    return pl.pallas_call(
        body, name=name, grid=(D_MODEL // tk,),
        in_specs=[pl.BlockSpec((N_INP, tk), lambda j: (0, j))],
        out_specs=pl.BlockSpec((N_DEV, W_ROWS // 2, tk), lambda j: (0, 0, j)),
        out_shape=jax.ShapeDtypeStruct((N_DEV, W_ROWS // 2, D_MODEL), F32),
        compiler_params=_cparams(("parallel",)),
    )(dwt)


def _rs_pair_sum(g, ra, own_slots, name):
    packed = g.dtype == F32
    _, rows, cols = g.shape
    out_rows = 2 * rows if packed else rows

    def body(s_ref, g_ref, ra_ref, o_ref):
        a, b = g_ref[...], ra_ref[...]
        if packed:
            a, b = pltpu.bitcast(a, BF16), pltpu.bitcast(b, BF16)
        o_ref[...] = (a.astype(F32) + b.astype(F32)).astype(BF16)

    return pl.pallas_call(
        body, name=name,
        grid_spec=pltpu.PrefetchScalarGridSpec(
            num_scalar_prefetch=1, grid=(4,),
            in_specs=[pl.BlockSpec((None, rows, cols), lambda j, s: (s[j], 0, 0)),
                      pl.BlockSpec((None, rows, cols), lambda j, s: (j, 0, 0))],
            out_specs=pl.BlockSpec((None, out_rows, cols), lambda j, s: (j, 0, 0))),
        out_shape=jax.ShapeDtypeStruct((4, out_rows, cols), BF16),
        compiler_params=_cparams(("parallel",)),
    )(own_slots, g, ra)


def _rs_final_sum(h, rb, chip_idx, name):
    _, rows, cols = h.shape

    def body(j_ref, h_ref, rb_ref, o_ref):
        o_ref[...] = ((h_ref[...].astype(F32) + rb_ref[0].astype(F32)) + rb_ref[1].astype(F32)) + rb_ref[2].astype(F32)

    return pl.pallas_call(
        body, name=name,
        grid_spec=pltpu.PrefetchScalarGridSpec(
            num_scalar_prefetch=1, grid=(1,),
            in_specs=[pl.BlockSpec((None, rows, cols), lambda i, j: (j[0], 0, 0)),
                      pl.BlockSpec((3, rows, cols), lambda i, j: (0, 0, 0))],
            out_specs=pl.BlockSpec((rows, cols), lambda i, j: (0, 0))),
        out_shape=jax.ShapeDtypeStruct((rows, cols), F32),
        compiler_params=_cparams(("arbitrary",)),
    )(chip_idx, h, rb)


def _adamw_math(w, g, m, v):
    m = ADAM_B1 * m + (1.0 - ADAM_B1) * g
    v = ADAM_B2 * v + (1.0 - ADAM_B2) * (g * g)
    m_hat = m / (1.0 - ADAM_B1 ** ADAM_STEP)
    v_hat = v / (1.0 - ADAM_B2 ** ADAM_STEP)
    delta = -ADAM_LR * (m_hat / (jnp.sqrt(v_hat) + ADAM_EPS) + ADAM_WD * w)
    return delta, m, v


def _adamw_rows(g, w, m, v, tr, name):
    layers, rows, cols = w.shape

    def body(g_ref, w_ref, m_ref, v_ref, d_out, m_out, v_out):
        delta, m_new, v_new = _adamw_math(w_ref[...], g_ref[...], m_ref[...], v_ref[...])
        d_out[...] = delta
        m_out[...] = m_new
        v_out[...] = v_new

    blk = pl.BlockSpec((None, tr, cols), lambda a, r: (a, r, 0))
    return pl.pallas_call(
        body, name=name, grid=(layers, rows // tr),
        in_specs=[blk] * 4, out_specs=[blk] * 3,
        out_shape=[jax.ShapeDtypeStruct((layers, rows, cols), F32)] * 3,
        compiler_params=_cparams(("parallel", "parallel")),
    )(g, w, m, v)


def _adamw_cols(g_t, w, m, v, name):
    layers, k, cols = w.shape
    tr = 56
    assert g_t.shape[1] % tr == 0
    views = [jnp.transpose(a, (2, 0, 1)) for a in (w, m, v)]

    def body(g_ref, w_ref, m_ref, v_ref, g_out, d_out, m_out, v_out):
        for a in range(layers):
            g = g_ref[a]
            delta, m_new, v_new = _adamw_math(w_ref[:, a, :], g, m_ref[:, a, :], v_ref[:, a, :])
            g_out[:, a, :] = g
            d_out[:, a, :] = delta
            m_out[:, a, :] = m_new
            v_out[:, a, :] = v_new

    col = pl.BlockSpec((tr, layers, k), lambda r: (r, 0, 0))
    outs = pl.pallas_call(
        body, name=name, grid=(pl.cdiv(cols, tr),),
        in_specs=[pl.BlockSpec((layers, tr, k), lambda r: (0, r, 0)), col, col, col], out_specs=[col] * 4,
        out_shape=[jax.ShapeDtypeStruct((cols, layers, k), F32)] * 4,
        compiler_params=_cparams(("parallel",)),
    )(g_t, *views)
    return [jnp.transpose(o, (1, 2, 0)) for o in outs]


def _sum_devices(parts, name):
    _, p, _ = parts.shape

    def body(x_ref, o_ref):
        acc = x_ref[0]
        for d in range(1, N_DEV):
            acc = acc + x_ref[d]
        o_ref[...] = acc

    return pl.pallas_call(
        body, name=name, grid=(1,),
        in_specs=[pl.BlockSpec((N_DEV, p, LANE), lambda i: (0, 0, 0))],
        out_specs=pl.BlockSpec((p, LANE), lambda i: (0, 0)),
        out_shape=jax.ShapeDtypeStruct((p, LANE), F32),
        compiler_params=_cparams(("arbitrary",)),
    )(parts)


def _adamw_small(g, w, m, v, name):
    p = g.shape[0]

    def body(g_ref, w_ref, m_ref, v_ref, d_out, m_out, v_out):
        delta, m_new, v_new = _adamw_math(w_ref[...], g_ref[...], m_ref[...], v_ref[...])
        d_out[...] = delta
        m_out[...] = m_new
        v_out[...] = v_new

    spec = pl.BlockSpec((p, LANE), lambda i: (0, 0))
    return pl.pallas_call(
        body, name=name, grid=(1,),
        in_specs=[spec] * 4, out_specs=[spec] * 3,
        out_shape=[jax.ShapeDtypeStruct((p, LANE), F32)] * 3,
        compiler_params=_cparams(("arbitrary",)),
    )(g, w, m, v)


PACK_ALIGN = SUB * LANE

SMALL_PARAMS = (
    ("meta", (N_META, D_MODEL), 1),
    ("pre_g", (DEPTH, D_MODEL), None),
    ("post_g", (DEPTH, D_MODEL), None),
    ("conv_a_w", (DEPTH, CONV_A_K, D_A), 2),
    ("ssm_conv_w", (DEPTH, SSM_K, N_XBC), 2),
    ("ssm_conv_b", (DEPTH, N_XBC), None),
    ("dt_bias", (DEPTH, N_HEADS), None),
    ("a_log", (DEPTH, N_HEADS), None),
    ("d_skip", (DEPTH, N_HEADS), None),
    ("ssm_norm_g", (DEPTH, D_B), None),
    ("conf_conv_w", (DEPTH, CONF_K, D_C), 2),
    ("conf_conv_b", (DEPTH, D_C), None),
    ("conf_ln_g", (DEPTH, D_C), None),
    ("conf_ln_b", (DEPTH, D_C), None),
)


def _local_shape(shape, axis):
    if axis is None:
        return shape
    return tuple(s // N_DEV if k == axis else s for k, s in enumerate(shape))


def _pack(arrays):
    flat = []
    for a in arrays:
        v = a.reshape(-1).astype(F32)
        flat.append(v)
        if v.shape[0] % PACK_ALIGN:
            flat.append(jnp.zeros(((-v.shape[0]) % PACK_ALIGN,), F32))
    return jnp.concatenate(flat).reshape(-1, LANE)


def _unpack(buf, shapes):
    flat = buf.reshape(-1)
    out, off = [], 0
    for s in shapes:
        size = 1
        for k in s:
            size *= k
        out.append(flat[off:off + size].reshape(s))
        off += size + (-size) % PACK_ALIGN
    return out


def kernel(x, meta, pre_g, post_g, w_in, w_out, conv_a_w, ssm_conv_w, ssm_conv_b, dt_bias, a_log, d_skip, ssm_norm_g, conf_conv_w, conf_conv_b, conf_ln_g, conf_ln_b, loss_target, m_meta, m_pre_g, m_post_g, m_w_in, m_w_out, m_conv_a_w, m_ssm_conv_w, m_ssm_conv_b, m_dt_bias, m_a_log, m_d_skip, m_ssm_norm_g, m_conf_conv_w, m_conf_conv_b, m_conf_ln_g, m_conf_ln_b, v_meta, v_pre_g, v_post_g, v_w_in, v_w_out, v_conv_a_w, v_ssm_conv_w, v_ssm_conv_b, v_dt_bias, v_a_log, v_d_skip, v_ssm_norm_g, v_conf_conv_w, v_conf_conv_b, v_conf_ln_g, v_conf_ln_b):
    weights = dict(meta=meta, pre_g=pre_g, post_g=post_g, conv_a_w=conv_a_w, ssm_conv_w=ssm_conv_w, ssm_conv_b=ssm_conv_b,
                   dt_bias=dt_bias, a_log=a_log, d_skip=d_skip, ssm_norm_g=ssm_norm_g, conf_conv_w=conf_conv_w,
                   conf_conv_b=conf_conv_b, conf_ln_g=conf_ln_g, conf_ln_b=conf_ln_b)
    mom1 = dict(meta=m_meta, pre_g=m_pre_g, post_g=m_post_g, conv_a_w=m_conv_a_w, ssm_conv_w=m_ssm_conv_w,
                ssm_conv_b=m_ssm_conv_b, dt_bias=m_dt_bias, a_log=m_a_log, d_skip=m_d_skip, ssm_norm_g=m_ssm_norm_g,
                conf_conv_w=m_conf_conv_w, conf_conv_b=m_conf_conv_b, conf_ln_g=m_conf_ln_g, conf_ln_b=m_conf_ln_b)
    mom2 = dict(meta=v_meta, pre_g=v_pre_g, post_g=v_post_g, conv_a_w=v_conv_a_w, ssm_conv_w=v_ssm_conv_w,
                ssm_conv_b=v_ssm_conv_b, dt_bias=v_dt_bias, a_log=v_a_log, d_skip=v_d_skip, ssm_norm_g=v_ssm_norm_g,
                conf_conv_w=v_conf_conv_w, conf_conv_b=v_conf_conv_b, conf_ln_g=v_conf_ln_g, conf_ln_b=v_conf_ln_b)
    nb, seq, d = x.shape
    lp = PAD + N_META + seq
    t = nb * lp
    assert lp % (SHORT_TILES * HALO_C) == 0 and t % (3 * LANE) == 0 and d == D_MODEL
    rt = lp // 3
    xi, yi, ci = _mesh_pos()
    dev = 4 * xi + 2 * yi + ci
    ci32 = ci.astype(jnp.int32)
    chip_idx = (2 * xi + yi).astype(jnp.int32).reshape(1)
    own_plain = jnp.stack([2 * j + ci32 for j in range(4)])
    own_mix = jnp.stack([_slot_mix_rows(2 * j + ci32) for j in range(4)]).astype(jnp.int32)
    n_in_loc = N_IN // N_DEV
    n_out_loc = 2 * D_MODEL // N_DEV
    slots = (_slot_plain, _slot_mix_rows)

    sharded_small = [n for n, _, ax in SMALL_PARAMS if ax is not None]
    small_shapes = {n: s for n, s, _ in SMALL_PARAMS}
    small_axis = {n: ax for n, _, ax in SMALL_PARAMS}
    sw_pack = _pack([weights[n] for n in sharded_small])
    wt_loc = jnp.pad(jnp.swapaxes(w_in, 1, 2).astype(BF16), ((0, 0), (0, W_ROWS - n_in_loc), (0, 0)))
    wo_loc = w_out.astype(BF16)
    plain2 = (_slot_plain, _slot_plain)
    head = _run_comm(_ag_ici([wt_loc[0], sw_pack], plain2), "ag_ici")
    g_in_t, sw_g = _run_comm(_ag_d2d(head, plain2), "ag_d2d")

    def gathered(bufs):
        return _w_gather_rows(bufs[0], "w_gather_rows"), bufs[1].reshape(2 * D_MODEL, D_MODEL)

    full = dict(weights)
    per_dev = [_unpack(sw_g[k], [_local_shape(small_shapes[n], small_axis[n]) for n in sharded_small]) for k in range(N_DEV)]
    for q, n in enumerate(sharded_small):
        full[n] = jnp.concatenate([per_dev[k][q] for k in range(N_DEV)], axis=small_axis[n])

    e_mat = (lax.broadcasted_iota(jnp.int32, (LANE, D_B), 0) == lax.broadcasted_iota(jnp.int32, (LANE, D_B), 1) // HEAD_DIM)
    e_mat = e_mat.astype(BF16)
    e_mat_t = e_mat.T

    front = jnp.concatenate([jnp.zeros((PAD, d), F32), full["meta"]], axis=0)
    h = jnp.concatenate([jnp.concatenate([front, x[b]], axis=0) for b in range(nb)], axis=0)
    saved = []
    w_t, w_o = _w_gather_rows(g_in_t, "w_gather_rows"), None
    hn = _rms_fwd(h, pre_g[0].reshape(1, -1), rt, "rms_fwd")
    for i in range(DEPTH):
        row = lambda a: a[i].reshape(1, -1)
        if i == 0:
            proj, wo_buf = _mm(hn, w_t, "nt", BF16, t // 3, MM_TN, D_MODEL, "mm_proj", cols_outer=True,
                               comm=_ag_ici([wo_loc[0]], (_slot_mix_rows,)))
        else:
            proj = _mm(hn, w_t, "nt", BF16, t // 3, MM_TN, D_MODEL, "mm_proj", cols_outer=True)
        ycat = _a_fwd(proj, full["conv_a_w"][i], lp, "a_fwd")
        xbc = _xbc_fwd(proj, full["ssm_conv_w"][i], row(ssm_conv_b), lp, "xbc_fwd")
        ycat, u1 = _c_fwd(proj, full["conf_conv_w"][i], row(conf_conv_b), row(conf_ln_g), row(conf_ln_b), ycat, lp, "c_fwd")
        d_skip_x = jnp.repeat(d_skip[i], HEAD_DIM).reshape(1, D_B)
        nxt = _ag_ici([wt_loc[i + 1], wo_loc[i + 1]], slots) if i + 1 < DEPTH else None
        if i == 0:
            nxt = _merge_comm(nxt, _ag_d2d([wo_buf], (_slot_mix_rows,)))
        ycat, yssd, states, *bufs = _ssd_fwd(xbc, proj, dt_bias[i], a_log[i], d_skip_x, row(ssm_norm_g), e_mat, ycat, lp,
                                             "ssd_fwd", comm=nxt)
        if i == 0:
            w_o = bufs.pop().reshape(2 * D_MODEL, D_MODEL)
        if bufs:
            m, *bufs = _mm(ycat, w_o, "nn", F32, t // 3, D_MODEL, 2 * D_MODEL, "mm_out", comm=_ag_d2d(bufs, slots))
        else:
            m = _mm(ycat, w_o, "nn", F32, t // 3, D_MODEL, 2 * D_MODEL, "mm_out")
        saved.append((h, hn, proj, ycat, yssd, states, xbc, m, d_skip_x, w_t, w_o, u1))
        if i + 1 < DEPTH:
            h, hn = _post_rms_fwd(h, m, row(post_g), pre_g[i + 1].reshape(1, -1), rt, "post_rms_fwd")
        else:
            h = _post_fwd(h, m, row(post_g), rt, "post_fwd")
        if bufs:
            w_t, w_o = gathered(bufs)

    dh, loss_blk = _loss_kernel(h, loss_target.reshape(nb * seq, d), lp, "loss")

    grads = {n: [None] * DEPTH for n, _, _ in SMALL_PARAMS if n != "meta"}
    gt_in, gr_out = [None] * DEPTH, [None] * DEPTH
    dmeta = None
    pending = None
    dm = None

    def rs_pair_sums(blocks, ras):
        return [_rs_pair_sum(blocks[0], ras[0], own_plain, "rs_pair_sum_in"),
                _rs_pair_sum(blocks[1], ras[1], own_mix, "rs_pair_sum_out")]

    def rs_finish(layer, hs, rbs):
        gt_in[layer] = _rs_final_sum(hs[0], rbs[0], chip_idx, "rs_final_sum_in")
        gr_out[layer] = _rs_final_sum(hs[1], rbs[1], chip_idx, "rs_final_sum_out")

    for i in reversed(range(DEPTH)):
        row = lambda a: a[i].reshape(1, -1)
        h_i, hn, proj, ycat, yssd, states, xbc, m, d_skip_x, w_t, w_o, u1 = saved[i]
        if dm is None:
            dm, grads["post_g"][i] = _post_bwd(dh, m, row(post_g), rt, "post_bwd")
        if pending is not None:
            dy, *ras = _mm(dm, w_o, "nt", BF16, t // 3, D_MODEL, D_MODEL, "mm_dy", cols_outer=True,
                           comm=_rs_d2d(list(pending), slots))
            hs = rs_pair_sums(pending, ras)
        else:
            dy = _mm(dm, w_o, "nt", BF16, t // 3, D_MODEL, D_MODEL, "mm_dy", cols_outer=True)
            hs = None
        dw_out = _mm(ycat, dm, "tn", BF16, D_MODEL, D_MODEL, t // 3, "mm_dwout")
        dxbc, dproj, grads["ssm_norm_g"][i], ddtb, dal, dds, *rbs = _ssd_bwd(
            dy, yssd, xbc, proj, states, dt_bias[i], a_log[i], d_skip_x, row(ssm_norm_g), e_mat, e_mat_t, lp, "ssd_bwd",
            comm=_rs_ici(hs) if hs is not None else None)
        if hs is not None:
            rs_finish(i + 1, hs, rbs)
        grads["dt_bias"][i] = ddtb[:, :N_HEADS]
        grads["a_log"][i] = dal[:, :N_HEADS]
        grads["d_skip"][i] = dds[:, :N_HEADS]
        dproj, grads["conv_a_w"][i] = _a_bwd(dy, proj, full["conv_a_w"][i], dproj, lp, "a_bwd")
        dproj, grads["ssm_conv_w"][i], grads["ssm_conv_b"][i] = _xbc_bwd(
            dxbc, proj, full["ssm_conv_w"][i], row(ssm_conv_b), dproj, lp, "xbc_bwd")
        dproj, grads["conf_conv_w"][i], grads["conf_conv_b"][i], grads["conf_ln_g"][i], grads["conf_ln_b"][i] = _c_bwd(
            dy, proj, u1, full["conf_conv_w"][i], row(conf_conv_b), row(conf_ln_g), row(conf_ln_b), dproj, lp, "c_bwd")
        p_out = dw_out.reshape(N_DEV, n_out_loc, D_MODEL)
        if i > 0:
            dhn = _mm(dproj, w_t, "nn", BF16, MM_TM, D_MODEL, N_INP, "mm_dhn")
            dw_in_t = _mm(dproj, hn, "tn", F32, MM_TN, D_MODEL, t, "mm_dwin")
        else:
            dhn, ra_out = _mm(dproj, w_t, "nn", BF16, MM_TM, D_MODEL, N_INP, "mm_dhn",
                              comm=_rs_d2d([p_out], (_slot_mix_rows,)))
            h_out = _rs_pair_sum(p_out, ra_out, own_mix, "rs_pair_sum_out")
            dw_in_t, rb_out = _mm(dproj, hn, "tn", F32, MM_TN, D_MODEL, t, "mm_dwin", comm=_rs_ici([h_out]))
            gr_out[0] = _rs_final_sum(h_out, rb_out, chip_idx, "rs_final_sum_out")
        if i > 0:
            dh, grads["pre_g"][i], dm, grads["post_g"][i - 1] = _rms_post_bwd(
                dh, dhn, h_i, row(pre_g), saved[i - 1][7], post_g[i - 1].reshape(1, -1), rt, "rms_post_bwd")
        else:
            dh, grads["pre_g"][i], dmeta = _rms_bwd(dh, dhn, h_i, row(pre_g), lp, rt, "rms_bwd")
        pending = (_w_split_rows(dw_in_t, "w_split_rows"), p_out)
    grad_x = dh.reshape(nb, lp, d)[:, PAD + N_META:]

    names = [n for n, _, _ in SMALL_PARAMS]
    partial = [loss_blk[0:1, 0:1], dmeta] + [jnp.concatenate(grads[n], axis=0) for n in names[1:]]
    part_pack = _pack(partial)
    p_in = pending[0]
    ra_in, part_buf = _run_comm(_merge_comm(_rs_d2d([p_in], (_slot_plain,)), _ag_ici([part_pack], (_slot_plain,))), "rs_d2d")
    h_in = _rs_pair_sum(p_in, ra_in, own_plain, "rs_pair_sum_in")
    rb_in, parts_g = _run_comm(_merge_comm(_rs_ici([h_in]), _ag_d2d([part_buf], (_slot_plain,))), "rs_ici")
    gt_in[0] = _rs_final_sum(h_in, rb_in, chip_idx, "rs_final_sum_in")

    g_w_out = jnp.stack(gr_out)
    big = {"w_in": _adamw_cols(jnp.stack(gt_in), w_in, m_w_in, v_w_in, "adamw_w_in"),
           "w_out": [g_w_out, *_adamw_rows(g_w_out, w_out, m_w_out, v_w_out, n_out_loc, "adamw_w_out")]}

    total = _unpack(_sum_devices(parts_g, "sum_small_grads"), [(1,)] + [small_shapes[n] for n in names])
    loss = total[0][0]
    g_small = {}
    for n, g in zip(names, total[1:]):
        ax = small_axis[n]
        if ax is not None:
            g = lax.dynamic_slice_in_dim(g, dev * (small_shapes[n][ax] // N_DEV), small_shapes[n][ax] // N_DEV, axis=ax)
        g_small[n] = g
    loc_shapes = [_local_shape(small_shapes[n], small_axis[n]) for n in names]
    d_pack, m_pack, v_pack = _adamw_small(_pack([g_small[n] for n in names]), _pack([weights[n] for n in names]),
                                          _pack([mom1[n] for n in names]), _pack([mom2[n] for n in names]), "adamw_small")
    d_small = dict(zip(names, _unpack(d_pack, loc_shapes)))
    m_small = dict(zip(names, _unpack(m_pack, loc_shapes)))
    v_small = dict(zip(names, _unpack(v_pack, loc_shapes)))

    order = ["meta", "pre_g", "post_g", "w_in", "w_out", "conv_a_w", "ssm_conv_w", "ssm_conv_b", "dt_bias", "a_log",
             "d_skip", "ssm_norm_g", "conf_conv_w", "conf_conv_b", "conf_ln_g", "conf_ln_b"]

    def pick(k, small):
        return [big[n][k] if n in big else small[n] for n in order]

    return (loss, grad_x, *pick(0, g_small), *pick(1, d_small), *pick(2, m_small), *pick(3, v_small))
```

```python
import jax
import jax.numpy as jnp
from jax import lax
from jax.experimental import pallas as pl
from jax.experimental.pallas import tpu as pltpu

F32 = jnp.float32
BF16 = jnp.bfloat16

D_MODEL = 1024
DEPTH = 4
SEQ = 2048
CHUNK = 64
N_META = 16
PAD = 48
LP = PAD + N_META + SEQ
D_A = 512
D_B = 1024
D_C = 512
N_HEADS = 16
HEAD_DIM = 64
N_STATE = 128
N_GROUPS = 2
GROUP_W = D_B // N_GROUPS
N_XBC = D_B + 2 * N_GROUPS * N_STATE
CONV_A_K = 3
SSM_K = 4
CONF_K = 31
NORM_EPS = 1e-6
LN_EPS = 1e-5
N_IN = 6160
N_INP = 6272
COL_BZ = 2048
COL_DT = 3072
COL_XBC = 3200
COL_C = 4736
COL_MAP = ((0, 3072, 0), (3072, 4608, COL_XBC), (4608, 4624, COL_DT), (4624, 6160, COL_C))
LANE = 128
SUB = 8

ADAM_LR = 0.001
ADAM_B1 = 0.9
ADAM_B2 = 0.999
ADAM_EPS = 1e-08
ADAM_WD = 0.01
ADAM_STEP = 10

SHORT_TILES = 6
HALO_A = 16
HALO_C = 32
MM_TM = 384
MM_TN = 896
VMEM_LIMIT = 56 * 1024 * 1024

MESH = pl.DeviceIdType.MESH
N_DEV = 8


def _silu(x):
    return x * jax.nn.sigmoid(x)


def _dsilu(x):
    s = jax.nn.sigmoid(x)
    return s * (1.0 + x * (1.0 - s))


def _cparams(sem=None):
    return pltpu.CompilerParams(dimension_semantics=sem, vmem_limit_bytes=VMEM_LIMIT)


def _mm(a, b, mode, out_dtype, tm, tn, tk, name, cols_outer=False, comm=None):
    def ix(f):
        if cols_outer:
            return lambda j, i, q: f(i, j, q)
        return f

    if mode == "nn":
        (m, k), (_, n) = a.shape, b.shape
        a_spec = pl.BlockSpec((tm, tk), ix(lambda i, j, q: (i, q)))
        b_spec = pl.BlockSpec((tk, tn), ix(lambda i, j, q: (q, j)))
        dims = (((1,), (0,)), ((), ()))
    elif mode == "nt":
        (m, k), (n, _) = a.shape, b.shape
        a_spec = pl.BlockSpec((tm, tk), ix(lambda i, j, q: (i, q)))
        b_spec = pl.BlockSpec((tn, tk), ix(lambda i, j, q: (j, q)))
        dims = (((1,), (1,)), ((), ()))
    else:
        (k, m), (_, n) = a.shape, b.shape
        a_spec = pl.BlockSpec((tk, tm), ix(lambda i, j, q: (q, i)))
        b_spec = pl.BlockSpec((tk, tn), ix(lambda i, j, q: (q, j)))
        dims = (((0,), (0,)), ((), ()))
    assert m % tm == 0 and n % tn == 0 and k % tk == 0, (name, a.shape, b.shape)
    nk = k // tk
    grid = (n // tn, m // tm, nk) if cols_outer else (m // tm, n // tn, nk)

    def body(a_ref, b_ref, o_ref, acc_ref):
        part = lax.dot_general(a_ref[...].astype(BF16), b_ref[...].astype(BF16), dims, preferred_element_type=F32)
        if nk == 1:
            o_ref[...] = part.astype(o_ref.dtype)
        else:
            q = pl.program_id(2)

            @pl.when(q == 0)
            def _():
                acc_ref[...] = part

            @pl.when(q > 0)
            def _():
                acc_ref[...] += part

            @pl.when(q == nk - 1)
            def _():
                o_ref[...] = acc_ref[...].astype(o_ref.dtype)

    res = _grid_call(
        body, name, grid, [a_spec, b_spec], [pl.BlockSpec((tm, tn), ix(lambda i, j, q: (i, j)))],
        [jax.ShapeDtypeStruct((m, n), out_dtype)], [pltpu.VMEM((tm, tn) if nk > 1 else (SUB, LANE), F32)], (a, b), comm)
    return res[0] if comm is None else res


def _row_mask(i, tpe, rows):
    r = lax.broadcasted_iota(jnp.int32, (rows, 1), 0)
    return jnp.logical_or((i % tpe) != 0, r >= PAD)


def _rms_fwd(h, g, rt, name):
    t, d = h.shape

    def body(h_ref, g_ref, o_ref):
        x = h_ref[...]
        r = lax.rsqrt(jnp.mean(x * x, axis=-1, keepdims=True) + NORM_EPS)
        o_ref[...] = (x * r * g_ref[...]).astype(BF16)

    return pl.pallas_call(
        body, name=name, grid=(t // rt,),
        in_specs=[pl.BlockSpec((rt, d), lambda i: (i, 0)), pl.BlockSpec((1, d), lambda i: (0, 0))],
        out_specs=pl.BlockSpec((rt, d), lambda i: (i, 0)),
        out_shape=jax.ShapeDtypeStruct((t, d), BF16),
        compiler_params=_cparams(("parallel",)),
    )(h, g)


def _post_fwd(h, m, g, rt, name):
    t, d = h.shape

    def body(h_ref, m_ref, g_ref, o_ref):
        x = m_ref[...]
        r = lax.rsqrt(jnp.mean(x * x, axis=-1, keepdims=True) + NORM_EPS)
        o_ref[...] = h_ref[...] + x * r * g_ref[...]

    row = pl.BlockSpec((rt, d), lambda i: (i, 0))
    return pl.pallas_call(
        body, name=name, grid=(t // rt,),
        in_specs=[row, row, pl.BlockSpec((1, d), lambda i: (0, 0))], out_specs=row,
        out_shape=jax.ShapeDtypeStruct((t, d), F32),
        compiler_params=_cparams(("parallel",)),
    )(h, m, g)


def _post_rms_fwd(h, m, g_post, g_next, rt, name):
    t, d = h.shape

    def body(h_ref, m_ref, gp_ref, gn_ref, o_ref, n_ref):
        x = m_ref[...]
        r = lax.rsqrt(jnp.mean(x * x, axis=-1, keepdims=True) + NORM_EPS)
        y = h_ref[...] + x * r * gp_ref[...]
        o_ref[...] = y
        r2 = lax.rsqrt(jnp.mean(y * y, axis=-1, keepdims=True) + NORM_EPS)
        n_ref[...] = (y * r2 * gn_ref[...]).astype(BF16)

    row = pl.BlockSpec((rt, d), lambda i: (i, 0))
    vec = pl.BlockSpec((1, d), lambda i: (0, 0))
    return pl.pallas_call(
        body, name=name, grid=(t // rt,),
        in_specs=[row, row, vec, vec], out_specs=[row, row],
        out_shape=[jax.ShapeDtypeStruct((t, d), F32), jax.ShapeDtypeStruct((t, d), BF16)],
        compiler_params=_cparams(("parallel",)),
    )(h, m, g_post, g_next)


def _rms_bwd_math(x, g, dy):
    r = lax.rsqrt(jnp.mean(x * x, axis=-1, keepdims=True) + NORM_EPS)
    gdy = dy * g
    dx = r * gdy - x * (r * r * r) * jnp.mean(gdy * x, axis=-1, keepdims=True)
    return dx, dy * x * r


def _post_bwd(dh, m, g, rt, name):
    t, d = dh.shape

    def body(dh_ref, m_ref, g_ref, dm_ref, dg_ref):
        dm, dgt = _rms_bwd_math(m_ref[...], g_ref[...], dh_ref[...])
        dm_ref[...] = dm.astype(BF16)

        @pl.when(pl.program_id(0) == 0)
        def _():
            dg_ref[...] = jnp.zeros_like(dg_ref)

        dg_ref[...] += jnp.sum(dgt, axis=0, keepdims=True)

    row = pl.BlockSpec((rt, d), lambda i: (i, 0))
    vec = pl.BlockSpec((1, d), lambda i: (0, 0))
    return pl.pallas_call(
        body, name=name, grid=(t // rt,),
        in_specs=[row, row, vec], out_specs=[row, vec],
        out_shape=[jax.ShapeDtypeStruct((t, d), BF16), jax.ShapeDtypeStruct((1, d), F32)],
        compiler_params=_cparams(("arbitrary",)),
    )(dh, m, g)


def _rms_bwd(dh_res, dhn, h, g, lp, rt, name):
    t, d = h.shape
    tpe = lp // rt

    def body(dr_ref, dn_ref, h_ref, g_ref, dh_ref, dg_ref, dmeta_ref):
        i = pl.program_id(0)
        dx, dgt = _rms_bwd_math(h_ref[...], g_ref[...], dn_ref[...].astype(F32))
        dh = dr_ref[...] + dx
        dh_ref[...] = dh

        @pl.when(i == 0)
        def _():
            dg_ref[...] = jnp.zeros_like(dg_ref)
            dmeta_ref[...] = jnp.zeros_like(dmeta_ref)

        dg_ref[...] += jnp.sum(dgt, axis=0, keepdims=True)

        @pl.when((i % tpe) == 0)
        def _():
            dmeta_ref[...] += dh[PAD:PAD + N_META, :]

    row = pl.BlockSpec((rt, d), lambda i: (i, 0))
    vec = pl.BlockSpec((1, d), lambda i: (0, 0))
    return pl.pallas_call(
        body, name=name, grid=(t // rt,),
        in_specs=[row, row, row, vec],
        out_specs=[row, vec, pl.BlockSpec((N_META, d), lambda i: (0, 0))],
        out_shape=[jax.ShapeDtypeStruct((t, d), F32), jax.ShapeDtypeStruct((1, d), F32),
                   jax.ShapeDtypeStruct((N_META, d), F32)],
        compiler_params=_cparams(("arbitrary",)),
    )(dh_res, dhn, h, g)


def _rms_post_bwd(dh_res, dhn, h, g, m_prev, g_prev, rt, name):
    t, d = h.shape

    def body(dr_ref, dn_ref, h_ref, g_ref, m_ref, gp_ref, dh_ref, dg_ref, dm_ref, dgp_ref):
        dx, dgt = _rms_bwd_math(h_ref[...], g_ref[...], dn_ref[...].astype(F32))
        dh = dr_ref[...] + dx
        dh_ref[...] = dh
        dm, dgt_prev = _rms_bwd_math(m_ref[...], gp_ref[...], dh)
        dm_ref[...] = dm.astype(BF16)

        @pl.when(pl.program_id(0) == 0)
        def _():
            dg_ref[...] = jnp.zeros_like(dg_ref)
            dgp_ref[...] = jnp.zeros_like(dgp_ref)

        dg_ref[...] += jnp.sum(dgt, axis=0, keepdims=True)
        dgp_ref[...] += jnp.sum(dgt_prev, axis=0, keepdims=True)

    row = pl.BlockSpec((rt, d), lambda i: (i, 0))
    vec = pl.BlockSpec((1, d), lambda i: (0, 0))
    return pl.pallas_call(
        body, name=name, grid=(t // rt,),
        in_specs=[row, row, row, vec, row, vec], out_specs=[row, vec, row, vec],
        out_shape=[jax.ShapeDtypeStruct((t, d), F32), jax.ShapeDtypeStruct((1, d), F32),
                   jax.ShapeDtypeStruct((t, d), BF16), jax.ShapeDtypeStruct((1, d), F32)],
        compiler_params=_cparams(("arbitrary",)),
    )(dh_res, dhn, h, g, m_prev, g_prev)


def _loss_kernel(h, target, lp, name):
    t, d = h.shape
    nb = t // lp
    seq = lp - PAD - N_META
    rows = next(r for r in (1024, 512, 256, 128, CHUNK) if seq % r == 0)
    per = seq // rows

    def body(h_ref, t_ref, zero_ref, dh_ref, loss_ref):
        @pl.when(jnp.logical_and(pl.program_id(0) == 0, pl.program_id(1) == 0))
        def _():
            loss_ref[...] = jnp.zeros_like(loss_ref)

        err = h_ref[...] - t_ref[...]
        dh_ref[...] = err * (1.0 / d)
        loss_ref[...] += (0.5 / d) * jnp.sum(err * err)

    win = pl.BlockSpec((pl.Element(rows), pl.Element(d)),
                       lambda b, j: (pl.multiple_of(b * lp + PAD + N_META + j * rows, CHUNK), 0))
    return pl.pallas_call(
        body, name=name, grid=(nb, per),
        in_specs=[win, pl.BlockSpec((rows, d), lambda b, j: (b * per + j, 0)), HBM_SPEC],
        out_specs=[win, pl.BlockSpec((SUB, LANE), lambda b, j: (0, 0))],
        out_shape=[jax.ShapeDtypeStruct((t, d), F32), jax.ShapeDtypeStruct((SUB, LANE), F32)],
        input_output_aliases={2: 0},
        compiler_params=_cparams(("arbitrary", "arbitrary")),
    )(h, target, jnp.zeros((t, d), F32))


def _window_spec(rows, width, col):
    return pl.BlockSpec((pl.Element(rows), pl.Element(width)), lambda i: (i * rows, col))


def _halo_specs(t, width, col, halo, tile):
    cur = _window_spec(tile, width, col)
    prev = pl.BlockSpec((pl.Element(halo), pl.Element(width)),
                        lambda i: (pl.multiple_of(jnp.maximum(i * tile - halo, 0), halo), col))
    nxt = pl.BlockSpec((pl.Element(halo), pl.Element(width)),
                       lambda i: (pl.multiple_of(jnp.minimum((i + 1) * tile, t - halo), halo), col))
    return cur, prev, nxt


def _f32(ref, lo, hi):
    return ref[:, lo:hi].astype(F32)


def _rows_from(x, start, rows):
    s = start % SUB
    if s == 0:
        return x[start:start + rows]
    return pltpu.roll(x, x.shape[0] - s, axis=0)[start - s:start - s + rows]


def _conv_a(ve, w):
    rows = ve.shape[0] - HALO_A
    return (w[0:1] * _rows_from(ve, HALO_A - 2, rows) + w[1:2] * _rows_from(ve, HALO_A - 1, rows)
            + w[2:3] * ve[HALO_A:HALO_A + rows])


def _a_fwd(proj, w, lp, name):
    t = proj.shape[0]
    tile = lp // SHORT_TILES
    tpe = lp // tile
    cur, prev, _ = _halo_specs(t, 4 * D_A, 0, HALO_A, tile)

    def body(p_ref, ph_ref, w_ref, y_ref):
        first = (pl.program_id(0) % tpe) == 0
        v = _f32(p_ref, D_A, 2 * D_A) * _f32(p_ref, 2 * D_A, 3 * D_A)
        vh = jnp.where(first, 0.0, _f32(ph_ref, D_A, 2 * D_A) * _f32(ph_ref, 2 * D_A, 3 * D_A))
        cv = _conv_a(jnp.concatenate([vh, v], axis=0), w_ref[...])
        y_ref[...] = (_f32(p_ref, 0, D_A) * cv * _silu(_f32(p_ref, 3 * D_A, 4 * D_A))).astype(BF16)

    return pl.pallas_call(
        body, name=name, grid=(t // tile,),
        in_specs=[cur, prev, pl.BlockSpec((CONV_A_K, D_A), lambda i: (0, 0))],
        out_specs=_window_spec(tile, D_A, D_B),
        out_shape=jax.ShapeDtypeStruct((t, D_B + D_A + D_C), BF16),
        compiler_params=_cparams(("parallel",)),
    )(proj, proj, w)


def _a_bwd(dy, proj, w, dproj, lp, name):
    t = proj.shape[0]
    tile = lp // SHORT_TILES
    tpe = lp // tile
    cur, prev, nxt = _halo_specs(t, 4 * D_A, 0, HALO_A, tile)
    dcur, _, dnxt = _halo_specs(t, D_A, D_B, HALO_A, tile)

    def body(dy_ref, dyn_ref, p_ref, ph_ref, pn_ref, w_ref, dproj_ref, dp_ref, dw_ref):
        i = pl.program_id(0)
        first = (i % tpe) == 0
        last = (i % tpe) == tpe - 1
        w = w_ref[...]
        ab, ac, ax, az = (_f32(p_ref, k * D_A, (k + 1) * D_A) for k in range(4))
        v = ac * ax
        vh = jnp.where(first, 0.0, _f32(ph_ref, D_A, 2 * D_A) * _f32(ph_ref, 2 * D_A, 3 * D_A))
        ve = jnp.concatenate([vh, v], axis=0)
        taps = [_rows_from(ve, HALO_A - 2 + k, tile) for k in range(CONV_A_K)]
        cv = w[0:1] * taps[0] + w[1:2] * taps[1] + w[2:3] * taps[2]
        s = _silu(az)
        dy_ = dy_ref[...].astype(F32)
        dcv = dy_ * ab * s
        dcvn = jnp.where(last, 0.0, dyn_ref[...].astype(F32) * _f32(pn_ref, 0, D_A) * _silu(_f32(pn_ref, 3 * D_A, 4 * D_A)))
        dce = jnp.concatenate([dcv, dcvn], axis=0)
        dv = w[2:3] * dce[0:tile] + w[1:2] * _rows_from(dce, 1, tile) + w[0:1] * _rows_from(dce, 2, tile)
        dp = jnp.concatenate([dy_ * cv * s, dv * ax, dv * ac, dy_ * ab * cv * _dsilu(az)], axis=1)
        dp_ref[...] = jnp.where(_row_mask(i, tpe, tile), dp, 0.0).astype(BF16)
        dw = jnp.concatenate(
            [jnp.sum(dcv * taps[k], axis=0, keepdims=True) for k in range(CONV_A_K)], axis=0)

        @pl.when(i == 0)
        def _():
            dw_ref[...] = jnp.zeros_like(dw_ref)

        dw_ref[...] += dw

    wspec = pl.BlockSpec((CONV_A_K, D_A), lambda i: (0, 0))
    return pl.pallas_call(
        body, name=name, grid=(t // tile,),
        in_specs=[dcur, dnxt, cur, prev, nxt, wspec, HBM_SPEC],
        out_specs=[_window_spec(tile, 4 * D_A, 0), wspec],
        out_shape=[jax.ShapeDtypeStruct(dproj.shape, dproj.dtype), jax.ShapeDtypeStruct((CONV_A_K, D_A), F32)],
        input_output_aliases={6: 0},
        compiler_params=_cparams(("arbitrary",)),
    )(dy, dy, proj, proj, proj, w, dproj)


def _conv_ssm(xe, w, rows, off):
    acc = w[0:1] * _rows_from(xe, off - 3, rows)
    for k in range(1, SSM_K):
        acc = acc + w[k:k + 1] * _rows_from(xe, off - 3 + k, rows)
    return acc


def _xbc_fwd(proj, w, b, lp, name):
    t = proj.shape[0]
    tile = lp // SHORT_TILES
    tpe = lp // tile
    cur, prev, _ = _halo_specs(t, N_XBC, COL_XBC, HALO_A, tile)

    def body(x_ref, xh_ref, w_ref, b_ref, o_ref):
        first = (pl.program_id(0) % tpe) == 0
        xh = jnp.where(first, 0.0, xh_ref[...].astype(F32))
        xe = jnp.concatenate([xh, x_ref[...].astype(F32)], axis=0)
        o_ref[...] = _silu(_conv_ssm(xe, w_ref[...], tile, HALO_A) + b_ref[...]).astype(BF16)

    return pl.pallas_call(
        body, name=name, grid=(t // tile,),
        in_specs=[cur, prev, pl.BlockSpec((SSM_K, N_XBC), lambda i: (0, 0)), pl.BlockSpec((1, N_XBC), lambda i: (0, 0))],
        out_specs=pl.BlockSpec((tile, N_XBC), lambda i: (i, 0)),
        out_shape=jax.ShapeDtypeStruct((t, N_XBC), BF16),
        compiler_params=_cparams(("parallel",)),
    )(proj, proj, w, b)


def _xbc_bwd(dxbc, proj, w, b, dproj, lp, name):
    t = proj.shape[0]
    tile = lp // SHORT_TILES
    tpe = lp // tile
    cur, prev, nxt = _halo_specs(t, N_XBC, COL_XBC, HALO_A, tile)
    dcur, _, dnxt = _halo_specs(t, N_XBC, 0, HALO_A, tile)

    def body(d_ref, dn_ref, x_ref, xh_ref, xn_ref, w_ref, b_ref, dproj_ref, dx_ref, dw_ref, db_ref):
        i = pl.program_id(0)
        first = (i % tpe) == 0
        last = (i % tpe) == tpe - 1
        w = w_ref[...]
        xh = jnp.where(first, 0.0, xh_ref[...].astype(F32))
        xe = jnp.concatenate([xh, x_ref[...].astype(F32), xn_ref[...].astype(F32)], axis=0)
        taps = [_rows_from(xe, HALO_A - 3 + k, tile + HALO_A) for k in range(SSM_K)]
        pre = b_ref[...] + w[0:1] * taps[0]
        for k in range(1, SSM_K):
            pre = pre + w[k:k + 1] * taps[k]
        de = jnp.concatenate([d_ref[...].astype(F32), jnp.where(last, 0.0, dn_ref[...].astype(F32))], axis=0)
        dpre = de * _dsilu(pre)
        dx = w[3:4] * dpre[0:tile]
        for k in range(SSM_K - 1):
            dx = dx + w[k:k + 1] * _rows_from(dpre, 3 - k, tile)
        dx_ref[...] = jnp.where(_row_mask(i, tpe, tile), dx, 0.0).astype(BF16)
        dpc = dpre[0:tile]
        dw = jnp.concatenate(
            [jnp.sum(dpc * taps[k][0:tile], axis=0, keepdims=True) for k in range(SSM_K)], axis=0)

        @pl.when(i == 0)
        def _():
            dw_ref[...] = jnp.zeros_like(dw_ref)
            db_ref[...] = jnp.zeros_like(db_ref)

        dw_ref[...] += dw
        db_ref[...] += jnp.sum(dpc, axis=0, keepdims=True)

    wspec = pl.BlockSpec((SSM_K, N_XBC), lambda i: (0, 0))
    bspec = pl.BlockSpec((1, N_XBC), lambda i: (0, 0))
    return pl.pallas_call(
        body, name=name, grid=(t // tile,),
        in_specs=[dcur, dnxt, cur, prev, nxt, wspec, bspec, HBM_SPEC],
        out_specs=[_window_spec(tile, N_XBC, COL_XBC), wspec, bspec],
        out_shape=[jax.ShapeDtypeStruct(dproj.shape, dproj.dtype), jax.ShapeDtypeStruct((SSM_K, N_XBC), F32),
                   jax.ShapeDtypeStruct((1, N_XBC), F32)],
        input_output_aliases={7: 0},
        compiler_params=_cparams(("arbitrary",)),
    )(dxbc, dxbc, proj, proj, proj, w, b, dproj)


SUBROWS = 32


def _fill_shifted(scr, x):
    scr[0] = x
    for s in range(1, SUB):
        scr[s] = pltpu.roll(x, x.shape[0] - s, axis=0)


def _window(scr, start, rows):
    s = start % SUB
    return scr[s, start - s:start - s + rows, :]


def _conv_conf(scr, w, rows, off, base):
    acc = w[0:1] * _window(scr, base + off - (CONF_K - 1), rows)
    for k in range(1, CONF_K):
        acc = acc + w[k:k + 1] * _window(scr, base + off - (CONF_K - 1) + k, rows)
    return acc


def _ln_fwd(u1, g, b):
    mu = jnp.mean(u1, axis=-1, keepdims=True)
    xc = u1 - mu
    rstd = lax.rsqrt(jnp.mean(xc * xc, axis=-1, keepdims=True) + LN_EPS)
    n = xc * rstd
    return n, rstd, n * g + b


def _c_fwd(proj, w, cb, g, b, ybuf, lp, name):
    t = proj.shape[0]
    tile = lp // SHORT_TILES
    tpe = lp // tile
    cur, prev, _ = _halo_specs(t, 3 * D_C, COL_C, HALO_C, tile)

    def body(p_ref, ph_ref, w_ref, cb_ref, g_ref, b_ref, ybuf_ref, y_ref, u1_ref, u0_scr):
        first = (pl.program_id(0) % tpe) == 0
        u0h = jnp.where(first, 0.0, _f32(ph_ref, 0, D_C) * jax.nn.sigmoid(_f32(ph_ref, D_C, 2 * D_C)))
        _fill_shifted(u0_scr, jnp.concatenate([u0h, _f32(p_ref, 0, D_C) * jax.nn.sigmoid(_f32(p_ref, D_C, 2 * D_C))], axis=0))
        w = w_ref[...]
        for r0 in range(0, tile, SUBROWS):
            u1 = _conv_conf(u0_scr, w, SUBROWS, HALO_C, r0) + cb_ref[...]
            u1_ref[r0:r0 + SUBROWS, :] = u1.astype(BF16)
            _, _, u2 = _ln_fwd(u1, g_ref[...], b_ref[...])
            cz = p_ref[r0:r0 + SUBROWS, 2 * D_C:3 * D_C].astype(F32)
            y_ref[r0:r0 + SUBROWS, :] = (_silu(u2) * _silu(cz)).astype(BF16)

    vec = pl.BlockSpec((1, D_C), lambda i: (0, 0))
    return pl.pallas_call(
        body, name=name, grid=(t // tile,),
        in_specs=[cur, prev, pl.BlockSpec((CONF_K, D_C), lambda i: (0, 0)), vec, vec, vec, HBM_SPEC],
        out_specs=[_window_spec(tile, D_C, D_B + D_A), pl.BlockSpec((tile, D_C), lambda i: (i, 0))],
        out_shape=[jax.ShapeDtypeStruct(ybuf.shape, ybuf.dtype), jax.ShapeDtypeStruct((t, D_C), BF16)],
        scratch_shapes=[pltpu.VMEM((SUB, HALO_C + tile, D_C), F32)],
        input_output_aliases={6: 0},
        compiler_params=_cparams(("parallel",)),
    )(proj, proj, w, cb, g, b, ybuf)


def _c_bwd(dy, proj, u1, w, cb, g, b, dproj, lp, name):
    t = proj.shape[0]
    tile = lp // SHORT_TILES
    tpe = lp // tile
    cur, prev, nxt = _halo_specs(t, 3 * D_C, COL_C, HALO_C, tile)
    dcur, _, dnxt = _halo_specs(t, D_C, D_B + D_A, HALO_C, tile)
    ucur, _, unxt = _halo_specs(t, D_C, 0, HALO_C, tile)
    ext = tile + HALO_C

    def body(dy_ref, dyn_ref, p_ref, ph_ref, pn_ref, u1_ref, u1n_ref, w_ref, cb_ref, g_ref, b_ref, dproj_ref,
             dp_ref, dw_ref, dcb_ref, dg_ref, db_ref, u0_scr, du1_scr, wacc_scr):
        i = pl.program_id(0)
        first = (i % tpe) == 0
        last = (i % tpe) == tpe - 1
        w = w_ref[...]

        @pl.when(i == 0)
        def _():
            dw_ref[...] = jnp.zeros_like(dw_ref)
            dcb_ref[...] = jnp.zeros_like(dcb_ref)
            dg_ref[...] = jnp.zeros_like(dg_ref)
            db_ref[...] = jnp.zeros_like(db_ref)

        u0h = jnp.where(first, 0.0, _f32(ph_ref, 0, D_C) * jax.nn.sigmoid(_f32(ph_ref, D_C, 2 * D_C)))
        _fill_shifted(u0_scr, jnp.concatenate(
            [u0h, _f32(p_ref, 0, D_C) * jax.nn.sigmoid(_f32(p_ref, D_C, 2 * D_C))], axis=0))
        dcb = jnp.zeros((1, D_C), F32)
        dg = jnp.zeros((1, D_C), F32)
        db = jnp.zeros((1, D_C), F32)
        for r0 in range(0, ext, SUBROWS):
            in_tile = r0 < tile
            src, dsrc, usrc, q0 = (p_ref, dy_ref, u1_ref, r0) if in_tile else (pn_ref, dyn_ref, u1n_ref, r0 - tile)
            u1 = usrc[q0:q0 + SUBROWS, :].astype(F32)
            n, rstd, u2 = _ln_fwd(u1, g_ref[...], b_ref[...])
            cz = src[q0:q0 + SUBROWS, 2 * D_C:3 * D_C].astype(F32)
            dyc = dsrc[q0:q0 + SUBROWS, :].astype(F32)
            if not in_tile:
                dyc = jnp.where(last, 0.0, dyc)
            du2 = dyc * _silu(cz) * _dsilu(u2)
            dn = du2 * g_ref[...]
            du1 = rstd * (dn - jnp.mean(dn, axis=-1, keepdims=True) - n * jnp.mean(dn * n, axis=-1, keepdims=True))
            du1_scr[0, r0:r0 + SUBROWS, :] = du1
            if in_tile:
                dp_ref[r0:r0 + SUBROWS, 2 * D_C:3 * D_C] = (dyc * _silu(u2) * _dsilu(cz)).astype(BF16)
                dg = dg + jnp.sum(du2 * n, axis=0, keepdims=True)
                db = db + jnp.sum(du2, axis=0, keepdims=True)
                dcb = dcb + jnp.sum(du1, axis=0, keepdims=True)
        dcb_ref[...] += dcb
        dg_ref[...] += dg
        db_ref[...] += db
        mask = _row_mask(i, tpe, tile)
        _fill_shifted(du1_scr, du1_scr[0])
        for r0 in range(0, tile, SUBROWS):
            acc = w[0:1] * _window(du1_scr, r0 + CONF_K - 1, SUBROWS)
            for k in range(1, CONF_K):
                acc = acc + w[k:k + 1] * _window(du1_scr, r0 + CONF_K - 1 - k, SUBROWS)
            ca = p_ref[r0:r0 + SUBROWS, 0:D_C].astype(F32)
            sg = jax.nn.sigmoid(p_ref[r0:r0 + SUBROWS, D_C:2 * D_C].astype(F32))
            m = mask[r0:r0 + SUBROWS]
            dp_ref[r0:r0 + SUBROWS, 0:D_C] = jnp.where(m, acc * sg, 0.0).astype(BF16)
            dp_ref[r0:r0 + SUBROWS, D_C:2 * D_C] = jnp.where(m, acc * ca * sg * (1.0 - sg), 0.0).astype(BF16)
        for k in range(CONF_K):
            part = jnp.zeros((SUB, D_C), F32)
            for r0 in range(0, tile, SUBROWS):
                prod = du1_scr[0, r0:r0 + SUBROWS, :] * _window(u0_scr, HALO_C + r0 - (CONF_K - 1) + k, SUBROWS)
                for q in range(0, SUBROWS, SUB):
                    part = part + prod[q:q + SUB]
            wacc_scr[k:k + 1, :] = jnp.sum(part, axis=0, keepdims=True)
        dw_ref[...] += wacc_scr[0:CONF_K, :]

    vec = pl.BlockSpec((1, D_C), lambda i: (0, 0))
    wspec = pl.BlockSpec((CONF_K, D_C), lambda i: (0, 0))
    return pl.pallas_call(
        body, name=name, grid=(t // tile,),
        in_specs=[dcur, dnxt, cur, prev, nxt, ucur, unxt, wspec, vec, vec, vec, HBM_SPEC],
        out_specs=[_window_spec(tile, 3 * D_C, COL_C), wspec, vec, vec, vec],
        out_shape=[jax.ShapeDtypeStruct(dproj.shape, dproj.dtype), jax.ShapeDtypeStruct((CONF_K, D_C), F32),
                   jax.ShapeDtypeStruct((1, D_C), F32), jax.ShapeDtypeStruct((1, D_C), F32),
                   jax.ShapeDtypeStruct((1, D_C), F32)],
        scratch_shapes=[pltpu.VMEM((SUB, ext, D_C), F32), pltpu.VMEM((SUB, ext, D_C), F32),
                        pltpu.VMEM((HALO_C, D_C), F32)],
        input_output_aliases={11: 0},
        compiler_params=_cparams(("arbitrary",)),
    )(dy, dy, proj, proj, proj, u1, u1, w, cb, g, b, dproj)


def _split_dot(x, m_bf16, terms):
    acc = None
    rem = x
    for _ in range(terms):
        hi = rem.astype(BF16)
        part = jnp.dot(hi, m_bf16, preferred_element_type=F32)
        acc = part if acc is None else acc + part
        rem = rem - hi.astype(F32)
    return acc


def _split_pieces(x, terms):
    out, rem = [], x
    for _ in range(terms):
        hi = rem.astype(BF16)
        out.append(hi)
        rem = rem - hi.astype(F32)
    return out


def _split_dot_many(xs, terms, m_bf16):
    pieces = [p for x, n in zip(xs, terms) for p in _split_pieces(x, n)]
    prod = jnp.dot(jnp.concatenate(pieces, axis=0), m_bf16, preferred_element_type=F32)
    out, off = [], 0
    for x, n in zip(xs, terms):
        rows = x.shape[0]
        acc = prod[off:off + rows]
        for q in range(1, n):
            acc = acc + prod[off + q * rows:off + (q + 1) * rows]
        out.append(acc)
        off += n * rows
    return out


def _split_dot_left(m_bf16, x, terms):
    cols = x.shape[1]
    prod = jnp.dot(m_bf16, jnp.concatenate(_split_pieces(x, terms), axis=1), preferred_element_type=F32)
    acc = prod[:, 0:cols]
    for q in range(1, terms):
        acc = acc + prod[:, q * cols:(q + 1) * cols]
    return acc


def _tri(rows_ge_cols):
    r = lax.broadcasted_iota(jnp.int32, (CHUNK, CHUNK), 0)
    c = lax.broadcasted_iota(jnp.int32, (CHUNK, CHUNK), 1)
    return (r >= c) if rows_ge_cols else (r <= c)


def _softplus(x):
    return jnp.maximum(x, 0.0) + jnp.log(1.0 + jnp.exp(-jnp.abs(x)))


def _ssd_common(dtraw, dtb, dtb_t, a_log, a_log_t, e_mat, valid_col, valid_row, x_terms=2):
    a = -jnp.exp(a_log)
    lane = lax.broadcasted_iota(jnp.int32, (1, LANE), 1)
    a = jnp.where(lane < N_HEADS, a, 0.0)
    a_t = -jnp.exp(a_log_t)
    dt = jnp.where(valid_col, _softplus(dtraw + dtb), 0.0)
    dt = jnp.where(lane < N_HEADS, dt, 0.0)
    dt_t = jnp.where(valid_row, _softplus(dtraw.T[0:N_HEADS, :] + dtb_t), 0.0)
    ltri = _tri(True).astype(BF16)
    utri = _tri(False).astype(BF16)
    big_a = _split_dot_left(ltri, dt * a, 3)
    big_a_t = _split_dot_many([dt_t * a_t], [3], utri)[0]
    e_a = jnp.exp(big_a)
    d_s = jnp.exp(big_a[CHUNK - 1:CHUNK, :] - big_a)
    if x_terms == 1:
        dt_x, e_a_x, d_s_x = _split_dot_many([dt, e_a, d_s], [1, 2, 1], e_mat)
    else:
        dt_x, e_a_x, d_s_x = (_split_dot(q, e_mat, x_terms) for q in (dt, e_a, d_s))
    cd_x = e_a_x[CHUNK - 1:CHUNK, :]
    return a, dt, big_a, big_a_t, e_a, d_s, dt_x, e_a_x, d_s_x, cd_x


def _decay(big_a, big_a_t, h, transposed):
    col = big_a[:, h:h + 1]
    row = big_a_t[h:h + 1, :]
    if not transposed:
        seg = col - row
        return jnp.where(_tri(True), jnp.exp(jnp.minimum(seg, 0.0)), 0.0)
    seg = row - col
    return jnp.where(_tri(False), jnp.exp(jnp.minimum(seg, 0.0)), 0.0)


NT_DIMS = (((1,), (1,)), ((), ()))
TN_DIMS = (((0,), (0,)), ((), ()))
HBM_SPEC = pl.BlockSpec(memory_space=pl.ANY)


class _Comm:
    def __init__(self, inputs, out_shapes, sem_shapes, copies, aliases=None):
        self.inputs, self.out_shapes, self.sem_shapes, self.copies = inputs, out_shapes, sem_shapes, copies
        self.aliases = aliases or {}

    def start(self, ins, outs, sems):
        local, sends, _ = self.copies(ins, outs, sems, False)
        for cp in local + sends:
            cp.start()

    def wait(self, ins, outs, sems):
        local, sends, recvs = self.copies(ins, outs, sems, True)
        for cp in recvs:
            cp.wait_recv()
        for cp in sends:
            cp.wait_send()
        for cp in local:
            cp.wait()


def _merge_comm(*comms):
    comms = [c for c in comms if c is not None]
    if len(comms) <= 1:
        return comms[0] if comms else None

    def copies(ins, outs, sems, with_recvs):
        local, sends, recvs = [], [], []
        i0 = o0 = s0 = 0
        for c in comms:
            ni, no, ns = len(c.inputs), len(c.out_shapes), len(c.sem_shapes)
            loc, snd, rcv = c.copies(ins[i0:i0 + ni], outs[o0:o0 + no], sems[s0:s0 + ns], with_recvs)
            local, sends, recvs = local + loc, sends + snd, recvs + rcv
            i0, o0, s0 = i0 + ni, o0 + no, s0 + ns
        return local, sends, recvs

    aliases, i0, o0 = {}, 0, 0
    for c in comms:
        aliases.update({i0 + k: o0 + v for k, v in c.aliases.items()})
        i0, o0 = i0 + len(c.inputs), o0 + len(c.out_shapes)
    return _Comm([a for c in comms for a in c.inputs], [s for c in comms for s in c.out_shapes],
                 [s for c in comms for s in c.sem_shapes], copies, aliases)


def _grid_call(body, name, grid, in_specs, out_specs, out_shape, scratch_shapes, operands, comm=None, aliases=None):
    aliases = dict(aliases or {})
    if comm is None:
        return pl.pallas_call(
            body, name=name, grid=grid, in_specs=in_specs, out_specs=out_specs, out_shape=out_shape,
            scratch_shapes=scratch_shapes, input_output_aliases=aliases,
            compiler_params=_cparams(("arbitrary",) * len(grid)))(*operands)
    n_in, n_out, n_scr = len(in_specs), len(out_specs), len(scratch_shapes)
    nci, nco = len(comm.inputs), len(comm.out_shapes)

    def wrapped(*refs):
        ins, cins = refs[:n_in], refs[n_in:n_in + nci]
        o0 = n_in + nci
        outs, couts = refs[o0:o0 + n_out], refs[o0 + n_out:o0 + n_out + nco]
        s0 = o0 + n_out + nco
        scr, csems = refs[s0:s0 + n_scr], refs[s0 + n_scr:]
        first = pl.program_id(0) == 0
        last = pl.program_id(0) == grid[0] - 1
        for k in range(1, len(grid)):
            first = jnp.logical_and(first, pl.program_id(k) == 0)
            last = jnp.logical_and(last, pl.program_id(k) == grid[k] - 1)

        @pl.when(first)
        def _():
            comm.start(cins, couts, csems)

        body(*ins, *outs, *scr)

        @pl.when(last)
        def _():
            comm.wait(cins, couts, csems)

    res = pl.pallas_call(
        wrapped, name=name, grid=grid,
        in_specs=list(in_specs) + [HBM_SPEC] * nci, out_specs=list(out_specs) + [HBM_SPEC] * nco,
        out_shape=list(out_shape) + list(comm.out_shapes),
        scratch_shapes=list(scratch_shapes) + list(comm.sem_shapes),
        input_output_aliases={**aliases, **{n_in + k: n_out + v for k, v in comm.aliases.items()}},
        compiler_params=_cparams(("arbitrary",) * len(grid)))(*operands, *comm.inputs)
    return res


def _ssd_fwd(xbc, proj, dtb, a_log, d_skip_x, norm_g, e_mat, ybuf, lp, name, comm=None):
    t = xbc.shape[0]
    cpe = lp // CHUNK
    nb = t // lp
    dtb_p = jnp.pad(dtb.reshape(1, N_HEADS), ((0, 0), (0, LANE - N_HEADS)))
    alog_p = jnp.pad(a_log.reshape(1, N_HEADS), ((0, 0), (0, LANE - N_HEADS)))
    dtb_t = dtb.reshape(N_HEADS, 1)
    alog_t = a_log.reshape(N_HEADS, 1)

    def body(xbc_ref, bz_ref, dt_ref, dtb_ref, dtbt_ref, al_ref, alt_ref, dx_ref, g_ref, e_ref, ybuf_ref,
             yb_ref, ys_ref, st_ref, s_scr):
        c = pl.program_id(0)

        @pl.when(c == 0)
        def _():
            s_scr[...] = jnp.zeros_like(s_scr)

        rows = lax.broadcasted_iota(jnp.int32, (CHUNK, 1), 0)
        cols = lax.broadcasted_iota(jnp.int32, (1, CHUNK), 1)
        valid_col = jnp.logical_or(c > 0, rows >= PAD)
        valid_row = jnp.logical_or(c > 0, cols >= PAD)
        e_mat = e_ref[...]
        ex = []
        for b in range(nb):
            _, _, big_a, big_a_t, _, _, dt_x, e_a_x, d_s_x, cd_x = _ssd_common(
                dt_ref[b].astype(F32), dtb_ref[...], dtbt_ref[...], al_ref[...], alt_ref[...], e_mat, valid_col, valid_row)
            xs = xbc_ref[b, :, 0:D_B].astype(F32)
            xdt = xs * dt_x
            st_prev = s_scr[b]
            ex.append(dict(big_a=big_a, big_a_t=big_a_t, e_a_x=e_a_x, cd_x=cd_x, xs=xs,
                           bs=xbc_ref[b, :, D_B:D_B + N_GROUPS * N_STATE], cs=xbc_ref[b, :, D_B + N_GROUPS * N_STATE:N_XBC],
                           xdt_b=xdt.astype(BF16), st_prev=st_prev, st_b=st_prev.astype(BF16),
                           u_b=(xdt * d_s_x).astype(BF16), bz=bz_ref[b].astype(F32), y_parts=[], new_st=[]))
        for g in range(N_GROUPS):
            gs = slice(g * N_STATE, (g + 1) * N_STATE)
            gw = slice(g * GROUP_W, (g + 1) * GROUP_W)
            for q in ex:
                cb = lax.dot_general(q["cs"][:, gs], q["bs"][:, gs], NT_DIMS, preferred_element_type=F32)
                q["y_off"] = jnp.dot(q["cs"][:, gs], q["st_b"][:, gw], preferred_element_type=F32)
                q["cb"] = cb
                q["diag"] = []
            for e in range(N_HEADS // N_GROUPS):
                h = g * (N_HEADS // N_GROUPS) + e
                for q in ex:
                    m = (q["cb"] * _decay(q["big_a"], q["big_a_t"], h, False)).astype(BF16)
                    q["diag"].append(jnp.dot(m, q["xdt_b"][:, h * HEAD_DIM:(h + 1) * HEAD_DIM], preferred_element_type=F32))
            for q in ex:
                q["y_parts"].append(jnp.concatenate(q["diag"], axis=1) + q["y_off"] * q["e_a_x"][:, gw])
                upd = lax.dot_general(q["bs"][:, gs], q["u_b"][:, gw], TN_DIMS, preferred_element_type=F32)
                q["new_st"].append(q["st_prev"][:, gw] * q["cd_x"][:, gw] + upd)
        for b, q in enumerate(ex):
            y = jnp.concatenate(q["y_parts"], axis=1) + q["xs"] * dx_ref[...]
            z = y * _silu(q["bz"])
            r = lax.rsqrt(jnp.mean(z * z, axis=-1, keepdims=True) + NORM_EPS)
            q["y"], q["yb"] = y, (z * r * g_ref[...]).astype(BF16)
        for b, q in enumerate(ex):
            st_ref[b] = q["st_b"]
            s_scr[b] = jnp.concatenate(q["new_st"], axis=1)
            ys_ref[b] = q["y"].astype(BF16)
            yb_ref[b] = q["yb"]

    def row(width, col):
        return pl.BlockSpec((nb, CHUNK, width), lambda c: (0, c, col))

    def const(shape):
        return pl.BlockSpec(shape, lambda c: (0,) * len(shape))

    proj3 = proj.reshape(nb, lp, N_INP)
    ybuf, yssd, states, *rest = _grid_call(
        body, name, (cpe,),
        [row(N_XBC, 0), row(D_B, COL_BZ // D_B), row(LANE, COL_DT // LANE),
         const((1, LANE)), const((N_HEADS, 1)), const((1, LANE)), const((N_HEADS, 1)),
         const((1, D_B)), const((1, D_B)), const((LANE, D_B)), HBM_SPEC],
        [row(D_B, 0), row(D_B, 0), pl.BlockSpec((nb, None, N_STATE, D_B), lambda c: (0, c, 0, 0))],
        [jax.ShapeDtypeStruct((nb, lp, ybuf.shape[1]), ybuf.dtype), jax.ShapeDtypeStruct((nb, lp, D_B), BF16),
         jax.ShapeDtypeStruct((nb, cpe, N_STATE, D_B), BF16)],
        [pltpu.VMEM((nb, N_STATE, D_B), F32)],
        (xbc.reshape(nb, lp, N_XBC), proj3, proj3, dtb_p, dtb_t, alog_p, alog_t, d_skip_x, norm_g, e_mat,
         ybuf.reshape(nb, lp, ybuf.shape[1])), comm, aliases={10: 0})
    return (ybuf.reshape(t, -1), yssd.reshape(t, D_B), states, *rest)


def _ssd_bwd(dy, y_ssd, xbc, proj, states, dtb, a_log, d_skip_x, norm_g, e_mat, e_mat_t, lp, name, comm=None):
    t = xbc.shape[0]
    cpe = lp // CHUNK
    nb = t // lp
    hpg = N_HEADS // N_GROUPS
    dtb_p = jnp.pad(dtb.reshape(1, N_HEADS), ((0, 0), (0, LANE - N_HEADS)))
    alog_p = jnp.pad(a_log.reshape(1, N_HEADS), ((0, 0), (0, LANE - N_HEADS)))
    dtb_t = dtb.reshape(N_HEADS, 1)
    alog_t = a_log.reshape(N_HEADS, 1)

    def body(dy_ref, ys_ref, xbc_ref, bz_ref, dt_ref, st_ref, dtb_ref, dtbt_ref, al_ref, alt_ref, dx_ref, g_ref,
             e_ref, et_ref, dxbc_ref, dpw_ref, dg_ref, ddtb_ref, dal_ref, dd_ref, ds_scr):
        @pl.when(pl.program_id(0) == 0)
        def _():
            ds_scr[...] = jnp.zeros_like(ds_scr)
            dg_ref[...] = jnp.zeros_like(dg_ref)
            ddtb_ref[...] = jnp.zeros_like(ddtb_ref)
            dal_ref[...] = jnp.zeros_like(dal_ref)
            dd_ref[...] = jnp.zeros_like(dd_ref)

        stores, sums = [], {}
        chains = [one_example(dy_ref.at[b], ys_ref.at[b], xbc_ref.at[b], bz_ref.at[b], dt_ref.at[b], st_ref.at[b], dtb_ref,
                              dtbt_ref, al_ref, alt_ref, dx_ref, g_ref, e_ref, et_ref, dxbc_ref.at[b], dpw_ref.at[b],
                              ds_scr.at[b], stores, sums) for b in range(nb)]
        live = list(chains)
        while live:
            live = [ch for ch in live if next(ch, "done") != "done"]
        for ref, idx, val in stores:
            ref[idx] = val
        for ref, name_ in ((dg_ref, "dg"), (ddtb_ref, "ddtb"), (dal_ref, "dal"), (dd_ref, "dd")):
            total = sums[name_][0]
            for part in sums[name_][1:]:
                total = total + part
            ref[...] += total

    def one_example(dy_ref, ys_ref, xbc_ref, bz_ref, dt_ref, st_ref, dtb_ref, dtbt_ref, al_ref, alt_ref, dx_ref, g_ref,
                    e_ref, et_ref, dxbc_ref, dpw_ref, ds_scr, stores, sums):
        everything = (slice(None), slice(None))
        cc = cpe - 1 - pl.program_id(0)
        rows = lax.broadcasted_iota(jnp.int32, (CHUNK, 1), 0)
        cols = lax.broadcasted_iota(jnp.int32, (1, CHUNK), 1)
        valid_col = jnp.logical_or(cc > 0, rows >= PAD)
        valid_row = jnp.logical_or(cc > 0, cols >= PAD)
        e_mat = e_ref[...]
        e_mat_t = et_ref[...]
        dtraw = dt_ref[...].astype(F32)
        a, dt, big_a, big_a_t, e_a, d_s, dt_x, e_a_x, d_s_x, cd_x = _ssd_common(
            dtraw, dtb_ref[...], dtbt_ref[...], al_ref[...], alt_ref[...], e_mat, valid_col, valid_row, x_terms=1)
        xs = xbc_ref[:, 0:D_B].astype(F32)
        bs = xbc_ref[:, D_B:D_B + N_GROUPS * N_STATE]
        cs = xbc_ref[:, D_B + N_GROUPS * N_STATE:N_XBC]
        xdt = xs * dt_x
        xdt_b = xdt.astype(BF16)
        st_b = st_ref[...]
        dst = ds_scr[...]
        dst_b = dst.astype(BF16)

        ys = ys_ref[...].astype(F32)
        bz = bz_ref[...].astype(F32)
        sil = _silu(bz)
        z = ys * sil
        dz, dgt = _rms_bwd_math(z, g_ref[...], dy_ref[...].astype(F32))
        sums.setdefault("dg", []).append(jnp.sum(dgt, axis=0, keepdims=True))
        stores.append((dpw_ref, (slice(None), slice(0, D_B)), (dz * ys * _dsilu(bz)).astype(BF16)))
        dys = dz * sil

        dd_lane = jnp.sum(dys * xs, axis=0, keepdims=True)
        dxs = dys * dx_ref[...]
        w_x = dys * e_a_x
        w_b = w_x.astype(BF16)
        dys_b = dys.astype(BF16)
        u_b = (xdt * d_s_x).astype(BF16)
        dxdt_parts, dbs_parts, dcs_parts, off_parts, g1_parts = [], [], [], [], []
        da_diag = jnp.zeros((CHUNK, LANE), F32)
        lane = lax.broadcasted_iota(jnp.int32, (1, LANE), 1)
        for g in range(N_GROUPS):
            gs = slice(g * N_STATE, (g + 1) * N_STATE)
            gw = slice(g * GROUP_W, (g + 1) * GROUP_W)
            cs_g, bs_g = cs[:, gs], bs[:, gs]
            dcs = lax.dot_general(w_b[:, gw], st_b[:, gw], NT_DIMS, preferred_element_type=F32)
            y_off = jnp.dot(cs_g, st_b[:, gw], preferred_element_type=F32)
            off_parts.append(y_off)
            dst_new = lax.dot_general(cs_g, w_b[:, gw], TN_DIMS, preferred_element_type=F32)
            g1 = jnp.dot(bs_g, dst_b[:, gw], preferred_element_type=F32)
            g1_parts.append(g1)
            dbs = lax.dot_general(u_b[:, gw], dst_b[:, gw], NT_DIMS, preferred_element_type=F32)
            cb = lax.dot_general(cs_g, bs_g, NT_DIMS, preferred_element_type=F32)
            cbt = lax.dot_general(bs_g, cs_g, NT_DIMS, preferred_element_type=F32)
            dcb = jnp.zeros((CHUNK, CHUNK), F32)
            dcbt = jnp.zeros((CHUNK, CHUNK), F32)
            dxdt_h = []
            for e in range(hpg):
                h = g * hpg + e
                hs = slice(h * HEAD_DIM, (h + 1) * HEAD_DIM)
                dec = _decay(big_a, big_a_t, h, False)
                dect = _decay(big_a, big_a_t, h, True)
                m = cb * dec
                mt = cbt * dect
                dxdt_h.append(jnp.dot(mt.astype(BF16), dys_b[:, hs], preferred_element_type=F32))
                dm = lax.dot_general(dys_b[:, hs], xdt_b[:, hs], NT_DIMS, preferred_element_type=F32)
                dmt = lax.dot_general(xdt_b[:, hs], dys_b[:, hs], NT_DIMS, preferred_element_type=F32)
                dcb = dcb + dm * dec
                dcbt = dcbt + dmt * dect
                da_h = jnp.sum(dm * m - dmt * mt, axis=1, keepdims=True)
                da_diag = da_diag + jnp.where(lane == h, da_h, 0.0)
                yield
            dcs = dcs + jnp.dot(dcb.astype(BF16), bs_g, preferred_element_type=F32)
            dbs = dbs + jnp.dot(dcbt.astype(BF16), cs_g, preferred_element_type=F32)
            dxdt_parts.append(jnp.concatenate(dxdt_h, axis=1) + g1 * d_s_x[:, gw])
            dbs_parts.append(dbs)
            dcs_parts.append(dcs)
            stores.append((ds_scr, (slice(None), gw), dst[:, gw] * cd_x[:, gw] + dst_new))
            yield
        dxdt = jnp.concatenate(dxdt_parts, axis=1)
        y_off = jnp.concatenate(off_parts, axis=1)
        g1 = jnp.concatenate(g1_parts, axis=1)
        dcd_lane = jnp.sum(dst * st_b.astype(F32), axis=0, keepdims=True)
        vecs = jnp.concatenate([jnp.broadcast_to(dcd_lane, (SUB, D_B)), jnp.broadcast_to(dd_lane, (SUB, D_B))], axis=0)
        dds, da_off, ddt_x, vec_sums = _split_dot_many(
            [g1 * xdt, w_x * y_off, dxdt * xs, vecs], [1, 1, 1, 1], e_mat_t)
        dcd = vec_sums[0:1]
        sums.setdefault("dd", []).append(vec_sums[SUB:SUB + 1])
        yield
        t_ds = dds * d_s
        d_a = da_diag + da_off - t_ds
        last_row = jnp.sum(t_ds, axis=0, keepdims=True) + dcd * e_a[CHUNK - 1:CHUNK, :]
        d_a = d_a + jnp.where(rows == CHUNK - 1, last_row, 0.0)
        dda = _split_dot_left(_tri(False).astype(BF16), d_a, 3)
        ddt = dda * a + ddt_x
        sums.setdefault("dal", []).append(jnp.sum(dda * dt * a, axis=0, keepdims=True))
        ddtraw = jnp.where(valid_col, ddt * jax.nn.sigmoid(dtraw + dtb_ref[...]), 0.0)
        ddtraw = jnp.where(lane < N_HEADS, ddtraw, 0.0)
        sums.setdefault("ddtb", []).append(jnp.sum(ddtraw, axis=0, keepdims=True))
        stores.append((dpw_ref, (slice(None), slice(D_B, D_B + LANE)), ddtraw.astype(BF16)))
        dxs = dxs + dxdt * dt_x
        dxbc = jnp.concatenate([dxs] + dbs_parts + dcs_parts, axis=1)
        stores.append((dxbc_ref, everything, jnp.where(valid_col, dxbc, 0.0).astype(BF16)))

    def row(width, col):
        return pl.BlockSpec((nb, CHUNK, width), lambda c: (0, cpe - 1 - c, col))

    def const(shape):
        return pl.BlockSpec(shape, lambda c: (0,) * len(shape))

    proj3 = proj.reshape(nb, lp, N_INP)
    dxbc, dproj, *rest = _grid_call(
        body, name, (cpe,),
        [row(D_B, 0), row(D_B, 0), row(N_XBC, 0), row(D_B, COL_BZ // D_B), row(LANE, COL_DT // LANE),
         pl.BlockSpec((nb, None, N_STATE, D_B), lambda c: (0, cpe - 1 - c, 0, 0)),
         const((1, LANE)), const((N_HEADS, 1)), const((1, LANE)), const((N_HEADS, 1)),
         const((1, D_B)), const((1, D_B)), const((LANE, D_B)), const((D_B, LANE))],
        [row(N_XBC, 0),
         pl.BlockSpec((pl.Element(nb), pl.Element(CHUNK), pl.Element(D_B + LANE)),
                      lambda c: (0, (cpe - 1 - c) * CHUNK, COL_BZ)),
         const((1, D_B)), const((1, LANE)), const((1, LANE)), const((1, LANE))],
        [jax.ShapeDtypeStruct((nb, lp, N_XBC), BF16), jax.ShapeDtypeStruct((nb, lp, N_INP), BF16),
         jax.ShapeDtypeStruct((1, D_B), F32),
         jax.ShapeDtypeStruct((1, LANE), F32), jax.ShapeDtypeStruct((1, LANE), F32),
         jax.ShapeDtypeStruct((1, LANE), F32)],
        [pltpu.VMEM((nb, N_STATE, D_B), F32)],
        (dy.reshape(nb, lp, -1), y_ssd.reshape(nb, lp, D_B), xbc.reshape(nb, lp, N_XBC), proj3, proj3, states,
         dtb_p, dtb_t, alog_p, alog_t, d_skip_x, norm_g, e_mat, e_mat_t), comm)
    return (dxbc.reshape(t, N_XBC), dproj.reshape(t, N_INP), *rest)


HBM_SPEC = pl.BlockSpec(memory_space=pl.ANY)


def _mesh_pos():
    return lax.axis_index("x"), lax.axis_index("y"), lax.axis_index("c")


def _remote(src, dst, send_sem, recv_sem, to):
    return pltpu.make_async_remote_copy(src_ref=src, dst_ref=dst, send_sem=send_sem, recv_sem=recv_sem,
                                        device_id=to, device_id_type=MESH)


def _slot_plain(d):
    return d


def _slot_mix_rows(d):
    return jnp.where(d < 2, d + 4, jnp.where(d < 6, d - 2, d))


def _ag_ici(pieces, slots):
    n = len(pieces)

    def copies(ins, outs, sems, with_recvs):
        send_sems, recv_sems, local_sems = sems
        x, y, c = _mesh_pos()
        local, sends, recvs = [], [], []
        for a in range(n):
            mine = outs[a].at[slots[a](4 * x + 2 * y + c)]
            local.append(pltpu.make_async_copy(ins[a], mine, local_sems.at[a]))
            for k, (px, py) in enumerate([(1 - x, y), (x, 1 - y), (1 - x, 1 - y)]):
                sends.append(_remote(ins[a], mine, send_sems.at[a, k], recv_sems.at[a, k], (px, py, c)))
                if with_recvs:
                    theirs = outs[a].at[slots[a](4 * px + 2 * py + c)]
                    recvs.append(_remote(ins[a], theirs, send_sems.at[a, k], recv_sems.at[a, k], (px, py, c)))
        return local, sends, recvs

    return _Comm(pieces, [jax.ShapeDtypeStruct((N_DEV,) + p.shape, p.dtype) for p in pieces],
                 [pltpu.SemaphoreType.DMA((n, 3)), pltpu.SemaphoreType.DMA((n, 3)), pltpu.SemaphoreType.DMA((n,))], copies)


def _ag_d2d(bufs, slots):
    n = len(bufs)

    def copies(ins, outs, sems, with_recvs):
        send_sems, recv_sems = sems
        x, y, c = _mesh_pos()
        chips = [(x, y), (1 - x, y), (x, 1 - y), (1 - x, 1 - y)]
        sends, recvs = [], []
        for a in range(n):
            for k, (px, py) in enumerate(chips):
                held = slots[a](4 * px + 2 * py + c)
                sends.append(_remote(ins[a].at[held], outs[a].at[held], send_sems.at[a, k], recv_sems.at[a, k], (x, y, 1 - c)))
                if with_recvs:
                    got = slots[a](4 * px + 2 * py + 1 - c)
                    recvs.append(_remote(ins[a].at[got], outs[a].at[got], send_sems.at[a, k], recv_sems.at[a, k], (x, y, 1 - c)))
        return [], sends, recvs

    return _Comm(bufs, [jax.ShapeDtypeStruct(b.shape, b.dtype) for b in bufs],
                 [pltpu.SemaphoreType.DMA((n, 4)), pltpu.SemaphoreType.DMA((n, 4))], copies,
                 aliases={a: a for a in range(n)})


def _rs_d2d(blocks, slots):
    n = len(blocks)

    def copies(ins, outs, sems, with_recvs):
        send_sems, recv_sems = sems
        x, y, c = _mesh_pos()
        sends, recvs = [], []
        for a in range(n):
            for j in range(4):
                src = ins[a].at[slots[a](2 * j + 1 - c)]
                sends.append(_remote(src, outs[a].at[j], send_sems.at[a, j], recv_sems.at[a, j], (x, y, 1 - c)))
                if with_recvs:
                    recvs.append(_remote(src, outs[a].at[j], send_sems.at[a, j], recv_sems.at[a, j], (x, y, 1 - c)))
        return [], sends, recvs

    return _Comm(blocks, [jax.ShapeDtypeStruct((4,) + b.shape[1:], b.dtype) for b in blocks],
                 [pltpu.SemaphoreType.DMA((n, 4)), pltpu.SemaphoreType.DMA((n, 4))], copies)


def _rs_ici(blocks):
    n = len(blocks)

    def copies(ins, outs, sems, with_recvs):
        send_sems, recv_sems = sems
        x, y, c = _mesh_pos()
        sends, recvs = [], []
        for a in range(n):
            for k, (px, py) in enumerate([(1 - x, y), (x, 1 - y), (1 - x, 1 - y)]):
                src = ins[a].at[2 * px + py]
                sends.append(_remote(src, outs[a].at[k], send_sems.at[a, k], recv_sems.at[a, k], (px, py, c)))
                if with_recvs:
                    recvs.append(_remote(src, outs[a].at[k], send_sems.at[a, k], recv_sems.at[a, k], (px, py, c)))
        return [], sends, recvs

    return _Comm(blocks, [jax.ShapeDtypeStruct((3,) + b.shape[1:], b.dtype) for b in blocks],
                 [pltpu.SemaphoreType.DMA((n, 3)), pltpu.SemaphoreType.DMA((n, 3))], copies)


def _run_comm(comm, name):
    n_in, n_out = len(comm.inputs), len(comm.out_shapes)

    def body(*refs):
        ins, outs, sems = refs[:n_in], refs[n_in:n_in + n_out], refs[n_in + n_out:]
        comm.start(ins, outs, sems)
        comm.wait(ins, outs, sems)

    return pl.pallas_call(
        body, name=name, in_specs=[HBM_SPEC] * n_in, out_specs=[HBM_SPEC] * n_out, out_shape=comm.out_shapes,
        scratch_shapes=comm.sem_shapes, input_output_aliases=comm.aliases,
    )(*comm.inputs)


W_ROWS = 784


def _w_segments():
    per = N_IN // N_DEV
    out = []
    for d in range(N_DEV):
        lo, hi = per * d, per * (d + 1)
        for a, b, start in COL_MAP:
            s, e = max(lo, a), min(hi, b)
            if s < e:
                out.append((d, s - lo, e - s, s - a + start))
    return out


def _w_gather_rows(g, name):
    tk = D_MODEL // 2
    u32 = jnp.uint32

    def body(g_ref, o_ref, scr):
        for d in range(N_DEV):
            x32 = pltpu.bitcast(g_ref[d], u32)
            for dd, src, rows, dst in _w_segments():
                if dd == d:
                    scr[dst // 2:(dst + rows) // 2, :] = x32[src // 2:(src + rows) // 2]
        scr[(COL_DT + N_HEADS) // 2:COL_XBC // 2, :] = jnp.zeros(((COL_XBC - COL_DT - N_HEADS) // 2, tk), u32)
        o_ref[...] = pltpu.bitcast(scr[...], BF16)

    return pl.pallas_call(
        body, name=name, grid=(D_MODEL // tk,),
        in_specs=[pl.BlockSpec((N_DEV, W_ROWS, tk), lambda j: (0, 0, j))],
        out_specs=pl.BlockSpec((N_INP, tk), lambda j: (0, j)),
        out_shape=jax.ShapeDtypeStruct((N_INP, D_MODEL), BF16),
        scratch_shapes=[pltpu.VMEM((N_INP // 2, tk), u32)],
        compiler_params=_cparams(("parallel",)),
    )(g)


def _w_split_rows(dwt, name):
    tk = D_MODEL // 4
    per = N_IN // N_DEV

    def body(w_ref, o_ref):
        for d, dst, rows, src in _w_segments():
            o_ref[d, dst // 2:(dst + rows) // 2, :] = pltpu.bitcast(w_ref[src:src + rows, :].astype(BF16), F32)
        for d in range(N_DEV):
            o_ref[d, per // 2:W_ROWS // 2, :] = jnp.zeros(((W_ROWS - per) // 2, tk), F32)

    return pl.pallas_call(
        body, name=name, grid=(D_MODEL // tk,),
        in_specs=[pl.BlockSpec((N_INP, tk), lambda j: (0, j))],
        out_specs=pl.BlockSpec((N_DEV, W_ROWS // 2, tk), lambda j: (0, 0, j)),
        out_shape=jax.ShapeDtypeStruct((N_DEV, W_ROWS // 2, D_MODEL), F32),
        compiler_params=_cparams(("parallel",)),
    )(dwt)


def _rs_pair_sum(g, ra, own_slots, name):
    packed = g.dtype == F32
    _, rows, cols = g.shape
    out_rows = 2 * rows if packed else rows

    def body(s_ref, g_ref, ra_ref, o_ref):
        a, b = g_ref[...], ra_ref[...]
        if packed:
            a, b = pltpu.bitcast(a, BF16), pltpu.bitcast(b, BF16)
        o_ref[...] = (a.astype(F32) + b.astype(F32)).astype(BF16)

    return pl.pallas_call(
        body, name=name,
        grid_spec=pltpu.PrefetchScalarGridSpec(
            num_scalar_prefetch=1, grid=(4,),
            in_specs=[pl.BlockSpec((None, rows, cols), lambda j, s: (s[j], 0, 0)),
                      pl.BlockSpec((None, rows, cols), lambda j, s: (j, 0, 0))],
            out_specs=pl.BlockSpec((None, out_rows, cols), lambda j, s: (j, 0, 0))),
        out_shape=jax.ShapeDtypeStruct((4, out_rows, cols), BF16),
        compiler_params=_cparams(("parallel",)),
    )(own_slots, g, ra)


def _rs_final_sum(h, rb, chip_idx, name):
    _, rows, cols = h.shape

    def body(j_ref, h_ref, rb_ref, o_ref):
        o_ref[...] = ((h_ref[...].astype(F32) + rb_ref[0].astype(F32)) + rb_ref[1].astype(F32)) + rb_ref[2].astype(F32)

    return pl.pallas_call(
        body, name=name,
        grid_spec=pltpu.PrefetchScalarGridSpec(
            num_scalar_prefetch=1, grid=(1,),
            in_specs=[pl.BlockSpec((None, rows, cols), lambda i, j: (j[0], 0, 0)),
                      pl.BlockSpec((3, rows, cols), lambda i, j: (0, 0, 0))],
            out_specs=pl.BlockSpec((rows, cols), lambda i, j: (0, 0))),
        out_shape=jax.ShapeDtypeStruct((rows, cols), F32),
        compiler_params=_cparams(("arbitrary",)),
    )(chip_idx, h, rb)


def _adamw_math(w, g, m, v):
    m = ADAM_B1 * m + (1.0 - ADAM_B1) * g
    v = ADAM_B2 * v + (1.0 - ADAM_B2) * (g * g)
    m_hat = m / (1.0 - ADAM_B1 ** ADAM_STEP)
    v_hat = v / (1.0 - ADAM_B2 ** ADAM_STEP)
    delta = -ADAM_LR * (m_hat / (jnp.sqrt(v_hat) + ADAM_EPS) + ADAM_WD * w)
    return delta, m, v


def _adamw_rows(g, w, m, v, tr, name):
    layers, rows, cols = w.shape

    def body(g_ref, w_ref, m_ref, v_ref, d_out, m_out, v_out):
        delta, m_new, v_new = _adamw_math(w_ref[...], g_ref[...], m_ref[...], v_ref[...])
        d_out[...] = delta
        m_out[...] = m_new
        v_out[...] = v_new

    blk = pl.BlockSpec((None, tr, cols), lambda a, r: (a, r, 0))
    return pl.pallas_call(
        body, name=name, grid=(layers, rows // tr),
        in_specs=[blk] * 4, out_specs=[blk] * 3,
        out_shape=[jax.ShapeDtypeStruct((layers, rows, cols), F32)] * 3,
        compiler_params=_cparams(("parallel", "parallel")),
    )(g, w, m, v)


def _adamw_cols(g_t, w, m, v, name):
    layers, k, cols = w.shape
    tr = 56
    assert g_t.shape[1] % tr == 0
    views = [jnp.transpose(a, (2, 0, 1)) for a in (w, m, v)]

    def body(g_ref, w_ref, m_ref, v_ref, g_out, d_out, m_out, v_out):
        for a in range(layers):
            g = g_ref[a]
            delta, m_new, v_new = _adamw_math(w_ref[:, a, :], g, m_ref[:, a, :], v_ref[:, a, :])
            g_out[:, a, :] = g
            d_out[:, a, :] = delta
            m_out[:, a, :] = m_new
            v_out[:, a, :] = v_new

    col = pl.BlockSpec((tr, layers, k), lambda r: (r, 0, 0))
    outs = pl.pallas_call(
        body, name=name, grid=(pl.cdiv(cols, tr),),
        in_specs=[pl.BlockSpec((layers, tr, k), lambda r: (0, r, 0)), col, col, col], out_specs=[col] * 4,
        out_shape=[jax.ShapeDtypeStruct((cols, layers, k), F32)] * 4,
        compiler_params=_cparams(("parallel",)),
    )(g_t, *views)
    return [jnp.transpose(o, (1, 2, 0)) for o in outs]


def _sum_devices(parts, name):
    _, p, _ = parts.shape

    def body(x_ref, o_ref):
        acc = x_ref[0]
        for d in range(1, N_DEV):
            acc = acc + x_ref[d]
        o_ref[...] = acc

    return pl.pallas_call(
        body, name=name, grid=(1,),
        in_specs=[pl.BlockSpec((N_DEV, p, LANE), lambda i: (0, 0, 0))],
        out_specs=pl.BlockSpec((p, LANE), lambda i: (0, 0)),
        out_shape=jax.ShapeDtypeStruct((p, LANE), F32),
        compiler_params=_cparams(("arbitrary",)),
    )(parts)


def _adamw_small(g, w, m, v, name):
    p = g.shape[0]

    def body(g_ref, w_ref, m_ref, v_ref, d_out, m_out, v_out):
        delta, m_new, v_new = _adamw_math(w_ref[...], g_ref[...], m_ref[...], v_ref[...])
        d_out[...] = delta
        m_out[...] = m_new
        v_out[...] = v_new

    spec = pl.BlockSpec((p, LANE), lambda i: (0, 0))
    return pl.pallas_call(
        body, name=name, grid=(1,),
        in_specs=[spec] * 4, out_specs=[spec] * 3,
        out_shape=[jax.ShapeDtypeStruct((p, LANE), F32)] * 3,
        compiler_params=_cparams(("arbitrary",)),
    )(g, w, m, v)


PACK_ALIGN = SUB * LANE

SMALL_PARAMS = (
    ("meta", (N_META, D_MODEL), 1),
    ("pre_g", (DEPTH, D_MODEL), None),
    ("post_g", (DEPTH, D_MODEL), None),
    ("conv_a_w", (DEPTH, CONV_A_K, D_A), 2),
    ("ssm_conv_w", (DEPTH, SSM_K, N_XBC), 2),
    ("ssm_conv_b", (DEPTH, N_XBC), None),
    ("dt_bias", (DEPTH, N_HEADS), None),
    ("a_log", (DEPTH, N_HEADS), None),
    ("d_skip", (DEPTH, N_HEADS), None),
    ("ssm_norm_g", (DEPTH, D_B), None),
    ("conf_conv_w", (DEPTH, CONF_K, D_C), 2),
    ("conf_conv_b", (DEPTH, D_C), None),
    ("conf_ln_g", (DEPTH, D_C), None),
    ("conf_ln_b", (DEPTH, D_C), None),
)


def _local_shape(shape, axis):
    if axis is None:
        return shape
    return tuple(s // N_DEV if k == axis else s for k, s in enumerate(shape))


def _pack(arrays):
    flat = []
    for a in arrays:
        v = a.reshape(-1).astype(F32)
        flat.append(v)
        if v.shape[0] % PACK_ALIGN:
            flat.append(jnp.zeros(((-v.shape[0]) % PACK_ALIGN,), F32))
    return jnp.concatenate(flat).reshape(-1, LANE)


def _unpack(buf, shapes):
    flat = buf.reshape(-1)
    out, off = [], 0
    for s in shapes:
        size = 1
        for k in s:
            size *= k
        out.append(flat[off:off + size].reshape(s))
        off += size + (-size) % PACK_ALIGN
    return out


def kernel(x, meta, pre_g, post_g, w_in, w_out, conv_a_w, ssm_conv_w, ssm_conv_b, dt_bias, a_log, d_skip, ssm_norm_g, conf_conv_w, conf_conv_b, conf_ln_g, conf_ln_b, loss_target, m_meta, m_pre_g, m_post_g, m_w_in, m_w_out, m_conv_a_w, m_ssm_conv_w, m_ssm_conv_b, m_dt_bias, m_a_log, m_d_skip, m_ssm_norm_g, m_conf_conv_w, m_conf_conv_b, m_conf_ln_g, m_conf_ln_b, v_meta, v_pre_g, v_post_g, v_w_in, v_w_out, v_conv_a_w, v_ssm_conv_w, v_ssm_conv_b, v_dt_bias, v_a_log, v_d_skip, v_ssm_norm_g, v_conf_conv_w, v_conf_conv_b, v_conf_ln_g, v_conf_ln_b):
    weights = dict(meta=meta, pre_g=pre_g, post_g=post_g, conv_a_w=conv_a_w, ssm_conv_w=ssm_conv_w, ssm_conv_b=ssm_conv_b,
                   dt_bias=dt_bias, a_log=a_log, d_skip=d_skip, ssm_norm_g=ssm_norm_g, conf_conv_w=conf_conv_w,
                   conf_conv_b=conf_conv_b, conf_ln_g=conf_ln_g, conf_ln_b=conf_ln_b)
    mom1 = dict(meta=m_meta, pre_g=m_pre_g, post_g=m_post_g, conv_a_w=m_conv_a_w, ssm_conv_w=m_ssm_conv_w,
                ssm_conv_b=m_ssm_conv_b, dt_bias=m_dt_bias, a_log=m_a_log, d_skip=m_d_skip, ssm_norm_g=m_ssm_norm_g,
                conf_conv_w=m_conf_conv_w, conf_conv_b=m_conf_conv_b, conf_ln_g=m_conf_ln_g, conf_ln_b=m_conf_ln_b)
    mom2 = dict(meta=v_meta, pre_g=v_pre_g, post_g=v_post_g, conv_a_w=v_conv_a_w, ssm_conv_w=v_ssm_conv_w,
                ssm_conv_b=v_ssm_conv_b, dt_bias=v_dt_bias, a_log=v_a_log, d_skip=v_d_skip, ssm_norm_g=v_ssm_norm_g,
                conf_conv_w=v_conf_conv_w, conf_conv_b=v_conf_conv_b, conf_ln_g=v_conf_ln_g, conf_ln_b=v_conf_ln_b)
    nb, seq, d = x.shape
    lp = PAD + N_META + seq
    t = nb * lp
    assert lp % (SHORT_TILES * HALO_C) == 0 and t % (3 * LANE) == 0 and d == D_MODEL
    rt = lp // 3
    xi, yi, ci = _mesh_pos()
    dev = 4 * xi + 2 * yi + ci
    ci32 = ci.astype(jnp.int32)
    chip_idx = (2 * xi + yi).astype(jnp.int32).reshape(1)
    own_plain = jnp.stack([2 * j + ci32 for j in range(4)])
    own_mix = jnp.stack([_slot_mix_rows(2 * j + ci32) for j in range(4)]).astype(jnp.int32)
    n_in_loc = N_IN // N_DEV
    n_out_loc = 2 * D_MODEL // N_DEV
    slots = (_slot_plain, _slot_mix_rows)

    sharded_small = [n for n, _, ax in SMALL_PARAMS if ax is not None]
    small_shapes = {n: s for n, s, _ in SMALL_PARAMS}
    small_axis = {n: ax for n, _, ax in SMALL_PARAMS}
    sw_pack = _pack([weights[n] for n in sharded_small])
    wt_loc = jnp.pad(jnp.swapaxes(w_in, 1, 2).astype(BF16), ((0, 0), (0, W_ROWS - n_in_loc), (0, 0)))
    wo_loc = w_out.astype(BF16)
    plain2 = (_slot_plain, _slot_plain)
    head = _run_comm(_ag_ici([wt_loc[0], sw_pack], plain2), "ag_ici")
    g_in_t, sw_g = _run_comm(_ag_d2d(head, plain2), "ag_d2d")

    def gathered(bufs):
        return _w_gather_rows(bufs[0], "w_gather_rows"), bufs[1].reshape(2 * D_MODEL, D_MODEL)

    full = dict(weights)
    per_dev = [_unpack(sw_g[k], [_local_shape(small_shapes[n], small_axis[n]) for n in sharded_small]) for k in range(N_DEV)]
    for q, n in enumerate(sharded_small):
        full[n] = jnp.concatenate([per_dev[k][q] for k in range(N_DEV)], axis=small_axis[n])

    e_mat = (lax.broadcasted_iota(jnp.int32, (LANE, D_B), 0) == lax.broadcasted_iota(jnp.int32, (LANE, D_B), 1) // HEAD_DIM)
    e_mat = e_mat.astype(BF16)
    e_mat_t = e_mat.T

    front = jnp.concatenate([jnp.zeros((PAD, d), F32), full["meta"]], axis=0)
    h = jnp.concatenate([jnp.concatenate([front, x[b]], axis=0) for b in range(nb)], axis=0)
    saved = []
    w_t, w_o = _w_gather_rows(g_in_t, "w_gather_rows"), None
    hn = _rms_fwd(h, pre_g[0].reshape(1, -1), rt, "rms_fwd")
    for i in range(DEPTH):
        row = lambda a: a[i].reshape(1, -1)
        if i == 0:
            proj, wo_buf = _mm(hn, w_t, "nt", BF16, t // 2, MM_TN, D_MODEL, "mm_proj", cols_outer=True,
                               comm=_ag_ici([wo_loc[0]], (_slot_mix_rows,)))
        else:
            proj = _mm(hn, w_t, "nt", BF16, t // 2, MM_TN, D_MODEL, "mm_proj", cols_outer=True)
        ycat = _a_fwd(proj, full["conv_a_w"][i], lp, "a_fwd")
        xbc = _xbc_fwd(proj, full["ssm_conv_w"][i], row(ssm_conv_b), lp, "xbc_fwd")
        ycat, u1 = _c_fwd(proj, full["conf_conv_w"][i], row(conf_conv_b), row(conf_ln_g), row(conf_ln_b), ycat, lp, "c_fwd")
        d_skip_x = jnp.repeat(d_skip[i], HEAD_DIM).reshape(1, D_B)
        nxt = _ag_ici([wt_loc[i + 1], wo_loc[i + 1]], slots) if i + 1 < DEPTH else None
        if i == 0:
            nxt = _merge_comm(nxt, _ag_d2d([wo_buf], (_slot_mix_rows,)))
        ycat, yssd, states, *bufs = _ssd_fwd(xbc, proj, dt_bias[i], a_log[i], d_skip_x, row(ssm_norm_g), e_mat, ycat, lp,
                                             "ssd_fwd", comm=nxt)
        if i == 0:
            w_o = bufs.pop().reshape(2 * D_MODEL, D_MODEL)
        if bufs:
            m, *bufs = _mm(ycat, w_o, "nn", F32, t // 3, D_MODEL, 2 * D_MODEL, "mm_out", comm=_ag_d2d(bufs, slots))
        else:
            m = _mm(ycat, w_o, "nn", F32, t // 3, D_MODEL, 2 * D_MODEL, "mm_out")
        saved.append((h, hn, proj, ycat, yssd, states, xbc, m, d_skip_x, w_t, w_o, u1))
        if i + 1 < DEPTH:
            h, hn = _post_rms_fwd(h, m, row(post_g), pre_g[i + 1].reshape(1, -1), rt, "post_rms_fwd")
        else:
            h = _post_fwd(h, m, row(post_g), rt, "post_fwd")
        if bufs:
            w_t, w_o = gathered(bufs)

    dh, loss_blk = _loss_kernel(h, loss_target.reshape(nb * seq, d), lp, "loss")

    grads = {n: [None] * DEPTH for n, _, _ in SMALL_PARAMS if n != "meta"}
    gt_in, gr_out = [None] * DEPTH, [None] * DEPTH
    dmeta = None
    pending = None
    dm = None

    def rs_pair_sums(blocks, ras):
        return [_rs_pair_sum(blocks[0], ras[0], own_plain, "rs_pair_sum_in"),
                _rs_pair_sum(blocks[1], ras[1], own_mix, "rs_pair_sum_out")]

    def rs_finish(layer, hs, rbs):
        gt_in[layer] = _rs_final_sum(hs[0], rbs[0], chip_idx, "rs_final_sum_in")
        gr_out[layer] = _rs_final_sum(hs[1], rbs[1], chip_idx, "rs_final_sum_out")

    for i in reversed(range(DEPTH)):
        row = lambda a: a[i].reshape(1, -1)
        h_i, hn, proj, ycat, yssd, states, xbc, m, d_skip_x, w_t, w_o, u1 = saved[i]
        if dm is None:
            dm, grads["post_g"][i] = _post_bwd(dh, m, row(post_g), rt, "post_bwd")
        if pending is not None:
            dy, *ras = _mm(dm, w_o, "nt", BF16, t // 3, D_MODEL, D_MODEL, "mm_dy", cols_outer=True,
                           comm=_rs_d2d(list(pending), slots))
            hs = rs_pair_sums(pending, ras)
        else:
            dy = _mm(dm, w_o, "nt", BF16, t // 3, D_MODEL, D_MODEL, "mm_dy", cols_outer=True)
            hs = None
        dw_out = _mm(ycat, dm, "tn", BF16, D_MODEL, D_MODEL, t // 3, "mm_dwout")
        dxbc, dproj, grads["ssm_norm_g"][i], ddtb, dal, dds, *rbs = _ssd_bwd(
            dy, yssd, xbc, proj, states, dt_bias[i], a_log[i], d_skip_x, row(ssm_norm_g), e_mat, e_mat_t, lp, "ssd_bwd",
            comm=_rs_ici(hs) if hs is not None else None)
        if hs is not None:
            rs_finish(i + 1, hs, rbs)
        grads["dt_bias"][i] = ddtb[:, :N_HEADS]
        grads["a_log"][i] = dal[:, :N_HEADS]
        grads["d_skip"][i] = dds[:, :N_HEADS]
        dproj, grads["conv_a_w"][i] = _a_bwd(dy, proj, full["conv_a_w"][i], dproj, lp, "a_bwd")
        dproj, grads["ssm_conv_w"][i], grads["ssm_conv_b"][i] = _xbc_bwd(
            dxbc, proj, full["ssm_conv_w"][i], row(ssm_conv_b), dproj, lp, "xbc_bwd")
        dproj, grads["conf_conv_w"][i], grads["conf_conv_b"][i], grads["conf_ln_g"][i], grads["conf_ln_b"][i] = _c_bwd(
            dy, proj, u1, full["conf_conv_w"][i], row(conf_conv_b), row(conf_ln_g), row(conf_ln_b), dproj, lp, "c_bwd")
        p_out = dw_out.reshape(N_DEV, n_out_loc, D_MODEL)
        if i > 0:
            dhn = _mm(dproj, w_t, "nn", BF16, MM_TM, D_MODEL, N_INP, "mm_dhn")
            dw_in_t = _mm(dproj, hn, "tn", F32, MM_TN, D_MODEL, t, "mm_dwin")
        else:
            dhn, ra_out = _mm(dproj, w_t, "nn", BF16, MM_TM, D_MODEL, N_INP, "mm_dhn",
                              comm=_rs_d2d([p_out], (_slot_mix_rows,)))
            h_out = _rs_pair_sum(p_out, ra_out, own_mix, "rs_pair_sum_out")
            dw_in_t, rb_out = _mm(dproj, hn, "tn", F32, MM_TN, D_MODEL, t, "mm_dwin", comm=_rs_ici([h_out]))
            gr_out[0] = _rs_final_sum(h_out, rb_out, chip_idx, "rs_final_sum_out")
        if i > 0:
            dh, grads["pre_g"][i], dm, grads["post_g"][i - 1] = _rms_post_bwd(
                dh, dhn, h_i, row(pre_g), saved[i - 1][7], post_g[i - 1].reshape(1, -1), rt, "rms_post_bwd")
        else:
            dh, grads["pre_g"][i], dmeta = _rms_bwd(dh, dhn, h_i, row(pre_g), lp, rt, "rms_bwd")
        pending = (_w_split_rows(dw_in_t, "w_split_rows"), p_out)
    grad_x = dh.reshape(nb, lp, d)[:, PAD + N_META:]

    names = [n for n, _, _ in SMALL_PARAMS]
    partial = [loss_blk[0:1, 0:1], dmeta] + [jnp.concatenate(grads[n], axis=0) for n in names[1:]]
    part_pack = _pack(partial)
    p_in = pending[0]
    ra_in, part_buf = _run_comm(_merge_comm(_rs_d2d([p_in], (_slot_plain,)), _ag_ici([part_pack], (_slot_plain,))), "rs_d2d")
    h_in = _rs_pair_sum(p_in, ra_in, own_plain, "rs_pair_sum_in")
    rb_in, parts_g = _run_comm(_merge_comm(_rs_ici([h_in]), _ag_d2d([part_buf], (_slot_plain,))), "rs_ici")
    gt_in[0] = _rs_final_sum(h_in, rb_in, chip_idx, "rs_final_sum_in")

    g_w_out = jnp.stack(gr_out)
    big = {"w_in": _adamw_cols(jnp.stack(gt_in), w_in, m_w_in, v_w_in, "adamw_w_in"),
           "w_out": [g_w_out, *_adamw_rows(g_w_out, w_out, m_w_out, v_w_out, n_out_loc, "adamw_w_out")]}

    total = _unpack(_sum_devices(parts_g, "sum_small_grads"), [(1,)] + [small_shapes[n] for n in names])
    loss = total[0][0]
    g_small = {}
    for n, g in zip(names, total[1:]):
        ax = small_axis[n]
        if ax is not None:
            g = lax.dynamic_slice_in_dim(g, dev * (small_shapes[n][ax] // N_DEV), small_shapes[n][ax] // N_DEV, axis=ax)
        g_small[n] = g
    loc_shapes = [_local_shape(small_shapes[n], small_axis[n]) for n in names]
    d_pack, m_pack, v_pack = _adamw_small(_pack([g_small[n] for n in names]), _pack([weights[n] for n in names]),
                                          _pack([mom1[n] for n in names]), _pack([mom2[n] for n in names]), "adamw_small")
    d_small = dict(zip(names, _unpack(d_pack, loc_shapes)))
    m_small = dict(zip(names, _unpack(m_pack, loc_shapes)))
    v_small = dict(zip(names, _unpack(v_pack, loc_shapes)))

    order = ["meta", "pre_g", "post_g", "w_in", "w_out", "conv_a_w", "ssm_conv_w", "ssm_conv_b", "dt_bias", "a_log",
             "d_skip", "ssm_norm_g", "conf_conv_w", "conf_conv_b", "conf_ln_g", "conf_ln_b"]

    def pick(k, small):
        return [big[n][k] if n in big else small[n] for n in order]

    return (loss, grad_x, *pick(0, g_small), *pick(1, d_small), *pick(2, m_small), *pick(3, v_small))
```

```python
import jax
import jax.numpy as jnp
from jax import lax
from jax.experimental import pallas as pl
from jax.experimental.pallas import tpu as pltpu

F32 = jnp.float32
BF16 = jnp.bfloat16

D_MODEL = 1024
DEPTH = 4
SEQ = 2048
CHUNK = 64
N_META = 16
PAD = 48
LP = PAD + N_META + SEQ
D_A = 512
D_B = 1024
D_C = 512
N_HEADS = 16
HEAD_DIM = 64
N_STATE = 128
N_GROUPS = 2
GROUP_W = D_B // N_GROUPS
N_XBC = D_B + 2 * N_GROUPS * N_STATE
CONV_A_K = 3
SSM_K = 4
CONF_K = 31
NORM_EPS = 1e-6
LN_EPS = 1e-5
N_IN = 6160
N_INP = 6272
COL_BZ = 2048
COL_DT = 3072
COL_XBC = 3200
COL_C = 4736
COL_MAP = ((0, 3072, 0), (3072, 4608, COL_XBC), (4608, 4624, COL_DT), (4624, 6160, COL_C))
LANE = 128
SUB = 8

ADAM_LR = 0.001
ADAM_B1 = 0.9
ADAM_B2 = 0.999
ADAM_EPS = 1e-08
ADAM_WD = 0.01
ADAM_STEP = 10

SHORT_TILES = 6
HALO_A = 16
HALO_C = 32
MM_TM = 384
MM_TN = 896
VMEM_LIMIT = 56 * 1024 * 1024

MESH = pl.DeviceIdType.MESH
N_DEV = 8


def _silu(x):
    return x * jax.nn.sigmoid(x)


def _dsilu(x):
    s = jax.nn.sigmoid(x)
    return s * (1.0 + x * (1.0 - s))


def _cparams(sem=None):
    return pltpu.CompilerParams(dimension_semantics=sem, vmem_limit_bytes=VMEM_LIMIT)


def _mm(a, b, mode, out_dtype, tm, tn, tk, name, cols_outer=False, comm=None):
    def ix(f):
        if cols_outer:
            return lambda j, i, q: f(i, j, q)
        return f

    if mode == "nn":
        (m, k), (_, n) = a.shape, b.shape
        a_spec = pl.BlockSpec((tm, tk), ix(lambda i, j, q: (i, q)))
        b_spec = pl.BlockSpec((tk, tn), ix(lambda i, j, q: (q, j)))
        dims = (((1,), (0,)), ((), ()))
    elif mode == "nt":
        (m, k), (n, _) = a.shape, b.shape
        a_spec = pl.BlockSpec((tm, tk), ix(lambda i, j, q: (i, q)))
        b_spec = pl.BlockSpec((tn, tk), ix(lambda i, j, q: (j, q)))
        dims = (((1,), (1,)), ((), ()))
    else:
        (k, m), (_, n) = a.shape, b.shape
        a_spec = pl.BlockSpec((tk, tm), ix(lambda i, j, q: (q, i)))
        b_spec = pl.BlockSpec((tk, tn), ix(lambda i, j, q: (q, j)))
        dims = (((0,), (0,)), ((), ()))
    assert m % tm == 0 and n % tn == 0 and k % tk == 0, (name, a.shape, b.shape)
    nk = k // tk
    grid = (n // tn, m // tm, nk) if cols_outer else (m // tm, n // tn, nk)

    def body(a_ref, b_ref, o_ref, acc_ref):
        part = lax.dot_general(a_ref[...].astype(BF16), b_ref[...].astype(BF16), dims, preferred_element_type=F32)
        if nk == 1:
            o_ref[...] = part.astype(o_ref.dtype)
        else:
            q = pl.program_id(2)

            @pl.when(q == 0)
            def _():
                acc_ref[...] = part

            @pl.when(q > 0)
            def _():
                acc_ref[...] += part

            @pl.when(q == nk - 1)
            def _():
                o_ref[...] = acc_ref[...].astype(o_ref.dtype)

    res = _grid_call(
        body, name, grid, [a_spec, b_spec], [pl.BlockSpec((tm, tn), ix(lambda i, j, q: (i, j)))],
        [jax.ShapeDtypeStruct((m, n), out_dtype)], [pltpu.VMEM((tm, tn) if nk > 1 else (SUB, LANE), F32)], (a, b), comm)
    return res[0] if comm is None else res


def _row_mask(i, tpe, rows):
    r = lax.broadcasted_iota(jnp.int32, (rows, 1), 0)
    return jnp.logical_or((i % tpe) != 0, r >= PAD)


def _rms_fwd(h, g, rt, name):
    t, d = h.shape

    def body(h_ref, g_ref, o_ref):
        x = h_ref[...]
        r = lax.rsqrt(jnp.mean(x * x, axis=-1, keepdims=True) + NORM_EPS)
        o_ref[...] = (x * r * g_ref[...]).astype(BF16)

    return pl.pallas_call(
        body, name=name, grid=(t // rt,),
        in_specs=[pl.BlockSpec((rt, d), lambda i: (i, 0)), pl.BlockSpec((1, d), lambda i: (0, 0))],
        out_specs=pl.BlockSpec((rt, d), lambda i: (i, 0)),
        out_shape=jax.ShapeDtypeStruct((t, d), BF16),
        compiler_params=_cparams(("parallel",)),
    )(h, g)


def _post_fwd(h, m, g, rt, name):
    t, d = h.shape

    def body(h_ref, m_ref, g_ref, o_ref):
        x = m_ref[...]
        r = lax.rsqrt(jnp.mean(x * x, axis=-1, keepdims=True) + NORM_EPS)
        o_ref[...] = h_ref[...] + x * r * g_ref[...]

    row = pl.BlockSpec((rt, d), lambda i: (i, 0))
    return pl.pallas_call(
        body, name=name, grid=(t // rt,),
        in_specs=[row, row, pl.BlockSpec((1, d), lambda i: (0, 0))], out_specs=row,
        out_shape=jax.ShapeDtypeStruct((t, d), F32),
        compiler_params=_cparams(("parallel",)),
    )(h, m, g)


def _post_rms_fwd(h, m, g_post, g_next, rt, name):
    t, d = h.shape

    def body(h_ref, m_ref, gp_ref, gn_ref, o_ref, n_ref):
        x = m_ref[...]
        r = lax.rsqrt(jnp.mean(x * x, axis=-1, keepdims=True) + NORM_EPS)
        y = h_ref[...] + x * r * gp_ref[...]
        o_ref[...] = y
        r2 = lax.rsqrt(jnp.mean(y * y, axis=-1, keepdims=True) + NORM_EPS)
        n_ref[...] = (y * r2 * gn_ref[...]).astype(BF16)

    row = pl.BlockSpec((rt, d), lambda i: (i, 0))
    vec = pl.BlockSpec((1, d), lambda i: (0, 0))
    return pl.pallas_call(
        body, name=name, grid=(t // rt,),
        in_specs=[row, row, vec, vec], out_specs=[row, row],
        out_shape=[jax.ShapeDtypeStruct((t, d), F32), jax.ShapeDtypeStruct((t, d), BF16)],
        compiler_params=_cparams(("parallel",)),
    )(h, m, g_post, g_next)


def _rms_bwd_math(x, g, dy):
    r = lax.rsqrt(jnp.mean(x * x, axis=-1, keepdims=True) + NORM_EPS)
    gdy = dy * g
    dx = r * gdy - x * (r * r * r) * jnp.mean(gdy * x, axis=-1, keepdims=True)
    return dx, dy * x * r


def _post_bwd(dh, m, g, rt, name):
    t, d = dh.shape

    def body(dh_ref, m_ref, g_ref, dm_ref, dg_ref):
        dm, dgt = _rms_bwd_math(m_ref[...], g_ref[...], dh_ref[...])
        dm_ref[...] = dm.astype(BF16)

        @pl.when(pl.program_id(0) == 0)
        def _():
            dg_ref[...] = jnp.zeros_like(dg_ref)

        dg_ref[...] += jnp.sum(dgt, axis=0, keepdims=True)

    row = pl.BlockSpec((rt, d), lambda i: (i, 0))
    vec = pl.BlockSpec((1, d), lambda i: (0, 0))
    return pl.pallas_call(
        body, name=name, grid=(t // rt,),
        in_specs=[row, row, vec], out_specs=[row, vec],
        out_shape=[jax.ShapeDtypeStruct((t, d), BF16), jax.ShapeDtypeStruct((1, d), F32)],
        compiler_params=_cparams(("arbitrary",)),
    )(dh, m, g)


def _rms_bwd(dh_res, dhn, h, g, lp, rt, name):
    t, d = h.shape
    tpe = lp // rt

    def body(dr_ref, dn_ref, h_ref, g_ref, dh_ref, dg_ref, dmeta_ref):
        i = pl.program_id(0)
        dx, dgt = _rms_bwd_math(h_ref[...], g_ref[...], dn_ref[...].astype(F32))
        dh = dr_ref[...] + dx
        dh_ref[...] = dh

        @pl.when(i == 0)
        def _():
            dg_ref[...] = jnp.zeros_like(dg_ref)
            dmeta_ref[...] = jnp.zeros_like(dmeta_ref)

        dg_ref[...] += jnp.sum(dgt, axis=0, keepdims=True)

        @pl.when((i % tpe) == 0)
        def _():
            dmeta_ref[...] += dh[PAD:PAD + N_META, :]

    row = pl.BlockSpec((rt, d), lambda i: (i, 0))
    vec = pl.BlockSpec((1, d), lambda i: (0, 0))
    return pl.pallas_call(
        body, name=name, grid=(t // rt,),
        in_specs=[row, row, row, vec],
        out_specs=[row, vec, pl.BlockSpec((N_META, d), lambda i: (0, 0))],
        out_shape=[jax.ShapeDtypeStruct((t, d), F32), jax.ShapeDtypeStruct((1, d), F32),
                   jax.ShapeDtypeStruct((N_META, d), F32)],
        compiler_params=_cparams(("arbitrary",)),
    )(dh_res, dhn, h, g)


def _rms_post_bwd(dh_res, dhn, h, g, m_prev, g_prev, rt, name):
    t, d = h.shape

    def body(dr_ref, dn_ref, h_ref, g_ref, m_ref, gp_ref, dh_ref, dg_ref, dm_ref, dgp_ref):
        dx, dgt = _rms_bwd_math(h_ref[...], g_ref[...], dn_ref[...].astype(F32))
        dh = dr_ref[...] + dx
        dh_ref[...] = dh
        dm, dgt_prev = _rms_bwd_math(m_ref[...], gp_ref[...], dh)
        dm_ref[...] = dm.astype(BF16)

        @pl.when(pl.program_id(0) == 0)
        def _():
            dg_ref[...] = jnp.zeros_like(dg_ref)
            dgp_ref[...] = jnp.zeros_like(dgp_ref)

        dg_ref[...] += jnp.sum(dgt, axis=0, keepdims=True)
        dgp_ref[...] += jnp.sum(dgt_prev, axis=0, keepdims=True)

    row = pl.BlockSpec((rt, d), lambda i: (i, 0))
    vec = pl.BlockSpec((1, d), lambda i: (0, 0))
    return pl.pallas_call(
        body, name=name, grid=(t // rt,),
        in_specs=[row, row, row, vec, row, vec], out_specs=[row, vec, row, vec],
        out_shape=[jax.ShapeDtypeStruct((t, d), F32), jax.ShapeDtypeStruct((1, d), F32),
                   jax.ShapeDtypeStruct((t, d), BF16), jax.ShapeDtypeStruct((1, d), F32)],
        compiler_params=_cparams(("arbitrary",)),
    )(dh_res, dhn, h, g, m_prev, g_prev)


def _loss_kernel(h, target, lp, name):
    t, d = h.shape
    nb = t // lp
    seq = lp - PAD - N_META
    rows = next(r for r in (1024, 512, 256, 128, CHUNK) if seq % r == 0)
    per = seq // rows

    def body(h_ref, t_ref, zero_ref, dh_ref, loss_ref):
        @pl.when(jnp.logical_and(pl.program_id(0) == 0, pl.program_id(1) == 0))
        def _():
            loss_ref[...] = jnp.zeros_like(loss_ref)

        err = h_ref[...] - t_ref[...]
        dh_ref[...] = err * (1.0 / d)
        loss_ref[...] += (0.5 / d) * jnp.sum(err * err)

    win = pl.BlockSpec((pl.Element(rows), pl.Element(d)),
                       lambda b, j: (pl.multiple_of(b * lp + PAD + N_META + j * rows, CHUNK), 0))
    return pl.pallas_call(
        body, name=name, grid=(nb, per),
        in_specs=[win, pl.BlockSpec((rows, d), lambda b, j: (b * per + j, 0)), HBM_SPEC],
        out_specs=[win, pl.BlockSpec((SUB, LANE), lambda b, j: (0, 0))],
        out_shape=[jax.ShapeDtypeStruct((t, d), F32), jax.ShapeDtypeStruct((SUB, LANE), F32)],
        input_output_aliases={2: 0},
        compiler_params=_cparams(("arbitrary", "arbitrary")),
    )(h, target, jnp.zeros((t, d), F32))


def _window_spec(rows, width, col):
    return pl.BlockSpec((pl.Element(rows), pl.Element(width)), lambda i: (i * rows, col))


def _halo_specs(t, width, col, halo, tile):
    cur = _window_spec(tile, width, col)
    prev = pl.BlockSpec((pl.Element(halo), pl.Element(width)),
                        lambda i: (pl.multiple_of(jnp.maximum(i * tile - halo, 0), halo), col))
    nxt = pl.BlockSpec((pl.Element(halo), pl.Element(width)),
                       lambda i: (pl.multiple_of(jnp.minimum((i + 1) * tile, t - halo), halo), col))
    return cur, prev, nxt


def _f32(ref, lo, hi):
    return ref[:, lo:hi].astype(F32)


def _rows_from(x, start, rows):
    s = start % SUB
    if s == 0:
        return x[start:start + rows]
    return pltpu.roll(x, x.shape[0] - s, axis=0)[start - s:start - s + rows]


def _conv_a(ve, w):
    rows = ve.shape[0] - HALO_A
    return (w[0:1] * _rows_from(ve, HALO_A - 2, rows) + w[1:2] * _rows_from(ve, HALO_A - 1, rows)
            + w[2:3] * ve[HALO_A:HALO_A + rows])


def _a_fwd(proj, w, lp, name):
    t = proj.shape[0]
    tile = lp // SHORT_TILES
    tpe = lp // tile
    cur, prev, _ = _halo_specs(t, 4 * D_A, 0, HALO_A, tile)

    def body(p_ref, ph_ref, w_ref, y_ref):
        first = (pl.program_id(0) % tpe) == 0
        v = _f32(p_ref, D_A, 2 * D_A) * _f32(p_ref, 2 * D_A, 3 * D_A)
        vh = jnp.where(first, 0.0, _f32(ph_ref, D_A, 2 * D_A) * _f32(ph_ref, 2 * D_A, 3 * D_A))
        cv = _conv_a(jnp.concatenate([vh, v], axis=0), w_ref[...])
        y_ref[...] = (_f32(p_ref, 0, D_A) * cv * _silu(_f32(p_ref, 3 * D_A, 4 * D_A))).astype(BF16)

    return pl.pallas_call(
        body, name=name, grid=(t // tile,),
        in_specs=[cur, prev, pl.BlockSpec((CONV_A_K, D_A), lambda i: (0, 0))],
        out_specs=_window_spec(tile, D_A, D_B),
        out_shape=jax.ShapeDtypeStruct((t, D_B + D_A + D_C), BF16),
        compiler_params=_cparams(("parallel",)),
    )(proj, proj, w)


def _a_bwd(dy, proj, w, dproj, lp, name):
    t = proj.shape[0]
    tile = lp // SHORT_TILES
    tpe = lp // tile
    cur, prev, nxt = _halo_specs(t, 4 * D_A, 0, HALO_A, tile)
    dcur, _, dnxt = _halo_specs(t, D_A, D_B, HALO_A, tile)

    def body(dy_ref, dyn_ref, p_ref, ph_ref, pn_ref, w_ref, dproj_ref, dp_ref, dw_ref):
        i = pl.program_id(0)
        first = (i % tpe) == 0
        last = (i % tpe) == tpe - 1
        w = w_ref[...]
        ab, ac, ax, az = (_f32(p_ref, k * D_A, (k + 1) * D_A) for k in range(4))
        v = ac * ax
        vh = jnp.where(first, 0.0, _f32(ph_ref, D_A, 2 * D_A) * _f32(ph_ref, 2 * D_A, 3 * D_A))
        ve = jnp.concatenate([vh, v], axis=0)
        taps = [_rows_from(ve, HALO_A - 2 + k, tile) for k in range(CONV_A_K)]
        cv = w[0:1] * taps[0] + w[1:2] * taps[1] + w[2:3] * taps[2]
        s = _silu(az)
        dy_ = dy_ref[...].astype(F32)
        dcv = dy_ * ab * s
        dcvn = jnp.where(last, 0.0, dyn_ref[...].astype(F32) * _f32(pn_ref, 0, D_A) * _silu(_f32(pn_ref, 3 * D_A, 4 * D_A)))
        dce = jnp.concatenate([dcv, dcvn], axis=0)
        dv = w[2:3] * dce[0:tile] + w[1:2] * _rows_from(dce, 1, tile) + w[0:1] * _rows_from(dce, 2, tile)
        dp = jnp.concatenate([dy_ * cv * s, dv * ax, dv * ac, dy_ * ab * cv * _dsilu(az)], axis=1)
        dp_ref[...] = jnp.where(_row_mask(i, tpe, tile), dp, 0.0).astype(BF16)
        dw = jnp.concatenate(
            [jnp.sum(dcv * taps[k], axis=0, keepdims=True) for k in range(CONV_A_K)], axis=0)

        @pl.when(i == 0)
        def _():
            dw_ref[...] = jnp.zeros_like(dw_ref)

        dw_ref[...] += dw

    wspec = pl.BlockSpec((CONV_A_K, D_A), lambda i: (0, 0))
    return pl.pallas_call(
        body, name=name, grid=(t // tile,),
        in_specs=[dcur, dnxt, cur, prev, nxt, wspec, HBM_SPEC],
        out_specs=[_window_spec(tile, 4 * D_A, 0), wspec],
        out_shape=[jax.ShapeDtypeStruct(dproj.shape, dproj.dtype), jax.ShapeDtypeStruct((CONV_A_K, D_A), F32)],
        input_output_aliases={6: 0},
        compiler_params=_cparams(("arbitrary",)),
    )(dy, dy, proj, proj, proj, w, dproj)


def _conv_ssm(xe, w, rows, off):
    acc = w[0:1] * _rows_from(xe, off - 3, rows)
    for k in range(1, SSM_K):
        acc = acc + w[k:k + 1] * _rows_from(xe, off - 3 + k, rows)
    return acc


def _xbc_fwd(proj, w, b, lp, name):
    t = proj.shape[0]
    tile = lp // SHORT_TILES
    tpe = lp // tile
    cur, prev, _ = _halo_specs(t, N_XBC, COL_XBC, HALO_A, tile)

    def body(x_ref, xh_ref, w_ref, b_ref, o_ref):
        first = (pl.program_id(0) % tpe) == 0
        xh = jnp.where(first, 0.0, xh_ref[...].astype(F32))
        xe = jnp.concatenate([xh, x_ref[...].astype(F32)], axis=0)
        o_ref[...] = _silu(_conv_ssm(xe, w_ref[...], tile, HALO_A) + b_ref[...]).astype(BF16)

    return pl.pallas_call(
        body, name=name, grid=(t // tile,),
        in_specs=[cur, prev, pl.BlockSpec((SSM_K, N_XBC), lambda i: (0, 0)), pl.BlockSpec((1, N_XBC), lambda i: (0, 0))],
        out_specs=pl.BlockSpec((tile, N_XBC), lambda i: (i, 0)),
        out_shape=jax.ShapeDtypeStruct((t, N_XBC), BF16),
        compiler_params=_cparams(("parallel",)),
    )(proj, proj, w, b)


def _xbc_bwd(dxbc, proj, w, b, dproj, lp, name):
    t = proj.shape[0]
    tile = lp // SHORT_TILES
    tpe = lp // tile
    cur, prev, nxt = _halo_specs(t, N_XBC, COL_XBC, HALO_A, tile)
    dcur, _, dnxt = _halo_specs(t, N_XBC, 0, HALO_A, tile)

    def body(d_ref, dn_ref, x_ref, xh_ref, xn_ref, w_ref, b_ref, dproj_ref, dx_ref, dw_ref, db_ref):
        i = pl.program_id(0)
        first = (i % tpe) == 0
        last = (i % tpe) == tpe - 1
        w = w_ref[...]
        xh = jnp.where(first, 0.0, xh_ref[...].astype(F32))
        xe = jnp.concatenate([xh, x_ref[...].astype(F32), xn_ref[...].astype(F32)], axis=0)
        taps = [_rows_from(xe, HALO_A - 3 + k, tile + HALO_A) for k in range(SSM_K)]
        pre = b_ref[...] + w[0:1] * taps[0]
        for k in range(1, SSM_K):
            pre = pre + w[k:k + 1] * taps[k]
        de = jnp.concatenate([d_ref[...].astype(F32), jnp.where(last, 0.0, dn_ref[...].astype(F32))], axis=0)
        dpre = de * _dsilu(pre)
        dx = w[3:4] * dpre[0:tile]
        for k in range(SSM_K - 1):
            dx = dx + w[k:k + 1] * _rows_from(dpre, 3 - k, tile)
        dx_ref[...] = jnp.where(_row_mask(i, tpe, tile), dx, 0.0).astype(BF16)
        dpc = dpre[0:tile]
        dw = jnp.concatenate(
            [jnp.sum(dpc * taps[k][0:tile], axis=0, keepdims=True) for k in range(SSM_K)], axis=0)

        @pl.when(i == 0)
        def _():
            dw_ref[...] = jnp.zeros_like(dw_ref)
            db_ref[...] = jnp.zeros_like(db_ref)

        dw_ref[...] += dw
        db_ref[...] += jnp.sum(dpc, axis=0, keepdims=True)

    wspec = pl.BlockSpec((SSM_K, N_XBC), lambda i: (0, 0))
    bspec = pl.BlockSpec((1, N_XBC), lambda i: (0, 0))
    return pl.pallas_call(
        body, name=name, grid=(t // tile,),
        in_specs=[dcur, dnxt, cur, prev, nxt, wspec, bspec, HBM_SPEC],
        out_specs=[_window_spec(tile, N_XBC, COL_XBC), wspec, bspec],
        out_shape=[jax.ShapeDtypeStruct(dproj.shape, dproj.dtype), jax.ShapeDtypeStruct((SSM_K, N_XBC), F32),
                   jax.ShapeDtypeStruct((1, N_XBC), F32)],
        input_output_aliases={7: 0},
        compiler_params=_cparams(("arbitrary",)),
    )(dxbc, dxbc, proj, proj, proj, w, b, dproj)


SUBROWS = 32


def _fill_shifted(scr, x):
    scr[0] = x
    for s in range(1, SUB):
        scr[s] = pltpu.roll(x, x.shape[0] - s, axis=0)


def _window(scr, start, rows):
    s = start % SUB
    return scr[s, start - s:start - s + rows, :]


def _conv_conf(scr, w, rows, off, base):
    acc = w[0:1] * _window(scr, base + off - (CONF_K - 1), rows)
    for k in range(1, CONF_K):
        acc = acc + w[k:k + 1] * _window(scr, base + off - (CONF_K - 1) + k, rows)
    return acc


def _ln_fwd(u1, g, b):
    mu = jnp.mean(u1, axis=-1, keepdims=True)
    xc = u1 - mu
    rstd = lax.rsqrt(jnp.mean(xc * xc, axis=-1, keepdims=True) + LN_EPS)
    n = xc * rstd
    return n, rstd, n * g + b


def _c_fwd(proj, w, cb, g, b, ybuf, lp, name):
    t = proj.shape[0]
    tile = lp // SHORT_TILES
    tpe = lp // tile
    cur, prev, _ = _halo_specs(t, 3 * D_C, COL_C, HALO_C, tile)

    def body(p_ref, ph_ref, w_ref, cb_ref, g_ref, b_ref, ybuf_ref, y_ref, u1_ref, u0_scr):
        first = (pl.program_id(0) % tpe) == 0
        u0h = jnp.where(first, 0.0, _f32(ph_ref, 0, D_C) * jax.nn.sigmoid(_f32(ph_ref, D_C, 2 * D_C)))
        _fill_shifted(u0_scr, jnp.concatenate([u0h, _f32(p_ref, 0, D_C) * jax.nn.sigmoid(_f32(p_ref, D_C, 2 * D_C))], axis=0))
        w = w_ref[...]
        for r0 in range(0, tile, SUBROWS):
            u1 = _conv_conf(u0_scr, w, SUBROWS, HALO_C, r0) + cb_ref[...]
            u1_ref[r0:r0 + SUBROWS, :] = u1.astype(BF16)
            _, _, u2 = _ln_fwd(u1, g_ref[...], b_ref[...])
            cz = p_ref[r0:r0 + SUBROWS, 2 * D_C:3 * D_C].astype(F32)
            y_ref[r0:r0 + SUBROWS, :] = (_silu(u2) * _silu(cz)).astype(BF16)

    vec = pl.BlockSpec((1, D_C), lambda i: (0, 0))
    return pl.pallas_call(
        body, name=name, grid=(t // tile,),
        in_specs=[cur, prev, pl.BlockSpec((CONF_K, D_C), lambda i: (0, 0)), vec, vec, vec, HBM_SPEC],
        out_specs=[_window_spec(tile, D_C, D_B + D_A), pl.BlockSpec((tile, D_C), lambda i: (i, 0))],
        out_shape=[jax.ShapeDtypeStruct(ybuf.shape, ybuf.dtype), jax.ShapeDtypeStruct((t, D_C), BF16)],
        scratch_shapes=[pltpu.VMEM((SUB, HALO_C + tile, D_C), F32)],
        input_output_aliases={6: 0},
        compiler_params=_cparams(("parallel",)),
    )(proj, proj, w, cb, g, b, ybuf)


def _c_bwd(dy, proj, u1, w, cb, g, b, dproj, lp, name):
    t = proj.shape[0]
    tile = lp // SHORT_TILES
    tpe = lp // tile
    cur, prev, nxt = _halo_specs(t, 3 * D_C, COL_C, HALO_C, tile)
    dcur, _, dnxt = _halo_specs(t, D_C, D_B + D_A, HALO_C, tile)
    ucur, _, unxt = _halo_specs(t, D_C, 0, HALO_C, tile)
    ext = tile + HALO_C

    def body(dy_ref, dyn_ref, p_ref, ph_ref, pn_ref, u1_ref, u1n_ref, w_ref, cb_ref, g_ref, b_ref, dproj_ref,
             dp_ref, dw_ref, dcb_ref, dg_ref, db_ref, u0_scr, du1_scr, wacc_scr):
        i = pl.program_id(0)
        first = (i % tpe) == 0
        last = (i % tpe) == tpe - 1
        w = w_ref[...]

        @pl.when(i == 0)
        def _():
            dw_ref[...] = jnp.zeros_like(dw_ref)
            dcb_ref[...] = jnp.zeros_like(dcb_ref)
            dg_ref[...] = jnp.zeros_like(dg_ref)
            db_ref[...] = jnp.zeros_like(db_ref)

        u0h = jnp.where(first, 0.0, _f32(ph_ref, 0, D_C) * jax.nn.sigmoid(_f32(ph_ref, D_C, 2 * D_C)))
        _fill_shifted(u0_scr, jnp.concatenate(
            [u0h, _f32(p_ref, 0, D_C) * jax.nn.sigmoid(_f32(p_ref, D_C, 2 * D_C))], axis=0))
        dcb = jnp.zeros((1, D_C), F32)
        dg = jnp.zeros((1, D_C), F32)
        db = jnp.zeros((1, D_C), F32)
        for r0 in range(0, ext, SUBROWS):
            in_tile = r0 < tile
            src, dsrc, usrc, q0 = (p_ref, dy_ref, u1_ref, r0) if in_tile else (pn_ref, dyn_ref, u1n_ref, r0 - tile)
            u1 = usrc[q0:q0 + SUBROWS, :].astype(F32)
            n, rstd, u2 = _ln_fwd(u1, g_ref[...], b_ref[...])
            cz = src[q0:q0 + SUBROWS, 2 * D_C:3 * D_C].astype(F32)
            dyc = dsrc[q0:q0 + SUBROWS, :].astype(F32)
            if not in_tile:
                dyc = jnp.where(last, 0.0, dyc)
            du2 = dyc * _silu(cz) * _dsilu(u2)
            dn = du2 * g_ref[...]
            du1 = rstd * (dn - jnp.mean(dn, axis=-1, keepdims=True) - n * jnp.mean(dn * n, axis=-1, keepdims=True))
            du1_scr[0, r0:r0 + SUBROWS, :] = du1
            if in_tile:
                dp_ref[r0:r0 + SUBROWS, 2 * D_C:3 * D_C] = (dyc * _silu(u2) * _dsilu(cz)).astype(BF16)
                dg = dg + jnp.sum(du2 * n, axis=0, keepdims=True)
                db = db + jnp.sum(du2, axis=0, keepdims=True)
                dcb = dcb + jnp.sum(du1, axis=0, keepdims=True)
        dcb_ref[...] += dcb
        dg_ref[...] += dg
        db_ref[...] += db
        mask = _row_mask(i, tpe, tile)
        _fill_shifted(du1_scr, du1_scr[0])
        for r0 in range(0, tile, SUBROWS):
            acc = w[0:1] * _window(du1_scr, r0 + CONF_K - 1, SUBROWS)
            for k in range(1, CONF_K):
                acc = acc + w[k:k + 1] * _window(du1_scr, r0 + CONF_K - 1 - k, SUBROWS)
            ca = p_ref[r0:r0 + SUBROWS, 0:D_C].astype(F32)
            sg = jax.nn.sigmoid(p_ref[r0:r0 + SUBROWS, D_C:2 * D_C].astype(F32))
            m = mask[r0:r0 + SUBROWS]
            dp_ref[r0:r0 + SUBROWS, 0:D_C] = jnp.where(m, acc * sg, 0.0).astype(BF16)
            dp_ref[r0:r0 + SUBROWS, D_C:2 * D_C] = jnp.where(m, acc * ca * sg * (1.0 - sg), 0.0).astype(BF16)
        for k in range(CONF_K):
            part = jnp.zeros((SUB, D_C), F32)
            for r0 in range(0, tile, SUBROWS):
                prod = du1_scr[0, r0:r0 + SUBROWS, :] * _window(u0_scr, HALO_C + r0 - (CONF_K - 1) + k, SUBROWS)
                for q in range(0, SUBROWS, SUB):
                    part = part + prod[q:q + SUB]
            wacc_scr[k:k + 1, :] = jnp.sum(part, axis=0, keepdims=True)
        dw_ref[...] += wacc_scr[0:CONF_K, :]

    vec = pl.BlockSpec((1, D_C), lambda i: (0, 0))
    wspec = pl.BlockSpec((CONF_K, D_C), lambda i: (0, 0))
    return pl.pallas_call(
        body, name=name, grid=(t // tile,),
        in_specs=[dcur, dnxt, cur, prev, nxt, ucur, unxt, wspec, vec, vec, vec, HBM_SPEC],
        out_specs=[_window_spec(tile, 3 * D_C, COL_C), wspec, vec, vec, vec],
        out_shape=[jax.ShapeDtypeStruct(dproj.shape, dproj.dtype), jax.ShapeDtypeStruct((CONF_K, D_C), F32),
                   jax.ShapeDtypeStruct((1, D_C), F32), jax.ShapeDtypeStruct((1, D_C), F32),
                   jax.ShapeDtypeStruct((1, D_C), F32)],
        scratch_shapes=[pltpu.VMEM((SUB, ext, D_C), F32), pltpu.VMEM((SUB, ext, D_C), F32),
                        pltpu.VMEM((HALO_C, D_C), F32)],
        input_output_aliases={11: 0},
        compiler_params=_cparams(("arbitrary",)),
    )(dy, dy, proj, proj, proj, u1, u1, w, cb, g, b, dproj)


def _split_dot(x, m_bf16, terms):
    acc = None
    rem = x
    for _ in range(terms):
        hi = rem.astype(BF16)
        part = jnp.dot(hi, m_bf16, preferred_element_type=F32)
        acc = part if acc is None else acc + part
        rem = rem - hi.astype(F32)
    return acc


def _split_pieces(x, terms):
    out, rem = [], x
    for _ in range(terms):
        hi = rem.astype(BF16)
        out.append(hi)
        rem = rem - hi.astype(F32)
    return out


def _split_dot_many(xs, terms, m_bf16):
    pieces = [p for x, n in zip(xs, terms) for p in _split_pieces(x, n)]
    prod = jnp.dot(jnp.concatenate(pieces, axis=0), m_bf16, preferred_element_type=F32)
    out, off = [], 0
    for x, n in zip(xs, terms):
        rows = x.shape[0]
        acc = prod[off:off + rows]
        for q in range(1, n):
            acc = acc + prod[off + q * rows:off + (q + 1) * rows]
        out.append(acc)
        off += n * rows
    return out


def _split_dot_left(m_bf16, x, terms):
    cols = x.shape[1]
    prod = jnp.dot(m_bf16, jnp.concatenate(_split_pieces(x, terms), axis=1), preferred_element_type=F32)
    acc = prod[:, 0:cols]
    for q in range(1, terms):
        acc = acc + prod[:, q * cols:(q + 1) * cols]
    return acc


def _tri(rows_ge_cols):
    r = lax.broadcasted_iota(jnp.int32, (CHUNK, CHUNK), 0)
    c = lax.broadcasted_iota(jnp.int32, (CHUNK, CHUNK), 1)
    return (r >= c) if rows_ge_cols else (r <= c)


def _softplus(x):
    return jnp.maximum(x, 0.0) + jnp.log(1.0 + jnp.exp(-jnp.abs(x)))


def _ssd_common(dtraw, dtb, dtb_t, a_log, a_log_t, e_mat, valid_col, valid_row, x_terms=2):
    a = -jnp.exp(a_log)
    lane = lax.broadcasted_iota(jnp.int32, (1, LANE), 1)
    a = jnp.where(lane < N_HEADS, a, 0.0)
    a_t = -jnp.exp(a_log_t)
    dt = jnp.where(valid_col, _softplus(dtraw + dtb), 0.0)
    dt = jnp.where(lane < N_HEADS, dt, 0.0)
    dt_t = jnp.where(valid_row, _softplus(dtraw.T[0:N_HEADS, :] + dtb_t), 0.0)
    ltri = _tri(True).astype(BF16)
    utri = _tri(False).astype(BF16)
    big_a = _split_dot_left(ltri, dt * a, 3)
    big_a_t = _split_dot_many([dt_t * a_t], [3], utri)[0]
    e_a = jnp.exp(big_a)
    d_s = jnp.exp(big_a[CHUNK - 1:CHUNK, :] - big_a)
    if x_terms == 1:
        dt_x, e_a_x, d_s_x = _split_dot_many([dt, e_a, d_s], [1, 2, 1], e_mat)
    else:
        dt_x, e_a_x, d_s_x = (_split_dot(q, e_mat, x_terms) for q in (dt, e_a, d_s))
    cd_x = e_a_x[CHUNK - 1:CHUNK, :]
    return a, dt, big_a, big_a_t, e_a, d_s, dt_x, e_a_x, d_s_x, cd_x


def _decay(big_a, big_a_t, h, transposed):
    col = big_a[:, h:h + 1]
    row = big_a_t[h:h + 1, :]
    if not transposed:
        seg = col - row
        return jnp.where(_tri(True), jnp.exp(jnp.minimum(seg, 0.0)), 0.0)
    seg = row - col
    return jnp.where(_tri(False), jnp.exp(jnp.minimum(seg, 0.0)), 0.0)


NT_DIMS = (((1,), (1,)), ((), ()))
TN_DIMS = (((0,), (0,)), ((), ()))
HBM_SPEC = pl.BlockSpec(memory_space=pl.ANY)


class _Comm:
    def __init__(self, inputs, out_shapes, sem_shapes, copies, aliases=None):
        self.inputs, self.out_shapes, self.sem_shapes, self.copies = inputs, out_shapes, sem_shapes, copies
        self.aliases = aliases or {}

    def start(self, ins, outs, sems):
        local, sends, _ = self.copies(ins, outs, sems, False)
        for cp in local + sends:
            cp.start()

    def wait(self, ins, outs, sems):
        local, sends, recvs = self.copies(ins, outs, sems, True)
        for cp in recvs:
            cp.wait_recv()
        for cp in sends:
            cp.wait_send()
        for cp in local:
            cp.wait()


def _merge_comm(*comms):
    comms = [c for c in comms if c is not None]
    if len(comms) <= 1:
        return comms[0] if comms else None

    def copies(ins, outs, sems, with_recvs):
        local, sends, recvs = [], [], []
        i0 = o0 = s0 = 0
        for c in comms:
            ni, no, ns = len(c.inputs), len(c.out_shapes), len(c.sem_shapes)
            loc, snd, rcv = c.copies(ins[i0:i0 + ni], outs[o0:o0 + no], sems[s0:s0 + ns], with_recvs)
            local, sends, recvs = local + loc, sends + snd, recvs + rcv
            i0, o0, s0 = i0 + ni, o0 + no, s0 + ns
        return local, sends, recvs

    aliases, i0, o0 = {}, 0, 0
    for c in comms:
        aliases.update({i0 + k: o0 + v for k, v in c.aliases.items()})
        i0, o0 = i0 + len(c.inputs), o0 + len(c.out_shapes)
    return _Comm([a for c in comms for a in c.inputs], [s for c in comms for s in c.out_shapes],
                 [s for c in comms for s in c.sem_shapes], copies, aliases)


def _grid_call(body, name, grid, in_specs, out_specs, out_shape, scratch_shapes, operands, comm=None, aliases=None):
    aliases = dict(aliases or {})
    if comm is None:
        return pl.pallas_call(
            body, name=name, grid=grid, in_specs=in_specs, out_specs=out_specs, out_shape=out_shape,
            scratch_shapes=scratch_shapes, input_output_aliases=aliases,
            compiler_params=_cparams(("arbitrary",) * len(grid)))(*operands)
    n_in, n_out, n_scr = len(in_specs), len(out_specs), len(scratch_shapes)
    nci, nco = len(comm.inputs), len(comm.out_shapes)

    def wrapped(*refs):
        ins, cins = refs[:n_in], refs[n_in:n_in + nci]
        o0 = n_in + nci
        outs, couts = refs[o0:o0 + n_out], refs[o0 + n_out:o0 + n_out + nco]
        s0 = o0 + n_out + nco
        scr, csems = refs[s0:s0 + n_scr], refs[s0 + n_scr:]
        first = pl.program_id(0) == 0
        last = pl.program_id(0) == grid[0] - 1
        for k in range(1, len(grid)):
            first = jnp.logical_and(first, pl.program_id(k) == 0)
            last = jnp.logical_and(last, pl.program_id(k) == grid[k] - 1)

        @pl.when(first)
        def _():
            comm.start(cins, couts, csems)

        body(*ins, *outs, *scr)

        @pl.when(last)
        def _():
            comm.wait(cins, couts, csems)

    res = pl.pallas_call(
        wrapped, name=name, grid=grid,
        in_specs=list(in_specs) + [HBM_SPEC] * nci, out_specs=list(out_specs) + [HBM_SPEC] * nco,
        out_shape=list(out_shape) + list(comm.out_shapes),
        scratch_shapes=list(scratch_shapes) + list(comm.sem_shapes),
        input_output_aliases={**aliases, **{n_in + k: n_out + v for k, v in comm.aliases.items()}},
        compiler_params=_cparams(("arbitrary",) * len(grid)))(*operands, *comm.inputs)
    return res


def _ssd_fwd(xbc, proj, dtb, a_log, d_skip_x, norm_g, e_mat, ybuf, lp, name, comm=None):
    t = xbc.shape[0]
    cpe = lp // CHUNK
    nb = t // lp
    dtb_p = jnp.pad(dtb.reshape(1, N_HEADS), ((0, 0), (0, LANE - N_HEADS)))
    alog_p = jnp.pad(a_log.reshape(1, N_HEADS), ((0, 0), (0, LANE - N_HEADS)))
    dtb_t = dtb.reshape(N_HEADS, 1)
    alog_t = a_log.reshape(N_HEADS, 1)

    def body(xbc_ref, bz_ref, dt_ref, dtb_ref, dtbt_ref, al_ref, alt_ref, dx_ref, g_ref, e_ref, ybuf_ref,
             yb_ref, ys_ref, st_ref, s_scr):
        c = pl.program_id(0)

        @pl.when(c == 0)
        def _():
            s_scr[...] = jnp.zeros_like(s_scr)

        rows = lax.broadcasted_iota(jnp.int32, (CHUNK, 1), 0)
        cols = lax.broadcasted_iota(jnp.int32, (1, CHUNK), 1)
        valid_col = jnp.logical_or(c > 0, rows >= PAD)
        valid_row = jnp.logical_or(c > 0, cols >= PAD)
        e_mat = e_ref[...]
        ex = []
        for b in range(nb):
            _, _, big_a, big_a_t, _, _, dt_x, e_a_x, d_s_x, cd_x = _ssd_common(
                dt_ref[b].astype(F32), dtb_ref[...], dtbt_ref[...], al_ref[...], alt_ref[...], e_mat, valid_col, valid_row)
            xs = xbc_ref[b, :, 0:D_B].astype(F32)
            xdt = xs * dt_x
            st_prev = s_scr[b]
            ex.append(dict(big_a=big_a, big_a_t=big_a_t, e_a_x=e_a_x, cd_x=cd_x, xs=xs,
                           bs=xbc_ref[b, :, D_B:D_B + N_GROUPS * N_STATE], cs=xbc_ref[b, :, D_B + N_GROUPS * N_STATE:N_XBC],
                           xdt_b=xdt.astype(BF16), st_prev=st_prev, st_b=st_prev.astype(BF16),
                           u_b=(xdt * d_s_x).astype(BF16), bz=bz_ref[b].astype(F32), y_parts=[], new_st=[]))
        for g in range(N_GROUPS):
            gs = slice(g * N_STATE, (g + 1) * N_STATE)
            gw = slice(g * GROUP_W, (g + 1) * GROUP_W)
            for q in ex:
                cb = lax.dot_general(q["cs"][:, gs], q["bs"][:, gs], NT_DIMS, preferred_element_type=F32)
                q["y_off"] = jnp.dot(q["cs"][:, gs], q["st_b"][:, gw], preferred_element_type=F32)
                q["cb"] = cb
                q["diag"] = []
            for e in range(N_HEADS // N_GROUPS):
                h = g * (N_HEADS // N_GROUPS) + e
                for q in ex:
                    m = (q["cb"] * _decay(q["big_a"], q["big_a_t"], h, False)).astype(BF16)
                    q["diag"].append(jnp.dot(m, q["xdt_b"][:, h * HEAD_DIM:(h + 1) * HEAD_DIM], preferred_element_type=F32))
            for q in ex:
                q["y_parts"].append(jnp.concatenate(q["diag"], axis=1) + q["y_off"] * q["e_a_x"][:, gw])
                upd = lax.dot_general(q["bs"][:, gs], q["u_b"][:, gw], TN_DIMS, preferred_element_type=F32)
                q["new_st"].append(q["st_prev"][:, gw] * q["cd_x"][:, gw] + upd)
        for b, q in enumerate(ex):
            y = jnp.concatenate(q["y_parts"], axis=1) + q["xs"] * dx_ref[...]
            z = y * _silu(q["bz"])
            r = lax.rsqrt(jnp.mean(z * z, axis=-1, keepdims=True) + NORM_EPS)
            q["y"], q["yb"] = y, (z * r * g_ref[...]).astype(BF16)
        for b, q in enumerate(ex):
            st_ref[b] = q["st_b"]
            s_scr[b] = jnp.concatenate(q["new_st"], axis=1)
            ys_ref[b] = q["y"].astype(BF16)
            yb_ref[b] = q["yb"]

    def row(width, col):
        return pl.BlockSpec((nb, CHUNK, width), lambda c: (0, c, col))

    def const(shape):
        return pl.BlockSpec(shape, lambda c: (0,) * len(shape))

    proj3 = proj.reshape(nb, lp, N_INP)
    ybuf, yssd, states, *rest = _grid_call(
        body, name, (cpe,),
        [row(N_XBC, 0), row(D_B, COL_BZ // D_B), row(LANE, COL_DT // LANE),
         const((1, LANE)), const((N_HEADS, 1)), const((1, LANE)), const((N_HEADS, 1)),
         const((1, D_B)), const((1, D_B)), const((LANE, D_B)), HBM_SPEC],
        [row(D_B, 0), row(D_B, 0), pl.BlockSpec((nb, None, N_STATE, D_B), lambda c: (0, c, 0, 0))],
        [jax.ShapeDtypeStruct((nb, lp, ybuf.shape[1]), ybuf.dtype), jax.ShapeDtypeStruct((nb, lp, D_B), BF16),
         jax.ShapeDtypeStruct((nb, cpe, N_STATE, D_B), BF16)],
        [pltpu.VMEM((nb, N_STATE, D_B), F32)],
        (xbc.reshape(nb, lp, N_XBC), proj3, proj3, dtb_p, dtb_t, alog_p, alog_t, d_skip_x, norm_g, e_mat,
         ybuf.reshape(nb, lp, ybuf.shape[1])), comm, aliases={10: 0})
    return (ybuf.reshape(t, -1), yssd.reshape(t, D_B), states, *rest)


def _ssd_bwd(dy, y_ssd, xbc, proj, states, dtb, a_log, d_skip_x, norm_g, e_mat, e_mat_t, lp, name, comm=None):
    t = xbc.shape[0]
    cpe = lp // CHUNK
    nb = t // lp
    hpg = N_HEADS // N_GROUPS
    dtb_p = jnp.pad(dtb.reshape(1, N_HEADS), ((0, 0), (0, LANE - N_HEADS)))
    alog_p = jnp.pad(a_log.reshape(1, N_HEADS), ((0, 0), (0, LANE - N_HEADS)))
    dtb_t = dtb.reshape(N_HEADS, 1)
    alog_t = a_log.reshape(N_HEADS, 1)

    def body(dy_ref, ys_ref, xbc_ref, bz_ref, dt_ref, st_ref, dtb_ref, dtbt_ref, al_ref, alt_ref, dx_ref, g_ref,
             e_ref, et_ref, dxbc_ref, dpw_ref, dg_ref, ddtb_ref, dal_ref, dd_ref, ds_scr):
        @pl.when(pl.program_id(0) == 0)
        def _():
            ds_scr[...] = jnp.zeros_like(ds_scr)
            dg_ref[...] = jnp.zeros_like(dg_ref)
            ddtb_ref[...] = jnp.zeros_like(ddtb_ref)
            dal_ref[...] = jnp.zeros_like(dal_ref)
            dd_ref[...] = jnp.zeros_like(dd_ref)

        stores, sums = [], {}
        chains = [one_example(dy_ref.at[b], ys_ref.at[b], xbc_ref.at[b], bz_ref.at[b], dt_ref.at[b], st_ref.at[b], dtb_ref,
                              dtbt_ref, al_ref, alt_ref, dx_ref, g_ref, e_ref, et_ref, dxbc_ref.at[b], dpw_ref.at[b],
                              ds_scr.at[b], stores, sums) for b in range(nb)]
        live = list(chains)
        while live:
            live = [ch for ch in live if next(ch, "done") != "done"]
        for ref, idx, val in stores:
            ref[idx] = val
        for ref, name_ in ((dg_ref, "dg"), (ddtb_ref, "ddtb"), (dal_ref, "dal"), (dd_ref, "dd")):
            total = sums[name_][0]
            for part in sums[name_][1:]:
                total = total + part
            ref[...] += total

    def one_example(dy_ref, ys_ref, xbc_ref, bz_ref, dt_ref, st_ref, dtb_ref, dtbt_ref, al_ref, alt_ref, dx_ref, g_ref,
                    e_ref, et_ref, dxbc_ref, dpw_ref, ds_scr, stores, sums):
        everything = (slice(None), slice(None))
        cc = cpe - 1 - pl.program_id(0)
        rows = lax.broadcasted_iota(jnp.int32, (CHUNK, 1), 0)
        cols = lax.broadcasted_iota(jnp.int32, (1, CHUNK), 1)
        valid_col = jnp.logical_or(cc > 0, rows >= PAD)
        valid_row = jnp.logical_or(cc > 0, cols >= PAD)
        e_mat = e_ref[...]
        e_mat_t = et_ref[...]
        dtraw = dt_ref[...].astype(F32)
        a, dt, big_a, big_a_t, e_a, d_s, dt_x, e_a_x, d_s_x, cd_x = _ssd_common(
            dtraw, dtb_ref[...], dtbt_ref[...], al_ref[...], alt_ref[...], e_mat, valid_col, valid_row, x_terms=1)
        xs = xbc_ref[:, 0:D_B].astype(F32)
        bs = xbc_ref[:, D_B:D_B + N_GROUPS * N_STATE]
        cs = xbc_ref[:, D_B + N_GROUPS * N_STATE:N_XBC]
        xdt = xs * dt_x
        xdt_b = xdt.astype(BF16)
        st_b = st_ref[...]
        dst = ds_scr[...]
        dst_b = dst.astype(BF16)

        ys = ys_ref[...].astype(F32)
        bz = bz_ref[...].astype(F32)
        sil = _silu(bz)
        z = ys * sil
        dz, dgt = _rms_bwd_math(z, g_ref[...], dy_ref[...].astype(F32))
        sums.setdefault("dg", []).append(jnp.sum(dgt, axis=0, keepdims=True))
        stores.append((dpw_ref, (slice(None), slice(0, D_B)), (dz * ys * _dsilu(bz)).astype(BF16)))
        dys = dz * sil

        dd_lane = jnp.sum(dys * xs, axis=0, keepdims=True)
        dxs = dys * dx_ref[...]
        w_x = dys * e_a_x
        w_b = w_x.astype(BF16)
        dys_b = dys.astype(BF16)
        u_b = (xdt * d_s_x).astype(BF16)
        dxdt_parts, dbs_parts, dcs_parts, off_parts, g1_parts = [], [], [], [], []
        da_diag = jnp.zeros((CHUNK, LANE), F32)
        lane = lax.broadcasted_iota(jnp.int32, (1, LANE), 1)
        for g in range(N_GROUPS):
            gs = slice(g * N_STATE, (g + 1) * N_STATE)
            gw = slice(g * GROUP_W, (g + 1) * GROUP_W)
            cs_g, bs_g = cs[:, gs], bs[:, gs]
            dcs = lax.dot_general(w_b[:, gw], st_b[:, gw], NT_DIMS, preferred_element_type=F32)
            y_off = jnp.dot(cs_g, st_b[:, gw], preferred_element_type=F32)
            off_parts.append(y_off)
            dst_new = lax.dot_general(cs_g, w_b[:, gw], TN_DIMS, preferred_element_type=F32)
            g1 = jnp.dot(bs_g, dst_b[:, gw], preferred_element_type=F32)
            g1_parts.append(g1)
            dbs = lax.dot_general(u_b[:, gw], dst_b[:, gw], NT_DIMS, preferred_element_type=F32)
            cb = lax.dot_general(cs_g, bs_g, NT_DIMS, preferred_element_type=F32)
            cbt = lax.dot_general(bs_g, cs_g, NT_DIMS, preferred_element_type=F32)
            dcb = jnp.zeros((CHUNK, CHUNK), F32)
            dcbt = jnp.zeros((CHUNK, CHUNK), F32)
            dxdt_h = []
            for e in range(hpg):
                h = g * hpg + e
                hs = slice(h * HEAD_DIM, (h + 1) * HEAD_DIM)
                dec = _decay(big_a, big_a_t, h, False)
                dect = _decay(big_a, big_a_t, h, True)
                m = cb * dec
                mt = cbt * dect
                dxdt_h.append(jnp.dot(mt.astype(BF16), dys_b[:, hs], preferred_element_type=F32))
                dm = lax.dot_general(dys_b[:, hs], xdt_b[:, hs], NT_DIMS, preferred_element_type=F32)
                dmt = lax.dot_general(xdt_b[:, hs], dys_b[:, hs], NT_DIMS, preferred_element_type=F32)
                dcb = dcb + dm * dec
                dcbt = dcbt + dmt * dect
                da_h = jnp.sum(dm * m - dmt * mt, axis=1, keepdims=True)
                da_diag = da_diag + jnp.where(lane == h, da_h, 0.0)
                yield
            dcs = dcs + jnp.dot(dcb.astype(BF16), bs_g, preferred_element_type=F32)
            dbs = dbs + jnp.dot(dcbt.astype(BF16), cs_g, preferred_element_type=F32)
            dxdt_parts.append(jnp.concatenate(dxdt_h, axis=1) + g1 * d_s_x[:, gw])
            dbs_parts.append(dbs)
            dcs_parts.append(dcs)
            stores.append((ds_scr, (slice(None), gw), dst[:, gw] * cd_x[:, gw] + dst_new))
            yield
        dxdt = jnp.concatenate(dxdt_parts, axis=1)
        y_off = jnp.concatenate(off_parts, axis=1)
        g1 = jnp.concatenate(g1_parts, axis=1)
        dcd_lane = jnp.sum(dst * st_b.astype(F32), axis=0, keepdims=True)
        vecs = jnp.concatenate([jnp.broadcast_to(dcd_lane, (SUB, D_B)), jnp.broadcast_to(dd_lane, (SUB, D_B))], axis=0)
        dds, da_off, ddt_x, vec_sums = _split_dot_many(
            [g1 * xdt, w_x * y_off, dxdt * xs, vecs], [1, 1, 1, 1], e_mat_t)
        dcd = vec_sums[0:1]
        sums.setdefault("dd", []).append(vec_sums[SUB:SUB + 1])
        yield
        t_ds = dds * d_s
        d_a = da_diag + da_off - t_ds
        last_row = jnp.sum(t_ds, axis=0, keepdims=True) + dcd * e_a[CHUNK - 1:CHUNK, :]
        d_a = d_a + jnp.where(rows == CHUNK - 1, last_row, 0.0)
        dda = _split_dot_left(_tri(False).astype(BF16), d_a, 3)
        ddt = dda * a + ddt_x
        sums.setdefault("dal", []).append(jnp.sum(dda * dt * a, axis=0, keepdims=True))
        ddtraw = jnp.where(valid_col, ddt * jax.nn.sigmoid(dtraw + dtb_ref[...]), 0.0)
        ddtraw = jnp.where(lane < N_HEADS, ddtraw, 0.0)
        sums.setdefault("ddtb", []).append(jnp.sum(ddtraw, axis=0, keepdims=True))
        stores.append((dpw_ref, (slice(None), slice(D_B, D_B + LANE)), ddtraw.astype(BF16)))
        dxs = dxs + dxdt * dt_x
        dxbc = jnp.concatenate([dxs] + dbs_parts + dcs_parts, axis=1)
        stores.append((dxbc_ref, everything, jnp.where(valid_col, dxbc, 0.0).astype(BF16)))

    def row(width, col):
        return pl.BlockSpec((nb, CHUNK, width), lambda c: (0, cpe - 1 - c, col))

    def const(shape):
        return pl.BlockSpec(shape, lambda c: (0,) * len(shape))

    proj3 = proj.reshape(nb, lp, N_INP)
    dxbc, dproj, *rest = _grid_call(
        body, name, (cpe,),
        [row(D_B, 0), row(D_B, 0), row(N_XBC, 0), row(D_B, COL_BZ // D_B), row(LANE, COL_DT // LANE),
         pl.BlockSpec((nb, None, N_STATE, D_B), lambda c: (0, cpe - 1 - c, 0, 0)),
         const((1, LANE)), const((N_HEADS, 1)), const((1, LANE)), const((N_HEADS, 1)),
         const((1, D_B)), const((1, D_B)), const((LANE, D_B)), const((D_B, LANE))],
        [row(N_XBC, 0),
         pl.BlockSpec((pl.Element(nb), pl.Element(CHUNK), pl.Element(D_B + LANE)),
                      lambda c: (0, (cpe - 1 - c) * CHUNK, COL_BZ)),
         const((1, D_B)), const((1, LANE)), const((1, LANE)), const((1, LANE))],
        [jax.ShapeDtypeStruct((nb, lp, N_XBC), BF16), jax.ShapeDtypeStruct((nb, lp, N_INP), BF16),
         jax.ShapeDtypeStruct((1, D_B), F32),
         jax.ShapeDtypeStruct((1, LANE), F32), jax.ShapeDtypeStruct((1, LANE), F32),
         jax.ShapeDtypeStruct((1, LANE), F32)],
        [pltpu.VMEM((nb, N_STATE, D_B), F32)],
        (dy.reshape(nb, lp, -1), y_ssd.reshape(nb, lp, D_B), xbc.reshape(nb, lp, N_XBC), proj3, proj3, states,
         dtb_p, dtb_t, alog_p, alog_t, d_skip_x, norm_g, e_mat, e_mat_t), comm)
    return (dxbc.reshape(t, N_XBC), dproj.reshape(t, N_INP), *rest)


HBM_SPEC = pl.BlockSpec(memory_space=pl.ANY)


def _mesh_pos():
    return lax.axis_index("x"), lax.axis_index("y"), lax.axis_index("c")


def _remote(src, dst, send_sem, recv_sem, to):
    return pltpu.make_async_remote_copy(src_ref=src, dst_ref=dst, send_sem=send_sem, recv_sem=recv_sem,
                                        device_id=to, device_id_type=MESH)


def _slot_plain(d):
    return d


def _slot_mix_rows(d):
    return jnp.where(d < 2, d + 4, jnp.where(d < 6, d - 2, d))


def _ag_ici(pieces, slots):
    n = len(pieces)

    def copies(ins, outs, sems, with_recvs):
        send_sems, recv_sems, local_sems = sems
        x, y, c = _mesh_pos()
        local, sends, recvs = [], [], []
        for a in range(n):
            mine = outs[a].at[slots[a](4 * x + 2 * y + c)]
            local.append(pltpu.make_async_copy(ins[a], mine, local_sems.at[a]))
            for k, (px, py) in enumerate([(1 - x, y), (x, 1 - y), (1 - x, 1 - y)]):
                sends.append(_remote(ins[a], mine, send_sems.at[a, k], recv_sems.at[a, k], (px, py, c)))
                if with_recvs:
                    theirs = outs[a].at[slots[a](4 * px + 2 * py + c)]
                    recvs.append(_remote(ins[a], theirs, send_sems.at[a, k], recv_sems.at[a, k], (px, py, c)))
        return local, sends, recvs

    return _Comm(pieces, [jax.ShapeDtypeStruct((N_DEV,) + p.shape, p.dtype) for p in pieces],
                 [pltpu.SemaphoreType.DMA((n, 3)), pltpu.SemaphoreType.DMA((n, 3)), pltpu.SemaphoreType.DMA((n,))], copies)


def _ag_d2d(bufs, slots):
    n = len(bufs)

    def copies(ins, outs, sems, with_recvs):
        send_sems, recv_sems = sems
        x, y, c = _mesh_pos()
        chips = [(x, y), (1 - x, y), (x, 1 - y), (1 - x, 1 - y)]
        sends, recvs = [], []
        for a in range(n):
            for k, (px, py) in enumerate(chips):
                held = slots[a](4 * px + 2 * py + c)
                sends.append(_remote(ins[a].at[held], outs[a].at[held], send_sems.at[a, k], recv_sems.at[a, k], (x, y, 1 - c)))
                if with_recvs:
                    got = slots[a](4 * px + 2 * py + 1 - c)
                    recvs.append(_remote(ins[a].at[got], outs[a].at[got], send_sems.at[a, k], recv_sems.at[a, k], (x, y, 1 - c)))
        return [], sends, recvs

    return _Comm(bufs, [jax.ShapeDtypeStruct(b.shape, b.dtype) for b in bufs],
                 [pltpu.SemaphoreType.DMA((n, 4)), pltpu.SemaphoreType.DMA((n, 4))], copies,
                 aliases={a: a for a in range(n)})


def _rs_d2d(blocks, slots):
    n = len(blocks)

    def copies(ins, outs, sems, with_recvs):
        send_sems, recv_sems = sems
        x, y, c = _mesh_pos()
        sends, recvs = [], []
        for a in range(n):
            for j in range(4):
                src = ins[a].at[slots[a](2 * j + 1 - c)]
                sends.append(_remote(src, outs[a].at[j], send_sems.at[a, j], recv_sems.at[a, j], (x, y, 1 - c)))
                if with_recvs:
                    recvs.append(_remote(src, outs[a].at[j], send_sems.at[a, j], recv_sems.at[a, j], (x, y, 1 - c)))
        return [], sends, recvs

    return _Comm(blocks, [jax.ShapeDtypeStruct((4,) + b.shape[1:], b.dtype) for b in blocks],
                 [pltpu.SemaphoreType.DMA((n, 4)), pltpu.SemaphoreType.DMA((n, 4))], copies)


def _rs_ici(blocks):
    n = len(blocks)

    def copies(ins, outs, sems, with_recvs):
        send_sems, recv_sems = sems
        x, y, c = _mesh_pos()
        sends, recvs = [], []
        for a in range(n):
            for k, (px, py) in enumerate([(1 - x, y), (x, 1 - y), (1 - x, 1 - y)]):
                src = ins[a].at[2 * px + py]
                sends.append(_remote(src, outs[a].at[k], send_sems.at[a, k], recv_sems.at[a, k], (px, py, c)))
                if with_recvs:
                    recvs.append(_remote(src, outs[a].at[k], send_sems.at[a, k], recv_sems.at[a, k], (px, py, c)))
        return [], sends, recvs

    return _Comm(blocks, [jax.ShapeDtypeStruct((3,) + b.shape[1:], b.dtype) for b in blocks],
                 [pltpu.SemaphoreType.DMA((n, 3)), pltpu.SemaphoreType.DMA((n, 3))], copies)


def _run_comm(comm, name):
    n_in, n_out = len(comm.inputs), len(comm.out_shapes)

    def body(*refs):
        ins, outs, sems = refs[:n_in], refs[n_in:n_in + n_out], refs[n_in + n_out:]
        comm.start(ins, outs, sems)
        comm.wait(ins, outs, sems)

    return pl.pallas_call(
        body, name=name, in_specs=[HBM_SPEC] * n_in, out_specs=[HBM_SPEC] * n_out, out_shape=comm.out_shapes,
        scratch_shapes=comm.sem_shapes, input_output_aliases=comm.aliases,
    )(*comm.inputs)


W_ROWS = 784


def _w_segments():
    per = N_IN // N_DEV
    out = []
    for d in range(N_DEV):
        lo, hi = per * d, per * (d + 1)
        for a, b, start in COL_MAP:
            s, e = max(lo, a), min(hi, b)
            if s < e:
                out.append((d, s - lo, e - s, s - a + start))
    return out


def _w_gather_rows(g, name):
    tk = D_MODEL // 2
    u32 = jnp.uint32

    def body(g_ref, o_ref, scr):
        for d in range(N_DEV):
            x32 = pltpu.bitcast(g_ref[d], u32)
            for dd, src, rows, dst in _w_segments():
                if dd == d:
                    scr[dst // 2:(dst + rows) // 2, :] = x32[src // 2:(src + rows) // 2]
        scr[(COL_DT + N_HEADS) // 2:COL_XBC // 2, :] = jnp.zeros(((COL_XBC - COL_DT - N_HEADS) // 2, tk), u32)
        o_ref[...] = pltpu.bitcast(scr[...], BF16)

    return pl.pallas_call(
        body, name=name, grid=(D_MODEL // tk,),
        in_specs=[pl.BlockSpec((N_DEV, W_ROWS, tk), lambda j: (0, 0, j))],
        out_specs=pl.BlockSpec((N_INP, tk), lambda j: (0, j)),
        out_shape=jax.ShapeDtypeStruct((N_INP, D_MODEL), BF16),
        scratch_shapes=[pltpu.VMEM((N_INP // 2, tk), u32)],
        compiler_params=_cparams(("parallel",)),
    )(g)


def _w_split_rows(dwt, name):
    tk = D_MODEL // 4
    per = N_IN // N_DEV

    def body(w_ref, o_ref):
        for d, dst, rows, src in _w_segments():
            o_ref[d, dst // 2:(dst + rows) // 2, :] = pltpu.bitcast(w_ref[src:src + rows, :].astype(BF16), F32)
        for d in range(N_DEV):
            o_ref[d, per // 2:W_ROWS // 2, :] = jnp.zeros(((W_ROWS - per) // 2, tk), F32)

    return pl.pallas_call(
        body, name=name, grid=(D_MODEL // tk,),
        in_specs=[pl.BlockSpec((N_INP, tk), lambda j: (0, j))],
        out_specs=pl.BlockSpec((N_DEV, W_ROWS // 2, tk), lambda j: (0, 0, j)),
        out_shape=jax.ShapeDtypeStruct((N_DEV, W_ROWS // 2, D_MODEL), F32),
        compiler_params=_cparams(("parallel",)),
    )(dwt)


def _rs_pair_sum(g, ra, own_slots, name):
    packed = g.dtype == F32
    _, rows, cols = g.shape
    out_rows = 2 * rows if packed else rows

    def body(s_ref, g_ref, ra_ref, o_ref):
        a, b = g_ref[...], ra_ref[...]
        if packed:
            a, b = pltpu.bitcast(a, BF16), pltpu.bitcast(b, BF16)
        o_ref[...] = (a.astype(F32) + b.astype(F32)).astype(BF16)

    return pl.pallas_call(
        body, name=name,
        grid_spec=pltpu.PrefetchScalarGridSpec(
            num_scalar_prefetch=1, grid=(4,),
            in_specs=[pl.BlockSpec((None, rows, cols), lambda j, s: (s[j], 0, 0)),
                      pl.BlockSpec((None, rows, cols), lambda j, s: (j, 0, 0))],
            out_specs=pl.BlockSpec((None, out_rows, cols), lambda j, s: (j, 0, 0))),
        out_shape=jax.ShapeDtypeStruct((4, out_rows, cols), BF16),
        compiler_params=_cparams(("parallel",)),
    )(own_slots, g, ra)


def _rs_final_sum(h, rb, chip_idx, name):
    _, rows, cols = h.shape

    def body(j_ref, h_ref, rb_ref, o_ref):
        o_ref[...] = ((h_ref[...].astype(F32) + rb_ref[0].astype(F32)) + rb_ref[1].astype(F32)) + rb_ref[2].astype(F32)

    return pl.pallas_call(
        body, name=name,
        grid_spec=pltpu.PrefetchScalarGridSpec(
            num_scalar_prefetch=1, grid=(1,),
            in_specs=[pl.BlockSpec((None, rows, cols), lambda i, j: (j[0], 0, 0)),
                      pl.BlockSpec((3, rows, cols), lambda i, j: (0, 0, 0))],
            out_specs=pl.BlockSpec((rows, cols), lambda i, j: (0, 0))),
        out_shape=jax.ShapeDtypeStruct((rows, cols), F32),
        compiler_params=_cparams(("arbitrary",)),
    )(chip_idx, h, rb)


def _adamw_math(w, g, m, v):
    m = ADAM_B1 * m + (1.0 - ADAM_B1) * g
    v = ADAM_B2 * v + (1.0 - ADAM_B2) * (g * g)
    m_hat = m / (1.0 - ADAM_B1 ** ADAM_STEP)
    v_hat = v / (1.0 - ADAM_B2 ** ADAM_STEP)
    delta = -ADAM_LR * (m_hat / (jnp.sqrt(v_hat) + ADAM_EPS) + ADAM_WD * w)
    return delta, m, v


def _adamw_rows(g, w, m, v, tr, name):
    layers, rows, cols = w.shape

    def body(g_ref, w_ref, m_ref, v_ref, d_out, m_out, v_out):
        delta, m_new, v_new = _adamw_math(w_ref[...], g_ref[...], m_ref[...], v_ref[...])
        d_out[...] = delta
        m_out[...] = m_new
        v_out[...] = v_new

    blk = pl.BlockSpec((None, tr, cols), lambda a, r: (a, r, 0))
    return pl.pallas_call(
        body, name=name, grid=(layers, rows // tr),
        in_specs=[blk] * 4, out_specs=[blk] * 3,
        out_shape=[jax.ShapeDtypeStruct((layers, rows, cols), F32)] * 3,
        compiler_params=_cparams(("parallel", "parallel")),
    )(g, w, m, v)


def _adamw_cols(g_t, w, m, v, name):
    layers, k, cols = w.shape
    tr = 56
    assert g_t.shape[1] % tr == 0
    views = [jnp.transpose(a, (2, 0, 1)) for a in (w, m, v)]

    def body(g_ref, w_ref, m_ref, v_ref, g_out, d_out, m_out, v_out):
        for a in range(layers):
            g = g_ref[a]
            delta, m_new, v_new = _adamw_math(w_ref[:, a, :], g, m_ref[:, a, :], v_ref[:, a, :])
            g_out[:, a, :] = g
            d_out[:, a, :] = delta
            m_out[:, a, :] = m_new
            v_out[:, a, :] = v_new

    col = pl.BlockSpec((tr, layers, k), lambda r: (r, 0, 0))
    outs = pl.pallas_call(
        body, name=name, grid=(pl.cdiv(cols, tr),),
        in_specs=[pl.BlockSpec((layers, tr, k), lambda r: (0, r, 0)), col, col, col], out_specs=[col] * 4,
        out_shape=[jax.ShapeDtypeStruct((cols, layers, k), F32)] * 4,
        compiler_params=_cparams(("parallel",)),
    )(g_t, *views)
    return [jnp.transpose(o, (1, 2, 0)) for o in outs]


def _sum_devices(parts, name):
    _, p, _ = parts.shape

    def body(x_ref, o_ref):
        acc = x_ref[0]
        for d in range(1, N_DEV):
            acc = acc + x_ref[d]
        o_ref[...] = acc

    return pl.pallas_call(
        body, name=name, grid=(1,),
        in_specs=[pl.BlockSpec((N_DEV, p, LANE), lambda i: (0, 0, 0))],
        out_specs=pl.BlockSpec((p, LANE), lambda i: (0, 0)),
        out_shape=jax.ShapeDtypeStruct((p, LANE), F32),
        compiler_params=_cparams(("arbitrary",)),
    )(parts)


def _adamw_small(g, w, m, v, name):
    p = g.shape[0]

    def body(g_ref, w_ref, m_ref, v_ref, d_out, m_out, v_out):
        delta, m_new, v_new = _adamw_math(w_ref[...], g_ref[...], m_ref[...], v_ref[...])
        d_out[...] = delta
        m_out[...] = m_new
        v_out[...] = v_new

    spec = pl.BlockSpec((p, LANE), lambda i: (0, 0))
    return pl.pallas_call(
        body, name=name, grid=(1,),
        in_specs=[spec] * 4, out_specs=[spec] * 3,
        out_shape=[jax.ShapeDtypeStruct((p, LANE), F32)] * 3,
        compiler_params=_cparams(("arbitrary",)),
    )(g, w, m, v)


PACK_ALIGN = SUB * LANE

SMALL_PARAMS = (
    ("meta", (N_META, D_MODEL), 1),
    ("pre_g", (DEPTH, D_MODEL), None),
    ("post_g", (DEPTH, D_MODEL), None),
    ("conv_a_w", (DEPTH, CONV_A_K, D_A), 2),
    ("ssm_conv_w", (DEPTH, SSM_K, N_XBC), 2),
    ("ssm_conv_b", (DEPTH, N_XBC), None),
    ("dt_bias", (DEPTH, N_HEADS), None),
    ("a_log", (DEPTH, N_HEADS), None),
    ("d_skip", (DEPTH, N_HEADS), None),
    ("ssm_norm_g", (DEPTH, D_B), None),
    ("conf_conv_w", (DEPTH, CONF_K, D_C), 2),
    ("conf_conv_b", (DEPTH, D_C), None),
    ("conf_ln_g", (DEPTH, D_C), None),
    ("conf_ln_b", (DEPTH, D_C), None),
)


def _local_shape(shape, axis):
    if axis is None:
        return shape
    return tuple(s // N_DEV if k == axis else s for k, s in enumerate(shape))


def _pack(arrays):
    flat = []
    for a in arrays:
        v = a.reshape(-1).astype(F32)
        flat.append(v)
        if v.shape[0] % PACK_ALIGN:
            flat.append(jnp.zeros(((-v.shape[0]) % PACK_ALIGN,), F32))
    return jnp.concatenate(flat).reshape(-1, LANE)


def _unpack(buf, shapes):
    flat = buf.reshape(-1)
    out, off = [], 0
    for s in shapes:
        size = 1
        for k in s:
            size *= k
        out.append(flat[off:off + size].reshape(s))
        off += size + (-size) % PACK_ALIGN
    return out


def kernel(x, meta, pre_g, post_g, w_in, w_out, conv_a_w, ssm_conv_w, ssm_conv_b, dt_bias, a_log, d_skip, ssm_norm_g, conf_conv_w, conf_conv_b, conf_ln_g, conf_ln_b, loss_target, m_meta, m_pre_g, m_post_g, m_w_in, m_w_out, m_conv_a_w, m_ssm_conv_w, m_ssm_conv_b, m_dt_bias, m_a_log, m_d_skip, m_ssm_norm_g, m_conf_conv_w, m_conf_conv_b, m_conf_ln_g, m_conf_ln_b, v_meta, v_pre_g, v_post_g, v_w_in, v_w_out, v_conv_a_w, v_ssm_conv_w, v_ssm_conv_b, v_dt_bias, v_a_log, v_d_skip, v_ssm_norm_g, v_conf_conv_w, v_conf_conv_b, v_conf_ln_g, v_conf_ln_b):
    weights = dict(meta=meta, pre_g=pre_g, post_g=post_g, conv_a_w=conv_a_w, ssm_conv_w=ssm_conv_w, ssm_conv_b=ssm_conv_b,
                   dt_bias=dt_bias, a_log=a_log, d_skip=d_skip, ssm_norm_g=ssm_norm_g, conf_conv_w=conf_conv_w,
                   conf_conv_b=conf_conv_b, conf_ln_g=conf_ln_g, conf_ln_b=conf_ln_b)
    mom1 = dict(meta=m_meta, pre_g=m_pre_g, post_g=m_post_g, conv_a_w=m_conv_a_w, ssm_conv_w=m_ssm_conv_w,
                ssm_conv_b=m_ssm_conv_b, dt_bias=m_dt_bias, a_log=m_a_log, d_skip=m_d_skip, ssm_norm_g=m_ssm_norm_g,
                conf_conv_w=m_conf_conv_w, conf_conv_b=m_conf_conv_b, conf_ln_g=m_conf_ln_g, conf_ln_b=m_conf_ln_b)
    mom2 = dict(meta=v_meta, pre_g=v_pre_g, post_g=v_post_g, conv_a_w=v_conv_a_w, ssm_conv_w=v_ssm_conv_w,
                ssm_conv_b=v_ssm_conv_b, dt_bias=v_dt_bias, a_log=v_a_log, d_skip=v_d_skip, ssm_norm_g=v_ssm_norm_g,
                conf_conv_w=v_conf_conv_w, conf_conv_b=v_conf_conv_b, conf_ln_g=v_conf_ln_g, conf_ln_b=v_conf_ln_b)
    nb, seq, d = x.shape
    lp = PAD + N_META + seq
    t = nb * lp
    assert lp % (SHORT_TILES * HALO_C) == 0 and t % (3 * LANE) == 0 and d == D_MODEL
    rt = lp // 3
    xi, yi, ci = _mesh_pos()
    dev = 4 * xi + 2 * yi + ci
    ci32 = ci.astype(jnp.int32)
    chip_idx = (2 * xi + yi).astype(jnp.int32).reshape(1)
    own_plain = jnp.stack([2 * j + ci32 for j in range(4)])
    own_mix = jnp.stack([_slot_mix_rows(2 * j + ci32) for j in range(4)]).astype(jnp.int32)
    n_in_loc = N_IN // N_DEV
    n_out_loc = 2 * D_MODEL // N_DEV
    slots = (_slot_plain, _slot_mix_rows)

    sharded_small = [n for n, _, ax in SMALL_PARAMS if ax is not None]
    small_shapes = {n: s for n, s, _ in SMALL_PARAMS}
    small_axis = {n: ax for n, _, ax in SMALL_PARAMS}
    sw_pack = _pack([weights[n] for n in sharded_small])
    wt_loc = jnp.pad(jnp.swapaxes(w_in, 1, 2).astype(BF16), ((0, 0), (0, W_ROWS - n_in_loc), (0, 0)))
    wo_loc = w_out.astype(BF16)
    plain2 = (_slot_plain, _slot_plain)
    head = _run_comm(_ag_ici([wt_loc[0], sw_pack], plain2), "ag_ici")
    g_in_t, sw_g = _run_comm(_ag_d2d(head, plain2), "ag_d2d")

    def gathered(bufs):
        return _w_gather_rows(bufs[0], "w_gather_rows"), bufs[1].reshape(2 * D_MODEL, D_MODEL)

    full = dict(weights)
    per_dev = [_unpack(sw_g[k], [_local_shape(small_shapes[n], small_axis[n]) for n in sharded_small]) for k in range(N_DEV)]
    for q, n in enumerate(sharded_small):
        full[n] = jnp.concatenate([per_dev[k][q] for k in range(N_DEV)], axis=small_axis[n])

    e_mat = (lax.broadcasted_iota(jnp.int32, (LANE, D_B), 0) == lax.broadcasted_iota(jnp.int32, (LANE, D_B), 1) // HEAD_DIM)
    e_mat = e_mat.astype(BF16)
    e_mat_t = e_mat.T

    front = jnp.concatenate([jnp.zeros((PAD, d), F32), full["meta"]], axis=0)
    h = jnp.concatenate([jnp.concatenate([front, x[b]], axis=0) for b in range(nb)], axis=0)
    saved = []
    w_t, w_o = _w_gather_rows(g_in_t, "w_gather_rows"), None
    hn = _rms_fwd(h, pre_g[0].reshape(1, -1), rt, "rms_fwd")
    for i in range(DEPTH):
        row = lambda a: a[i].reshape(1, -1)
        if i == 0:
            proj, wo_buf = _mm(hn, w_t, "nt", BF16, t // 3, MM_TN, D_MODEL, "mm_proj", cols_outer=True,
                               comm=_ag_ici([wo_loc[0]], (_slot_mix_rows,)))
        else:
            proj = _mm(hn, w_t, "nt", BF16, t // 3, MM_TN, D_MODEL, "mm_proj", cols_outer=True)
        ycat = _a_fwd(proj, full["conv_a_w"][i], lp, "a_fwd")
        xbc = _xbc_fwd(proj, full["ssm_conv_w"][i], row(ssm_conv_b), lp, "xbc_fwd")
        ycat, u1 = _c_fwd(proj, full["conf_conv_w"][i], row(conf_conv_b), row(conf_ln_g), row(conf_ln_b), ycat, lp, "c_fwd")
        d_skip_x = jnp.repeat(d_skip[i], HEAD_DIM).reshape(1, D_B)
        nxt = _ag_ici([wt_loc[i + 1], wo_loc[i + 1]], slots) if i + 1 < DEPTH else None
        if i == 0:
            nxt = _merge_comm(nxt, _ag_d2d([wo_buf], (_slot_mix_rows,)))
        ycat, yssd, states, *bufs = _ssd_fwd(xbc, proj, dt_bias[i], a_log[i], d_skip_x, row(ssm_norm_g), e_mat, ycat, lp,
                                             "ssd_fwd", comm=nxt)
        if i == 0:
            w_o = bufs.pop().reshape(2 * D_MODEL, D_MODEL)
        if bufs:
            m, *bufs = _mm(ycat, w_o, "nn", F32, t // 6, D_MODEL, 2 * D_MODEL, "mm_out", comm=_ag_d2d(bufs, slots))
        else:
            m = _mm(ycat, w_o, "nn", F32, t // 6, D_MODEL, 2 * D_MODEL, "mm_out")
        saved.append((h, hn, proj, ycat, yssd, states, xbc, m, d_skip_x, w_t, w_o, u1))
        if i + 1 < DEPTH:
            h, hn = _post_rms_fwd(h, m, row(post_g), pre_g[i + 1].reshape(1, -1), rt, "post_rms_fwd")
        else:
            h = _post_fwd(h, m, row(post_g), rt, "post_fwd")
        if bufs:
            w_t, w_o = gathered(bufs)

    dh, loss_blk = _loss_kernel(h, loss_target.reshape(nb * seq, d), lp, "loss")

    grads = {n: [None] * DEPTH for n, _, _ in SMALL_PARAMS if n != "meta"}
    gt_in, gr_out = [None] * DEPTH, [None] * DEPTH
    dmeta = None
    pending = None
    dm = None

    def rs_pair_sums(blocks, ras):
        return [_rs_pair_sum(blocks[0], ras[0], own_plain, "rs_pair_sum_in"),
                _rs_pair_sum(blocks[1], ras[1], own_mix, "rs_pair_sum_out")]

    def rs_finish(layer, hs, rbs):
        gt_in[layer] = _rs_final_sum(hs[0], rbs[0], chip_idx, "rs_final_sum_in")
        gr_out[layer] = _rs_final_sum(hs[1], rbs[1], chip_idx, "rs_final_sum_out")

    for i in reversed(range(DEPTH)):
        row = lambda a: a[i].reshape(1, -1)
        h_i, hn, proj, ycat, yssd, states, xbc, m, d_skip_x, w_t, w_o, u1 = saved[i]
        if dm is None:
            dm, grads["post_g"][i] = _post_bwd(dh, m, row(post_g), rt, "post_bwd")
        if pending is not None:
            dy, *ras = _mm(dm, w_o, "nt", BF16, t // 3, D_MODEL, D_MODEL, "mm_dy", cols_outer=True,
                           comm=_rs_d2d(list(pending), slots))
            hs = rs_pair_sums(pending, ras)
        else:
            dy = _mm(dm, w_o, "nt", BF16, t // 3, D_MODEL, D_MODEL, "mm_dy", cols_outer=True)
            hs = None
        dw_out = _mm(ycat, dm, "tn", BF16, D_MODEL, D_MODEL, t, "mm_dwout")
        dxbc, dproj, grads["ssm_norm_g"][i], ddtb, dal, dds, *rbs = _ssd_bwd(
            dy, yssd, xbc, proj, states, dt_bias[i], a_log[i], d_skip_x, row(ssm_norm_g), e_mat, e_mat_t, lp, "ssd_bwd",
            comm=_rs_ici(hs) if hs is not None else None)
        if hs is not None:
            rs_finish(i + 1, hs, rbs)
        grads["dt_bias"][i] = ddtb[:, :N_HEADS]
        grads["a_log"][i] = dal[:, :N_HEADS]
        grads["d_skip"][i] = dds[:, :N_HEADS]
        dproj, grads["conv_a_w"][i] = _a_bwd(dy, proj, full["conv_a_w"][i], dproj, lp, "a_bwd")
        dproj, grads["ssm_conv_w"][i], grads["ssm_conv_b"][i] = _xbc_bwd(
            dxbc, proj, full["ssm_conv_w"][i], row(ssm_conv_b), dproj, lp, "xbc_bwd")
        dproj, grads["conf_conv_w"][i], grads["conf_conv_b"][i], grads["conf_ln_g"][i], grads["conf_ln_b"][i] = _c_bwd(
            dy, proj, u1, full["conf_conv_w"][i], row(conf_conv_b), row(conf_ln_g), row(conf_ln_b), dproj, lp, "c_bwd")
        p_out = dw_out.reshape(N_DEV, n_out_loc, D_MODEL)
        if i > 0:
            dhn = _mm(dproj, w_t, "nn", BF16, MM_TM, D_MODEL, N_INP, "mm_dhn")
            dw_in_t = _mm(dproj, hn, "tn", F32, MM_TN, D_MODEL, t, "mm_dwin")
        else:
            dhn, ra_out = _mm(dproj, w_t, "nn", BF16, MM_TM, D_MODEL, N_INP, "mm_dhn",
                              comm=_rs_d2d([p_out], (_slot_mix_rows,)))
            h_out = _rs_pair_sum(p_out, ra_out, own_mix, "rs_pair_sum_out")
            dw_in_t, rb_out = _mm(dproj, hn, "tn", F32, MM_TN, D_MODEL, t, "mm_dwin", comm=_rs_ici([h_out]))
            gr_out[0] = _rs_final_sum(h_out, rb_out, chip_idx, "rs_final_sum_out")
        if i > 0:
            dh, grads["pre_g"][i], dm, grads["post_g"][i - 1] = _rms_post_bwd(
                dh, dhn, h_i, row(pre_g), saved[i - 1][7], post_g[i - 1].reshape(1, -1), rt, "rms_post_bwd")
        else:
            dh, grads["pre_g"][i], dmeta = _rms_bwd(dh, dhn, h_i, row(pre_g), lp, rt, "rms_bwd")
        pending = (_w_split_rows(dw_in_t, "w_split_rows"), p_out)
    grad_x = dh.reshape(nb, lp, d)[:, PAD + N_META:]

    names = [n for n, _, _ in SMALL_PARAMS]
    partial = [loss_blk[0:1, 0:1], dmeta] + [jnp.concatenate(grads[n], axis=0) for n in names[1:]]
    part_pack = _pack(partial)
    p_in = pending[0]
    ra_in, part_buf = _run_comm(_merge_comm(_rs_d2d([p_in], (_slot_plain,)), _ag_ici([part_pack], (_slot_plain,))), "rs_d2d")
    h_in = _rs_pair_sum(p_in, ra_in, own_plain, "rs_pair_sum_in")
    rb_in, parts_g = _run_comm(_merge_comm(_rs_ici([h_in]), _ag_d2d([part_buf], (_slot_plain,))), "rs_ici")
    gt_in[0] = _rs_final_sum(h_in, rb_in, chip_idx, "rs_final_sum_in")

    g_w_out = jnp.stack(gr_out)
    big = {"w_in": _adamw_cols(jnp.stack(gt_in), w_in, m_w_in, v_w_in, "adamw_w_in"),
           "w_out": [g_w_out, *_adamw_rows(g_w_out, w_out, m_w_out, v_w_out, n_out_loc, "adamw_w_out")]}

    total = _unpack(_sum_devices(parts_g, "sum_small_grads"), [(1,)] + [small_shapes[n] for n in names])
    loss = total[0][0]
    g_small = {}
    for n, g in zip(names, total[1:]):
        ax = small_axis[n]
        if ax is not None:
            g = lax.dynamic_slice_in_dim(g, dev * (small_shapes[n][ax] // N_DEV), small_shapes[n][ax] // N_DEV, axis=ax)
        g_small[n] = g
    loc_shapes = [_local_shape(small_shapes[n], small_axis[n]) for n in names]
    d_pack, m_pack, v_pack = _adamw_small(_pack([g_small[n] for n in names]), _pack([weights[n] for n in names]),
                                          _pack([mom1[n] for n in names]), _pack([mom2[n] for n in names]), "adamw_small")
    d_small = dict(zip(names, _unpack(d_pack, loc_shapes)))
    m_small = dict(zip(names, _unpack(m_pack, loc_shapes)))
    v_small = dict(zip(names, _unpack(v_pack, loc_shapes)))

    order = ["meta", "pre_g", "post_g", "w_in", "w_out", "conv_a_w", "ssm_conv_w", "ssm_conv_b", "dt_bias", "a_log",
             "d_skip", "ssm_norm_g", "conf_conv_w", "conf_conv_b", "conf_ln_g", "conf_ln_b"]

    def pick(k, small):
        return [big[n][k] if n in big else small[n] for n in order]

    return (loss, grad_x, *pick(0, g_small), *pick(1, d_small), *pick(2, m_small), *pick(3, v_small))
```

```python
import jax
import jax.numpy as jnp
from jax import lax
from jax.experimental import pallas as pl
from jax.experimental.pallas import tpu as pltpu

F32 = jnp.float32
BF16 = jnp.bfloat16

D_MODEL = 1024
DEPTH = 4
SEQ = 2048
CHUNK = 64
N_META = 16
PAD = 48
LP = PAD + N_META + SEQ
D_A = 512
D_B = 1024
D_C = 512
N_HEADS = 16
HEAD_DIM = 64
N_STATE = 128
N_GROUPS = 2
GROUP_W = D_B // N_GROUPS
N_XBC = D_B + 2 * N_GROUPS * N_STATE
CONV_A_K = 3
SSM_K = 4
CONF_K = 31
NORM_EPS = 1e-6
LN_EPS = 1e-5
N_IN = 6160
N_INP = 6272
COL_BZ = 2048
COL_DT = 3072
COL_XBC = 3200
COL_C = 4736
COL_MAP = ((0, 3072, 0), (3072, 4608, COL_XBC), (4608, 4624, COL_DT), (4624, 6160, COL_C))
LANE = 128
SUB = 8

ADAM_LR = 0.001
ADAM_B1 = 0.9
ADAM_B2 = 0.999
ADAM_EPS = 1e-08
ADAM_WD = 0.01
ADAM_STEP = 10

SHORT_TILES = 6
HALO_A = 16
HALO_C = 32
MM_TM = 384
MM_TN = 896
VMEM_LIMIT = 56 * 1024 * 1024

MESH = pl.DeviceIdType.MESH
N_DEV = 8


def _silu(x):
    return x * jax.nn.sigmoid(x)


def _dsilu(x):
    s = jax.nn.sigmoid(x)
    return s * (1.0 + x * (1.0 - s))


def _cparams(sem=None):
    return pltpu.CompilerParams(dimension_semantics=sem, vmem_limit_bytes=VMEM_LIMIT)


def _mm(a, b, mode, out_dtype, tm, tn, tk, name, cols_outer=False, comm=None):
    def ix(f):
        if cols_outer:
            return lambda j, i, q: f(i, j, q)
        return f

    if mode == "nn":
        (m, k), (_, n) = a.shape, b.shape
        a_spec = pl.BlockSpec((tm, tk), ix(lambda i, j, q: (i, q)))
        b_spec = pl.BlockSpec((tk, tn), ix(lambda i, j, q: (q, j)))
        dims = (((1,), (0,)), ((), ()))
    elif mode == "nt":
        (m, k), (n, _) = a.shape, b.shape
        a_spec = pl.BlockSpec((tm, tk), ix(lambda i, j, q: (i, q)))
        b_spec = pl.BlockSpec((tn, tk), ix(lambda i, j, q: (j, q)))
        dims = (((1,), (1,)), ((), ()))
    else:
        (k, m), (_, n) = a.shape, b.shape
        a_spec = pl.BlockSpec((tk, tm), ix(lambda i, j, q: (q, i)))
        b_spec = pl.BlockSpec((tk, tn), ix(lambda i, j, q: (q, j)))
        dims = (((0,), (0,)), ((), ()))
    assert m % tm == 0 and n % tn == 0 and k % tk == 0, (name, a.shape, b.shape)
    nk = k // tk
    grid = (n // tn, m // tm, nk) if cols_outer else (m // tm, n // tn, nk)

    def body(a_ref, b_ref, o_ref, acc_ref):
        part = lax.dot_general(a_ref[...].astype(BF16), b_ref[...].astype(BF16), dims, preferred_element_type=F32)
        if nk == 1:
            o_ref[...] = part.astype(o_ref.dtype)
        else:
            q = pl.program_id(2)

            @pl.when(q == 0)
            def _():
                acc_ref[...] = part

            @pl.when(q > 0)
            def _():
                acc_ref[...] += part

            @pl.when(q == nk - 1)
            def _():
                o_ref[...] = acc_ref[...].astype(o_ref.dtype)

    res = _grid_call(
        body, name, grid, [a_spec, b_spec], [pl.BlockSpec((tm, tn), ix(lambda i, j, q: (i, j)))],
        [jax.ShapeDtypeStruct((m, n), out_dtype)], [pltpu.VMEM((tm, tn) if nk > 1 else (SUB, LANE), F32)], (a, b), comm)
    return res[0] if comm is None else res


def _row_mask(i, tpe, rows):
    r = lax.broadcasted_iota(jnp.int32, (rows, 1), 0)
    return jnp.logical_or((i % tpe) != 0, r >= PAD)


def _rms_fwd(h, g, rt, name):
    t, d = h.shape

    def body(h_ref, g_ref, o_ref):
        x = h_ref[...]
        r = lax.rsqrt(jnp.mean(x * x, axis=-1, keepdims=True) + NORM_EPS)
        o_ref[...] = (x * r * g_ref[...]).astype(BF16)

    return pl.pallas_call(
        body, name=name, grid=(t // rt,),
        in_specs=[pl.BlockSpec((rt, d), lambda i: (i, 0)), pl.BlockSpec((1, d), lambda i: (0, 0))],
        out_specs=pl.BlockSpec((rt, d), lambda i: (i, 0)),
        out_shape=jax.ShapeDtypeStruct((t, d), BF16),
        compiler_params=_cparams(("parallel",)),
    )(h, g)


def _post_fwd(h, m, g, rt, name):
    t, d = h.shape

    def body(h_ref, m_ref, g_ref, o_ref):
        x = m_ref[...]
        r = lax.rsqrt(jnp.mean(x * x, axis=-1, keepdims=True) + NORM_EPS)
        o_ref[...] = h_ref[...] + x * r * g_ref[...]

    row = pl.BlockSpec((rt, d), lambda i: (i, 0))
    return pl.pallas_call(
        body, name=name, grid=(t // rt,),
        in_specs=[row, row, pl.BlockSpec((1, d), lambda i: (0, 0))], out_specs=row,
        out_shape=jax.ShapeDtypeStruct((t, d), F32),
        compiler_params=_cparams(("parallel",)),
    )(h, m, g)


def _post_rms_fwd(h, m, g_post, g_next, rt, name):
    t, d = h.shape

    def body(h_ref, m_ref, gp_ref, gn_ref, o_ref, n_ref):
        x = m_ref[...]
        r = lax.rsqrt(jnp.mean(x * x, axis=-1, keepdims=True) + NORM_EPS)
        y = h_ref[...] + x * r * gp_ref[...]
        o_ref[...] = y
        r2 = lax.rsqrt(jnp.mean(y * y, axis=-1, keepdims=True) + NORM_EPS)
        n_ref[...] = (y * r2 * gn_ref[...]).astype(BF16)

    row = pl.BlockSpec((rt, d), lambda i: (i, 0))
    vec = pl.BlockSpec((1, d), lambda i: (0, 0))
    return pl.pallas_call(
        body, name=name, grid=(t // rt,),
        in_specs=[row, row, vec, vec], out_specs=[row, row],
        out_shape=[jax.ShapeDtypeStruct((t, d), F32), jax.ShapeDtypeStruct((t, d), BF16)],
        compiler_params=_cparams(("parallel",)),
    )(h, m, g_post, g_next)


def _rms_bwd_math(x, g, dy):
    r = lax.rsqrt(jnp.mean(x * x, axis=-1, keepdims=True) + NORM_EPS)
    gdy = dy * g
    dx = r * gdy - x * (r * r * r) * jnp.mean(gdy * x, axis=-1, keepdims=True)
    return dx, dy * x * r


def _post_bwd(dh, m, g, rt, name):
    t, d = dh.shape

    def body(dh_ref, m_ref, g_ref, dm_ref, dg_ref):
        dm, dgt = _rms_bwd_math(m_ref[...], g_ref[...], dh_ref[...])
        dm_ref[...] = dm.astype(BF16)

        @pl.when(pl.program_id(0) == 0)
        def _():
            dg_ref[...] = jnp.zeros_like(dg_ref)

        dg_ref[...] += jnp.sum(dgt, axis=0, keepdims=True)

    row = pl.BlockSpec((rt, d), lambda i: (i, 0))
    vec = pl.BlockSpec((1, d), lambda i: (0, 0))
    return pl.pallas_call(
        body, name=name, grid=(t // rt,),
        in_specs=[row, row, vec], out_specs=[row, vec],
        out_shape=[jax.ShapeDtypeStruct((t, d), BF16), jax.ShapeDtypeStruct((1, d), F32)],
        compiler_params=_cparams(("arbitrary",)),
    )(dh, m, g)


def _rms_bwd(dh_res, dhn, h, g, lp, rt, name):
    t, d = h.shape
    tpe = lp // rt

    def body(dr_ref, dn_ref, h_ref, g_ref, dh_ref, dg_ref, dmeta_ref):
        i = pl.program_id(0)
        dx, dgt = _rms_bwd_math(h_ref[...], g_ref[...], dn_ref[...].astype(F32))
        dh = dr_ref[...] + dx
        dh_ref[...] = dh

        @pl.when(i == 0)
        def _():
            dg_ref[...] = jnp.zeros_like(dg_ref)
            dmeta_ref[...] = jnp.zeros_like(dmeta_ref)

        dg_ref[...] += jnp.sum(dgt, axis=0, keepdims=True)

        @pl.when((i % tpe) == 0)
        def _():
            dmeta_ref[...] += dh[PAD:PAD + N_META, :]

    row = pl.BlockSpec((rt, d), lambda i: (i, 0))
    vec = pl.BlockSpec((1, d), lambda i: (0, 0))
    return pl.pallas_call(
        body, name=name, grid=(t // rt,),
        in_specs=[row, row, row, vec],
        out_specs=[row, vec, pl.BlockSpec((N_META, d), lambda i: (0, 0))],
        out_shape=[jax.ShapeDtypeStruct((t, d), F32), jax.ShapeDtypeStruct((1, d), F32),
                   jax.ShapeDtypeStruct((N_META, d), F32)],
        compiler_params=_cparams(("arbitrary",)),
    )(dh_res, dhn, h, g)


def _rms_post_bwd(dh_res, dhn, h, g, m_prev, g_prev, rt, name):
    t, d = h.shape

    def body(dr_ref, dn_ref, h_ref, g_ref, m_ref, gp_ref, dh_ref, dg_ref, dm_ref, dgp_ref):
        dx, dgt = _rms_bwd_math(h_ref[...], g_ref[...], dn_ref[...].astype(F32))
        dh = dr_ref[...] + dx
        dh_ref[...] = dh
        dm, dgt_prev = _rms_bwd_math(m_ref[...], gp_ref[...], dh)
        dm_ref[...] = dm.astype(BF16)

        @pl.when(pl.program_id(0) == 0)
        def _():
            dg_ref[...] = jnp.zeros_like(dg_ref)
            dgp_ref[...] = jnp.zeros_like(dgp_ref)

        dg_ref[...] += jnp.sum(dgt, axis=0, keepdims=True)
        dgp_ref[...] += jnp.sum(dgt_prev, axis=0, keepdims=True)

    row = pl.BlockSpec((rt, d), lambda i: (i, 0))
    vec = pl.BlockSpec((1, d), lambda i: (0, 0))
    return pl.pallas_call(
        body, name=name, grid=(t // rt,),
        in_specs=[row, row, row, vec, row, vec], out_specs=[row, vec, row, vec],
        out_shape=[jax.ShapeDtypeStruct((t, d), F32), jax.ShapeDtypeStruct((1, d), F32),
                   jax.ShapeDtypeStruct((t, d), BF16), jax.ShapeDtypeStruct((1, d), F32)],
        compiler_params=_cparams(("arbitrary",)),
    )(dh_res, dhn, h, g, m_prev, g_prev)


def _loss_kernel(h, target, lp, name):
    t, d = h.shape
    nb = t // lp
    seq = lp - PAD - N_META
    rows = next(r for r in (1024, 512, 256, 128, CHUNK) if seq % r == 0)
    per = seq // rows

    def body(h_ref, t_ref, zero_ref, dh_ref, loss_ref):
        @pl.when(jnp.logical_and(pl.program_id(0) == 0, pl.program_id(1) == 0))
        def _():
            loss_ref[...] = jnp.zeros_like(loss_ref)

        err = h_ref[...] - t_ref[...]
        dh_ref[...] = err * (1.0 / d)
        loss_ref[...] += (0.5 / d) * jnp.sum(err * err)

    win = pl.BlockSpec((pl.Element(rows), pl.Element(d)),
                       lambda b, j: (pl.multiple_of(b * lp + PAD + N_META + j * rows, CHUNK), 0))
    return pl.pallas_call(
        body, name=name, grid=(nb, per),
        in_specs=[win, pl.BlockSpec((rows, d), lambda b, j: (b * per + j, 0)), HBM_SPEC],
        out_specs=[win, pl.BlockSpec((SUB, LANE), lambda b, j: (0, 0))],
        out_shape=[jax.ShapeDtypeStruct((t, d), F32), jax.ShapeDtypeStruct((SUB, LANE), F32)],
        input_output_aliases={2: 0},
        compiler_params=_cparams(("arbitrary", "arbitrary")),
    )(h, target, jnp.zeros((t, d), F32))


def _window_spec(rows, width, col):
    return pl.BlockSpec((pl.Element(rows), pl.Element(width)), lambda i: (i * rows, col))


def _halo_specs(t, width, col, halo, tile):
    cur = _window_spec(tile, width, col)
    prev = pl.BlockSpec((pl.Element(halo), pl.Element(width)),
                        lambda i: (pl.multiple_of(jnp.maximum(i * tile - halo, 0), halo), col))
    nxt = pl.BlockSpec((pl.Element(halo), pl.Element(width)),
                       lambda i: (pl.multiple_of(jnp.minimum((i + 1) * tile, t - halo), halo), col))
    return cur, prev, nxt


def _f32(ref, lo, hi):
    return ref[:, lo:hi].astype(F32)


def _rows_from(x, start, rows):
    s = start % SUB
    if s == 0:
        return x[start:start + rows]
    return pltpu.roll(x, x.shape[0] - s, axis=0)[start - s:start - s + rows]


def _conv_a(ve, w):
    rows = ve.shape[0] - HALO_A
    return (w[0:1] * _rows_from(ve, HALO_A - 2, rows) + w[1:2] * _rows_from(ve, HALO_A - 1, rows)
            + w[2:3] * ve[HALO_A:HALO_A + rows])


def _a_fwd(proj, w, lp, name):
    t = proj.shape[0]
    tile = lp // SHORT_TILES
    tpe = lp // tile
    cur, prev, _ = _halo_specs(t, 4 * D_A, 0, HALO_A, tile)

    def body(p_ref, ph_ref, w_ref, y_ref):
        first = (pl.program_id(0) % tpe) == 0
        v = _f32(p_ref, D_A, 2 * D_A) * _f32(p_ref, 2 * D_A, 3 * D_A)
        vh = jnp.where(first, 0.0, _f32(ph_ref, D_A, 2 * D_A) * _f32(ph_ref, 2 * D_A, 3 * D_A))
        cv = _conv_a(jnp.concatenate([vh, v], axis=0), w_ref[...])
        y_ref[...] = (_f32(p_ref, 0, D_A) * cv * _silu(_f32(p_ref, 3 * D_A, 4 * D_A))).astype(BF16)

    return pl.pallas_call(
        body, name=name, grid=(t // tile,),
        in_specs=[cur, prev, pl.BlockSpec((CONV_A_K, D_A), lambda i: (0, 0))],
        out_specs=_window_spec(tile, D_A, D_B),
        out_shape=jax.ShapeDtypeStruct((t, D_B + D_A + D_C), BF16),
        compiler_params=_cparams(("parallel",)),
    )(proj, proj, w)


def _a_bwd(dy, proj, w, dproj, lp, name):
    t = proj.shape[0]
    tile = lp // SHORT_TILES
    tpe = lp // tile
    cur, prev, nxt = _halo_specs(t, 4 * D_A, 0, HALO_A, tile)
    dcur, _, dnxt = _halo_specs(t, D_A, D_B, HALO_A, tile)

    def body(dy_ref, dyn_ref, p_ref, ph_ref, pn_ref, w_ref, dproj_ref, dp_ref, dw_ref):
        i = pl.program_id(0)
        first = (i % tpe) == 0
        last = (i % tpe) == tpe - 1
        w = w_ref[...]
        ab, ac, ax, az = (_f32(p_ref, k * D_A, (k + 1) * D_A) for k in range(4))
        v = ac * ax
        vh = jnp.where(first, 0.0, _f32(ph_ref, D_A, 2 * D_A) * _f32(ph_ref, 2 * D_A, 3 * D_A))
        ve = jnp.concatenate([vh, v], axis=0)
        taps = [_rows_from(ve, HALO_A - 2 + k, tile) for k in range(CONV_A_K)]
        cv = w[0:1] * taps[0] + w[1:2] * taps[1] + w[2:3] * taps[2]
        s = _silu(az)
        dy_ = dy_ref[...].astype(F32)
        dcv = dy_ * ab * s
        dcvn = jnp.where(last, 0.0, dyn_ref[...].astype(F32) * _f32(pn_ref, 0, D_A) * _silu(_f32(pn_ref, 3 * D_A, 4 * D_A)))
        dce = jnp.concatenate([dcv, dcvn], axis=0)
        dv = w[2:3] * dce[0:tile] + w[1:2] * _rows_from(dce, 1, tile) + w[0:1] * _rows_from(dce, 2, tile)
        dp = jnp.concatenate([dy_ * cv * s, dv * ax, dv * ac, dy_ * ab * cv * _dsilu(az)], axis=1)
        dp_ref[...] = jnp.where(_row_mask(i, tpe, tile), dp, 0.0).astype(BF16)
        dw = jnp.concatenate(
            [jnp.sum(dcv * taps[k], axis=0, keepdims=True) for k in range(CONV_A_K)], axis=0)

        @pl.when(i == 0)
        def _():
            dw_ref[...] = jnp.zeros_like(dw_ref)

        dw_ref[...] += dw

    wspec = pl.BlockSpec((CONV_A_K, D_A), lambda i: (0, 0))
    return pl.pallas_call(
        body, name=name, grid=(t // tile,),
        in_specs=[dcur, dnxt, cur, prev, nxt, wspec, HBM_SPEC],
        out_specs=[_window_spec(tile, 4 * D_A, 0), wspec],
        out_shape=[jax.ShapeDtypeStruct(dproj.shape, dproj.dtype), jax.ShapeDtypeStruct((CONV_A_K, D_A), F32)],
        input_output_aliases={6: 0},
        compiler_params=_cparams(("arbitrary",)),
    )(dy, dy, proj, proj, proj, w, dproj)


def _conv_ssm(xe, w, rows, off):
    acc = w[0:1] * _rows_from(xe, off - 3, rows)
    for k in range(1, SSM_K):
        acc = acc + w[k:k + 1] * _rows_from(xe, off - 3 + k, rows)
    return acc


def _xbc_fwd(proj, w, b, lp, name):
    t = proj.shape[0]
    tile = lp // SHORT_TILES
    tpe = lp // tile
    cur, prev, _ = _halo_specs(t, N_XBC, COL_XBC, HALO_A, tile)

    def body(x_ref, xh_ref, w_ref, b_ref, o_ref):
        first = (pl.program_id(0) % tpe) == 0
        xh = jnp.where(first, 0.0, xh_ref[...].astype(F32))
        xe = jnp.concatenate([xh, x_ref[...].astype(F32)], axis=0)
        o_ref[...] = _silu(_conv_ssm(xe, w_ref[...], tile, HALO_A) + b_ref[...]).astype(BF16)

    return pl.pallas_call(
        body, name=name, grid=(t // tile,),
        in_specs=[cur, prev, pl.BlockSpec((SSM_K, N_XBC), lambda i: (0, 0)), pl.BlockSpec((1, N_XBC), lambda i: (0, 0))],
        out_specs=pl.BlockSpec((tile, N_XBC), lambda i: (i, 0)),
        out_shape=jax.ShapeDtypeStruct((t, N_XBC), BF16),
        compiler_params=_cparams(("parallel",)),
    )(proj, proj, w, b)


def _xbc_bwd(dxbc, proj, w, b, dproj, lp, name):
    t = proj.shape[0]
    tile = lp // SHORT_TILES
    tpe = lp // tile
    cur, prev, nxt = _halo_specs(t, N_XBC, COL_XBC, HALO_A, tile)
    dcur, _, dnxt = _halo_specs(t, N_XBC, 0, HALO_A, tile)

    def body(d_ref, dn_ref, x_ref, xh_ref, xn_ref, w_ref, b_ref, dproj_ref, dx_ref, dw_ref, db_ref):
        i = pl.program_id(0)
        first = (i % tpe) == 0
        last = (i % tpe) == tpe - 1
        w = w_ref[...]
        xh = jnp.where(first, 0.0, xh_ref[...].astype(F32))
        xe = jnp.concatenate([xh, x_ref[...].astype(F32), xn_ref[...].astype(F32)], axis=0)
        taps = [_rows_from(xe, HALO_A - 3 + k, tile + HALO_A) for k in range(SSM_K)]
        pre = b_ref[...] + w[0:1] * taps[0]
        for k in range(1, SSM_K):
            pre = pre + w[k:k + 1] * taps[k]
        de = jnp.concatenate([d_ref[...].astype(F32), jnp.where(last, 0.0, dn_ref[...].astype(F32))], axis=0)
        dpre = de * _dsilu(pre)
        dx = w[3:4] * dpre[0:tile]
        for k in range(SSM_K - 1):
            dx = dx + w[k:k + 1] * _rows_from(dpre, 3 - k, tile)
        dx_ref[...] = jnp.where(_row_mask(i, tpe, tile), dx, 0.0).astype(BF16)
        dpc = dpre[0:tile]
        dw = jnp.concatenate(
            [jnp.sum(dpc * taps[k][0:tile], axis=0, keepdims=True) for k in range(SSM_K)], axis=0)

        @pl.when(i == 0)
        def _():
            dw_ref[...] = jnp.zeros_like(dw_ref)
            db_ref[...] = jnp.zeros_like(db_ref)

        dw_ref[...] += dw
        db_ref[...] += jnp.sum(dpc, axis=0, keepdims=True)

    wspec = pl.BlockSpec((SSM_K, N_XBC), lambda i: (0, 0))
    bspec = pl.BlockSpec((1, N_XBC), lambda i: (0, 0))
    return pl.pallas_call(
        body, name=name, grid=(t // tile,),
        in_specs=[dcur, dnxt, cur, prev, nxt, wspec, bspec, HBM_SPEC],
        out_specs=[_window_spec(tile, N_XBC, COL_XBC), wspec, bspec],
        out_shape=[jax.ShapeDtypeStruct(dproj.shape, dproj.dtype), jax.ShapeDtypeStruct((SSM_K, N_XBC), F32),
                   jax.ShapeDtypeStruct((1, N_XBC), F32)],
        input_output_aliases={7: 0},
        compiler_params=_cparams(("arbitrary",)),
    )(dxbc, dxbc, proj, proj, proj, w, b, dproj)


SUBROWS = 32


def _fill_shifted(scr, x):
    scr[0] = x
    for s in range(1, SUB):
        scr[s] = pltpu.roll(x, x.shape[0] - s, axis=0)


def _window(scr, start, rows):
    s = start % SUB
    return scr[s, start - s:start - s + rows, :]


def _conv_conf(scr, w, rows, off, base):
    acc = w[0:1] * _window(scr, base + off - (CONF_K - 1), rows)
    for k in range(1, CONF_K):
        acc = acc + w[k:k + 1] * _window(scr, base + off - (CONF_K - 1) + k, rows)
    return acc


def _ln_fwd(u1, g, b):
    mu = jnp.mean(u1, axis=-1, keepdims=True)
    xc = u1 - mu
    rstd = lax.rsqrt(jnp.mean(xc * xc, axis=-1, keepdims=True) + LN_EPS)
    n = xc * rstd
    return n, rstd, n * g + b


def _c_fwd(proj, w, cb, g, b, ybuf, lp, name):
    t = proj.shape[0]
    tile = lp // SHORT_TILES
    tpe = lp // tile
    cur, prev, _ = _halo_specs(t, 3 * D_C, COL_C, HALO_C, tile)

    def body(p_ref, ph_ref, w_ref, cb_ref, g_ref, b_ref, ybuf_ref, y_ref, u1_ref, u0_scr):
        first = (pl.program_id(0) % tpe) == 0
        u0h = jnp.where(first, 0.0, _f32(ph_ref, 0, D_C) * jax.nn.sigmoid(_f32(ph_ref, D_C, 2 * D_C)))
        _fill_shifted(u0_scr, jnp.concatenate([u0h, _f32(p_ref, 0, D_C) * jax.nn.sigmoid(_f32(p_ref, D_C, 2 * D_C))], axis=0))
        w = w_ref[...]
        for r0 in range(0, tile, SUBROWS):
            u1 = _conv_conf(u0_scr, w, SUBROWS, HALO_C, r0) + cb_ref[...]
            u1_ref[r0:r0 + SUBROWS, :] = u1.astype(BF16)
            _, _, u2 = _ln_fwd(u1, g_ref[...], b_ref[...])
            cz = p_ref[r0:r0 + SUBROWS, 2 * D_C:3 * D_C].astype(F32)
            y_ref[r0:r0 + SUBROWS, :] = (_silu(u2) * _silu(cz)).astype(BF16)

    vec = pl.BlockSpec((1, D_C), lambda i: (0, 0))
    return pl.pallas_call(
        body, name=name, grid=(t // tile,),
        in_specs=[cur, prev, pl.BlockSpec((CONF_K, D_C), lambda i: (0, 0)), vec, vec, vec, HBM_SPEC],
        out_specs=[_window_spec(tile, D_C, D_B + D_A), pl.BlockSpec((tile, D_C), lambda i: (i, 0))],
        out_shape=[jax.ShapeDtypeStruct(ybuf.shape, ybuf.dtype), jax.ShapeDtypeStruct((t, D_C), BF16)],
        scratch_shapes=[pltpu.VMEM((SUB, HALO_C + tile, D_C), F32)],
        input_output_aliases={6: 0},
        compiler_params=_cparams(("parallel",)),
    )(proj, proj, w, cb, g, b, ybuf)


def _c_bwd(dy, proj, u1, w, cb, g, b, dproj, lp, name):
    t = proj.shape[0]
    tile = lp // SHORT_TILES
    tpe = lp // tile
    cur, prev, nxt = _halo_specs(t, 3 * D_C, COL_C, HALO_C, tile)
    dcur, _, dnxt = _halo_specs(t, D_C, D_B + D_A, HALO_C, tile)
    ucur, _, unxt = _halo_specs(t, D_C, 0, HALO_C, tile)
    ext = tile + HALO_C

    def body(dy_ref, dyn_ref, p_ref, ph_ref, pn_ref, u1_ref, u1n_ref, w_ref, cb_ref, g_ref, b_ref, dproj_ref,
             dp_ref, dw_ref, dcb_ref, dg_ref, db_ref, u0_scr, du1_scr, wacc_scr):
        i = pl.program_id(0)
        first = (i % tpe) == 0
        last = (i % tpe) == tpe - 1
        w = w_ref[...]

        @pl.when(i == 0)
        def _():
            dw_ref[...] = jnp.zeros_like(dw_ref)
            dcb_ref[...] = jnp.zeros_like(dcb_ref)
            dg_ref[...] = jnp.zeros_like(dg_ref)
            db_ref[...] = jnp.zeros_like(db_ref)

        u0h = jnp.where(first, 0.0, _f32(ph_ref, 0, D_C) * jax.nn.sigmoid(_f32(ph_ref, D_C, 2 * D_C)))
        _fill_shifted(u0_scr, jnp.concatenate(
            [u0h, _f32(p_ref, 0, D_C) * jax.nn.sigmoid(_f32(p_ref, D_C, 2 * D_C))], axis=0))
        dcb = jnp.zeros((1, D_C), F32)
        dg = jnp.zeros((1, D_C), F32)
        db = jnp.zeros((1, D_C), F32)
        for r0 in range(0, ext, SUBROWS):
            in_tile = r0 < tile
            src, dsrc, usrc, q0 = (p_ref, dy_ref, u1_ref, r0) if in_tile else (pn_ref, dyn_ref, u1n_ref, r0 - tile)
            u1 = usrc[q0:q0 + SUBROWS, :].astype(F32)
            n, rstd, u2 = _ln_fwd(u1, g_ref[...], b_ref[...])
            cz = src[q0:q0 + SUBROWS, 2 * D_C:3 * D_C].astype(F32)
            dyc = dsrc[q0:q0 + SUBROWS, :].astype(F32)
            if not in_tile:
                dyc = jnp.where(last, 0.0, dyc)
            du2 = dyc * _silu(cz) * _dsilu(u2)
            dn = du2 * g_ref[...]
            du1 = rstd * (dn - jnp.mean(dn, axis=-1, keepdims=True) - n * jnp.mean(dn * n, axis=-1, keepdims=True))
            du1_scr[0, r0:r0 + SUBROWS, :] = du1
            if in_tile:
                dp_ref[r0:r0 + SUBROWS, 2 * D_C:3 * D_C] = (dyc * _silu(u2) * _dsilu(cz)).astype(BF16)
                dg = dg + jnp.sum(du2 * n, axis=0, keepdims=True)
                db = db + jnp.sum(du2, axis=0, keepdims=True)
                dcb = dcb + jnp.sum(du1, axis=0, keepdims=True)
        dcb_ref[...] += dcb
        dg_ref[...] += dg
        db_ref[...] += db
        mask = _row_mask(i, tpe, tile)
        _fill_shifted(du1_scr, du1_scr[0])
        for r0 in range(0, tile, SUBROWS):
            acc = w[0:1] * _window(du1_scr, r0 + CONF_K - 1, SUBROWS)
            for k in range(1, CONF_K):
                acc = acc + w[k:k + 1] * _window(du1_scr, r0 + CONF_K - 1 - k, SUBROWS)
            ca = p_ref[r0:r0 + SUBROWS, 0:D_C].astype(F32)
            sg = jax.nn.sigmoid(p_ref[r0:r0 + SUBROWS, D_C:2 * D_C].astype(F32))
            m = mask[r0:r0 + SUBROWS]
            dp_ref[r0:r0 + SUBROWS, 0:D_C] = jnp.where(m, acc * sg, 0.0).astype(BF16)
            dp_ref[r0:r0 + SUBROWS, D_C:2 * D_C] = jnp.where(m, acc * ca * sg * (1.0 - sg), 0.0).astype(BF16)
        for k in range(CONF_K):
            part = jnp.zeros((SUB, D_C), F32)
            for r0 in range(0, tile, SUBROWS):
                prod = du1_scr[0, r0:r0 + SUBROWS, :] * _window(u0_scr, HALO_C + r0 - (CONF_K - 1) + k, SUBROWS)
                for q in range(0, SUBROWS, SUB):
                    part = part + prod[q:q + SUB]
            wacc_scr[k:k + 1, :] = jnp.sum(part, axis=0, keepdims=True)
        dw_ref[...] += wacc_scr[0:CONF_K, :]

    vec = pl.BlockSpec((1, D_C), lambda i: (0, 0))
    wspec = pl.BlockSpec((CONF_K, D_C), lambda i: (0, 0))
    return pl.pallas_call(
        body, name=name, grid=(t // tile,),
        in_specs=[dcur, dnxt, cur, prev, nxt, ucur, unxt, wspec, vec, vec, vec, HBM_SPEC],
        out_specs=[_window_spec(tile, 3 * D_C, COL_C), wspec, vec, vec, vec],
        out_shape=[jax.ShapeDtypeStruct(dproj.shape, dproj.dtype), jax.ShapeDtypeStruct((CONF_K, D_C), F32),
                   jax.ShapeDtypeStruct((1, D_C), F32), jax.ShapeDtypeStruct((1, D_C), F32),
                   jax.ShapeDtypeStruct((1, D_C), F32)],
        scratch_shapes=[pltpu.VMEM((SUB, ext, D_C), F32), pltpu.VMEM((SUB, ext, D_C), F32),
                        pltpu.VMEM((HALO_C, D_C), F32)],
        input_output_aliases={11: 0},
        compiler_params=_cparams(("arbitrary",)),
    )(dy, dy, proj, proj, proj, u1, u1, w, cb, g, b, dproj)


def _split_dot(x, m_bf16, terms):
    acc = None
    rem = x
    for _ in range(terms):
        hi = rem.astype(BF16)
        part = jnp.dot(hi, m_bf16, preferred_element_type=F32)
        acc = part if acc is None else acc + part
        rem = rem - hi.astype(F32)
    return acc


def _split_pieces(x, terms):
    out, rem = [], x
    for _ in range(terms):
        hi = rem.astype(BF16)
        out.append(hi)
        rem = rem - hi.astype(F32)
    return out


def _split_dot_many(xs, terms, m_bf16):
    pieces = [p for x, n in zip(xs, terms) for p in _split_pieces(x, n)]
    prod = jnp.dot(jnp.concatenate(pieces, axis=0), m_bf16, preferred_element_type=F32)
    out, off = [], 0
    for x, n in zip(xs, terms):
        rows = x.shape[0]
        acc = prod[off:off + rows]
        for q in range(1, n):
            acc = acc + prod[off + q * rows:off + (q + 1) * rows]
        out.append(acc)
        off += n * rows
    return out


def _split_dot_left(m_bf16, x, terms):
    cols = x.shape[1]
    prod = jnp.dot(m_bf16, jnp.concatenate(_split_pieces(x, terms), axis=1), preferred_element_type=F32)
    acc = prod[:, 0:cols]
    for q in range(1, terms):
        acc = acc + prod[:, q * cols:(q + 1) * cols]
    return acc


def _tri(rows_ge_cols):
    r = lax.broadcasted_iota(jnp.int32, (CHUNK, CHUNK), 0)
    c = lax.broadcasted_iota(jnp.int32, (CHUNK, CHUNK), 1)
    return (r >= c) if rows_ge_cols else (r <= c)


def _softplus(x):
    return jnp.maximum(x, 0.0) + jnp.log(1.0 + jnp.exp(-jnp.abs(x)))


def _ssd_common(dtraw, dtb, dtb_t, a_log, a_log_t, e_mat, valid_col, valid_row, x_terms=2):
    a = -jnp.exp(a_log)
    lane = lax.broadcasted_iota(jnp.int32, (1, LANE), 1)
    a = jnp.where(lane < N_HEADS, a, 0.0)
    a_t = -jnp.exp(a_log_t)
    dt = jnp.where(valid_col, _softplus(dtraw + dtb), 0.0)
    dt = jnp.where(lane < N_HEADS, dt, 0.0)
    dt_t = jnp.where(valid_row, _softplus(dtraw.T[0:N_HEADS, :] + dtb_t), 0.0)
    ltri = _tri(True).astype(BF16)
    utri = _tri(False).astype(BF16)
    big_a = _split_dot_left(ltri, dt * a, 3)
    big_a_t = _split_dot_many([dt_t * a_t], [3], utri)[0]
    e_a = jnp.exp(big_a)
    d_s = jnp.exp(big_a[CHUNK - 1:CHUNK, :] - big_a)
    if x_terms == 1:
        dt_x, e_a_x, d_s_x = _split_dot_many([dt, e_a, d_s], [1, 2, 1], e_mat)
    else:
        dt_x, e_a_x, d_s_x = (_split_dot(q, e_mat, x_terms) for q in (dt, e_a, d_s))
    cd_x = e_a_x[CHUNK - 1:CHUNK, :]
    return a, dt, big_a, big_a_t, e_a, d_s, dt_x, e_a_x, d_s_x, cd_x


def _decay(big_a, big_a_t, h, transposed):
    col = big_a[:, h:h + 1]
    row = big_a_t[h:h + 1, :]
    if not transposed:
        seg = col - row
        return jnp.where(_tri(True), jnp.exp(jnp.minimum(seg, 0.0)), 0.0)
    seg = row - col
    return jnp.where(_tri(False), jnp.exp(jnp.minimum(seg, 0.0)), 0.0)


NT_DIMS = (((1,), (1,)), ((), ()))
TN_DIMS = (((0,), (0,)), ((), ()))
HBM_SPEC = pl.BlockSpec(memory_space=pl.ANY)


class _Comm:
    def __init__(self, inputs, out_shapes, sem_shapes, copies, aliases=None):
        self.inputs, self.out_shapes, self.sem_shapes, self.copies = inputs, out_shapes, sem_shapes, copies
        self.aliases = aliases or {}

    def start(self, ins, outs, sems):
        local, sends, _ = self.copies(ins, outs, sems, False)
        for cp in local + sends:
            cp.start()

    def wait(self, ins, outs, sems):
        local, sends, recvs = self.copies(ins, outs, sems, True)
        for cp in recvs:
            cp.wait_recv()
        for cp in sends:
            cp.wait_send()
        for cp in local:
            cp.wait()


def _merge_comm(*comms):
    comms = [c for c in comms if c is not None]
    if len(comms) <= 1:
        return comms[0] if comms else None

    def copies(ins, outs, sems, with_recvs):
        local, sends, recvs = [], [], []
        i0 = o0 = s0 = 0
        for c in comms:
            ni, no, ns = len(c.inputs), len(c.out_shapes), len(c.sem_shapes)
            loc, snd, rcv = c.copies(ins[i0:i0 + ni], outs[o0:o0 + no], sems[s0:s0 + ns], with_recvs)
            local, sends, recvs = local + loc, sends + snd, recvs + rcv
            i0, o0, s0 = i0 + ni, o0 + no, s0 + ns
        return local, sends, recvs

    aliases, i0, o0 = {}, 0, 0
    for c in comms:
        aliases.update({i0 + k: o0 + v for k, v in c.aliases.items()})
        i0, o0 = i0 + len(c.inputs), o0 + len(c.out_shapes)
    return _Comm([a for c in comms for a in c.inputs], [s for c in comms for s in c.out_shapes],
                 [s for c in comms for s in c.sem_shapes], copies, aliases)


def _grid_call(body, name, grid, in_specs, out_specs, out_shape, scratch_shapes, operands, comm=None, aliases=None):
    aliases = dict(aliases or {})
    if comm is None:
        return pl.pallas_call(
            body, name=name, grid=grid, in_specs=in_specs, out_specs=out_specs, out_shape=out_shape,
            scratch_shapes=scratch_shapes, input_output_aliases=aliases,
            compiler_params=_cparams(("arbitrary",) * len(grid)))(*operands)
    n_in, n_out, n_scr = len(in_specs), len(out_specs), len(scratch_shapes)
    nci, nco = len(comm.inputs), len(comm.out_shapes)

    def wrapped(*refs):
        ins, cins = refs[:n_in], refs[n_in:n_in + nci]
        o0 = n_in + nci
        outs, couts = refs[o0:o0 + n_out], refs[o0 + n_out:o0 + n_out + nco]
        s0 = o0 + n_out + nco
        scr, csems = refs[s0:s0 + n_scr], refs[s0 + n_scr:]
        first = pl.program_id(0) == 0
        last = pl.program_id(0) == grid[0] - 1
        for k in range(1, len(grid)):
            first = jnp.logical_and(first, pl.program_id(k) == 0)
            last = jnp.logical_and(last, pl.program_id(k) == grid[k] - 1)

        @pl.when(first)
        def _():
            comm.start(cins, couts, csems)

        body(*ins, *outs, *scr)

        @pl.when(last)
        def _():
            comm.wait(cins, couts, csems)

    res = pl.pallas_call(
        wrapped, name=name, grid=grid,
        in_specs=list(in_specs) + [HBM_SPEC] * nci, out_specs=list(out_specs) + [HBM_SPEC] * nco,
        out_shape=list(out_shape) + list(comm.out_shapes),
        scratch_shapes=list(scratch_shapes) + list(comm.sem_shapes),
        input_output_aliases={**aliases, **{n_in + k: n_out + v for k, v in comm.aliases.items()}},
        compiler_params=_cparams(("arbitrary",) * len(grid)))(*operands, *comm.inputs)
    return res


def _ssd_fwd(xbc, proj, dtb, a_log, d_skip_x, norm_g, e_mat, ybuf, lp, name, comm=None):
    t = xbc.shape[0]
    cpe = lp // CHUNK
    nb = t // lp
    dtb_p = jnp.pad(dtb.reshape(1, N_HEADS), ((0, 0), (0, LANE - N_HEADS)))
    alog_p = jnp.pad(a_log.reshape(1, N_HEADS), ((0, 0), (0, LANE - N_HEADS)))
    dtb_t = dtb.reshape(N_HEADS, 1)
    alog_t = a_log.reshape(N_HEADS, 1)

    def body(xbc_ref, bz_ref, dt_ref, dtb_ref, dtbt_ref, al_ref, alt_ref, dx_ref, g_ref, e_ref, ybuf_ref,
             yb_ref, ys_ref, st_ref, s_scr):
        c = pl.program_id(0)

        @pl.when(c == 0)
        def _():
            s_scr[...] = jnp.zeros_like(s_scr)

        rows = lax.broadcasted_iota(jnp.int32, (CHUNK, 1), 0)
        cols = lax.broadcasted_iota(jnp.int32, (1, CHUNK), 1)
        valid_col = jnp.logical_or(c > 0, rows >= PAD)
        valid_row = jnp.logical_or(c > 0, cols >= PAD)
        e_mat = e_ref[...]
        ex = []
        for b in range(nb):
            _, _, big_a, big_a_t, _, _, dt_x, e_a_x, d_s_x, cd_x = _ssd_common(
                dt_ref[b].astype(F32), dtb_ref[...], dtbt_ref[...], al_ref[...], alt_ref[...], e_mat, valid_col, valid_row)
            xs = xbc_ref[b, :, 0:D_B].astype(F32)
            xdt = xs * dt_x
            st_prev = s_scr[b]
            ex.append(dict(big_a=big_a, big_a_t=big_a_t, e_a_x=e_a_x, cd_x=cd_x, xs=xs,
                           bs=xbc_ref[b, :, D_B:D_B + N_GROUPS * N_STATE], cs=xbc_ref[b, :, D_B + N_GROUPS * N_STATE:N_XBC],
                           xdt_b=xdt.astype(BF16), st_prev=st_prev, st_b=st_prev.astype(BF16),
                           u_b=(xdt * d_s_x).astype(BF16), bz=bz_ref[b].astype(F32), y_parts=[], new_st=[]))
        for g in range(N_GROUPS):
            gs = slice(g * N_STATE, (g + 1) * N_STATE)
            gw = slice(g * GROUP_W, (g + 1) * GROUP_W)
            for q in ex:
                cb = lax.dot_general(q["cs"][:, gs], q["bs"][:, gs], NT_DIMS, preferred_element_type=F32)
                q["y_off"] = jnp.dot(q["cs"][:, gs], q["st_b"][:, gw], preferred_element_type=F32)
                q["cb"] = cb
                q["diag"] = []
            for e in range(N_HEADS // N_GROUPS):
                h = g * (N_HEADS // N_GROUPS) + e
                for q in ex:
                    m = (q["cb"] * _decay(q["big_a"], q["big_a_t"], h, False)).astype(BF16)
                    q["diag"].append(jnp.dot(m, q["xdt_b"][:, h * HEAD_DIM:(h + 1) * HEAD_DIM], preferred_element_type=F32))
            for q in ex:
                q["y_parts"].append(jnp.concatenate(q["diag"], axis=1) + q["y_off"] * q["e_a_x"][:, gw])
                upd = lax.dot_general(q["bs"][:, gs], q["u_b"][:, gw], TN_DIMS, preferred_element_type=F32)
                q["new_st"].append(q["st_prev"][:, gw] * q["cd_x"][:, gw] + upd)
        for b, q in enumerate(ex):
            y = jnp.concatenate(q["y_parts"], axis=1) + q["xs"] * dx_ref[...]
            z = y * _silu(q["bz"])
            r = lax.rsqrt(jnp.mean(z * z, axis=-1, keepdims=True) + NORM_EPS)
            q["y"], q["yb"] = y, (z * r * g_ref[...]).astype(BF16)
        for b, q in enumerate(ex):
            st_ref[b] = q["st_b"]
            s_scr[b] = jnp.concatenate(q["new_st"], axis=1)
            ys_ref[b] = q["y"].astype(BF16)
            yb_ref[b] = q["yb"]

    def row(width, col):
        return pl.BlockSpec((nb, CHUNK, width), lambda c: (0, c, col))

    def const(shape):
        return pl.BlockSpec(shape, lambda c: (0,) * len(shape))

    proj3 = proj.reshape(nb, lp, N_INP)
    ybuf, yssd, states, *rest = _grid_call(
        body, name, (cpe,),
        [row(N_XBC, 0), row(D_B, COL_BZ // D_B), row(LANE, COL_DT // LANE),
         const((1, LANE)), const((N_HEADS, 1)), const((1, LANE)), const((N_HEADS, 1)),
         const((1, D_B)), const((1, D_B)), const((LANE, D_B)), HBM_SPEC],
        [row(D_B, 0), row(D_B, 0), pl.BlockSpec((nb, None, N_STATE, D_B), lambda c: (0, c, 0, 0))],
        [jax.ShapeDtypeStruct((nb, lp, ybuf.shape[1]), ybuf.dtype), jax.ShapeDtypeStruct((nb, lp, D_B), BF16),
         jax.ShapeDtypeStruct((nb, cpe, N_STATE, D_B), BF16)],
        [pltpu.VMEM((nb, N_STATE, D_B), F32)],
        (xbc.reshape(nb, lp, N_XBC), proj3, proj3, dtb_p, dtb_t, alog_p, alog_t, d_skip_x, norm_g, e_mat,
         ybuf.reshape(nb, lp, ybuf.shape[1])), comm, aliases={10: 0})
    return (ybuf.reshape(t, -1), yssd.reshape(t, D_B), states, *rest)


def _ssd_bwd(dy, y_ssd, xbc, proj, states, dtb, a_log, d_skip_x, norm_g, e_mat, e_mat_t, lp, name, comm=None):
    t = xbc.shape[0]
    cpe = lp // CHUNK
    nb = t // lp
    hpg = N_HEADS // N_GROUPS
    dtb_p = jnp.pad(dtb.reshape(1, N_HEADS), ((0, 0), (0, LANE - N_HEADS)))
    alog_p = jnp.pad(a_log.reshape(1, N_HEADS), ((0, 0), (0, LANE - N_HEADS)))
    dtb_t = dtb.reshape(N_HEADS, 1)
    alog_t = a_log.reshape(N_HEADS, 1)

    def body(dy_ref, ys_ref, xbc_ref, bz_ref, dt_ref, st_ref, dtb_ref, dtbt_ref, al_ref, alt_ref, dx_ref, g_ref,
             e_ref, et_ref, dxbc_ref, dpw_ref, dg_ref, ddtb_ref, dal_ref, dd_ref, ds_scr):
        @pl.when(pl.program_id(0) == 0)
        def _():
            ds_scr[...] = jnp.zeros_like(ds_scr)
            dg_ref[...] = jnp.zeros_like(dg_ref)
            ddtb_ref[...] = jnp.zeros_like(ddtb_ref)
            dal_ref[...] = jnp.zeros_like(dal_ref)
            dd_ref[...] = jnp.zeros_like(dd_ref)

        stores, sums = [], {}
        chains = [one_example(dy_ref.at[b], ys_ref.at[b], xbc_ref.at[b], bz_ref.at[b], dt_ref.at[b], st_ref.at[b], dtb_ref,
                              dtbt_ref, al_ref, alt_ref, dx_ref, g_ref, e_ref, et_ref, dxbc_ref.at[b], dpw_ref.at[b],
                              ds_scr.at[b], stores, sums) for b in range(nb)]
        live = list(chains)
        while live:
            live = [ch for ch in live if next(ch, "done") != "done"]
        for ref, idx, val in stores:
            ref[idx] = val
        for ref, name_ in ((dg_ref, "dg"), (ddtb_ref, "ddtb"), (dal_ref, "dal"), (dd_ref, "dd")):
            total = sums[name_][0]
            for part in sums[name_][1:]:
                total = total + part
            ref[...] += total

    def one_example(dy_ref, ys_ref, xbc_ref, bz_ref, dt_ref, st_ref, dtb_ref, dtbt_ref, al_ref, alt_ref, dx_ref, g_ref,
                    e_ref, et_ref, dxbc_ref, dpw_ref, ds_scr, stores, sums):
        everything = (slice(None), slice(None))
        cc = cpe - 1 - pl.program_id(0)
        rows = lax.broadcasted_iota(jnp.int32, (CHUNK, 1), 0)
        cols = lax.broadcasted_iota(jnp.int32, (1, CHUNK), 1)
        valid_col = jnp.logical_or(cc > 0, rows >= PAD)
        valid_row = jnp.logical_or(cc > 0, cols >= PAD)
        e_mat = e_ref[...]
        e_mat_t = et_ref[...]
        dtraw = dt_ref[...].astype(F32)
        a, dt, big_a, big_a_t, e_a, d_s, dt_x, e_a_x, d_s_x, cd_x = _ssd_common(
            dtraw, dtb_ref[...], dtbt_ref[...], al_ref[...], alt_ref[...], e_mat, valid_col, valid_row, x_terms=1)
        xs = xbc_ref[:, 0:D_B].astype(F32)
        bs = xbc_ref[:, D_B:D_B + N_GROUPS * N_STATE]
        cs = xbc_ref[:, D_B + N_GROUPS * N_STATE:N_XBC]
        xdt = xs * dt_x
        xdt_b = xdt.astype(BF16)
        st_b = st_ref[...]
        dst = ds_scr[...]
        dst_b = dst.astype(BF16)

        ys = ys_ref[...].astype(F32)
        bz = bz_ref[...].astype(F32)
        sil = _silu(bz)
        z = ys * sil
        dz, dgt = _rms_bwd_math(z, g_ref[...], dy_ref[...].astype(F32))
        sums.setdefault("dg", []).append(jnp.sum(dgt, axis=0, keepdims=True))
        stores.append((dpw_ref, (slice(None), slice(0, D_B)), (dz * ys * _dsilu(bz)).astype(BF16)))
        dys = dz * sil

        dd_lane = jnp.sum(dys * xs, axis=0, keepdims=True)
        dxs = dys * dx_ref[...]
        w_x = dys * e_a_x
        w_b = w_x.astype(BF16)
        dys_b = dys.astype(BF16)
        u_b = (xdt * d_s_x).astype(BF16)
        dxdt_parts, dbs_parts, dcs_parts, off_parts, g1_parts = [], [], [], [], []
        da_diag = jnp.zeros((CHUNK, LANE), F32)
        lane = lax.broadcasted_iota(jnp.int32, (1, LANE), 1)
        for g in range(N_GROUPS):
            gs = slice(g * N_STATE, (g + 1) * N_STATE)
            gw = slice(g * GROUP_W, (g + 1) * GROUP_W)
            cs_g, bs_g = cs[:, gs], bs[:, gs]
            dcs = lax.dot_general(w_b[:, gw], st_b[:, gw], NT_DIMS, preferred_element_type=F32)
            y_off = jnp.dot(cs_g, st_b[:, gw], preferred_element_type=F32)
            off_parts.append(y_off)
            dst_new = lax.dot_general(cs_g, w_b[:, gw], TN_DIMS, preferred_element_type=F32)
            g1 = jnp.dot(bs_g, dst_b[:, gw], preferred_element_type=F32)
            g1_parts.append(g1)
            dbs = lax.dot_general(u_b[:, gw], dst_b[:, gw], NT_DIMS, preferred_element_type=F32)
            cb = lax.dot_general(cs_g, bs_g, NT_DIMS, preferred_element_type=F32)
            cbt = lax.dot_general(bs_g, cs_g, NT_DIMS, preferred_element_type=F32)
            dcb = jnp.zeros((CHUNK, CHUNK), F32)
            dcbt = jnp.zeros((CHUNK, CHUNK), F32)
            dxdt_h = []
            for e in range(hpg):
                h = g * hpg + e
                hs = slice(h * HEAD_DIM, (h + 1) * HEAD_DIM)
                dec = _decay(big_a, big_a_t, h, False)
                dect = _decay(big_a, big_a_t, h, True)
                m = cb * dec
                mt = cbt * dect
                dxdt_h.append(jnp.dot(mt.astype(BF16), dys_b[:, hs], preferred_element_type=F32))
                dm = lax.dot_general(dys_b[:, hs], xdt_b[:, hs], NT_DIMS, preferred_element_type=F32)
                dmt = lax.dot_general(xdt_b[:, hs], dys_b[:, hs], NT_DIMS, preferred_element_type=F32)
                dcb = dcb + dm * dec
                dcbt = dcbt + dmt * dect
                da_h = jnp.sum(dm * m - dmt * mt, axis=1, keepdims=True)
                da_diag = da_diag + jnp.where(lane == h, da_h, 0.0)
                yield
            dcs = dcs + jnp.dot(dcb.astype(BF16), bs_g, preferred_element_type=F32)
            dbs = dbs + jnp.dot(dcbt.astype(BF16), cs_g, preferred_element_type=F32)
            dxdt_parts.append(jnp.concatenate(dxdt_h, axis=1) + g1 * d_s_x[:, gw])
            dbs_parts.append(dbs)
            dcs_parts.append(dcs)
            stores.append((ds_scr, (slice(None), gw), dst[:, gw] * cd_x[:, gw] + dst_new))
            yield
        dxdt = jnp.concatenate(dxdt_parts, axis=1)
        y_off = jnp.concatenate(off_parts, axis=1)
        g1 = jnp.concatenate(g1_parts, axis=1)
        dcd_lane = jnp.sum(dst * st_b.astype(F32), axis=0, keepdims=True)
        vecs = jnp.concatenate([jnp.broadcast_to(dcd_lane, (SUB, D_B)), jnp.broadcast_to(dd_lane, (SUB, D_B))], axis=0)
        dds, da_off, ddt_x, vec_sums = _split_dot_many(
            [g1 * xdt, w_x * y_off, dxdt * xs, vecs], [1, 1, 1, 1], e_mat_t)
        dcd = vec_sums[0:1]
        sums.setdefault("dd", []).append(vec_sums[SUB:SUB + 1])
        yield
        t_ds = dds * d_s
        d_a = da_diag + da_off - t_ds
        last_row = jnp.sum(t_ds, axis=0, keepdims=True) + dcd * e_a[CHUNK - 1:CHUNK, :]
        d_a = d_a + jnp.where(rows == CHUNK - 1, last_row, 0.0)
        dda = _split_dot_left(_tri(False).astype(BF16), d_a, 3)
        ddt = dda * a + ddt_x
        sums.setdefault("dal", []).append(jnp.sum(dda * dt * a, axis=0, keepdims=True))
        ddtraw = jnp.where(valid_col, ddt * jax.nn.sigmoid(dtraw + dtb_ref[...]), 0.0)
        ddtraw = jnp.where(lane < N_HEADS, ddtraw, 0.0)
        sums.setdefault("ddtb", []).append(jnp.sum(ddtraw, axis=0, keepdims=True))
        stores.append((dpw_ref, (slice(None), slice(D_B, D_B + LANE)), ddtraw.astype(BF16)))
        dxs = dxs + dxdt * dt_x
        dxbc = jnp.concatenate([dxs] + dbs_parts + dcs_parts, axis=1)
        stores.append((dxbc_ref, everything, jnp.where(valid_col, dxbc, 0.0).astype(BF16)))

    def row(width, col):
        return pl.BlockSpec((nb, CHUNK, width), lambda c: (0, cpe - 1 - c, col))

    def const(shape):
        return pl.BlockSpec(shape, lambda c: (0,) * len(shape))

    proj3 = proj.reshape(nb, lp, N_INP)
    dxbc, dproj, *rest = _grid_call(
        body, name, (cpe,),
        [row(D_B, 0), row(D_B, 0), row(N_XBC, 0), row(D_B, COL_BZ // D_B), row(LANE, COL_DT // LANE),
         pl.BlockSpec((nb, None, N_STATE, D_B), lambda c: (0, cpe - 1 - c, 0, 0)),
         const((1, LANE)), const((N_HEADS, 1)), const((1, LANE)), const((N_HEADS, 1)),
         const((1, D_B)), const((1, D_B)), const((LANE, D_B)), const((D_B, LANE))],
        [row(N_XBC, 0),
         pl.BlockSpec((pl.Element(nb), pl.Element(CHUNK), pl.Element(D_B + LANE)),
                      lambda c: (0, (cpe - 1 - c) * CHUNK, COL_BZ)),
         const((1, D_B)), const((1, LANE)), const((1, LANE)), const((1, LANE))],
        [jax.ShapeDtypeStruct((nb, lp, N_XBC), BF16), jax.ShapeDtypeStruct((nb, lp, N_INP), BF16),
         jax.ShapeDtypeStruct((1, D_B), F32),
         jax.ShapeDtypeStruct((1, LANE), F32), jax.ShapeDtypeStruct((1, LANE), F32),
         jax.ShapeDtypeStruct((1, LANE), F32)],
        [pltpu.VMEM((nb, N_STATE, D_B), F32)],
        (dy.reshape(nb, lp, -1), y_ssd.reshape(nb, lp, D_B), xbc.reshape(nb, lp, N_XBC), proj3, proj3, states,
         dtb_p, dtb_t, alog_p, alog_t, d_skip_x, norm_g, e_mat, e_mat_t), comm)
    return (dxbc.reshape(t, N_XBC), dproj.reshape(t, N_INP), *rest)


HBM_SPEC = pl.BlockSpec(memory_space=pl.ANY)


def _mesh_pos():
    return lax.axis_index("x"), lax.axis_index("y"), lax.axis_index("c")


def _remote(src, dst, send_sem, recv_sem, to):
    return pltpu.make_async_remote_copy(src_ref=src, dst_ref=dst, send_sem=send_sem, recv_sem=recv_sem,
                                        device_id=to, device_id_type=MESH)


def _slot_plain(d):
    return d


def _slot_mix_rows(d):
    return jnp.where(d < 2, d + 4, jnp.where(d < 6, d - 2, d))


def _ag_ici(pieces, slots):
    n = len(pieces)

    def copies(ins, outs, sems, with_recvs):
        send_sems, recv_sems, local_sems = sems
        x, y, c = _mesh_pos()
        local, sends, recvs = [], [], []
        for a in range(n):
            mine = outs[a].at[slots[a](4 * x + 2 * y + c)]
            local.append(pltpu.make_async_copy(ins[a], mine, local_sems.at[a]))
            for k, (px, py) in enumerate([(1 - x, y), (x, 1 - y), (1 - x, 1 - y)]):
                sends.append(_remote(ins[a], mine, send_sems.at[a, k], recv_sems.at[a, k], (px, py, c)))
                if with_recvs:
                    theirs = outs[a].at[slots[a](4 * px + 2 * py + c)]
                    recvs.append(_remote(ins[a], theirs, send_sems.at[a, k], recv_sems.at[a, k], (px, py, c)))
        return local, sends, recvs

    return _Comm(pieces, [jax.ShapeDtypeStruct((N_DEV,) + p.shape, p.dtype) for p in pieces],
                 [pltpu.SemaphoreType.DMA((n, 3)), pltpu.SemaphoreType.DMA((n, 3)), pltpu.SemaphoreType.DMA((n,))], copies)


def _ag_d2d(bufs, slots):
    n = len(bufs)

    def copies(ins, outs, sems, with_recvs):
        send_sems, recv_sems = sems
        x, y, c = _mesh_pos()
        chips = [(x, y), (1 - x, y), (x, 1 - y), (1 - x, 1 - y)]
        sends, recvs = [], []
        for a in range(n):
            for k, (px, py) in enumerate(chips):
                held = slots[a](4 * px + 2 * py + c)
                sends.append(_remote(ins[a].at[held], outs[a].at[held], send_sems.at[a, k], recv_sems.at[a, k], (x, y, 1 - c)))
                if with_recvs:
                    got = slots[a](4 * px + 2 * py + 1 - c)
                    recvs.append(_remote(ins[a].at[got], outs[a].at[got], send_sems.at[a, k], recv_sems.at[a, k], (x, y, 1 - c)))
        return [], sends, recvs

    return _Comm(bufs, [jax.ShapeDtypeStruct(b.shape, b.dtype) for b in bufs],
                 [pltpu.SemaphoreType.DMA((n, 4)), pltpu.SemaphoreType.DMA((n, 4))], copies,
                 aliases={a: a for a in range(n)})


def _rs_d2d(blocks, slots):
    n = len(blocks)

    def copies(ins, outs, sems, with_recvs):
        send_sems, recv_sems = sems
        x, y, c = _mesh_pos()
        sends, recvs = [], []
        for a in range(n):
            for j in range(4):
                src = ins[a].at[slots[a](2 * j + 1 - c)]
                sends.append(_remote(src, outs[a].at[j], send_sems.at[a, j], recv_sems.at[a, j], (x, y, 1 - c)))
                if with_recvs:
                    recvs.append(_remote(src, outs[a].at[j], send_sems.at[a, j], recv_sems.at[a, j], (x, y, 1 - c)))
        return [], sends, recvs

    return _Comm(blocks, [jax.ShapeDtypeStruct((4,) + b.shape[1:], b.dtype) for b in blocks],
                 [pltpu.SemaphoreType.DMA((n, 4)), pltpu.SemaphoreType.DMA((n, 4))], copies)


def _rs_ici(blocks):
    n = len(blocks)

    def copies(ins, outs, sems, with_recvs):
        send_sems, recv_sems = sems
        x, y, c = _mesh_pos()
        sends, recvs = [], []
        for a in range(n):
            for k, (px, py) in enumerate([(1 - x, y), (x, 1 - y), (1 - x, 1 - y)]):
                src = ins[a].at[2 * px + py]
                sends.append(_remote(src, outs[a].at[k], send_sems.at[a, k], recv_sems.at[a, k], (px, py, c)))
                if with_recvs:
                    recvs.append(_remote(src, outs[a].at[k], send_sems.at[a, k], recv_sems.at[a, k], (px, py, c)))
        return [], sends, recvs

    return _Comm(blocks, [jax.ShapeDtypeStruct((3,) + b.shape[1:], b.dtype) for b in blocks],
                 [pltpu.SemaphoreType.DMA((n, 3)), pltpu.SemaphoreType.DMA((n, 3))], copies)


def _run_comm(comm, name):
    n_in, n_out = len(comm.inputs), len(comm.out_shapes)

    def body(*refs):
        ins, outs, sems = refs[:n_in], refs[n_in:n_in + n_out], refs[n_in + n_out:]
        comm.start(ins, outs, sems)
        comm.wait(ins, outs, sems)

    return pl.pallas_call(
        body, name=name, in_specs=[HBM_SPEC] * n_in, out_specs=[HBM_SPEC] * n_out, out_shape=comm.out_shapes,
        scratch_shapes=comm.sem_shapes, input_output_aliases=comm.aliases,
    )(*comm.inputs)


W_ROWS = 784


def _w_segments():
    per = N_IN // N_DEV
    out = []
    for d in range(N_DEV):
        lo, hi = per * d, per * (d + 1)
        for a, b, start in COL_MAP:
            s, e = max(lo, a), min(hi, b)
            if s < e:
                out.append((d, s - lo, e - s, s - a + start))
    return out


def _w_gather_rows(g, name):
    tk = D_MODEL // 2
    u32 = jnp.uint32

    def body(g_ref, o_ref, scr):
        for d in range(N_DEV):
            x32 = pltpu.bitcast(g_ref[d], u32)
            for dd, src, rows, dst in _w_segments():
                if dd == d:
                    scr[dst // 2:(dst + rows) // 2, :] = x32[src // 2:(src + rows) // 2]
        scr[(COL_DT + N_HEADS) // 2:COL_XBC // 2, :] = jnp.zeros(((COL_XBC - COL_DT - N_HEADS) // 2, tk), u32)
        o_ref[...] = pltpu.bitcast(scr[...], BF16)

    return pl.pallas_call(
        body, name=name, grid=(D_MODEL // tk,),
        in_specs=[pl.BlockSpec((N_DEV, W_ROWS, tk), lambda j: (0, 0, j))],
        out_specs=pl.BlockSpec((N_INP, tk), lambda j: (0, j)),
        out_shape=jax.ShapeDtypeStruct((N_INP, D_MODEL), BF16),
        scratch_shapes=[pltpu.VMEM((N_INP // 2, tk), u32)],
        compiler_params=_cparams(("parallel",)),
    )(g)


def _w_split_rows(dwt, name):
    tk = D_MODEL // 4
    per = N_IN // N_DEV

    def body(w_ref, o_ref):
        for d, dst, rows, src in _w_segments():
            o_ref[d, dst // 2:(dst + rows) // 2, :] = pltpu.bitcast(w_ref[src:src + rows, :].astype(BF16), F32)
        for d in range(N_DEV):
            o_ref[d, per // 2:W_ROWS // 2, :] = jnp.zeros(((W_ROWS - per) // 2, tk), F32)

    return pl.pallas_call(
        body, name=name, grid=(D_MODEL // tk,),
        in_specs=[pl.BlockSpec((N_INP, tk), lambda j: (0, j))],
        out_specs=pl.BlockSpec((N_DEV, W_ROWS // 2, tk), lambda j: (0, 0, j)),
        out_shape=jax.ShapeDtypeStruct((N_DEV, W_ROWS // 2, D_MODEL), F32),
        compiler_params=_cparams(("parallel",)),
    )(dwt)


def _rs_pair_sum(g, ra, own_slots, name):
    packed = g.dtype == F32
    _, rows, cols = g.shape
    out_rows = 2 * rows if packed else rows

    def body(s_ref, g_ref, ra_ref, o_ref):
        a, b = g_ref[...], ra_ref[...]
        if packed:
            a, b = pltpu.bitcast(a, BF16), pltpu.bitcast(b, BF16)
        o_ref[...] = (a.astype(F32) + b.astype(F32)).astype(BF16)

    return pl.pallas_call(
        body, name=name,
        grid_spec=pltpu.PrefetchScalarGridSpec(
            num_scalar_prefetch=1, grid=(4,),
            in_specs=[pl.BlockSpec((None, rows, cols), lambda j, s: (s[j], 0, 0)),
                      pl.BlockSpec((None, rows, cols), lambda j, s: (j, 0, 0))],
            out_specs=pl.BlockSpec((None, out_rows, cols), lambda j, s: (j, 0, 0))),
        out_shape=jax.ShapeDtypeStruct((4, out_rows, cols), BF16),
        compiler_params=_cparams(("parallel",)),
    )(own_slots, g, ra)


def _rs_final_sum(h, rb, chip_idx, name):
    _, rows, cols = h.shape

    def body(j_ref, h_ref, rb_ref, o_ref):
        o_ref[...] = ((h_ref[...].astype(F32) + rb_ref[0].astype(F32)) + rb_ref[1].astype(F32)) + rb_ref[2].astype(F32)

    return pl.pallas_call(
        body, name=name,
        grid_spec=pltpu.PrefetchScalarGridSpec(
            num_scalar_prefetch=1, grid=(1,),
            in_specs=[pl.BlockSpec((None, rows, cols), lambda i, j: (j[0], 0, 0)),
                      pl.BlockSpec((3, rows, cols), lambda i, j: (0, 0, 0))],
            out_specs=pl.BlockSpec((rows, cols), lambda i, j: (0, 0))),
        out_shape=jax.ShapeDtypeStruct((rows, cols), F32),
        compiler_params=_cparams(("arbitrary",)),
    )(chip_idx, h, rb)


def _adamw_math(w, g, m, v):
    m = ADAM_B1 * m + (1.0 - ADAM_B1) * g
    v = ADAM_B2 * v + (1.0 - ADAM_B2) * (g * g)
    m_hat = m / (1.0 - ADAM_B1 ** ADAM_STEP)
    v_hat = v / (1.0 - ADAM_B2 ** ADAM_STEP)
    delta = -ADAM_LR * (m_hat / (jnp.sqrt(v_hat) + ADAM_EPS) + ADAM_WD * w)
    return delta, m, v


def _adamw_rows(g, w, m, v, tr, name):
    layers, rows, cols = w.shape

    def body(g_ref, w_ref, m_ref, v_ref, d_out, m_out, v_out):
        delta, m_new, v_new = _adamw_math(w_ref[...], g_ref[...], m_ref[...], v_ref[...])
        d_out[...] = delta
        m_out[...] = m_new
        v_out[...] = v_new

    blk = pl.BlockSpec((None, tr, cols), lambda a, r: (a, r, 0))
    return pl.pallas_call(
        body, name=name, grid=(layers, rows // tr),
        in_specs=[blk] * 4, out_specs=[blk] * 3,
        out_shape=[jax.ShapeDtypeStruct((layers, rows, cols), F32)] * 3,
        compiler_params=_cparams(("parallel", "parallel")),
    )(g, w, m, v)


def _adamw_cols(g_t, w, m, v, name):
    layers, k, cols = w.shape
    tr = 56
    assert g_t.shape[1] % tr == 0
    views = [jnp.transpose(a, (2, 0, 1)) for a in (w, m, v)]

    def body(g_ref, w_ref, m_ref, v_ref, g_out, d_out, m_out, v_out):
        for a in range(layers):
            g = g_ref[a]
            delta, m_new, v_new = _adamw_math(w_ref[:, a, :], g, m_ref[:, a, :], v_ref[:, a, :])
            g_out[:, a, :] = g
            d_out[:, a, :] = delta
            m_out[:, a, :] = m_new
            v_out[:, a, :] = v_new

    col = pl.BlockSpec((tr, layers, k), lambda r: (r, 0, 0))
    outs = pl.pallas_call(
        body, name=name, grid=(pl.cdiv(cols, tr),),
        in_specs=[pl.BlockSpec((layers, tr, k), lambda r: (0, r, 0)), col, col, col], out_specs=[col] * 4,
        out_shape=[jax.ShapeDtypeStruct((cols, layers, k), F32)] * 4,
        compiler_params=_cparams(("parallel",)),
    )(g_t, *views)
    return [jnp.transpose(o, (1, 2, 0)) for o in outs]


def _sum_devices(parts, name):
    _, p, _ = parts.shape

    def body(x_ref, o_ref):
        acc = x_ref[0]
        for d in range(1, N_DEV):
            acc = acc + x_ref[d]
        o_ref[...] = acc

    return pl.pallas_call(
        body, name=name, grid=(1,),
        in_specs=[pl.BlockSpec((N_DEV, p, LANE), lambda i: (0, 0, 0))],
        out_specs=pl.BlockSpec((p, LANE), lambda i: (0, 0)),
        out_shape=jax.ShapeDtypeStruct((p, LANE), F32),
        compiler_params=_cparams(("arbitrary",)),
    )(parts)


def _adamw_small(g, w, m, v, name):
    p = g.shape[0]

    def body(g_ref, w_ref, m_ref, v_ref, d_out, m_out, v_out):
        delta, m_new, v_new = _adamw_math(w_ref[...], g_ref[...], m_ref[...], v_ref[...])
        d_out[...] = delta
        m_out[...] = m_new
        v_out[...] = v_new

    spec = pl.BlockSpec((p, LANE), lambda i: (0, 0))
    return pl.pallas_call(
        body, name=name, grid=(1,),
        in_specs=[spec] * 4, out_specs=[spec] * 3,
        out_shape=[jax.ShapeDtypeStruct((p, LANE), F32)] * 3,
        compiler_params=_cparams(("arbitrary",)),
    )(g, w, m, v)


PACK_ALIGN = SUB * LANE

SMALL_PARAMS = (
    ("meta", (N_META, D_MODEL), 1),
    ("pre_g", (DEPTH, D_MODEL), None),
    ("post_g", (DEPTH, D_MODEL), None),
    ("conv_a_w", (DEPTH, CONV_A_K, D_A), 2),
    ("ssm_conv_w", (DEPTH, SSM_K, N_XBC), 2),
    ("ssm_conv_b", (DEPTH, N_XBC), None),
    ("dt_bias", (DEPTH, N_HEADS), None),
    ("a_log", (DEPTH, N_HEADS), None),
    ("d_skip", (DEPTH, N_HEADS), None),
    ("ssm_norm_g", (DEPTH, D_B), None),
    ("conf_conv_w", (DEPTH, CONF_K, D_C), 2),
    ("conf_conv_b", (DEPTH, D_C), None),
    ("conf_ln_g", (DEPTH, D_C), None),
    ("conf_ln_b", (DEPTH, D_C), None),
)


def _local_shape(shape, axis):
    if axis is None:
        return shape
    return tuple(s // N_DEV if k == axis else s for k, s in enumerate(shape))


def _pack(arrays):
    flat = []
    for a in arrays:
        v = a.reshape(-1).astype(F32)
        flat.append(v)
        if v.shape[0] % PACK_ALIGN:
            flat.append(jnp.zeros(((-v.shape[0]) % PACK_ALIGN,), F32))
    return jnp.concatenate(flat).reshape(-1, LANE)


def _unpack(buf, shapes):
    flat = buf.reshape(-1)
    out, off = [], 0
    for s in shapes:
        size = 1
        for k in s:
            size *= k
        out.append(flat[off:off + size].reshape(s))
        off += size + (-size) % PACK_ALIGN
    return out


def kernel(x, meta, pre_g, post_g, w_in, w_out, conv_a_w, ssm_conv_w, ssm_conv_b, dt_bias, a_log, d_skip, ssm_norm_g, conf_conv_w, conf_conv_b, conf_ln_g, conf_ln_b, loss_target, m_meta, m_pre_g, m_post_g, m_w_in, m_w_out, m_conv_a_w, m_ssm_conv_w, m_ssm_conv_b, m_dt_bias, m_a_log, m_d_skip, m_ssm_norm_g, m_conf_conv_w, m_conf_conv_b, m_conf_ln_g, m_conf_ln_b, v_meta, v_pre_g, v_post_g, v_w_in, v_w_out, v_conv_a_w, v_ssm_conv_w, v_ssm_conv_b, v_dt_bias, v_a_log, v_d_skip, v_ssm_norm_g, v_conf_conv_w, v_conf_conv_b, v_conf_ln_g, v_conf_ln_b):
    weights = dict(meta=meta, pre_g=pre_g, post_g=post_g, conv_a_w=conv_a_w, ssm_conv_w=ssm_conv_w, ssm_conv_b=ssm_conv_b,
                   dt_bias=dt_bias, a_log=a_log, d_skip=d_skip, ssm_norm_g=ssm_norm_g, conf_conv_w=conf_conv_w,
                   conf_conv_b=conf_conv_b, conf_ln_g=conf_ln_g, conf_ln_b=conf_ln_b)
    mom1 = dict(meta=m_meta, pre_g=m_pre_g, post_g=m_post_g, conv_a_w=m_conv_a_w, ssm_conv_w=m_ssm_conv_w,
                ssm_conv_b=m_ssm_conv_b, dt_bias=m_dt_bias, a_log=m_a_log, d_skip=m_d_skip, ssm_norm_g=m_ssm_norm_g,
                conf_conv_w=m_conf_conv_w, conf_conv_b=m_conf_conv_b, conf_ln_g=m_conf_ln_g, conf_ln_b=m_conf_ln_b)
    mom2 = dict(meta=v_meta, pre_g=v_pre_g, post_g=v_post_g, conv_a_w=v_conv_a_w, ssm_conv_w=v_ssm_conv_w,
                ssm_conv_b=v_ssm_conv_b, dt_bias=v_dt_bias, a_log=v_a_log, d_skip=v_d_skip, ssm_norm_g=v_ssm_norm_g,
                conf_conv_w=v_conf_conv_w, conf_conv_b=v_conf_conv_b, conf_ln_g=v_conf_ln_g, conf_ln_b=v_conf_ln_b)
    nb, seq, d = x.shape
    lp = PAD + N_META + seq
    t = nb * lp
    assert lp % (SHORT_TILES * HALO_C) == 0 and t % (3 * LANE) == 0 and d == D_MODEL
    rt = lp // 3
    xi, yi, ci = _mesh_pos()
    dev = 4 * xi + 2 * yi + ci
    ci32 = ci.astype(jnp.int32)
    chip_idx = (2 * xi + yi).astype(jnp.int32).reshape(1)
    own_plain = jnp.stack([2 * j + ci32 for j in range(4)])
    own_mix = jnp.stack([_slot_mix_rows(2 * j + ci32) for j in range(4)]).astype(jnp.int32)
    n_in_loc = N_IN // N_DEV
    n_out_loc = 2 * D_MODEL // N_DEV
    slots = (_slot_plain, _slot_mix_rows)

    sharded_small = [n for n, _, ax in SMALL_PARAMS if ax is not None]
    small_shapes = {n: s for n, s, _ in SMALL_PARAMS}
    small_axis = {n: ax for n, _, ax in SMALL_PARAMS}
    sw_pack = _pack([weights[n] for n in sharded_small])
    wt_loc = jnp.pad(jnp.swapaxes(w_in, 1, 2).astype(BF16), ((0, 0), (0, W_ROWS - n_in_loc), (0, 0)))
    wo_loc = w_out.astype(BF16)
    plain2 = (_slot_plain, _slot_plain)
    head = _run_comm(_ag_ici([wt_loc[0], sw_pack], plain2), "ag_ici")
    g_in_t, sw_g = _run_comm(_ag_d2d(head, plain2), "ag_d2d")

    def gathered(bufs):
        return _w_gather_rows(bufs[0], "w_gather_rows"), bufs[1].reshape(2 * D_MODEL, D_MODEL)

    full = dict(weights)
    per_dev = [_unpack(sw_g[k], [_local_shape(small_shapes[n], small_axis[n]) for n in sharded_small]) for k in range(N_DEV)]
    for q, n in enumerate(sharded_small):
        full[n] = jnp.concatenate([per_dev[k][q] for k in range(N_DEV)], axis=small_axis[n])

    e_mat = (lax.broadcasted_iota(jnp.int32, (LANE, D_B), 0) == lax.broadcasted_iota(jnp.int32, (LANE, D_B), 1) // HEAD_DIM)
    e_mat = e_mat.astype(BF16)
    e_mat_t = e_mat.T

    front = jnp.concatenate([jnp.zeros((PAD, d), F32), full["meta"]], axis=0)
    h = jnp.concatenate([jnp.concatenate([front, x[b]], axis=0) for b in range(nb)], axis=0)
    saved = []
    w_t, w_o = _w_gather_rows(g_in_t, "w_gather_rows"), None
    hn = _rms_fwd(h, pre_g[0].reshape(1, -1), rt, "rms_fwd")
    for i in range(DEPTH):
        row = lambda a: a[i].reshape(1, -1)
        if i == 0:
            proj, wo_buf = _mm(hn, w_t, "nt", BF16, t // 3, MM_TN, D_MODEL, "mm_proj", cols_outer=True,
                               comm=_ag_ici([wo_loc[0]], (_slot_mix_rows,)))
        else:
            proj = _mm(hn, w_t, "nt", BF16, t // 3, MM_TN, D_MODEL, "mm_proj", cols_outer=True)
        ycat = _a_fwd(proj, full["conv_a_w"][i], lp, "a_fwd")
        xbc = _xbc_fwd(proj, full["ssm_conv_w"][i], row(ssm_conv_b), lp, "xbc_fwd")
        ycat, u1 = _c_fwd(proj, full["conf_conv_w"][i], row(conf_conv_b), row(conf_ln_g), row(conf_ln_b), ycat, lp, "c_fwd")
        d_skip_x = jnp.repeat(d_skip[i], HEAD_DIM).reshape(1, D_B)
        nxt = _ag_ici([wt_loc[i + 1], wo_loc[i + 1]], slots) if i + 1 < DEPTH else None
        if i == 0:
            nxt = _merge_comm(nxt, _ag_d2d([wo_buf], (_slot_mix_rows,)))
        ycat, yssd, states, *bufs = _ssd_fwd(xbc, proj, dt_bias[i], a_log[i], d_skip_x, row(ssm_norm_g), e_mat, ycat, lp,
                                             "ssd_fwd", comm=nxt)
        if i == 0:
            w_o = bufs.pop().reshape(2 * D_MODEL, D_MODEL)
        if bufs:
            m, *bufs = _mm(ycat, w_o, "nn", F32, t // 6, D_MODEL, 2 * D_MODEL, "mm_out", comm=_ag_d2d(bufs, slots))
        else:
            m = _mm(ycat, w_o, "nn", F32, t // 6, D_MODEL, 2 * D_MODEL, "mm_out")
        saved.append((h, hn, proj, ycat, yssd, states, xbc, m, d_skip_x, w_t, w_o, u1))
        if i + 1 < DEPTH:
            h, hn = _post_rms_fwd(h, m, row(post_g), pre_g[i + 1].reshape(1, -1), rt, "post_rms_fwd")
        else:
            h = _post_fwd(h, m, row(post_g), rt, "post_fwd")
        if bufs:
            w_t, w_o = gathered(bufs)

    dh, loss_blk = _loss_kernel(h, loss_target.reshape(nb * seq, d), lp, "loss")

    grads = {n: [None] * DEPTH for n, _, _ in SMALL_PARAMS if n != "meta"}
    gt_in, gr_out = [None] * DEPTH, [None] * DEPTH
    dmeta = None
    pending = None
    dm = None

    def rs_pair_sums(blocks, ras):
        return [_rs_pair_sum(blocks[0], ras[0], own_plain, "rs_pair_sum_in"),
                _rs_pair_sum(blocks[1], ras[1], own_mix, "rs_pair_sum_out")]

    def rs_finish(layer, hs, rbs):
        gt_in[layer] = _rs_final_sum(hs[0], rbs[0], chip_idx, "rs_final_sum_in")
        gr_out[layer] = _rs_final_sum(hs[1], rbs[1], chip_idx, "rs_final_sum_out")

    for i in reversed(range(DEPTH)):
        row = lambda a: a[i].reshape(1, -1)
        h_i, hn, proj, ycat, yssd, states, xbc, m, d_skip_x, w_t, w_o, u1 = saved[i]
        if dm is None:
            dm, grads["post_g"][i] = _post_bwd(dh, m, row(post_g), rt, "post_bwd")
        if pending is not None:
            dy, *ras = _mm(dm, w_o, "nt", BF16, t // 6, D_MODEL, D_MODEL, "mm_dy", cols_outer=True,
                           comm=_rs_d2d(list(pending), slots))
            hs = rs_pair_sums(pending, ras)
        else:
            dy = _mm(dm, w_o, "nt", BF16, t // 6, D_MODEL, D_MODEL, "mm_dy", cols_outer=True)
            hs = None
        dw_out = _mm(ycat, dm, "tn", BF16, D_MODEL, D_MODEL, t, "mm_dwout")
        dxbc, dproj, grads["ssm_norm_g"][i], ddtb, dal, dds, *rbs = _ssd_bwd(
            dy, yssd, xbc, proj, states, dt_bias[i], a_log[i], d_skip_x, row(ssm_norm_g), e_mat, e_mat_t, lp, "ssd_bwd",
            comm=_rs_ici(hs) if hs is not None else None)
        if hs is not None:
            rs_finish(i + 1, hs, rbs)
        grads["dt_bias"][i] = ddtb[:, :N_HEADS]
        grads["a_log"][i] = dal[:, :N_HEADS]
        grads["d_skip"][i] = dds[:, :N_HEADS]
        dproj, grads["conv_a_w"][i] = _a_bwd(dy, proj, full["conv_a_w"][i], dproj, lp, "a_bwd")
        dproj, grads["ssm_conv_w"][i], grads["ssm_conv_b"][i] = _xbc_bwd(
            dxbc, proj, full["ssm_conv_w"][i], row(ssm_conv_b), dproj, lp, "xbc_bwd")
        dproj, grads["conf_conv_w"][i], grads["conf_conv_b"][i], grads["conf_ln_g"][i], grads["conf_ln_b"][i] = _c_bwd(
            dy, proj, u1, full["conf_conv_w"][i], row(conf_conv_b), row(conf_ln_g), row(conf_ln_b), dproj, lp, "c_bwd")
        p_out = dw_out.reshape(N_DEV, n_out_loc, D_MODEL)
        if i > 0:
            dhn = _mm(dproj, w_t, "nn", BF16, MM_TM, D_MODEL, N_INP, "mm_dhn")
            dw_in_t = _mm(dproj, hn, "tn", F32, MM_TN, D_MODEL, t, "mm_dwin")
        else:
            dhn, ra_out = _mm(dproj, w_t, "nn", BF16, MM_TM, D_MODEL, N_INP, "mm_dhn",
                              comm=_rs_d2d([p_out], (_slot_mix_rows,)))
            h_out = _rs_pair_sum(p_out, ra_out, own_mix, "rs_pair_sum_out")
            dw_in_t, rb_out = _mm(dproj, hn, "tn", F32, MM_TN, D_MODEL, t, "mm_dwin", comm=_rs_ici([h_out]))
            gr_out[0] = _rs_final_sum(h_out, rb_out, chip_idx, "rs_final_sum_out")
        if i > 0:
            dh, grads["pre_g"][i], dm, grads["post_g"][i - 1] = _rms_post_bwd(
                dh, dhn, h_i, row(pre_g), saved[i - 1][7], post_g[i - 1].reshape(1, -1), rt, "rms_post_bwd")
        else:
            dh, grads["pre_g"][i], dmeta = _rms_bwd(dh, dhn, h_i, row(pre_g), lp, rt, "rms_bwd")
        pending = (_w_split_rows(dw_in_t, "w_split_rows"), p_out)
    grad_x = dh.reshape(nb, lp, d)[:, PAD + N_META:]

    names = [n for n, _, _ in SMALL_PARAMS]
    partial = [loss_blk[0:1, 0:1], dmeta] + [jnp.concatenate(grads[n], axis=0) for n in names[1:]]
    part_pack = _pack(partial)
    p_in = pending[0]
    ra_in, part_buf = _run_comm(_merge_comm(_rs_d2d([p_in], (_slot_plain,)), _ag_ici([part_pack], (_slot_plain,))), "rs_d2d")
    h_in = _rs_pair_sum(p_in, ra_in, own_plain, "rs_pair_sum_in")
    rb_in, parts_g = _run_comm(_merge_comm(_rs_ici([h_in]), _ag_d2d([part_buf], (_slot_plain,))), "rs_ici")
    gt_in[0] = _rs_final_sum(h_in, rb_in, chip_idx, "rs_final_sum_in")

    g_w_out = jnp.stack(gr_out)
    big = {"w_in": _adamw_cols(jnp.stack(gt_in), w_in, m_w_in, v_w_in, "adamw_w_in"),
           "w_out": [g_w_out, *_adamw_rows(g_w_out, w_out, m_w_out, v_w_out, n_out_loc, "adamw_w_out")]}

    total = _unpack(_sum_devices(parts_g, "sum_small_grads"), [(1,)] + [small_shapes[n] for n in names])
    loss = total[0][0]
    g_small = {}
    for n, g in zip(names, total[1:]):
        ax = small_axis[n]
        if ax is not None:
            g = lax.dynamic_slice_in_dim(g, dev * (small_shapes[n][ax] // N_DEV), small_shapes[n][ax] // N_DEV, axis=ax)
        g_small[n] = g
    loc_shapes = [_local_shape(small_shapes[n], small_axis[n]) for n in names]
    d_pack, m_pack, v_pack = _adamw_small(_pack([g_small[n] for n in names]), _pack([weights[n] for n in names]),
                                          _pack([mom1[n] for n in names]), _pack([mom2[n] for n in names]), "adamw_small")
    d_small = dict(zip(names, _unpack(d_pack, loc_shapes)))
    m_small = dict(zip(names, _unpack(m_pack, loc_shapes)))
    v_small = dict(zip(names, _unpack(v_pack, loc_shapes)))

    order = ["meta", "pre_g", "post_g", "w_in", "w_out", "conv_a_w", "ssm_conv_w", "ssm_conv_b", "dt_bias", "a_log",
             "d_skip", "ssm_norm_g", "conf_conv_w", "conf_conv_b", "conf_ln_g", "conf_ln_b"]

    def pick(k, small):
        return [big[n][k] if n in big else small[n] for n in order]

    return (loss, grad_x, *pick(0, g_small), *pick(1, d_small), *pick(2, m_small), *pick(3, v_small))
```
